```python
import math
import jax, jax.numpy as jnp
from jax import lax
import numpy as np

D_MODEL = 1024
BATCH = 8
SEQ = 4096
DEPTH = 1

RET_HEADS = 8
RET_DK = 128
RET_DV = 256
RET_CHUNK = 128
ROPE_BASE = 10000.0
CONV_CH = D_MODEL
CONV_WIDTH = 31
D_FF = 2816
LN_EPS = 1e-5

QK_W = RET_HEADS * RET_DK
V_W = RET_HEADS * RET_DV
SPLIT_POINTS = tuple(int(p) for p in np.cumsum([QK_W, QK_W, V_W, V_W, CONV_CH, CONV_CH, D_MODEL]))
IN_W = 2 * QK_W + 2 * V_W + 2 * CONV_CH + 2 * D_MODEL

kernel_name = "hybrid_retention_conformer_macaron_deepnorm"


def layer_norm(x, g, b):
    xf = x.astype(jnp.float32)
    mu = jnp.mean(xf, axis=-1, keepdims=True)
    var = jnp.mean(jnp.square(xf - mu), axis=-1, keepdims=True)
    y = (xf - mu) * lax.rsqrt(var + LN_EPS) * g.astype(jnp.float32) + b.astype(jnp.float32)
    return y.astype(x.dtype)


def swiglu_ffn(h, w_gate, w_up, w_down):
    return (jax.nn.silu(h @ w_gate) * (h @ w_up)) @ w_down


def rotary(x, cos, sin):
    half = x.shape[-1] // 2
    x1, x2 = x[..., :half], x[..., half:]
    return jnp.concatenate([x1 * cos - x2 * sin, x2 * cos + x1 * sin], axis=-1).astype(x.dtype)


def retention_chunkwise(q, k, v):
    b, s, h, dk = q.shape
    dv = v.shape[-1]
    c = RET_CHUNK
    n = s // c
    log_g = jnp.log(1.0 - jnp.exp2(-5.0 - jnp.arange(h, dtype=jnp.float32)))
    idx = jnp.arange(c, dtype=jnp.float32)
    diff = idx[:, None] - idx[None, :]
    decay_mask = jnp.where(diff[None] >= 0,
                           jnp.exp(jnp.maximum(diff, 0.0)[None] * log_g[:, None, None]), 0.0)
    qc = q.reshape(b, n, c, h, dk)
    kc = k.reshape(b, n, c, h, dk)
    vc = v.reshape(b, n, c, h, dv)
    scores = jnp.einsum('bnihd,bnjhd->bnhij', qc, kc) * decay_mask
    intra = jnp.einsum('bnhij,bnjhe->bnihe', scores, vc)
    xi = jnp.exp((idx[:, None] + 1.0) * log_g[None, :])
    zeta = jnp.exp((c - 1.0 - idx)[:, None] * log_g[None, :])
    chunk_decay = jnp.exp(c * log_g)

    def step(state, xs):
        qn, kn, vn = xs
        cross = jnp.einsum('bihd,bhde->bihe', qn, state) * xi[None, :, :, None]
        new_state = chunk_decay[None, :, None, None] * state + jnp.einsum(
            'bjhd,bjhe->bhde', kn * zeta[None, :, :, None], vn)
        return new_state, cross

    state0 = jnp.zeros((b, h, dk, dv), jnp.float32)
    _, cross = lax.scan(step, state0, (jnp.moveaxis(qc, 1, 0), jnp.moveaxis(kc, 1, 0), jnp.moveaxis(vc, 1, 0)))
    cross = jnp.moveaxis(cross, 0, 1)
    return (intra + cross).reshape(b, s, h, dv)


def hybrid_mixer(h, w_in, b_in, ret_gn_g, conv_k, conv_b, conv_ln_g, conv_ln_b,
                 w_ret_o, w_conv_o, w_out, cos, sin):
    bsz, s, _ = h.shape
    proj = h @ w_in + b_in
    q, k, v, g, glu_a, glu_b, gate_r, gate_c = jnp.split(proj, SPLIT_POINTS, axis=-1)

    q = rotary(q.reshape(bsz, s, RET_HEADS, RET_DK), cos, sin) * (RET_DK ** -0.5)
    k = rotary(k.reshape(bsz, s, RET_HEADS, RET_DK), cos, sin)
    v = v.reshape(bsz, s, RET_HEADS, RET_DV)
    r = retention_chunkwise(q, k, v)
    mu = jnp.mean(r, axis=-1, keepdims=True)
    var = jnp.mean(jnp.square(r - mu), axis=-1, keepdims=True)
    r = ((r - mu) * lax.rsqrt(var + LN_EPS)).reshape(bsz, s, V_W) * ret_gn_g.astype(jnp.float32)
    ret_out = (jax.nn.silu(g) * r.astype(h.dtype)) @ w_ret_o

    u = glu_a * jax.nn.sigmoid(glu_b)
    u = lax.conv_general_dilated(u, conv_k, window_strides=(1,), padding=[(CONV_WIDTH - 1, 0)],
                                 dimension_numbers=('NWC', 'WIO', 'NWC'),
                                 feature_group_count=CONV_CH) + conv_b
    u = jax.nn.silu(layer_norm(u, conv_ln_g, conv_ln_b))
    conv_out = u @ w_conv_o

    merged = jax.nn.sigmoid(gate_r) * ret_out + jax.nn.sigmoid(gate_c) * conv_out
    return merged @ w_out


def _fwd_setup_inputs(seed: int = 0) -> dict:
    key = jax.random.key(seed)
    ks = jax.random.split(key, 24)
    beta = (8.0 * DEPTH) ** -0.25
    L = DEPTH

    def nrm(k, shape, scale):
        return jax.random.normal(k, shape, jnp.float32) * scale

    def gain(k, shape):
        return 1.0 + 0.02 * jax.random.normal(k, shape, jnp.float32)

    return {
        "x": jax.random.normal(ks[0], (BATCH, SEQ, D_MODEL), jnp.float32),
        "ffn1_w_gate": nrm(ks[1], (L, D_MODEL, D_FF), D_MODEL ** -0.5),
        "ffn1_w_up": nrm(ks[2], (L, D_MODEL, D_FF), D_MODEL ** -0.5),
        "ffn1_w_down": nrm(ks[3], (L, D_FF, D_MODEL), beta * D_FF ** -0.5),
        "ln1_g": gain(ks[4], (L, D_MODEL)),
        "ln1_b": nrm(ks[5], (L, D_MODEL), 0.02),
        "w_in": nrm(ks[6], (L, D_MODEL, IN_W), D_MODEL ** -0.5),
        "b_in": nrm(ks[7], (L, IN_W), 0.02),
        "ret_gn_g": gain(ks[8], (L, V_W)),
        "conv_k": nrm(ks[9], (L, CONV_WIDTH, 1, CONV_CH), CONV_WIDTH ** -0.5),
        "conv_b": nrm(ks[10], (L, CONV_CH), 0.02),
        "conv_ln_g": gain(ks[11], (L, CONV_CH)),
        "conv_ln_b": nrm(ks[12], (L, CONV_CH), 0.02),
        "w_ret_o": nrm(ks[13], (L, V_W, D_MODEL), beta * V_W ** -0.5),
        "w_conv_o": nrm(ks[14], (L, CONV_CH, D_MODEL), beta * CONV_CH ** -0.5),
        "w_out": nrm(ks[15], (L, D_MODEL, D_MODEL), beta * D_MODEL ** -0.5),
        "ln2_g": gain(ks[16], (L, D_MODEL)),
        "ln2_b": nrm(ks[17], (L, D_MODEL), 0.02),
        "ffn2_w_gate": nrm(ks[18], (L, D_MODEL, D_FF), D_MODEL ** -0.5),
        "ffn2_w_up": nrm(ks[19], (L, D_MODEL, D_FF), D_MODEL ** -0.5),
        "ffn2_w_down": nrm(ks[20], (L, D_FF, D_MODEL), beta * D_FF ** -0.5),
        "ln3_g": gain(ks[21], (L, D_MODEL)),
        "ln3_b": nrm(ks[22], (L, D_MODEL), 0.02),
    }


def _fwd_reference(x, ffn1_w_gate, ffn1_w_up, ffn1_w_down, ln1_g, ln1_b, w_in, b_in, ret_gn_g,
              conv_k, conv_b, conv_ln_g, conv_ln_b, w_ret_o, w_conv_o, w_out, ln2_g, ln2_b,
              ffn2_w_gate, ffn2_w_up, ffn2_w_down, ln3_g, ln3_b):
    alpha = (2.0 * DEPTH) ** 0.25
    s = x.shape[1]
    half = RET_DK // 2
    freqs = ROPE_BASE ** (-jnp.arange(half, dtype=jnp.float32) / half)
    ang = jnp.arange(s, dtype=jnp.float32)[:, None] * freqs[None, :]
    cos = jnp.cos(ang)[:, None, :]
    sin = jnp.sin(ang)[:, None, :]

    for l in range(DEPTH):
        x = layer_norm(alpha * x + 0.5 * swiglu_ffn(x, ffn1_w_gate[l], ffn1_w_up[l], ffn1_w_down[l]),
                       ln1_g[l], ln1_b[l])
        m = hybrid_mixer(x, w_in[l], b_in[l], ret_gn_g[l], conv_k[l], conv_b[l], conv_ln_g[l],
                         conv_ln_b[l], w_ret_o[l], w_conv_o[l], w_out[l], cos, sin)
        x = layer_norm(alpha * x + m, ln2_g[l], ln2_b[l])
        x = layer_norm(alpha * x + 0.5 * swiglu_ffn(x, ffn2_w_gate[l], ffn2_w_up[l], ffn2_w_down[l]),
                       ln3_g[l], ln3_b[l])
    return x


import jax as _jax
import jax.numpy as _jnp

TWIN_FORMAT = 'train_step'
FWD_PARAMS = ['x', 'ffn1_w_gate', 'ffn1_w_up', 'ffn1_w_down', 'ln1_g', 'ln1_b', 'w_in', 'b_in', 'ret_gn_g', 'conv_k', 'conv_b', 'conv_ln_g', 'conv_ln_b', 'w_ret_o', 'w_conv_o', 'w_out', 'ln2_g', 'ln2_b', 'ffn2_w_gate', 'ffn2_w_up', 'ffn2_w_down', 'ln3_g', 'ln3_b']
TWIN_WEIGHTS = ['ffn1_w_gate', 'ffn1_w_up', 'ffn1_w_down', 'ln1_g', 'ln1_b', 'w_in', 'b_in', 'ret_gn_g', 'conv_k', 'conv_b', 'conv_ln_g', 'conv_ln_b', 'w_ret_o', 'w_conv_o', 'w_out', 'ln2_g', 'ln2_b', 'ffn2_w_gate', 'ffn2_w_up', 'ffn2_w_down', 'ln3_g', 'ln3_b']
TWIN_DIFF_INPUT = 'x'
TWIN_INPUTS = ['x', 'ffn1_w_gate', 'ffn1_w_up', 'ffn1_w_down', 'ln1_g', 'ln1_b', 'w_in', 'b_in', 'ret_gn_g', 'conv_k', 'conv_b', 'conv_ln_g', 'conv_ln_b', 'w_ret_o', 'w_conv_o', 'w_out', 'ln2_g', 'ln2_b', 'ffn2_w_gate', 'ffn2_w_up', 'ffn2_w_down', 'ln3_g', 'ln3_b', 'loss_target', 'm_ffn1_w_gate', 'm_ffn1_w_up', 'm_ffn1_w_down', 'm_ln1_g', 'm_ln1_b', 'm_w_in', 'm_b_in', 'm_ret_gn_g', 'm_conv_k', 'm_conv_b', 'm_conv_ln_g', 'm_conv_ln_b', 'm_w_ret_o', 'm_w_conv_o', 'm_w_out', 'm_ln2_g', 'm_ln2_b', 'm_ffn2_w_gate', 'm_ffn2_w_up', 'm_ffn2_w_down', 'm_ln3_g', 'm_ln3_b', 'v_ffn1_w_gate', 'v_ffn1_w_up', 'v_ffn1_w_down', 'v_ln1_g', 'v_ln1_b', 'v_w_in', 'v_b_in', 'v_ret_gn_g', 'v_conv_k', 'v_conv_b', 'v_conv_ln_g', 'v_conv_ln_b', 'v_w_ret_o', 'v_w_conv_o', 'v_w_out', 'v_ln2_g', 'v_ln2_b', 'v_ffn2_w_gate', 'v_ffn2_w_up', 'v_ffn2_w_down', 'v_ln3_g', 'v_ln3_b']
TWIN_OUTPUTS = ['loss', 'grad_x', 'grad_ffn1_w_gate', 'grad_ffn1_w_up', 'grad_ffn1_w_down', 'grad_ln1_g', 'grad_ln1_b', 'grad_w_in', 'grad_b_in', 'grad_ret_gn_g', 'grad_conv_k', 'grad_conv_b', 'grad_conv_ln_g', 'grad_conv_ln_b', 'grad_w_ret_o', 'grad_w_conv_o', 'grad_w_out', 'grad_ln2_g', 'grad_ln2_b', 'grad_ffn2_w_gate', 'grad_ffn2_w_up', 'grad_ffn2_w_down', 'grad_ln3_g', 'grad_ln3_b', 'delta_ffn1_w_gate', 'delta_ffn1_w_up', 'delta_ffn1_w_down', 'delta_ln1_g', 'delta_ln1_b', 'delta_w_in', 'delta_b_in', 'delta_ret_gn_g', 'delta_conv_k', 'delta_conv_b', 'delta_conv_ln_g', 'delta_conv_ln_b', 'delta_w_ret_o', 'delta_w_conv_o', 'delta_w_out', 'delta_ln2_g', 'delta_ln2_b', 'delta_ffn2_w_gate', 'delta_ffn2_w_up', 'delta_ffn2_w_down', 'delta_ln3_g', 'delta_ln3_b', 'new_m_ffn1_w_gate', 'new_m_ffn1_w_up', 'new_m_ffn1_w_down', 'new_m_ln1_g', 'new_m_ln1_b', 'new_m_w_in', 'new_m_b_in', 'new_m_ret_gn_g', 'new_m_conv_k', 'new_m_conv_b', 'new_m_conv_ln_g', 'new_m_conv_ln_b', 'new_m_w_ret_o', 'new_m_w_conv_o', 'new_m_w_out', 'new_m_ln2_g', 'new_m_ln2_b', 'new_m_ffn2_w_gate', 'new_m_ffn2_w_up', 'new_m_ffn2_w_down', 'new_m_ln3_g', 'new_m_ln3_b', 'new_v_ffn1_w_gate', 'new_v_ffn1_w_up', 'new_v_ffn1_w_down', 'new_v_ln1_g', 'new_v_ln1_b', 'new_v_w_in', 'new_v_b_in', 'new_v_ret_gn_g', 'new_v_conv_k', 'new_v_conv_b', 'new_v_conv_ln_g', 'new_v_conv_ln_b', 'new_v_w_ret_o', 'new_v_w_conv_o', 'new_v_w_out', 'new_v_ln2_g', 'new_v_ln2_b', 'new_v_ffn2_w_gate', 'new_v_ffn2_w_up', 'new_v_ffn2_w_down', 'new_v_ln3_g', 'new_v_ln3_b']
TWIN_LEAF_KINDS = {'loss': 'loss', 'grad_x': 'grad_x', 'grad_ffn1_w_gate': 'grad_w', 'grad_ffn1_w_up': 'grad_w', 'grad_ffn1_w_down': 'grad_w', 'grad_ln1_g': 'grad_w', 'grad_ln1_b': 'grad_w', 'grad_w_in': 'grad_w', 'grad_b_in': 'grad_w', 'grad_ret_gn_g': 'grad_w', 'grad_conv_k': 'grad_w', 'grad_conv_b': 'grad_w', 'grad_conv_ln_g': 'grad_w', 'grad_conv_ln_b': 'grad_w', 'grad_w_ret_o': 'grad_w', 'grad_w_conv_o': 'grad_w', 'grad_w_out': 'grad_w', 'grad_ln2_g': 'grad_w', 'grad_ln2_b': 'grad_w', 'grad_ffn2_w_gate': 'grad_w', 'grad_ffn2_w_up': 'grad_w', 'grad_ffn2_w_down': 'grad_w', 'grad_ln3_g': 'grad_w', 'grad_ln3_b': 'grad_w', 'delta_ffn1_w_gate': 'delta_w', 'delta_ffn1_w_up': 'delta_w', 'delta_ffn1_w_down': 'delta_w', 'delta_ln1_g': 'delta_w', 'delta_ln1_b': 'delta_w', 'delta_w_in': 'delta_w', 'delta_b_in': 'delta_w', 'delta_ret_gn_g': 'delta_w', 'delta_conv_k': 'delta_w', 'delta_conv_b': 'delta_w', 'delta_conv_ln_g': 'delta_w', 'delta_conv_ln_b': 'delta_w', 'delta_w_ret_o': 'delta_w', 'delta_w_conv_o': 'delta_w', 'delta_w_out': 'delta_w', 'delta_ln2_g': 'delta_w', 'delta_ln2_b': 'delta_w', 'delta_ffn2_w_gate': 'delta_w', 'delta_ffn2_w_up': 'delta_w', 'delta_ffn2_w_down': 'delta_w', 'delta_ln3_g': 'delta_w', 'delta_ln3_b': 'delta_w', 'new_m_ffn1_w_gate': 'new_m', 'new_m_ffn1_w_up': 'new_m', 'new_m_ffn1_w_down': 'new_m', 'new_m_ln1_g': 'new_m', 'new_m_ln1_b': 'new_m', 'new_m_w_in': 'new_m', 'new_m_b_in': 'new_m', 'new_m_ret_gn_g': 'new_m', 'new_m_conv_k': 'new_m', 'new_m_conv_b': 'new_m', 'new_m_conv_ln_g': 'new_m', 'new_m_conv_ln_b': 'new_m', 'new_m_w_ret_o': 'new_m', 'new_m_w_conv_o': 'new_m', 'new_m_w_out': 'new_m', 'new_m_ln2_g': 'new_m', 'new_m_ln2_b': 'new_m', 'new_m_ffn2_w_gate': 'new_m', 'new_m_ffn2_w_up': 'new_m', 'new_m_ffn2_w_down': 'new_m', 'new_m_ln3_g': 'new_m', 'new_m_ln3_b': 'new_m', 'new_v_ffn1_w_gate': 'new_v', 'new_v_ffn1_w_up': 'new_v', 'new_v_ffn1_w_down': 'new_v', 'new_v_ln1_g': 'new_v', 'new_v_ln1_b': 'new_v', 'new_v_w_in': 'new_v', 'new_v_b_in': 'new_v', 'new_v_ret_gn_g': 'new_v', 'new_v_conv_k': 'new_v', 'new_v_conv_b': 'new_v', 'new_v_conv_ln_g': 'new_v', 'new_v_conv_ln_b': 'new_v', 'new_v_w_ret_o': 'new_v', 'new_v_w_conv_o': 'new_v', 'new_v_w_out': 'new_v', 'new_v_ln2_g': 'new_v', 'new_v_ln2_b': 'new_v', 'new_v_ffn2_w_gate': 'new_v', 'new_v_ffn2_w_up': 'new_v', 'new_v_ffn2_w_down': 'new_v', 'new_v_ln3_g': 'new_v', 'new_v_ln3_b': 'new_v'}


def _forward(args):
    return _fwd_reference(*[args[k] for k in FWD_PARAMS])


def _output_shape():
    out = _jax.eval_shape(lambda: _forward(_fwd_setup_inputs(0)))
    return out.shape, out.dtype

N_MICROBATCH = 1
ADAM_LR = 0.001
ADAM_B1 = 0.9
ADAM_B2 = 0.999
ADAM_EPS = 1e-08
ADAM_WD = 0.01
ADAM_STEP = 10
PER_EXAMPLE_BATCH_AXIS = {'x': 0, 'loss_target': 0}
SHARED_INPUTS = []
_WEIGHT_DTYPES = {'ffn1_w_gate': _jnp.float32, 'ffn1_w_up': _jnp.float32, 'ffn1_w_down': _jnp.float32, 'ln1_g': _jnp.float32, 'ln1_b': _jnp.float32, 'w_in': _jnp.float32, 'b_in': _jnp.float32, 'ret_gn_g': _jnp.float32, 'conv_k': _jnp.float32, 'conv_b': _jnp.float32, 'conv_ln_g': _jnp.float32, 'conv_ln_b': _jnp.float32, 'w_ret_o': _jnp.float32, 'w_conv_o': _jnp.float32, 'w_out': _jnp.float32, 'ln2_g': _jnp.float32, 'ln2_b': _jnp.float32, 'ffn2_w_gate': _jnp.float32, 'ffn2_w_up': _jnp.float32, 'ffn2_w_down': _jnp.float32, 'ln3_g': _jnp.float32, 'ln3_b': _jnp.float32}
MOMENT_SCALE = {'ffn1_w_gate': 1.670748e-02, 'ffn1_w_up': 1.624057e-02, 'ffn1_w_down': 4.521745e-02, 'ln1_g': 9.984239e-01, 'ln1_b': 4.480118e-01, 'w_in': 1.291926e-02, 'b_in': 1.486827e-02, 'ret_gn_g': 1.207633e-02, 'conv_k': 1.815550e-02, 'conv_b': 5.467086e-02, 'conv_ln_g': 2.792123e-02, 'conv_ln_b': 3.315449e-02, 'w_ret_o': 2.866334e-02, 'w_conv_o': 3.436953e-02, 'w_out': 4.432424e-02, 'ln2_g': 1.042651e+00, 'ln2_b': 4.423141e-01, 'ffn2_w_gate': 1.629361e-02, 'ffn2_w_up': 1.582359e-02, 'ffn2_w_down': 4.414529e-02, 'ln3_g': 3.206986e+01, 'ln3_b': 1.248006e+00}


def _to_microbatches(a, axis):
    t = _jnp.moveaxis(a, axis, 0)
    t = t.reshape((N_MICROBATCH, t.shape[0] // N_MICROBATCH) + t.shape[1:])
    return _jnp.moveaxis(t, 1, axis + 1)


def setup_inputs(seed: int = 0) -> dict:
    inp = _fwd_setup_inputs(seed)
    key = _jax.random.fold_in(_jax.random.key(seed), 7919)
    shape, _ = _output_shape()
    out = dict(inp)
    out["loss_target"] = _jax.random.normal(_jax.random.fold_in(key, 0), shape, _jnp.float32)
    for i, name in enumerate(TWIN_WEIGHTS):
        w = inp[name].astype(_jnp.float32)
        if MOMENT_SCALE is None:
            s = _jnp.sqrt(_jnp.mean(_jnp.square(w)) + 1e-30)
        else:
            s = MOMENT_SCALE[name]
        km, kv = _jax.random.split(_jax.random.fold_in(key, i + 1))
        out[name] = w
        out["m_" + name] = s * _jax.random.normal(km, w.shape, _jnp.float32)
        out["v_" + name] = (s * s) * _jax.random.uniform(kv, w.shape, _jnp.float32, 0.5, 1.5)
    if N_MICROBATCH > 1:
        for name, axis in PER_EXAMPLE_BATCH_AXIS.items():
            out[name] = _to_microbatches(out[name], axis)
    return {'x': out['x'], 'ffn1_w_gate': out['ffn1_w_gate'], 'ffn1_w_up': out['ffn1_w_up'], 'ffn1_w_down': out['ffn1_w_down'], 'ln1_g': out['ln1_g'], 'ln1_b': out['ln1_b'], 'w_in': out['w_in'], 'b_in': out['b_in'], 'ret_gn_g': out['ret_gn_g'], 'conv_k': out['conv_k'], 'conv_b': out['conv_b'], 'conv_ln_g': out['conv_ln_g'], 'conv_ln_b': out['conv_ln_b'], 'w_ret_o': out['w_ret_o'], 'w_conv_o': out['w_conv_o'], 'w_out': out['w_out'], 'ln2_g': out['ln2_g'], 'ln2_b': out['ln2_b'], 'ffn2_w_gate': out['ffn2_w_gate'], 'ffn2_w_up': out['ffn2_w_up'], 'ffn2_w_down': out['ffn2_w_down'], 'ln3_g': out['ln3_g'], 'ln3_b': out['ln3_b'], 'loss_target': out['loss_target'], 'm_ffn1_w_gate': out['m_ffn1_w_gate'], 'm_ffn1_w_up': out['m_ffn1_w_up'], 'm_ffn1_w_down': out['m_ffn1_w_down'], 'm_ln1_g': out['m_ln1_g'], 'm_ln1_b': out['m_ln1_b'], 'm_w_in': out['m_w_in'], 'm_b_in': out['m_b_in'], 'm_ret_gn_g': out['m_ret_gn_g'], 'm_conv_k': out['m_conv_k'], 'm_conv_b': out['m_conv_b'], 'm_conv_ln_g': out['m_conv_ln_g'], 'm_conv_ln_b': out['m_conv_ln_b'], 'm_w_ret_o': out['m_w_ret_o'], 'm_w_conv_o': out['m_w_conv_o'], 'm_w_out': out['m_w_out'], 'm_ln2_g': out['m_ln2_g'], 'm_ln2_b': out['m_ln2_b'], 'm_ffn2_w_gate': out['m_ffn2_w_gate'], 'm_ffn2_w_up': out['m_ffn2_w_up'], 'm_ffn2_w_down': out['m_ffn2_w_down'], 'm_ln3_g': out['m_ln3_g'], 'm_ln3_b': out['m_ln3_b'], 'v_ffn1_w_gate': out['v_ffn1_w_gate'], 'v_ffn1_w_up': out['v_ffn1_w_up'], 'v_ffn1_w_down': out['v_ffn1_w_down'], 'v_ln1_g': out['v_ln1_g'], 'v_ln1_b': out['v_ln1_b'], 'v_w_in': out['v_w_in'], 'v_b_in': out['v_b_in'], 'v_ret_gn_g': out['v_ret_gn_g'], 'v_conv_k': out['v_conv_k'], 'v_conv_b': out['v_conv_b'], 'v_conv_ln_g': out['v_conv_ln_g'], 'v_conv_ln_b': out['v_conv_ln_b'], 'v_w_ret_o': out['v_w_ret_o'], 'v_w_conv_o': out['v_w_conv_o'], 'v_w_out': out['v_w_out'], 'v_ln2_g': out['v_ln2_g'], 'v_ln2_b': out['v_ln2_b'], 'v_ffn2_w_gate': out['v_ffn2_w_gate'], 'v_ffn2_w_up': out['v_ffn2_w_up'], 'v_ffn2_w_down': out['v_ffn2_w_down'], 'v_ln3_g': out['v_ln3_g'], 'v_ln3_b': out['v_ln3_b']}


def _loss(weights, diff, rest, loss_target):
    with _jax.named_scope("forward"):
        args = {**rest, TWIN_DIFF_INPUT: diff, **{k: w.astype(_WEIGHT_DTYPES[k]) for k, w in weights.items()}}
        y = _forward(args)
    with _jax.named_scope("loss_head"):
        err = _jnp.square(y.astype(_jnp.float32) - loss_target)
        return 0.5 * _jnp.sum(_jnp.mean(err, axis=-1)) if err.ndim else 0.5 * err


def _adamw(w, g, m, v):
    m = ADAM_B1 * m + (1.0 - ADAM_B1) * g
    v = ADAM_B2 * v + (1.0 - ADAM_B2) * _jnp.square(g)
    m_hat = m / (1.0 - ADAM_B1 ** ADAM_STEP)
    v_hat = v / (1.0 - ADAM_B2 ** ADAM_STEP)
    delta = -ADAM_LR * (m_hat / (_jnp.sqrt(v_hat) + ADAM_EPS) + ADAM_WD * w)
    return delta, m, v


def reference(x, ffn1_w_gate, ffn1_w_up, ffn1_w_down, ln1_g, ln1_b, w_in, b_in, ret_gn_g, conv_k, conv_b, conv_ln_g, conv_ln_b, w_ret_o, w_conv_o, w_out, ln2_g, ln2_b, ffn2_w_gate, ffn2_w_up, ffn2_w_down, ln3_g, ln3_b, loss_target, m_ffn1_w_gate, m_ffn1_w_up, m_ffn1_w_down, m_ln1_g, m_ln1_b, m_w_in, m_b_in, m_ret_gn_g, m_conv_k, m_conv_b, m_conv_ln_g, m_conv_ln_b, m_w_ret_o, m_w_conv_o, m_w_out, m_ln2_g, m_ln2_b, m_ffn2_w_gate, m_ffn2_w_up, m_ffn2_w_down, m_ln3_g, m_ln3_b, v_ffn1_w_gate, v_ffn1_w_up, v_ffn1_w_down, v_ln1_g, v_ln1_b, v_w_in, v_b_in, v_ret_gn_g, v_conv_k, v_conv_b, v_conv_ln_g, v_conv_ln_b, v_w_ret_o, v_w_conv_o, v_w_out, v_ln2_g, v_ln2_b, v_ffn2_w_gate, v_ffn2_w_up, v_ffn2_w_down, v_ln3_g, v_ln3_b):
    given = dict(x=x, ffn1_w_gate=ffn1_w_gate, ffn1_w_up=ffn1_w_up, ffn1_w_down=ffn1_w_down, ln1_g=ln1_g, ln1_b=ln1_b, w_in=w_in, b_in=b_in, ret_gn_g=ret_gn_g, conv_k=conv_k, conv_b=conv_b, conv_ln_g=conv_ln_g, conv_ln_b=conv_ln_b, w_ret_o=w_ret_o, w_conv_o=w_conv_o, w_out=w_out, ln2_g=ln2_g, ln2_b=ln2_b, ffn2_w_gate=ffn2_w_gate, ffn2_w_up=ffn2_w_up, ffn2_w_down=ffn2_w_down, ln3_g=ln3_g, ln3_b=ln3_b, loss_target=loss_target, m_ffn1_w_gate=m_ffn1_w_gate, m_ffn1_w_up=m_ffn1_w_up, m_ffn1_w_down=m_ffn1_w_down, m_ln1_g=m_ln1_g, m_ln1_b=m_ln1_b, m_w_in=m_w_in, m_b_in=m_b_in, m_ret_gn_g=m_ret_gn_g, m_conv_k=m_conv_k, m_conv_b=m_conv_b, m_conv_ln_g=m_conv_ln_g, m_conv_ln_b=m_conv_ln_b, m_w_ret_o=m_w_ret_o, m_w_conv_o=m_w_conv_o, m_w_out=m_w_out, m_ln2_g=m_ln2_g, m_ln2_b=m_ln2_b, m_ffn2_w_gate=m_ffn2_w_gate, m_ffn2_w_up=m_ffn2_w_up, m_ffn2_w_down=m_ffn2_w_down, m_ln3_g=m_ln3_g, m_ln3_b=m_ln3_b, v_ffn1_w_gate=v_ffn1_w_gate, v_ffn1_w_up=v_ffn1_w_up, v_ffn1_w_down=v_ffn1_w_down, v_ln1_g=v_ln1_g, v_ln1_b=v_ln1_b, v_w_in=v_w_in, v_b_in=v_b_in, v_ret_gn_g=v_ret_gn_g, v_conv_k=v_conv_k, v_conv_b=v_conv_b, v_conv_ln_g=v_conv_ln_g, v_conv_ln_b=v_conv_ln_b, v_w_ret_o=v_w_ret_o, v_w_conv_o=v_w_conv_o, v_w_out=v_w_out, v_ln2_g=v_ln2_g, v_ln2_b=v_ln2_b, v_ffn2_w_gate=v_ffn2_w_gate, v_ffn2_w_up=v_ffn2_w_up, v_ffn2_w_down=v_ffn2_w_down, v_ln3_g=v_ln3_g, v_ln3_b=v_ln3_b)
    weights = {n: given[n] for n in TWIN_WEIGHTS}
    shared = {n: given[n] for n in SHARED_INPUTS}
    per_example = {n: given[n] for n in ['x']}
    grad_fn = _jax.value_and_grad(_loss, argnums=(0, 1))

    def one_microbatch(ex, loss_target):
        ex = dict(ex)
        diff = ex.pop(TWIN_DIFF_INPUT)
        return grad_fn(weights, diff, {**shared, **ex}, loss_target)

    if N_MICROBATCH == 1:
        loss, (grad_w, grad_x) = one_microbatch(per_example, given["loss_target"])
    else:
        def body(carry, xs):
            loss_sum, grad_sum = carry
            l_k, (gw_k, gx_k) = one_microbatch(xs[0], xs[1])
            with _jax.named_scope("update"):
                return (loss_sum + l_k, _jax.tree.map(_jnp.add, grad_sum, gw_k)), gx_k

        init = (_jnp.zeros((), _jnp.float32), _jax.tree.map(_jnp.zeros_like, weights))
        (loss, grad_w), grad_x = _jax.lax.scan(body, init, (per_example, given["loss_target"]))
    with _jax.named_scope("update"):
        delta_w, new_m, new_v = {}, {}, {}
        for n in TWIN_WEIGHTS:
            delta_w[n], new_m[n], new_v[n] = _adamw(weights[n], grad_w[n], given["m_" + n], given["v_" + n])
    return (loss, grad_x, *[grad_w[n] for n in TWIN_WEIGHTS], *[delta_w[n] for n in TWIN_WEIGHTS],
            *[new_m[n] for n in TWIN_WEIGHTS], *[new_v[n] for n in TWIN_WEIGHTS])
```

```python
import functools
import math

import numpy as np
import jax
import jax.numpy as jnp
from jax import lax
from jax.experimental import pallas as pl
from jax.experimental.pallas import tpu as pltpu

F32 = jnp.float32
BF16 = jnp.bfloat16

D = 1024
FS = 704
FSP = 768
FP = 4 * FSP
H = 8
DK = 128
DV = 256
CH = 128
VW = H * DV
INW = 10240
INS = INW // 4
CONV_W = 31
HALO = 32
EPS = 1e-5
ALPHA = 2.0 ** 0.25
ROPE_BASE = 10000.0
NCHIP = 4

ADAM_LR, ADAM_B1, ADAM_B2, ADAM_EPS, ADAM_WD, ADAM_STEP = 0.001, 0.9, 0.999, 1e-08, 0.01, 10

OFF = {"w_in": 0, "w_ret_o": 2560, "g1": 3072, "u1": 3840, "d1": 4608,
       "g2": 5376, "u2": 6144, "d2": 6912, "w_conv_o": 7680, "w_out": 7936}
WCOLS = 8192
VMEM_LIMIT = 56 << 20


def _cp(*sem, **kw):
    return pltpu.CompilerParams(dimension_semantics=sem, vmem_limit_bytes=VMEM_LIMIT, **kw)


def _sig(x):
    return 1.0 / (1.0 + jnp.exp(-x))


def _dot(a, b):
    return jnp.dot(a, b, preferred_element_type=F32)


def _dot_nt(a, b):
    return lax.dot_general(a, b, (((1,), (1,)), ((), ())), preferred_element_type=F32)


def _ln_fwd(z, g, b):
    mu = jnp.mean(z, axis=-1, keepdims=True)
    xc = z - mu
    var = jnp.mean(xc * xc, axis=-1, keepdims=True)
    rstd = lax.rsqrt(var + EPS)
    xh = xc * rstd
    return xh * g + b, xh, rstd


def _ln_bwd(dy, xh, rstd, g):
    dxh = dy * g
    m1 = jnp.mean(dxh, axis=-1, keepdims=True)
    m2 = jnp.mean(dxh * xh, axis=-1, keepdims=True)
    return rstd * (dxh - m1 - xh * m2)


def _colsum(x):
    return jnp.sum(x, axis=0, keepdims=True)


def _acc_rows(ref, first, val):
    @pl.when(first)
    def _():
        ref[...] = val

    @pl.when(jnp.logical_not(first))
    def _():
        ref[...] += val


def _rope_tables(T):
    half = DK // 2
    freqs = ROPE_BASE ** (-np.arange(half, dtype=np.float32) / half)
    ang = (np.arange(T, dtype=np.float32)[:, None] * freqs[None, :]).astype(np.float32)
    cos, sin = np.cos(ang), np.sin(ang)
    return (jnp.asarray(np.concatenate([cos, cos], 1), F32),
            jnp.asarray(np.concatenate([-sin, sin], 1), F32))


def _decay_tables():
    h = np.arange(H, dtype=np.float64)
    log_g = np.log(1.0 - np.exp2(-5.0 - h))
    idx = np.arange(CH, dtype=np.float64)
    diff = idx[:, None] - idx[None, :]
    dm = np.where(diff[None] >= 0, np.exp(np.maximum(diff, 0.0)[None] * log_g[:, None, None]), 0.0)
    xi = np.exp((idx[None, :] + 1.0) * log_g[:, None])
    zeta = np.exp((CH - 1.0 - idx)[None, :] * log_g[:, None])
    cd = np.exp(CH * log_g)
    xi_t = np.broadcast_to(xi[:, :, None], (H, CH, DV))
    zeta_t = np.broadcast_to(zeta[:, :, None], (H, CH, DK))
    return (jnp.asarray(dm, F32), jnp.asarray(xi_t, F32), jnp.asarray(zeta_t, F32),
            [float(v) for v in cd])


def _cast_t(x):
    T = x.shape[0]
    tm = min(T, 512)

    def body(x_ref, xb_ref, xt_ref):
        v = x_ref[...]
        xb_ref[...] = v.astype(BF16)
        xt_ref[...] = v.T.astype(BF16)

    return pl.pallas_call(
        body, name="cast_t", grid=(T // tm,),
        in_specs=[pl.BlockSpec((tm, D), lambda i: (i, 0))],
        out_specs=[pl.BlockSpec((tm, D), lambda i: (i, 0)), pl.BlockSpec((D, tm), lambda i: (0, i))],
        out_shape=[jax.ShapeDtypeStruct((T, D), BF16), jax.ShapeDtypeStruct((D, T), BF16)],
        compiler_params=_cp("parallel"))(x)


def _ffn_up(xb, wall, og, ou, name):
    T = xb.shape[0]
    tm, tn = min(T, 1024), 256
    nps = FSP // tn

    def body(x_ref, wg_ref, wu_ref, a_ref, b_ref, h_ref):
        x = x_ref[...]
        a = _dot(x, wg_ref[...])
        b = _dot(x, wu_ref[...])
        a_ref[...] = a.astype(BF16)
        b_ref[...] = b.astype(BF16)
        h_ref[...] = (a * _sig(a) * b).astype(BF16)

    def wspec(off):
        return pl.BlockSpec((None, D, tn), lambda i, j: (j // nps, 0, off // tn + j % nps))

    ospec = pl.BlockSpec((tm, tn), lambda i, j: (i, j))
    return pl.pallas_call(
        body, name=name, grid=(T // tm, FP // tn),
        in_specs=[pl.BlockSpec((tm, D), lambda i, j: (i, 0)), wspec(og), wspec(ou)],
        out_specs=[ospec] * 3, out_shape=[jax.ShapeDtypeStruct((T, FP), BF16)] * 3,
        compiler_params=_cp("parallel", "parallel"))(xb, wall, wall)


def _proj_ln(hb, wall, off, res, g, b, coef, name, want_b=True):
    T, K = hb.shape
    ks = K // NCHIP
    tm = min(T, 256)

    def body(h_ref, w_ref, r_ref, g_ref, b_ref, z_ref, *rest):
        acc = _dot_nt(h_ref[:, 0:ks], w_ref[0])
        for s in range(1, NCHIP):
            acc += _dot_nt(h_ref[:, s * ks:(s + 1) * ks], w_ref[s])
        z = ALPHA * r_ref[...] + coef * acc
        z_ref[...] = z
        if want_b:
            y, _, _ = _ln_fwd(z, g_ref[...], b_ref[...])
            y_ref, yb_ref, yt_ref = rest
            y_ref[...] = y
            yb_ref[...] = y.astype(BF16)
            yt_ref[...] = y.T.astype(BF16)

    row = pl.BlockSpec((tm, D), lambda i: (i, 0))
    vec = pl.BlockSpec((1, D), lambda i: (0, 0))
    out_specs = [row]
    out_shape = [jax.ShapeDtypeStruct((T, D), F32)]
    if want_b:
        out_specs += [row, row, pl.BlockSpec((D, tm), lambda i: (0, i))]
        out_shape += [jax.ShapeDtypeStruct((T, D), F32), jax.ShapeDtypeStruct((T, D), BF16),
                      jax.ShapeDtypeStruct((D, T), BF16)]
    return pl.pallas_call(
        body, name=name, grid=(T // tm,),
        in_specs=[pl.BlockSpec((tm, K), lambda i: (i, 0)),
                  pl.BlockSpec((NCHIP, D, ks), lambda i: (0, 0, off // ks)), row, vec, vec],
        out_specs=out_specs, out_shape=out_shape,
        compiler_params=_cp("parallel"))(hb, wall, res, g, b)


def _inproj(xb, wall, b_in, cos_t, sin_t):
    T = xb.shape[0]
    tm, tn = min(T, 1024), 256
    nps = INS // tn
    nqk = 2 * D // tn

    def body(x_ref, w_ref, bias_ref, cos_ref, sin_ref, o_ref):
        j = pl.program_id(1)
        acc = _dot(x_ref[...], w_ref[...]) + bias_ref[...]

        @pl.when(j >= nqk)
        def _():
            o_ref[...] = acc.astype(BF16)

        @pl.when(j < nqk)
        def _():
            scale = jnp.where(j < nqk // 2, DK ** -0.5, 1.0).astype(F32)
            c = cos_ref[...]
            s = sin_ref[...]
            for hh in range(tn // DK):
                xh = acc[:, hh * DK:(hh + 1) * DK]
                o = (xh * c + pltpu.roll(xh, DK // 2, 1) * s) * scale
                o_ref[:, hh * DK:(hh + 1) * DK] = o.astype(BF16)

    return pl.pallas_call(
        body, name="inproj", grid=(T // tm, INW // tn),
        in_specs=[pl.BlockSpec((tm, D), lambda i, j: (i, 0)),
                  pl.BlockSpec((None, D, tn), lambda i, j: (j // nps, 0, OFF["w_in"] // tn + j % nps)),
                  pl.BlockSpec((1, tn), lambda i, j: (0, j)),
                  pl.BlockSpec((tm, DK), lambda i, j: (i, 0)),
                  pl.BlockSpec((tm, DK), lambda i, j: (i, 0))],
        out_specs=pl.BlockSpec((tm, tn), lambda i, j: (i, j)),
        out_shape=jax.ShapeDtypeStruct((T, INW), BF16),
        compiler_params=_cp("parallel", "parallel"))(xb, wall, b_in, cos_t, sin_t)


def _retention_fwd(proj, gn_g, dm_t, xi_t, zeta_t, cds):
    T = proj.shape[0]
    n = T // CH

    def body(q_ref, k_ref, v_ref, g_ref, gn_ref, dm_ref, xi_ref, zt_ref, r_ref, ri_ref, st_ref, state):
        @pl.when(pl.program_id(0) == 0)
        def _():
            state[...] = jnp.zeros_like(state)

        for h in range(H):
            q = q_ref[:, h * DK:(h + 1) * DK]
            k = k_ref[:, h * DK:(h + 1) * DK]
            v = v_ref[:, h * DV:(h + 1) * DV]
            rows = slice(h * DK, (h + 1) * DK)
            s_prev = state[rows, :]
            s_b = s_prev.astype(BF16)
            st_ref[rows, :] = s_b
            sc = _dot_nt(q, k) * dm_ref[h]
            r = _dot(sc.astype(BF16), v) + _dot(q, s_b) * xi_ref[h]
            kz = k.astype(F32) * zt_ref[h]
            state[rows, :] = cds[h] * s_prev + _dot(kz.T.astype(BF16), v)
            cols = slice(h * DV, (h + 1) * DV)
            r_ref[:, cols] = r
            mu = jnp.mean(r, axis=-1, keepdims=True)
            xc = r - mu
            var = jnp.mean(xc * xc, axis=-1, keepdims=True)
            y = xc * lax.rsqrt(var + EPS) * gn_ref[:, cols]
            g = g_ref[:, cols].astype(F32)
            ri_ref[:, cols] = (g * _sig(g) * y).astype(BF16)

    full3 = lambda shp: pl.BlockSpec(shp, lambda c: (0, 0, 0))
    return pl.pallas_call(
        body, name="retention_fwd", grid=(n,),
        in_specs=[pl.BlockSpec((CH, D), lambda c: (c, 0)),
                  pl.BlockSpec((CH, D), lambda c: (c, 1)),
                  pl.BlockSpec((CH, VW), lambda c: (c, 1)),
                  pl.BlockSpec((CH, VW), lambda c: (c, 2)),
                  pl.BlockSpec((1, VW), lambda c: (0, 0)),
                  full3((H, CH, CH)), full3((H, CH, DV)), full3((H, CH, DK))],
        out_specs=[pl.BlockSpec((CH, VW), lambda c: (c, 0)), pl.BlockSpec((CH, VW), lambda c: (c, 0)),
                   pl.BlockSpec((None, H * DK, DV), lambda c: (c, 0, 0))],
        out_shape=[jax.ShapeDtypeStruct((T, VW), F32), jax.ShapeDtypeStruct((T, VW), BF16),
                   jax.ShapeDtypeStruct((n, H * DK, DV), BF16)],
        scratch_shapes=[pltpu.VMEM((H * DK, DV), F32)],
        compiler_params=_cp("arbitrary"))(proj, proj, proj, proj, gn_g, dm_t, xi_t, zeta_t)


CONV_TT = 256
CONV_SB = 64
CONV_CB = 256


def _glu(a_ref, b_ref, rows=slice(None)):
    a = a_ref[rows, :].astype(F32)
    sb = _sig(b_ref[rows, :].astype(F32))
    return a, sb


def _conv_fwd(proj, conv_k, conv_b, ln_g, ln_b):
    T = proj.shape[0]
    tt = min(T, CONV_TT)
    ca, cb = 6 * D // D, 7 * D // D

    def body(a_ref, b_ref, pa_ref, pb_ref, k_ref, cb_ref, g_ref, bb_ref, u1_ref, u3_ref, win):
        i = pl.program_id(0)
        a, sb = _glu(a_ref, b_ref)
        win[HALO:, :] = a * sb
        pa, psb = _glu(pa_ref, pb_ref, slice(tt - HALO, tt))
        win[0:HALO, :] = jnp.where(i > 0, pa * psb, 0.0)
        for c0 in range(0, D, CONV_CB):
            cs = slice(c0, c0 + CONV_CB)
            for r0 in range(0, tt, CONV_SB):
                acc = jnp.zeros((CONV_SB, CONV_CB), F32)
                for w in range(CONV_W):
                    st = r0 + HALO - (CONV_W - 1) + w
                    acc += win[st:st + CONV_SB, cs] * k_ref[w:w + 1, cs]
                u1_ref[r0:r0 + CONV_SB, cs] = acc + cb_ref[:, cs]
        u2, _, _ = _ln_fwd(u1_ref[...], g_ref[...], bb_ref[...])
        u3_ref[...] = (u2 * _sig(u2)).astype(BF16)

    vec = pl.BlockSpec((1, D), lambda i: (0, 0))
    row = pl.BlockSpec((tt, D), lambda i: (i, 0))
    return pl.pallas_call(
        body, name="conv_fwd", grid=(T // tt,),
        in_specs=[pl.BlockSpec((tt, D), lambda i: (i, ca)), pl.BlockSpec((tt, D), lambda i: (i, cb)),
                  pl.BlockSpec((tt, D), lambda i: (jnp.maximum(i - 1, 0), ca)),
                  pl.BlockSpec((tt, D), lambda i: (jnp.maximum(i - 1, 0), cb)),
                  pl.BlockSpec((CONV_W, D), lambda i: (0, 0)), vec, vec, vec],
        out_specs=[row, row],
        out_shape=[jax.ShapeDtypeStruct((T, D), F32), jax.ShapeDtypeStruct((T, D), BF16)],
        scratch_shapes=[pltpu.VMEM((tt + HALO, D), F32)],
        compiler_params=_cp("parallel"))(proj, proj, proj, proj, conv_k, conv_b, ln_g, ln_b)


def _merge(ret_in, u3, proj, wall):
    T = ret_in.shape[0]
    tm = min(T, 512)
    kr, kc = VW // NCHIP, D // NCHIP

    def body(ri_ref, u3_ref, gr_ref, gc_ref, wr_ref, wc_ref, ro_ref, co_ref, m_ref):
        ro = _dot_nt(ri_ref[:, 0:kr], wr_ref[0])
        co = _dot_nt(u3_ref[:, 0:kc], wc_ref[0])
        for s in range(1, NCHIP):
            ro += _dot_nt(ri_ref[:, s * kr:(s + 1) * kr], wr_ref[s])
            co += _dot_nt(u3_ref[:, s * kc:(s + 1) * kc], wc_ref[s])
        ro_ref[...] = ro.astype(BF16)
        co_ref[...] = co.astype(BF16)
        m = _sig(gr_ref[...].astype(F32)) * ro + _sig(gc_ref[...].astype(F32)) * co
        m_ref[...] = m.astype(BF16)

    row = pl.BlockSpec((tm, D), lambda i: (i, 0))
    return pl.pallas_call(
        body, name="merge", grid=(T // tm,),
        in_specs=[pl.BlockSpec((tm, VW), lambda i: (i, 0)), row,
                  pl.BlockSpec((tm, D), lambda i: (i, 8)), pl.BlockSpec((tm, D), lambda i: (i, 9)),
                  pl.BlockSpec((NCHIP, D, kr), lambda i: (0, 0, OFF["w_ret_o"] // kr)),
                  pl.BlockSpec((NCHIP, D, kc), lambda i: (0, 0, OFF["w_conv_o"] // kc))],
        out_specs=[row] * 3, out_shape=[jax.ShapeDtypeStruct((T, D), BF16)] * 3,
        compiler_params=_cp("parallel"))(ret_in, u3, proj, proj, wall, wall)


def _loss_ln_bwd(z, g, b, target, coef):
    T = z.shape[0]
    tm = min(T, 256)
    nt = T // tm

    def body(z_ref, g_ref, b_ref, t_ref, loss_ref, dzb_ref, dzt_ref, dz_ref, dg_ref, db_ref, lacc):
        i = pl.program_id(0)
        gam = g_ref[...]
        y, xh, rstd = _ln_fwd(z_ref[...], gam, b_ref[...])
        e = y - t_ref[...]
        part = _colsum(e * e)
        _acc_rows(lacc, i == 0, part)
        dy = e * (1.0 / D)
        dz = _ln_bwd(dy, xh, rstd, gam)
        dz_ref[...] = dz
        dzc = coef * dz
        dzb_ref[...] = dzc.astype(BF16)
        dzt_ref[...] = dzc.T.astype(BF16)
        _acc_rows(dg_ref, i == 0, _colsum(dy * xh))
        _acc_rows(db_ref, i == 0, _colsum(dy))

        @pl.when(i == nt - 1)
        def _():
            loss_ref[...] = (0.5 / D) * jnp.sum(lacc[...], axis=1, keepdims=True)

    row = pl.BlockSpec((tm, D), lambda i: (i, 0))
    vec = pl.BlockSpec((1, D), lambda i: (0, 0))
    return pl.pallas_call(
        body, name="loss_ln_bwd", grid=(nt,),
        in_specs=[row, vec, vec, row],
        out_specs=[pl.BlockSpec((1, 1), lambda i: (0, 0)), row, pl.BlockSpec((D, tm), lambda i: (0, i)),
                   row, vec, vec],
        out_shape=[jax.ShapeDtypeStruct((1, 1), F32), jax.ShapeDtypeStruct((T, D), BF16),
                   jax.ShapeDtypeStruct((D, T), BF16), jax.ShapeDtypeStruct((T, D), F32),
                   jax.ShapeDtypeStruct((1, D), F32), jax.ShapeDtypeStruct((1, D), F32)],
        scratch_shapes=[pltpu.VMEM((1, D), F32)],
        compiler_params=_cp("arbitrary"))(z, g, b, target)


def _ffn_bwd_h(dfb, wall, od, a, b, name):
    T = dfb.shape[0]
    tm, tn = min(T, 1024), 256
    nps = FSP // tn

    def body(d_ref, w_ref, a_ref, b_ref, da_ref, db_ref):
        dh = _dot(d_ref[...], w_ref[...])
        a = a_ref[...].astype(F32)
        sg = _sig(a)
        da_ref[...] = (dh * b_ref[...].astype(F32) * (sg * (1.0 + a * (1.0 - sg)))).astype(BF16)
        db_ref[...] = (dh * a * sg).astype(BF16)

    ospec = pl.BlockSpec((tm, tn), lambda i, j: (i, j))
    return pl.pallas_call(
        body, name=name, grid=(T // tm, FP // tn),
        in_specs=[pl.BlockSpec((tm, D), lambda i, j: (i, 0)),
                  pl.BlockSpec((None, D, tn), lambda i, j: (j // nps, 0, od // tn + j % nps)),
                  ospec, ospec],
        out_specs=[ospec] * 2, out_shape=[jax.ShapeDtypeStruct((T, FP), BF16)] * 2,
        compiler_params=_cp("parallel", "parallel"))(dfb, wall, a, b)


def _dx_bwd(lhs, offs, tk, wall, dz_next, name, ln=None, colsum=False):
    T, K = lhs[0].shape
    tm = min(T, 512)
    nk = K // tk
    kps = nk // NCHIP
    nl = len(lhs)

    def body(*refs):
        l_refs = refs[:nl]
        w_refs = refs[nl:2 * nl]
        dzn_ref = refs[2 * nl]
        pos = 2 * nl + 1
        if ln is not None:
            z_ref, g_ref = refs[pos:pos + 2]
            pos += 2
        outs = refs[pos:-1]
        acc = refs[-1]
        i, k = pl.program_id(0), pl.program_id(1)
        part = _dot_nt(l_refs[0][...], w_refs[0][...])
        for l in range(1, nl):
            part += _dot_nt(l_refs[l][...], w_refs[l][...])
        _acc_rows(acc, k == 0, part)
        if colsum:
            cs_ref = outs[-1]
            val = _colsum(l_refs[0][...].astype(F32))
            col = pl.ds(pl.multiple_of(k * tk, tk), tk)

            @pl.when(i == 0)
            def _():
                cs_ref[:, col] = val

            @pl.when(i > 0)
            def _():
                cs_ref[:, col] += val

        @pl.when(k == nk - 1)
        def _():
            dy = acc[...] + ALPHA * dzn_ref[...]
            if ln is None:
                outs[0][...] = dy
            else:
                gam = g_ref[...]
                _, xh, rstd = _ln_fwd(z_ref[...], gam, 0.0)
                dz = _ln_bwd(dy, xh, rstd, gam)
                dzc = ln[2] * dz
                outs[0][...] = dzc.astype(BF16)
                outs[1][...] = dzc.T.astype(BF16)
                outs[2][...] = dz
                _acc_rows(outs[3], i == 0, _colsum(dy * xh))
                _acc_rows(outs[4], i == 0, _colsum(dy))

    row = pl.BlockSpec((tm, D), lambda i, k: (i, 0))
    vec = pl.BlockSpec((1, D), lambda i, k: (0, 0))
    in_specs = [pl.BlockSpec((tm, tk), lambda i, k: (i, k))] * nl
    in_specs += [pl.BlockSpec((None, D, tk), functools.partial(lambda o, i, k: (k // kps, 0, o // tk + k % kps), o))
                 for o in offs]
    in_specs += [row]
    args = list(lhs) + [wall] * nl + [dz_next]
    if ln is None:
        out_specs = [row]
        out_shape = [jax.ShapeDtypeStruct((T, D), F32)]
    else:
        in_specs += [row, vec]
        args += [ln[0], ln[1]]
        out_specs = [row, pl.BlockSpec((D, tm), lambda i, k: (0, i)), row, vec, vec]
        out_shape = [jax.ShapeDtypeStruct((T, D), BF16), jax.ShapeDtypeStruct((D, T), BF16),
                     jax.ShapeDtypeStruct((T, D), F32), jax.ShapeDtypeStruct((1, D), F32),
                     jax.ShapeDtypeStruct((1, D), F32)]
    if colsum:
        out_specs += [pl.BlockSpec((1, K), lambda i, k: (0, 0))]
        out_shape += [jax.ShapeDtypeStruct((1, K), F32)]
    return pl.pallas_call(
        body, name=name, grid=(T // tm, nk), in_specs=in_specs, out_specs=out_specs, out_shape=out_shape,
        scratch_shapes=[pltpu.VMEM((tm, D), F32)],
        compiler_params=_cp("arbitrary", "arbitrary"))(*args)


def _wgrad(lhs_t, rhs, off, name, g_all=None):
    T, N = rhs.shape
    tn = 256
    nps = N // NCHIP // tn

    def body(*refs):
        l_ref, r_ref, o_ref = refs[0], refs[1], refs[-1]
        o_ref[...] = _dot(l_ref[...], r_ref[...])

    in_specs = [pl.BlockSpec((D, T), lambda j: (0, 0)), pl.BlockSpec((T, tn), lambda j: (0, j))]
    args = [lhs_t, rhs]
    aliases = {}
    if g_all is not None:
        in_specs.append(pl.BlockSpec(memory_space=pl.ANY))
        args.append(g_all)
        aliases = {2: 0}
    return pl.pallas_call(
        body, name=name, grid=(N // tn,), in_specs=in_specs,
        out_specs=pl.BlockSpec((None, D, tn), lambda j: (j // nps, 0, off // tn + j % nps)),
        out_shape=jax.ShapeDtypeStruct((NCHIP, D, WCOLS), F32),
        input_output_aliases=aliases,
        compiler_params=_cp("parallel"))(*args)


def _merge_bwd(dmb, wall, proj, ro, co):
    T = dmb.shape[0]
    tm = min(T, 512)
    ks = D // NCHIP

    def body(d_ref, w_ref, gr_ref, gc_ref, ro_ref, co_ref, dro_ref, drot_ref, dco_ref, dcot_ref, dp_ref):
        d = d_ref[...]
        dmg = jnp.concatenate([_dot(d, w_ref[s]) for s in range(NCHIP)], axis=1)
        sr = _sig(gr_ref[...].astype(F32))
        sc = _sig(gc_ref[...].astype(F32))
        dro = dmg * sr
        dco = dmg * sc
        dro_ref[...] = dro.astype(BF16)
        drot_ref[...] = dro.T.astype(BF16)
        dco_ref[...] = dco.astype(BF16)
        dcot_ref[...] = dco.T.astype(BF16)
        dp_ref[:, 0:D] = (dmg * ro_ref[...].astype(F32) * sr * (1.0 - sr)).astype(BF16)
        dp_ref[:, D:2 * D] = (dmg * co_ref[...].astype(F32) * sc * (1.0 - sc)).astype(BF16)

    row = pl.BlockSpec((tm, D), lambda i: (i, 0))
    col = pl.BlockSpec((D, tm), lambda i: (0, i))
    return pl.pallas_call(
        body, name="merge_bwd", grid=(T // tm,),
        in_specs=[row, pl.BlockSpec((NCHIP, D, ks), lambda i: (0, 0, OFF["w_out"] // ks)),
                  pl.BlockSpec((tm, D), lambda i: (i, 8)), pl.BlockSpec((tm, D), lambda i: (i, 9)), row, row],
        out_specs=[row, col, row, col, pl.BlockSpec((tm, 2 * D), lambda i: (i, 4))],
        out_shape=[jax.ShapeDtypeStruct((T, D), BF16), jax.ShapeDtypeStruct((D, T), BF16),
                   jax.ShapeDtypeStruct((T, D), BF16), jax.ShapeDtypeStruct((D, T), BF16),
                   jax.ShapeDtypeStruct((T, INW), BF16)],
        compiler_params=_cp("parallel"))(dmb, wall, proj, proj, ro, co)


def _reto_bwd(dro, wall, r, proj, gn_g, dproj):
    T = dro.shape[0]
    tm = min(T, 512)
    hps = H // NCHIP

    def body(d_ref, w_ref, r_ref, g_ref, gn_ref, _, dr_ref, dgn_ref, dp_ref):
        i = pl.program_id(1)
        dri = _dot(d_ref[...], w_ref[...])
        rr = r_ref[...]
        mu = jnp.mean(rr, axis=-1, keepdims=True)
        xc = rr - mu
        var = jnp.mean(xc * xc, axis=-1, keepdims=True)
        rstd = lax.rsqrt(var + EPS)
        rn = xc * rstd
        gn = gn_ref[...]
        g = g_ref[...].astype(F32)
        sg = _sig(g)
        dy = dri * (g * sg)
        dp_ref[...] = (dri * (rn * gn) * (sg * (1.0 + g * (1.0 - sg)))).astype(BF16)
        _acc_rows(dgn_ref, i == 0, _colsum(dy * rn))
        dr_ref[...] = _ln_bwd(dy, rn, rstd, gn).astype(BF16)

    return pl.pallas_call(
        body, name="reto_bwd", grid=(H, T // tm),
        in_specs=[pl.BlockSpec((tm, D), lambda j, i: (i, 0)),
                  pl.BlockSpec((None, D, DV), lambda j, i: (j // hps, 0, OFF["w_ret_o"] // DV + j % hps)),
                  pl.BlockSpec((tm, DV), lambda j, i: (i, j)),
                  pl.BlockSpec((tm, DV), lambda j, i: (i, 2 * VW // DV + j)),
                  pl.BlockSpec((1, DV), lambda j, i: (0, j)),
                  pl.BlockSpec(memory_space=pl.ANY)],
        out_specs=[pl.BlockSpec((tm, DV), lambda j, i: (i, j)), pl.BlockSpec((1, DV), lambda j, i: (0, j)),
                   pl.BlockSpec((tm, DV), lambda j, i: (i, 2 * VW // DV + j))],
        out_shape=[jax.ShapeDtypeStruct((T, VW), BF16), jax.ShapeDtypeStruct((1, VW), F32),
                   jax.ShapeDtypeStruct((T, INW), BF16)],
        input_output_aliases={5: 2},
        compiler_params=_cp("arbitrary", "arbitrary"))(dro, wall, r, proj, gn_g, dproj)


def _retention_bwd(proj, dr, states, cos_t, sin_t, dm_t, xi_t, zeta_t, cds, dproj):
    T = proj.shape[0]
    n = T // CH
    scale = DK ** -0.5

    def body(q_ref, k_ref, v_ref, dr_ref, st_ref, cos_ref, sin_ref, dm_ref, xi_ref, zt_ref, _, dp_ref, ds):
        @pl.when(pl.program_id(0) == 0)
        def _():
            ds[...] = jnp.zeros_like(ds)

        cos = cos_ref[...]
        sin = sin_ref[...]

        def unrope(d):
            return d * cos + pltpu.roll(d * sin, DK // 2, 1)

        for h in range(H):
            q = q_ref[:, h * DK:(h + 1) * DK]
            k = k_ref[:, h * DK:(h + 1) * DK]
            v = v_ref[:, h * DV:(h + 1) * DV]
            d_r = dr_ref[:, h * DV:(h + 1) * DV]
            rows = slice(h * DK, (h + 1) * DK)
            s_b = st_ref[rows, :]
            dm = dm_ref[h]
            zt = zt_ref[h]
            sc = _dot_nt(q, k) * dm
            dsc = _dot_nt(d_r, v) * dm
            drx = (d_r.astype(F32) * xi_ref[h]).astype(BF16)
            ds_prev = ds[rows, :]
            ds_b = ds_prev.astype(BF16)
            kz = (k.astype(F32) * zt).astype(BF16)
            dq = _dot(dsc.astype(BF16), k) + _dot_nt(drx, s_b)
            dk = _dot(dsc.T.astype(BF16), q) + _dot_nt(v, ds_b) * zt
            dv = _dot(sc.T.astype(BF16), d_r) + _dot(kz, ds_b)
            ds[rows, :] = cds[h] * ds_prev + _dot(q.astype(F32).T.astype(BF16), drx)
            dp_ref[:, h * DK:(h + 1) * DK] = unrope(dq * scale).astype(BF16)
            dp_ref[:, D + h * DK:D + (h + 1) * DK] = unrope(dk).astype(BF16)
            dp_ref[:, 2 * D + h * DV:2 * D + (h + 1) * DV] = dv.astype(BF16)

    rv = lambda c: n - 1 - c
    full3 = lambda shp: pl.BlockSpec(shp, lambda c: (0, 0, 0))
    return pl.pallas_call(
        body, name="retention_bwd", grid=(n,),
        in_specs=[pl.BlockSpec((CH, D), lambda c: (rv(c), 0)),
                  pl.BlockSpec((CH, D), lambda c: (rv(c), 1)),
                  pl.BlockSpec((CH, VW), lambda c: (rv(c), 1)),
                  pl.BlockSpec((CH, VW), lambda c: (rv(c), 0)),
                  pl.BlockSpec((None, H * DK, DV), lambda c: (rv(c), 0, 0)),
                  pl.BlockSpec((CH, DK), lambda c: (rv(c), 0)),
                  pl.BlockSpec((CH, DK), lambda c: (rv(c), 0)),
                  full3((H, CH, CH)), full3((H, CH, DV)), full3((H, CH, DK)),
                  pl.BlockSpec(memory_space=pl.ANY)],
        out_specs=pl.BlockSpec((CH, 2 * D + VW), lambda c: (rv(c), 0)),
        out_shape=jax.ShapeDtypeStruct((T, INW), BF16),
        input_output_aliases={10: 0},
        scratch_shapes=[pltpu.VMEM((H * DK, DV), F32)],
        compiler_params=_cp("arbitrary"))(proj, proj, proj, dr, states, cos_t, sin_t, dm_t, xi_t, zeta_t, dproj)


def _convo_bwd(dco, wall, u1, ln_g, ln_b):
    T = dco.shape[0]
    tm = min(T, 512)
    ks = D // NCHIP

    def body(d_ref, w_ref, u1_ref, g_ref, b_ref, du1_ref, dg_ref, db_ref, dcb_ref):
        i = pl.program_id(0)
        d = d_ref[...]
        du3 = jnp.concatenate([_dot(d, w_ref[s]) for s in range(NCHIP)], axis=1)
        gam = g_ref[...]
        u2, xh, rstd = _ln_fwd(u1_ref[...], gam, b_ref[...])
        sg = _sig(u2)
        du2 = du3 * (sg * (1.0 + u2 * (1.0 - sg)))
        du1 = _ln_bwd(du2, xh, rstd, gam)
        du1_ref[...] = du1
        _acc_rows(dg_ref, i == 0, _colsum(du2 * xh))
        _acc_rows(db_ref, i == 0, _colsum(du2))
        _acc_rows(dcb_ref, i == 0, _colsum(du1))

    row = pl.BlockSpec((tm, D), lambda i: (i, 0))
    vec = pl.BlockSpec((1, D), lambda i: (0, 0))
    return pl.pallas_call(
        body, name="convo_bwd", grid=(T // tm,),
        in_specs=[row, pl.BlockSpec((NCHIP, D, ks), lambda i: (0, 0, OFF["w_conv_o"] // ks)), row, vec, vec],
        out_specs=[row, vec, vec, vec],
        out_shape=[jax.ShapeDtypeStruct((T, D), F32)] + [jax.ShapeDtypeStruct((1, D), F32)] * 3,
        compiler_params=_cp("arbitrary"))(dco, wall, u1, ln_g, ln_b)


def _conv_bwd(du1, proj, conv_k, dproj):
    T = du1.shape[0]
    tt = min(T, CONV_TT)
    nt = T // tt
    ca, cb = 6, 7

    def body(d_ref, dn_ref, a_ref, b_ref, pa_ref, pb_ref, k_ref, _, dp_ref, dk_ref, win_u, win_d):
        i = pl.program_id(0)
        a, sb = _glu(a_ref, b_ref)
        win_u[HALO:, :] = a * sb
        pa, psb = _glu(pa_ref, pb_ref, slice(tt - HALO, tt))
        win_u[0:HALO, :] = jnp.where(i > 0, pa * psb, 0.0)
        win_d[0:tt, :] = d_ref[...]
        win_d[tt:, :] = jnp.where(i < nt - 1, dn_ref[0:HALO, :], 0.0)

        @pl.when(i == 0)
        def _():
            dk_ref[...] = jnp.zeros_like(dk_ref)

        for c0 in range(0, D, CONV_CB):
            cs = slice(c0, c0 + CONV_CB)
            for r0 in range(0, tt, CONV_SB):
                acc = jnp.zeros((CONV_SB, CONV_CB), F32)
                for w in range(CONV_W):
                    st = r0 + (CONV_W - 1) - w
                    acc += win_d[st:st + CONV_SB, cs] * k_ref[w:w + 1, cs]
                aa = a_ref[r0:r0 + CONV_SB, cs].astype(F32)
                ss = _sig(b_ref[r0:r0 + CONV_SB, cs].astype(F32))
                dp_ref[r0:r0 + CONV_SB, cs] = (acc * ss).astype(BF16)
                dp_ref[r0:r0 + CONV_SB, c0 + D:c0 + D + CONV_CB] = (acc * aa * ss * (1.0 - ss)).astype(BF16)
            for w in range(CONV_W):
                acc = jnp.zeros((CONV_SB, CONV_CB), F32)
                for r0 in range(0, tt, CONV_SB):
                    st = r0 + HALO - (CONV_W - 1) + w
                    acc += win_d[r0:r0 + CONV_SB, cs] * win_u[st:st + CONV_SB, cs]
                dk_ref[w:w + 1, cs] += _colsum(acc)

    blk = lambda f, c: pl.BlockSpec((tt, D), lambda i: (f(i), c))
    cur = lambda i: i
    prv = lambda i: jnp.maximum(i - 1, 0)
    nxt = lambda i: jnp.minimum(i + 1, nt - 1)
    return pl.pallas_call(
        body, name="conv_bwd", grid=(nt,),
        in_specs=[blk(cur, 0), blk(nxt, 0), blk(cur, ca), blk(cur, cb), blk(prv, ca), blk(prv, cb),
                  pl.BlockSpec((CONV_W, D), lambda i: (0, 0)), pl.BlockSpec(memory_space=pl.ANY)],
        out_specs=[pl.BlockSpec((tt, 2 * D), lambda i: (i, 3)), pl.BlockSpec((HALO, D), lambda i: (0, 0))],
        out_shape=[jax.ShapeDtypeStruct((T, INW), BF16), jax.ShapeDtypeStruct((HALO, D), F32)],
        input_output_aliases={7: 0},
        scratch_shapes=[pltpu.VMEM((tt + HALO, D), F32), pltpu.VMEM((tt + HALO, D), F32)],
        compiler_params=_cp("arbitrary"))(du1, du1, proj, proj, proj, proj, conv_k, dproj)


def _local_step(x, target, wall, sp):
    T = x.shape[0]
    cos_t, sin_t = _rope_tables(T)
    dm_t, xi_t, zeta_t, cds = _decay_tables()

    xb, xt = _cast_t(x)
    a1, b1, h1 = _ffn_up(xb, wall, OFF["g1"], OFF["u1"], "ffn1_up")
    z1, x1, x1b, x1t = _proj_ln(h1, wall, OFF["d1"], x, sp["ln1_g"], sp["ln1_b"], 0.5, "ffn1_down_ln")
    proj = _inproj(x1b, wall, sp["b_in"], cos_t, sin_t)
    r, ret_in, states = _retention_fwd(proj, sp["ret_gn_g"], dm_t, xi_t, zeta_t, cds)
    u1, u3 = _conv_fwd(proj, sp["conv_k"], sp["conv_b"], sp["conv_ln_g"], sp["conv_ln_b"])
    ro, co, merged = _merge(ret_in, u3, proj, wall)
    z2, x2, x2b, x2t = _proj_ln(merged, wall, OFF["w_out"], x1, sp["ln2_g"], sp["ln2_b"], 1.0, "out_proj_ln")
    a2, b2, h2 = _ffn_up(x2b, wall, OFF["g2"], OFF["u2"], "ffn2_up")
    (z3,) = _proj_ln(h2, wall, OFF["d2"], x2, sp["ln3_g"], sp["ln3_b"], 0.5, "ffn2_down", want_b=False)

    sg = {}
    loss, df2b, df2t, dz3, sg["ln3_g"], sg["ln3_b"] = _loss_ln_bwd(z3, sp["ln3_g"], sp["ln3_b"], target, 0.5)
    da2, db2 = _ffn_bwd_h(df2b, wall, OFF["d2"], a2, b2, "ffn2_bwd_h")
    g_all = _wgrad(df2t, h2, OFF["d2"], "wgrad_d2")
    g_all = _wgrad(x2t, da2, OFF["g2"], "wgrad_g2", g_all)
    g_all = _wgrad(x2t, db2, OFF["u2"], "wgrad_u2", g_all)
    dmb, dmt, dz2, sg["ln2_g"], sg["ln2_b"] = _dx_bwd(
        [da2, db2], [OFF["g2"], OFF["u2"]], FSP, wall, dz3, "ffn2_dx_ln", ln=(z2, sp["ln2_g"], 1.0))
    g_all = _wgrad(dmt, merged, OFF["w_out"], "wgrad_out", g_all)
    dro, drot, dco, dcot, dproj = _merge_bwd(dmb, wall, proj, ro, co)
    g_all = _wgrad(drot, ret_in, OFF["w_ret_o"], "wgrad_ret_o", g_all)
    g_all = _wgrad(dcot, u3, OFF["w_conv_o"], "wgrad_conv_o", g_all)
    dr, sg["ret_gn_g"], dproj = _reto_bwd(dro, wall, r, proj, sp["ret_gn_g"], dproj)
    dproj = _retention_bwd(proj, dr, states, cos_t, sin_t, dm_t, xi_t, zeta_t, cds, dproj)
    du1, sg["conv_ln_g"], sg["conv_ln_b"], sg["conv_b"] = _convo_bwd(dco, wall, u1, sp["conv_ln_g"], sp["conv_ln_b"])
    dproj, dck = _conv_bwd(du1, proj, sp["conv_k"], dproj)
    sg["conv_k"] = dck[:CONV_W]
    g_all = _wgrad(x1t, dproj, OFF["w_in"], "wgrad_in", g_all)
    df1b, df1t, dz1, sg["ln1_g"], sg["ln1_b"], sg["b_in"] = _dx_bwd(
        [dproj], [OFF["w_in"]], 512, wall, dz2, "mixer_dx_ln", ln=(z1, sp["ln1_g"], 0.5), colsum=True)
    da1, db1 = _ffn_bwd_h(df1b, wall, OFF["d1"], a1, b1, "ffn1_bwd_h")
    g_all = _wgrad(df1t, h1, OFF["d1"], "wgrad_d1", g_all)
    g_all = _wgrad(xt, da1, OFF["g1"], "wgrad_g1", g_all)
    g_all = _wgrad(xt, db1, OFF["u1"], "wgrad_u1", g_all)
    (grad_x,) = _dx_bwd([da1, db1], [OFF["g1"], OFF["u1"]], FSP, wall, dz1, "ffn1_dx")
    return loss, grad_x, g_all, sg


MESH = pl.DeviceIdType.MESH
ANY = pl.BlockSpec(memory_space=pl.ANY)
HALF = D // 2


def _place():
    x, y, c = lax.axis_index("x"), lax.axis_index("y"), lax.axis_index("c")
    chips = [(1 - x, y), (x, 1 - y), (1 - x, 1 - y)]
    return x, y, c, chips


def _gather_weights(wloc):
    def body(w_ref, o_ref, lsem, s1, r1, s2, r2):
        x, y, c, chips = _place()
        me = 2 * x + y
        sib = (x, y, 1 - c)
        mine = pltpu.make_async_copy(w_ref, o_ref.at[me], lsem)
        mine.start()

        def rc(src, dst, ss, rs, dev):
            return pltpu.make_async_remote_copy(src_ref=src, dst_ref=dst, send_sem=ss, recv_sem=rs,
                                                device_id=dev, device_id_type=MESH)

        first = [rc(w_ref.at[c], o_ref.at[me, c], s1.at[j], r1.at[j], (*chip, c)) for j, chip in enumerate(chips)]
        for cp in first:
            cp.start()
        passed = []
        for j, (px, py) in enumerate(chips):
            slot = o_ref.at[2 * px + py, c]
            rc(slot, slot, s1.at[j], r1.at[j], (px, py, c)).wait_recv()
            cp = rc(slot, slot, s2.at[j], r2.at[j], sib)
            cp.start()
            passed.append(cp)
        for j, (px, py) in enumerate(chips):
            slot = o_ref.at[2 * px + py, 1 - c]
            rc(slot, slot, s2.at[j], r2.at[j], sib).wait_recv()
        for cp in first + passed:
            cp.wait_send()
        mine.wait()

    dma = pltpu.SemaphoreType.DMA
    return pl.pallas_call(
        body, name="gather_weights", in_specs=[ANY], out_specs=ANY,
        out_shape=jax.ShapeDtypeStruct((NCHIP, 2, HALF, WCOLS), BF16),
        scratch_shapes=[dma(()), dma((3,)), dma((3,)), dma((3,)), dma((3,))],
    )(wloc)


def _pair_exchange(g5):
    def body(g_ref, o_ref, ss, rs):
        x, y, c, _ = _place()
        sib = (x, y, 1 - c)
        cps = [pltpu.make_async_remote_copy(src_ref=g_ref.at[j, 1 - c], dst_ref=o_ref.at[j], send_sem=ss.at[j],
                                            recv_sem=rs.at[j], device_id=sib, device_id_type=MESH)
               for j in range(NCHIP)]
        for cp in cps:
            cp.start()
        for cp in cps:
            cp.wait()

    dma = pltpu.SemaphoreType.DMA
    return pl.pallas_call(
        body, name="pair_exchange", in_specs=[ANY], out_specs=ANY,
        out_shape=jax.ShapeDtypeStruct((NCHIP, HALF, WCOLS), F32),
        scratch_shapes=[dma((NCHIP,)), dma((NCHIP,))])(g5)


RS_TR = 128
RS_TC = 2048


def _pair_sum(pos, g5, got):
    def body(pos_ref, g_ref, r_ref, o_ref):
        o_ref[...] = (g_ref[...] + r_ref[...]).astype(BF16)

    grid_spec = pltpu.PrefetchScalarGridSpec(
        num_scalar_prefetch=1, grid=(NCHIP, HALF // RS_TR, WCOLS // RS_TC),
        in_specs=[pl.BlockSpec((None, None, RS_TR, RS_TC), lambda j, i, k, p: (j, p[0], i, k)),
                  pl.BlockSpec((None, RS_TR, RS_TC), lambda j, i, k, p: (j, i, k))],
        out_specs=pl.BlockSpec((None, RS_TR, RS_TC), lambda j, i, k, p: (j, i, k)))
    return pl.pallas_call(
        body, name="pair_sum", grid_spec=grid_spec,
        out_shape=jax.ShapeDtypeStruct((NCHIP, HALF, WCOLS), BF16),
        compiler_params=_cp("parallel", "parallel", "parallel"))(pos, g5, got)


def _chip_exchange(pb):
    def body(p_ref, o_ref, ss, rs):
        x, y, c, chips = _place()
        cps = [pltpu.make_async_remote_copy(src_ref=p_ref.at[2 * px + py], dst_ref=o_ref.at[j], send_sem=ss.at[j],
                                            recv_sem=rs.at[j], device_id=(px, py, c), device_id_type=MESH)
               for j, (px, py) in enumerate(chips)]
        for cp in cps:
            cp.start()
        for cp in cps:
            cp.wait()

    dma = pltpu.SemaphoreType.DMA
    return pl.pallas_call(
        body, name="chip_exchange", in_specs=[ANY], out_specs=ANY,
        out_shape=jax.ShapeDtypeStruct((3, HALF, WCOLS), BF16),
        scratch_shapes=[dma((3,)), dma((3,))])(pb)


def _chip_sum(pos, g5, got, peers):
    def body(pos_ref, g_ref, r_ref, p_ref, o_ref):
        acc = g_ref[...] + r_ref[...]
        for j in range(3):
            acc += p_ref[j].astype(F32)
        o_ref[...] = acc

    grid_spec = pltpu.PrefetchScalarGridSpec(
        num_scalar_prefetch=1, grid=(HALF // RS_TR, WCOLS // RS_TC),
        in_specs=[pl.BlockSpec((None, None, RS_TR, RS_TC), lambda i, k, p: (p[0], p[1], i, k)),
                  pl.BlockSpec((None, RS_TR, RS_TC), lambda i, k, p: (p[0], i, k)),
                  pl.BlockSpec((3, RS_TR, RS_TC), lambda i, k, p: (0, i, k))],
        out_specs=pl.BlockSpec((None, RS_TR, RS_TC), lambda i, k, p: (p[1], i, k)))
    return pl.pallas_call(
        body, name="chip_sum", grid_spec=grid_spec,
        out_shape=jax.ShapeDtypeStruct((2, HALF, WCOLS), F32),
        compiler_params=_cp("parallel", "parallel"))(pos, g5, got, peers)


def _pair_share(gsum):
    def body(g_ref, o_ref, ss, rs):
        x, y, c, _ = _place()
        cp = pltpu.make_async_remote_copy(src_ref=o_ref.at[c], dst_ref=o_ref.at[c], send_sem=ss, recv_sem=rs,
                                          device_id=(x, y, 1 - c), device_id_type=MESH)
        cp.start()
        cp.wait_send()
        pltpu.make_async_remote_copy(src_ref=o_ref.at[1 - c], dst_ref=o_ref.at[1 - c], send_sem=ss, recv_sem=rs,
                                     device_id=(x, y, 1 - c), device_id_type=MESH).wait_recv()

    dma = pltpu.SemaphoreType.DMA
    return pl.pallas_call(
        body, name="pair_share", in_specs=[ANY], out_specs=ANY,
        out_shape=jax.ShapeDtypeStruct((2, HALF, WCOLS), F32),
        input_output_aliases={0: 0},
        scratch_shapes=[dma(()), dma(())])(gsum)


SMALL_ROWS = 56


def _allreduce_small(vec, name):
    def body(v_ref, o_ref, buf, ss, rs):
        x, y, c, _ = _place()
        me = 4 * x + 2 * y + c
        buf[me] = v_ref[...]
        cps = []
        for m in range(1, 8):
            dev = (x ^ (m >> 2), y ^ ((m >> 1) & 1), c ^ (m & 1))
            cp = pltpu.make_async_remote_copy(src_ref=v_ref, dst_ref=buf.at[me], send_sem=ss.at[m - 1],
                                              recv_sem=rs.at[m - 1], device_id=dev, device_id_type=MESH)
            cp.start()
            cps.append(cp)
        for cp in cps:
            cp.wait()
        acc = buf[0]
        for d in range(1, 8):
            acc += buf[d]
        o_ref[...] = acc

    dma = pltpu.SemaphoreType.DMA
    vm = pl.BlockSpec(memory_space=pltpu.VMEM)
    return pl.pallas_call(
        body, name=name, in_specs=[vm], out_specs=vm,
        out_shape=jax.ShapeDtypeStruct((SMALL_ROWS, D), F32),
        scratch_shapes=[pltpu.VMEM((8, SMALL_ROWS, D), F32), dma((7,)), dma((7,))])(vec)


def _adamw_math(w, g, m, v):
    m2 = ADAM_B1 * m + (1.0 - ADAM_B1) * g
    v2 = ADAM_B2 * v + (1.0 - ADAM_B2) * (g * g)
    m_hat = m2 / (1.0 - ADAM_B1 ** ADAM_STEP)
    v_hat = v2 / (1.0 - ADAM_B2 ** ADAM_STEP)
    delta = -ADAM_LR * (m_hat / (jnp.sqrt(v_hat) + ADAM_EPS) + ADAM_WD * w)
    return delta, m2, v2


def _adamw(w, g, m, v, name, g_block=None):
    R, C = w.shape
    tr = R
    gw_hint = C if g_block is None else g_block[0]
    for cand in (256, 176, 128, 64, 32, 16, 8):
        if R % cand == 0 and cand * max(C, gw_hint) * 4 <= (1 << 20):
            tr = cand
            break
    gw, gi = (C, 0) if g_block is None else g_block

    def body(w_ref, g_ref, m_ref, v_ref, go_ref, d_ref, mo_ref, vo_ref):
        g = g_ref[:, 0:C]
        d, m2, v2 = _adamw_math(w_ref[...], g, m_ref[...], v_ref[...])
        go_ref[...] = g
        d_ref[...] = d
        mo_ref[...] = m2
        vo_ref[...] = v2

    spec = pl.BlockSpec((tr, C), lambda i: (i, 0))
    return pl.pallas_call(
        body, name=name, grid=(R // tr,),
        in_specs=[spec, pl.BlockSpec((tr, gw), lambda i: (i, gi)), spec, spec],
        out_specs=[spec] * 4, out_shape=[jax.ShapeDtypeStruct((R, C), F32)] * 4,
        compiler_params=_cp("parallel"))(w, g, m, v)


BIG = ["ffn1_w_gate", "ffn1_w_up", "ffn1_w_down", "w_in", "w_ret_o", "w_conv_o", "w_out",
       "ffn2_w_gate", "ffn2_w_up", "ffn2_w_down"]
SLAB = {"ffn1_w_gate": "g1", "ffn1_w_up": "u1", "ffn1_w_down": "d1", "w_in": "w_in", "w_ret_o": "w_ret_o",
        "w_conv_o": "w_conv_o", "w_out": "w_out", "ffn2_w_gate": "g2", "ffn2_w_up": "u2", "ffn2_w_down": "d2"}
TRANSPOSED = {"ffn1_w_down", "ffn2_w_down", "w_ret_o", "w_conv_o", "w_out"}
SMALL = ["ln1_g", "ln1_b", "ln2_g", "ln2_b", "ln3_g", "ln3_b", "conv_ln_g", "conv_ln_b", "conv_b",
         "ret_gn_g", "b_in"]
ORDER = ["ffn1_w_gate", "ffn1_w_up", "ffn1_w_down", "ln1_g", "ln1_b", "w_in", "b_in", "ret_gn_g", "conv_k",
         "conv_b", "conv_ln_g", "conv_ln_b", "w_ret_o", "w_conv_o", "w_out", "ln2_g", "ln2_b",
         "ffn2_w_gate", "ffn2_w_up", "ffn2_w_down", "ln3_g", "ln3_b"]


def _slab_width(name):
    return {"w_in": INS, "w_ret_o": VW // NCHIP, "w_conv_o": D // NCHIP, "w_out": D // NCHIP}.get(name, FSP)


def _pack_slab(weights):
    parts = {}
    for name in BIG:
        w = weights[name]
        w = w.T if name in TRANSPOSED else w
        parts[OFF[SLAB[name]]] = jnp.pad(w, ((0, 0), (0, _slab_width(name) - w.shape[1])))
    return jnp.concatenate([parts[o] for o in sorted(parts)], axis=1).astype(BF16)


def _pack_small(vals, rows):
    flat = jnp.concatenate([vals[n].reshape(-1) for n in SMALL] + [vals["conv_k"].reshape(-1)])
    return jnp.pad(flat, (0, rows * D - flat.shape[0])).reshape(rows, D)


def _unpack_small(arr, shapes):
    flat = arr.reshape(-1)
    out, pos = {}, 0
    for n in SMALL + ["conv_k"]:
        size = int(np.prod(shapes[n]))
        out[n] = flat[pos:pos + size].reshape(shapes[n])
        pos += size
    return out


def kernel(x, ffn1_w_gate, ffn1_w_up, ffn1_w_down, ln1_g, ln1_b, w_in, b_in, ret_gn_g, conv_k, conv_b, conv_ln_g, conv_ln_b, w_ret_o, w_conv_o, w_out, ln2_g, ln2_b, ffn2_w_gate, ffn2_w_up, ffn2_w_down, ln3_g, ln3_b, loss_target, m_ffn1_w_gate, m_ffn1_w_up, m_ffn1_w_down, m_ln1_g, m_ln1_b, m_w_in, m_b_in, m_ret_gn_g, m_conv_k, m_conv_b, m_conv_ln_g, m_conv_ln_b, m_w_ret_o, m_w_conv_o, m_w_out, m_ln2_g, m_ln2_b, m_ffn2_w_gate, m_ffn2_w_up, m_ffn2_w_down, m_ln3_g, m_ln3_b, v_ffn1_w_gate, v_ffn1_w_up, v_ffn1_w_down, v_ln1_g, v_ln1_b, v_w_in, v_b_in, v_ret_gn_g, v_conv_k, v_conv_b, v_conv_ln_g, v_conv_ln_b, v_w_ret_o, v_w_conv_o, v_w_out, v_ln2_g, v_ln2_b, v_ffn2_w_gate, v_ffn2_w_up, v_ffn2_w_down, v_ln3_g, v_ln3_b):
    args = dict(locals())
    w = {n: args[n] for n in ORDER}
    m = {n: args["m_" + n] for n in ORDER}
    v = {n: args["v_" + n] for n in ORDER}
    xi, yi, ci = lax.axis_index("x"), lax.axis_index("y"), lax.axis_index("c")
    chip = 2 * xi + yi

    wloc = _pack_slab({n: w[n][0] for n in BIG})
    wall = _gather_weights(wloc.reshape(2, HALF, WCOLS)).reshape(NCHIP, D, WCOLS)

    sp = {n: w[n] for n in SMALL}
    sp["conv_k"] = None
    kfull_shape = (CONV_W, D)
    kpad = jnp.zeros(kfull_shape, F32)
    kpad = lax.dynamic_update_slice(kpad, w["conv_k"][0, :, 0, :] * jnp.where(ci == 0, 1.0, 0.0), (0, chip * (D // NCHIP)))
    kvec = jnp.pad(kpad.reshape(-1), (0, SMALL_ROWS * D - CONV_W * D)).reshape(SMALL_ROWS, D)
    sp["conv_k"] = _allreduce_small(kvec, "gather_conv_k").reshape(-1)[:CONV_W * D].reshape(kfull_shape)
    loss, grad_x, g_all, sg = _local_step(x[0], loss_target[0], wall, sp)

    g5 = g_all.reshape(NCHIP, 2, HALF, WCOLS)
    pos_c = jnp.reshape(ci, (1,)).astype(jnp.int32)
    pos_sc = jnp.stack([chip, ci]).astype(jnp.int32)
    got = _pair_exchange(g5)
    pb = _pair_sum(pos_c, g5, got)
    peers = _chip_exchange(pb)
    gsum = _chip_sum(pos_sc, g5, got, peers)
    gfull = _pair_share(gsum).reshape(D, WCOLS)

    shapes = {n: sg[n].shape for n in SMALL + ["conv_k"]}
    small = _unpack_small(_allreduce_small(_pack_small(sg, SMALL_ROWS), "allreduce_small"), shapes)

    out = {}
    for n in BIG:
        off, width = OFF[SLAB[n]], _slab_width(n)
        w2 = w[n][0]
        if n in TRANSPOSED:
            g2 = gfull[:, off:off + w2.shape[0]].T
            res = _adamw(w2, g2, m[n][0], v[n][0], "adamw_" + n)
        else:
            res = _adamw(w2, gfull, m[n][0], v[n][0], "adamw_" + n, g_block=(width, off // width))
        out[n] = [r[None] for r in res]
    for n in SMALL:
        res = _adamw(w[n], small[n], m[n], v[n], "adamw_" + n)
        out[n] = list(res)
    gk = lax.dynamic_slice(small["conv_k"], (0, chip * (D // NCHIP)), (CONV_W, D // NCHIP))
    res = _adamw(w["conv_k"][0, :, 0, :], gk, m["conv_k"][0, :, 0, :], v["conv_k"][0, :, 0, :], "adamw_conv_k")
    out["conv_k"] = [r[None, :, None, :] for r in res]

    total = lax.psum(loss[0, 0], ("x", "y", "c"))
    grads = [out[n][0] for n in ORDER]
    deltas = [out[n][1] for n in ORDER]
    new_m = [out[n][2] for n in ORDER]
    new_v = [out[n][3] for n in ORDER]
    return (total, grad_x[None], *grads, *deltas, *new_m, *new_v)
```

```python
import functools
import math

import numpy as np
import jax
import jax.numpy as jnp
from jax import lax
from jax.experimental import pallas as pl
from jax.experimental.pallas import tpu as pltpu
from jax.experimental.pallas import tpu_sc as plsc

F32 = jnp.float32
BF16 = jnp.bfloat16

D = 1024
FS = 704
FSP = 768
FP = 4 * FSP
H = 8
DK = 128
DV = 256
CH = 128
VW = H * DV
INW = 10240
INS = INW // 4
CONV_W = 31
HALO = 32
EPS = 1e-5
ALPHA = 2.0 ** 0.25
ROPE_BASE = 10000.0
NCHIP = 4

ADAM_LR, ADAM_B1, ADAM_B2, ADAM_EPS, ADAM_WD, ADAM_STEP = 0.001, 0.9, 0.999, 1e-08, 0.01, 10

OFF = {"w_in": 0, "w_ret_o": 2560, "g1": 3072, "u1": 3840, "d1": 4608,
       "g2": 5376, "u2": 6144, "d2": 6912, "w_conv_o": 7680, "w_out": 7936}
WCOLS = 8192
WIDTH = {"w_in": INS, "w_ret_o": VW // NCHIP, "w_conv_o": D // NCHIP, "w_out": D // NCHIP,
         "g1": FSP, "u1": FSP, "d1": FSP, "g2": FSP, "u2": FSP, "d2": FSP}
GROUPS = (("g1", "u1"), ("d1",), ("w_in",), ("w_ret_o", "w_conv_o", "w_out"), ("g2", "u2", "d2"))
LOC = {}
for _gi, _keys in enumerate(GROUPS):
    _off = 0
    for _k in _keys:
        LOC[_k] = (_gi, _off)
        _off += WIDTH[_k]
VMEM_LIMIT = 56 << 20


def _cp(*sem, **kw):
    return pltpu.CompilerParams(dimension_semantics=sem, vmem_limit_bytes=VMEM_LIMIT, **kw)


def _sig(x):
    return 1.0 / (1.0 + jnp.exp(-x))


def _dot(a, b):
    return jnp.dot(a, b, preferred_element_type=F32)


def _dot_nt(a, b):
    return lax.dot_general(a, b, (((1,), (1,)), ((), ())), preferred_element_type=F32)


def _ln_fwd(z, g, b):
    mu = jnp.mean(z, axis=-1, keepdims=True)
    xc = z - mu
    var = jnp.mean(xc * xc, axis=-1, keepdims=True)
    rstd = lax.rsqrt(var + EPS)
    xh = xc * rstd
    return xh * g + b, xh, rstd


def _ln_bwd(dy, xh, rstd, g):
    dxh = dy * g
    m1 = jnp.mean(dxh, axis=-1, keepdims=True)
    m2 = jnp.mean(dxh * xh, axis=-1, keepdims=True)
    return rstd * (dxh - m1 - xh * m2)


def _colsum(x):
    return jnp.sum(x, axis=0, keepdims=True)


def _acc_rows(ref, first, val):
    @pl.when(first)
    def _():
        ref[...] = val

    @pl.when(jnp.logical_not(first))
    def _():
        ref[...] += val


def _rope_tables(T):
    half = DK // 2
    freqs = ROPE_BASE ** (-np.arange(half, dtype=np.float32) / half)
    ang = (np.arange(T, dtype=np.float32)[:, None] * freqs[None, :]).astype(np.float32)
    cos, sin = np.cos(ang), np.sin(ang)
    return (jnp.asarray(np.concatenate([cos, cos], 1), F32),
            jnp.asarray(np.concatenate([-sin, sin], 1), F32))


def _decay_tables():
    h = np.arange(H, dtype=np.float64)
    log_g = np.log(1.0 - np.exp2(-5.0 - h))
    idx = np.arange(CH, dtype=np.float64)
    diff = idx[:, None] - idx[None, :]
    dm = np.where(diff[None] >= 0, np.exp(np.maximum(diff, 0.0)[None] * log_g[:, None, None]), 0.0)
    xi = np.exp((idx[None, :] + 1.0) * log_g[:, None])
    zeta = np.exp((CH - 1.0 - idx)[None, :] * log_g[:, None])
    cd = np.exp(CH * log_g)
    xi_t = np.broadcast_to(xi[:, :, None], (H, CH, DV))
    zeta_t = np.broadcast_to(zeta[:, :, None], (H, CH, DK))
    return (jnp.asarray(dm, F32), jnp.asarray(xi_t, F32), jnp.asarray(zeta_t, F32),
            [float(v) for v in cd])


def _cast_t(x):
    T = x.shape[0]
    tm = min(T, 512)

    def body(x_ref, xb_ref, xt_ref):
        v = x_ref[...]
        xb_ref[...] = v.astype(BF16)
        xt_ref[...] = v.T.astype(BF16)

    return pl.pallas_call(
        body, name="cast_t", grid=(T // tm,),
        in_specs=[pl.BlockSpec((tm, D), lambda i: (i, 0))],
        out_specs=[pl.BlockSpec((tm, D), lambda i: (i, 0)), pl.BlockSpec((D, tm), lambda i: (0, i))],
        out_shape=[jax.ShapeDtypeStruct((T, D), BF16), jax.ShapeDtypeStruct((D, T), BF16)],
        compiler_params=_cp("parallel"))(x)


def _ffn_up(xb, wall, og, ou, name):
    T = xb.shape[0]
    tm, tn = min(T, 1024), 256
    nps = FSP // tn

    def body(x_ref, wg_ref, wu_ref, a_ref, b_ref, h_ref):
        x = x_ref[...]
        a = _dot(x, wg_ref[...])
        b = _dot(x, wu_ref[...])
        a_ref[...] = a.astype(BF16)
        b_ref[...] = b.astype(BF16)
        h_ref[...] = (a * _sig(a) * b).astype(BF16)

    def wspec(off):
        return pl.BlockSpec((None, D, tn), lambda i, j: (j // nps, 0, off // tn + j % nps))

    ospec = pl.BlockSpec((tm, tn), lambda i, j: (i, j))
    return pl.pallas_call(
        body, name=name, grid=(T // tm, FP // tn),
        in_specs=[pl.BlockSpec((tm, D), lambda i, j: (i, 0)), wspec(og), wspec(ou)],
        out_specs=[ospec] * 3, out_shape=[jax.ShapeDtypeStruct((T, FP), BF16)] * 3,
        compiler_params=_cp("parallel", "parallel"))(xb, wall, wall)


def _proj_ln(hb, wall, off, res, g, b, coef, name, want_b=True):
    T, K = hb.shape
    ks = K // NCHIP
    tm = min(T, 256)

    def body(h_ref, w_ref, r_ref, g_ref, b_ref, z_ref, *rest):
        acc = _dot_nt(h_ref[:, 0:ks], w_ref[0])
        for s in range(1, NCHIP):
            acc += _dot_nt(h_ref[:, s * ks:(s + 1) * ks], w_ref[s])
        z = ALPHA * r_ref[...] + coef * acc
        z_ref[...] = z
        if want_b:
            y, _, _ = _ln_fwd(z, g_ref[...], b_ref[...])
            y_ref, yb_ref, yt_ref = rest
            y_ref[...] = y
            yb_ref[...] = y.astype(BF16)
            yt_ref[...] = y.T.astype(BF16)

    row = pl.BlockSpec((tm, D), lambda i: (i, 0))
    vec = pl.BlockSpec((1, D), lambda i: (0, 0))
    out_specs = [row]
    out_shape = [jax.ShapeDtypeStruct((T, D), F32)]
    if want_b:
        out_specs += [row, row, pl.BlockSpec((D, tm), lambda i: (0, i))]
        out_shape += [jax.ShapeDtypeStruct((T, D), F32), jax.ShapeDtypeStruct((T, D), BF16),
                      jax.ShapeDtypeStruct((D, T), BF16)]
    return pl.pallas_call(
        body, name=name, grid=(T // tm,),
        in_specs=[pl.BlockSpec((tm, K), lambda i: (i, 0)),
                  pl.BlockSpec((NCHIP, D, ks), lambda i: (0, 0, off // ks)), row, vec, vec],
        out_specs=out_specs, out_shape=out_shape,
        compiler_params=_cp("parallel"))(hb, wall, res, g, b)


def _inproj(xb, wall, off, b_in, cos_t, sin_t):
    T = xb.shape[0]
    tm, tn = min(T, 1024), 256
    nps = INS // tn
    nqk = 2 * D // tn

    def body(x_ref, w_ref, bias_ref, cos_ref, sin_ref, o_ref):
        j = pl.program_id(1)
        acc = _dot(x_ref[...], w_ref[...]) + bias_ref[...]

        @pl.when(j >= nqk)
        def _():
            o_ref[...] = acc.astype(BF16)

        @pl.when(j < nqk)
        def _():
            scale = jnp.where(j < nqk // 2, DK ** -0.5, 1.0).astype(F32)
            c = cos_ref[...]
            s = sin_ref[...]
            for hh in range(tn // DK):
                xh = acc[:, hh * DK:(hh + 1) * DK]
                o = (xh * c + pltpu.roll(xh, DK // 2, 1) * s) * scale
                o_ref[:, hh * DK:(hh + 1) * DK] = o.astype(BF16)

    return pl.pallas_call(
        body, name="inproj", grid=(T // tm, INW // tn),
        in_specs=[pl.BlockSpec((tm, D), lambda i, j: (i, 0)),
                  pl.BlockSpec((None, D, tn), lambda i, j: (j // nps, 0, off // tn + j % nps)),
                  pl.BlockSpec((1, tn), lambda i, j: (0, j)),
                  pl.BlockSpec((tm, DK), lambda i, j: (i, 0)),
                  pl.BlockSpec((tm, DK), lambda i, j: (i, 0))],
        out_specs=pl.BlockSpec((tm, tn), lambda i, j: (i, j)),
        out_shape=jax.ShapeDtypeStruct((T, INW), BF16),
        compiler_params=_cp("parallel", "parallel"))(xb, wall, b_in, cos_t, sin_t)


def _retention_fwd(proj, gn_g, dm_t, xi_t, zeta_t, cds):
    T = proj.shape[0]
    n = T // CH

    def body(q_ref, k_ref, v_ref, g_ref, gn_ref, dm_ref, xi_ref, zt_ref, r_ref, ri_ref, st_ref, state):
        @pl.when(pl.program_id(0) == 0)
        def _():
            state[...] = jnp.zeros_like(state)

        for h in range(H):
            q = q_ref[:, h * DK:(h + 1) * DK]
            k = k_ref[:, h * DK:(h + 1) * DK]
            v = v_ref[:, h * DV:(h + 1) * DV]
            rows = slice(h * DK, (h + 1) * DK)
            s_prev = state[rows, :]
            s_b = s_prev.astype(BF16)
            st_ref[rows, :] = s_b
            sc = _dot_nt(q, k) * dm_ref[h]
            r = _dot(sc.astype(BF16), v) + _dot(q, s_b) * xi_ref[h]
            kz = k.astype(F32) * zt_ref[h]
            state[rows, :] = cds[h] * s_prev + _dot(kz.T.astype(BF16), v)
            cols = slice(h * DV, (h + 1) * DV)
            r_ref[:, cols] = r
            mu = jnp.mean(r, axis=-1, keepdims=True)
            xc = r - mu
            var = jnp.mean(xc * xc, axis=-1, keepdims=True)
            y = xc * lax.rsqrt(var + EPS) * gn_ref[:, cols]
            g = g_ref[:, cols].astype(F32)
            ri_ref[:, cols] = (g * _sig(g) * y).astype(BF16)

    full3 = lambda shp: pl.BlockSpec(shp, lambda c: (0, 0, 0))
    return pl.pallas_call(
        body, name="retention_fwd", grid=(n,),
        in_specs=[pl.BlockSpec((CH, D), lambda c: (c, 0)),
                  pl.BlockSpec((CH, D), lambda c: (c, 1)),
                  pl.BlockSpec((CH, VW), lambda c: (c, 1)),
                  pl.BlockSpec((CH, VW), lambda c: (c, 2)),
                  pl.BlockSpec((1, VW), lambda c: (0, 0)),
                  full3((H, CH, CH)), full3((H, CH, DV)), full3((H, CH, DK))],
        out_specs=[pl.BlockSpec((CH, VW), lambda c: (c, 0)), pl.BlockSpec((CH, VW), lambda c: (c, 0)),
                   pl.BlockSpec((None, H * DK, DV), lambda c: (c, 0, 0))],
        out_shape=[jax.ShapeDtypeStruct((T, VW), F32), jax.ShapeDtypeStruct((T, VW), BF16),
                   jax.ShapeDtypeStruct((n, H * DK, DV), BF16)],
        scratch_shapes=[pltpu.VMEM((H * DK, DV), F32)],
        compiler_params=_cp("arbitrary"))(proj, proj, proj, proj, gn_g, dm_t, xi_t, zeta_t)


CONV_TT = 256
CONV_SB = 64
CONV_CB = 256


def _glu(a_ref, b_ref, rows=slice(None)):
    a = a_ref[rows, :].astype(F32)
    sb = _sig(b_ref[rows, :].astype(F32))
    return a, sb


def _conv_fwd(proj, conv_k, conv_b, ln_g, ln_b):
    T = proj.shape[0]
    tt = min(T, CONV_TT)
    ca, cb = 6 * D // D, 7 * D // D

    def body(a_ref, b_ref, pa_ref, pb_ref, k_ref, cb_ref, g_ref, bb_ref, u1_ref, u3_ref, win):
        i = pl.program_id(0)
        a, sb = _glu(a_ref, b_ref)
        win[HALO:, :] = a * sb
        pa, psb = _glu(pa_ref, pb_ref, slice(tt - HALO, tt))
        win[0:HALO, :] = jnp.where(i > 0, pa * psb, 0.0)
        for c0 in range(0, D, CONV_CB):
            cs = slice(c0, c0 + CONV_CB)
            for r0 in range(0, tt, CONV_SB):
                acc = jnp.zeros((CONV_SB, CONV_CB), F32)
                for w in range(CONV_W):
                    st = r0 + HALO - (CONV_W - 1) + w
                    acc += win[st:st + CONV_SB, cs] * k_ref[w:w + 1, cs]
                u1_ref[r0:r0 + CONV_SB, cs] = acc + cb_ref[:, cs]
        u2, _, _ = _ln_fwd(u1_ref[...], g_ref[...], bb_ref[...])
        u3_ref[...] = (u2 * _sig(u2)).astype(BF16)

    vec = pl.BlockSpec((1, D), lambda i: (0, 0))
    row = pl.BlockSpec((tt, D), lambda i: (i, 0))
    return pl.pallas_call(
        body, name="conv_fwd", grid=(T // tt,),
        in_specs=[pl.BlockSpec((tt, D), lambda i: (i, ca)), pl.BlockSpec((tt, D), lambda i: (i, cb)),
                  pl.BlockSpec((tt, D), lambda i: (jnp.maximum(i - 1, 0), ca)),
                  pl.BlockSpec((tt, D), lambda i: (jnp.maximum(i - 1, 0), cb)),
                  pl.BlockSpec((CONV_W, D), lambda i: (0, 0)), vec, vec, vec],
        out_specs=[row, row],
        out_shape=[jax.ShapeDtypeStruct((T, D), F32), jax.ShapeDtypeStruct((T, D), BF16)],
        scratch_shapes=[pltpu.VMEM((tt + HALO, D), F32)],
        compiler_params=_cp("parallel"))(proj, proj, proj, proj, conv_k, conv_b, ln_g, ln_b)


def _merge(ret_in, u3, proj, wall, off_r, off_c):
    T = ret_in.shape[0]
    tm = min(T, 512)
    kr, kc = VW // NCHIP, D // NCHIP

    def body(ri_ref, u3_ref, gr_ref, gc_ref, wr_ref, wc_ref, ro_ref, co_ref, m_ref):
        ro = _dot_nt(ri_ref[:, 0:kr], wr_ref[0])
        co = _dot_nt(u3_ref[:, 0:kc], wc_ref[0])
        for s in range(1, NCHIP):
            ro += _dot_nt(ri_ref[:, s * kr:(s + 1) * kr], wr_ref[s])
            co += _dot_nt(u3_ref[:, s * kc:(s + 1) * kc], wc_ref[s])
        ro_ref[...] = ro.astype(BF16)
        co_ref[...] = co.astype(BF16)
        m = _sig(gr_ref[...].astype(F32)) * ro + _sig(gc_ref[...].astype(F32)) * co
        m_ref[...] = m.astype(BF16)

    row = pl.BlockSpec((tm, D), lambda i: (i, 0))
    return pl.pallas_call(
        body, name="merge", grid=(T // tm,),
        in_specs=[pl.BlockSpec((tm, VW), lambda i: (i, 0)), row,
                  pl.BlockSpec((tm, D), lambda i: (i, 8)), pl.BlockSpec((tm, D), lambda i: (i, 9)),
                  pl.BlockSpec((NCHIP, D, kr), lambda i: (0, 0, off_r // kr)),
                  pl.BlockSpec((NCHIP, D, kc), lambda i: (0, 0, off_c // kc))],
        out_specs=[row] * 3, out_shape=[jax.ShapeDtypeStruct((T, D), BF16)] * 3,
        compiler_params=_cp("parallel"))(ret_in, u3, proj, proj, wall, wall)


def _loss_ln_bwd(z, g, b, target, coef):
    T = z.shape[0]
    tm = min(T, 256)
    nt = T // tm

    def body(z_ref, g_ref, b_ref, t_ref, loss_ref, dzb_ref, dzt_ref, dz_ref, dg_ref, db_ref, lacc):
        i = pl.program_id(0)
        gam = g_ref[...]
        y, xh, rstd = _ln_fwd(z_ref[...], gam, b_ref[...])
        e = y - t_ref[...]
        part = _colsum(e * e)
        _acc_rows(lacc, i == 0, part)
        dy = e * (1.0 / D)
        dz = _ln_bwd(dy, xh, rstd, gam)
        dz_ref[...] = dz
        dzc = coef * dz
        dzb_ref[...] = dzc.astype(BF16)
        dzt_ref[...] = dzc.T.astype(BF16)
        _acc_rows(dg_ref, i == 0, _colsum(dy * xh))
        _acc_rows(db_ref, i == 0, _colsum(dy))

        @pl.when(i == nt - 1)
        def _():
            loss_ref[...] = (0.5 / D) * jnp.sum(lacc[...], axis=1, keepdims=True)

    row = pl.BlockSpec((tm, D), lambda i: (i, 0))
    vec = pl.BlockSpec((1, D), lambda i: (0, 0))
    return pl.pallas_call(
        body, name="loss_ln_bwd", grid=(nt,),
        in_specs=[row, vec, vec, row],
        out_specs=[pl.BlockSpec((1, 1), lambda i: (0, 0)), row, pl.BlockSpec((D, tm), lambda i: (0, i)),
                   row, vec, vec],
        out_shape=[jax.ShapeDtypeStruct((1, 1), F32), jax.ShapeDtypeStruct((T, D), BF16),
                   jax.ShapeDtypeStruct((D, T), BF16), jax.ShapeDtypeStruct((T, D), F32),
                   jax.ShapeDtypeStruct((1, D), F32), jax.ShapeDtypeStruct((1, D), F32)],
        scratch_shapes=[pltpu.VMEM((1, D), F32)],
        compiler_params=_cp("arbitrary"))(z, g, b, target)


def _ffn_bwd_h(dfb, wall, od, a, b, name):
    T = dfb.shape[0]
    tm, tn = min(T, 1024), 256
    nps = FSP // tn

    def body(d_ref, w_ref, a_ref, b_ref, da_ref, db_ref):
        dh = _dot(d_ref[...], w_ref[...])
        a = a_ref[...].astype(F32)
        sg = _sig(a)
        da_ref[...] = (dh * b_ref[...].astype(F32) * (sg * (1.0 + a * (1.0 - sg)))).astype(BF16)
        db_ref[...] = (dh * a * sg).astype(BF16)

    ospec = pl.BlockSpec((tm, tn), lambda i, j: (i, j))
    return pl.pallas_call(
        body, name=name, grid=(T // tm, FP // tn),
        in_specs=[pl.BlockSpec((tm, D), lambda i, j: (i, 0)),
                  pl.BlockSpec((None, D, tn), lambda i, j: (j // nps, 0, od // tn + j % nps)),
                  ospec, ospec],
        out_specs=[ospec] * 2, out_shape=[jax.ShapeDtypeStruct((T, FP), BF16)] * 2,
        compiler_params=_cp("parallel", "parallel"))(dfb, wall, a, b)


def _dx_bwd(lhs, offs, tk, wall, dz_next, name, ln=None, colsum=False):
    T, K = lhs[0].shape
    tm = min(T, 512)
    nk = K // tk
    kps = nk // NCHIP
    nl = len(lhs)

    def body(*refs):
        l_refs = refs[:nl]
        w_refs = refs[nl:2 * nl]
        dzn_ref = refs[2 * nl]
        pos = 2 * nl + 1
        if ln is not None:
            z_ref, g_ref = refs[pos:pos + 2]
            pos += 2
        outs = refs[pos:-1]
        acc = refs[-1]
        i, k = pl.program_id(0), pl.program_id(1)
        part = _dot_nt(l_refs[0][...], w_refs[0][...])
        for l in range(1, nl):
            part += _dot_nt(l_refs[l][...], w_refs[l][...])
        _acc_rows(acc, k == 0, part)
        if colsum:
            cs_ref = outs[-1]
            val = _colsum(l_refs[0][...].astype(F32))
            col = pl.ds(pl.multiple_of(k * tk, tk), tk)

            @pl.when(i == 0)
            def _():
                cs_ref[:, col] = val

            @pl.when(i > 0)
            def _():
                cs_ref[:, col] += val

        @pl.when(k == nk - 1)
        def _():
            dy = acc[...] + ALPHA * dzn_ref[...]
            if ln is None:
                outs[0][...] = dy
            else:
                gam = g_ref[...]
                _, xh, rstd = _ln_fwd(z_ref[...], gam, 0.0)
                dz = _ln_bwd(dy, xh, rstd, gam)
                dzc = ln[2] * dz
                outs[0][...] = dzc.astype(BF16)
                outs[1][...] = dzc.T.astype(BF16)
                outs[2][...] = dz
                _acc_rows(outs[3], i == 0, _colsum(dy * xh))
                _acc_rows(outs[4], i == 0, _colsum(dy))

    row = pl.BlockSpec((tm, D), lambda i, k: (i, 0))
    vec = pl.BlockSpec((1, D), lambda i, k: (0, 0))
    in_specs = [pl.BlockSpec((tm, tk), lambda i, k: (i, k))] * nl
    in_specs += [pl.BlockSpec((None, D, tk), functools.partial(lambda o, i, k: (k // kps, 0, o // tk + k % kps), o))
                 for o in offs]
    in_specs += [row]
    args = list(lhs) + [wall] * nl + [dz_next]
    if ln is None:
        out_specs = [row]
        out_shape = [jax.ShapeDtypeStruct((T, D), F32)]
    else:
        in_specs += [row, vec]
        args += [ln[0], ln[1]]
        out_specs = [row, pl.BlockSpec((D, tm), lambda i, k: (0, i)), row, vec, vec]
        out_shape = [jax.ShapeDtypeStruct((T, D), BF16), jax.ShapeDtypeStruct((D, T), BF16),
                     jax.ShapeDtypeStruct((T, D), F32), jax.ShapeDtypeStruct((1, D), F32),
                     jax.ShapeDtypeStruct((1, D), F32)]
    if colsum:
        out_specs += [pl.BlockSpec((1, K), lambda i, k: (0, 0))]
        out_shape += [jax.ShapeDtypeStruct((1, K), F32)]
    return pl.pallas_call(
        body, name=name, grid=(T // tm, nk), in_specs=in_specs, out_specs=out_specs, out_shape=out_shape,
        scratch_shapes=[pltpu.VMEM((tm, D), F32)],
        compiler_params=_cp("arbitrary", "arbitrary"))(*args)


def _wgrad(lhs_t, rhs, off, name, g_all=None):
    T, N = rhs.shape
    tn = 256
    nps = N // NCHIP // tn

    def body(*refs):
        l_ref, r_ref, o_ref = refs[0], refs[1], refs[-1]
        o_ref[...] = _dot(l_ref[...], r_ref[...])

    in_specs = [pl.BlockSpec((D, T), lambda j: (0, 0)), pl.BlockSpec((T, tn), lambda j: (0, j))]
    args = [lhs_t, rhs]
    aliases = {}
    if g_all is not None:
        in_specs.append(pl.BlockSpec(memory_space=pl.ANY))
        args.append(g_all)
        aliases = {2: 0}
    return pl.pallas_call(
        body, name=name, grid=(N // tn,), in_specs=in_specs,
        out_specs=pl.BlockSpec((None, D, tn), lambda j: (j // nps, 0, off // tn + j % nps)),
        out_shape=jax.ShapeDtypeStruct((NCHIP, D, WCOLS), F32),
        input_output_aliases=aliases,
        compiler_params=_cp("parallel"))(*args)


def _merge_bwd(dmb, wall, off, proj, ro, co):
    T = dmb.shape[0]
    tm = min(T, 512)
    ks = D // NCHIP

    def body(d_ref, w_ref, gr_ref, gc_ref, ro_ref, co_ref, dro_ref, drot_ref, dco_ref, dcot_ref, dp_ref):
        d = d_ref[...]
        dmg = jnp.concatenate([_dot(d, w_ref[s]) for s in range(NCHIP)], axis=1)
        sr = _sig(gr_ref[...].astype(F32))
        sc = _sig(gc_ref[...].astype(F32))
        dro = dmg * sr
        dco = dmg * sc
        dro_ref[...] = dro.astype(BF16)
        drot_ref[...] = dro.T.astype(BF16)
        dco_ref[...] = dco.astype(BF16)
        dcot_ref[...] = dco.T.astype(BF16)
        dp_ref[:, 0:D] = (dmg * ro_ref[...].astype(F32) * sr * (1.0 - sr)).astype(BF16)
        dp_ref[:, D:2 * D] = (dmg * co_ref[...].astype(F32) * sc * (1.0 - sc)).astype(BF16)

    row = pl.BlockSpec((tm, D), lambda i: (i, 0))
    col = pl.BlockSpec((D, tm), lambda i: (0, i))
    return pl.pallas_call(
        body, name="merge_bwd", grid=(T // tm,),
        in_specs=[row, pl.BlockSpec((NCHIP, D, ks), lambda i: (0, 0, off // ks)),
                  pl.BlockSpec((tm, D), lambda i: (i, 8)), pl.BlockSpec((tm, D), lambda i: (i, 9)), row, row],
        out_specs=[row, col, row, col, pl.BlockSpec((tm, 2 * D), lambda i: (i, 4))],
        out_shape=[jax.ShapeDtypeStruct((T, D), BF16), jax.ShapeDtypeStruct((D, T), BF16),
                   jax.ShapeDtypeStruct((T, D), BF16), jax.ShapeDtypeStruct((D, T), BF16),
                   jax.ShapeDtypeStruct((T, INW), BF16)],
        compiler_params=_cp("parallel"))(dmb, wall, proj, proj, ro, co)


def _reto_bwd(dro, wall, off, r, proj, gn_g, dproj):
    T = dro.shape[0]
    tm = min(T, 512)
    hps = H // NCHIP

    def body(d_ref, w_ref, r_ref, g_ref, gn_ref, _, dr_ref, dgn_ref, dp_ref):
        i = pl.program_id(1)
        dri = _dot(d_ref[...], w_ref[...])
        rr = r_ref[...]
        mu = jnp.mean(rr, axis=-1, keepdims=True)
        xc = rr - mu
        var = jnp.mean(xc * xc, axis=-1, keepdims=True)
        rstd = lax.rsqrt(var + EPS)
        rn = xc * rstd
        gn = gn_ref[...]
        g = g_ref[...].astype(F32)
        sg = _sig(g)
        dy = dri * (g * sg)
        dp_ref[...] = (dri * (rn * gn) * (sg * (1.0 + g * (1.0 - sg)))).astype(BF16)
        _acc_rows(dgn_ref, i == 0, _colsum(dy * rn))
        dr_ref[...] = _ln_bwd(dy, rn, rstd, gn).astype(BF16)

    return pl.pallas_call(
        body, name="reto_bwd", grid=(H, T // tm),
        in_specs=[pl.BlockSpec((tm, D), lambda j, i: (i, 0)),
                  pl.BlockSpec((None, D, DV), lambda j, i: (j // hps, 0, off // DV + j % hps)),
                  pl.BlockSpec((tm, DV), lambda j, i: (i, j)),
                  pl.BlockSpec((tm, DV), lambda j, i: (i, 2 * VW // DV + j)),
                  pl.BlockSpec((1, DV), lambda j, i: (0, j)),
                  pl.BlockSpec(memory_space=pl.ANY)],
        out_specs=[pl.BlockSpec((tm, DV), lambda j, i: (i, j)), pl.BlockSpec((1, DV), lambda j, i: (0, j)),
                   pl.BlockSpec((tm, DV), lambda j, i: (i, 2 * VW // DV + j))],
        out_shape=[jax.ShapeDtypeStruct((T, VW), BF16), jax.ShapeDtypeStruct((1, VW), F32),
                   jax.ShapeDtypeStruct((T, INW), BF16)],
        input_output_aliases={5: 2},
        compiler_params=_cp("arbitrary", "arbitrary"))(dro, wall, r, proj, gn_g, dproj)


def _retention_bwd(proj, dr, states, cos_t, sin_t, dm_t, xi_t, zeta_t, cds, dproj):
    T = proj.shape[0]
    n = T // CH
    scale = DK ** -0.5

    def body(q_ref, k_ref, v_ref, dr_ref, st_ref, cos_ref, sin_ref, dm_ref, xi_ref, zt_ref, _, dp_ref, ds):
        @pl.when(pl.program_id(0) == 0)
        def _():
            ds[...] = jnp.zeros_like(ds)

        cos = cos_ref[...]
        sin = sin_ref[...]

        def unrope(d):
            return d * cos + pltpu.roll(d * sin, DK // 2, 1)

        for h in range(H):
            q = q_ref[:, h * DK:(h + 1) * DK]
            k = k_ref[:, h * DK:(h + 1) * DK]
            v = v_ref[:, h * DV:(h + 1) * DV]
            d_r = dr_ref[:, h * DV:(h + 1) * DV]
            rows = slice(h * DK, (h + 1) * DK)
            s_b = st_ref[rows, :]
            dm = dm_ref[h]
            zt = zt_ref[h]
            sc = _dot_nt(q, k) * dm
            dsc = _dot_nt(d_r, v) * dm
            drx = (d_r.astype(F32) * xi_ref[h]).astype(BF16)
            ds_prev = ds[rows, :]
            ds_b = ds_prev.astype(BF16)
            kz = (k.astype(F32) * zt).astype(BF16)
            dq = _dot(dsc.astype(BF16), k) + _dot_nt(drx, s_b)
            dk = _dot(dsc.T.astype(BF16), q) + _dot_nt(v, ds_b) * zt
            dv = _dot(sc.T.astype(BF16), d_r) + _dot(kz, ds_b)
            ds[rows, :] = cds[h] * ds_prev + _dot(q.astype(F32).T.astype(BF16), drx)
            dp_ref[:, h * DK:(h + 1) * DK] = unrope(dq * scale).astype(BF16)
            dp_ref[:, D + h * DK:D + (h + 1) * DK] = unrope(dk).astype(BF16)
            dp_ref[:, 2 * D + h * DV:2 * D + (h + 1) * DV] = dv.astype(BF16)

    rv = lambda c: n - 1 - c
    full3 = lambda shp: pl.BlockSpec(shp, lambda c: (0, 0, 0))
    return pl.pallas_call(
        body, name="retention_bwd", grid=(n,),
        in_specs=[pl.BlockSpec((CH, D), lambda c: (rv(c), 0)),
                  pl.BlockSpec((CH, D), lambda c: (rv(c), 1)),
                  pl.BlockSpec((CH, VW), lambda c: (rv(c), 1)),
                  pl.BlockSpec((CH, VW), lambda c: (rv(c), 0)),
                  pl.BlockSpec((None, H * DK, DV), lambda c: (rv(c), 0, 0)),
                  pl.BlockSpec((CH, DK), lambda c: (rv(c), 0)),
                  pl.BlockSpec((CH, DK), lambda c: (rv(c), 0)),
                  full3((H, CH, CH)), full3((H, CH, DV)), full3((H, CH, DK)),
                  pl.BlockSpec(memory_space=pl.ANY)],
        out_specs=pl.BlockSpec((CH, 2 * D + VW), lambda c: (rv(c), 0)),
        out_shape=jax.ShapeDtypeStruct((T, INW), BF16),
        input_output_aliases={10: 0},
        scratch_shapes=[pltpu.VMEM((H * DK, DV), F32)],
        compiler_params=_cp("arbitrary"))(proj, proj, proj, dr, states, cos_t, sin_t, dm_t, xi_t, zeta_t, dproj)


def _convo_bwd(dco, wall, off, u1, ln_g, ln_b):
    T = dco.shape[0]
    tm = min(T, 512)
    ks = D // NCHIP

    def body(d_ref, w_ref, u1_ref, g_ref, b_ref, du1_ref, dg_ref, db_ref, dcb_ref):
        i = pl.program_id(0)
        d = d_ref[...]
        du3 = jnp.concatenate([_dot(d, w_ref[s]) for s in range(NCHIP)], axis=1)
        gam = g_ref[...]
        u2, xh, rstd = _ln_fwd(u1_ref[...], gam, b_ref[...])
        sg = _sig(u2)
        du2 = du3 * (sg * (1.0 + u2 * (1.0 - sg)))
        du1 = _ln_bwd(du2, xh, rstd, gam)
        du1_ref[...] = du1
        _acc_rows(dg_ref, i == 0, _colsum(du2 * xh))
        _acc_rows(db_ref, i == 0, _colsum(du2))
        _acc_rows(dcb_ref, i == 0, _colsum(du1))

    row = pl.BlockSpec((tm, D), lambda i: (i, 0))
    vec = pl.BlockSpec((1, D), lambda i: (0, 0))
    return pl.pallas_call(
        body, name="convo_bwd", grid=(T // tm,),
        in_specs=[row, pl.BlockSpec((NCHIP, D, ks), lambda i: (0, 0, off // ks)), row, vec, vec],
        out_specs=[row, vec, vec, vec],
        out_shape=[jax.ShapeDtypeStruct((T, D), F32)] + [jax.ShapeDtypeStruct((1, D), F32)] * 3,
        compiler_params=_cp("arbitrary"))(dco, wall, u1, ln_g, ln_b)


def _conv_bwd(du1, proj, conv_k, dproj):
    T = du1.shape[0]
    tt = min(T, CONV_TT)
    nt = T // tt
    ca, cb = 6, 7

    def body(d_ref, dn_ref, a_ref, b_ref, pa_ref, pb_ref, k_ref, _, dp_ref, dk_ref, win_u, win_d):
        i = pl.program_id(0)
        a, sb = _glu(a_ref, b_ref)
        win_u[HALO:, :] = a * sb
        pa, psb = _glu(pa_ref, pb_ref, slice(tt - HALO, tt))
        win_u[0:HALO, :] = jnp.where(i > 0, pa * psb, 0.0)
        win_d[0:tt, :] = d_ref[...]
        win_d[tt:, :] = jnp.where(i < nt - 1, dn_ref[0:HALO, :], 0.0)

        @pl.when(i == 0)
        def _():
            dk_ref[...] = jnp.zeros_like(dk_ref)

        for c0 in range(0, D, CONV_CB):
            cs = slice(c0, c0 + CONV_CB)
            for r0 in range(0, tt, CONV_SB):
                acc = jnp.zeros((CONV_SB, CONV_CB), F32)
                for w in range(CONV_W):
                    st = r0 + (CONV_W - 1) - w
                    acc += win_d[st:st + CONV_SB, cs] * k_ref[w:w + 1, cs]
                aa = a_ref[r0:r0 + CONV_SB, cs].astype(F32)
                ss = _sig(b_ref[r0:r0 + CONV_SB, cs].astype(F32))
                dp_ref[r0:r0 + CONV_SB, cs] = (acc * ss).astype(BF16)
                dp_ref[r0:r0 + CONV_SB, c0 + D:c0 + D + CONV_CB] = (acc * aa * ss * (1.0 - ss)).astype(BF16)
            for w in range(CONV_W):
                acc = jnp.zeros((CONV_SB, CONV_CB), F32)
                for r0 in range(0, tt, CONV_SB):
                    st = r0 + HALO - (CONV_W - 1) + w
                    acc += win_d[r0:r0 + CONV_SB, cs] * win_u[st:st + CONV_SB, cs]
                dk_ref[w:w + 1, cs] += _colsum(acc)

    blk = lambda f, c: pl.BlockSpec((tt, D), lambda i: (f(i), c))
    cur = lambda i: i
    prv = lambda i: jnp.maximum(i - 1, 0)
    nxt = lambda i: jnp.minimum(i + 1, nt - 1)
    return pl.pallas_call(
        body, name="conv_bwd", grid=(nt,),
        in_specs=[blk(cur, 0), blk(nxt, 0), blk(cur, ca), blk(cur, cb), blk(prv, ca), blk(prv, cb),
                  pl.BlockSpec((CONV_W, D), lambda i: (0, 0)), pl.BlockSpec(memory_space=pl.ANY)],
        out_specs=[pl.BlockSpec((tt, 2 * D), lambda i: (i, 3)), pl.BlockSpec((HALO, D), lambda i: (0, 0))],
        out_shape=[jax.ShapeDtypeStruct((T, INW), BF16), jax.ShapeDtypeStruct((HALO, D), F32)],
        input_output_aliases={7: 0},
        scratch_shapes=[pltpu.VMEM((tt + HALO, D), F32), pltpu.VMEM((tt + HALO, D), F32)],
        compiler_params=_cp("arbitrary"))(du1, du1, proj, proj, proj, proj, conv_k, dproj)


def _local_step(x, target, wts, sp):
    T = x.shape[0]
    cos_t, sin_t = _rope_tables(T)
    dm_t, xi_t, zeta_t, cds = _decay_tables()
    wa = lambda key: wts[LOC[key][0]]
    wo = lambda key: LOC[key][1]

    xb, xt = _cast_t(x)
    a1, b1, h1 = _ffn_up(xb, wa("g1"), wo("g1"), wo("u1"), "ffn1_up")
    z1, x1, x1b, x1t = _proj_ln(h1, wa("d1"), wo("d1"), x, sp["ln1_g"], sp["ln1_b"], 0.5, "ffn1_down_ln")
    proj = _inproj(x1b, wa("w_in"), wo("w_in"), sp["b_in"], cos_t, sin_t)
    r, ret_in, states = _retention_fwd(proj, sp["ret_gn_g"], dm_t, xi_t, zeta_t, cds)
    u1, u3 = _conv_fwd(proj, sp["conv_k"], sp["conv_b"], sp["conv_ln_g"], sp["conv_ln_b"])
    ro, co, merged = _merge(ret_in, u3, proj, wa("w_ret_o"), wo("w_ret_o"), wo("w_conv_o"))
    z2, x2, x2b, x2t = _proj_ln(merged, wa("w_out"), wo("w_out"), x1, sp["ln2_g"], sp["ln2_b"], 1.0, "out_proj_ln")
    a2, b2, h2 = _ffn_up(x2b, wa("g2"), wo("g2"), wo("u2"), "ffn2_up")
    (z3,) = _proj_ln(h2, wa("d2"), wo("d2"), x2, sp["ln3_g"], sp["ln3_b"], 0.5, "ffn2_down", want_b=False)

    sg = {}
    loss, df2b, df2t, dz3, sg["ln3_g"], sg["ln3_b"] = _loss_ln_bwd(z3, sp["ln3_g"], sp["ln3_b"], target, 0.5)
    da2, db2 = _ffn_bwd_h(df2b, wa("d2"), wo("d2"), a2, b2, "ffn2_bwd_h")
    g_all = _wgrad(df2t, h2, OFF["d2"], "wgrad_d2")
    g_all = _wgrad(x2t, da2, OFF["g2"], "wgrad_g2", g_all)
    g_all = _wgrad(x2t, db2, OFF["u2"], "wgrad_u2", g_all)
    dmb, dmt, dz2, sg["ln2_g"], sg["ln2_b"] = _dx_bwd(
        [da2, db2], [wo("g2"), wo("u2")], FSP, wa("g2"), dz3, "ffn2_dx_ln", ln=(z2, sp["ln2_g"], 1.0))
    g_all = _wgrad(dmt, merged, OFF["w_out"], "wgrad_out", g_all)
    dro, drot, dco, dcot, dproj = _merge_bwd(dmb, wa("w_out"), wo("w_out"), proj, ro, co)
    g_all = _wgrad(drot, ret_in, OFF["w_ret_o"], "wgrad_ret_o", g_all)
    g_all = _wgrad(dcot, u3, OFF["w_conv_o"], "wgrad_conv_o", g_all)
    dr, sg["ret_gn_g"], dproj = _reto_bwd(dro, wa("w_ret_o"), wo("w_ret_o"), r, proj, sp["ret_gn_g"], dproj)
    dproj = _retention_bwd(proj, dr, states, cos_t, sin_t, dm_t, xi_t, zeta_t, cds, dproj)
    du1, sg["conv_ln_g"], sg["conv_ln_b"], sg["conv_b"] = _convo_bwd(
        dco, wa("w_conv_o"), wo("w_conv_o"), u1, sp["conv_ln_g"], sp["conv_ln_b"])
    dproj, dck = _conv_bwd(du1, proj, sp["conv_k"], dproj)
    sg["conv_k"] = dck[:CONV_W]
    g_all = _wgrad(x1t, dproj, OFF["w_in"], "wgrad_in", g_all)
    df1b, df1t, dz1, sg["ln1_g"], sg["ln1_b"], sg["b_in"] = _dx_bwd(
        [dproj], [wo("w_in")], 512, wa("w_in"), dz2, "mixer_dx_ln", ln=(z1, sp["ln1_g"], 0.5), colsum=True)
    da1, db1 = _ffn_bwd_h(df1b, wa("d1"), wo("d1"), a1, b1, "ffn1_bwd_h")
    g_all = _wgrad(df1t, h1, OFF["d1"], "wgrad_d1", g_all)
    g_all = _wgrad(xt, da1, OFF["g1"], "wgrad_g1", g_all)
    g_all = _wgrad(xt, db1, OFF["u1"], "wgrad_u1", g_all)
    (grad_x,) = _dx_bwd([da1, db1], [wo("g1"), wo("u1")], FSP, wa("g1"), dz1, "ffn1_dx")
    return loss, grad_x, g_all, sg


MESH = pl.DeviceIdType.MESH
ANY = pl.BlockSpec(memory_space=pl.ANY)
HALF = D // 2


def _place():
    x, y, c = lax.axis_index("x"), lax.axis_index("y"), lax.axis_index("c")
    chips = [(1 - x, y), (x, 1 - y), (1 - x, 1 - y)]
    return x, y, c, chips


GATHER_ID = 1


def _gather_weights(wloc, name):
    w_ref = jax.new_ref(wloc, memory_space=pltpu.MemorySpace.HBM)
    o_ref = jax.empty_ref(jax.ShapeDtypeStruct((NCHIP, 2, HALF, wloc.shape[-1]), BF16),
                          memory_space=pltpu.MemorySpace.HBM)
    dma = pltpu.SemaphoreType.DMA

    @pl.kernel(mesh=plsc.ScalarSubcoreMesh(axis_name="sc", num_cores=1), name=name,
               scratch_types=(dma(()), dma((3,)), dma((3,)), dma((3,)), dma((3,))),
               compiler_params=pltpu.CompilerParams(collective_id=GATHER_ID))
    def launch(lsem, s1, r1, s2, r2):
        x, y, c, chips = _place()
        me = 2 * x + y
        sib = (x, y, 1 - c)
        barrier = pltpu.get_barrier_semaphore()
        for peer in [sib] + [(px, py, c) for px, py in chips]:
            pl.semaphore_signal(barrier, inc=1, device_id=peer, device_id_type=MESH)
        pl.semaphore_wait(barrier, 4)
        mine = pltpu.make_async_copy(w_ref, o_ref.at[me], lsem)
        mine.start()

        def rc(src, dst, ss, rs, dev):
            return pltpu.make_async_remote_copy(src_ref=src, dst_ref=dst, send_sem=ss, recv_sem=rs,
                                                device_id=dev, device_id_type=MESH)

        first = [rc(w_ref.at[c], o_ref.at[me, c], s1.at[j], r1.at[j], (*chip, c)) for j, chip in enumerate(chips)]
        for cp in first:
            cp.start()
        passed = []
        for j, (px, py) in enumerate(chips):
            slot = o_ref.at[2 * px + py, c]
            rc(slot, slot, s1.at[j], r1.at[j], (px, py, c)).wait_recv()
            cp = rc(slot, slot, s2.at[j], r2.at[j], sib)
            cp.start()
            passed.append(cp)
        for j, (px, py) in enumerate(chips):
            slot = o_ref.at[2 * px + py, 1 - c]
            rc(slot, slot, s2.at[j], r2.at[j], sib).wait_recv()
        for cp in first + passed:
            cp.wait_send()
        mine.wait()

    launch()
    return o_ref[...]


def _pair_exchange(g5):
    def body(g_ref, o_ref, ss, rs):
        x, y, c, _ = _place()
        sib = (x, y, 1 - c)
        cps = [pltpu.make_async_remote_copy(src_ref=g_ref.at[j, 1 - c], dst_ref=o_ref.at[j], send_sem=ss.at[j],
                                            recv_sem=rs.at[j], device_id=sib, device_id_type=MESH)
               for j in range(NCHIP)]
        for cp in cps:
            cp.start()
        for cp in cps:
            cp.wait()

    dma = pltpu.SemaphoreType.DMA
    return pl.pallas_call(
        body, name="pair_exchange", in_specs=[ANY], out_specs=ANY,
        out_shape=jax.ShapeDtypeStruct((NCHIP, HALF, WCOLS), F32),
        scratch_shapes=[dma((NCHIP,)), dma((NCHIP,))])(g5)


RS_TR = 128
RS_TC = 2048


def _pair_sum(pos, g5, got):
    def body(pos_ref, g_ref, r_ref, o_ref):
        o_ref[...] = (g_ref[...] + r_ref[...]).astype(BF16)

    grid_spec = pltpu.PrefetchScalarGridSpec(
        num_scalar_prefetch=1, grid=(NCHIP, HALF // RS_TR, WCOLS // RS_TC),
        in_specs=[pl.BlockSpec((None, None, RS_TR, RS_TC), lambda j, i, k, p: (j, p[0], i, k)),
                  pl.BlockSpec((None, RS_TR, RS_TC), lambda j, i, k, p: (j, i, k))],
        out_specs=pl.BlockSpec((None, RS_TR, RS_TC), lambda j, i, k, p: (j, i, k)))
    return pl.pallas_call(
        body, name="pair_sum", grid_spec=grid_spec,
        out_shape=jax.ShapeDtypeStruct((NCHIP, HALF, WCOLS), BF16),
        compiler_params=_cp("parallel", "parallel", "parallel"))(pos, g5, got)


def _chip_exchange(pb):
    def body(p_ref, o_ref, ss, rs):
        x, y, c, chips = _place()
        cps = [pltpu.make_async_remote_copy(src_ref=p_ref.at[2 * px + py], dst_ref=o_ref.at[j], send_sem=ss.at[j],
                                            recv_sem=rs.at[j], device_id=(px, py, c), device_id_type=MESH)
               for j, (px, py) in enumerate(chips)]
        for cp in cps:
            cp.start()
        for cp in cps:
            cp.wait()

    dma = pltpu.SemaphoreType.DMA
    return pl.pallas_call(
        body, name="chip_exchange", in_specs=[ANY], out_specs=ANY,
        out_shape=jax.ShapeDtypeStruct((3, HALF, WCOLS), BF16),
        scratch_shapes=[dma((3,)), dma((3,))])(pb)


def _chip_sum(pos, g5, got, peers):
    def body(pos_ref, g_ref, r_ref, p_ref, o_ref):
        acc = g_ref[...] + r_ref[...]
        for j in range(3):
            acc += p_ref[j].astype(F32)
        o_ref[...] = acc

    grid_spec = pltpu.PrefetchScalarGridSpec(
        num_scalar_prefetch=1, grid=(HALF // RS_TR, WCOLS // RS_TC),
        in_specs=[pl.BlockSpec((None, None, RS_TR, RS_TC), lambda i, k, p: (p[0], p[1], i, k)),
                  pl.BlockSpec((None, RS_TR, RS_TC), lambda i, k, p: (p[0], i, k)),
                  pl.BlockSpec((3, RS_TR, RS_TC), lambda i, k, p: (0, i, k))],
        out_specs=pl.BlockSpec((None, RS_TR, RS_TC), lambda i, k, p: (p[1], i, k)))
    return pl.pallas_call(
        body, name="chip_sum", grid_spec=grid_spec,
        out_shape=jax.ShapeDtypeStruct((2, HALF, WCOLS), F32),
        compiler_params=_cp("parallel", "parallel"))(pos, g5, got, peers)


def _pair_share(gsum):
    def body(g_ref, o_ref, ss, rs):
        x, y, c, _ = _place()
        cp = pltpu.make_async_remote_copy(src_ref=o_ref.at[c], dst_ref=o_ref.at[c], send_sem=ss, recv_sem=rs,
                                          device_id=(x, y, 1 - c), device_id_type=MESH)
        cp.start()
        cp.wait_send()
        pltpu.make_async_remote_copy(src_ref=o_ref.at[1 - c], dst_ref=o_ref.at[1 - c], send_sem=ss, recv_sem=rs,
                                     device_id=(x, y, 1 - c), device_id_type=MESH).wait_recv()

    dma = pltpu.SemaphoreType.DMA
    return pl.pallas_call(
        body, name="pair_share", in_specs=[ANY], out_specs=ANY,
        out_shape=jax.ShapeDtypeStruct((2, HALF, WCOLS), F32),
        input_output_aliases={0: 0},
        scratch_shapes=[dma(()), dma(())])(gsum)


SMALL_ROWS = 56


def _allreduce_small(vec, name):
    def body(v_ref, o_ref, buf, ss, rs):
        x, y, c, _ = _place()
        me = 4 * x + 2 * y + c
        buf[me] = v_ref[...]
        cps = []
        for m in range(1, 8):
            dev = (x ^ (m >> 2), y ^ ((m >> 1) & 1), c ^ (m & 1))
            cp = pltpu.make_async_remote_copy(src_ref=v_ref, dst_ref=buf.at[me], send_sem=ss.at[m - 1],
                                              recv_sem=rs.at[m - 1], device_id=dev, device_id_type=MESH)
            cp.start()
            cps.append(cp)
        for cp in cps:
            cp.wait()
        acc = buf[0]
        for d in range(1, 8):
            acc += buf[d]
        o_ref[...] = acc

    dma = pltpu.SemaphoreType.DMA
    vm = pl.BlockSpec(memory_space=pltpu.VMEM)
    return pl.pallas_call(
        body, name=name, in_specs=[vm], out_specs=vm,
        out_shape=jax.ShapeDtypeStruct((SMALL_ROWS, D), F32),
        scratch_shapes=[pltpu.VMEM((8, SMALL_ROWS, D), F32), dma((7,)), dma((7,))])(vec)


def _adamw_math(w, g, m, v):
    m2 = ADAM_B1 * m + (1.0 - ADAM_B1) * g
    v2 = ADAM_B2 * v + (1.0 - ADAM_B2) * (g * g)
    m_hat = m2 / (1.0 - ADAM_B1 ** ADAM_STEP)
    v_hat = v2 / (1.0 - ADAM_B2 ** ADAM_STEP)
    delta = -ADAM_LR * (m_hat / (jnp.sqrt(v_hat) + ADAM_EPS) + ADAM_WD * w)
    return delta, m2, v2


def _adamw(w, g, m, v, name, g_block=None):
    R, C = w.shape
    tr = R
    gw_hint = C if g_block is None else g_block[0]
    for cand in (256, 176, 128, 64, 32, 16, 8):
        if R % cand == 0 and cand * max(C, gw_hint) * 4 <= (1 << 20):
            tr = cand
            break
    gw, gi = (C, 0) if g_block is None else g_block

    def body(w_ref, g_ref, m_ref, v_ref, go_ref, d_ref, mo_ref, vo_ref):
        g = g_ref[:, 0:C]
        d, m2, v2 = _adamw_math(w_ref[...], g, m_ref[...], v_ref[...])
        go_ref[...] = g
        d_ref[...] = d
        mo_ref[...] = m2
        vo_ref[...] = v2

    spec = pl.BlockSpec((tr, C), lambda i: (i, 0))
    return pl.pallas_call(
        body, name=name, grid=(R // tr,),
        in_specs=[spec, pl.BlockSpec((tr, gw), lambda i: (i, gi)), spec, spec],
        out_specs=[spec] * 4, out_shape=[jax.ShapeDtypeStruct((R, C), F32)] * 4,
        compiler_params=_cp("parallel"))(w, g, m, v)


BIG = ["ffn1_w_gate", "ffn1_w_up", "ffn1_w_down", "w_in", "w_ret_o", "w_conv_o", "w_out",
       "ffn2_w_gate", "ffn2_w_up", "ffn2_w_down"]
SLAB = {"ffn1_w_gate": "g1", "ffn1_w_up": "u1", "ffn1_w_down": "d1", "w_in": "w_in", "w_ret_o": "w_ret_o",
        "w_conv_o": "w_conv_o", "w_out": "w_out", "ffn2_w_gate": "g2", "ffn2_w_up": "u2", "ffn2_w_down": "d2"}
TRANSPOSED = {"ffn1_w_down", "ffn2_w_down", "w_ret_o", "w_conv_o", "w_out"}
SMALL = ["ln1_g", "ln1_b", "ln2_g", "ln2_b", "ln3_g", "ln3_b", "conv_ln_g", "conv_ln_b", "conv_b",
         "ret_gn_g", "b_in"]
ORDER = ["ffn1_w_gate", "ffn1_w_up", "ffn1_w_down", "ln1_g", "ln1_b", "w_in", "b_in", "ret_gn_g", "conv_k",
         "conv_b", "conv_ln_g", "conv_ln_b", "w_ret_o", "w_conv_o", "w_out", "ln2_g", "ln2_b",
         "ffn2_w_gate", "ffn2_w_up", "ffn2_w_down", "ln3_g", "ln3_b"]


def _slab_width(name):
    return WIDTH[SLAB[name]]


def _pack_group(weights, keys):
    by_key = {SLAB[n]: n for n in BIG}
    parts = []
    for key in keys:
        w = weights[by_key[key]]
        w = w.T if by_key[key] in TRANSPOSED else w
        parts.append(jnp.pad(w, ((0, 0), (0, WIDTH[key] - w.shape[1]))))
    return jnp.concatenate(parts, axis=1).astype(BF16)


def _pack_small(vals, rows):
    flat = jnp.concatenate([vals[n].reshape(-1) for n in SMALL] + [vals["conv_k"].reshape(-1)])
    return jnp.pad(flat, (0, rows * D - flat.shape[0])).reshape(rows, D)


def _unpack_small(arr, shapes):
    flat = arr.reshape(-1)
    out, pos = {}, 0
    for n in SMALL + ["conv_k"]:
        size = int(np.prod(shapes[n]))
        out[n] = flat[pos:pos + size].reshape(shapes[n])
        pos += size
    return out


def kernel(x, ffn1_w_gate, ffn1_w_up, ffn1_w_down, ln1_g, ln1_b, w_in, b_in, ret_gn_g, conv_k, conv_b, conv_ln_g, conv_ln_b, w_ret_o, w_conv_o, w_out, ln2_g, ln2_b, ffn2_w_gate, ffn2_w_up, ffn2_w_down, ln3_g, ln3_b, loss_target, m_ffn1_w_gate, m_ffn1_w_up, m_ffn1_w_down, m_ln1_g, m_ln1_b, m_w_in, m_b_in, m_ret_gn_g, m_conv_k, m_conv_b, m_conv_ln_g, m_conv_ln_b, m_w_ret_o, m_w_conv_o, m_w_out, m_ln2_g, m_ln2_b, m_ffn2_w_gate, m_ffn2_w_up, m_ffn2_w_down, m_ln3_g, m_ln3_b, v_ffn1_w_gate, v_ffn1_w_up, v_ffn1_w_down, v_ln1_g, v_ln1_b, v_w_in, v_b_in, v_ret_gn_g, v_conv_k, v_conv_b, v_conv_ln_g, v_conv_ln_b, v_w_ret_o, v_w_conv_o, v_w_out, v_ln2_g, v_ln2_b, v_ffn2_w_gate, v_ffn2_w_up, v_ffn2_w_down, v_ln3_g, v_ln3_b):
    args = dict(locals())
    w = {n: args[n] for n in ORDER}
    m = {n: args["m_" + n] for n in ORDER}
    v = {n: args["v_" + n] for n in ORDER}
    xi, yi, ci = lax.axis_index("x"), lax.axis_index("y"), lax.axis_index("c")
    chip = 2 * xi + yi

    shards = {n: w[n][0] for n in BIG}
    wts = []
    for gi, keys in enumerate(GROUPS):
        slab = _pack_group(shards, keys)
        cols = slab.shape[1]
        wts.append(_gather_weights(slab.reshape(2, HALF, cols), f"gather_{gi}").reshape(NCHIP, D, cols))

    sp = {n: w[n] for n in SMALL}
    sp["conv_k"] = None
    kfull_shape = (CONV_W, D)
    kpad = jnp.zeros(kfull_shape, F32)
    kpad = lax.dynamic_update_slice(kpad, w["conv_k"][0, :, 0, :] * jnp.where(ci == 0, 1.0, 0.0), (0, chip * (D // NCHIP)))
    kvec = jnp.pad(kpad.reshape(-1), (0, SMALL_ROWS * D - CONV_W * D)).reshape(SMALL_ROWS, D)
    sp["conv_k"] = _allreduce_small(kvec, "gather_conv_k").reshape(-1)[:CONV_W * D].reshape(kfull_shape)
    loss, grad_x, g_all, sg = _local_step(x[0], loss_target[0], wts, sp)

    g5 = g_all.reshape(NCHIP, 2, HALF, WCOLS)
    pos_c = jnp.reshape(ci, (1,)).astype(jnp.int32)
    pos_sc = jnp.stack([chip, ci]).astype(jnp.int32)
    got = _pair_exchange(g5)
    pb = _pair_sum(pos_c, g5, got)
    peers = _chip_exchange(pb)
    gsum = _chip_sum(pos_sc, g5, got, peers)
    gfull = _pair_share(gsum).reshape(D, WCOLS)

    shapes = {n: sg[n].shape for n in SMALL + ["conv_k"]}
    small = _unpack_small(_allreduce_small(_pack_small(sg, SMALL_ROWS), "allreduce_small"), shapes)

    out = {}
    for n in BIG:
        off, width = OFF[SLAB[n]], _slab_width(n)
        w2 = w[n][0]
        if n in TRANSPOSED:
            g2 = gfull[:, off:off + w2.shape[0]].T
            res = _adamw(w2, g2, m[n][0], v[n][0], "adamw_" + n)
        else:
            res = _adamw(w2, gfull, m[n][0], v[n][0], "adamw_" + n, g_block=(width, off // width))
        out[n] = [r[None] for r in res]
    for n in SMALL:
        res = _adamw(w[n], small[n], m[n], v[n], "adamw_" + n)
        out[n] = list(res)
    gk = lax.dynamic_slice(small["conv_k"], (0, chip * (D // NCHIP)), (CONV_W, D // NCHIP))
    res = _adamw(w["conv_k"][0, :, 0, :], gk, m["conv_k"][0, :, 0, :], v["conv_k"][0, :, 0, :], "adamw_conv_k")
    out["conv_k"] = [r[None, :, None, :] for r in res]

    total = lax.psum(loss[0, 0], ("x", "y", "c"))
    grads = [out[n][0] for n in ORDER]
    deltas = [out[n][1] for n in ORDER]
    new_m = [out[n][2] for n in ORDER]
    new_v = [out[n][3] for n in ORDER]
    return (total, grad_x[None], *grads, *deltas, *new_m, *new_v)
```

```python
import functools
import math

import numpy as np
import jax
import jax.numpy as jnp
from jax import lax
from jax.experimental import pallas as pl
from jax.experimental.pallas import tpu as pltpu
from jax.experimental.pallas import tpu_sc as plsc

F32 = jnp.float32
BF16 = jnp.bfloat16

D = 1024
FS = 704
FSP = 768
FP = 4 * FSP
H = 8
DK = 128
DV = 256
CH = 128
VW = H * DV
INW = 10240
INS = INW // 4
CONV_W = 31
HALO = 32
EPS = 1e-5
ALPHA = 2.0 ** 0.25
ROPE_BASE = 10000.0
NCHIP = 4

ADAM_LR, ADAM_B1, ADAM_B2, ADAM_EPS, ADAM_WD, ADAM_STEP = 0.001, 0.9, 0.999, 1e-08, 0.01, 10

OFF = {"w_in": 0, "w_ret_o": 2560, "g1": 3072, "u1": 3840, "d1": 4608,
       "g2": 5376, "u2": 6144, "d2": 6912, "w_conv_o": 7680, "w_out": 7936}
WCOLS = 8192
WIDTH = {"w_in": INS, "w_ret_o": VW // NCHIP, "w_conv_o": D // NCHIP, "w_out": D // NCHIP,
         "g1": FSP, "u1": FSP, "d1": FSP, "g2": FSP, "u2": FSP, "d2": FSP}
GROUPS = (("g1", "u1"), ("d1",), ("w_in",), ("w_ret_o", "w_conv_o", "w_out"), ("g2", "u2", "d2"))
LOC = {}
for _gi, _keys in enumerate(GROUPS):
    _off = 0
    for _k in _keys:
        LOC[_k] = (_gi, _off)
        _off += WIDTH[_k]
GCOLS = [sum(WIDTH[k] for k in keys) for keys in GROUPS]
VMEM_LIMIT = 56 << 20


def _cp(*sem, **kw):
    return pltpu.CompilerParams(dimension_semantics=sem, vmem_limit_bytes=VMEM_LIMIT, **kw)


def _sig(x):
    return 1.0 / (1.0 + jnp.exp(-x))


def _dot(a, b):
    return jnp.dot(a, b, preferred_element_type=F32)


def _dot_nt(a, b):
    return lax.dot_general(a, b, (((1,), (1,)), ((), ())), preferred_element_type=F32)


def _ln_fwd(z, g, b):
    mu = jnp.mean(z, axis=-1, keepdims=True)
    xc = z - mu
    var = jnp.mean(xc * xc, axis=-1, keepdims=True)
    rstd = lax.rsqrt(var + EPS)
    xh = xc * rstd
    return xh * g + b, xh, rstd


def _ln_bwd(dy, xh, rstd, g):
    dxh = dy * g
    m1 = jnp.mean(dxh, axis=-1, keepdims=True)
    m2 = jnp.mean(dxh * xh, axis=-1, keepdims=True)
    return rstd * (dxh - m1 - xh * m2)


def _colsum(x):
    return jnp.sum(x, axis=0, keepdims=True)


def _acc_rows(ref, first, val):
    @pl.when(first)
    def _():
        ref[...] = val

    @pl.when(jnp.logical_not(first))
    def _():
        ref[...] += val


def _rope_tables(T):
    half = DK // 2
    freqs = ROPE_BASE ** (-np.arange(half, dtype=np.float32) / half)
    ang = (np.arange(T, dtype=np.float32)[:, None] * freqs[None, :]).astype(np.float32)
    cos, sin = np.cos(ang), np.sin(ang)
    return (jnp.asarray(np.concatenate([cos, cos], 1), F32),
            jnp.asarray(np.concatenate([-sin, sin], 1), F32))


def _decay_tables():
    h = np.arange(H, dtype=np.float64)
    log_g = np.log(1.0 - np.exp2(-5.0 - h))
    idx = np.arange(CH, dtype=np.float64)
    diff = idx[:, None] - idx[None, :]
    dm = np.where(diff[None] >= 0, np.exp(np.maximum(diff, 0.0)[None] * log_g[:, None, None]), 0.0)
    xi = np.exp((idx[None, :] + 1.0) * log_g[:, None])
    zeta = np.exp((CH - 1.0 - idx)[None, :] * log_g[:, None])
    cd = np.exp(CH * log_g)
    xi_t = np.broadcast_to(xi[:, :, None], (H, CH, DV))
    zeta_t = np.broadcast_to(zeta[:, :, None], (H, CH, DK))
    return (jnp.asarray(dm, F32), jnp.asarray(xi_t, F32), jnp.asarray(zeta_t, F32),
            [float(v) for v in cd])


def _cast_t(x):
    T = x.shape[0]
    tm = min(T, 512)

    def body(x_ref, xb_ref, xt_ref):
        v = x_ref[...]
        xb_ref[...] = v.astype(BF16)
        xt_ref[...] = v.T.astype(BF16)

    return pl.pallas_call(
        body, name="cast_t", grid=(T // tm,),
        in_specs=[pl.BlockSpec((tm, D), lambda i: (i, 0))],
        out_specs=[pl.BlockSpec((tm, D), lambda i: (i, 0)), pl.BlockSpec((D, tm), lambda i: (0, i))],
        out_shape=[jax.ShapeDtypeStruct((T, D), BF16), jax.ShapeDtypeStruct((D, T), BF16)],
        compiler_params=_cp("parallel"))(x)


def _ffn_up(xb, wall, og, ou, name):
    T = xb.shape[0]
    tm, tn = min(T, 1024), 256
    nps = FSP // tn

    def body(x_ref, wg_ref, wu_ref, a_ref, b_ref, h_ref):
        x = x_ref[...]
        a = _dot(x, wg_ref[...])
        b = _dot(x, wu_ref[...])
        a_ref[...] = a.astype(BF16)
        b_ref[...] = b.astype(BF16)
        h_ref[...] = (a * _sig(a) * b).astype(BF16)

    def wspec(off):
        return pl.BlockSpec((None, D, tn), lambda i, j: (j // nps, 0, off // tn + j % nps))

    ospec = pl.BlockSpec((tm, tn), lambda i, j: (i, j))
    return pl.pallas_call(
        body, name=name, grid=(T // tm, FP // tn),
        in_specs=[pl.BlockSpec((tm, D), lambda i, j: (i, 0)), wspec(og), wspec(ou)],
        out_specs=[ospec] * 3, out_shape=[jax.ShapeDtypeStruct((T, FP), BF16)] * 3,
        compiler_params=_cp("parallel", "parallel"))(xb, wall, wall)


def _proj_ln(hb, wall, off, res, g, b, coef, name, want_b=True):
    T, K = hb.shape
    ks = K // NCHIP
    tm = min(T, 256)

    def body(h_ref, w_ref, r_ref, g_ref, b_ref, z_ref, *rest):
        acc = _dot_nt(h_ref[:, 0:ks], w_ref[0])
        for s in range(1, NCHIP):
            acc += _dot_nt(h_ref[:, s * ks:(s + 1) * ks], w_ref[s])
        z = ALPHA * r_ref[...] + coef * acc
        z_ref[...] = z
        if want_b:
            y, _, _ = _ln_fwd(z, g_ref[...], b_ref[...])
            y_ref, yb_ref, yt_ref = rest
            y_ref[...] = y
            yb_ref[...] = y.astype(BF16)
            yt_ref[...] = y.T.astype(BF16)

    row = pl.BlockSpec((tm, D), lambda i: (i, 0))
    vec = pl.BlockSpec((1, D), lambda i: (0, 0))
    out_specs = [row]
    out_shape = [jax.ShapeDtypeStruct((T, D), F32)]
    if want_b:
        out_specs += [row, row, pl.BlockSpec((D, tm), lambda i: (0, i))]
        out_shape += [jax.ShapeDtypeStruct((T, D), F32), jax.ShapeDtypeStruct((T, D), BF16),
                      jax.ShapeDtypeStruct((D, T), BF16)]
    return pl.pallas_call(
        body, name=name, grid=(T // tm,),
        in_specs=[pl.BlockSpec((tm, K), lambda i: (i, 0)),
                  pl.BlockSpec((NCHIP, D, ks), lambda i: (0, 0, off // ks)), row, vec, vec],
        out_specs=out_specs, out_shape=out_shape,
        compiler_params=_cp("parallel"))(hb, wall, res, g, b)


def _inproj(xb, wall, off, b_in, cos_t, sin_t):
    T = xb.shape[0]
    tm, tn = min(T, 1024), 256
    nps = INS // tn
    nqk = 2 * D // tn

    def body(x_ref, w_ref, bias_ref, cos_ref, sin_ref, o_ref):
        j = pl.program_id(1)
        acc = _dot(x_ref[...], w_ref[...]) + bias_ref[...]

        @pl.when(j >= nqk)
        def _():
            o_ref[...] = acc.astype(BF16)

        @pl.when(j < nqk)
        def _():
            scale = jnp.where(j < nqk // 2, DK ** -0.5, 1.0).astype(F32)
            c = cos_ref[...]
            s = sin_ref[...]
            for hh in range(tn // DK):
                xh = acc[:, hh * DK:(hh + 1) * DK]
                o = (xh * c + pltpu.roll(xh, DK // 2, 1) * s) * scale
                o_ref[:, hh * DK:(hh + 1) * DK] = o.astype(BF16)

    return pl.pallas_call(
        body, name="inproj", grid=(T // tm, INW // tn),
        in_specs=[pl.BlockSpec((tm, D), lambda i, j: (i, 0)),
                  pl.BlockSpec((None, D, tn), lambda i, j: (j // nps, 0, off // tn + j % nps)),
                  pl.BlockSpec((1, tn), lambda i, j: (0, j)),
                  pl.BlockSpec((tm, DK), lambda i, j: (i, 0)),
                  pl.BlockSpec((tm, DK), lambda i, j: (i, 0))],
        out_specs=pl.BlockSpec((tm, tn), lambda i, j: (i, j)),
        out_shape=jax.ShapeDtypeStruct((T, INW), BF16),
        compiler_params=_cp("parallel", "parallel"))(xb, wall, b_in, cos_t, sin_t)


def _retention_fwd(proj, gn_g, dm_t, xi_t, zeta_t, cds):
    T = proj.shape[0]
    n = T // CH

    def body(q_ref, k_ref, v_ref, g_ref, gn_ref, dm_ref, xi_ref, zt_ref, r_ref, ri_ref, st_ref, state):
        @pl.when(pl.program_id(0) == 0)
        def _():
            state[...] = jnp.zeros_like(state)

        for h in range(H):
            q = q_ref[:, h * DK:(h + 1) * DK]
            k = k_ref[:, h * DK:(h + 1) * DK]
            v = v_ref[:, h * DV:(h + 1) * DV]
            rows = slice(h * DK, (h + 1) * DK)
            s_prev = state[rows, :]
            s_b = s_prev.astype(BF16)
            st_ref[rows, :] = s_b
            sc = _dot_nt(q, k) * dm_ref[h]
            r = _dot(sc.astype(BF16), v) + _dot(q, s_b) * xi_ref[h]
            kz = k.astype(F32) * zt_ref[h]
            state[rows, :] = cds[h] * s_prev + _dot(kz.T.astype(BF16), v)
            cols = slice(h * DV, (h + 1) * DV)
            r_ref[:, cols] = r
            mu = jnp.mean(r, axis=-1, keepdims=True)
            xc = r - mu
            var = jnp.mean(xc * xc, axis=-1, keepdims=True)
            y = xc * lax.rsqrt(var + EPS) * gn_ref[:, cols]
            g = g_ref[:, cols].astype(F32)
            ri_ref[:, cols] = (g * _sig(g) * y).astype(BF16)

    full3 = lambda shp: pl.BlockSpec(shp, lambda c: (0, 0, 0))
    return pl.pallas_call(
        body, name="retention_fwd", grid=(n,),
        in_specs=[pl.BlockSpec((CH, D), lambda c: (c, 0)),
                  pl.BlockSpec((CH, D), lambda c: (c, 1)),
                  pl.BlockSpec((CH, VW), lambda c: (c, 1)),
                  pl.BlockSpec((CH, VW), lambda c: (c, 2)),
                  pl.BlockSpec((1, VW), lambda c: (0, 0)),
                  full3((H, CH, CH)), full3((H, CH, DV)), full3((H, CH, DK))],
        out_specs=[pl.BlockSpec((CH, VW), lambda c: (c, 0)), pl.BlockSpec((CH, VW), lambda c: (c, 0)),
                   pl.BlockSpec((None, H * DK, DV), lambda c: (c, 0, 0))],
        out_shape=[jax.ShapeDtypeStruct((T, VW), F32), jax.ShapeDtypeStruct((T, VW), BF16),
                   jax.ShapeDtypeStruct((n, H * DK, DV), BF16)],
        scratch_shapes=[pltpu.VMEM((H * DK, DV), F32)],
        compiler_params=_cp("arbitrary"))(proj, proj, proj, proj, gn_g, dm_t, xi_t, zeta_t)


CONV_TT = 256
CONV_SB = 64
CONV_CB = 256


def _glu(a_ref, b_ref, rows=slice(None)):
    a = a_ref[rows, :].astype(F32)
    sb = _sig(b_ref[rows, :].astype(F32))
    return a, sb


def _conv_fwd(proj, conv_k, conv_b, ln_g, ln_b):
    T = proj.shape[0]
    tt = min(T, CONV_TT)
    ca, cb = 6 * D // D, 7 * D // D

    def body(a_ref, b_ref, pa_ref, pb_ref, k_ref, cb_ref, g_ref, bb_ref, u1_ref, u3_ref, win):
        i = pl.program_id(0)
        a, sb = _glu(a_ref, b_ref)
        win[HALO:, :] = a * sb
        pa, psb = _glu(pa_ref, pb_ref, slice(tt - HALO, tt))
        win[0:HALO, :] = jnp.where(i > 0, pa * psb, 0.0)
        for c0 in range(0, D, CONV_CB):
            cs = slice(c0, c0 + CONV_CB)
            for r0 in range(0, tt, CONV_SB):
                acc = jnp.zeros((CONV_SB, CONV_CB), F32)
                for w in range(CONV_W):
                    st = r0 + HALO - (CONV_W - 1) + w
                    acc += win[st:st + CONV_SB, cs] * k_ref[w:w + 1, cs]
                u1_ref[r0:r0 + CONV_SB, cs] = acc + cb_ref[:, cs]
        u2, _, _ = _ln_fwd(u1_ref[...], g_ref[...], bb_ref[...])
        u3_ref[...] = (u2 * _sig(u2)).astype(BF16)

    vec = pl.BlockSpec((1, D), lambda i: (0, 0))
    row = pl.BlockSpec((tt, D), lambda i: (i, 0))
    return pl.pallas_call(
        body, name="conv_fwd", grid=(T // tt,),
        in_specs=[pl.BlockSpec((tt, D), lambda i: (i, ca)), pl.BlockSpec((tt, D), lambda i: (i, cb)),
                  pl.BlockSpec((tt, D), lambda i: (jnp.maximum(i - 1, 0), ca)),
                  pl.BlockSpec((tt, D), lambda i: (jnp.maximum(i - 1, 0), cb)),
                  pl.BlockSpec((CONV_W, D), lambda i: (0, 0)), vec, vec, vec],
        out_specs=[row, row],
        out_shape=[jax.ShapeDtypeStruct((T, D), F32), jax.ShapeDtypeStruct((T, D), BF16)],
        scratch_shapes=[pltpu.VMEM((tt + HALO, D), F32)],
        compiler_params=_cp("parallel"))(proj, proj, proj, proj, conv_k, conv_b, ln_g, ln_b)


def _merge(ret_in, u3, proj, wall, off_r, off_c):
    T = ret_in.shape[0]
    tm = min(T, 512)
    kr, kc = VW // NCHIP, D // NCHIP

    def body(ri_ref, u3_ref, gr_ref, gc_ref, wr_ref, wc_ref, ro_ref, co_ref, m_ref):
        ro = _dot_nt(ri_ref[:, 0:kr], wr_ref[0])
        co = _dot_nt(u3_ref[:, 0:kc], wc_ref[0])
        for s in range(1, NCHIP):
            ro += _dot_nt(ri_ref[:, s * kr:(s + 1) * kr], wr_ref[s])
            co += _dot_nt(u3_ref[:, s * kc:(s + 1) * kc], wc_ref[s])
        ro_ref[...] = ro.astype(BF16)
        co_ref[...] = co.astype(BF16)
        m = _sig(gr_ref[...].astype(F32)) * ro + _sig(gc_ref[...].astype(F32)) * co
        m_ref[...] = m.astype(BF16)

    row = pl.BlockSpec((tm, D), lambda i: (i, 0))
    return pl.pallas_call(
        body, name="merge", grid=(T // tm,),
        in_specs=[pl.BlockSpec((tm, VW), lambda i: (i, 0)), row,
                  pl.BlockSpec((tm, D), lambda i: (i, 8)), pl.BlockSpec((tm, D), lambda i: (i, 9)),
                  pl.BlockSpec((NCHIP, D, kr), lambda i: (0, 0, off_r // kr)),
                  pl.BlockSpec((NCHIP, D, kc), lambda i: (0, 0, off_c // kc))],
        out_specs=[row] * 3, out_shape=[jax.ShapeDtypeStruct((T, D), BF16)] * 3,
        compiler_params=_cp("parallel"))(ret_in, u3, proj, proj, wall, wall)


def _loss_ln_bwd(z, g, b, target, coef):
    T = z.shape[0]
    tm = min(T, 256)
    nt = T // tm

    def body(z_ref, g_ref, b_ref, t_ref, loss_ref, dzb_ref, dzt_ref, dz_ref, dg_ref, db_ref, lacc):
        i = pl.program_id(0)
        gam = g_ref[...]
        y, xh, rstd = _ln_fwd(z_ref[...], gam, b_ref[...])
        e = y - t_ref[...]
        part = _colsum(e * e)
        _acc_rows(lacc, i == 0, part)
        dy = e * (1.0 / D)
        dz = _ln_bwd(dy, xh, rstd, gam)
        dz_ref[...] = dz
        dzc = coef * dz
        dzb_ref[...] = dzc.astype(BF16)
        dzt_ref[...] = dzc.T.astype(BF16)
        _acc_rows(dg_ref, i == 0, _colsum(dy * xh))
        _acc_rows(db_ref, i == 0, _colsum(dy))

        @pl.when(i == nt - 1)
        def _():
            loss_ref[...] = (0.5 / D) * jnp.sum(lacc[...], axis=1, keepdims=True)

    row = pl.BlockSpec((tm, D), lambda i: (i, 0))
    vec = pl.BlockSpec((1, D), lambda i: (0, 0))
    return pl.pallas_call(
        body, name="loss_ln_bwd", grid=(nt,),
        in_specs=[row, vec, vec, row],
        out_specs=[pl.BlockSpec((1, 1), lambda i: (0, 0)), row, pl.BlockSpec((D, tm), lambda i: (0, i)),
                   row, vec, vec],
        out_shape=[jax.ShapeDtypeStruct((1, 1), F32), jax.ShapeDtypeStruct((T, D), BF16),
                   jax.ShapeDtypeStruct((D, T), BF16), jax.ShapeDtypeStruct((T, D), F32),
                   jax.ShapeDtypeStruct((1, D), F32), jax.ShapeDtypeStruct((1, D), F32)],
        scratch_shapes=[pltpu.VMEM((1, D), F32)],
        compiler_params=_cp("arbitrary"))(z, g, b, target)


def _ffn_bwd_h(dfb, wall, od, a, b, name):
    T = dfb.shape[0]
    tm, tn = min(T, 1024), 256
    nps = FSP // tn

    def body(d_ref, w_ref, a_ref, b_ref, da_ref, db_ref):
        dh = _dot(d_ref[...], w_ref[...])
        a = a_ref[...].astype(F32)
        sg = _sig(a)
        da_ref[...] = (dh * b_ref[...].astype(F32) * (sg * (1.0 + a * (1.0 - sg)))).astype(BF16)
        db_ref[...] = (dh * a * sg).astype(BF16)

    ospec = pl.BlockSpec((tm, tn), lambda i, j: (i, j))
    return pl.pallas_call(
        body, name=name, grid=(T // tm, FP // tn),
        in_specs=[pl.BlockSpec((tm, D), lambda i, j: (i, 0)),
                  pl.BlockSpec((None, D, tn), lambda i, j: (j // nps, 0, od // tn + j % nps)),
                  ospec, ospec],
        out_specs=[ospec] * 2, out_shape=[jax.ShapeDtypeStruct((T, FP), BF16)] * 2,
        compiler_params=_cp("parallel", "parallel"))(dfb, wall, a, b)


def _dx_bwd(lhs, offs, tk, wall, dz_next, name, ln=None, colsum=False):
    T, K = lhs[0].shape
    tm = min(T, 512)
    nk = K // tk
    kps = nk // NCHIP
    nl = len(lhs)

    def body(*refs):
        l_refs = refs[:nl]
        w_refs = refs[nl:2 * nl]
        dzn_ref = refs[2 * nl]
        pos = 2 * nl + 1
        if ln is not None:
            z_ref, g_ref = refs[pos:pos + 2]
            pos += 2
        outs = refs[pos:-1]
        acc = refs[-1]
        i, k = pl.program_id(0), pl.program_id(1)
        part = _dot_nt(l_refs[0][...], w_refs[0][...])
        for l in range(1, nl):
            part += _dot_nt(l_refs[l][...], w_refs[l][...])
        _acc_rows(acc, k == 0, part)
        if colsum:
            cs_ref = outs[-1]
            val = _colsum(l_refs[0][...].astype(F32))
            col = pl.ds(pl.multiple_of(k * tk, tk), tk)

            @pl.when(i == 0)
            def _():
                cs_ref[:, col] = val

            @pl.when(i > 0)
            def _():
                cs_ref[:, col] += val

        @pl.when(k == nk - 1)
        def _():
            dy = acc[...] + ALPHA * dzn_ref[...]
            if ln is None:
                outs[0][...] = dy
            else:
                gam = g_ref[...]
                _, xh, rstd = _ln_fwd(z_ref[...], gam, 0.0)
                dz = _ln_bwd(dy, xh, rstd, gam)
                dzc = ln[2] * dz
                outs[0][...] = dzc.astype(BF16)
                outs[1][...] = dzc.T.astype(BF16)
                outs[2][...] = dz
                _acc_rows(outs[3], i == 0, _colsum(dy * xh))
                _acc_rows(outs[4], i == 0, _colsum(dy))

    row = pl.BlockSpec((tm, D), lambda i, k: (i, 0))
    vec = pl.BlockSpec((1, D), lambda i, k: (0, 0))
    in_specs = [pl.BlockSpec((tm, tk), lambda i, k: (i, k))] * nl
    in_specs += [pl.BlockSpec((None, D, tk), functools.partial(lambda o, i, k: (k // kps, 0, o // tk + k % kps), o))
                 for o in offs]
    in_specs += [row]
    args = list(lhs) + [wall] * nl + [dz_next]
    if ln is None:
        out_specs = [row]
        out_shape = [jax.ShapeDtypeStruct((T, D), F32)]
    else:
        in_specs += [row, vec]
        args += [ln[0], ln[1]]
        out_specs = [row, pl.BlockSpec((D, tm), lambda i, k: (0, i)), row, vec, vec]
        out_shape = [jax.ShapeDtypeStruct((T, D), BF16), jax.ShapeDtypeStruct((D, T), BF16),
                     jax.ShapeDtypeStruct((T, D), F32), jax.ShapeDtypeStruct((1, D), F32),
                     jax.ShapeDtypeStruct((1, D), F32)]
    if colsum:
        out_specs += [pl.BlockSpec((1, K), lambda i, k: (0, 0))]
        out_shape += [jax.ShapeDtypeStruct((1, K), F32)]
    return pl.pallas_call(
        body, name=name, grid=(T // tm, nk), in_specs=in_specs, out_specs=out_specs, out_shape=out_shape,
        scratch_shapes=[pltpu.VMEM((tm, D), F32)],
        compiler_params=_cp("arbitrary", "arbitrary"))(*args)


def _wgrad(lhs_t, rhs, key, name, g_all=None):
    T, N = rhs.shape
    tn = 256
    nps = N // NCHIP // tn
    off = LOC[key][1]
    cols = GCOLS[LOC[key][0]]

    def body(*refs):
        l_ref, r_ref, o_ref = refs[0], refs[1], refs[-1]
        o_ref[...] = _dot(l_ref[...], r_ref[...])

    in_specs = [pl.BlockSpec((D, T), lambda j: (0, 0)), pl.BlockSpec((T, tn), lambda j: (0, j))]
    args = [lhs_t, rhs]
    aliases = {}
    if g_all is not None:
        in_specs.append(pl.BlockSpec(memory_space=pl.ANY))
        args.append(g_all)
        aliases = {2: 0}
    return pl.pallas_call(
        body, name=name, grid=(N // tn,), in_specs=in_specs,
        out_specs=pl.BlockSpec((None, D, tn), lambda j: (j // nps, 0, off // tn + j % nps)),
        out_shape=jax.ShapeDtypeStruct((NCHIP, D, cols), F32),
        input_output_aliases=aliases,
        compiler_params=_cp("parallel"))(*args)


def _merge_bwd(dmb, wall, off, proj, ro, co):
    T = dmb.shape[0]
    tm = min(T, 512)
    ks = D // NCHIP

    def body(d_ref, w_ref, gr_ref, gc_ref, ro_ref, co_ref, dro_ref, drot_ref, dco_ref, dcot_ref, dp_ref):
        d = d_ref[...]
        dmg = jnp.concatenate([_dot(d, w_ref[s]) for s in range(NCHIP)], axis=1)
        sr = _sig(gr_ref[...].astype(F32))
        sc = _sig(gc_ref[...].astype(F32))
        dro = dmg * sr
        dco = dmg * sc
        dro_ref[...] = dro.astype(BF16)
        drot_ref[...] = dro.T.astype(BF16)
        dco_ref[...] = dco.astype(BF16)
        dcot_ref[...] = dco.T.astype(BF16)
        dp_ref[:, 0:D] = (dmg * ro_ref[...].astype(F32) * sr * (1.0 - sr)).astype(BF16)
        dp_ref[:, D:2 * D] = (dmg * co_ref[...].astype(F32) * sc * (1.0 - sc)).astype(BF16)

    row = pl.BlockSpec((tm, D), lambda i: (i, 0))
    col = pl.BlockSpec((D, tm), lambda i: (0, i))
    return pl.pallas_call(
        body, name="merge_bwd", grid=(T // tm,),
        in_specs=[row, pl.BlockSpec((NCHIP, D, ks), lambda i: (0, 0, off // ks)),
                  pl.BlockSpec((tm, D), lambda i: (i, 8)), pl.BlockSpec((tm, D), lambda i: (i, 9)), row, row],
        out_specs=[row, col, row, col, pl.BlockSpec((tm, 2 * D), lambda i: (i, 4))],
        out_shape=[jax.ShapeDtypeStruct((T, D), BF16), jax.ShapeDtypeStruct((D, T), BF16),
                   jax.ShapeDtypeStruct((T, D), BF16), jax.ShapeDtypeStruct((D, T), BF16),
                   jax.ShapeDtypeStruct((T, INW), BF16)],
        compiler_params=_cp("parallel"))(dmb, wall, proj, proj, ro, co)


def _reto_bwd(dro, wall, off, r, proj, gn_g, dproj):
    T = dro.shape[0]
    tm = min(T, 512)
    hps = H // NCHIP

    def body(d_ref, w_ref, r_ref, g_ref, gn_ref, _, dr_ref, dgn_ref, dp_ref):
        i = pl.program_id(1)
        dri = _dot(d_ref[...], w_ref[...])
        rr = r_ref[...]
        mu = jnp.mean(rr, axis=-1, keepdims=True)
        xc = rr - mu
        var = jnp.mean(xc * xc, axis=-1, keepdims=True)
        rstd = lax.rsqrt(var + EPS)
        rn = xc * rstd
        gn = gn_ref[...]
        g = g_ref[...].astype(F32)
        sg = _sig(g)
        dy = dri * (g * sg)
        dp_ref[...] = (dri * (rn * gn) * (sg * (1.0 + g * (1.0 - sg)))).astype(BF16)
        _acc_rows(dgn_ref, i == 0, _colsum(dy * rn))
        dr_ref[...] = _ln_bwd(dy, rn, rstd, gn).astype(BF16)

    return pl.pallas_call(
        body, name="reto_bwd", grid=(H, T // tm),
        in_specs=[pl.BlockSpec((tm, D), lambda j, i: (i, 0)),
                  pl.BlockSpec((None, D, DV), lambda j, i: (j // hps, 0, off // DV + j % hps)),
                  pl.BlockSpec((tm, DV), lambda j, i: (i, j)),
                  pl.BlockSpec((tm, DV), lambda j, i: (i, 2 * VW // DV + j)),
                  pl.BlockSpec((1, DV), lambda j, i: (0, j)),
                  pl.BlockSpec(memory_space=pl.ANY)],
        out_specs=[pl.BlockSpec((tm, DV), lambda j, i: (i, j)), pl.BlockSpec((1, DV), lambda j, i: (0, j)),
                   pl.BlockSpec((tm, DV), lambda j, i: (i, 2 * VW // DV + j))],
        out_shape=[jax.ShapeDtypeStruct((T, VW), BF16), jax.ShapeDtypeStruct((1, VW), F32),
                   jax.ShapeDtypeStruct((T, INW), BF16)],
        input_output_aliases={5: 2},
        compiler_params=_cp("arbitrary", "arbitrary"))(dro, wall, r, proj, gn_g, dproj)


def _retention_bwd(proj, dr, states, cos_t, sin_t, dm_t, xi_t, zeta_t, cds, dproj):
    T = proj.shape[0]
    n = T // CH
    scale = DK ** -0.5

    def body(q_ref, k_ref, v_ref, dr_ref, st_ref, cos_ref, sin_ref, dm_ref, xi_ref, zt_ref, _, dp_ref, ds):
        @pl.when(pl.program_id(0) == 0)
        def _():
            ds[...] = jnp.zeros_like(ds)

        cos = cos_ref[...]
        sin = sin_ref[...]

        def unrope(d):
            return d * cos + pltpu.roll(d * sin, DK // 2, 1)

        for h in range(H):
            q = q_ref[:, h * DK:(h + 1) * DK]
            k = k_ref[:, h * DK:(h + 1) * DK]
            v = v_ref[:, h * DV:(h + 1) * DV]
            d_r = dr_ref[:, h * DV:(h + 1) * DV]
            rows = slice(h * DK, (h + 1) * DK)
            s_b = st_ref[rows, :]
            dm = dm_ref[h]
            zt = zt_ref[h]
            sc = _dot_nt(q, k) * dm
            dsc = _dot_nt(d_r, v) * dm
            drx = (d_r.astype(F32) * xi_ref[h]).astype(BF16)
            ds_prev = ds[rows, :]
            ds_b = ds_prev.astype(BF16)
            kz = (k.astype(F32) * zt).astype(BF16)
            dq = _dot(dsc.astype(BF16), k) + _dot_nt(drx, s_b)
            dk = _dot(dsc.T.astype(BF16), q) + _dot_nt(v, ds_b) * zt
            dv = _dot(sc.T.astype(BF16), d_r) + _dot(kz, ds_b)
            ds[rows, :] = cds[h] * ds_prev + _dot(q.astype(F32).T.astype(BF16), drx)
            dp_ref[:, h * DK:(h + 1) * DK] = unrope(dq * scale).astype(BF16)
            dp_ref[:, D + h * DK:D + (h + 1) * DK] = unrope(dk).astype(BF16)
            dp_ref[:, 2 * D + h * DV:2 * D + (h + 1) * DV] = dv.astype(BF16)

    rv = lambda c: n - 1 - c
    full3 = lambda shp: pl.BlockSpec(shp, lambda c: (0, 0, 0))
    return pl.pallas_call(
        body, name="retention_bwd", grid=(n,),
        in_specs=[pl.BlockSpec((CH, D), lambda c: (rv(c), 0)),
                  pl.BlockSpec((CH, D), lambda c: (rv(c), 1)),
                  pl.BlockSpec((CH, VW), lambda c: (rv(c), 1)),
                  pl.BlockSpec((CH, VW), lambda c: (rv(c), 0)),
                  pl.BlockSpec((None, H * DK, DV), lambda c: (rv(c), 0, 0)),
                  pl.BlockSpec((CH, DK), lambda c: (rv(c), 0)),
                  pl.BlockSpec((CH, DK), lambda c: (rv(c), 0)),
                  full3((H, CH, CH)), full3((H, CH, DV)), full3((H, CH, DK)),
                  pl.BlockSpec(memory_space=pl.ANY)],
        out_specs=pl.BlockSpec((CH, 2 * D + VW), lambda c: (rv(c), 0)),
        out_shape=jax.ShapeDtypeStruct((T, INW), BF16),
        input_output_aliases={10: 0},
        scratch_shapes=[pltpu.VMEM((H * DK, DV), F32)],
        compiler_params=_cp("arbitrary"))(proj, proj, proj, dr, states, cos_t, sin_t, dm_t, xi_t, zeta_t, dproj)


def _convo_bwd(dco, wall, off, u1, ln_g, ln_b):
    T = dco.shape[0]
    tm = min(T, 512)
    ks = D // NCHIP

    def body(d_ref, w_ref, u1_ref, g_ref, b_ref, du1_ref, dg_ref, db_ref, dcb_ref):
        i = pl.program_id(0)
        d = d_ref[...]
        du3 = jnp.concatenate([_dot(d, w_ref[s]) for s in range(NCHIP)], axis=1)
        gam = g_ref[...]
        u2, xh, rstd = _ln_fwd(u1_ref[...], gam, b_ref[...])
        sg = _sig(u2)
        du2 = du3 * (sg * (1.0 + u2 * (1.0 - sg)))
        du1 = _ln_bwd(du2, xh, rstd, gam)
        du1_ref[...] = du1
        _acc_rows(dg_ref, i == 0, _colsum(du2 * xh))
        _acc_rows(db_ref, i == 0, _colsum(du2))
        _acc_rows(dcb_ref, i == 0, _colsum(du1))

    row = pl.BlockSpec((tm, D), lambda i: (i, 0))
    vec = pl.BlockSpec((1, D), lambda i: (0, 0))
    return pl.pallas_call(
        body, name="convo_bwd", grid=(T // tm,),
        in_specs=[row, pl.BlockSpec((NCHIP, D, ks), lambda i: (0, 0, off // ks)), row, vec, vec],
        out_specs=[row, vec, vec, vec],
        out_shape=[jax.ShapeDtypeStruct((T, D), F32)] + [jax.ShapeDtypeStruct((1, D), F32)] * 3,
        compiler_params=_cp("arbitrary"))(dco, wall, u1, ln_g, ln_b)


def _conv_bwd(du1, proj, conv_k, dproj):
    T = du1.shape[0]
    tt = min(T, CONV_TT)
    nt = T // tt
    ca, cb = 6, 7

    def body(d_ref, dn_ref, a_ref, b_ref, pa_ref, pb_ref, k_ref, _, dp_ref, dk_ref, win_u, win_d):
        i = pl.program_id(0)
        a, sb = _glu(a_ref, b_ref)
        win_u[HALO:, :] = a * sb
        pa, psb = _glu(pa_ref, pb_ref, slice(tt - HALO, tt))
        win_u[0:HALO, :] = jnp.where(i > 0, pa * psb, 0.0)
        win_d[0:tt, :] = d_ref[...]
        win_d[tt:, :] = jnp.where(i < nt - 1, dn_ref[0:HALO, :], 0.0)

        @pl.when(i == 0)
        def _():
            dk_ref[...] = jnp.zeros_like(dk_ref)

        for c0 in range(0, D, CONV_CB):
            cs = slice(c0, c0 + CONV_CB)
            for r0 in range(0, tt, CONV_SB):
                acc = jnp.zeros((CONV_SB, CONV_CB), F32)
                for w in range(CONV_W):
                    st = r0 + (CONV_W - 1) - w
                    acc += win_d[st:st + CONV_SB, cs] * k_ref[w:w + 1, cs]
                aa = a_ref[r0:r0 + CONV_SB, cs].astype(F32)
                ss = _sig(b_ref[r0:r0 + CONV_SB, cs].astype(F32))
                dp_ref[r0:r0 + CONV_SB, cs] = (acc * ss).astype(BF16)
                dp_ref[r0:r0 + CONV_SB, c0 + D:c0 + D + CONV_CB] = (acc * aa * ss * (1.0 - ss)).astype(BF16)
            for w in range(CONV_W):
                acc = jnp.zeros((CONV_SB, CONV_CB), F32)
                for r0 in range(0, tt, CONV_SB):
                    st = r0 + HALO - (CONV_W - 1) + w
                    acc += win_d[r0:r0 + CONV_SB, cs] * win_u[st:st + CONV_SB, cs]
                dk_ref[w:w + 1, cs] += _colsum(acc)

    blk = lambda f, c: pl.BlockSpec((tt, D), lambda i: (f(i), c))
    cur = lambda i: i
    prv = lambda i: jnp.maximum(i - 1, 0)
    nxt = lambda i: jnp.minimum(i + 1, nt - 1)
    return pl.pallas_call(
        body, name="conv_bwd", grid=(nt,),
        in_specs=[blk(cur, 0), blk(nxt, 0), blk(cur, ca), blk(cur, cb), blk(prv, ca), blk(prv, cb),
                  pl.BlockSpec((CONV_W, D), lambda i: (0, 0)), pl.BlockSpec(memory_space=pl.ANY)],
        out_specs=[pl.BlockSpec((tt, 2 * D), lambda i: (i, 3)), pl.BlockSpec((HALO, D), lambda i: (0, 0))],
        out_shape=[jax.ShapeDtypeStruct((T, INW), BF16), jax.ShapeDtypeStruct((HALO, D), F32)],
        input_output_aliases={7: 0},
        scratch_shapes=[pltpu.VMEM((tt + HALO, D), F32), pltpu.VMEM((tt + HALO, D), F32)],
        compiler_params=_cp("arbitrary"))(du1, du1, proj, proj, proj, proj, conv_k, dproj)


def _local_step(x, target, wts, sp, pos_c, pos_sc):
    T = x.shape[0]
    cos_t, sin_t = _rope_tables(T)
    dm_t, xi_t, zeta_t, cds = _decay_tables()
    wa = lambda key: wts[LOC[key][0]]
    wo = lambda key: LOC[key][1]

    xb, xt = _cast_t(x)
    a1, b1, h1 = _ffn_up(xb, wa("g1"), wo("g1"), wo("u1"), "ffn1_up")
    z1, x1, x1b, x1t = _proj_ln(h1, wa("d1"), wo("d1"), x, sp["ln1_g"], sp["ln1_b"], 0.5, "ffn1_down_ln")
    proj = _inproj(x1b, wa("w_in"), wo("w_in"), sp["b_in"], cos_t, sin_t)
    r, ret_in, states = _retention_fwd(proj, sp["ret_gn_g"], dm_t, xi_t, zeta_t, cds)
    u1, u3 = _conv_fwd(proj, sp["conv_k"], sp["conv_b"], sp["conv_ln_g"], sp["conv_ln_b"])
    ro, co, merged = _merge(ret_in, u3, proj, wa("w_ret_o"), wo("w_ret_o"), wo("w_conv_o"))
    z2, x2, x2b, x2t = _proj_ln(merged, wa("w_out"), wo("w_out"), x1, sp["ln2_g"], sp["ln2_b"], 1.0, "out_proj_ln")
    a2, b2, h2 = _ffn_up(x2b, wa("g2"), wo("g2"), wo("u2"), "ffn2_up")
    (z3,) = _proj_ln(h2, wa("d2"), wo("d2"), x2, sp["ln3_g"], sp["ln3_b"], 0.5, "ffn2_down", want_b=False)

    sg = {}
    rs = {}
    loss, df2b, df2t, dz3, sg["ln3_g"], sg["ln3_b"] = _loss_ln_bwd(z3, sp["ln3_g"], sp["ln3_b"], target, 0.5)
    da2, db2 = _ffn_bwd_h(df2b, wa("d2"), wo("d2"), a2, b2, "ffn2_bwd_h")
    g4 = _wgrad(df2t, h2, "d2", "wgrad_d2")
    g4 = _wgrad(x2t, da2, "g2", "wgrad_g2", g4)
    g4 = _wgrad(x2t, db2, "u2", "wgrad_u2", g4)
    rs[4] = _ReduceScatter(g4, 4, pos_c, pos_sc)
    dmb, dmt, dz2, sg["ln2_g"], sg["ln2_b"] = _dx_bwd(
        [da2, db2], [wo("g2"), wo("u2")], FSP, wa("g2"), dz3, "ffn2_dx_ln", ln=(z2, sp["ln2_g"], 1.0))
    rs[4].phase2()
    g3 = _wgrad(dmt, merged, "w_out", "wgrad_out")
    dro, drot, dco, dcot, dproj = _merge_bwd(dmb, wa("w_out"), wo("w_out"), proj, ro, co)
    g3 = _wgrad(drot, ret_in, "w_ret_o", "wgrad_ret_o", g3)
    g3 = _wgrad(dcot, u3, "w_conv_o", "wgrad_conv_o", g3)
    rs[3] = _ReduceScatter(g3, 3, pos_c, pos_sc)
    dr, sg["ret_gn_g"], dproj = _reto_bwd(dro, wa("w_ret_o"), wo("w_ret_o"), r, proj, sp["ret_gn_g"], dproj)
    rs[4].phase3()
    rs[3].phase2()
    dproj = _retention_bwd(proj, dr, states, cos_t, sin_t, dm_t, xi_t, zeta_t, cds, dproj)
    du1, sg["conv_ln_g"], sg["conv_ln_b"], sg["conv_b"] = _convo_bwd(
        dco, wa("w_conv_o"), wo("w_conv_o"), u1, sp["conv_ln_g"], sp["conv_ln_b"])
    dproj, dck = _conv_bwd(du1, proj, sp["conv_k"], dproj)
    sg["conv_k"] = dck[:CONV_W]
    rs[3].phase3()
    g2 = _wgrad(x1t, dproj, "w_in", "wgrad_in")
    rs[2] = _ReduceScatter(g2, 2, pos_c, pos_sc)
    df1b, df1t, dz1, sg["ln1_g"], sg["ln1_b"], sg["b_in"] = _dx_bwd(
        [dproj], [wo("w_in")], 512, wa("w_in"), dz2, "mixer_dx_ln", ln=(z1, sp["ln1_g"], 0.5), colsum=True)
    rs[2].phase2()
    da1, db1 = _ffn_bwd_h(df1b, wa("d1"), wo("d1"), a1, b1, "ffn1_bwd_h")
    g1 = _wgrad(df1t, h1, "d1", "wgrad_d1")
    rs[1] = _ReduceScatter(g1, 1, pos_c, pos_sc)
    g0 = _wgrad(xt, da1, "g1", "wgrad_g1")
    g0 = _wgrad(xt, db1, "u1", "wgrad_u1", g0)
    rs[0] = _ReduceScatter(g0, 0, pos_c, pos_sc)
    rs[2].phase3()
    rs[1].phase2()
    rs[0].phase2()
    (grad_x,) = _dx_bwd([da1, db1], [wo("g1"), wo("u1")], FSP, wa("g1"), dz1, "ffn1_dx")
    rs[1].phase3()
    rs[0].phase3()
    return loss, grad_x, [rs[gi].result() for gi in range(len(GROUPS))], sg


MESH = pl.DeviceIdType.MESH
ANY = pl.BlockSpec(memory_space=pl.ANY)
HALF = D // 2


def _place():
    x, y, c = lax.axis_index("x"), lax.axis_index("y"), lax.axis_index("c")
    chips = [(1 - x, y), (x, 1 - y), (1 - x, 1 - y)]
    return x, y, c, chips


GATHER_ID = 1


def _gather_weights(wloc, name):
    w_ref = jax.new_ref(wloc, memory_space=pltpu.MemorySpace.HBM)
    o_ref = jax.empty_ref(jax.ShapeDtypeStruct((NCHIP, 2, HALF, wloc.shape[-1]), BF16),
                          memory_space=pltpu.MemorySpace.HBM)
    dma = pltpu.SemaphoreType.DMA

    @pl.kernel(mesh=plsc.ScalarSubcoreMesh(axis_name="sc", num_cores=1), name=name,
               scratch_types=(dma(()), dma((3,)), dma((3,)), dma((3,)), dma((3,))),
               compiler_params=pltpu.CompilerParams(collective_id=GATHER_ID))
    def launch(lsem, s1, r1, s2, r2):
        x, y, c, chips = _place()
        me = 2 * x + y
        sib = (x, y, 1 - c)
        barrier = pltpu.get_barrier_semaphore()
        for peer in [sib] + [(px, py, c) for px, py in chips]:
            pl.semaphore_signal(barrier, inc=1, device_id=peer, device_id_type=MESH)
        pl.semaphore_wait(barrier, 4)
        mine = pltpu.make_async_copy(w_ref, o_ref.at[me], lsem)
        mine.start()

        def rc(src, dst, ss, rs, dev):
            return pltpu.make_async_remote_copy(src_ref=src, dst_ref=dst, send_sem=ss, recv_sem=rs,
                                                device_id=dev, device_id_type=MESH)

        first = [rc(w_ref.at[c], o_ref.at[me, c], s1.at[j], r1.at[j], (*chip, c)) for j, chip in enumerate(chips)]
        for cp in first:
            cp.start()
        passed = []
        for j, (px, py) in enumerate(chips):
            slot = o_ref.at[2 * px + py, c]
            rc(slot, slot, s1.at[j], r1.at[j], (px, py, c)).wait_recv()
            cp = rc(slot, slot, s2.at[j], r2.at[j], sib)
            cp.start()
            passed.append(cp)
        for j, (px, py) in enumerate(chips):
            slot = o_ref.at[2 * px + py, 1 - c]
            rc(slot, slot, s2.at[j], r2.at[j], sib).wait_recv()
        for cp in first + passed:
            cp.wait_send()
        mine.wait()

    launch()
    return o_ref[...]


PAIR_ID = 2
CHIP_ID = 3
HBM = pltpu.MemorySpace.HBM


def _sequencer(name, collective_id, n_sems):
    dma = pltpu.SemaphoreType.DMA
    return pl.kernel(mesh=plsc.ScalarSubcoreMesh(axis_name="sc", num_cores=1), name=name,
                     scratch_types=(dma((n_sems,)), dma((n_sems,))),
                     compiler_params=pltpu.CompilerParams(collective_id=collective_id))


def _handshake(peers):
    barrier = pltpu.get_barrier_semaphore()
    for peer in peers:
        pl.semaphore_signal(barrier, inc=1, device_id=peer, device_id_type=MESH)
    pl.semaphore_wait(barrier, len(peers))


def _pair_exchange(g5, name):
    _, _, hr, cols = g5.shape
    g_ref = jax.new_ref(g5, memory_space=HBM)
    o_ref = jax.empty_ref(jax.ShapeDtypeStruct((NCHIP, hr, cols), F32), memory_space=HBM)

    @_sequencer(name, PAIR_ID, NCHIP)
    def launch(ss, rs):
        x, y, c, _ = _place()
        sib = (x, y, 1 - c)
        _handshake([sib])
        cps = [pltpu.make_async_remote_copy(src_ref=g_ref.at[j, 1 - c], dst_ref=o_ref.at[j], send_sem=ss.at[j],
                                            recv_sem=rs.at[j], device_id=sib, device_id_type=MESH)
               for j in range(NCHIP)]
        for cp in cps:
            cp.start()
        for cp in cps:
            cp.wait()

    launch()
    return o_ref[...]


RS_TR = 128


def _pair_sum(pos, g5, got, name):
    _, _, hr, cols = g5.shape

    def body(pos_ref, g_ref, r_ref, o_ref):
        o_ref[...] = (g_ref[...] + r_ref[...]).astype(BF16)

    grid_spec = pltpu.PrefetchScalarGridSpec(
        num_scalar_prefetch=1, grid=(NCHIP, hr // RS_TR),
        in_specs=[pl.BlockSpec((None, None, RS_TR, cols), lambda j, i, p: (j, p[0], i, 0)),
                  pl.BlockSpec((None, RS_TR, cols), lambda j, i, p: (j, i, 0))],
        out_specs=pl.BlockSpec((None, RS_TR, cols), lambda j, i, p: (j, i, 0)))
    return pl.pallas_call(
        body, name=name, grid_spec=grid_spec,
        out_shape=jax.ShapeDtypeStruct((NCHIP, hr, cols), BF16),
        compiler_params=_cp("parallel", "parallel"))(pos, g5, got)


def _chip_exchange(pb, name):
    _, hr, cols = pb.shape
    p_ref = jax.new_ref(pb, memory_space=HBM)
    o_ref = jax.empty_ref(jax.ShapeDtypeStruct((3, hr, cols), BF16), memory_space=HBM)

    @_sequencer(name, CHIP_ID, 3)
    def launch(ss, rs):
        x, y, c, chips = _place()
        _handshake([(px, py, c) for px, py in chips])
        cps = [pltpu.make_async_remote_copy(src_ref=p_ref.at[2 * px + py], dst_ref=o_ref.at[j], send_sem=ss.at[j],
                                            recv_sem=rs.at[j], device_id=(px, py, c), device_id_type=MESH)
               for j, (px, py) in enumerate(chips)]
        for cp in cps:
            cp.start()
        for cp in cps:
            cp.wait()

    launch()
    return o_ref[...]


def _chip_sum(pos, g5, got, peers, name):
    _, _, hr, cols = g5.shape

    def body(pos_ref, g_ref, r_ref, p_ref, o_ref):
        acc = g_ref[...] + r_ref[...]
        for j in range(3):
            acc += p_ref[j].astype(F32)
        o_ref[...] = acc

    grid_spec = pltpu.PrefetchScalarGridSpec(
        num_scalar_prefetch=1, grid=(hr // RS_TR,),
        in_specs=[pl.BlockSpec((None, None, RS_TR, cols), lambda i, p: (p[0], p[1], i, 0)),
                  pl.BlockSpec((None, RS_TR, cols), lambda i, p: (p[0], i, 0)),
                  pl.BlockSpec((3, RS_TR, cols), lambda i, p: (0, i, 0))],
        out_specs=pl.BlockSpec((None, RS_TR, cols), lambda i, p: (p[1], i, 0)))
    return pl.pallas_call(
        body, name=name, grid_spec=grid_spec,
        out_shape=jax.ShapeDtypeStruct((2, hr, cols), F32),
        compiler_params=_cp("parallel"))(pos, g5, got, peers)


def _pair_share(gsum, name):
    g_ref = jax.new_ref(gsum, memory_space=HBM)

    @_sequencer(name, PAIR_ID, 1)
    def launch(ss, rs):
        x, y, c, _ = _place()
        sib = (x, y, 1 - c)
        _handshake([sib])
        cp = pltpu.make_async_remote_copy(src_ref=g_ref.at[c], dst_ref=g_ref.at[c], send_sem=ss.at[0],
                                          recv_sem=rs.at[0], device_id=sib, device_id_type=MESH)
        cp.start()
        cp.wait_send()
        pltpu.make_async_remote_copy(src_ref=g_ref.at[1 - c], dst_ref=g_ref.at[1 - c], send_sem=ss.at[0],
                                     recv_sem=rs.at[0], device_id=sib, device_id_type=MESH).wait_recv()

    launch()
    return g_ref[...]


class _ReduceScatter:
    def __init__(self, g_arr, gi, pos_c, pos_sc):
        _, rows, cols = g_arr.shape
        self.g5 = g_arr.reshape(NCHIP, 2, rows // 2, cols)
        self.gi, self.pos_c, self.pos_sc = gi, pos_c, pos_sc
        self.got = _pair_exchange(self.g5, f"pair_exchange_{gi}")

    def phase2(self):
        pb = _pair_sum(self.pos_c, self.g5, self.got, f"pair_sum_{self.gi}")
        self.peers = _chip_exchange(pb, f"chip_exchange_{self.gi}")

    def phase3(self):
        gsum = _chip_sum(self.pos_sc, self.g5, self.got, self.peers, f"chip_sum_{self.gi}")
        self.full = _pair_share(gsum, f"pair_share_{self.gi}")

    def result(self):
        _, hr, cols = self.full.shape
        return self.full.reshape(2 * hr, cols)


SMALL_ROWS = 56


def _allreduce_small(vec, name):
    def body(v_ref, o_ref, buf, ss, rs):
        x, y, c, _ = _place()
        me = 4 * x + 2 * y + c
        buf[me] = v_ref[...]
        cps = []
        for m in range(1, 8):
            dev = (x ^ (m >> 2), y ^ ((m >> 1) & 1), c ^ (m & 1))
            cp = pltpu.make_async_remote_copy(src_ref=v_ref, dst_ref=buf.at[me], send_sem=ss.at[m - 1],
                                              recv_sem=rs.at[m - 1], device_id=dev, device_id_type=MESH)
            cp.start()
            cps.append(cp)
        for cp in cps:
            cp.wait()
        acc = buf[0]
        for d in range(1, 8):
            acc += buf[d]
        o_ref[...] = acc

    dma = pltpu.SemaphoreType.DMA
    vm = pl.BlockSpec(memory_space=pltpu.VMEM)
    return pl.pallas_call(
        body, name=name, in_specs=[vm], out_specs=vm,
        out_shape=jax.ShapeDtypeStruct((SMALL_ROWS, D), F32),
        scratch_shapes=[pltpu.VMEM((8, SMALL_ROWS, D), F32), dma((7,)), dma((7,))])(vec)


def _adamw_math(w, g, m, v):
    m2 = ADAM_B1 * m + (1.0 - ADAM_B1) * g
    v2 = ADAM_B2 * v + (1.0 - ADAM_B2) * (g * g)
    m_hat = m2 / (1.0 - ADAM_B1 ** ADAM_STEP)
    v_hat = v2 / (1.0 - ADAM_B2 ** ADAM_STEP)
    delta = -ADAM_LR * (m_hat / (jnp.sqrt(v_hat) + ADAM_EPS) + ADAM_WD * w)
    return delta, m2, v2


def _adamw(w, g, m, v, name, g_block=None):
    R, C = w.shape
    tr = R
    gw_hint = C if g_block is None else g_block[0]
    for cand in (256, 176, 128, 64, 32, 16, 8):
        if R % cand == 0 and cand * max(C, gw_hint) * 4 <= (1 << 20):
            tr = cand
            break
    gw, gi = (C, 0) if g_block is None else g_block

    def body(w_ref, g_ref, m_ref, v_ref, go_ref, d_ref, mo_ref, vo_ref):
        g = g_ref[:, 0:C]
        d, m2, v2 = _adamw_math(w_ref[...], g, m_ref[...], v_ref[...])
        go_ref[...] = g
        d_ref[...] = d
        mo_ref[...] = m2
        vo_ref[...] = v2

    spec = pl.BlockSpec((tr, C), lambda i: (i, 0))
    return pl.pallas_call(
        body, name=name, grid=(R // tr,),
        in_specs=[spec, pl.BlockSpec((tr, gw), lambda i: (i, gi)), spec, spec],
        out_specs=[spec] * 4, out_shape=[jax.ShapeDtypeStruct((R, C), F32)] * 4,
        compiler_params=_cp("parallel"))(w, g, m, v)


BIG = ["ffn1_w_gate", "ffn1_w_up", "ffn1_w_down", "w_in", "w_ret_o", "w_conv_o", "w_out",
       "ffn2_w_gate", "ffn2_w_up", "ffn2_w_down"]
SLAB = {"ffn1_w_gate": "g1", "ffn1_w_up": "u1", "ffn1_w_down": "d1", "w_in": "w_in", "w_ret_o": "w_ret_o",
        "w_conv_o": "w_conv_o", "w_out": "w_out", "ffn2_w_gate": "g2", "ffn2_w_up": "u2", "ffn2_w_down": "d2"}
TRANSPOSED = {"ffn1_w_down", "ffn2_w_down", "w_ret_o", "w_conv_o", "w_out"}
SMALL = ["ln1_g", "ln1_b", "ln2_g", "ln2_b", "ln3_g", "ln3_b", "conv_ln_g", "conv_ln_b", "conv_b",
         "ret_gn_g", "b_in"]
ORDER = ["ffn1_w_gate", "ffn1_w_up", "ffn1_w_down", "ln1_g", "ln1_b", "w_in", "b_in", "ret_gn_g", "conv_k",
         "conv_b", "conv_ln_g", "conv_ln_b", "w_ret_o", "w_conv_o", "w_out", "ln2_g", "ln2_b",
         "ffn2_w_gate", "ffn2_w_up", "ffn2_w_down", "ln3_g", "ln3_b"]


def _slab_width(name):
    return WIDTH[SLAB[name]]


def _pack_group(weights, keys):
    by_key = {SLAB[n]: n for n in BIG}
    parts = []
    for key in keys:
        w = weights[by_key[key]]
        w = w.T if by_key[key] in TRANSPOSED else w
        parts.append(jnp.pad(w, ((0, 0), (0, WIDTH[key] - w.shape[1]))))
    return jnp.concatenate(parts, axis=1).astype(BF16)


def _pack_small(vals, rows):
    flat = jnp.concatenate([vals[n].reshape(-1) for n in SMALL] + [vals["conv_k"].reshape(-1)])
    return jnp.pad(flat, (0, rows * D - flat.shape[0])).reshape(rows, D)


def _unpack_small(arr, shapes):
    flat = arr.reshape(-1)
    out, pos = {}, 0
    for n in SMALL + ["conv_k"]:
        size = int(np.prod(shapes[n]))
        out[n] = flat[pos:pos + size].reshape(shapes[n])
        pos += size
    return out


def kernel(x, ffn1_w_gate, ffn1_w_up, ffn1_w_down, ln1_g, ln1_b, w_in, b_in, ret_gn_g, conv_k, conv_b, conv_ln_g, conv_ln_b, w_ret_o, w_conv_o, w_out, ln2_g, ln2_b, ffn2_w_gate, ffn2_w_up, ffn2_w_down, ln3_g, ln3_b, loss_target, m_ffn1_w_gate, m_ffn1_w_up, m_ffn1_w_down, m_ln1_g, m_ln1_b, m_w_in, m_b_in, m_ret_gn_g, m_conv_k, m_conv_b, m_conv_ln_g, m_conv_ln_b, m_w_ret_o, m_w_conv_o, m_w_out, m_ln2_g, m_ln2_b, m_ffn2_w_gate, m_ffn2_w_up, m_ffn2_w_down, m_ln3_g, m_ln3_b, v_ffn1_w_gate, v_ffn1_w_up, v_ffn1_w_down, v_ln1_g, v_ln1_b, v_w_in, v_b_in, v_ret_gn_g, v_conv_k, v_conv_b, v_conv_ln_g, v_conv_ln_b, v_w_ret_o, v_w_conv_o, v_w_out, v_ln2_g, v_ln2_b, v_ffn2_w_gate, v_ffn2_w_up, v_ffn2_w_down, v_ln3_g, v_ln3_b):
    args = dict(locals())
    w = {n: args[n] for n in ORDER}
    m = {n: args["m_" + n] for n in ORDER}
    v = {n: args["v_" + n] for n in ORDER}
    xi, yi, ci = lax.axis_index("x"), lax.axis_index("y"), lax.axis_index("c")
    chip = 2 * xi + yi

    shards = {n: w[n][0] for n in BIG}
    wts = []
    for gi, keys in enumerate(GROUPS):
        slab = _pack_group(shards, keys)
        cols = slab.shape[1]
        wts.append(_gather_weights(slab.reshape(2, HALF, cols), f"gather_{gi}").reshape(NCHIP, D, cols))

    sp = {n: w[n] for n in SMALL}
    sp["conv_k"] = None
    kfull_shape = (CONV_W, D)
    kpad = jnp.zeros(kfull_shape, F32)
    kpad = lax.dynamic_update_slice(kpad, w["conv_k"][0, :, 0, :] * jnp.where(ci == 0, 1.0, 0.0), (0, chip * (D // NCHIP)))
    kvec = jnp.pad(kpad.reshape(-1), (0, SMALL_ROWS * D - CONV_W * D)).reshape(SMALL_ROWS, D)
    sp["conv_k"] = _allreduce_small(kvec, "gather_conv_k").reshape(-1)[:CONV_W * D].reshape(kfull_shape)
    pos_c = jnp.reshape(ci, (1,)).astype(jnp.int32)
    pos_sc = jnp.stack([chip, ci]).astype(jnp.int32)
    loss, grad_x, gfull, sg = _local_step(x[0], loss_target[0], wts, sp, pos_c, pos_sc)

    shapes = {n: sg[n].shape for n in SMALL + ["conv_k"]}
    small = _unpack_small(_allreduce_small(_pack_small(sg, SMALL_ROWS), "allreduce_small"), shapes)

    out = {}
    for n in BIG:
        (gi, off), width = LOC[SLAB[n]], _slab_width(n)
        w2 = w[n][0]
        if n in TRANSPOSED:
            g2 = gfull[gi][:, off:off + w2.shape[0]].T
            res = _adamw(w2, g2, m[n][0], v[n][0], "adamw_" + n)
        else:
            res = _adamw(w2, gfull[gi], m[n][0], v[n][0], "adamw_" + n, g_block=(width, off // width))
        out[n] = [r[None] for r in res]
    for n in SMALL:
        res = _adamw(w[n], small[n], m[n], v[n], "adamw_" + n)
        out[n] = list(res)
    gk = lax.dynamic_slice(small["conv_k"], (0, chip * (D // NCHIP)), (CONV_W, D // NCHIP))
    res = _adamw(w["conv_k"][0, :, 0, :], gk, m["conv_k"][0, :, 0, :], v["conv_k"][0, :, 0, :], "adamw_conv_k")
    out["conv_k"] = [r[None, :, None, :] for r in res]

    total = lax.psum(loss[0, 0], ("x", "y", "c"))
    grads = [out[n][0] for n in ORDER]
    deltas = [out[n][1] for n in ORDER]
    new_m = [out[n][2] for n in ORDER]
    new_v = [out[n][3] for n in ORDER]
    return (total, grad_x[None], *grads, *deltas, *new_m, *new_v)
```

```python
import dataclasses
import functools

import numpy as np
import jax
import jax.numpy as jnp
from jax import lax
from jax.experimental import pallas as pl
from jax.experimental.pallas import tpu as pltpu
from jax.experimental.pallas import tpu_sc as plsc

F32 = jnp.float32
BF16 = jnp.bfloat16

D = 1024
FS = 704
FSP = 768
FP = 4 * FSP
H = 8
DK = 128
DV = 256
CH = 128
VW = H * DV
INW = 10240
INS = INW // 4
CONV_W = 31
HALO = 32
EPS = 1e-5
ALPHA = 2.0 ** 0.25
ROPE_BASE = 10000.0
NCHIP = 4

ADAM_LR, ADAM_B1, ADAM_B2, ADAM_EPS, ADAM_WD, ADAM_STEP = 0.001, 0.9, 0.999, 1e-08, 0.01, 10

OFF = {"w_in": 0, "w_ret_o": 2560, "g1": 3072, "u1": 3840, "d1": 4608,
       "g2": 5376, "u2": 6144, "d2": 6912, "w_conv_o": 7680, "w_out": 7936}
WCOLS = 8192
WIDTH = {"w_in": INS, "w_ret_o": VW // NCHIP, "w_conv_o": D // NCHIP, "w_out": D // NCHIP,
         "g1": FSP, "u1": FSP, "d1": FSP, "g2": FSP, "u2": FSP, "d2": FSP}
GROUPS = (("g1", "u1"), ("d1",), ("w_in",), ("w_ret_o", "w_conv_o", "w_out"), ("g2", "u2", "d2"))
LOC = {}
for _gi, _keys in enumerate(GROUPS):
    _off = 0
    for _k in _keys:
        LOC[_k] = (_gi, _off)
        _off += WIDTH[_k]
GCOLS = [sum(WIDTH[k] for k in keys) for keys in GROUPS]
VMEM_LIMIT = 56 << 20


def _cp(*sem, **kw):
    return pltpu.CompilerParams(dimension_semantics=sem, vmem_limit_bytes=VMEM_LIMIT, **kw)


class _ProgramOrder:
    def __init__(self):
        self.active = False
        self.token = None


_ORDER = _ProgramOrder()


def _pcall(body, *, in_specs, scalar_prefetch=0, **kw):
    def call(*args):
        dep = _ORDER.token if _ORDER.active else None
        specs, fn = list(in_specs), body
        if dep is not None:
            n = len(args)

            def fn(*refs):
                return body(*refs[:n], *refs[n + 1:])

            specs.append(pl.BlockSpec(memory_space=pl.ANY))
            args = (*args, dep)
        params = dict(kw)
        if scalar_prefetch:
            params["grid_spec"] = pltpu.PrefetchScalarGridSpec(
                num_scalar_prefetch=scalar_prefetch, grid=params.pop("grid"), in_specs=specs,
                out_specs=params.pop("out_specs"))
        else:
            params["in_specs"] = specs
        out = pl.pallas_call(fn, **params)(*args)
        if _ORDER.active:
            _ORDER.token = jax.tree.leaves(out)[-1]
        return out

    return call


def _sig(x):
    return 1.0 / (1.0 + jnp.exp(-x))


def _dot(a, b):
    return jnp.dot(a, b, preferred_element_type=F32)


def _dot_nt(a, b):
    return lax.dot_general(a, b, (((1,), (1,)), ((), ())), preferred_element_type=F32)


def _ln_fwd(z, g, b):
    mu = jnp.mean(z, axis=-1, keepdims=True)
    xc = z - mu
    var = jnp.mean(xc * xc, axis=-1, keepdims=True)
    rstd = lax.rsqrt(var + EPS)
    xh = xc * rstd
    return xh * g + b, xh, rstd


def _ln_bwd(dy, xh, rstd, g):
    dxh = dy * g
    m1 = jnp.mean(dxh, axis=-1, keepdims=True)
    m2 = jnp.mean(dxh * xh, axis=-1, keepdims=True)
    return rstd * (dxh - m1 - xh * m2)


def _colsum(x):
    return jnp.sum(x, axis=0, keepdims=True)


def _acc_rows(ref, first, val):
    @pl.when(first)
    def _():
        ref[...] = val

    @pl.when(jnp.logical_not(first))
    def _():
        ref[...] += val


def _rope_tables(T):
    half = DK // 2
    freqs = ROPE_BASE ** (-np.arange(half, dtype=np.float32) / half)
    ang = (np.arange(T, dtype=np.float32)[:, None] * freqs[None, :]).astype(np.float32)
    cos, sin = np.cos(ang), np.sin(ang)
    return (jnp.asarray(np.concatenate([cos, cos], 1), F32),
            jnp.asarray(np.concatenate([-sin, sin], 1), F32))


def _decay_tables():
    h = np.arange(H, dtype=np.float64)
    log_g = np.log(1.0 - np.exp2(-5.0 - h))
    idx = np.arange(CH, dtype=np.float64)
    diff = idx[:, None] - idx[None, :]
    dm = np.where(diff[None] >= 0, np.exp(np.maximum(diff, 0.0)[None] * log_g[:, None, None]), 0.0)
    xi = np.exp((idx[None, :] + 1.0) * log_g[:, None])
    zeta = np.exp((CH - 1.0 - idx)[None, :] * log_g[:, None])
    cd = np.exp(CH * log_g)
    xi_t = np.broadcast_to(xi[:, :, None], (H, CH, DV))
    zeta_t = np.broadcast_to(zeta[:, :, None], (H, CH, DK))
    return (jnp.asarray(dm, F32), jnp.asarray(xi_t, F32), jnp.asarray(zeta_t, F32),
            [float(v) for v in cd])


def _cast_t(x):
    T = x.shape[0]
    tm = min(T, 512)

    def body(x_ref, xb_ref, xt_ref):
        v = x_ref[...]
        xb_ref[...] = v.astype(BF16)
        xt_ref[...] = v.T.astype(BF16)

    return _pcall(
        body, name="cast_t", grid=(T // tm,),
        in_specs=[pl.BlockSpec((tm, D), lambda i: (i, 0))],
        out_specs=[pl.BlockSpec((tm, D), lambda i: (i, 0)), pl.BlockSpec((D, tm), lambda i: (0, i))],
        out_shape=[jax.ShapeDtypeStruct((T, D), BF16), jax.ShapeDtypeStruct((D, T), BF16)],
        compiler_params=_cp("parallel"))(x)


def _ffn_up(xb, wall, og, ou, name):
    T = xb.shape[0]
    tm, tn = min(T, 1024), 256
    nps = FSP // tn

    def body(x_ref, wg_ref, wu_ref, a_ref, b_ref, h_ref):
        x = x_ref[...]
        a = _dot(x, wg_ref[...])
        b = _dot(x, wu_ref[...])
        a_ref[...] = a.astype(BF16)
        b_ref[...] = b.astype(BF16)
        h_ref[...] = (a * _sig(a) * b).astype(BF16)

    def wspec(off):
        return pl.BlockSpec((None, D, tn), lambda i, j: (j // nps, 0, off // tn + j % nps))

    ospec = pl.BlockSpec((tm, tn), lambda i, j: (i, j))
    return _pcall(
        body, name=name, grid=(T // tm, FP // tn),
        in_specs=[pl.BlockSpec((tm, D), lambda i, j: (i, 0)), wspec(og), wspec(ou)],
        out_specs=[ospec] * 3, out_shape=[jax.ShapeDtypeStruct((T, FP), BF16)] * 3,
        compiler_params=_cp("parallel", "parallel"))(xb, wall, wall)


def _proj_ln(hb, wall, off, res, g, b, coef, name, want_b=True):
    T, K = hb.shape
    ks = K // NCHIP
    tm = min(T, 256)

    def body(h_ref, w_ref, r_ref, g_ref, b_ref, z_ref, *rest):
        acc = _dot_nt(h_ref[:, 0:ks], w_ref[0])
        for s in range(1, NCHIP):
            acc += _dot_nt(h_ref[:, s * ks:(s + 1) * ks], w_ref[s])
        z = ALPHA * r_ref[...] + coef * acc
        z_ref[...] = z
        if want_b:
            y, _, _ = _ln_fwd(z, g_ref[...], b_ref[...])
            y_ref, yb_ref, yt_ref = rest
            y_ref[...] = y
            yb_ref[...] = y.astype(BF16)
            yt_ref[...] = y.T.astype(BF16)

    row = pl.BlockSpec((tm, D), lambda i: (i, 0))
    vec = pl.BlockSpec((1, D), lambda i: (0, 0))
    out_specs = [row]
    out_shape = [jax.ShapeDtypeStruct((T, D), F32)]
    if want_b:
        out_specs += [row, row, pl.BlockSpec((D, tm), lambda i: (0, i))]
        out_shape += [jax.ShapeDtypeStruct((T, D), F32), jax.ShapeDtypeStruct((T, D), BF16),
                      jax.ShapeDtypeStruct((D, T), BF16)]
    return _pcall(
        body, name=name, grid=(T // tm,),
        in_specs=[pl.BlockSpec((tm, K), lambda i: (i, 0)),
                  pl.BlockSpec((NCHIP, D, ks), lambda i: (0, 0, off // ks)), row, vec, vec],
        out_specs=out_specs, out_shape=out_shape,
        compiler_params=_cp("parallel"))(hb, wall, res, g, b)


def _inproj(xb, wall, off, b_in, cos_t, sin_t):
    T = xb.shape[0]
    tm, tn = min(T, 1024), 256
    nps = INS // tn
    nqk = 2 * D // tn

    def body(x_ref, w_ref, bias_ref, cos_ref, sin_ref, o_ref):
        j = pl.program_id(1)
        acc = _dot(x_ref[...], w_ref[...]) + bias_ref[...]

        @pl.when(j >= nqk)
        def _():
            o_ref[...] = acc.astype(BF16)

        @pl.when(j < nqk)
        def _():
            scale = jnp.where(j < nqk // 2, DK ** -0.5, 1.0).astype(F32)
            c = cos_ref[...]
            s = sin_ref[...]
            for hh in range(tn // DK):
                xh = acc[:, hh * DK:(hh + 1) * DK]
                o = (xh * c + pltpu.roll(xh, DK // 2, 1) * s) * scale
                o_ref[:, hh * DK:(hh + 1) * DK] = o.astype(BF16)

    return _pcall(
        body, name="inproj", grid=(T // tm, INW // tn),
        in_specs=[pl.BlockSpec((tm, D), lambda i, j: (i, 0)),
                  pl.BlockSpec((None, D, tn), lambda i, j: (j // nps, 0, off // tn + j % nps)),
                  pl.BlockSpec((1, tn), lambda i, j: (0, j)),
                  pl.BlockSpec((tm, DK), lambda i, j: (i, 0)),
                  pl.BlockSpec((tm, DK), lambda i, j: (i, 0))],
        out_specs=pl.BlockSpec((tm, tn), lambda i, j: (i, j)),
        out_shape=jax.ShapeDtypeStruct((T, INW), BF16),
        compiler_params=_cp("parallel", "parallel"))(xb, wall, b_in, cos_t, sin_t)


def _retention_fwd(proj, gn_g, dm_t, xi_t, zeta_t, cds):
    T = proj.shape[0]
    n = T // CH

    def body(q_ref, k_ref, v_ref, g_ref, gn_ref, dm_ref, xi_ref, zt_ref, r_ref, ri_ref, st_ref, state):
        @pl.when(pl.program_id(0) == 0)
        def _():
            state[...] = jnp.zeros_like(state)

        for h in range(H):
            q = q_ref[:, h * DK:(h + 1) * DK]
            k = k_ref[:, h * DK:(h + 1) * DK]
            v = v_ref[:, h * DV:(h + 1) * DV]
            rows = slice(h * DK, (h + 1) * DK)
            s_prev = state[rows, :]
            s_b = s_prev.astype(BF16)
            st_ref[rows, :] = s_b
            sc = _dot_nt(q, k) * dm_ref[h]
            r = _dot(sc.astype(BF16), v) + _dot(q, s_b) * xi_ref[h]
            kz = k.astype(F32) * zt_ref[h]
            state[rows, :] = cds[h] * s_prev + _dot(kz.T.astype(BF16), v)
            cols = slice(h * DV, (h + 1) * DV)
            r_ref[:, cols] = r
            mu = jnp.mean(r, axis=-1, keepdims=True)
            xc = r - mu
            var = jnp.mean(xc * xc, axis=-1, keepdims=True)
            y = xc * lax.rsqrt(var + EPS) * gn_ref[:, cols]
            g = g_ref[:, cols].astype(F32)
            ri_ref[:, cols] = (g * _sig(g) * y).astype(BF16)

    full3 = lambda shp: pl.BlockSpec(shp, lambda c: (0, 0, 0))
    return _pcall(
        body, name="retention_fwd", grid=(n,),
        in_specs=[pl.BlockSpec((CH, D), lambda c: (c, 0)),
                  pl.BlockSpec((CH, D), lambda c: (c, 1)),
                  pl.BlockSpec((CH, VW), lambda c: (c, 1)),
                  pl.BlockSpec((CH, VW), lambda c: (c, 2)),
                  pl.BlockSpec((1, VW), lambda c: (0, 0)),
                  full3((H, CH, CH)), full3((H, CH, DV)), full3((H, CH, DK))],
        out_specs=[pl.BlockSpec((CH, VW), lambda c: (c, 0)), pl.BlockSpec((CH, VW), lambda c: (c, 0)),
                   pl.BlockSpec((None, H * DK, DV), lambda c: (c, 0, 0))],
        out_shape=[jax.ShapeDtypeStruct((T, VW), F32), jax.ShapeDtypeStruct((T, VW), BF16),
                   jax.ShapeDtypeStruct((n, H * DK, DV), BF16)],
        scratch_shapes=[pltpu.VMEM((H * DK, DV), F32)],
        compiler_params=_cp("arbitrary"))(proj, proj, proj, proj, gn_g, dm_t, xi_t, zeta_t)


CONV_TT = 256
CONV_SB = 64
CONV_CB = 256


def _glu(a_ref, b_ref, rows=slice(None)):
    a = a_ref[rows, :].astype(F32)
    sb = _sig(b_ref[rows, :].astype(F32))
    return a, sb


def _conv_fwd(proj, conv_k, conv_b, ln_g, ln_b):
    T = proj.shape[0]
    tt = min(T, CONV_TT)
    ca, cb = 6 * D // D, 7 * D // D

    def body(a_ref, b_ref, pa_ref, pb_ref, k_ref, cb_ref, g_ref, bb_ref, u1_ref, u3_ref, win):
        i = pl.program_id(0)
        a, sb = _glu(a_ref, b_ref)
        win[HALO:, :] = a * sb
        pa, psb = _glu(pa_ref, pb_ref, slice(tt - HALO, tt))
        win[0:HALO, :] = jnp.where(i > 0, pa * psb, 0.0)
        for c0 in range(0, D, CONV_CB):
            cs = slice(c0, c0 + CONV_CB)
            for r0 in range(0, tt, CONV_SB):
                acc = jnp.zeros((CONV_SB, CONV_CB), F32)
                for w in range(CONV_W):
                    st = r0 + HALO - (CONV_W - 1) + w
                    acc += win[st:st + CONV_SB, cs] * k_ref[w:w + 1, cs]
                u1_ref[r0:r0 + CONV_SB, cs] = acc + cb_ref[:, cs]
        u2, _, _ = _ln_fwd(u1_ref[...], g_ref[...], bb_ref[...])
        u3_ref[...] = (u2 * _sig(u2)).astype(BF16)

    vec = pl.BlockSpec((1, D), lambda i: (0, 0))
    row = pl.BlockSpec((tt, D), lambda i: (i, 0))
    return _pcall(
        body, name="conv_fwd", grid=(T // tt,),
        in_specs=[pl.BlockSpec((tt, D), lambda i: (i, ca)), pl.BlockSpec((tt, D), lambda i: (i, cb)),
                  pl.BlockSpec((tt, D), lambda i: (jnp.maximum(i - 1, 0), ca)),
                  pl.BlockSpec((tt, D), lambda i: (jnp.maximum(i - 1, 0), cb)),
                  pl.BlockSpec((CONV_W, D), lambda i: (0, 0)), vec, vec, vec],
        out_specs=[row, row],
        out_shape=[jax.ShapeDtypeStruct((T, D), F32), jax.ShapeDtypeStruct((T, D), BF16)],
        scratch_shapes=[pltpu.VMEM((tt + HALO, D), F32)],
        compiler_params=_cp("parallel"))(proj, proj, proj, proj, conv_k, conv_b, ln_g, ln_b)


def _merge(ret_in, u3, proj, wall, off_r, off_c):
    T = ret_in.shape[0]
    tm = min(T, 512)
    kr, kc = VW // NCHIP, D // NCHIP

    def body(ri_ref, u3_ref, gr_ref, gc_ref, wr_ref, wc_ref, ro_ref, co_ref, m_ref):
        ro = _dot_nt(ri_ref[:, 0:kr], wr_ref[0])
        co = _dot_nt(u3_ref[:, 0:kc], wc_ref[0])
        for s in range(1, NCHIP):
            ro += _dot_nt(ri_ref[:, s * kr:(s + 1) * kr], wr_ref[s])
            co += _dot_nt(u3_ref[:, s * kc:(s + 1) * kc], wc_ref[s])
        ro_ref[...] = ro.astype(BF16)
        co_ref[...] = co.astype(BF16)
        m = _sig(gr_ref[...].astype(F32)) * ro + _sig(gc_ref[...].astype(F32)) * co
        m_ref[...] = m.astype(BF16)

    row = pl.BlockSpec((tm, D), lambda i: (i, 0))
    return _pcall(
        body, name="merge", grid=(T // tm,),
        in_specs=[pl.BlockSpec((tm, VW), lambda i: (i, 0)), row,
                  pl.BlockSpec((tm, D), lambda i: (i, 8)), pl.BlockSpec((tm, D), lambda i: (i, 9)),
                  pl.BlockSpec((NCHIP, D, kr), lambda i: (0, 0, off_r // kr)),
                  pl.BlockSpec((NCHIP, D, kc), lambda i: (0, 0, off_c // kc))],
        out_specs=[row] * 3, out_shape=[jax.ShapeDtypeStruct((T, D), BF16)] * 3,
        compiler_params=_cp("parallel"))(ret_in, u3, proj, proj, wall, wall)


def _loss_ln_bwd(z, g, b, target, coef):
    T = z.shape[0]
    tm = min(T, 256)
    nt = T // tm

    def body(z_ref, g_ref, b_ref, t_ref, loss_ref, dzb_ref, dzt_ref, dz_ref, dg_ref, db_ref, lacc):
        i = pl.program_id(0)
        gam = g_ref[...]
        y, xh, rstd = _ln_fwd(z_ref[...], gam, b_ref[...])
        e = y - t_ref[...]
        part = _colsum(e * e)
        _acc_rows(lacc, i == 0, part)
        dy = e * (1.0 / D)
        dz = _ln_bwd(dy, xh, rstd, gam)
        dz_ref[...] = dz
        dzc = coef * dz
        dzb_ref[...] = dzc.astype(BF16)
        dzt_ref[...] = dzc.T.astype(BF16)
        _acc_rows(dg_ref, i == 0, _colsum(dy * xh))
        _acc_rows(db_ref, i == 0, _colsum(dy))

        @pl.when(i == nt - 1)
        def _():
            loss_ref[...] = (0.5 / D) * jnp.sum(lacc[...], axis=1, keepdims=True)

    row = pl.BlockSpec((tm, D), lambda i: (i, 0))
    vec = pl.BlockSpec((1, D), lambda i: (0, 0))
    return _pcall(
        body, name="loss_ln_bwd", grid=(nt,),
        in_specs=[row, vec, vec, row],
        out_specs=[pl.BlockSpec((1, 1), lambda i: (0, 0)), row, pl.BlockSpec((D, tm), lambda i: (0, i)),
                   row, vec, vec],
        out_shape=[jax.ShapeDtypeStruct((1, 1), F32), jax.ShapeDtypeStruct((T, D), BF16),
                   jax.ShapeDtypeStruct((D, T), BF16), jax.ShapeDtypeStruct((T, D), F32),
                   jax.ShapeDtypeStruct((1, D), F32), jax.ShapeDtypeStruct((1, D), F32)],
        scratch_shapes=[pltpu.VMEM((1, D), F32)],
        compiler_params=_cp("arbitrary"))(z, g, b, target)


def _ffn_bwd_h(dfb, wall, od, a, b, name):
    T = dfb.shape[0]
    tm, tn = min(T, 1024), 256
    nps = FSP // tn

    def body(d_ref, w_ref, a_ref, b_ref, da_ref, db_ref):
        dh = _dot(d_ref[...], w_ref[...])
        a = a_ref[...].astype(F32)
        sg = _sig(a)
        da_ref[...] = (dh * b_ref[...].astype(F32) * (sg * (1.0 + a * (1.0 - sg)))).astype(BF16)
        db_ref[...] = (dh * a * sg).astype(BF16)

    ospec = pl.BlockSpec((tm, tn), lambda i, j: (i, j))
    return _pcall(
        body, name=name, grid=(T // tm, FP // tn),
        in_specs=[pl.BlockSpec((tm, D), lambda i, j: (i, 0)),
                  pl.BlockSpec((None, D, tn), lambda i, j: (j // nps, 0, od // tn + j % nps)),
                  ospec, ospec],
        out_specs=[ospec] * 2, out_shape=[jax.ShapeDtypeStruct((T, FP), BF16)] * 2,
        compiler_params=_cp("parallel", "parallel"))(dfb, wall, a, b)


def _dx_bwd(lhs, offs, tk, wall, dz_next, name, ln=None, colsum=False):
    T, K = lhs[0].shape
    tm = min(T, 512)
    nk = K // tk
    kps = nk // NCHIP
    nl = len(lhs)

    def body(*refs):
        l_refs = refs[:nl]
        w_refs = refs[nl:2 * nl]
        dzn_ref = refs[2 * nl]
        pos = 2 * nl + 1
        if ln is not None:
            z_ref, g_ref = refs[pos:pos + 2]
            pos += 2
        outs = refs[pos:-1]
        acc = refs[-1]
        i, k = pl.program_id(0), pl.program_id(1)
        part = _dot_nt(l_refs[0][...], w_refs[0][...])
        for l in range(1, nl):
            part += _dot_nt(l_refs[l][...], w_refs[l][...])
        _acc_rows(acc, k == 0, part)
        if colsum:
            cs_ref = outs[-1]
            val = _colsum(l_refs[0][...].astype(F32))
            col = pl.ds(pl.multiple_of(k * tk, tk), tk)

            @pl.when(i == 0)
            def _():
                cs_ref[:, col] = val

            @pl.when(i > 0)
            def _():
                cs_ref[:, col] += val

        @pl.when(k == nk - 1)
        def _():
            dy = acc[...] + ALPHA * dzn_ref[...]
            if ln is None:
                outs[0][...] = dy
            else:
                gam = g_ref[...]
                _, xh, rstd = _ln_fwd(z_ref[...], gam, 0.0)
                dz = _ln_bwd(dy, xh, rstd, gam)
                dzc = ln[2] * dz
                outs[0][...] = dzc.astype(BF16)
                outs[1][...] = dzc.T.astype(BF16)
                outs[2][...] = dz
                _acc_rows(outs[3], i == 0, _colsum(dy * xh))
                _acc_rows(outs[4], i == 0, _colsum(dy))

    row = pl.BlockSpec((tm, D), lambda i, k: (i, 0))
    vec = pl.BlockSpec((1, D), lambda i, k: (0, 0))
    in_specs = [pl.BlockSpec((tm, tk), lambda i, k: (i, k))] * nl
    in_specs += [pl.BlockSpec((None, D, tk), functools.partial(lambda o, i, k: (k // kps, 0, o // tk + k % kps), o))
                 for o in offs]
    in_specs += [row]
    args = list(lhs) + [wall] * nl + [dz_next]
    if ln is None:
        out_specs = [row]
        out_shape = [jax.ShapeDtypeStruct((T, D), F32)]
    else:
        in_specs += [row, vec]
        args += [ln[0], ln[1]]
        out_specs = [row, pl.BlockSpec((D, tm), lambda i, k: (0, i)), row, vec, vec]
        out_shape = [jax.ShapeDtypeStruct((T, D), BF16), jax.ShapeDtypeStruct((D, T), BF16),
                     jax.ShapeDtypeStruct((T, D), F32), jax.ShapeDtypeStruct((1, D), F32),
                     jax.ShapeDtypeStruct((1, D), F32)]
    if colsum:
        out_specs += [pl.BlockSpec((1, K), lambda i, k: (0, 0))]
        out_shape += [jax.ShapeDtypeStruct((1, K), F32)]
    return _pcall(
        body, name=name, grid=(T // tm, nk), in_specs=in_specs, out_specs=out_specs, out_shape=out_shape,
        scratch_shapes=[pltpu.VMEM((tm, D), F32)],
        compiler_params=_cp("arbitrary", "arbitrary"))(*args)


def _wgrad(lhs_t, rhs, key, name, g_all=None):
    T, N = rhs.shape
    tn = 256
    nps = N // NCHIP // tn
    off = LOC[key][1]
    cols = GCOLS[LOC[key][0]]

    def body(*refs):
        l_ref, r_ref, o_ref = refs[0], refs[1], refs[-1]
        o_ref[...] = _dot(l_ref[...], r_ref[...])

    in_specs = [pl.BlockSpec((D, T), lambda j: (0, 0)), pl.BlockSpec((T, tn), lambda j: (0, j))]
    args = [lhs_t, rhs]
    aliases = {}
    if g_all is not None:
        in_specs.append(pl.BlockSpec(memory_space=pl.ANY))
        args.append(g_all)
        aliases = {2: 0}
    return _pcall(
        body, name=name, grid=(N // tn,), in_specs=in_specs,
        out_specs=pl.BlockSpec((None, D, tn), lambda j: (j // nps, 0, off // tn + j % nps)),
        out_shape=jax.ShapeDtypeStruct((NCHIP, D, cols), F32),
        input_output_aliases=aliases,
        compiler_params=_cp("parallel"))(*args)


def _merge_bwd(dmb, wall, off, proj, ro, co):
    T = dmb.shape[0]
    tm = min(T, 512)
    ks = D // NCHIP

    def body(d_ref, w_ref, gr_ref, gc_ref, ro_ref, co_ref, dro_ref, drot_ref, dco_ref, dcot_ref, dp_ref):
        d = d_ref[...]
        dmg = jnp.concatenate([_dot(d, w_ref[s]) for s in range(NCHIP)], axis=1)
        sr = _sig(gr_ref[...].astype(F32))
        sc = _sig(gc_ref[...].astype(F32))
        dro = dmg * sr
        dco = dmg * sc
        dro_ref[...] = dro.astype(BF16)
        drot_ref[...] = dro.T.astype(BF16)
        dco_ref[...] = dco.astype(BF16)
        dcot_ref[...] = dco.T.astype(BF16)
        dp_ref[:, 0:D] = (dmg * ro_ref[...].astype(F32) * sr * (1.0 - sr)).astype(BF16)
        dp_ref[:, D:2 * D] = (dmg * co_ref[...].astype(F32) * sc * (1.0 - sc)).astype(BF16)

    row = pl.BlockSpec((tm, D), lambda i: (i, 0))
    col = pl.BlockSpec((D, tm), lambda i: (0, i))
    return _pcall(
        body, name="merge_bwd", grid=(T // tm,),
        in_specs=[row, pl.BlockSpec((NCHIP, D, ks), lambda i: (0, 0, off // ks)),
                  pl.BlockSpec((tm, D), lambda i: (i, 8)), pl.BlockSpec((tm, D), lambda i: (i, 9)), row, row],
        out_specs=[row, col, row, col, pl.BlockSpec((tm, 2 * D), lambda i: (i, 4))],
        out_shape=[jax.ShapeDtypeStruct((T, D), BF16), jax.ShapeDtypeStruct((D, T), BF16),
                   jax.ShapeDtypeStruct((T, D), BF16), jax.ShapeDtypeStruct((D, T), BF16),
                   jax.ShapeDtypeStruct((T, INW), BF16)],
        compiler_params=_cp("parallel"))(dmb, wall, proj, proj, ro, co)


def _reto_bwd(dro, wall, off, r, proj, gn_g, dproj):
    T = dro.shape[0]
    tm = min(T, 512)
    hps = H // NCHIP

    def body(d_ref, w_ref, r_ref, g_ref, gn_ref, _, dr_ref, dgn_ref, dp_ref):
        i = pl.program_id(1)
        dri = _dot(d_ref[...], w_ref[...])
        rr = r_ref[...]
        mu = jnp.mean(rr, axis=-1, keepdims=True)
        xc = rr - mu
        var = jnp.mean(xc * xc, axis=-1, keepdims=True)
        rstd = lax.rsqrt(var + EPS)
        rn = xc * rstd
        gn = gn_ref[...]
        g = g_ref[...].astype(F32)
        sg = _sig(g)
        dy = dri * (g * sg)
        dp_ref[...] = (dri * (rn * gn) * (sg * (1.0 + g * (1.0 - sg)))).astype(BF16)
        _acc_rows(dgn_ref, i == 0, _colsum(dy * rn))
        dr_ref[...] = _ln_bwd(dy, rn, rstd, gn).astype(BF16)

    return _pcall(
        body, name="reto_bwd", grid=(H, T // tm),
        in_specs=[pl.BlockSpec((tm, D), lambda j, i: (i, 0)),
                  pl.BlockSpec((None, D, DV), lambda j, i: (j // hps, 0, off // DV + j % hps)),
                  pl.BlockSpec((tm, DV), lambda j, i: (i, j)),
                  pl.BlockSpec((tm, DV), lambda j, i: (i, 2 * VW // DV + j)),
                  pl.BlockSpec((1, DV), lambda j, i: (0, j)),
                  pl.BlockSpec(memory_space=pl.ANY)],
        out_specs=[pl.BlockSpec((tm, DV), lambda j, i: (i, j)), pl.BlockSpec((1, DV), lambda j, i: (0, j)),
                   pl.BlockSpec((tm, DV), lambda j, i: (i, 2 * VW // DV + j))],
        out_shape=[jax.ShapeDtypeStruct((T, VW), BF16), jax.ShapeDtypeStruct((1, VW), F32),
                   jax.ShapeDtypeStruct((T, INW), BF16)],
        input_output_aliases={5: 2},
        compiler_params=_cp("arbitrary", "arbitrary"))(dro, wall, r, proj, gn_g, dproj)


def _retention_bwd(proj, dr, states, cos_t, sin_t, dm_t, xi_t, zeta_t, cds, dproj):
    T = proj.shape[0]
    n = T // CH
    scale = DK ** -0.5

    def body(q_ref, k_ref, v_ref, dr_ref, st_ref, cos_ref, sin_ref, dm_ref, xi_ref, zt_ref, _, dp_ref, ds):
        @pl.when(pl.program_id(0) == 0)
        def _():
            ds[...] = jnp.zeros_like(ds)

        cos = cos_ref[...]
        sin = sin_ref[...]

        def unrope(d):
            return d * cos + pltpu.roll(d * sin, DK // 2, 1)

        for h in range(H):
            q = q_ref[:, h * DK:(h + 1) * DK]
            k = k_ref[:, h * DK:(h + 1) * DK]
            v = v_ref[:, h * DV:(h + 1) * DV]
            d_r = dr_ref[:, h * DV:(h + 1) * DV]
            rows = slice(h * DK, (h + 1) * DK)
            s_b = st_ref[rows, :]
            dm = dm_ref[h]
            zt = zt_ref[h]
            sc = _dot_nt(q, k) * dm
            dsc = _dot_nt(d_r, v) * dm
            drx = (d_r.astype(F32) * xi_ref[h]).astype(BF16)
            ds_prev = ds[rows, :]
            ds_b = ds_prev.astype(BF16)
            kz = (k.astype(F32) * zt).astype(BF16)
            dq = _dot(dsc.astype(BF16), k) + _dot_nt(drx, s_b)
            dk = _dot(dsc.T.astype(BF16), q) + _dot_nt(v, ds_b) * zt
            dv = _dot(sc.T.astype(BF16), d_r) + _dot(kz, ds_b)
            ds[rows, :] = cds[h] * ds_prev + _dot(q.astype(F32).T.astype(BF16), drx)
            dp_ref[:, h * DK:(h + 1) * DK] = unrope(dq * scale).astype(BF16)
            dp_ref[:, D + h * DK:D + (h + 1) * DK] = unrope(dk).astype(BF16)
            dp_ref[:, 2 * D + h * DV:2 * D + (h + 1) * DV] = dv.astype(BF16)

    rv = lambda c: n - 1 - c
    full3 = lambda shp: pl.BlockSpec(shp, lambda c: (0, 0, 0))
    return _pcall(
        body, name="retention_bwd", grid=(n,),
        in_specs=[pl.BlockSpec((CH, D), lambda c: (rv(c), 0)),
                  pl.BlockSpec((CH, D), lambda c: (rv(c), 1)),
                  pl.BlockSpec((CH, VW), lambda c: (rv(c), 1)),
                  pl.BlockSpec((CH, VW), lambda c: (rv(c), 0)),
                  pl.BlockSpec((None, H * DK, DV), lambda c: (rv(c), 0, 0)),
                  pl.BlockSpec((CH, DK), lambda c: (rv(c), 0)),
                  pl.BlockSpec((CH, DK), lambda c: (rv(c), 0)),
                  full3((H, CH, CH)), full3((H, CH, DV)), full3((H, CH, DK)),
                  pl.BlockSpec(memory_space=pl.ANY)],
        out_specs=pl.BlockSpec((CH, 2 * D + VW), lambda c: (rv(c), 0)),
        out_shape=jax.ShapeDtypeStruct((T, INW), BF16),
        input_output_aliases={10: 0},
        scratch_shapes=[pltpu.VMEM((H * DK, DV), F32)],
        compiler_params=_cp("arbitrary"))(proj, proj, proj, dr, states, cos_t, sin_t, dm_t, xi_t, zeta_t, dproj)


def _convo_bwd(dco, wall, off, u1, ln_g, ln_b):
    T = dco.shape[0]
    tm = min(T, 512)
    ks = D // NCHIP

    def body(d_ref, w_ref, u1_ref, g_ref, b_ref, du1_ref, dg_ref, db_ref, dcb_ref):
        i = pl.program_id(0)
        d = d_ref[...]
        du3 = jnp.concatenate([_dot(d, w_ref[s]) for s in range(NCHIP)], axis=1)
        gam = g_ref[...]
        u2, xh, rstd = _ln_fwd(u1_ref[...], gam, b_ref[...])
        sg = _sig(u2)
        du2 = du3 * (sg * (1.0 + u2 * (1.0 - sg)))
        du1 = _ln_bwd(du2, xh, rstd, gam)
        du1_ref[...] = du1
        _acc_rows(dg_ref, i == 0, _colsum(du2 * xh))
        _acc_rows(db_ref, i == 0, _colsum(du2))
        _acc_rows(dcb_ref, i == 0, _colsum(du1))

    row = pl.BlockSpec((tm, D), lambda i: (i, 0))
    vec = pl.BlockSpec((1, D), lambda i: (0, 0))
    return _pcall(
        body, name="convo_bwd", grid=(T // tm,),
        in_specs=[row, pl.BlockSpec((NCHIP, D, ks), lambda i: (0, 0, off // ks)), row, vec, vec],
        out_specs=[row, vec, vec, vec],
        out_shape=[jax.ShapeDtypeStruct((T, D), F32)] + [jax.ShapeDtypeStruct((1, D), F32)] * 3,
        compiler_params=_cp("arbitrary"))(dco, wall, u1, ln_g, ln_b)


def _conv_bwd(du1, proj, conv_k, dproj):
    T = du1.shape[0]
    tt = min(T, CONV_TT)
    nt = T // tt
    ca, cb = 6, 7

    def body(d_ref, dn_ref, a_ref, b_ref, pa_ref, pb_ref, k_ref, _, dp_ref, dk_ref, win_u, win_d):
        i = pl.program_id(0)
        a, sb = _glu(a_ref, b_ref)
        win_u[HALO:, :] = a * sb
        pa, psb = _glu(pa_ref, pb_ref, slice(tt - HALO, tt))
        win_u[0:HALO, :] = jnp.where(i > 0, pa * psb, 0.0)
        win_d[0:tt, :] = d_ref[...]
        win_d[tt:, :] = jnp.where(i < nt - 1, dn_ref[0:HALO, :], 0.0)

        @pl.when(i == 0)
        def _():
            dk_ref[...] = jnp.zeros_like(dk_ref)

        for c0 in range(0, D, CONV_CB):
            cs = slice(c0, c0 + CONV_CB)
            for r0 in range(0, tt, CONV_SB):
                acc = jnp.zeros((CONV_SB, CONV_CB), F32)
                for w in range(CONV_W):
                    st = r0 + (CONV_W - 1) - w
                    acc += win_d[st:st + CONV_SB, cs] * k_ref[w:w + 1, cs]
                aa = a_ref[r0:r0 + CONV_SB, cs].astype(F32)
                ss = _sig(b_ref[r0:r0 + CONV_SB, cs].astype(F32))
                dp_ref[r0:r0 + CONV_SB, cs] = (acc * ss).astype(BF16)
                dp_ref[r0:r0 + CONV_SB, c0 + D:c0 + D + CONV_CB] = (acc * aa * ss * (1.0 - ss)).astype(BF16)
            for w in range(CONV_W):
                acc = jnp.zeros((CONV_SB, CONV_CB), F32)
                for r0 in range(0, tt, CONV_SB):
                    st = r0 + HALO - (CONV_W - 1) + w
                    acc += win_d[r0:r0 + CONV_SB, cs] * win_u[st:st + CONV_SB, cs]
                dk_ref[w:w + 1, cs] += _colsum(acc)

    blk = lambda f, c: pl.BlockSpec((tt, D), lambda i: (f(i), c))
    cur = lambda i: i
    prv = lambda i: jnp.maximum(i - 1, 0)
    nxt = lambda i: jnp.minimum(i + 1, nt - 1)
    return _pcall(
        body, name="conv_bwd", grid=(nt,),
        in_specs=[blk(cur, 0), blk(nxt, 0), blk(cur, ca), blk(cur, cb), blk(prv, ca), blk(prv, cb),
                  pl.BlockSpec((CONV_W, D), lambda i: (0, 0)), pl.BlockSpec(memory_space=pl.ANY)],
        out_specs=[pl.BlockSpec((tt, 2 * D), lambda i: (i, 3)), pl.BlockSpec((HALO, D), lambda i: (0, 0))],
        out_shape=[jax.ShapeDtypeStruct((T, INW), BF16), jax.ShapeDtypeStruct((HALO, D), F32)],
        input_output_aliases={7: 0},
        scratch_shapes=[pltpu.VMEM((tt + HALO, D), F32), pltpu.VMEM((tt + HALO, D), F32)],
        compiler_params=_cp("arbitrary"))(du1, du1, proj, proj, proj, proj, conv_k, dproj)


def _local_step(x, target, wts, sp, pos_c, pos_sc, adam):
    T = x.shape[0]
    cos_t, sin_t = _rope_tables(T)
    dm_t, xi_t, zeta_t, cds = _decay_tables()
    wa = lambda key: wts[LOC[key][0]]
    wo = lambda key: LOC[key][1]
    _ORDER.active, _ORDER.token = True, None

    xb, xt = _cast_t(x)
    a1, b1, h1 = _ffn_up(xb, wa("g1"), wo("g1"), wo("u1"), "ffn1_up")
    z1, x1, x1b, x1t = _proj_ln(h1, wa("d1"), wo("d1"), x, sp["ln1_g"], sp["ln1_b"], 0.5, "ffn1_down_ln")
    proj = _inproj(x1b, wa("w_in"), wo("w_in"), sp["b_in"], cos_t, sin_t)
    r, ret_in, states = _retention_fwd(proj, sp["ret_gn_g"], dm_t, xi_t, zeta_t, cds)
    u1, u3 = _conv_fwd(proj, sp["conv_k"], sp["conv_b"], sp["conv_ln_g"], sp["conv_ln_b"])
    ro, co, merged = _merge(ret_in, u3, proj, wa("w_ret_o"), wo("w_ret_o"), wo("w_conv_o"))
    z2, x2, x2b, x2t = _proj_ln(merged, wa("w_out"), wo("w_out"), x1, sp["ln2_g"], sp["ln2_b"], 1.0, "out_proj_ln")
    a2, b2, h2 = _ffn_up(x2b, wa("g2"), wo("g2"), wo("u2"), "ffn2_up")
    (z3,) = _proj_ln(h2, wa("d2"), wo("d2"), x2, sp["ln3_g"], sp["ln3_b"], 0.5, "ffn2_down", want_b=False)

    sg = {}
    rs = {}
    loss, df2b, df2t, dz3, sg["ln3_g"], sg["ln3_b"] = _loss_ln_bwd(z3, sp["ln3_g"], sp["ln3_b"], target, 0.5)
    da2, db2 = _ffn_bwd_h(df2b, wa("d2"), wo("d2"), a2, b2, "ffn2_bwd_h")
    g4 = _wgrad(df2t, h2, "d2", "wgrad_d2")
    g4 = _wgrad(x2t, da2, "g2", "wgrad_g2", g4)
    g4 = _wgrad(x2t, db2, "u2", "wgrad_u2", g4)
    rs[4] = _ReduceScatter(g4, 4, pos_c, pos_sc)
    dmb, dmt, dz2, sg["ln2_g"], sg["ln2_b"] = _dx_bwd(
        [da2, db2], [wo("g2"), wo("u2")], FSP, wa("g2"), dz3, "ffn2_dx_ln", ln=(z2, sp["ln2_g"], 1.0))
    rs[4].phase2()
    g3 = _wgrad(dmt, merged, "w_out", "wgrad_out")
    dro, drot, dco, dcot, dproj = _merge_bwd(dmb, wa("w_out"), wo("w_out"), proj, ro, co)
    g3 = _wgrad(drot, ret_in, "w_ret_o", "wgrad_ret_o", g3)
    g3 = _wgrad(dcot, u3, "w_conv_o", "wgrad_conv_o", g3)
    rs[3] = _ReduceScatter(g3, 3, pos_c, pos_sc)
    dr, sg["ret_gn_g"], dproj = _reto_bwd(dro, wa("w_ret_o"), wo("w_ret_o"), r, proj, sp["ret_gn_g"], dproj)
    rs[4].phase3()
    rs[3].phase2()
    dproj = _retention_bwd(proj, dr, states, cos_t, sin_t, dm_t, xi_t, zeta_t, cds, dproj)
    du1, sg["conv_ln_g"], sg["conv_ln_b"], sg["conv_b"] = _convo_bwd(
        dco, wa("w_conv_o"), wo("w_conv_o"), u1, sp["conv_ln_g"], sp["conv_ln_b"])
    dproj, dck = _conv_bwd(du1, proj, sp["conv_k"], dproj)
    sg["conv_k"] = dck[:CONV_W]
    adam(4, rs[4].result())
    rs[3].phase3()
    g2 = _wgrad(x1t, dproj, "w_in", "wgrad_in")
    rs[2] = _ReduceScatter(g2, 2, pos_c, pos_sc)
    df1b, df1t, dz1, sg["ln1_g"], sg["ln1_b"], sg["b_in"] = _dx_bwd(
        [dproj], [wo("w_in")], 512, wa("w_in"), dz2, "mixer_dx_ln", ln=(z1, sp["ln1_g"], 0.5), colsum=True)
    adam(3, rs[3].result())
    rs[2].phase2()
    da1, db1 = _ffn_bwd_h(df1b, wa("d1"), wo("d1"), a1, b1, "ffn1_bwd_h")
    g1 = _wgrad(df1t, h1, "d1", "wgrad_d1")
    rs[1] = _ReduceScatter(g1, 1, pos_c, pos_sc)
    g0 = _wgrad(xt, da1, "g1", "wgrad_g1")
    g0 = _wgrad(xt, db1, "u1", "wgrad_u1", g0)
    rs[0] = _ReduceScatter(g0, 0, pos_c, pos_sc)
    rs[2].phase3()
    rs[1].phase2()
    rs[0].phase2()
    (grad_x,) = _dx_bwd([da1, db1], [wo("g1"), wo("u1")], FSP, wa("g1"), dz1, "ffn1_dx")
    adam(2, rs[2].result())
    rs[1].phase3()
    rs[0].phase3()
    adam(1, rs[1].result())
    adam(0, rs[0].result())
    _ORDER.active = False
    return loss, grad_x, sg


MESH = pl.DeviceIdType.MESH
ANY = pl.BlockSpec(memory_space=pl.ANY)
HALF = D // 2


def _place():
    x, y, c = lax.axis_index("x"), lax.axis_index("y"), lax.axis_index("c")
    chips = [(1 - x, y), (x, 1 - y), (1 - x, 1 - y)]
    return x, y, c, chips


GATHER_ID = 1


def _gather_weights(wloc, name):
    w_ref = jax.new_ref(wloc, memory_space=pltpu.MemorySpace.HBM)
    o_ref = jax.empty_ref(jax.ShapeDtypeStruct((NCHIP, 2, HALF, wloc.shape[-1]), BF16),
                          memory_space=pltpu.MemorySpace.HBM)
    dma = pltpu.SemaphoreType.DMA

    @pl.kernel(mesh=plsc.ScalarSubcoreMesh(axis_name="sc", num_cores=1), name=name,
               scratch_types=(dma(()), dma((3,)), dma((3,)), dma((3,)), dma((3,))),
               compiler_params=pltpu.CompilerParams(collective_id=GATHER_ID))
    def launch(lsem, s1, r1, s2, r2):
        x, y, c, chips = _place()
        me = 2 * x + y
        sib = (x, y, 1 - c)
        barrier = pltpu.get_barrier_semaphore()
        for peer in [sib] + [(px, py, c) for px, py in chips]:
            pl.semaphore_signal(barrier, inc=1, device_id=peer, device_id_type=MESH)
        pl.semaphore_wait(barrier, 4)
        mine = pltpu.make_async_copy(w_ref, o_ref.at[me], lsem)
        mine.start()

        def rc(src, dst, ss, rs, dev):
            return pltpu.make_async_remote_copy(src_ref=src, dst_ref=dst, send_sem=ss, recv_sem=rs,
                                                device_id=dev, device_id_type=MESH)

        first = [rc(w_ref.at[c], o_ref.at[me, c], s1.at[j], r1.at[j], (*chip, c)) for j, chip in enumerate(chips)]
        for cp in first:
            cp.start()
        passed = []
        for j, (px, py) in enumerate(chips):
            slot = o_ref.at[2 * px + py, c]
            rc(slot, slot, s1.at[j], r1.at[j], (px, py, c)).wait_recv()
            cp = rc(slot, slot, s2.at[j], r2.at[j], sib)
            cp.start()
            passed.append(cp)
        for j, (px, py) in enumerate(chips):
            slot = o_ref.at[2 * px + py, 1 - c]
            rc(slot, slot, s2.at[j], r2.at[j], sib).wait_recv()
        for cp in first + passed:
            cp.wait_send()
        mine.wait()

    launch()
    return o_ref[...]


PAIR_ID = 2
CHIP_ID = 3
HBM = pltpu.MemorySpace.HBM


def _sequencer(name, collective_id, n_sems):
    dma = pltpu.SemaphoreType.DMA
    return pl.kernel(mesh=plsc.ScalarSubcoreMesh(axis_name="sc", num_cores=1), name=name,
                     scratch_types=(dma((n_sems,)), dma((n_sems,))),
                     compiler_params=pltpu.CompilerParams(collective_id=collective_id))


def _handshake(peers):
    barrier = pltpu.get_barrier_semaphore()
    for peer in peers:
        pl.semaphore_signal(barrier, inc=1, device_id=peer, device_id_type=MESH)
    pl.semaphore_wait(barrier, len(peers))


def _pair_exchange(g5, name):
    _, _, hr, cols = g5.shape
    g_ref = jax.new_ref(g5, memory_space=HBM)
    o_ref = jax.empty_ref(jax.ShapeDtypeStruct((NCHIP, hr, cols), F32), memory_space=HBM)

    @_sequencer(name, PAIR_ID, NCHIP)
    def launch(ss, rs):
        x, y, c, _ = _place()
        sib = (x, y, 1 - c)
        _handshake([sib])
        cps = [pltpu.make_async_remote_copy(src_ref=g_ref.at[j, 1 - c], dst_ref=o_ref.at[j], send_sem=ss.at[j],
                                            recv_sem=rs.at[j], device_id=sib, device_id_type=MESH)
               for j in range(NCHIP)]
        for cp in cps:
            cp.start()
        for cp in cps:
            cp.wait()

    launch()
    return o_ref[...]


RS_TR = 128


def _pair_sum(pos, g5, got, name):
    _, _, hr, cols = g5.shape

    def body(pos_ref, g_ref, r_ref, o_ref):
        o_ref[...] = (g_ref[...] + r_ref[...]).astype(BF16)

    return _pcall(
        body, name=name, scalar_prefetch=1, grid=(NCHIP, hr // RS_TR),
        in_specs=[pl.BlockSpec((None, None, RS_TR, cols), lambda j, i, p: (j, p[0], i, 0)),
                  pl.BlockSpec((None, RS_TR, cols), lambda j, i, p: (j, i, 0))],
        out_specs=pl.BlockSpec((None, RS_TR, cols), lambda j, i, p: (j, i, 0)),
        out_shape=jax.ShapeDtypeStruct((NCHIP, hr, cols), BF16),
        compiler_params=_cp("parallel", "parallel"))(pos, g5, got)


def _chip_exchange(pb, name):
    _, hr, cols = pb.shape
    p_ref = jax.new_ref(pb, memory_space=HBM)
    o_ref = jax.empty_ref(jax.ShapeDtypeStruct((3, hr, cols), BF16), memory_space=HBM)

    @_sequencer(name, CHIP_ID, 3)
    def launch(ss, rs):
        x, y, c, chips = _place()
        _handshake([(px, py, c) for px, py in chips])
        cps = [pltpu.make_async_remote_copy(src_ref=p_ref.at[2 * px + py], dst_ref=o_ref.at[j], send_sem=ss.at[j],
                                            recv_sem=rs.at[j], device_id=(px, py, c), device_id_type=MESH)
               for j, (px, py) in enumerate(chips)]
        for cp in cps:
            cp.start()
        for cp in cps:
            cp.wait()

    launch()
    return o_ref[...]


def _chip_sum(pos, g5, got, peers, name):
    _, _, hr, cols = g5.shape

    def body(pos_ref, g_ref, r_ref, p_ref, o_ref, t_ref):
        acc = g_ref[...] + r_ref[...]
        for j in range(3):
            acc += p_ref[j].astype(F32)
        o_ref[...] = acc
        t_ref[...] = jnp.zeros_like(t_ref)

    return _pcall(
        body, name=name, scalar_prefetch=1, grid=(hr // RS_TR,),
        in_specs=[pl.BlockSpec((None, None, RS_TR, cols), lambda i, p: (p[0], p[1], i, 0)),
                  pl.BlockSpec((None, RS_TR, cols), lambda i, p: (p[0], i, 0)),
                  pl.BlockSpec((3, RS_TR, cols), lambda i, p: (0, i, 0))],
        out_specs=[pl.BlockSpec((None, RS_TR, cols), lambda i, p: (p[1], i, 0)),
                   pl.BlockSpec((8, 128), lambda i, p: (0, 0))],
        out_shape=[jax.ShapeDtypeStruct((2, hr, cols), F32), jax.ShapeDtypeStruct((8, 128), F32)],
        compiler_params=_cp("arbitrary"))(pos, g5, got, peers)


def _pair_share(gsum, name):
    g_ref = jax.new_ref(gsum, memory_space=HBM)

    @_sequencer(name, PAIR_ID, 1)
    def launch(ss, rs):
        x, y, c, _ = _place()
        sib = (x, y, 1 - c)
        _handshake([sib])
        cp = pltpu.make_async_remote_copy(src_ref=g_ref.at[c], dst_ref=g_ref.at[c], send_sem=ss.at[0],
                                          recv_sem=rs.at[0], device_id=sib, device_id_type=MESH)
        cp.start()
        cp.wait_send()
        pltpu.make_async_remote_copy(src_ref=g_ref.at[1 - c], dst_ref=g_ref.at[1 - c], send_sem=ss.at[0],
                                     recv_sem=rs.at[0], device_id=sib, device_id_type=MESH).wait_recv()

    launch()
    return g_ref[...]


class _ReduceScatter:
    def __init__(self, g_arr, gi, pos_c, pos_sc):
        _, rows, cols = g_arr.shape
        self.g5 = g_arr.reshape(NCHIP, 2, rows // 2, cols)
        self.gi, self.pos_c, self.pos_sc = gi, pos_c, pos_sc
        self.got = _pair_exchange(self.g5, f"pair_exchange_{gi}")

    def phase2(self):
        pb = _pair_sum(self.pos_c, self.g5, self.got, f"pair_sum_{self.gi}")
        self.peers = _chip_exchange(pb, f"chip_exchange_{self.gi}")

    def phase3(self):
        gsum, _ = _chip_sum(self.pos_sc, self.g5, self.got, self.peers, f"chip_sum_{self.gi}")
        self.full = _pair_share(gsum, f"pair_share_{self.gi}")

    def result(self):
        _, hr, cols = self.full.shape
        return self.full.reshape(2 * hr, cols)


SMALL_ROWS = 56


def _allreduce_small(vec, name):
    def body(v_ref, o_ref, buf, ss, rs):
        x, y, c, _ = _place()
        me = 4 * x + 2 * y + c
        buf[me] = v_ref[...]
        cps = []
        for m in range(1, 8):
            dev = (x ^ (m >> 2), y ^ ((m >> 1) & 1), c ^ (m & 1))
            cp = pltpu.make_async_remote_copy(src_ref=v_ref, dst_ref=buf.at[me], send_sem=ss.at[m - 1],
                                              recv_sem=rs.at[m - 1], device_id=dev, device_id_type=MESH)
            cp.start()
            cps.append(cp)
        for cp in cps:
            cp.wait()
        acc = buf[0]
        for d in range(1, 8):
            acc += buf[d]
        o_ref[...] = acc

    dma = pltpu.SemaphoreType.DMA
    vm = pl.BlockSpec(memory_space=pltpu.VMEM)
    return _pcall(
        body, name=name, in_specs=[vm], out_specs=vm,
        out_shape=jax.ShapeDtypeStruct((SMALL_ROWS, D), F32),
        scratch_shapes=[pltpu.VMEM((8, SMALL_ROWS, D), F32), dma((7,)), dma((7,))])(vec)


def _adamw_math(w, g, m, v):
    m2 = ADAM_B1 * m + (1.0 - ADAM_B1) * g
    v2 = ADAM_B2 * v + (1.0 - ADAM_B2) * (g * g)
    m_hat = m2 / (1.0 - ADAM_B1 ** ADAM_STEP)
    v_hat = v2 / (1.0 - ADAM_B2 ** ADAM_STEP)
    delta = -ADAM_LR * (m_hat / (jnp.sqrt(v_hat) + ADAM_EPS) + ADAM_WD * w)
    return delta, m2, v2


def _adamw(w, g, m, v, name, g_block=None):
    R, C = w.shape
    tr = R
    gw_hint = C if g_block is None else g_block[0]
    for cand in (256, 176, 128, 64, 32, 16, 8):
        if R % cand == 0 and cand * max(C, gw_hint) * 4 <= (1 << 20):
            tr = cand
            break
    gw, gi = (C, 0) if g_block is None else g_block

    def body(w_ref, g_ref, m_ref, v_ref, go_ref, d_ref, mo_ref, vo_ref):
        g = g_ref[:, 0:C]
        d, m2, v2 = _adamw_math(w_ref[...], g, m_ref[...], v_ref[...])
        go_ref[...] = g
        d_ref[...] = d
        mo_ref[...] = m2
        vo_ref[...] = v2

    spec = pl.BlockSpec((tr, C), lambda i: (i, 0))
    return _pcall(
        body, name=name, grid=(R // tr,),
        in_specs=[spec, pl.BlockSpec((tr, gw), lambda i: (i, gi)), spec, spec],
        out_specs=[spec] * 4, out_shape=[jax.ShapeDtypeStruct((R, C), F32)] * 4,
        compiler_params=_cp("parallel"))(w, g, m, v)


BIG = ["ffn1_w_gate", "ffn1_w_up", "ffn1_w_down", "w_in", "w_ret_o", "w_conv_o", "w_out",
       "ffn2_w_gate", "ffn2_w_up", "ffn2_w_down"]
SLAB = {"ffn1_w_gate": "g1", "ffn1_w_up": "u1", "ffn1_w_down": "d1", "w_in": "w_in", "w_ret_o": "w_ret_o",
        "w_conv_o": "w_conv_o", "w_out": "w_out", "ffn2_w_gate": "g2", "ffn2_w_up": "u2", "ffn2_w_down": "d2"}
TRANSPOSED = {"ffn1_w_down", "ffn2_w_down", "w_ret_o", "w_conv_o", "w_out"}
SMALL = ["ln1_g", "ln1_b", "ln2_g", "ln2_b", "ln3_g", "ln3_b", "conv_ln_g", "conv_ln_b", "conv_b",
         "ret_gn_g", "b_in"]
ORDER = ["ffn1_w_gate", "ffn1_w_up", "ffn1_w_down", "ln1_g", "ln1_b", "w_in", "b_in", "ret_gn_g", "conv_k",
         "conv_b", "conv_ln_g", "conv_ln_b", "w_ret_o", "w_conv_o", "w_out", "ln2_g", "ln2_b",
         "ffn2_w_gate", "ffn2_w_up", "ffn2_w_down", "ln3_g", "ln3_b"]


def _slab_width(name):
    return WIDTH[SLAB[name]]


def _pack_group(weights, keys):
    by_key = {SLAB[n]: n for n in BIG}
    parts = []
    for key in keys:
        w = weights[by_key[key]]
        w = w.T if by_key[key] in TRANSPOSED else w
        parts.append(jnp.pad(w, ((0, 0), (0, WIDTH[key] - w.shape[1]))))
    return jnp.concatenate(parts, axis=1).astype(BF16)


def _pack_small(vals, rows):
    flat = jnp.concatenate([vals[n].reshape(-1) for n in SMALL] + [vals["conv_k"].reshape(-1)])
    return jnp.pad(flat, (0, rows * D - flat.shape[0])).reshape(rows, D)


def _unpack_small(arr, shapes):
    flat = arr.reshape(-1)
    out, pos = {}, 0
    for n in SMALL + ["conv_k"]:
        size = int(np.prod(shapes[n]))
        out[n] = flat[pos:pos + size].reshape(shapes[n])
        pos += size
    return out


def kernel(x, ffn1_w_gate, ffn1_w_up, ffn1_w_down, ln1_g, ln1_b, w_in, b_in, ret_gn_g, conv_k, conv_b, conv_ln_g, conv_ln_b, w_ret_o, w_conv_o, w_out, ln2_g, ln2_b, ffn2_w_gate, ffn2_w_up, ffn2_w_down, ln3_g, ln3_b, loss_target, m_ffn1_w_gate, m_ffn1_w_up, m_ffn1_w_down, m_ln1_g, m_ln1_b, m_w_in, m_b_in, m_ret_gn_g, m_conv_k, m_conv_b, m_conv_ln_g, m_conv_ln_b, m_w_ret_o, m_w_conv_o, m_w_out, m_ln2_g, m_ln2_b, m_ffn2_w_gate, m_ffn2_w_up, m_ffn2_w_down, m_ln3_g, m_ln3_b, v_ffn1_w_gate, v_ffn1_w_up, v_ffn1_w_down, v_ln1_g, v_ln1_b, v_w_in, v_b_in, v_ret_gn_g, v_conv_k, v_conv_b, v_conv_ln_g, v_conv_ln_b, v_w_ret_o, v_w_conv_o, v_w_out, v_ln2_g, v_ln2_b, v_ffn2_w_gate, v_ffn2_w_up, v_ffn2_w_down, v_ln3_g, v_ln3_b):
    args = dict(locals())
    w = {n: args[n] for n in ORDER}
    m = {n: args["m_" + n] for n in ORDER}
    v = {n: args["v_" + n] for n in ORDER}
    xi, yi, ci = lax.axis_index("x"), lax.axis_index("y"), lax.axis_index("c")
    chip = 2 * xi + yi

    shards = {n: w[n][0] for n in BIG}
    wts = []
    for gi, keys in enumerate(GROUPS):
        slab = _pack_group(shards, keys)
        cols = slab.shape[1]
        wts.append(_gather_weights(slab.reshape(2, HALF, cols), f"gather_{gi}").reshape(NCHIP, D, cols))

    sp = {n: w[n] for n in SMALL}
    sp["conv_k"] = None
    kfull_shape = (CONV_W, D)
    kpad = jnp.zeros(kfull_shape, F32)
    kpad = lax.dynamic_update_slice(kpad, w["conv_k"][0, :, 0, :] * jnp.where(ci == 0, 1.0, 0.0), (0, chip * (D // NCHIP)))
    kvec = jnp.pad(kpad.reshape(-1), (0, SMALL_ROWS * D - CONV_W * D)).reshape(SMALL_ROWS, D)
    sp["conv_k"] = _allreduce_small(kvec, "gather_conv_k").reshape(-1)[:CONV_W * D].reshape(kfull_shape)
    pos_c = jnp.reshape(ci, (1,)).astype(jnp.int32)
    pos_sc = jnp.stack([chip, ci]).astype(jnp.int32)
    out = {}

    def adam(gi, slab):
        for n in BIG:
            (g_of, off), width = LOC[SLAB[n]], _slab_width(n)
            if g_of != gi:
                continue
            w2 = w[n][0]
            if n in TRANSPOSED:
                res = _adamw(w2, slab[:, off:off + w2.shape[0]].T, m[n][0], v[n][0], "adamw_" + n)
            else:
                res = _adamw(w2, slab, m[n][0], v[n][0], "adamw_" + n, g_block=(width, off // width))
            out[n] = [r[None] for r in res]

    loss, grad_x, sg = _local_step(x[0], loss_target[0], wts, sp, pos_c, pos_sc, adam)

    shapes = {n: sg[n].shape for n in SMALL + ["conv_k"]}
    small = _unpack_small(_allreduce_small(_pack_small(sg, SMALL_ROWS), "allreduce_small"), shapes)

    for n in SMALL:
        res = _adamw(w[n], small[n], m[n], v[n], "adamw_" + n)
        out[n] = list(res)
    gk = lax.dynamic_slice(small["conv_k"], (0, chip * (D // NCHIP)), (CONV_W, D // NCHIP))
    res = _adamw(w["conv_k"][0, :, 0, :], gk, m["conv_k"][0, :, 0, :], v["conv_k"][0, :, 0, :], "adamw_conv_k")
    out["conv_k"] = [r[None, :, None, :] for r in res]

    total = lax.psum(loss[0, 0], ("x", "y", "c"))
    grads = [out[n][0] for n in ORDER]
    deltas = [out[n][1] for n in ORDER]
    new_m = [out[n][2] for n in ORDER]
    new_v = [out[n][3] for n in ORDER]
    return (total, grad_x[None], *grads, *deltas, *new_m, *new_v)
```

```python
import dataclasses
import functools

import numpy as np
import jax
import jax.numpy as jnp
from jax import lax
from jax.experimental import pallas as pl
from jax.experimental.pallas import tpu as pltpu
from jax.experimental.pallas import tpu_sc as plsc

F32 = jnp.float32
BF16 = jnp.bfloat16

D = 1024
FS = 704
FSP = 768
FP = 4 * FSP
H = 8
DK = 128
DV = 256
CH = 128
VW = H * DV
INW = 10240
INS = INW // 4
CONV_W = 31
HALO = 32
EPS = 1e-5
ALPHA = 2.0 ** 0.25
ROPE_BASE = 10000.0
NCHIP = 4

ADAM_LR, ADAM_B1, ADAM_B2, ADAM_EPS, ADAM_WD, ADAM_STEP = 0.001, 0.9, 0.999, 1e-08, 0.01, 10

OFF = {"w_in": 0, "w_ret_o": 2560, "g1": 3072, "u1": 3840, "d1": 4608,
       "g2": 5376, "u2": 6144, "d2": 6912, "w_conv_o": 7680, "w_out": 7936}
WCOLS = 8192
WIDTH = {"w_in": INS, "w_ret_o": VW // NCHIP, "w_conv_o": D // NCHIP, "w_out": D // NCHIP,
         "g1": FSP, "u1": FSP, "d1": FSP, "g2": FSP, "u2": FSP, "d2": FSP}
GROUPS = (("g1", "u1"), ("d1",), ("w_in",), ("w_ret_o", "w_conv_o", "w_out"), ("g2", "u2", "d2"))
LOC = {}
for _gi, _keys in enumerate(GROUPS):
    _off = 0
    for _k in _keys:
        LOC[_k] = (_gi, _off)
        _off += WIDTH[_k]
GCOLS = [sum(WIDTH[k] for k in keys) for keys in GROUPS]
VMEM_LIMIT = 56 << 20


def _cp(*sem, **kw):
    return pltpu.CompilerParams(dimension_semantics=sem, vmem_limit_bytes=VMEM_LIMIT, **kw)


class _ProgramOrder:
    def __init__(self):
        self.active = False
        self.token = None


_ORDER = _ProgramOrder()


def _pcall(body, *, in_specs, scalar_prefetch=0, **kw):
    def call(*args):
        dep = _ORDER.token if _ORDER.active else None
        specs, fn = list(in_specs), body
        if dep is not None:
            n = len(args)

            def fn(*refs):
                return body(*refs[:n], *refs[n + 1:])

            specs.append(pl.BlockSpec(memory_space=pl.ANY))
            args = (*args, dep)
        params = dict(kw)
        if scalar_prefetch:
            params["grid_spec"] = pltpu.PrefetchScalarGridSpec(
                num_scalar_prefetch=scalar_prefetch, grid=params.pop("grid"), in_specs=specs,
                out_specs=params.pop("out_specs"))
        else:
            params["in_specs"] = specs
        out = pl.pallas_call(fn, **params)(*args)
        if _ORDER.active:
            _ORDER.token = jax.tree.leaves(out)[-1]
        return out

    return call


def _resident(shape, col_block):
    lead = (0,) * (len(shape) - 1)
    return pl.BlockSpec(shape, lambda *_: (*lead, col_block), pipeline_mode=pl.Buffered(1))


def _sig(x):
    return 1.0 / (1.0 + jnp.exp(-x))


def _dot(a, b):
    return jnp.dot(a, b, preferred_element_type=F32)


def _dot_nt(a, b):
    return lax.dot_general(a, b, (((1,), (1,)), ((), ())), preferred_element_type=F32)


def _ln_fwd(z, g, b):
    mu = jnp.mean(z, axis=-1, keepdims=True)
    xc = z - mu
    var = jnp.mean(xc * xc, axis=-1, keepdims=True)
    rstd = lax.rsqrt(var + EPS)
    xh = xc * rstd
    return xh * g + b, xh, rstd


def _ln_bwd(dy, xh, rstd, g):
    dxh = dy * g
    m1 = jnp.mean(dxh, axis=-1, keepdims=True)
    m2 = jnp.mean(dxh * xh, axis=-1, keepdims=True)
    return rstd * (dxh - m1 - xh * m2)


def _colsum(x):
    return jnp.sum(x, axis=0, keepdims=True)


def _acc_rows(ref, first, val):
    @pl.when(first)
    def _():
        ref[...] = val

    @pl.when(jnp.logical_not(first))
    def _():
        ref[...] += val


def _rope_tables(T):
    half = DK // 2
    freqs = ROPE_BASE ** (-np.arange(half, dtype=np.float32) / half)
    ang = (np.arange(T, dtype=np.float32)[:, None] * freqs[None, :]).astype(np.float32)
    cos, sin = np.cos(ang), np.sin(ang)
    return (jnp.asarray(np.concatenate([cos, cos], 1), F32),
            jnp.asarray(np.concatenate([-sin, sin], 1), F32))


def _decay_tables():
    h = np.arange(H, dtype=np.float64)
    log_g = np.log(1.0 - np.exp2(-5.0 - h))
    idx = np.arange(CH, dtype=np.float64)
    diff = idx[:, None] - idx[None, :]
    dm = np.where(diff[None] >= 0, np.exp(np.maximum(diff, 0.0)[None] * log_g[:, None, None]), 0.0)
    xi = np.exp((idx[None, :] + 1.0) * log_g[:, None])
    zeta = np.exp((CH - 1.0 - idx)[None, :] * log_g[:, None])
    cd = np.exp(CH * log_g)
    xi_t = np.broadcast_to(xi[:, :, None], (H, CH, DV))
    zeta_t = np.broadcast_to(zeta[:, :, None], (H, CH, DK))
    return (jnp.asarray(dm, F32), jnp.asarray(xi_t, F32), jnp.asarray(zeta_t, F32),
            [float(v) for v in cd])


def _cast_t(x):
    T = x.shape[0]
    tm = min(T, 512)

    def body(x_ref, xb_ref, xt_ref):
        v = x_ref[...]
        xb_ref[...] = v.astype(BF16)
        xt_ref[...] = v.T.astype(BF16)

    return _pcall(
        body, name="cast_t", grid=(T // tm,),
        in_specs=[pl.BlockSpec((tm, D), lambda i: (i, 0))],
        out_specs=[pl.BlockSpec((tm, D), lambda i: (i, 0)), pl.BlockSpec((D, tm), lambda i: (0, i))],
        out_shape=[jax.ShapeDtypeStruct((T, D), BF16), jax.ShapeDtypeStruct((D, T), BF16)],
        compiler_params=_cp("parallel"))(x)


def _ffn_up(xb, wall, og, ou, name):
    T = xb.shape[0]
    tm = min(T, 512)
    assert ou == og + FSP

    def body(x_ref, w_ref, a_ref, b_ref, h_ref):
        x = x_ref[...]
        for s in range(NCHIP):
            cols = slice(s * FSP, (s + 1) * FSP)
            a = _dot(x, w_ref[s, :, 0:FSP])
            b = _dot(x, w_ref[s, :, FSP:2 * FSP])
            a_ref[:, cols] = a.astype(BF16)
            b_ref[:, cols] = b.astype(BF16)
            h_ref[:, cols] = (a * _sig(a) * b).astype(BF16)

    ospec = pl.BlockSpec((tm, FP), lambda i: (i, 0))
    return _pcall(
        body, name=name, grid=(T // tm,),
        in_specs=[pl.BlockSpec((tm, D), lambda i: (i, 0)), _resident((NCHIP, D, 2 * FSP), og // (2 * FSP))],
        out_specs=[ospec] * 3, out_shape=[jax.ShapeDtypeStruct((T, FP), BF16)] * 3,
        compiler_params=_cp("parallel"))(xb, wall)


def _proj_ln(hb, wall, off, res, g, b, coef, name, want_b=True):
    T, K = hb.shape
    ks = K // NCHIP
    tm = min(T, 256)

    def body(h_ref, w_ref, r_ref, g_ref, b_ref, z_ref, *rest):
        acc = _dot_nt(h_ref[:, 0:ks], w_ref[0])
        for s in range(1, NCHIP):
            acc += _dot_nt(h_ref[:, s * ks:(s + 1) * ks], w_ref[s])
        z = ALPHA * r_ref[...] + coef * acc
        z_ref[...] = z
        if want_b:
            y, _, _ = _ln_fwd(z, g_ref[...], b_ref[...])
            y_ref, yb_ref, yt_ref = rest
            y_ref[...] = y
            yb_ref[...] = y.astype(BF16)
            yt_ref[...] = y.T.astype(BF16)

    row = pl.BlockSpec((tm, D), lambda i: (i, 0))
    vec = pl.BlockSpec((1, D), lambda i: (0, 0))
    out_specs = [row]
    out_shape = [jax.ShapeDtypeStruct((T, D), F32)]
    if want_b:
        out_specs += [row, row, pl.BlockSpec((D, tm), lambda i: (0, i))]
        out_shape += [jax.ShapeDtypeStruct((T, D), F32), jax.ShapeDtypeStruct((T, D), BF16),
                      jax.ShapeDtypeStruct((D, T), BF16)]
    return _pcall(
        body, name=name, grid=(T // tm,),
        in_specs=[pl.BlockSpec((tm, K), lambda i: (i, 0)),
                  pl.BlockSpec((NCHIP, D, ks), lambda i: (0, 0, off // ks)), row, vec, vec],
        out_specs=out_specs, out_shape=out_shape,
        compiler_params=_cp("parallel"))(hb, wall, res, g, b)


def _inproj(xb, wall, off, b_in, cos_t, sin_t):
    T = xb.shape[0]
    tm, tn = min(T, 256), 512
    assert off == 0

    def body(x_ref, w_ref, bias_ref, cos_ref, sin_ref, o_ref):
        x = x_ref[...]
        c = cos_ref[...]
        s = sin_ref[...]
        for n0 in range(0, INW, tn):
            chip, c0 = divmod(n0, INS)
            acc = _dot(x, w_ref[chip, :, c0:c0 + tn]) + bias_ref[:, n0:n0 + tn]
            if n0 >= 2 * D:
                o_ref[:, n0:n0 + tn] = acc.astype(BF16)
                continue
            scale = DK ** -0.5 if n0 < D else 1.0
            for hh in range(tn // DK):
                xh = acc[:, hh * DK:(hh + 1) * DK]
                o = (xh * c + pltpu.roll(xh, DK // 2, 1) * s) * scale
                o_ref[:, n0 + hh * DK:n0 + (hh + 1) * DK] = o.astype(BF16)

    return _pcall(
        body, name="inproj", grid=(T // tm,),
        in_specs=[pl.BlockSpec((tm, D), lambda i: (i, 0)),
                  _resident((NCHIP, D, INS), 0),
                  pl.BlockSpec((1, INW), lambda i: (0, 0)),
                  pl.BlockSpec((tm, DK), lambda i: (i, 0)),
                  pl.BlockSpec((tm, DK), lambda i: (i, 0))],
        out_specs=pl.BlockSpec((tm, INW), lambda i: (i, 0)),
        out_shape=jax.ShapeDtypeStruct((T, INW), BF16),
        compiler_params=_cp("parallel"))(xb, wall, b_in, cos_t, sin_t)


def _retention_fwd(proj, gn_g, dm_t, xi_t, zeta_t, cds):
    T = proj.shape[0]
    n = T // CH

    def body(q_ref, k_ref, v_ref, g_ref, gn_ref, dm_ref, xi_ref, zt_ref, r_ref, ri_ref, st_ref, state):
        @pl.when(pl.program_id(0) == 0)
        def _():
            state[...] = jnp.zeros_like(state)

        for h in range(H):
            q = q_ref[:, h * DK:(h + 1) * DK]
            k = k_ref[:, h * DK:(h + 1) * DK]
            v = v_ref[:, h * DV:(h + 1) * DV]
            rows = slice(h * DK, (h + 1) * DK)
            s_prev = state[rows, :]
            s_b = s_prev.astype(BF16)
            st_ref[rows, :] = s_b
            sc = _dot_nt(q, k) * dm_ref[h]
            r = _dot(sc.astype(BF16), v) + _dot(q, s_b) * xi_ref[h]
            kz = k.astype(F32) * zt_ref[h]
            state[rows, :] = cds[h] * s_prev + _dot(kz.T.astype(BF16), v)
            cols = slice(h * DV, (h + 1) * DV)
            r_ref[:, cols] = r
            mu = jnp.mean(r, axis=-1, keepdims=True)
            xc = r - mu
            var = jnp.mean(xc * xc, axis=-1, keepdims=True)
            y = xc * lax.rsqrt(var + EPS) * gn_ref[:, cols]
            g = g_ref[:, cols].astype(F32)
            ri_ref[:, cols] = (g * _sig(g) * y).astype(BF16)

    full3 = lambda shp: pl.BlockSpec(shp, lambda c: (0, 0, 0))
    return _pcall(
        body, name="retention_fwd", grid=(n,),
        in_specs=[pl.BlockSpec((CH, D), lambda c: (c, 0)),
                  pl.BlockSpec((CH, D), lambda c: (c, 1)),
                  pl.BlockSpec((CH, VW), lambda c: (c, 1)),
                  pl.BlockSpec((CH, VW), lambda c: (c, 2)),
                  pl.BlockSpec((1, VW), lambda c: (0, 0)),
                  full3((H, CH, CH)), full3((H, CH, DV)), full3((H, CH, DK))],
        out_specs=[pl.BlockSpec((CH, VW), lambda c: (c, 0)), pl.BlockSpec((CH, VW), lambda c: (c, 0)),
                   pl.BlockSpec((None, H * DK, DV), lambda c: (c, 0, 0))],
        out_shape=[jax.ShapeDtypeStruct((T, VW), F32), jax.ShapeDtypeStruct((T, VW), BF16),
                   jax.ShapeDtypeStruct((n, H * DK, DV), BF16)],
        scratch_shapes=[pltpu.VMEM((H * DK, DV), F32)],
        compiler_params=_cp("arbitrary"))(proj, proj, proj, proj, gn_g, dm_t, xi_t, zeta_t)


CONV_TT = 256
CONV_SB = 64
CONV_CB = 256


def _glu(a_ref, b_ref, rows=slice(None)):
    a = a_ref[rows, :].astype(F32)
    sb = _sig(b_ref[rows, :].astype(F32))
    return a, sb


def _conv_fwd(proj, conv_k, conv_b, ln_g, ln_b):
    T = proj.shape[0]
    tt = min(T, CONV_TT)
    ca, cb = 6 * D // D, 7 * D // D

    def body(a_ref, b_ref, pa_ref, pb_ref, k_ref, cb_ref, g_ref, bb_ref, u1_ref, u3_ref, win):
        i = pl.program_id(0)
        a, sb = _glu(a_ref, b_ref)
        win[HALO:, :] = a * sb
        pa, psb = _glu(pa_ref, pb_ref, slice(tt - HALO, tt))
        win[0:HALO, :] = jnp.where(i > 0, pa * psb, 0.0)
        for c0 in range(0, D, CONV_CB):
            cs = slice(c0, c0 + CONV_CB)
            for r0 in range(0, tt, CONV_SB):
                acc = jnp.zeros((CONV_SB, CONV_CB), F32)
                for w in range(CONV_W):
                    st = r0 + HALO - (CONV_W - 1) + w
                    acc += win[st:st + CONV_SB, cs] * k_ref[w:w + 1, cs]
                u1_ref[r0:r0 + CONV_SB, cs] = acc + cb_ref[:, cs]
        u2, _, _ = _ln_fwd(u1_ref[...], g_ref[...], bb_ref[...])
        u3_ref[...] = (u2 * _sig(u2)).astype(BF16)

    vec = pl.BlockSpec((1, D), lambda i: (0, 0))
    row = pl.BlockSpec((tt, D), lambda i: (i, 0))
    return _pcall(
        body, name="conv_fwd", grid=(T // tt,),
        in_specs=[pl.BlockSpec((tt, D), lambda i: (i, ca)), pl.BlockSpec((tt, D), lambda i: (i, cb)),
                  pl.BlockSpec((tt, D), lambda i: (jnp.maximum(i - 1, 0), ca)),
                  pl.BlockSpec((tt, D), lambda i: (jnp.maximum(i - 1, 0), cb)),
                  pl.BlockSpec((CONV_W, D), lambda i: (0, 0)), vec, vec, vec],
        out_specs=[row, row],
        out_shape=[jax.ShapeDtypeStruct((T, D), F32), jax.ShapeDtypeStruct((T, D), BF16)],
        scratch_shapes=[pltpu.VMEM((tt + HALO, D), F32)],
        compiler_params=_cp("parallel"))(proj, proj, proj, proj, conv_k, conv_b, ln_g, ln_b)


def _merge(ret_in, u3, proj, wall, off_r, off_c):
    T = ret_in.shape[0]
    tm = min(T, 512)
    kr, kc = VW // NCHIP, D // NCHIP

    def body(ri_ref, u3_ref, gr_ref, gc_ref, wr_ref, wc_ref, ro_ref, co_ref, m_ref):
        ro = _dot_nt(ri_ref[:, 0:kr], wr_ref[0])
        co = _dot_nt(u3_ref[:, 0:kc], wc_ref[0])
        for s in range(1, NCHIP):
            ro += _dot_nt(ri_ref[:, s * kr:(s + 1) * kr], wr_ref[s])
            co += _dot_nt(u3_ref[:, s * kc:(s + 1) * kc], wc_ref[s])
        ro_ref[...] = ro.astype(BF16)
        co_ref[...] = co.astype(BF16)
        m = _sig(gr_ref[...].astype(F32)) * ro + _sig(gc_ref[...].astype(F32)) * co
        m_ref[...] = m.astype(BF16)

    row = pl.BlockSpec((tm, D), lambda i: (i, 0))
    return _pcall(
        body, name="merge", grid=(T // tm,),
        in_specs=[pl.BlockSpec((tm, VW), lambda i: (i, 0)), row,
                  pl.BlockSpec((tm, D), lambda i: (i, 8)), pl.BlockSpec((tm, D), lambda i: (i, 9)),
                  pl.BlockSpec((NCHIP, D, kr), lambda i: (0, 0, off_r // kr)),
                  pl.BlockSpec((NCHIP, D, kc), lambda i: (0, 0, off_c // kc))],
        out_specs=[row] * 3, out_shape=[jax.ShapeDtypeStruct((T, D), BF16)] * 3,
        compiler_params=_cp("parallel"))(ret_in, u3, proj, proj, wall, wall)


def _loss_ln_bwd(z, g, b, target, coef):
    T = z.shape[0]
    tm = min(T, 256)
    nt = T // tm

    def body(z_ref, g_ref, b_ref, t_ref, loss_ref, dzb_ref, dzt_ref, dz_ref, dg_ref, db_ref, lacc):
        i = pl.program_id(0)
        gam = g_ref[...]
        y, xh, rstd = _ln_fwd(z_ref[...], gam, b_ref[...])
        e = y - t_ref[...]
        part = _colsum(e * e)
        _acc_rows(lacc, i == 0, part)
        dy = e * (1.0 / D)
        dz = _ln_bwd(dy, xh, rstd, gam)
        dz_ref[...] = dz
        dzc = coef * dz
        dzb_ref[...] = dzc.astype(BF16)
        dzt_ref[...] = dzc.T.astype(BF16)
        _acc_rows(dg_ref, i == 0, _colsum(dy * xh))
        _acc_rows(db_ref, i == 0, _colsum(dy))

        @pl.when(i == nt - 1)
        def _():
            loss_ref[...] = (0.5 / D) * jnp.sum(lacc[...], axis=1, keepdims=True)

    row = pl.BlockSpec((tm, D), lambda i: (i, 0))
    vec = pl.BlockSpec((1, D), lambda i: (0, 0))
    return _pcall(
        body, name="loss_ln_bwd", grid=(nt,),
        in_specs=[row, vec, vec, row],
        out_specs=[pl.BlockSpec((1, 1), lambda i: (0, 0)), row, pl.BlockSpec((D, tm), lambda i: (0, i)),
                   row, vec, vec],
        out_shape=[jax.ShapeDtypeStruct((1, 1), F32), jax.ShapeDtypeStruct((T, D), BF16),
                   jax.ShapeDtypeStruct((D, T), BF16), jax.ShapeDtypeStruct((T, D), F32),
                   jax.ShapeDtypeStruct((1, D), F32), jax.ShapeDtypeStruct((1, D), F32)],
        scratch_shapes=[pltpu.VMEM((1, D), F32)],
        compiler_params=_cp("arbitrary"))(z, g, b, target)


def _ffn_bwd_h(dfb, wall, od, a, b, name):
    T = dfb.shape[0]
    tm = min(T, 512)

    def body(d_ref, w_ref, a_ref, b_ref, da_ref, db_ref):
        d = d_ref[...]
        for s in range(NCHIP):
            cols = slice(s * FSP, (s + 1) * FSP)
            dh = _dot(d, w_ref[s])
            a = a_ref[:, cols].astype(F32)
            sg = _sig(a)
            da_ref[:, cols] = (dh * b_ref[:, cols].astype(F32) * (sg * (1.0 + a * (1.0 - sg)))).astype(BF16)
            db_ref[:, cols] = (dh * a * sg).astype(BF16)

    ospec = pl.BlockSpec((tm, FP), lambda i: (i, 0))
    return _pcall(
        body, name=name, grid=(T // tm,),
        in_specs=[pl.BlockSpec((tm, D), lambda i: (i, 0)), _resident((NCHIP, D, FSP), od // FSP), ospec, ospec],
        out_specs=[ospec] * 2, out_shape=[jax.ShapeDtypeStruct((T, FP), BF16)] * 2,
        compiler_params=_cp("parallel"))(dfb, wall, a, b)


def _dx_bwd(lhs, offs, wall, dz_next, name, ln=None, colsum=False):
    T, K = lhs[0].shape
    ks = K // NCHIP
    nl = len(lhs)
    assert list(offs) == [l * ks for l in range(nl)]
    tm = min(T, 512 if K <= FP else 256)

    def body(*refs):
        l_refs = refs[:nl]
        w_ref = refs[nl]
        dzn_ref = refs[nl + 1]
        pos = nl + 2
        if ln is not None:
            z_ref, g_ref = refs[pos:pos + 2]
            pos += 2
        outs = refs[pos:]
        i = pl.program_id(0)
        acc = None
        for s in range(NCHIP):
            rows = slice(s * ks, (s + 1) * ks)
            for l in range(nl):
                part = _dot_nt(l_refs[l][:, rows], w_ref[s, :, l * ks:(l + 1) * ks])
                acc = part if acc is None else acc + part
            if colsum:
                _acc_rows(outs[-1].at[:, rows], i == 0, _colsum(l_refs[0][:, rows].astype(F32)))
        dy = acc + ALPHA * dzn_ref[...]
        if ln is None:
            outs[0][...] = dy
        else:
            gam = g_ref[...]
            _, xh, rstd = _ln_fwd(z_ref[...], gam, 0.0)
            dz = _ln_bwd(dy, xh, rstd, gam)
            dzc = ln[2] * dz
            outs[0][...] = dzc.astype(BF16)
            outs[1][...] = dzc.T.astype(BF16)
            outs[2][...] = dz
            _acc_rows(outs[3], i == 0, _colsum(dy * xh))
            _acc_rows(outs[4], i == 0, _colsum(dy))

    row = pl.BlockSpec((tm, D), lambda i: (i, 0))
    vec = pl.BlockSpec((1, D), lambda i: (0, 0))
    in_specs = [pl.BlockSpec((tm, K), lambda i: (i, 0))] * nl + [_resident((NCHIP, D, nl * ks), 0), row]
    args = list(lhs) + [wall, dz_next]
    if ln is None:
        out_specs = [row]
        out_shape = [jax.ShapeDtypeStruct((T, D), F32)]
    else:
        in_specs += [row, vec]
        args += [ln[0], ln[1]]
        out_specs = [row, pl.BlockSpec((D, tm), lambda i: (0, i)), row, vec, vec]
        out_shape = [jax.ShapeDtypeStruct((T, D), BF16), jax.ShapeDtypeStruct((D, T), BF16),
                     jax.ShapeDtypeStruct((T, D), F32), jax.ShapeDtypeStruct((1, D), F32),
                     jax.ShapeDtypeStruct((1, D), F32)]
    if colsum:
        out_specs += [pl.BlockSpec((1, K), lambda i: (0, 0))]
        out_shape += [jax.ShapeDtypeStruct((1, K), F32)]
    return _pcall(
        body, name=name, grid=(T // tm,), in_specs=in_specs, out_specs=out_specs, out_shape=out_shape,
        compiler_params=_cp("arbitrary"))(*args)


def _wgrad(lhs_t, rhs, key, name, g_all=None):
    T, N = rhs.shape
    tn = 256
    nps = N // NCHIP // tn
    off = LOC[key][1]
    cols = GCOLS[LOC[key][0]]

    def body(*refs):
        l_ref, r_ref, o_ref, t_ref = refs[0], refs[1], refs[-2], refs[-1]
        o_ref[...] = _dot(l_ref[...], r_ref[...])
        t_ref[...] = jnp.zeros_like(t_ref)

    in_specs = [pl.BlockSpec((D, T), lambda j: (0, 0)), pl.BlockSpec((T, tn), lambda j: (0, j))]
    args = [lhs_t, rhs]
    aliases = {}
    if g_all is not None:
        in_specs.append(pl.BlockSpec(memory_space=pl.ANY))
        args.append(g_all)
        aliases = {2: 0}
    return _pcall(
        body, name=name, grid=(N // tn,), in_specs=in_specs,
        out_specs=[pl.BlockSpec((None, D, tn), lambda j: (j // nps, 0, off // tn + j % nps)),
                   pl.BlockSpec((8, 128), lambda j: (0, 0))],
        out_shape=[jax.ShapeDtypeStruct((NCHIP, D, cols), F32), jax.ShapeDtypeStruct((8, 128), F32)],
        input_output_aliases=aliases,
        compiler_params=_cp("arbitrary"))(*args)[0]


def _merge_bwd(dmb, wall, off, proj, ro, co):
    T = dmb.shape[0]
    tm = min(T, 512)
    ks = D // NCHIP

    def body(d_ref, w_ref, gr_ref, gc_ref, ro_ref, co_ref, dro_ref, drot_ref, dco_ref, dcot_ref, dp_ref):
        d = d_ref[...]
        dmg = jnp.concatenate([_dot(d, w_ref[s]) for s in range(NCHIP)], axis=1)
        sr = _sig(gr_ref[...].astype(F32))
        sc = _sig(gc_ref[...].astype(F32))
        dro = dmg * sr
        dco = dmg * sc
        dro_ref[...] = dro.astype(BF16)
        drot_ref[...] = dro.T.astype(BF16)
        dco_ref[...] = dco.astype(BF16)
        dcot_ref[...] = dco.T.astype(BF16)
        dp_ref[:, 0:D] = (dmg * ro_ref[...].astype(F32) * sr * (1.0 - sr)).astype(BF16)
        dp_ref[:, D:2 * D] = (dmg * co_ref[...].astype(F32) * sc * (1.0 - sc)).astype(BF16)

    row = pl.BlockSpec((tm, D), lambda i: (i, 0))
    col = pl.BlockSpec((D, tm), lambda i: (0, i))
    return _pcall(
        body, name="merge_bwd", grid=(T // tm,),
        in_specs=[row, pl.BlockSpec((NCHIP, D, ks), lambda i: (0, 0, off // ks)),
                  pl.BlockSpec((tm, D), lambda i: (i, 8)), pl.BlockSpec((tm, D), lambda i: (i, 9)), row, row],
        out_specs=[row, col, row, col, pl.BlockSpec((tm, 2 * D), lambda i: (i, 4))],
        out_shape=[jax.ShapeDtypeStruct((T, D), BF16), jax.ShapeDtypeStruct((D, T), BF16),
                   jax.ShapeDtypeStruct((T, D), BF16), jax.ShapeDtypeStruct((D, T), BF16),
                   jax.ShapeDtypeStruct((T, INW), BF16)],
        compiler_params=_cp("parallel"))(dmb, wall, proj, proj, ro, co)


def _reto_bwd(dro, wall, off, r, proj, gn_g, dproj):
    T = dro.shape[0]
    tm = min(T, 512)
    hps = H // NCHIP

    def body(d_ref, w_ref, r_ref, g_ref, gn_ref, _, dr_ref, dgn_ref, dp_ref):
        i = pl.program_id(1)
        dri = _dot(d_ref[...], w_ref[...])
        rr = r_ref[...]
        mu = jnp.mean(rr, axis=-1, keepdims=True)
        xc = rr - mu
        var = jnp.mean(xc * xc, axis=-1, keepdims=True)
        rstd = lax.rsqrt(var + EPS)
        rn = xc * rstd
        gn = gn_ref[...]
        g = g_ref[...].astype(F32)
        sg = _sig(g)
        dy = dri * (g * sg)
        dp_ref[...] = (dri * (rn * gn) * (sg * (1.0 + g * (1.0 - sg)))).astype(BF16)
        _acc_rows(dgn_ref, i == 0, _colsum(dy * rn))
        dr_ref[...] = _ln_bwd(dy, rn, rstd, gn).astype(BF16)

    return _pcall(
        body, name="reto_bwd", grid=(H, T // tm),
        in_specs=[pl.BlockSpec((tm, D), lambda j, i: (i, 0)),
                  pl.BlockSpec((None, D, DV), lambda j, i: (j // hps, 0, off // DV + j % hps)),
                  pl.BlockSpec((tm, DV), lambda j, i: (i, j)),
                  pl.BlockSpec((tm, DV), lambda j, i: (i, 2 * VW // DV + j)),
                  pl.BlockSpec((1, DV), lambda j, i: (0, j)),
                  pl.BlockSpec(memory_space=pl.ANY)],
        out_specs=[pl.BlockSpec((tm, DV), lambda j, i: (i, j)), pl.BlockSpec((1, DV), lambda j, i: (0, j)),
                   pl.BlockSpec((tm, DV), lambda j, i: (i, 2 * VW // DV + j))],
        out_shape=[jax.ShapeDtypeStruct((T, VW), BF16), jax.ShapeDtypeStruct((1, VW), F32),
                   jax.ShapeDtypeStruct((T, INW), BF16)],
        input_output_aliases={5: 2},
        compiler_params=_cp("arbitrary", "arbitrary"))(dro, wall, r, proj, gn_g, dproj)


def _retention_bwd(proj, dr, states, cos_t, sin_t, dm_t, xi_t, zeta_t, cds, dproj):
    T = proj.shape[0]
    n = T // CH
    scale = DK ** -0.5

    def body(q_ref, k_ref, v_ref, dr_ref, st_ref, cos_ref, sin_ref, dm_ref, xi_ref, zt_ref, _, dp_ref, ds):
        @pl.when(pl.program_id(0) == 0)
        def _():
            ds[...] = jnp.zeros_like(ds)

        cos = cos_ref[...]
        sin = sin_ref[...]

        def unrope(d):
            return d * cos + pltpu.roll(d * sin, DK // 2, 1)

        for h in range(H):
            q = q_ref[:, h * DK:(h + 1) * DK]
            k = k_ref[:, h * DK:(h + 1) * DK]
            v = v_ref[:, h * DV:(h + 1) * DV]
            d_r = dr_ref[:, h * DV:(h + 1) * DV]
            rows = slice(h * DK, (h + 1) * DK)
            s_b = st_ref[rows, :]
            dm = dm_ref[h]
            zt = zt_ref[h]
            sc = _dot_nt(q, k) * dm
            dsc = _dot_nt(d_r, v) * dm
            drx = (d_r.astype(F32) * xi_ref[h]).astype(BF16)
            ds_prev = ds[rows, :]
            ds_b = ds_prev.astype(BF16)
            kz = (k.astype(F32) * zt).astype(BF16)
            dq = _dot(dsc.astype(BF16), k) + _dot_nt(drx, s_b)
            dk = _dot(dsc.T.astype(BF16), q) + _dot_nt(v, ds_b) * zt
            dv = _dot(sc.T.astype(BF16), d_r) + _dot(kz, ds_b)
            ds[rows, :] = cds[h] * ds_prev + _dot(q.astype(F32).T.astype(BF16), drx)
            dp_ref[:, h * DK:(h + 1) * DK] = unrope(dq * scale).astype(BF16)
            dp_ref[:, D + h * DK:D + (h + 1) * DK] = unrope(dk).astype(BF16)
            dp_ref[:, 2 * D + h * DV:2 * D + (h + 1) * DV] = dv.astype(BF16)

    rv = lambda c: n - 1 - c
    full3 = lambda shp: pl.BlockSpec(shp, lambda c: (0, 0, 0))
    return _pcall(
        body, name="retention_bwd", grid=(n,),
        in_specs=[pl.BlockSpec((CH, D), lambda c: (rv(c), 0)),
                  pl.BlockSpec((CH, D), lambda c: (rv(c), 1)),
                  pl.BlockSpec((CH, VW), lambda c: (rv(c), 1)),
                  pl.BlockSpec((CH, VW), lambda c: (rv(c), 0)),
                  pl.BlockSpec((None, H * DK, DV), lambda c: (rv(c), 0, 0)),
                  pl.BlockSpec((CH, DK), lambda c: (rv(c), 0)),
                  pl.BlockSpec((CH, DK), lambda c: (rv(c), 0)),
                  full3((H, CH, CH)), full3((H, CH, DV)), full3((H, CH, DK)),
                  pl.BlockSpec(memory_space=pl.ANY)],
        out_specs=pl.BlockSpec((CH, 2 * D + VW), lambda c: (rv(c), 0)),
        out_shape=jax.ShapeDtypeStruct((T, INW), BF16),
        input_output_aliases={10: 0},
        scratch_shapes=[pltpu.VMEM((H * DK, DV), F32)],
        compiler_params=_cp("arbitrary"))(proj, proj, proj, dr, states, cos_t, sin_t, dm_t, xi_t, zeta_t, dproj)


def _convo_bwd(dco, wall, off, u1, ln_g, ln_b):
    T = dco.shape[0]
    tm = min(T, 512)
    ks = D // NCHIP

    def body(d_ref, w_ref, u1_ref, g_ref, b_ref, du1_ref, dg_ref, db_ref, dcb_ref):
        i = pl.program_id(0)
        d = d_ref[...]
        du3 = jnp.concatenate([_dot(d, w_ref[s]) for s in range(NCHIP)], axis=1)
        gam = g_ref[...]
        u2, xh, rstd = _ln_fwd(u1_ref[...], gam, b_ref[...])
        sg = _sig(u2)
        du2 = du3 * (sg * (1.0 + u2 * (1.0 - sg)))
        du1 = _ln_bwd(du2, xh, rstd, gam)
        du1_ref[...] = du1
        _acc_rows(dg_ref, i == 0, _colsum(du2 * xh))
        _acc_rows(db_ref, i == 0, _colsum(du2))
        _acc_rows(dcb_ref, i == 0, _colsum(du1))

    row = pl.BlockSpec((tm, D), lambda i: (i, 0))
    vec = pl.BlockSpec((1, D), lambda i: (0, 0))
    return _pcall(
        body, name="convo_bwd", grid=(T // tm,),
        in_specs=[row, pl.BlockSpec((NCHIP, D, ks), lambda i: (0, 0, off // ks)), row, vec, vec],
        out_specs=[row, vec, vec, vec],
        out_shape=[jax.ShapeDtypeStruct((T, D), F32)] + [jax.ShapeDtypeStruct((1, D), F32)] * 3,
        compiler_params=_cp("arbitrary"))(dco, wall, u1, ln_g, ln_b)


def _conv_bwd(du1, proj, conv_k, dproj):
    T = du1.shape[0]
    tt = min(T, CONV_TT)
    nt = T // tt
    ca, cb = 6, 7

    def body(d_ref, dn_ref, a_ref, b_ref, pa_ref, pb_ref, k_ref, _, dp_ref, dk_ref, win_u, win_d):
        i = pl.program_id(0)
        a, sb = _glu(a_ref, b_ref)
        win_u[HALO:, :] = a * sb
        pa, psb = _glu(pa_ref, pb_ref, slice(tt - HALO, tt))
        win_u[0:HALO, :] = jnp.where(i > 0, pa * psb, 0.0)
        win_d[0:tt, :] = d_ref[...]
        win_d[tt:, :] = jnp.where(i < nt - 1, dn_ref[0:HALO, :], 0.0)

        @pl.when(i == 0)
        def _():
            dk_ref[...] = jnp.zeros_like(dk_ref)

        for c0 in range(0, D, CONV_CB):
            cs = slice(c0, c0 + CONV_CB)
            for r0 in range(0, tt, CONV_SB):
                acc = jnp.zeros((CONV_SB, CONV_CB), F32)
                for w in range(CONV_W):
                    st = r0 + (CONV_W - 1) - w
                    acc += win_d[st:st + CONV_SB, cs] * k_ref[w:w + 1, cs]
                aa = a_ref[r0:r0 + CONV_SB, cs].astype(F32)
                ss = _sig(b_ref[r0:r0 + CONV_SB, cs].astype(F32))
                dp_ref[r0:r0 + CONV_SB, cs] = (acc * ss).astype(BF16)
                dp_ref[r0:r0 + CONV_SB, c0 + D:c0 + D + CONV_CB] = (acc * aa * ss * (1.0 - ss)).astype(BF16)
            for w in range(CONV_W):
                acc = jnp.zeros((CONV_SB, CONV_CB), F32)
                for r0 in range(0, tt, CONV_SB):
                    st = r0 + HALO - (CONV_W - 1) + w
                    acc += win_d[r0:r0 + CONV_SB, cs] * win_u[st:st + CONV_SB, cs]
                dk_ref[w:w + 1, cs] += _colsum(acc)

    blk = lambda f, c: pl.BlockSpec((tt, D), lambda i: (f(i), c))
    cur = lambda i: i
    prv = lambda i: jnp.maximum(i - 1, 0)
    nxt = lambda i: jnp.minimum(i + 1, nt - 1)
    return _pcall(
        body, name="conv_bwd", grid=(nt,),
        in_specs=[blk(cur, 0), blk(nxt, 0), blk(cur, ca), blk(cur, cb), blk(prv, ca), blk(prv, cb),
                  pl.BlockSpec((CONV_W, D), lambda i: (0, 0)), pl.BlockSpec(memory_space=pl.ANY)],
        out_specs=[pl.BlockSpec((tt, 2 * D), lambda i: (i, 3)), pl.BlockSpec((HALO, D), lambda i: (0, 0))],
        out_shape=[jax.ShapeDtypeStruct((T, INW), BF16), jax.ShapeDtypeStruct((HALO, D), F32)],
        input_output_aliases={7: 0},
        scratch_shapes=[pltpu.VMEM((tt + HALO, D), F32), pltpu.VMEM((tt + HALO, D), F32)],
        compiler_params=_cp("arbitrary"))(du1, du1, proj, proj, proj, proj, conv_k, dproj)


def _local_step(x, target, wts, sp, pos_c, pos_sc, adam):
    T = x.shape[0]
    cos_t, sin_t = _rope_tables(T)
    dm_t, xi_t, zeta_t, cds = _decay_tables()
    wa = lambda key: wts[LOC[key][0]]
    wo = lambda key: LOC[key][1]
    _ORDER.active, _ORDER.token = True, None

    xb, xt = _cast_t(x)
    a1, b1, h1 = _ffn_up(xb, wa("g1"), wo("g1"), wo("u1"), "ffn1_up")
    z1, x1, x1b, x1t = _proj_ln(h1, wa("d1"), wo("d1"), x, sp["ln1_g"], sp["ln1_b"], 0.5, "ffn1_down_ln")
    proj = _inproj(x1b, wa("w_in"), wo("w_in"), sp["b_in"], cos_t, sin_t)
    r, ret_in, states = _retention_fwd(proj, sp["ret_gn_g"], dm_t, xi_t, zeta_t, cds)
    u1, u3 = _conv_fwd(proj, sp["conv_k"], sp["conv_b"], sp["conv_ln_g"], sp["conv_ln_b"])
    ro, co, merged = _merge(ret_in, u3, proj, wa("w_ret_o"), wo("w_ret_o"), wo("w_conv_o"))
    z2, x2, x2b, x2t = _proj_ln(merged, wa("w_out"), wo("w_out"), x1, sp["ln2_g"], sp["ln2_b"], 1.0, "out_proj_ln")
    a2, b2, h2 = _ffn_up(x2b, wa("g2"), wo("g2"), wo("u2"), "ffn2_up")
    (z3,) = _proj_ln(h2, wa("d2"), wo("d2"), x2, sp["ln3_g"], sp["ln3_b"], 0.5, "ffn2_down", want_b=False)

    sg = {}
    rs = {}
    loss, df2b, df2t, dz3, sg["ln3_g"], sg["ln3_b"] = _loss_ln_bwd(z3, sp["ln3_g"], sp["ln3_b"], target, 0.5)
    da2, db2 = _ffn_bwd_h(df2b, wa("d2"), wo("d2"), a2, b2, "ffn2_bwd_h")
    g4 = _wgrad(df2t, h2, "d2", "wgrad_d2")
    g4 = _wgrad(x2t, da2, "g2", "wgrad_g2", g4)
    g4 = _wgrad(x2t, db2, "u2", "wgrad_u2", g4)
    rs[4] = _ReduceScatter(g4, 4, pos_c, pos_sc)
    dmb, dmt, dz2, sg["ln2_g"], sg["ln2_b"] = _dx_bwd(
        [da2, db2], [wo("g2"), wo("u2")], wa("g2"), dz3, "ffn2_dx_ln", ln=(z2, sp["ln2_g"], 1.0))
    rs[4].phase2()
    g3 = _wgrad(dmt, merged, "w_out", "wgrad_out")
    dro, drot, dco, dcot, dproj = _merge_bwd(dmb, wa("w_out"), wo("w_out"), proj, ro, co)
    g3 = _wgrad(drot, ret_in, "w_ret_o", "wgrad_ret_o", g3)
    g3 = _wgrad(dcot, u3, "w_conv_o", "wgrad_conv_o", g3)
    rs[3] = _ReduceScatter(g3, 3, pos_c, pos_sc)
    dr, sg["ret_gn_g"], dproj = _reto_bwd(dro, wa("w_ret_o"), wo("w_ret_o"), r, proj, sp["ret_gn_g"], dproj)
    rs[4].phase3()
    rs[3].phase2()
    dproj = _retention_bwd(proj, dr, states, cos_t, sin_t, dm_t, xi_t, zeta_t, cds, dproj)
    du1, sg["conv_ln_g"], sg["conv_ln_b"], sg["conv_b"] = _convo_bwd(
        dco, wa("w_conv_o"), wo("w_conv_o"), u1, sp["conv_ln_g"], sp["conv_ln_b"])
    dproj, dck = _conv_bwd(du1, proj, sp["conv_k"], dproj)
    sg["conv_k"] = dck[:CONV_W]
    adam(4, rs[4].result())
    rs[3].phase3()
    g2 = _wgrad(x1t, dproj, "w_in", "wgrad_in")
    rs[2] = _ReduceScatter(g2, 2, pos_c, pos_sc)
    df1b, df1t, dz1, sg["ln1_g"], sg["ln1_b"], sg["b_in"] = _dx_bwd(
        [dproj], [wo("w_in")], wa("w_in"), dz2, "mixer_dx_ln", ln=(z1, sp["ln1_g"], 0.5), colsum=True)
    adam(3, rs[3].result())
    rs[2].phase2()
    shapes = {n: sg[n].shape for n in SMALL + ["conv_k"]}
    small_parts = _all_gather_small(_pack_small(sg, loss, SMALL_ROWS), "gather_small")
    da1, db1 = _ffn_bwd_h(df1b, wa("d1"), wo("d1"), a1, b1, "ffn1_bwd_h")
    small_sum = _sum_devices(small_parts, "sum_small")
    g1 = _wgrad(df1t, h1, "d1", "wgrad_d1")
    rs[1] = _ReduceScatter(g1, 1, pos_c, pos_sc)
    g0 = _wgrad(xt, da1, "g1", "wgrad_g1")
    g0 = _wgrad(xt, db1, "u1", "wgrad_u1", g0)
    rs[0] = _ReduceScatter(g0, 0, pos_c, pos_sc)
    rs[2].phase3()
    rs[1].phase2()
    rs[0].phase2()
    (grad_x,) = _dx_bwd([da1, db1], [wo("g1"), wo("u1")], wa("g1"), dz1, "ffn1_dx")
    adam(2, rs[2].result())
    rs[1].phase3()
    rs[0].phase3()
    adam(1, rs[1].result())
    adam(0, rs[0].result())
    _ORDER.active = False
    return grad_x, small_sum, shapes


MESH = pl.DeviceIdType.MESH
ANY = pl.BlockSpec(memory_space=pl.ANY)
HALF = D // 2


def _place():
    x, y, c = lax.axis_index("x"), lax.axis_index("y"), lax.axis_index("c")
    chips = [(1 - x, y), (x, 1 - y), (1 - x, 1 - y)]
    return x, y, c, chips


GATHER_ID = 1


def _gather_weights(wloc, name):
    w_ref = jax.new_ref(wloc, memory_space=pltpu.MemorySpace.HBM)
    o_ref = jax.empty_ref(jax.ShapeDtypeStruct((NCHIP, 2, HALF, wloc.shape[-1]), BF16),
                          memory_space=pltpu.MemorySpace.HBM)
    dma = pltpu.SemaphoreType.DMA

    @pl.kernel(mesh=plsc.ScalarSubcoreMesh(axis_name="sc", num_cores=1), name=name,
               scratch_types=(dma(()), dma((3,)), dma((3,)), dma((3,)), dma((3,))),
               compiler_params=pltpu.CompilerParams(collective_id=GATHER_ID))
    def launch(lsem, s1, r1, s2, r2):
        x, y, c, chips = _place()
        me = 2 * x + y
        sib = (x, y, 1 - c)
        barrier = pltpu.get_barrier_semaphore()
        for peer in [sib] + [(px, py, c) for px, py in chips]:
            pl.semaphore_signal(barrier, inc=1, device_id=peer, device_id_type=MESH)
        pl.semaphore_wait(barrier, 4)
        mine = pltpu.make_async_copy(w_ref, o_ref.at[me], lsem)
        mine.start()

        def rc(src, dst, ss, rs, dev):
            return pltpu.make_async_remote_copy(src_ref=src, dst_ref=dst, send_sem=ss, recv_sem=rs,
                                                device_id=dev, device_id_type=MESH)

        first = [rc(w_ref.at[c], o_ref.at[me, c], s1.at[j], r1.at[j], (*chip, c)) for j, chip in enumerate(chips)]
        for cp in first:
            cp.start()
        passed = []
        for j, (px, py) in enumerate(chips):
            slot = o_ref.at[2 * px + py, c]
            rc(slot, slot, s1.at[j], r1.at[j], (px, py, c)).wait_recv()
            cp = rc(slot, slot, s2.at[j], r2.at[j], sib)
            cp.start()
            passed.append(cp)
        for j, (px, py) in enumerate(chips):
            slot = o_ref.at[2 * px + py, 1 - c]
            rc(slot, slot, s2.at[j], r2.at[j], sib).wait_recv()
        for cp in first + passed:
            cp.wait_send()
        mine.wait()

    launch()
    return o_ref[...]


PAIR_ID = 2
CHIP_ID = 3
HBM = pltpu.MemorySpace.HBM


def _sequencer(name, collective_id, n_sems):
    dma = pltpu.SemaphoreType.DMA
    return pl.kernel(mesh=plsc.ScalarSubcoreMesh(axis_name="sc", num_cores=1), name=name,
                     scratch_types=(dma((n_sems,)), dma((n_sems,))),
                     compiler_params=pltpu.CompilerParams(collective_id=collective_id))


def _handshake(peers):
    barrier = pltpu.get_barrier_semaphore()
    for peer in peers:
        pl.semaphore_signal(barrier, inc=1, device_id=peer, device_id_type=MESH)
    pl.semaphore_wait(barrier, len(peers))


def _pair_exchange(g5, name):
    _, _, hr, cols = g5.shape
    g_ref = jax.new_ref(g5, memory_space=HBM)
    o_ref = jax.empty_ref(jax.ShapeDtypeStruct((NCHIP, hr, cols), F32), memory_space=HBM)

    @_sequencer(name, PAIR_ID, NCHIP)
    def launch(ss, rs):
        x, y, c, _ = _place()
        sib = (x, y, 1 - c)
        _handshake([sib])
        cps = [pltpu.make_async_remote_copy(src_ref=g_ref.at[j, 1 - c], dst_ref=o_ref.at[j], send_sem=ss.at[j],
                                            recv_sem=rs.at[j], device_id=sib, device_id_type=MESH)
               for j in range(NCHIP)]
        for cp in cps:
            cp.start()
        for cp in cps:
            cp.wait()

    launch()
    return o_ref[...]


RS_TR = 128


def _pair_sum(pos, g5, got, name):
    _, _, hr, cols = g5.shape

    def body(pos_ref, g_ref, r_ref, o_ref):
        o_ref[...] = (g_ref[...] + r_ref[...]).astype(BF16)

    return _pcall(
        body, name=name, scalar_prefetch=1, grid=(NCHIP, hr // RS_TR),
        in_specs=[pl.BlockSpec((None, None, RS_TR, cols), lambda j, i, p: (j, p[0], i, 0)),
                  pl.BlockSpec((None, RS_TR, cols), lambda j, i, p: (j, i, 0))],
        out_specs=pl.BlockSpec((None, RS_TR, cols), lambda j, i, p: (j, i, 0)),
        out_shape=jax.ShapeDtypeStruct((NCHIP, hr, cols), BF16),
        compiler_params=_cp("parallel", "parallel"))(pos, g5, got)


def _chip_exchange(pb, name):
    _, hr, cols = pb.shape
    p_ref = jax.new_ref(pb, memory_space=HBM)
    o_ref = jax.empty_ref(jax.ShapeDtypeStruct((3, hr, cols), BF16), memory_space=HBM)

    @_sequencer(name, CHIP_ID, 3)
    def launch(ss, rs):
        x, y, c, chips = _place()
        _handshake([(px, py, c) for px, py in chips])
        cps = [pltpu.make_async_remote_copy(src_ref=p_ref.at[2 * px + py], dst_ref=o_ref.at[j], send_sem=ss.at[j],
                                            recv_sem=rs.at[j], device_id=(px, py, c), device_id_type=MESH)
               for j, (px, py) in enumerate(chips)]
        for cp in cps:
            cp.start()
        for cp in cps:
            cp.wait()

    launch()
    return o_ref[...]


def _chip_sum(pos, g5, got, peers, name):
    _, _, hr, cols = g5.shape

    def body(pos_ref, g_ref, r_ref, p_ref, o_ref, t_ref):
        acc = g_ref[...] + r_ref[...]
        for j in range(3):
            acc += p_ref[j].astype(F32)
        o_ref[...] = acc
        t_ref[...] = jnp.zeros_like(t_ref)

    return _pcall(
        body, name=name, scalar_prefetch=1, grid=(hr // RS_TR,),
        in_specs=[pl.BlockSpec((None, None, RS_TR, cols), lambda i, p: (p[0], p[1], i, 0)),
                  pl.BlockSpec((None, RS_TR, cols), lambda i, p: (p[0], i, 0)),
                  pl.BlockSpec((3, RS_TR, cols), lambda i, p: (0, i, 0))],
        out_specs=[pl.BlockSpec((None, RS_TR, cols), lambda i, p: (p[1], i, 0)),
                   pl.BlockSpec((8, 128), lambda i, p: (0, 0))],
        out_shape=[jax.ShapeDtypeStruct((2, hr, cols), F32), jax.ShapeDtypeStruct((8, 128), F32)],
        compiler_params=_cp("arbitrary"))(pos, g5, got, peers)


def _pair_share(gsum, name):
    g_ref = jax.new_ref(gsum, memory_space=HBM)

    @_sequencer(name, PAIR_ID, 1)
    def launch(ss, rs):
        x, y, c, _ = _place()
        sib = (x, y, 1 - c)
        _handshake([sib])
        cp = pltpu.make_async_remote_copy(src_ref=g_ref.at[c], dst_ref=g_ref.at[c], send_sem=ss.at[0],
                                          recv_sem=rs.at[0], device_id=sib, device_id_type=MESH)
        cp.start()
        cp.wait_send()
        pltpu.make_async_remote_copy(src_ref=g_ref.at[1 - c], dst_ref=g_ref.at[1 - c], send_sem=ss.at[0],
                                     recv_sem=rs.at[0], device_id=sib, device_id_type=MESH).wait_recv()

    launch()
    return g_ref[...]


class _ReduceScatter:
    def __init__(self, g_arr, gi, pos_c, pos_sc):
        _, rows, cols = g_arr.shape
        self.g5 = g_arr.reshape(NCHIP, 2, rows // 2, cols)
        self.gi, self.pos_c, self.pos_sc = gi, pos_c, pos_sc
        self.got = _pair_exchange(self.g5, f"pair_exchange_{gi}")

    def phase2(self):
        pb = _pair_sum(self.pos_c, self.g5, self.got, f"pair_sum_{self.gi}")
        self.peers = _chip_exchange(pb, f"chip_exchange_{self.gi}")

    def phase3(self):
        gsum, _ = _chip_sum(self.pos_sc, self.g5, self.got, self.peers, f"chip_sum_{self.gi}")
        self.full = _pair_share(gsum, f"pair_share_{self.gi}")

    def result(self):
        _, hr, cols = self.full.shape
        return self.full.reshape(2 * hr, cols)


SMALL_ROWS = 56


ALL_ID = 4


def _all_gather_small(vec, name):
    v_ref = jax.new_ref(vec, memory_space=HBM)
    o_ref = jax.empty_ref(jax.ShapeDtypeStruct((8, SMALL_ROWS, D), F32), memory_space=HBM)

    @_sequencer(name, ALL_ID, 8)
    def launch(ss, rs):
        x, y, c, _ = _place()
        me = 4 * x + 2 * y + c
        flip = lambda v, bit: 1 - v if bit else v
        peers = [(flip(x, m >> 2), flip(y, (m >> 1) & 1), flip(c, m & 1)) for m in range(1, 8)]
        _handshake(peers)
        mine = pltpu.make_async_copy(v_ref, o_ref.at[me], ss.at[7])
        mine.start()
        cps = [pltpu.make_async_remote_copy(src_ref=v_ref, dst_ref=o_ref.at[me], send_sem=ss.at[k],
                                            recv_sem=rs.at[k], device_id=peer, device_id_type=MESH)
               for k, peer in enumerate(peers)]
        for cp in cps:
            cp.start()
        for cp in cps:
            cp.wait()
        mine.wait()

    launch()
    return o_ref[...]


def _sum_devices(parts, name):
    def body(p_ref, o_ref):
        acc = p_ref[0]
        for d in range(1, 8):
            acc += p_ref[d]
        o_ref[...] = acc

    return _pcall(
        body, name=name, grid=(SMALL_ROWS // 8,),
        in_specs=[pl.BlockSpec((8, 8, D), lambda i: (0, i, 0))],
        out_specs=pl.BlockSpec((8, D), lambda i: (i, 0)),
        out_shape=jax.ShapeDtypeStruct((SMALL_ROWS, D), F32),
        compiler_params=_cp("parallel"))(parts)


def _adamw_math(w, g, m, v):
    m2 = ADAM_B1 * m + (1.0 - ADAM_B1) * g
    v2 = ADAM_B2 * v + (1.0 - ADAM_B2) * (g * g)
    m_hat = m2 / (1.0 - ADAM_B1 ** ADAM_STEP)
    v_hat = v2 / (1.0 - ADAM_B2 ** ADAM_STEP)
    delta = -ADAM_LR * (m_hat / (jnp.sqrt(v_hat) + ADAM_EPS) + ADAM_WD * w)
    return delta, m2, v2


def _adamw(w, g, m, v, name, g_block=None):
    R, C = w.shape
    tr = R
    gw_hint = C if g_block is None else g_block[0]
    for cand in (256, 176, 128, 64, 32, 16, 8):
        if R % cand == 0 and cand * max(C, gw_hint) * 4 <= (1 << 20):
            tr = cand
            break
    gw, gi = (C, 0) if g_block is None else g_block

    def body(w_ref, g_ref, m_ref, v_ref, go_ref, d_ref, mo_ref, vo_ref):
        g = g_ref[:, 0:C]
        d, m2, v2 = _adamw_math(w_ref[...], g, m_ref[...], v_ref[...])
        go_ref[...] = g
        d_ref[...] = d
        mo_ref[...] = m2
        vo_ref[...] = v2

    spec = pl.BlockSpec((tr, C), lambda i: (i, 0))
    return _pcall(
        body, name=name, grid=(R // tr,),
        in_specs=[spec, pl.BlockSpec((tr, gw), lambda i: (i, gi)), spec, spec],
        out_specs=[spec] * 4, out_shape=[jax.ShapeDtypeStruct((R, C), F32)] * 4,
        compiler_params=_cp("parallel"))(w, g, m, v)


BIG = ["ffn1_w_gate", "ffn1_w_up", "ffn1_w_down", "w_in", "w_ret_o", "w_conv_o", "w_out",
       "ffn2_w_gate", "ffn2_w_up", "ffn2_w_down"]
SLAB = {"ffn1_w_gate": "g1", "ffn1_w_up": "u1", "ffn1_w_down": "d1", "w_in": "w_in", "w_ret_o": "w_ret_o",
        "w_conv_o": "w_conv_o", "w_out": "w_out", "ffn2_w_gate": "g2", "ffn2_w_up": "u2", "ffn2_w_down": "d2"}
TRANSPOSED = {"ffn1_w_down", "ffn2_w_down", "w_ret_o", "w_conv_o", "w_out"}
SMALL = ["ln1_g", "ln1_b", "ln2_g", "ln2_b", "ln3_g", "ln3_b", "conv_ln_g", "conv_ln_b", "conv_b",
         "ret_gn_g", "b_in"]
ORDER = ["ffn1_w_gate", "ffn1_w_up", "ffn1_w_down", "ln1_g", "ln1_b", "w_in", "b_in", "ret_gn_g", "conv_k",
         "conv_b", "conv_ln_g", "conv_ln_b", "w_ret_o", "w_conv_o", "w_out", "ln2_g", "ln2_b",
         "ffn2_w_gate", "ffn2_w_up", "ffn2_w_down", "ln3_g", "ln3_b"]


def _slab_width(name):
    return WIDTH[SLAB[name]]


def _pack_group(weights, keys):
    by_key = {SLAB[n]: n for n in BIG}
    parts = []
    for key in keys:
        w = weights[by_key[key]]
        w = w.T if by_key[key] in TRANSPOSED else w
        parts.append(jnp.pad(w, ((0, 0), (0, WIDTH[key] - w.shape[1]))))
    return jnp.concatenate(parts, axis=1).astype(BF16)


def _pack_small(vals, loss, rows):
    flat = jnp.concatenate([vals[n].reshape(-1) for n in SMALL] + [vals["conv_k"].reshape(-1), loss.reshape(-1)])
    return jnp.pad(flat, (0, rows * D - flat.shape[0])).reshape(rows, D)


def _unpack_small(arr, shapes):
    flat = arr.reshape(-1)
    out, pos = {}, 0
    for n in SMALL + ["conv_k"]:
        size = int(np.prod(shapes[n]))
        out[n] = flat[pos:pos + size].reshape(shapes[n])
        pos += size
    return out, flat[pos]


def kernel(x, ffn1_w_gate, ffn1_w_up, ffn1_w_down, ln1_g, ln1_b, w_in, b_in, ret_gn_g, conv_k, conv_b, conv_ln_g, conv_ln_b, w_ret_o, w_conv_o, w_out, ln2_g, ln2_b, ffn2_w_gate, ffn2_w_up, ffn2_w_down, ln3_g, ln3_b, loss_target, m_ffn1_w_gate, m_ffn1_w_up, m_ffn1_w_down, m_ln1_g, m_ln1_b, m_w_in, m_b_in, m_ret_gn_g, m_conv_k, m_conv_b, m_conv_ln_g, m_conv_ln_b, m_w_ret_o, m_w_conv_o, m_w_out, m_ln2_g, m_ln2_b, m_ffn2_w_gate, m_ffn2_w_up, m_ffn2_w_down, m_ln3_g, m_ln3_b, v_ffn1_w_gate, v_ffn1_w_up, v_ffn1_w_down, v_ln1_g, v_ln1_b, v_w_in, v_b_in, v_ret_gn_g, v_conv_k, v_conv_b, v_conv_ln_g, v_conv_ln_b, v_w_ret_o, v_w_conv_o, v_w_out, v_ln2_g, v_ln2_b, v_ffn2_w_gate, v_ffn2_w_up, v_ffn2_w_down, v_ln3_g, v_ln3_b):
    args = dict(locals())
    w = {n: args[n] for n in ORDER}
    m = {n: args["m_" + n] for n in ORDER}
    v = {n: args["v_" + n] for n in ORDER}
    xi, yi, ci = lax.axis_index("x"), lax.axis_index("y"), lax.axis_index("c")
    chip = 2 * xi + yi

    shards = {n: w[n][0] for n in BIG}
    wts = []
    for gi, keys in enumerate(GROUPS):
        slab = _pack_group(shards, keys)
        cols = slab.shape[1]
        wts.append(_gather_weights(slab.reshape(2, HALF, cols), f"gather_{gi}").reshape(NCHIP, D, cols))

    sp = {n: w[n] for n in SMALL}
    sp["conv_k"] = None
    kfull_shape = (CONV_W, D)
    kpad = jnp.zeros(kfull_shape, F32)
    kpad = lax.dynamic_update_slice(kpad, w["conv_k"][0, :, 0, :] * jnp.where(ci == 0, 1.0, 0.0), (0, chip * (D // NCHIP)))
    kvec = jnp.pad(kpad.reshape(-1), (0, SMALL_ROWS * D - CONV_W * D)).reshape(SMALL_ROWS, D)
    kall = _sum_devices(_all_gather_small(kvec, "gather_conv_k"), "sum_conv_k")
    sp["conv_k"] = kall.reshape(-1)[:CONV_W * D].reshape(kfull_shape)
    pos_c = jnp.reshape(ci, (1,)).astype(jnp.int32)
    pos_sc = jnp.stack([chip, ci]).astype(jnp.int32)
    out = {}

    def adam(gi, slab):
        for n in BIG:
            (g_of, off), width = LOC[SLAB[n]], _slab_width(n)
            if g_of != gi:
                continue
            w2 = w[n][0]
            if n in TRANSPOSED:
                res = _adamw(w2, slab[:, off:off + w2.shape[0]].T, m[n][0], v[n][0], "adamw_" + n)
            else:
                res = _adamw(w2, slab, m[n][0], v[n][0], "adamw_" + n, g_block=(width, off // width))
            out[n] = [r[None] for r in res]

    grad_x, small_sum, shapes = _local_step(x[0], loss_target[0], wts, sp, pos_c, pos_sc, adam)
    small, total = _unpack_small(small_sum, shapes)

    for n in SMALL:
        res = _adamw(w[n], small[n], m[n], v[n], "adamw_" + n)
        out[n] = list(res)
    gk = lax.dynamic_slice(small["conv_k"], (0, chip * (D // NCHIP)), (CONV_W, D // NCHIP))
    res = _adamw(w["conv_k"][0, :, 0, :], gk, m["conv_k"][0, :, 0, :], v["conv_k"][0, :, 0, :], "adamw_conv_k")
    out["conv_k"] = [r[None, :, None, :] for r in res]

    grads = [out[n][0] for n in ORDER]
    deltas = [out[n][1] for n in ORDER]
    new_m = [out[n][2] for n in ORDER]
    new_v = [out[n][3] for n in ORDER]
    return (total, grad_x[None], *grads, *deltas, *new_m, *new_v)
```

```python
import dataclasses
import functools

import numpy as np
import jax
import jax.numpy as jnp
from jax import lax
from jax.experimental import pallas as pl
from jax.experimental.pallas import tpu as pltpu
from jax.experimental.pallas import tpu_sc as plsc

F32 = jnp.float32
BF16 = jnp.bfloat16

D = 1024
FS = 704
FSP = 768
FP = 4 * FSP
H = 8
DK = 128
DV = 256
CH = 128
VW = H * DV
INW = 10240
INS = INW // 4
CONV_W = 31
HALO = 32
EPS = 1e-5
ALPHA = 2.0 ** 0.25
ROPE_BASE = 10000.0
NCHIP = 4

ADAM_LR, ADAM_B1, ADAM_B2, ADAM_EPS, ADAM_WD, ADAM_STEP = 0.001, 0.9, 0.999, 1e-08, 0.01, 10

OFF = {"w_in": 0, "w_ret_o": 2560, "g1": 3072, "u1": 3840, "d1": 4608,
       "g2": 5376, "u2": 6144, "d2": 6912, "w_conv_o": 7680, "w_out": 7936}
WCOLS = 8192
WIDTH = {"w_in": INS, "w_ret_o": VW // NCHIP, "w_conv_o": D // NCHIP, "w_out": D // NCHIP,
         "g1": FSP, "u1": FSP, "d1": FSP, "g2": FSP, "u2": FSP, "d2": FSP}
GROUPS = (("g1", "u1"), ("d1",), ("w_in",), ("w_ret_o", "w_conv_o", "w_out"), ("g2", "u2", "d2"))
LOC = {}
for _gi, _keys in enumerate(GROUPS):
    _off = 0
    for _k in _keys:
        LOC[_k] = (_gi, _off)
        _off += WIDTH[_k]
GCOLS = [sum(WIDTH[k] for k in keys) for keys in GROUPS]
VMEM_LIMIT = 56 << 20


def _cp(*sem, **kw):
    return pltpu.CompilerParams(dimension_semantics=sem, vmem_limit_bytes=VMEM_LIMIT, **kw)


class _ProgramOrder:
    def __init__(self):
        self.active = False
        self.token = None


_ORDER = _ProgramOrder()


def _pcall(body, *, in_specs, scalar_prefetch=0, **kw):
    def call(*args):
        dep = _ORDER.token if _ORDER.active else None
        specs, fn = list(in_specs), body
        if dep is not None:
            n = len(args)

            def fn(*refs):
                return body(*refs[:n], *refs[n + 1:])

            specs.append(pl.BlockSpec(memory_space=pl.ANY))
            args = (*args, dep)
        params = dict(kw)
        if scalar_prefetch:
            params["grid_spec"] = pltpu.PrefetchScalarGridSpec(
                num_scalar_prefetch=scalar_prefetch, grid=params.pop("grid"), in_specs=specs,
                out_specs=params.pop("out_specs"))
        else:
            params["in_specs"] = specs
        out = pl.pallas_call(fn, **params)(*args)
        if _ORDER.active:
            _ORDER.token = jax.tree.leaves(out)[-1]
        return out

    return call


def _resident(shape, col_block):
    lead = (0,) * (len(shape) - 1)
    return pl.BlockSpec(shape, lambda *_: (*lead, col_block), pipeline_mode=pl.Buffered(1))


def _sig(x):
    return 1.0 / (1.0 + jnp.exp(-x))


def _dot(a, b):
    return jnp.dot(a, b, preferred_element_type=F32)


def _dot_nt(a, b):
    return lax.dot_general(a, b, (((1,), (1,)), ((), ())), preferred_element_type=F32)


def _ln_fwd(z, g, b):
    mu = jnp.mean(z, axis=-1, keepdims=True)
    xc = z - mu
    var = jnp.mean(xc * xc, axis=-1, keepdims=True)
    rstd = lax.rsqrt(var + EPS)
    xh = xc * rstd
    return xh * g + b, xh, rstd


def _ln_bwd(dy, xh, rstd, g):
    dxh = dy * g
    m1 = jnp.mean(dxh, axis=-1, keepdims=True)
    m2 = jnp.mean(dxh * xh, axis=-1, keepdims=True)
    return rstd * (dxh - m1 - xh * m2)


def _colsum(x):
    return jnp.sum(x, axis=0, keepdims=True)


def _acc_rows(ref, first, val):
    @pl.when(first)
    def _():
        ref[...] = val

    @pl.when(jnp.logical_not(first))
    def _():
        ref[...] += val


def _rope_tables(T):
    half = DK // 2
    freqs = ROPE_BASE ** (-np.arange(half, dtype=np.float32) / half)
    ang = (np.arange(T, dtype=np.float32)[:, None] * freqs[None, :]).astype(np.float32)
    cos, sin = np.cos(ang), np.sin(ang)
    return (jnp.asarray(np.concatenate([cos, cos], 1), F32),
            jnp.asarray(np.concatenate([-sin, sin], 1), F32))


def _decay_tables():
    h = np.arange(H, dtype=np.float64)
    log_g = np.log(1.0 - np.exp2(-5.0 - h))
    idx = np.arange(CH, dtype=np.float64)
    diff = idx[:, None] - idx[None, :]
    dm = np.where(diff[None] >= 0, np.exp(np.maximum(diff, 0.0)[None] * log_g[:, None, None]), 0.0)
    xi = np.exp((idx[None, :] + 1.0) * log_g[:, None])
    zeta = np.exp((CH - 1.0 - idx)[None, :] * log_g[:, None])
    cd = np.exp(CH * log_g)
    xi_t = np.broadcast_to(xi[:, :, None], (H, CH, DV))
    zeta_t = np.broadcast_to(zeta[:, :, None], (H, CH, DK))
    return (jnp.asarray(dm, F32), jnp.asarray(xi_t, F32), jnp.asarray(zeta_t, F32),
            [float(v) for v in cd])


def _cast_t(x):
    T = x.shape[0]
    tm = min(T, 512)

    def body(x_ref, xb_ref, xt_ref):
        v = x_ref[...]
        xb_ref[...] = v.astype(BF16)
        xt_ref[...] = v.T.astype(BF16)

    return _pcall(
        body, name="cast_t", grid=(T // tm,),
        in_specs=[pl.BlockSpec((tm, D), lambda i: (i, 0))],
        out_specs=[pl.BlockSpec((tm, D), lambda i: (i, 0)), pl.BlockSpec((D, tm), lambda i: (0, i))],
        out_shape=[jax.ShapeDtypeStruct((T, D), BF16), jax.ShapeDtypeStruct((D, T), BF16)],
        compiler_params=_cp("parallel"))(x)


def _ffn_up(xb, wall, og, ou, name):
    T = xb.shape[0]
    tm = min(T, 512)
    assert ou == og + FSP

    def body(x_ref, w_ref, a_ref, b_ref, h_ref):
        x = x_ref[...]
        for s in range(NCHIP):
            cols = slice(s * FSP, (s + 1) * FSP)
            a = _dot(x, w_ref[s, :, 0:FSP])
            b = _dot(x, w_ref[s, :, FSP:2 * FSP])
            a_ref[:, cols] = a.astype(BF16)
            b_ref[:, cols] = b.astype(BF16)
            h_ref[:, cols] = (a * _sig(a) * b).astype(BF16)

    ospec = pl.BlockSpec((tm, FP), lambda i: (i, 0))
    return _pcall(
        body, name=name, grid=(T // tm,),
        in_specs=[pl.BlockSpec((tm, D), lambda i: (i, 0)), _resident((NCHIP, D, 2 * FSP), og // (2 * FSP))],
        out_specs=[ospec] * 3, out_shape=[jax.ShapeDtypeStruct((T, FP), BF16)] * 3,
        compiler_params=_cp("parallel"))(xb, wall)


def _proj_ln(hb, wall, off, res, g, b, coef, name, want_b=True):
    T, K = hb.shape
    ks = K // NCHIP
    tm = min(T, 256)

    def body(h_ref, w_ref, r_ref, g_ref, b_ref, z_ref, *rest):
        acc = _dot_nt(h_ref[:, 0:ks], w_ref[0])
        for s in range(1, NCHIP):
            acc += _dot_nt(h_ref[:, s * ks:(s + 1) * ks], w_ref[s])
        z = ALPHA * r_ref[...] + coef * acc
        z_ref[...] = z
        if want_b:
            y, _, _ = _ln_fwd(z, g_ref[...], b_ref[...])
            y_ref, yb_ref, yt_ref = rest
            y_ref[...] = y
            yb_ref[...] = y.astype(BF16)
            yt_ref[...] = y.T.astype(BF16)

    row = pl.BlockSpec((tm, D), lambda i: (i, 0))
    vec = pl.BlockSpec((1, D), lambda i: (0, 0))
    out_specs = [row]
    out_shape = [jax.ShapeDtypeStruct((T, D), F32)]
    if want_b:
        out_specs += [row, row, pl.BlockSpec((D, tm), lambda i: (0, i))]
        out_shape += [jax.ShapeDtypeStruct((T, D), F32), jax.ShapeDtypeStruct((T, D), BF16),
                      jax.ShapeDtypeStruct((D, T), BF16)]
    return _pcall(
        body, name=name, grid=(T // tm,),
        in_specs=[pl.BlockSpec((tm, K), lambda i: (i, 0)),
                  pl.BlockSpec((NCHIP, D, ks), lambda i: (0, 0, off // ks)), row, vec, vec],
        out_specs=out_specs, out_shape=out_shape,
        compiler_params=_cp("parallel"))(hb, wall, res, g, b)


def _inproj(xb, wall, off, b_in, cos_t, sin_t):
    T = xb.shape[0]
    tm, tn = min(T, 256), 512
    assert off == 0

    def body(x_ref, w_ref, bias_ref, cos_ref, sin_ref, o_ref):
        x = x_ref[...]
        c = cos_ref[...]
        s = sin_ref[...]
        for n0 in range(0, INW, tn):
            chip, c0 = divmod(n0, INS)
            acc = _dot(x, w_ref[chip, :, c0:c0 + tn]) + bias_ref[:, n0:n0 + tn]
            if n0 >= 2 * D:
                o_ref[:, n0:n0 + tn] = acc.astype(BF16)
                continue
            scale = DK ** -0.5 if n0 < D else 1.0
            for hh in range(tn // DK):
                xh = acc[:, hh * DK:(hh + 1) * DK]
                o = (xh * c + pltpu.roll(xh, DK // 2, 1) * s) * scale
                o_ref[:, n0 + hh * DK:n0 + (hh + 1) * DK] = o.astype(BF16)

    return _pcall(
        body, name="inproj", grid=(T // tm,),
        in_specs=[pl.BlockSpec((tm, D), lambda i: (i, 0)),
                  _resident((NCHIP, D, INS), 0),
                  pl.BlockSpec((1, INW), lambda i: (0, 0)),
                  pl.BlockSpec((tm, DK), lambda i: (i, 0)),
                  pl.BlockSpec((tm, DK), lambda i: (i, 0))],
        out_specs=pl.BlockSpec((tm, INW), lambda i: (i, 0)),
        out_shape=jax.ShapeDtypeStruct((T, INW), BF16),
        compiler_params=_cp("parallel"))(xb, wall, b_in, cos_t, sin_t)


def _retention_fwd(proj, gn_g, dm_t, xi_t, zeta_t, cds):
    T = proj.shape[0]
    n = T // CH

    def body(q_ref, k_ref, v_ref, g_ref, gn_ref, dm_ref, xi_ref, zt_ref, r_ref, ri_ref, st_ref, state):
        @pl.when(pl.program_id(0) == 0)
        def _():
            state[...] = jnp.zeros_like(state)

        for h in range(H):
            q = q_ref[:, h * DK:(h + 1) * DK]
            k = k_ref[:, h * DK:(h + 1) * DK]
            v = v_ref[:, h * DV:(h + 1) * DV]
            rows = slice(h * DK, (h + 1) * DK)
            s_prev = state[rows, :]
            s_b = s_prev.astype(BF16)
            st_ref[rows, :] = s_b
            sc = _dot_nt(q, k) * dm_ref[h]
            r = _dot(sc.astype(BF16), v) + _dot(q, s_b) * xi_ref[h]
            kz = k.astype(F32) * zt_ref[h]
            state[rows, :] = cds[h] * s_prev + _dot(kz.T.astype(BF16), v)
            cols = slice(h * DV, (h + 1) * DV)
            r_ref[:, cols] = r
            mu = jnp.mean(r, axis=-1, keepdims=True)
            xc = r - mu
            var = jnp.mean(xc * xc, axis=-1, keepdims=True)
            y = xc * lax.rsqrt(var + EPS) * gn_ref[:, cols]
            g = g_ref[:, cols].astype(F32)
            ri_ref[:, cols] = (g * _sig(g) * y).astype(BF16)

    full3 = lambda shp: pl.BlockSpec(shp, lambda c: (0, 0, 0))
    return _pcall(
        body, name="retention_fwd", grid=(n,),
        in_specs=[pl.BlockSpec((CH, D), lambda c: (c, 0)),
                  pl.BlockSpec((CH, D), lambda c: (c, 1)),
                  pl.BlockSpec((CH, VW), lambda c: (c, 1)),
                  pl.BlockSpec((CH, VW), lambda c: (c, 2)),
                  pl.BlockSpec((1, VW), lambda c: (0, 0)),
                  full3((H, CH, CH)), full3((H, CH, DV)), full3((H, CH, DK))],
        out_specs=[pl.BlockSpec((CH, VW), lambda c: (c, 0)), pl.BlockSpec((CH, VW), lambda c: (c, 0)),
                   pl.BlockSpec((None, H * DK, DV), lambda c: (c, 0, 0))],
        out_shape=[jax.ShapeDtypeStruct((T, VW), F32), jax.ShapeDtypeStruct((T, VW), BF16),
                   jax.ShapeDtypeStruct((n, H * DK, DV), BF16)],
        scratch_shapes=[pltpu.VMEM((H * DK, DV), F32)],
        compiler_params=_cp("arbitrary"))(proj, proj, proj, proj, gn_g, dm_t, xi_t, zeta_t)


CONV_TT = 256
CONV_SB = 64
CONV_CB = 256


def _glu(a_ref, b_ref, rows=slice(None)):
    a = a_ref[rows, :].astype(F32)
    sb = _sig(b_ref[rows, :].astype(F32))
    return a, sb


def _conv_fwd(proj, conv_k, conv_b, ln_g, ln_b):
    T = proj.shape[0]
    tt = min(T, CONV_TT)
    ca, cb = 6 * D // D, 7 * D // D

    def body(a_ref, b_ref, pa_ref, pb_ref, k_ref, cb_ref, g_ref, bb_ref, u1_ref, u3_ref, win):
        i = pl.program_id(0)
        a, sb = _glu(a_ref, b_ref)
        win[HALO:, :] = a * sb
        pa, psb = _glu(pa_ref, pb_ref, slice(tt - HALO, tt))
        win[0:HALO, :] = jnp.where(i > 0, pa * psb, 0.0)
        for c0 in range(0, D, CONV_CB):
            cs = slice(c0, c0 + CONV_CB)
            for r0 in range(0, tt, CONV_SB):
                acc = jnp.zeros((CONV_SB, CONV_CB), F32)
                for w in range(CONV_W):
                    st = r0 + HALO - (CONV_W - 1) + w
                    acc += win[st:st + CONV_SB, cs] * k_ref[w:w + 1, cs]
                u1_ref[r0:r0 + CONV_SB, cs] = acc + cb_ref[:, cs]
        u2, _, _ = _ln_fwd(u1_ref[...], g_ref[...], bb_ref[...])
        u3_ref[...] = (u2 * _sig(u2)).astype(BF16)

    vec = pl.BlockSpec((1, D), lambda i: (0, 0))
    row = pl.BlockSpec((tt, D), lambda i: (i, 0))
    return _pcall(
        body, name="conv_fwd", grid=(T // tt,),
        in_specs=[pl.BlockSpec((tt, D), lambda i: (i, ca)), pl.BlockSpec((tt, D), lambda i: (i, cb)),
                  pl.BlockSpec((tt, D), lambda i: (jnp.maximum(i - 1, 0), ca)),
                  pl.BlockSpec((tt, D), lambda i: (jnp.maximum(i - 1, 0), cb)),
                  pl.BlockSpec((CONV_W, D), lambda i: (0, 0)), vec, vec, vec],
        out_specs=[row, row],
        out_shape=[jax.ShapeDtypeStruct((T, D), F32), jax.ShapeDtypeStruct((T, D), BF16)],
        scratch_shapes=[pltpu.VMEM((tt + HALO, D), F32)],
        compiler_params=_cp("parallel"))(proj, proj, proj, proj, conv_k, conv_b, ln_g, ln_b)


def _merge(ret_in, u3, proj, wall, off_r, off_c):
    T = ret_in.shape[0]
    tm = min(T, 512)
    kr, kc = VW // NCHIP, D // NCHIP

    def body(ri_ref, u3_ref, gr_ref, gc_ref, wr_ref, wc_ref, ro_ref, co_ref, m_ref):
        ro = _dot_nt(ri_ref[:, 0:kr], wr_ref[0])
        co = _dot_nt(u3_ref[:, 0:kc], wc_ref[0])
        for s in range(1, NCHIP):
            ro += _dot_nt(ri_ref[:, s * kr:(s + 1) * kr], wr_ref[s])
            co += _dot_nt(u3_ref[:, s * kc:(s + 1) * kc], wc_ref[s])
        ro_ref[...] = ro.astype(BF16)
        co_ref[...] = co.astype(BF16)
        m = _sig(gr_ref[...].astype(F32)) * ro + _sig(gc_ref[...].astype(F32)) * co
        m_ref[...] = m.astype(BF16)

    row = pl.BlockSpec((tm, D), lambda i: (i, 0))
    return _pcall(
        body, name="merge", grid=(T // tm,),
        in_specs=[pl.BlockSpec((tm, VW), lambda i: (i, 0)), row,
                  pl.BlockSpec((tm, D), lambda i: (i, 8)), pl.BlockSpec((tm, D), lambda i: (i, 9)),
                  pl.BlockSpec((NCHIP, D, kr), lambda i: (0, 0, off_r // kr)),
                  pl.BlockSpec((NCHIP, D, kc), lambda i: (0, 0, off_c // kc))],
        out_specs=[row] * 3, out_shape=[jax.ShapeDtypeStruct((T, D), BF16)] * 3,
        compiler_params=_cp("parallel"))(ret_in, u3, proj, proj, wall, wall)


def _loss_ln_bwd(z, g, b, target, coef):
    T = z.shape[0]
    tm = min(T, 256)
    nt = T // tm

    def body(z_ref, g_ref, b_ref, t_ref, loss_ref, dzb_ref, dzt_ref, dz_ref, dg_ref, db_ref, lacc):
        i = pl.program_id(0)
        gam = g_ref[...]
        y, xh, rstd = _ln_fwd(z_ref[...], gam, b_ref[...])
        e = y - t_ref[...]
        part = _colsum(e * e)
        _acc_rows(lacc, i == 0, part)
        dy = e * (1.0 / D)
        dz = _ln_bwd(dy, xh, rstd, gam)
        dz_ref[...] = dz
        dzc = coef * dz
        dzb_ref[...] = dzc.astype(BF16)
        dzt_ref[...] = dzc.T.astype(BF16)
        _acc_rows(dg_ref, i == 0, _colsum(dy * xh))
        _acc_rows(db_ref, i == 0, _colsum(dy))

        @pl.when(i == nt - 1)
        def _():
            loss_ref[...] = (0.5 / D) * jnp.sum(lacc[...], axis=1, keepdims=True)

    row = pl.BlockSpec((tm, D), lambda i: (i, 0))
    vec = pl.BlockSpec((1, D), lambda i: (0, 0))
    return _pcall(
        body, name="loss_ln_bwd", grid=(nt,),
        in_specs=[row, vec, vec, row],
        out_specs=[pl.BlockSpec((1, 1), lambda i: (0, 0)), row, pl.BlockSpec((D, tm), lambda i: (0, i)),
                   row, vec, vec],
        out_shape=[jax.ShapeDtypeStruct((1, 1), F32), jax.ShapeDtypeStruct((T, D), BF16),
                   jax.ShapeDtypeStruct((D, T), BF16), jax.ShapeDtypeStruct((T, D), F32),
                   jax.ShapeDtypeStruct((1, D), F32), jax.ShapeDtypeStruct((1, D), F32)],
        scratch_shapes=[pltpu.VMEM((1, D), F32)],
        compiler_params=_cp("arbitrary"))(z, g, b, target)


def _ffn_bwd_h(dfb, wall, od, a, b, name):
    T = dfb.shape[0]
    tm = min(T, 512)

    def body(d_ref, w_ref, a_ref, b_ref, da_ref, db_ref):
        d = d_ref[...]
        for s in range(NCHIP):
            cols = slice(s * FSP, (s + 1) * FSP)
            dh = _dot(d, w_ref[s])
            a = a_ref[:, cols].astype(F32)
            sg = _sig(a)
            da_ref[:, cols] = (dh * b_ref[:, cols].astype(F32) * (sg * (1.0 + a * (1.0 - sg)))).astype(BF16)
            db_ref[:, cols] = (dh * a * sg).astype(BF16)

    ospec = pl.BlockSpec((tm, FP), lambda i: (i, 0))
    return _pcall(
        body, name=name, grid=(T // tm,),
        in_specs=[pl.BlockSpec((tm, D), lambda i: (i, 0)), _resident((NCHIP, D, FSP), od // FSP), ospec, ospec],
        out_specs=[ospec] * 2, out_shape=[jax.ShapeDtypeStruct((T, FP), BF16)] * 2,
        compiler_params=_cp("parallel"))(dfb, wall, a, b)


def _dx_bwd(lhs, offs, wall, dz_next, name, ln=None, colsum=False):
    T, K = lhs[0].shape
    ks = K // NCHIP
    nl = len(lhs)
    assert list(offs) == [l * ks for l in range(nl)]
    tm = min(T, 512 if K <= FP else 256)

    def body(*refs):
        l_refs = refs[:nl]
        w_ref = refs[nl]
        dzn_ref = refs[nl + 1]
        pos = nl + 2
        if ln is not None:
            z_ref, g_ref = refs[pos:pos + 2]
            pos += 2
        outs = refs[pos:]
        i = pl.program_id(0)
        acc = None
        for s in range(NCHIP):
            rows = slice(s * ks, (s + 1) * ks)
            for l in range(nl):
                part = _dot_nt(l_refs[l][:, rows], w_ref[s, :, l * ks:(l + 1) * ks])
                acc = part if acc is None else acc + part
            if colsum:
                _acc_rows(outs[-1].at[:, rows], i == 0, _colsum(l_refs[0][:, rows].astype(F32)))
        dy = acc + ALPHA * dzn_ref[...]
        if ln is None:
            outs[0][...] = dy
        else:
            gam = g_ref[...]
            _, xh, rstd = _ln_fwd(z_ref[...], gam, 0.0)
            dz = _ln_bwd(dy, xh, rstd, gam)
            dzc = ln[2] * dz
            outs[0][...] = dzc.astype(BF16)
            outs[1][...] = dzc.T.astype(BF16)
            outs[2][...] = dz
            _acc_rows(outs[3], i == 0, _colsum(dy * xh))
            _acc_rows(outs[4], i == 0, _colsum(dy))

    row = pl.BlockSpec((tm, D), lambda i: (i, 0))
    vec = pl.BlockSpec((1, D), lambda i: (0, 0))
    in_specs = [pl.BlockSpec((tm, K), lambda i: (i, 0))] * nl + [_resident((NCHIP, D, nl * ks), 0), row]
    args = list(lhs) + [wall, dz_next]
    if ln is None:
        out_specs = [row]
        out_shape = [jax.ShapeDtypeStruct((T, D), F32)]
    else:
        in_specs += [row, vec]
        args += [ln[0], ln[1]]
        out_specs = [row, pl.BlockSpec((D, tm), lambda i: (0, i)), row, vec, vec]
        out_shape = [jax.ShapeDtypeStruct((T, D), BF16), jax.ShapeDtypeStruct((D, T), BF16),
                     jax.ShapeDtypeStruct((T, D), F32), jax.ShapeDtypeStruct((1, D), F32),
                     jax.ShapeDtypeStruct((1, D), F32)]
    if colsum:
        out_specs += [pl.BlockSpec((1, K), lambda i: (0, 0))]
        out_shape += [jax.ShapeDtypeStruct((1, K), F32)]
    return _pcall(
        body, name=name, grid=(T // tm,), in_specs=in_specs, out_specs=out_specs, out_shape=out_shape,
        compiler_params=_cp("arbitrary"))(*args)


def _wgrad(lhs_t, rhs, key, name, g_all=None):
    T, N = rhs.shape
    tn = 256
    nps = N // NCHIP // tn
    off = LOC[key][1]
    cols = GCOLS[LOC[key][0]]

    def body(*refs):
        l_ref, r_ref, o_ref, t_ref = refs[0], refs[1], refs[-2], refs[-1]
        o_ref[...] = _dot(l_ref[...], r_ref[...])
        t_ref[...] = jnp.zeros_like(t_ref)

    in_specs = [pl.BlockSpec((D, T), lambda j: (0, 0)), pl.BlockSpec((T, tn), lambda j: (0, j))]
    args = [lhs_t, rhs]
    aliases = {}
    if g_all is not None:
        in_specs.append(pl.BlockSpec(memory_space=pl.ANY))
        args.append(g_all)
        aliases = {2: 0}
    return _pcall(
        body, name=name, grid=(N // tn,), in_specs=in_specs,
        out_specs=[pl.BlockSpec((None, D, tn), lambda j: (j // nps, 0, off // tn + j % nps)),
                   pl.BlockSpec((8, 128), lambda j: (0, 0))],
        out_shape=[jax.ShapeDtypeStruct((NCHIP, D, cols), F32), jax.ShapeDtypeStruct((8, 128), F32)],
        input_output_aliases=aliases,
        compiler_params=_cp("arbitrary"))(*args)[0]


def _merge_bwd(dmb, wall, off, proj, ro, co):
    T = dmb.shape[0]
    tm = min(T, 512)
    ks = D // NCHIP

    def body(d_ref, w_ref, gr_ref, gc_ref, ro_ref, co_ref, dro_ref, drot_ref, dco_ref, dcot_ref, dp_ref):
        d = d_ref[...]
        dmg = jnp.concatenate([_dot(d, w_ref[s]) for s in range(NCHIP)], axis=1)
        sr = _sig(gr_ref[...].astype(F32))
        sc = _sig(gc_ref[...].astype(F32))
        dro = dmg * sr
        dco = dmg * sc
        dro_ref[...] = dro.astype(BF16)
        drot_ref[...] = dro.T.astype(BF16)
        dco_ref[...] = dco.astype(BF16)
        dcot_ref[...] = dco.T.astype(BF16)
        dp_ref[:, 0:D] = (dmg * ro_ref[...].astype(F32) * sr * (1.0 - sr)).astype(BF16)
        dp_ref[:, D:2 * D] = (dmg * co_ref[...].astype(F32) * sc * (1.0 - sc)).astype(BF16)

    row = pl.BlockSpec((tm, D), lambda i: (i, 0))
    col = pl.BlockSpec((D, tm), lambda i: (0, i))
    return _pcall(
        body, name="merge_bwd", grid=(T // tm,),
        in_specs=[row, pl.BlockSpec((NCHIP, D, ks), lambda i: (0, 0, off // ks)),
                  pl.BlockSpec((tm, D), lambda i: (i, 8)), pl.BlockSpec((tm, D), lambda i: (i, 9)), row, row],
        out_specs=[row, col, row, col, pl.BlockSpec((tm, 2 * D), lambda i: (i, 4))],
        out_shape=[jax.ShapeDtypeStruct((T, D), BF16), jax.ShapeDtypeStruct((D, T), BF16),
                   jax.ShapeDtypeStruct((T, D), BF16), jax.ShapeDtypeStruct((D, T), BF16),
                   jax.ShapeDtypeStruct((T, INW), BF16)],
        compiler_params=_cp("parallel"))(dmb, wall, proj, proj, ro, co)


def _reto_bwd(dro, wall, off, r, proj, gn_g, dproj):
    T = dro.shape[0]
    tm = min(T, 512)
    hps = H // NCHIP

    def body(d_ref, w_ref, r_ref, g_ref, gn_ref, _, dr_ref, dgn_ref, dp_ref):
        i = pl.program_id(1)
        dri = _dot(d_ref[...], w_ref[...])
        rr = r_ref[...]
        mu = jnp.mean(rr, axis=-1, keepdims=True)
        xc = rr - mu
        var = jnp.mean(xc * xc, axis=-1, keepdims=True)
        rstd = lax.rsqrt(var + EPS)
        rn = xc * rstd
        gn = gn_ref[...]
        g = g_ref[...].astype(F32)
        sg = _sig(g)
        dy = dri * (g * sg)
        dp_ref[...] = (dri * (rn * gn) * (sg * (1.0 + g * (1.0 - sg)))).astype(BF16)
        _acc_rows(dgn_ref, i == 0, _colsum(dy * rn))
        dr_ref[...] = _ln_bwd(dy, rn, rstd, gn).astype(BF16)

    return _pcall(
        body, name="reto_bwd", grid=(H, T // tm),
        in_specs=[pl.BlockSpec((tm, D), lambda j, i: (i, 0)),
                  pl.BlockSpec((None, D, DV), lambda j, i: (j // hps, 0, off // DV + j % hps)),
                  pl.BlockSpec((tm, DV), lambda j, i: (i, j)),
                  pl.BlockSpec((tm, DV), lambda j, i: (i, 2 * VW // DV + j)),
                  pl.BlockSpec((1, DV), lambda j, i: (0, j)),
                  pl.BlockSpec(memory_space=pl.ANY)],
        out_specs=[pl.BlockSpec((tm, DV), lambda j, i: (i, j)), pl.BlockSpec((1, DV), lambda j, i: (0, j)),
                   pl.BlockSpec((tm, DV), lambda j, i: (i, 2 * VW // DV + j))],
        out_shape=[jax.ShapeDtypeStruct((T, VW), BF16), jax.ShapeDtypeStruct((1, VW), F32),
                   jax.ShapeDtypeStruct((T, INW), BF16)],
        input_output_aliases={5: 2},
        compiler_params=_cp("arbitrary", "arbitrary"))(dro, wall, r, proj, gn_g, dproj)


def _retention_bwd(proj, dr, states, cos_t, sin_t, dm_t, xi_t, zeta_t, cds, dproj):
    T = proj.shape[0]
    n = T // CH
    scale = DK ** -0.5

    def body(q_ref, k_ref, v_ref, dr_ref, st_ref, cos_ref, sin_ref, dm_ref, xi_ref, zt_ref, _, dp_ref, ds):
        @pl.when(pl.program_id(0) == 0)
        def _():
            ds[...] = jnp.zeros_like(ds)

        cos = cos_ref[...]
        sin = sin_ref[...]

        def unrope(d):
            return d * cos + pltpu.roll(d * sin, DK // 2, 1)

        for h in range(H):
            q = q_ref[:, h * DK:(h + 1) * DK]
            k = k_ref[:, h * DK:(h + 1) * DK]
            v = v_ref[:, h * DV:(h + 1) * DV]
            d_r = dr_ref[:, h * DV:(h + 1) * DV]
            rows = slice(h * DK, (h + 1) * DK)
            s_b = st_ref[rows, :]
            dm = dm_ref[h]
            zt = zt_ref[h]
            sc = _dot_nt(q, k) * dm
            dsc = _dot_nt(d_r, v) * dm
            drx = (d_r.astype(F32) * xi_ref[h]).astype(BF16)
            ds_prev = ds[rows, :]
            ds_b = ds_prev.astype(BF16)
            kz = (k.astype(F32) * zt).astype(BF16)
            dq = _dot(dsc.astype(BF16), k) + _dot_nt(drx, s_b)
            dk = _dot(dsc.T.astype(BF16), q) + _dot_nt(v, ds_b) * zt
            dv = _dot(sc.T.astype(BF16), d_r) + _dot(kz, ds_b)
            ds[rows, :] = cds[h] * ds_prev + _dot(q.astype(F32).T.astype(BF16), drx)
            dp_ref[:, h * DK:(h + 1) * DK] = unrope(dq * scale).astype(BF16)
            dp_ref[:, D + h * DK:D + (h + 1) * DK] = unrope(dk).astype(BF16)
            dp_ref[:, 2 * D + h * DV:2 * D + (h + 1) * DV] = dv.astype(BF16)

    rv = lambda c: n - 1 - c
    full3 = lambda shp: pl.BlockSpec(shp, lambda c: (0, 0, 0))
    return _pcall(
        body, name="retention_bwd", grid=(n,),
        in_specs=[pl.BlockSpec((CH, D), lambda c: (rv(c), 0)),
                  pl.BlockSpec((CH, D), lambda c: (rv(c), 1)),
                  pl.BlockSpec((CH, VW), lambda c: (rv(c), 1)),
                  pl.BlockSpec((CH, VW), lambda c: (rv(c), 0)),
                  pl.BlockSpec((None, H * DK, DV), lambda c: (rv(c), 0, 0)),
                  pl.BlockSpec((CH, DK), lambda c: (rv(c), 0)),
                  pl.BlockSpec((CH, DK), lambda c: (rv(c), 0)),
                  full3((H, CH, CH)), full3((H, CH, DV)), full3((H, CH, DK)),
                  pl.BlockSpec(memory_space=pl.ANY)],
        out_specs=pl.BlockSpec((CH, 2 * D + VW), lambda c: (rv(c), 0)),
        out_shape=jax.ShapeDtypeStruct((T, INW), BF16),
        input_output_aliases={10: 0},
        scratch_shapes=[pltpu.VMEM((H * DK, DV), F32)],
        compiler_params=_cp("arbitrary"))(proj, proj, proj, dr, states, cos_t, sin_t, dm_t, xi_t, zeta_t, dproj)


def _convo_bwd(dco, wall, off, u1, ln_g, ln_b):
    T = dco.shape[0]
    tm = min(T, 512)
    ks = D // NCHIP

    def body(d_ref, w_ref, u1_ref, g_ref, b_ref, du1_ref, dg_ref, db_ref, dcb_ref):
        i = pl.program_id(0)
        d = d_ref[...]
        du3 = jnp.concatenate([_dot(d, w_ref[s]) for s in range(NCHIP)], axis=1)
        gam = g_ref[...]
        u2, xh, rstd = _ln_fwd(u1_ref[...], gam, b_ref[...])
        sg = _sig(u2)
        du2 = du3 * (sg * (1.0 + u2 * (1.0 - sg)))
        du1 = _ln_bwd(du2, xh, rstd, gam)
        du1_ref[...] = du1
        _acc_rows(dg_ref, i == 0, _colsum(du2 * xh))
        _acc_rows(db_ref, i == 0, _colsum(du2))
        _acc_rows(dcb_ref, i == 0, _colsum(du1))

    row = pl.BlockSpec((tm, D), lambda i: (i, 0))
    vec = pl.BlockSpec((1, D), lambda i: (0, 0))
    return _pcall(
        body, name="convo_bwd", grid=(T // tm,),
        in_specs=[row, pl.BlockSpec((NCHIP, D, ks), lambda i: (0, 0, off // ks)), row, vec, vec],
        out_specs=[row, vec, vec, vec],
        out_shape=[jax.ShapeDtypeStruct((T, D), F32)] + [jax.ShapeDtypeStruct((1, D), F32)] * 3,
        compiler_params=_cp("arbitrary"))(dco, wall, u1, ln_g, ln_b)


def _conv_bwd(du1, proj, conv_k, dproj):
    T = du1.shape[0]
    tt = min(T, CONV_TT)
    nt = T // tt
    ca, cb = 6, 7

    def body(d_ref, dn_ref, a_ref, b_ref, pa_ref, pb_ref, k_ref, _, dp_ref, dk_ref, win_u, win_d):
        i = pl.program_id(0)
        a, sb = _glu(a_ref, b_ref)
        win_u[HALO:, :] = a * sb
        pa, psb = _glu(pa_ref, pb_ref, slice(tt - HALO, tt))
        win_u[0:HALO, :] = jnp.where(i > 0, pa * psb, 0.0)
        win_d[0:tt, :] = d_ref[...]
        win_d[tt:, :] = jnp.where(i < nt - 1, dn_ref[0:HALO, :], 0.0)

        @pl.when(i == 0)
        def _():
            dk_ref[...] = jnp.zeros_like(dk_ref)

        for c0 in range(0, D, CONV_CB):
            cs = slice(c0, c0 + CONV_CB)
            for r0 in range(0, tt, CONV_SB):
                acc = jnp.zeros((CONV_SB, CONV_CB), F32)
                for w in range(CONV_W):
                    st = r0 + (CONV_W - 1) - w
                    acc += win_d[st:st + CONV_SB, cs] * k_ref[w:w + 1, cs]
                aa = a_ref[r0:r0 + CONV_SB, cs].astype(F32)
                ss = _sig(b_ref[r0:r0 + CONV_SB, cs].astype(F32))
                dp_ref[r0:r0 + CONV_SB, cs] = (acc * ss).astype(BF16)
                dp_ref[r0:r0 + CONV_SB, c0 + D:c0 + D + CONV_CB] = (acc * aa * ss * (1.0 - ss)).astype(BF16)
            for w in range(CONV_W):
                acc = jnp.zeros((CONV_SB, CONV_CB), F32)
                for r0 in range(0, tt, CONV_SB):
                    st = r0 + HALO - (CONV_W - 1) + w
                    acc += win_d[r0:r0 + CONV_SB, cs] * win_u[st:st + CONV_SB, cs]
                dk_ref[w:w + 1, cs] += _colsum(acc)

    blk = lambda f, c: pl.BlockSpec((tt, D), lambda i: (f(i), c))
    cur = lambda i: i
    prv = lambda i: jnp.maximum(i - 1, 0)
    nxt = lambda i: jnp.minimum(i + 1, nt - 1)
    return _pcall(
        body, name="conv_bwd", grid=(nt,),
        in_specs=[blk(cur, 0), blk(nxt, 0), blk(cur, ca), blk(cur, cb), blk(prv, ca), blk(prv, cb),
                  pl.BlockSpec((CONV_W, D), lambda i: (0, 0)), pl.BlockSpec(memory_space=pl.ANY)],
        out_specs=[pl.BlockSpec((tt, 2 * D), lambda i: (i, 3)), pl.BlockSpec((HALO, D), lambda i: (0, 0))],
        out_shape=[jax.ShapeDtypeStruct((T, INW), BF16), jax.ShapeDtypeStruct((HALO, D), F32)],
        input_output_aliases={7: 0},
        scratch_shapes=[pltpu.VMEM((tt + HALO, D), F32), pltpu.VMEM((tt + HALO, D), F32)],
        compiler_params=_cp("arbitrary"))(du1, du1, proj, proj, proj, proj, conv_k, dproj)


def _local_step(x, target, wts, sp, pos_c, pos_sc, adam):
    T = x.shape[0]
    cos_t, sin_t = _rope_tables(T)
    dm_t, xi_t, zeta_t, cds = _decay_tables()
    wa = lambda key: wts[LOC[key][0]]
    wo = lambda key: LOC[key][1]
    _ORDER.active, _ORDER.token = True, None

    xb, xt = _cast_t(x)
    a1, b1, h1 = _ffn_up(xb, wa("g1"), wo("g1"), wo("u1"), "ffn1_up")
    z1, x1, x1b, x1t = _proj_ln(h1, wa("d1"), wo("d1"), x, sp["ln1_g"], sp["ln1_b"], 0.5, "ffn1_down_ln")
    proj = _inproj(x1b, wa("w_in"), wo("w_in"), sp["b_in"], cos_t, sin_t)
    r, ret_in, states = _retention_fwd(proj, sp["ret_gn_g"], dm_t, xi_t, zeta_t, cds)
    u1, u3 = _conv_fwd(proj, sp["conv_k"], sp["conv_b"], sp["conv_ln_g"], sp["conv_ln_b"])
    ro, co, merged = _merge(ret_in, u3, proj, wa("w_ret_o"), wo("w_ret_o"), wo("w_conv_o"))
    z2, x2, x2b, x2t = _proj_ln(merged, wa("w_out"), wo("w_out"), x1, sp["ln2_g"], sp["ln2_b"], 1.0, "out_proj_ln")
    a2, b2, h2 = _ffn_up(x2b, wa("g2"), wo("g2"), wo("u2"), "ffn2_up")
    (z3,) = _proj_ln(h2, wa("d2"), wo("d2"), x2, sp["ln3_g"], sp["ln3_b"], 0.5, "ffn2_down", want_b=False)

    sg = {}
    rs = {}
    loss, df2b, df2t, dz3, sg["ln3_g"], sg["ln3_b"] = _loss_ln_bwd(z3, sp["ln3_g"], sp["ln3_b"], target, 0.5)
    da2, db2 = _ffn_bwd_h(df2b, wa("d2"), wo("d2"), a2, b2, "ffn2_bwd_h")
    g4 = _wgrad(df2t, h2, "d2", "wgrad_d2")
    g4 = _wgrad(x2t, da2, "g2", "wgrad_g2", g4)
    g4 = _wgrad(x2t, db2, "u2", "wgrad_u2", g4)
    rs[4] = _ReduceScatter(g4, 4, pos_c, pos_sc)
    dmb, dmt, dz2, sg["ln2_g"], sg["ln2_b"] = _dx_bwd(
        [da2, db2], [wo("g2"), wo("u2")], wa("g2"), dz3, "ffn2_dx_ln", ln=(z2, sp["ln2_g"], 1.0))
    rs[4].phase2()
    g3 = _wgrad(dmt, merged, "w_out", "wgrad_out")
    dro, drot, dco, dcot, dproj = _merge_bwd(dmb, wa("w_out"), wo("w_out"), proj, ro, co)
    g3 = _wgrad(drot, ret_in, "w_ret_o", "wgrad_ret_o", g3)
    g3 = _wgrad(dcot, u3, "w_conv_o", "wgrad_conv_o", g3)
    rs[3] = _ReduceScatter(g3, 3, pos_c, pos_sc)
    dr, sg["ret_gn_g"], dproj = _reto_bwd(dro, wa("w_ret_o"), wo("w_ret_o"), r, proj, sp["ret_gn_g"], dproj)
    rs[4].phase3()
    rs[3].phase2()
    dproj = _retention_bwd(proj, dr, states, cos_t, sin_t, dm_t, xi_t, zeta_t, cds, dproj)
    du1, sg["conv_ln_g"], sg["conv_ln_b"], sg["conv_b"] = _convo_bwd(
        dco, wa("w_conv_o"), wo("w_conv_o"), u1, sp["conv_ln_g"], sp["conv_ln_b"])
    dproj, dck = _conv_bwd(du1, proj, sp["conv_k"], dproj)
    sg["conv_k"] = dck[:CONV_W]
    adam(4, rs[4].result())
    rs[3].phase3()
    g2 = _wgrad(x1t, dproj, "w_in", "wgrad_in")
    rs[2] = _ReduceScatter(g2, 2, pos_c, pos_sc)
    df1b, df1t, dz1, sg["ln1_g"], sg["ln1_b"], sg["b_in"] = _dx_bwd(
        [dproj], [wo("w_in")], wa("w_in"), dz2, "mixer_dx_ln", ln=(z1, sp["ln1_g"], 0.5), colsum=True)
    adam(3, rs[3].result())
    rs[2].phase2()
    shapes = {n: sg[n].shape for n in SMALL + ["conv_k"]}
    small_parts = _all_gather_small(_pack_small(sg, loss, SMALL_ROWS), "gather_small")
    da1, db1 = _ffn_bwd_h(df1b, wa("d1"), wo("d1"), a1, b1, "ffn1_bwd_h")
    small_sum = _sum_devices(small_parts, "sum_small")
    g1 = _wgrad(df1t, h1, "d1", "wgrad_d1")
    rs[1] = _ReduceScatter(g1, 1, pos_c, pos_sc)
    g0 = _wgrad(xt, da1, "g1", "wgrad_g1")
    g0 = _wgrad(xt, db1, "u1", "wgrad_u1", g0)
    rs[0] = _ReduceScatter(g0, 0, pos_c, pos_sc)
    rs[2].phase3()
    rs[1].phase2()
    rs[0].phase2()
    (grad_x,) = _dx_bwd([da1, db1], [wo("g1"), wo("u1")], wa("g1"), dz1, "ffn1_dx")
    adam(2, rs[2].result())
    rs[1].phase3()
    rs[0].phase3()
    adam(1, rs[1].result())
    adam(0, rs[0].result())
    _ORDER.active = False
    return grad_x, small_sum, shapes


MESH = pl.DeviceIdType.MESH
ANY = pl.BlockSpec(memory_space=pl.ANY)
HALF = D // 2


def _place():
    x, y, c = lax.axis_index("x"), lax.axis_index("y"), lax.axis_index("c")
    chips = [(1 - x, y), (x, 1 - y), (1 - x, 1 - y)]
    return x, y, c, chips


GATHER_ID = 1


def _gather_weights(wloc, name):
    w_ref = jax.new_ref(wloc, memory_space=pltpu.MemorySpace.HBM)
    o_ref = jax.empty_ref(jax.ShapeDtypeStruct((NCHIP, 2, HALF, wloc.shape[-1]), BF16),
                          memory_space=pltpu.MemorySpace.HBM)
    dma = pltpu.SemaphoreType.DMA

    @pl.kernel(mesh=plsc.ScalarSubcoreMesh(axis_name="sc", num_cores=1), name=name,
               scratch_types=(dma(()), dma((2,)), dma((2,)), dma((3,)), dma((3,)), dma(()), dma(())),
               compiler_params=pltpu.CompilerParams(collective_id=GATHER_ID))
    def launch(lsem, s1, r1, s2, r2, s3, r3):
        x, y, c, _ = _place()
        me = 2 * x + y
        sib = (x, y, 1 - c)
        x_nbr, y_nbr = (1 - x, y, c), (x, 1 - y, c)
        x_chip, y_chip, d_chip = 2 * (1 - x) + y, 2 * x + (1 - y), 2 * (1 - x) + (1 - y)
        _handshake([sib, x_nbr, y_nbr])
        mine = pltpu.make_async_copy(w_ref, o_ref.at[me], lsem)
        mine.start()

        def rc(src, dst, ss, rs, dev):
            return pltpu.make_async_remote_copy(src_ref=src, dst_ref=dst, send_sem=ss, recv_sem=rs,
                                                device_id=dev, device_id_type=MESH)

        first = [rc(w_ref.at[c], o_ref.at[me, c], s1.at[0], r1.at[0], x_nbr),
                 rc(w_ref.at[c], o_ref.at[me, c], s1.at[1], r1.at[1], y_nbr)]
        for cp in first:
            cp.start()
        on_chip = c * x_chip + (1 - c) * y_chip
        other_chip = c * y_chip + (1 - c) * x_chip
        on_to = (c * x + (1 - c) * (1 - x), c * (1 - y) + (1 - c) * y, c)
        slot = o_ref.at[on_chip, c]
        rc(slot, slot, s1.at[1 - c], r1.at[1 - c], sib).wait_recv()
        onward = rc(slot, slot, s3, r3, on_to)
        onward.start()
        passed = [rc(slot, slot, s2.at[0], r2.at[0], sib)]
        passed[0].start()
        slot = o_ref.at[other_chip, c]
        rc(slot, slot, s1.at[c], r1.at[c], sib).wait_recv()
        passed.append(rc(slot, slot, s2.at[1], r2.at[1], sib))
        passed[1].start()
        slot = o_ref.at[d_chip, c]
        rc(slot, slot, s3, r3, sib).wait_recv()
        passed.append(rc(slot, slot, s2.at[2], r2.at[2], sib))
        passed[2].start()
        for j, chip in enumerate([other_chip, on_chip, d_chip]):
            slot = o_ref.at[chip, 1 - c]
            rc(slot, slot, s2.at[j], r2.at[j], sib).wait_recv()
        for cp in first + [onward] + passed:
            cp.wait_send()
        mine.wait()

    launch()
    return o_ref[...]


PAIR_ID = 2
CHIP_ID = 3
HBM = pltpu.MemorySpace.HBM


def _sequencer(name, collective_id, n_sems):
    dma = pltpu.SemaphoreType.DMA
    return pl.kernel(mesh=plsc.ScalarSubcoreMesh(axis_name="sc", num_cores=1), name=name,
                     scratch_types=(dma((n_sems,)), dma((n_sems,))),
                     compiler_params=pltpu.CompilerParams(collective_id=collective_id))


def _handshake(peers):
    barrier = pltpu.get_barrier_semaphore()
    for peer in peers:
        pl.semaphore_signal(barrier, inc=1, device_id=peer, device_id_type=MESH)
    pl.semaphore_wait(barrier, len(peers))


def _pair_exchange(g5, name):
    _, _, hr, cols = g5.shape
    g_ref = jax.new_ref(g5, memory_space=HBM)
    o_ref = jax.empty_ref(jax.ShapeDtypeStruct((NCHIP, hr, cols), F32), memory_space=HBM)

    @_sequencer(name, PAIR_ID, NCHIP)
    def launch(ss, rs):
        x, y, c, _ = _place()
        sib = (x, y, 1 - c)
        _handshake([sib])
        cps = [pltpu.make_async_remote_copy(src_ref=g_ref.at[j, 1 - c], dst_ref=o_ref.at[j], send_sem=ss.at[j],
                                            recv_sem=rs.at[j], device_id=sib, device_id_type=MESH)
               for j in range(NCHIP)]
        for cp in cps:
            cp.start()
        for cp in cps:
            cp.wait()

    launch()
    return o_ref[...]


RS_TR = 128


def _pair_sum(pos, g5, got, name):
    _, _, hr, cols = g5.shape

    def body(pos_ref, g_ref, r_ref, o_ref):
        o_ref[...] = (g_ref[...] + r_ref[...]).astype(BF16)

    return _pcall(
        body, name=name, scalar_prefetch=1, grid=(NCHIP, hr // RS_TR),
        in_specs=[pl.BlockSpec((None, None, RS_TR, cols), lambda j, i, p: (j, p[0], i, 0)),
                  pl.BlockSpec((None, RS_TR, cols), lambda j, i, p: (j, i, 0))],
        out_specs=pl.BlockSpec((None, RS_TR, cols), lambda j, i, p: (j, i, 0)),
        out_shape=jax.ShapeDtypeStruct((NCHIP, hr, cols), BF16),
        compiler_params=_cp("parallel", "parallel"))(pos, g5, got)


def _chip_exchange(pb, name):
    _, hr, cols = pb.shape
    p_ref = jax.new_ref(pb, memory_space=HBM)
    o_ref = jax.empty_ref(jax.ShapeDtypeStruct((3, hr, cols), BF16), memory_space=HBM)

    @_sequencer(name, CHIP_ID, 3)
    def launch(ss, rs):
        x, y, c, chips = _place()
        _handshake([(px, py, c) for px, py in chips])
        cps = [pltpu.make_async_remote_copy(src_ref=p_ref.at[2 * px + py], dst_ref=o_ref.at[j], send_sem=ss.at[j],
                                            recv_sem=rs.at[j], device_id=(px, py, c), device_id_type=MESH)
               for j, (px, py) in enumerate(chips)]
        for cp in cps:
            cp.start()
        for cp in cps:
            cp.wait()

    launch()
    return o_ref[...]


def _chip_sum(pos, g5, got, peers, name):
    _, _, hr, cols = g5.shape

    def body(pos_ref, g_ref, r_ref, p_ref, o_ref, t_ref):
        acc = g_ref[...] + r_ref[...]
        for j in range(3):
            acc += p_ref[j].astype(F32)
        o_ref[...] = acc
        t_ref[...] = jnp.zeros_like(t_ref)

    return _pcall(
        body, name=name, scalar_prefetch=1, grid=(hr // RS_TR,),
        in_specs=[pl.BlockSpec((None, None, RS_TR, cols), lambda i, p: (p[0], p[1], i, 0)),
                  pl.BlockSpec((None, RS_TR, cols), lambda i, p: (p[0], i, 0)),
                  pl.BlockSpec((3, RS_TR, cols), lambda i, p: (0, i, 0))],
        out_specs=[pl.BlockSpec((None, RS_TR, cols), lambda i, p: (p[1], i, 0)),
                   pl.BlockSpec((8, 128), lambda i, p: (0, 0))],
        out_shape=[jax.ShapeDtypeStruct((2, hr, cols), F32), jax.ShapeDtypeStruct((8, 128), F32)],
        compiler_params=_cp("arbitrary"))(pos, g5, got, peers)


def _pair_share(gsum, name):
    g_ref = jax.new_ref(gsum, memory_space=HBM)

    @_sequencer(name, PAIR_ID, 1)
    def launch(ss, rs):
        x, y, c, _ = _place()
        sib = (x, y, 1 - c)
        _handshake([sib])
        cp = pltpu.make_async_remote_copy(src_ref=g_ref.at[c], dst_ref=g_ref.at[c], send_sem=ss.at[0],
                                          recv_sem=rs.at[0], device_id=sib, device_id_type=MESH)
        cp.start()
        cp.wait_send()
        pltpu.make_async_remote_copy(src_ref=g_ref.at[1 - c], dst_ref=g_ref.at[1 - c], send_sem=ss.at[0],
                                     recv_sem=rs.at[0], device_id=sib, device_id_type=MESH).wait_recv()

    launch()
    return g_ref[...]


class _ReduceScatter:
    def __init__(self, g_arr, gi, pos_c, pos_sc):
        _, rows, cols = g_arr.shape
        self.g5 = g_arr.reshape(NCHIP, 2, rows // 2, cols)
        self.gi, self.pos_c, self.pos_sc = gi, pos_c, pos_sc
        self.got = _pair_exchange(self.g5, f"pair_exchange_{gi}")

    def phase2(self):
        pb = _pair_sum(self.pos_c, self.g5, self.got, f"pair_sum_{self.gi}")
        self.peers = _chip_exchange(pb, f"chip_exchange_{self.gi}")

    def phase3(self):
        gsum, _ = _chip_sum(self.pos_sc, self.g5, self.got, self.peers, f"chip_sum_{self.gi}")
        self.full = _pair_share(gsum, f"pair_share_{self.gi}")

    def result(self):
        _, hr, cols = self.full.shape
        return self.full.reshape(2 * hr, cols)


SMALL_ROWS = 56


ALL_ID = 4


def _all_gather_small(vec, name):
    v_ref = jax.new_ref(vec, memory_space=HBM)
    o_ref = jax.empty_ref(jax.ShapeDtypeStruct((8, SMALL_ROWS, D), F32), memory_space=HBM)

    @_sequencer(name, ALL_ID, 8)
    def launch(ss, rs):
        x, y, c, _ = _place()
        me = 4 * x + 2 * y + c
        flip = lambda v, bit: 1 - v if bit else v
        peers = [(flip(x, m >> 2), flip(y, (m >> 1) & 1), flip(c, m & 1)) for m in range(1, 8)]
        _handshake(peers)
        mine = pltpu.make_async_copy(v_ref, o_ref.at[me], ss.at[7])
        mine.start()
        cps = [pltpu.make_async_remote_copy(src_ref=v_ref, dst_ref=o_ref.at[me], send_sem=ss.at[k],
                                            recv_sem=rs.at[k], device_id=peer, device_id_type=MESH)
               for k, peer in enumerate(peers)]
        for cp in cps:
            cp.start()
        for cp in cps:
            cp.wait()
        mine.wait()

    launch()
    return o_ref[...]


def _sum_devices(parts, name):
    def body(p_ref, o_ref):
        acc = p_ref[0]
        for d in range(1, 8):
            acc += p_ref[d]
        o_ref[...] = acc

    return _pcall(
        body, name=name, grid=(SMALL_ROWS // 8,),
        in_specs=[pl.BlockSpec((8, 8, D), lambda i: (0, i, 0))],
        out_specs=pl.BlockSpec((8, D), lambda i: (i, 0)),
        out_shape=jax.ShapeDtypeStruct((SMALL_ROWS, D), F32),
        compiler_params=_cp("parallel"))(parts)


def _adamw_math(w, g, m, v):
    m2 = ADAM_B1 * m + (1.0 - ADAM_B1) * g
    v2 = ADAM_B2 * v + (1.0 - ADAM_B2) * (g * g)
    m_hat = m2 / (1.0 - ADAM_B1 ** ADAM_STEP)
    v_hat = v2 / (1.0 - ADAM_B2 ** ADAM_STEP)
    delta = -ADAM_LR * (m_hat / (jnp.sqrt(v_hat) + ADAM_EPS) + ADAM_WD * w)
    return delta, m2, v2


def _adamw(w, g, m, v, name, g_block=None):
    R, C = w.shape
    tr = R
    gw_hint = C if g_block is None else g_block[0]
    for cand in (256, 176, 128, 64, 32, 16, 8):
        if R % cand == 0 and cand * max(C, gw_hint) * 4 <= (1 << 20):
            tr = cand
            break
    gw, gi = (C, 0) if g_block is None else g_block

    def body(w_ref, g_ref, m_ref, v_ref, go_ref, d_ref, mo_ref, vo_ref):
        g = g_ref[:, 0:C]
        d, m2, v2 = _adamw_math(w_ref[...], g, m_ref[...], v_ref[...])
        go_ref[...] = g
        d_ref[...] = d
        mo_ref[...] = m2
        vo_ref[...] = v2

    spec = pl.BlockSpec((tr, C), lambda i: (i, 0))
    return _pcall(
        body, name=name, grid=(R // tr,),
        in_specs=[spec, pl.BlockSpec((tr, gw), lambda i: (i, gi)), spec, spec],
        out_specs=[spec] * 4, out_shape=[jax.ShapeDtypeStruct((R, C), F32)] * 4,
        compiler_params=_cp("parallel"))(w, g, m, v)


BIG = ["ffn1_w_gate", "ffn1_w_up", "ffn1_w_down", "w_in", "w_ret_o", "w_conv_o", "w_out",
       "ffn2_w_gate", "ffn2_w_up", "ffn2_w_down"]
SLAB = {"ffn1_w_gate": "g1", "ffn1_w_up": "u1", "ffn1_w_down": "d1", "w_in": "w_in", "w_ret_o": "w_ret_o",
        "w_conv_o": "w_conv_o", "w_out": "w_out", "ffn2_w_gate": "g2", "ffn2_w_up": "u2", "ffn2_w_down": "d2"}
TRANSPOSED = {"ffn1_w_down", "ffn2_w_down", "w_ret_o", "w_conv_o", "w_out"}
MINOR_ROWS = {"ffn1_w_gate", "ffn1_w_up", "ffn2_w_gate", "ffn2_w_up"}
SMALL = ["ln1_g", "ln1_b", "ln2_g", "ln2_b", "ln3_g", "ln3_b", "conv_ln_g", "conv_ln_b", "conv_b",
         "ret_gn_g", "b_in"]
ORDER = ["ffn1_w_gate", "ffn1_w_up", "ffn1_w_down", "ln1_g", "ln1_b", "w_in", "b_in", "ret_gn_g", "conv_k",
         "conv_b", "conv_ln_g", "conv_ln_b", "w_ret_o", "w_conv_o", "w_out", "ln2_g", "ln2_b",
         "ffn2_w_gate", "ffn2_w_up", "ffn2_w_down", "ln3_g", "ln3_b"]


def _slab_width(name):
    return WIDTH[SLAB[name]]


def _pack_group(weights, keys):
    by_key = {SLAB[n]: n for n in BIG}
    parts = []
    for key in keys:
        w = weights[by_key[key]]
        w = w.T if by_key[key] in TRANSPOSED else w
        parts.append(jnp.pad(w, ((0, 0), (0, WIDTH[key] - w.shape[1]))))
    return jnp.concatenate(parts, axis=1).astype(BF16)


def _pack_small(vals, loss, rows):
    flat = jnp.concatenate([vals[n].reshape(-1) for n in SMALL] + [vals["conv_k"].reshape(-1), loss.reshape(-1)])
    return jnp.pad(flat, (0, rows * D - flat.shape[0])).reshape(rows, D)


def _unpack_small(arr, shapes):
    flat = arr.reshape(-1)
    out, pos = {}, 0
    for n in SMALL + ["conv_k"]:
        size = int(np.prod(shapes[n]))
        out[n] = flat[pos:pos + size].reshape(shapes[n])
        pos += size
    return out, flat[pos]


def kernel(x, ffn1_w_gate, ffn1_w_up, ffn1_w_down, ln1_g, ln1_b, w_in, b_in, ret_gn_g, conv_k, conv_b, conv_ln_g, conv_ln_b, w_ret_o, w_conv_o, w_out, ln2_g, ln2_b, ffn2_w_gate, ffn2_w_up, ffn2_w_down, ln3_g, ln3_b, loss_target, m_ffn1_w_gate, m_ffn1_w_up, m_ffn1_w_down, m_ln1_g, m_ln1_b, m_w_in, m_b_in, m_ret_gn_g, m_conv_k, m_conv_b, m_conv_ln_g, m_conv_ln_b, m_w_ret_o, m_w_conv_o, m_w_out, m_ln2_g, m_ln2_b, m_ffn2_w_gate, m_ffn2_w_up, m_ffn2_w_down, m_ln3_g, m_ln3_b, v_ffn1_w_gate, v_ffn1_w_up, v_ffn1_w_down, v_ln1_g, v_ln1_b, v_w_in, v_b_in, v_ret_gn_g, v_conv_k, v_conv_b, v_conv_ln_g, v_conv_ln_b, v_w_ret_o, v_w_conv_o, v_w_out, v_ln2_g, v_ln2_b, v_ffn2_w_gate, v_ffn2_w_up, v_ffn2_w_down, v_ln3_g, v_ln3_b):
    args = dict(locals())
    w = {n: args[n] for n in ORDER}
    m = {n: args["m_" + n] for n in ORDER}
    v = {n: args["v_" + n] for n in ORDER}
    xi, yi, ci = lax.axis_index("x"), lax.axis_index("y"), lax.axis_index("c")
    chip = 2 * xi + yi

    shards = {n: w[n][0] for n in BIG}
    wts = []
    for gi, keys in enumerate(GROUPS):
        slab = _pack_group(shards, keys)
        cols = slab.shape[1]
        wts.append(_gather_weights(slab.reshape(2, HALF, cols), f"gather_{gi}").reshape(NCHIP, D, cols))

    sp = {n: w[n] for n in SMALL}
    sp["conv_k"] = None
    kfull_shape = (CONV_W, D)
    kpad = jnp.zeros(kfull_shape, F32)
    kpad = lax.dynamic_update_slice(kpad, w["conv_k"][0, :, 0, :] * jnp.where(ci == 0, 1.0, 0.0), (0, chip * (D // NCHIP)))
    kvec = jnp.pad(kpad.reshape(-1), (0, SMALL_ROWS * D - CONV_W * D)).reshape(SMALL_ROWS, D)
    kall = _sum_devices(_all_gather_small(kvec, "gather_conv_k"), "sum_conv_k")
    sp["conv_k"] = kall.reshape(-1)[:CONV_W * D].reshape(kfull_shape)
    pos_c = jnp.reshape(ci, (1,)).astype(jnp.int32)
    pos_sc = jnp.stack([chip, ci]).astype(jnp.int32)
    out = {}

    def adam(gi, slab):
        for n in BIG:
            (g_of, off), width = LOC[SLAB[n]], _slab_width(n)
            if g_of != gi:
                continue
            w2 = w[n][0]
            if n in TRANSPOSED:
                res = _adamw(w2, slab[:, off:off + w2.shape[0]].T, m[n][0], v[n][0], "adamw_" + n)
                out[n] = [r[None] for r in res]
            elif n in MINOR_ROWS:
                res = _adamw(w2.T, slab[:, off:off + w2.shape[1]].T, m[n][0].T, v[n][0].T, "adamw_" + n)
                out[n] = [r.T[None] for r in res]
            else:
                res = _adamw(w2, slab, m[n][0], v[n][0], "adamw_" + n, g_block=(width, off // width))
                out[n] = [r[None] for r in res]

    grad_x, small_sum, shapes = _local_step(x[0], loss_target[0], wts, sp, pos_c, pos_sc, adam)
    small, total = _unpack_small(small_sum, shapes)

    for n in SMALL:
        res = _adamw(w[n], small[n], m[n], v[n], "adamw_" + n)
        out[n] = list(res)
    gk = lax.dynamic_slice(small["conv_k"], (0, chip * (D // NCHIP)), (CONV_W, D // NCHIP))
    res = _adamw(w["conv_k"][0, :, 0, :], gk, m["conv_k"][0, :, 0, :], v["conv_k"][0, :, 0, :], "adamw_conv_k")
    out["conv_k"] = [r[None, :, None, :] for r in res]

    grads = [out[n][0] for n in ORDER]
    deltas = [out[n][1] for n in ORDER]
    new_m = [out[n][2] for n in ORDER]
    new_v = [out[n][3] for n in ORDER]
    return (total, grad_x[None], *grads, *deltas, *new_m, *new_v)
```

```python
import dataclasses
import functools

import numpy as np
import jax
import jax.numpy as jnp
from jax import lax
from jax.experimental import pallas as pl
from jax.experimental.pallas import tpu as pltpu
from jax.experimental.pallas import tpu_sc as plsc

F32 = jnp.float32
BF16 = jnp.bfloat16

D = 1024
FS = 704
FSP = 768
FP = 4 * FSP
H = 8
DK = 128
DV = 256
CH = 128
VW = H * DV
INW = 10240
INS = INW // 4
CONV_W = 31
HALO = 32
EPS = 1e-5
ALPHA = 2.0 ** 0.25
ROPE_BASE = 10000.0
NCHIP = 4

ADAM_LR, ADAM_B1, ADAM_B2, ADAM_EPS, ADAM_WD, ADAM_STEP = 0.001, 0.9, 0.999, 1e-08, 0.01, 10

OFF = {"w_in": 0, "w_ret_o": 2560, "g1": 3072, "u1": 3840, "d1": 4608,
       "g2": 5376, "u2": 6144, "d2": 6912, "w_conv_o": 7680, "w_out": 7936}
WCOLS = 8192
WIDTH = {"w_in": INS, "w_ret_o": VW // NCHIP, "w_conv_o": D // NCHIP, "w_out": D // NCHIP,
         "g1": FSP, "u1": FSP, "d1": FSP, "g2": FSP, "u2": FSP, "d2": FSP}
GROUPS = (("g1", "u1"), ("d1",), ("w_in",), ("w_ret_o", "w_conv_o", "w_out"), ("g2", "u2", "d2"))
LOC = {}
for _gi, _keys in enumerate(GROUPS):
    _off = 0
    for _k in _keys:
        LOC[_k] = (_gi, _off)
        _off += WIDTH[_k]
GCOLS = [sum(WIDTH[k] for k in keys) for keys in GROUPS]
VMEM_LIMIT = 56 << 20


def _cp(*sem, **kw):
    return pltpu.CompilerParams(dimension_semantics=sem, vmem_limit_bytes=VMEM_LIMIT, **kw)


class _ProgramOrder:
    def __init__(self):
        self.active = False
        self.token = None


_ORDER = _ProgramOrder()


def _pcall(body, *, in_specs, scalar_prefetch=0, **kw):
    def call(*args):
        dep = _ORDER.token if _ORDER.active else None
        specs, fn = list(in_specs), body
        if dep is not None:
            n = len(args)

            def fn(*refs):
                return body(*refs[:n], *refs[n + 1:])

            specs.append(pl.BlockSpec(memory_space=pl.ANY))
            args = (*args, dep)
        params = dict(kw)
        if scalar_prefetch:
            params["grid_spec"] = pltpu.PrefetchScalarGridSpec(
                num_scalar_prefetch=scalar_prefetch, grid=params.pop("grid"), in_specs=specs,
                out_specs=params.pop("out_specs"))
        else:
            params["in_specs"] = specs
        out = pl.pallas_call(fn, **params)(*args)
        if _ORDER.active:
            _ORDER.token = jax.tree.leaves(out)[-1]
        return out

    return call


def _resident(shape, col_block):
    lead = (0,) * (len(shape) - 1)
    return pl.BlockSpec(shape, lambda *_: (*lead, col_block), pipeline_mode=pl.Buffered(1))


def _sig(x):
    return 1.0 / (1.0 + jnp.exp(-x))


def _dot(a, b):
    return jnp.dot(a, b, preferred_element_type=F32)


def _dot_nt(a, b):
    return lax.dot_general(a, b, (((1,), (1,)), ((), ())), preferred_element_type=F32)


def _ln_fwd(z, g, b):
    mu = jnp.mean(z, axis=-1, keepdims=True)
    xc = z - mu
    var = jnp.mean(xc * xc, axis=-1, keepdims=True)
    rstd = lax.rsqrt(var + EPS)
    xh = xc * rstd
    return xh * g + b, xh, rstd


def _ln_bwd(dy, xh, rstd, g):
    dxh = dy * g
    m1 = jnp.mean(dxh, axis=-1, keepdims=True)
    m2 = jnp.mean(dxh * xh, axis=-1, keepdims=True)
    return rstd * (dxh - m1 - xh * m2)


def _colsum(x):
    return jnp.sum(x, axis=0, keepdims=True)


def _acc_rows(ref, first, val):
    @pl.when(first)
    def _():
        ref[...] = val

    @pl.when(jnp.logical_not(first))
    def _():
        ref[...] += val


def _rope_tables(T):
    half = DK // 2
    freqs = ROPE_BASE ** (-np.arange(half, dtype=np.float32) / half)
    ang = (np.arange(T, dtype=np.float32)[:, None] * freqs[None, :]).astype(np.float32)
    cos, sin = np.cos(ang), np.sin(ang)
    return (jnp.asarray(np.concatenate([cos, cos], 1), F32),
            jnp.asarray(np.concatenate([-sin, sin], 1), F32))


def _decay_tables():
    h = np.arange(H, dtype=np.float64)
    log_g = np.log(1.0 - np.exp2(-5.0 - h))
    idx = np.arange(CH, dtype=np.float64)
    diff = idx[:, None] - idx[None, :]
    dm = np.where(diff[None] >= 0, np.exp(np.maximum(diff, 0.0)[None] * log_g[:, None, None]), 0.0)
    xi = np.exp((idx[None, :] + 1.0) * log_g[:, None])
    zeta = np.exp((CH - 1.0 - idx)[None, :] * log_g[:, None])
    cd = np.exp(CH * log_g)
    xi_t = np.broadcast_to(xi[:, :, None], (H, CH, DV))
    zeta_t = np.broadcast_to(zeta[:, :, None], (H, CH, DK))
    return (jnp.asarray(dm, F32), jnp.asarray(xi_t, F32), jnp.asarray(zeta_t, F32),
            [float(v) for v in cd])


def _cast_t(x):
    T = x.shape[0]
    tm = min(T, 512)

    def body(x_ref, xb_ref, xt_ref):
        v = x_ref[...]
        xb_ref[...] = v.astype(BF16)
        xt_ref[...] = v.T.astype(BF16)

    return _pcall(
        body, name="cast_t", grid=(T // tm,),
        in_specs=[pl.BlockSpec((tm, D), lambda i: (i, 0))],
        out_specs=[pl.BlockSpec((tm, D), lambda i: (i, 0)), pl.BlockSpec((D, tm), lambda i: (0, i))],
        out_shape=[jax.ShapeDtypeStruct((T, D), BF16), jax.ShapeDtypeStruct((D, T), BF16)],
        compiler_params=_cp("parallel"))(x)


def _ffn_up(xb, wall, og, ou, name):
    T = xb.shape[0]
    tm = min(T, 512)
    assert ou == og + FSP

    def body(x_ref, w_ref, a_ref, b_ref, h_ref):
        x = x_ref[...]
        for s in range(NCHIP):
            cols = slice(s * FSP, (s + 1) * FSP)
            a = _dot(x, w_ref[s, :, 0:FSP])
            b = _dot(x, w_ref[s, :, FSP:2 * FSP])
            a_ref[:, cols] = a.astype(BF16)
            b_ref[:, cols] = b.astype(BF16)
            h_ref[:, cols] = (a * _sig(a) * b).astype(BF16)

    ospec = pl.BlockSpec((tm, FP), lambda i: (i, 0))
    return _pcall(
        body, name=name, grid=(T // tm,),
        in_specs=[pl.BlockSpec((tm, D), lambda i: (i, 0)), _resident((NCHIP, D, 2 * FSP), og // (2 * FSP))],
        out_specs=[ospec] * 3, out_shape=[jax.ShapeDtypeStruct((T, FP), BF16)] * 3,
        compiler_params=_cp("parallel"))(xb, wall)


def _proj_ln(hb, wall, off, res, g, b, coef, name, want_b=True):
    T, K = hb.shape
    ks = K // NCHIP
    tm = min(T, 256)

    def body(h_ref, w_ref, r_ref, g_ref, b_ref, z_ref, *rest):
        acc = _dot_nt(h_ref[:, 0:ks], w_ref[0])
        for s in range(1, NCHIP):
            acc += _dot_nt(h_ref[:, s * ks:(s + 1) * ks], w_ref[s])
        z = ALPHA * r_ref[...] + coef * acc
        z_ref[...] = z
        if want_b:
            y, _, _ = _ln_fwd(z, g_ref[...], b_ref[...])
            y_ref, yb_ref, yt_ref = rest
            y_ref[...] = y
            yb_ref[...] = y.astype(BF16)
            yt_ref[...] = y.T.astype(BF16)

    row = pl.BlockSpec((tm, D), lambda i: (i, 0))
    vec = pl.BlockSpec((1, D), lambda i: (0, 0))
    out_specs = [row]
    out_shape = [jax.ShapeDtypeStruct((T, D), F32)]
    if want_b:
        out_specs += [row, row, pl.BlockSpec((D, tm), lambda i: (0, i))]
        out_shape += [jax.ShapeDtypeStruct((T, D), F32), jax.ShapeDtypeStruct((T, D), BF16),
                      jax.ShapeDtypeStruct((D, T), BF16)]
    return _pcall(
        body, name=name, grid=(T // tm,),
        in_specs=[pl.BlockSpec((tm, K), lambda i: (i, 0)),
                  pl.BlockSpec((NCHIP, D, ks), lambda i: (0, 0, off // ks)), row, vec, vec],
        out_specs=out_specs, out_shape=out_shape,
        compiler_params=_cp("parallel"))(hb, wall, res, g, b)


def _inproj(xb, wall, off, b_in, cos_t, sin_t):
    T = xb.shape[0]
    tm, tn = min(T, 256), 512
    assert off == 0

    def body(x_ref, w_ref, bias_ref, cos_ref, sin_ref, o_ref):
        x = x_ref[...]
        c = cos_ref[...]
        s = sin_ref[...]
        for n0 in range(0, INW, tn):
            chip, c0 = divmod(n0, INS)
            acc = _dot(x, w_ref[chip, :, c0:c0 + tn]) + bias_ref[:, n0:n0 + tn]
            if n0 >= 2 * D:
                o_ref[:, n0:n0 + tn] = acc.astype(BF16)
                continue
            scale = DK ** -0.5 if n0 < D else 1.0
            for hh in range(tn // DK):
                xh = acc[:, hh * DK:(hh + 1) * DK]
                o = (xh * c + pltpu.roll(xh, DK // 2, 1) * s) * scale
                o_ref[:, n0 + hh * DK:n0 + (hh + 1) * DK] = o.astype(BF16)

    return _pcall(
        body, name="inproj", grid=(T // tm,),
        in_specs=[pl.BlockSpec((tm, D), lambda i: (i, 0)),
                  _resident((NCHIP, D, INS), 0),
                  pl.BlockSpec((1, INW), lambda i: (0, 0)),
                  pl.BlockSpec((tm, DK), lambda i: (i, 0)),
                  pl.BlockSpec((tm, DK), lambda i: (i, 0))],
        out_specs=pl.BlockSpec((tm, INW), lambda i: (i, 0)),
        out_shape=jax.ShapeDtypeStruct((T, INW), BF16),
        compiler_params=_cp("parallel"))(xb, wall, b_in, cos_t, sin_t)


def _retention_fwd(proj, gn_g, dm_t, xi_t, zeta_t, cds):
    T = proj.shape[0]
    n = T // CH

    def body(q_ref, k_ref, v_ref, g_ref, gn_ref, dm_ref, xi_ref, zt_ref, r_ref, ri_ref, st_ref, state):
        @pl.when(pl.program_id(0) == 0)
        def _():
            state[...] = jnp.zeros_like(state)

        for h in range(H):
            q = q_ref[:, h * DK:(h + 1) * DK]
            k = k_ref[:, h * DK:(h + 1) * DK]
            v = v_ref[:, h * DV:(h + 1) * DV]
            rows = slice(h * DK, (h + 1) * DK)
            s_prev = state[rows, :]
            s_b = s_prev.astype(BF16)
            st_ref[rows, :] = s_b
            sc = _dot_nt(q, k) * dm_ref[h]
            r = _dot(sc.astype(BF16), v) + _dot(q, s_b) * xi_ref[h]
            kz = k.astype(F32) * zt_ref[h]
            state[rows, :] = cds[h] * s_prev + _dot(kz.T.astype(BF16), v)
            cols = slice(h * DV, (h + 1) * DV)
            r_ref[:, cols] = r
            mu = jnp.mean(r, axis=-1, keepdims=True)
            xc = r - mu
            var = jnp.mean(xc * xc, axis=-1, keepdims=True)
            y = xc * lax.rsqrt(var + EPS) * gn_ref[:, cols]
            g = g_ref[:, cols].astype(F32)
            ri_ref[:, cols] = (g * _sig(g) * y).astype(BF16)

    full3 = lambda shp: pl.BlockSpec(shp, lambda c: (0, 0, 0))
    return _pcall(
        body, name="retention_fwd", grid=(n,),
        in_specs=[pl.BlockSpec((CH, D), lambda c: (c, 0)),
                  pl.BlockSpec((CH, D), lambda c: (c, 1)),
                  pl.BlockSpec((CH, VW), lambda c: (c, 1)),
                  pl.BlockSpec((CH, VW), lambda c: (c, 2)),
                  pl.BlockSpec((1, VW), lambda c: (0, 0)),
                  full3((H, CH, CH)), full3((H, CH, DV)), full3((H, CH, DK))],
        out_specs=[pl.BlockSpec((CH, VW), lambda c: (c, 0)), pl.BlockSpec((CH, VW), lambda c: (c, 0)),
                   pl.BlockSpec((None, H * DK, DV), lambda c: (c, 0, 0))],
        out_shape=[jax.ShapeDtypeStruct((T, VW), F32), jax.ShapeDtypeStruct((T, VW), BF16),
                   jax.ShapeDtypeStruct((n, H * DK, DV), BF16)],
        scratch_shapes=[pltpu.VMEM((H * DK, DV), F32)],
        compiler_params=_cp("arbitrary"))(proj, proj, proj, proj, gn_g, dm_t, xi_t, zeta_t)


CONV_TT = 256
CONV_SB = 64
CONV_CB = 256


SUB = 8
CONV_PAD = 8


def _glu(a_ref, b_ref, rows=slice(None)):
    a = a_ref[rows, :].astype(F32)
    sb = _sig(b_ref[rows, :].astype(F32))
    return a, sb


def _shift_copies(win, sh, rows):
    win[rows:rows + CONV_PAD, :] = jnp.zeros((CONV_PAD, D), F32)
    for b in range(1, SUB):
        sh[b - 1, :, :] = win[b:b + rows, :]


def _tap(win, sh, start, size, cs):
    a, b = divmod(start, SUB)
    src = win if b == 0 else sh.at[b - 1]
    return src[SUB * a:SUB * a + size, cs]


def _conv_fwd(proj, conv_k, conv_b, ln_g, ln_b):
    T = proj.shape[0]
    tt = min(T, CONV_TT)
    ca, cb = 6 * D // D, 7 * D // D

    def body(a_ref, b_ref, pa_ref, pb_ref, k_ref, cb_ref, g_ref, bb_ref, u1_ref, u3_ref, win, sh):
        i = pl.program_id(0)
        a, sb = _glu(a_ref, b_ref)
        win[HALO:tt + HALO, :] = a * sb
        pa, psb = _glu(pa_ref, pb_ref, slice(tt - HALO, tt))
        win[0:HALO, :] = jnp.where(i > 0, pa * psb, 0.0)
        _shift_copies(win, sh, tt + HALO)
        for c0 in range(0, D, CONV_CB):
            cs = slice(c0, c0 + CONV_CB)
            for r0 in range(0, tt, CONV_SB):
                acc = jnp.zeros((CONV_SB, CONV_CB), F32)
                for w in range(CONV_W):
                    st = r0 + HALO - (CONV_W - 1) + w
                    acc += _tap(win, sh, st, CONV_SB, cs) * k_ref[w:w + 1, cs]
                u1_ref[r0:r0 + CONV_SB, cs] = acc + cb_ref[:, cs]
        u2, _, _ = _ln_fwd(u1_ref[...], g_ref[...], bb_ref[...])
        u3_ref[...] = (u2 * _sig(u2)).astype(BF16)

    vec = pl.BlockSpec((1, D), lambda i: (0, 0))
    row = pl.BlockSpec((tt, D), lambda i: (i, 0))
    return _pcall(
        body, name="conv_fwd", grid=(T // tt,),
        in_specs=[pl.BlockSpec((tt, D), lambda i: (i, ca)), pl.BlockSpec((tt, D), lambda i: (i, cb)),
                  pl.BlockSpec((tt, D), lambda i: (jnp.maximum(i - 1, 0), ca)),
                  pl.BlockSpec((tt, D), lambda i: (jnp.maximum(i - 1, 0), cb)),
                  pl.BlockSpec((CONV_W, D), lambda i: (0, 0)), vec, vec, vec],
        out_specs=[row, row],
        out_shape=[jax.ShapeDtypeStruct((T, D), F32), jax.ShapeDtypeStruct((T, D), BF16)],
        scratch_shapes=[pltpu.VMEM((tt + HALO + CONV_PAD, D), F32), pltpu.VMEM((SUB - 1, tt + HALO, D), F32)],
        compiler_params=_cp("parallel"))(proj, proj, proj, proj, conv_k, conv_b, ln_g, ln_b)


def _merge(ret_in, u3, proj, wall, off_r, off_c):
    T = ret_in.shape[0]
    tm = min(T, 512)
    kr, kc = VW // NCHIP, D // NCHIP

    def body(ri_ref, u3_ref, gr_ref, gc_ref, wr_ref, wc_ref, ro_ref, co_ref, m_ref):
        ro = _dot_nt(ri_ref[:, 0:kr], wr_ref[0])
        co = _dot_nt(u3_ref[:, 0:kc], wc_ref[0])
        for s in range(1, NCHIP):
            ro += _dot_nt(ri_ref[:, s * kr:(s + 1) * kr], wr_ref[s])
            co += _dot_nt(u3_ref[:, s * kc:(s + 1) * kc], wc_ref[s])
        ro_ref[...] = ro.astype(BF16)
        co_ref[...] = co.astype(BF16)
        m = _sig(gr_ref[...].astype(F32)) * ro + _sig(gc_ref[...].astype(F32)) * co
        m_ref[...] = m.astype(BF16)

    row = pl.BlockSpec((tm, D), lambda i: (i, 0))
    return _pcall(
        body, name="merge", grid=(T // tm,),
        in_specs=[pl.BlockSpec((tm, VW), lambda i: (i, 0)), row,
                  pl.BlockSpec((tm, D), lambda i: (i, 8)), pl.BlockSpec((tm, D), lambda i: (i, 9)),
                  pl.BlockSpec((NCHIP, D, kr), lambda i: (0, 0, off_r // kr)),
                  pl.BlockSpec((NCHIP, D, kc), lambda i: (0, 0, off_c // kc))],
        out_specs=[row] * 3, out_shape=[jax.ShapeDtypeStruct((T, D), BF16)] * 3,
        compiler_params=_cp("parallel"))(ret_in, u3, proj, proj, wall, wall)


def _loss_ln_bwd(z, g, b, target, coef):
    T = z.shape[0]
    tm = min(T, 256)
    nt = T // tm

    def body(z_ref, g_ref, b_ref, t_ref, loss_ref, dzb_ref, dzt_ref, dz_ref, dg_ref, db_ref, lacc):
        i = pl.program_id(0)
        gam = g_ref[...]
        y, xh, rstd = _ln_fwd(z_ref[...], gam, b_ref[...])
        e = y - t_ref[...]
        part = _colsum(e * e)
        _acc_rows(lacc, i == 0, part)
        dy = e * (1.0 / D)
        dz = _ln_bwd(dy, xh, rstd, gam)
        dz_ref[...] = dz
        dzc = coef * dz
        dzb_ref[...] = dzc.astype(BF16)
        dzt_ref[...] = dzc.T.astype(BF16)
        _acc_rows(dg_ref, i == 0, _colsum(dy * xh))
        _acc_rows(db_ref, i == 0, _colsum(dy))

        @pl.when(i == nt - 1)
        def _():
            loss_ref[...] = (0.5 / D) * jnp.sum(lacc[...], axis=1, keepdims=True)

    row = pl.BlockSpec((tm, D), lambda i: (i, 0))
    vec = pl.BlockSpec((1, D), lambda i: (0, 0))
    return _pcall(
        body, name="loss_ln_bwd", grid=(nt,),
        in_specs=[row, vec, vec, row],
        out_specs=[pl.BlockSpec((1, 1), lambda i: (0, 0)), row, pl.BlockSpec((D, tm), lambda i: (0, i)),
                   row, vec, vec],
        out_shape=[jax.ShapeDtypeStruct((1, 1), F32), jax.ShapeDtypeStruct((T, D), BF16),
                   jax.ShapeDtypeStruct((D, T), BF16), jax.ShapeDtypeStruct((T, D), F32),
                   jax.ShapeDtypeStruct((1, D), F32), jax.ShapeDtypeStruct((1, D), F32)],
        scratch_shapes=[pltpu.VMEM((1, D), F32)],
        compiler_params=_cp("arbitrary"))(z, g, b, target)


def _ffn_bwd_h(dfb, wall, od, a, b, name):
    T = dfb.shape[0]
    tm = min(T, 512)

    def body(d_ref, w_ref, a_ref, b_ref, da_ref, db_ref):
        d = d_ref[...]
        for s in range(NCHIP):
            cols = slice(s * FSP, (s + 1) * FSP)
            dh = _dot(d, w_ref[s])
            a = a_ref[:, cols].astype(F32)
            sg = _sig(a)
            da_ref[:, cols] = (dh * b_ref[:, cols].astype(F32) * (sg * (1.0 + a * (1.0 - sg)))).astype(BF16)
            db_ref[:, cols] = (dh * a * sg).astype(BF16)

    ospec = pl.BlockSpec((tm, FP), lambda i: (i, 0))
    return _pcall(
        body, name=name, grid=(T // tm,),
        in_specs=[pl.BlockSpec((tm, D), lambda i: (i, 0)), _resident((NCHIP, D, FSP), od // FSP), ospec, ospec],
        out_specs=[ospec] * 2, out_shape=[jax.ShapeDtypeStruct((T, FP), BF16)] * 2,
        compiler_params=_cp("parallel"))(dfb, wall, a, b)


def _dx_bwd(lhs, offs, wall, dz_next, name, ln=None, colsum=False):
    T, K = lhs[0].shape
    ks = K // NCHIP
    nl = len(lhs)
    assert list(offs) == [l * ks for l in range(nl)]
    tm = min(T, 512 if K <= FP else 256)

    def body(*refs):
        l_refs = refs[:nl]
        w_ref = refs[nl]
        dzn_ref = refs[nl + 1]
        pos = nl + 2
        if ln is not None:
            z_ref, g_ref = refs[pos:pos + 2]
            pos += 2
        outs = refs[pos:]
        i = pl.program_id(0)
        acc = None
        for s in range(NCHIP):
            rows = slice(s * ks, (s + 1) * ks)
            for l in range(nl):
                part = _dot_nt(l_refs[l][:, rows], w_ref[s, :, l * ks:(l + 1) * ks])
                acc = part if acc is None else acc + part
            if colsum:
                _acc_rows(outs[-1].at[:, rows], i == 0, _colsum(l_refs[0][:, rows].astype(F32)))
        dy = acc + ALPHA * dzn_ref[...]
        if ln is None:
            outs[0][...] = dy
        else:
            gam = g_ref[...]
            _, xh, rstd = _ln_fwd(z_ref[...], gam, 0.0)
            dz = _ln_bwd(dy, xh, rstd, gam)
            dzc = ln[2] * dz
            outs[0][...] = dzc.astype(BF16)
            outs[1][...] = dzc.T.astype(BF16)
            outs[2][...] = dz
            _acc_rows(outs[3], i == 0, _colsum(dy * xh))
            _acc_rows(outs[4], i == 0, _colsum(dy))

    row = pl.BlockSpec((tm, D), lambda i: (i, 0))
    vec = pl.BlockSpec((1, D), lambda i: (0, 0))
    in_specs = [pl.BlockSpec((tm, K), lambda i: (i, 0))] * nl + [_resident((NCHIP, D, nl * ks), 0), row]
    args = list(lhs) + [wall, dz_next]
    if ln is None:
        out_specs = [row]
        out_shape = [jax.ShapeDtypeStruct((T, D), F32)]
    else:
        in_specs += [row, vec]
        args += [ln[0], ln[1]]
        out_specs = [row, pl.BlockSpec((D, tm), lambda i: (0, i)), row, vec, vec]
        out_shape = [jax.ShapeDtypeStruct((T, D), BF16), jax.ShapeDtypeStruct((D, T), BF16),
                     jax.ShapeDtypeStruct((T, D), F32), jax.ShapeDtypeStruct((1, D), F32),
                     jax.ShapeDtypeStruct((1, D), F32)]
    if colsum:
        out_specs += [pl.BlockSpec((1, K), lambda i: (0, 0))]
        out_shape += [jax.ShapeDtypeStruct((1, K), F32)]
    return _pcall(
        body, name=name, grid=(T // tm,), in_specs=in_specs, out_specs=out_specs, out_shape=out_shape,
        compiler_params=_cp("arbitrary"))(*args)


def _wgrad(lhs_t, rhs, key, name, g_all=None):
    T, N = rhs.shape
    tn = 256
    nps = N // NCHIP // tn
    off = LOC[key][1]
    cols = GCOLS[LOC[key][0]]

    def body(*refs):
        l_ref, r_ref, o_ref, t_ref = refs[0], refs[1], refs[-2], refs[-1]
        o_ref[...] = _dot(l_ref[...], r_ref[...]).astype(BF16)
        t_ref[...] = jnp.zeros_like(t_ref)

    in_specs = [pl.BlockSpec((D, T), lambda j: (0, 0)), pl.BlockSpec((T, tn), lambda j: (0, j))]
    args = [lhs_t, rhs]
    aliases = {}
    if g_all is not None:
        in_specs.append(pl.BlockSpec(memory_space=pl.ANY))
        args.append(g_all)
        aliases = {2: 0}
    return _pcall(
        body, name=name, grid=(N // tn,), in_specs=in_specs,
        out_specs=[pl.BlockSpec((None, D, tn), lambda j: (j // nps, 0, off // tn + j % nps)),
                   pl.BlockSpec((8, 128), lambda j: (0, 0))],
        out_shape=[jax.ShapeDtypeStruct((NCHIP, D, cols), BF16), jax.ShapeDtypeStruct((8, 128), F32)],
        input_output_aliases=aliases,
        compiler_params=_cp("arbitrary"))(*args)[0]


def _merge_bwd(dmb, wall, off, proj, ro, co):
    T = dmb.shape[0]
    tm = min(T, 512)
    ks = D // NCHIP

    def body(d_ref, w_ref, gr_ref, gc_ref, ro_ref, co_ref, dro_ref, drot_ref, dco_ref, dcot_ref, dp_ref):
        d = d_ref[...]
        dmg = jnp.concatenate([_dot(d, w_ref[s]) for s in range(NCHIP)], axis=1)
        sr = _sig(gr_ref[...].astype(F32))
        sc = _sig(gc_ref[...].astype(F32))
        dro = dmg * sr
        dco = dmg * sc
        dro_ref[...] = dro.astype(BF16)
        drot_ref[...] = dro.T.astype(BF16)
        dco_ref[...] = dco.astype(BF16)
        dcot_ref[...] = dco.T.astype(BF16)
        dp_ref[:, 0:D] = (dmg * ro_ref[...].astype(F32) * sr * (1.0 - sr)).astype(BF16)
        dp_ref[:, D:2 * D] = (dmg * co_ref[...].astype(F32) * sc * (1.0 - sc)).astype(BF16)

    row = pl.BlockSpec((tm, D), lambda i: (i, 0))
    col = pl.BlockSpec((D, tm), lambda i: (0, i))
    return _pcall(
        body, name="merge_bwd", grid=(T // tm,),
        in_specs=[row, pl.BlockSpec((NCHIP, D, ks), lambda i: (0, 0, off // ks)),
                  pl.BlockSpec((tm, D), lambda i: (i, 8)), pl.BlockSpec((tm, D), lambda i: (i, 9)), row, row],
        out_specs=[row, col, row, col, pl.BlockSpec((tm, 2 * D), lambda i: (i, 4))],
        out_shape=[jax.ShapeDtypeStruct((T, D), BF16), jax.ShapeDtypeStruct((D, T), BF16),
                   jax.ShapeDtypeStruct((T, D), BF16), jax.ShapeDtypeStruct((D, T), BF16),
                   jax.ShapeDtypeStruct((T, INW), BF16)],
        compiler_params=_cp("parallel"))(dmb, wall, proj, proj, ro, co)


def _reto_bwd(dro, wall, off, r, proj, gn_g, dproj):
    T = dro.shape[0]
    tm = min(T, 512)
    hps = H // NCHIP

    def body(d_ref, w_ref, r_ref, g_ref, gn_ref, _, dr_ref, dgn_ref, dp_ref):
        i = pl.program_id(1)
        dri = _dot(d_ref[...], w_ref[...])
        rr = r_ref[...]
        mu = jnp.mean(rr, axis=-1, keepdims=True)
        xc = rr - mu
        var = jnp.mean(xc * xc, axis=-1, keepdims=True)
        rstd = lax.rsqrt(var + EPS)
        rn = xc * rstd
        gn = gn_ref[...]
        g = g_ref[...].astype(F32)
        sg = _sig(g)
        dy = dri * (g * sg)
        dp_ref[...] = (dri * (rn * gn) * (sg * (1.0 + g * (1.0 - sg)))).astype(BF16)
        _acc_rows(dgn_ref, i == 0, _colsum(dy * rn))
        dr_ref[...] = _ln_bwd(dy, rn, rstd, gn).astype(BF16)

    return _pcall(
        body, name="reto_bwd", grid=(H, T // tm),
        in_specs=[pl.BlockSpec((tm, D), lambda j, i: (i, 0)),
                  pl.BlockSpec((None, D, DV), lambda j, i: (j // hps, 0, off // DV + j % hps)),
                  pl.BlockSpec((tm, DV), lambda j, i: (i, j)),
                  pl.BlockSpec((tm, DV), lambda j, i: (i, 2 * VW // DV + j)),
                  pl.BlockSpec((1, DV), lambda j, i: (0, j)),
                  pl.BlockSpec(memory_space=pl.ANY)],
        out_specs=[pl.BlockSpec((tm, DV), lambda j, i: (i, j)), pl.BlockSpec((1, DV), lambda j, i: (0, j)),
                   pl.BlockSpec((tm, DV), lambda j, i: (i, 2 * VW // DV + j))],
        out_shape=[jax.ShapeDtypeStruct((T, VW), BF16), jax.ShapeDtypeStruct((1, VW), F32),
                   jax.ShapeDtypeStruct((T, INW), BF16)],
        input_output_aliases={5: 2},
        compiler_params=_cp("arbitrary", "arbitrary"))(dro, wall, r, proj, gn_g, dproj)


def _retention_bwd(proj, dr, states, cos_t, sin_t, dm_t, xi_t, zeta_t, cds, dproj):
    T = proj.shape[0]
    n = T // CH
    scale = DK ** -0.5

    def body(q_ref, k_ref, v_ref, dr_ref, st_ref, cos_ref, sin_ref, dm_ref, xi_ref, zt_ref, _, dp_ref, ds):
        @pl.when(pl.program_id(0) == 0)
        def _():
            ds[...] = jnp.zeros_like(ds)

        cos = cos_ref[...]
        sin = sin_ref[...]

        def unrope(d):
            return d * cos + pltpu.roll(d * sin, DK // 2, 1)

        for h in range(H):
            q = q_ref[:, h * DK:(h + 1) * DK]
            k = k_ref[:, h * DK:(h + 1) * DK]
            v = v_ref[:, h * DV:(h + 1) * DV]
            d_r = dr_ref[:, h * DV:(h + 1) * DV]
            rows = slice(h * DK, (h + 1) * DK)
            s_b = st_ref[rows, :]
            dm = dm_ref[h]
            zt = zt_ref[h]
            sc = _dot_nt(q, k) * dm
            dsc = _dot_nt(d_r, v) * dm
            drx = (d_r.astype(F32) * xi_ref[h]).astype(BF16)
            ds_prev = ds[rows, :]
            ds_b = ds_prev.astype(BF16)
            kz = (k.astype(F32) * zt).astype(BF16)
            dq = _dot(dsc.astype(BF16), k) + _dot_nt(drx, s_b)
            dk = _dot(dsc.T.astype(BF16), q) + _dot_nt(v, ds_b) * zt
            dv = _dot(sc.T.astype(BF16), d_r) + _dot(kz, ds_b)
            ds[rows, :] = cds[h] * ds_prev + _dot(q.astype(F32).T.astype(BF16), drx)
            dp_ref[:, h * DK:(h + 1) * DK] = unrope(dq * scale).astype(BF16)
            dp_ref[:, D + h * DK:D + (h + 1) * DK] = unrope(dk).astype(BF16)
            dp_ref[:, 2 * D + h * DV:2 * D + (h + 1) * DV] = dv.astype(BF16)

    rv = lambda c: n - 1 - c
    full3 = lambda shp: pl.BlockSpec(shp, lambda c: (0, 0, 0))
    return _pcall(
        body, name="retention_bwd", grid=(n,),
        in_specs=[pl.BlockSpec((CH, D), lambda c: (rv(c), 0)),
                  pl.BlockSpec((CH, D), lambda c: (rv(c), 1)),
                  pl.BlockSpec((CH, VW), lambda c: (rv(c), 1)),
                  pl.BlockSpec((CH, VW), lambda c: (rv(c), 0)),
                  pl.BlockSpec((None, H * DK, DV), lambda c: (rv(c), 0, 0)),
                  pl.BlockSpec((CH, DK), lambda c: (rv(c), 0)),
                  pl.BlockSpec((CH, DK), lambda c: (rv(c), 0)),
                  full3((H, CH, CH)), full3((H, CH, DV)), full3((H, CH, DK)),
                  pl.BlockSpec(memory_space=pl.ANY)],
        out_specs=pl.BlockSpec((CH, 2 * D + VW), lambda c: (rv(c), 0)),
        out_shape=jax.ShapeDtypeStruct((T, INW), BF16),
        input_output_aliases={10: 0},
        scratch_shapes=[pltpu.VMEM((H * DK, DV), F32)],
        compiler_params=_cp("arbitrary"))(proj, proj, proj, dr, states, cos_t, sin_t, dm_t, xi_t, zeta_t, dproj)


def _convo_bwd(dco, wall, off, u1, ln_g, ln_b):
    T = dco.shape[0]
    tm = min(T, 512)
    ks = D // NCHIP

    def body(d_ref, w_ref, u1_ref, g_ref, b_ref, du1_ref, dg_ref, db_ref, dcb_ref):
        i = pl.program_id(0)
        d = d_ref[...]
        du3 = jnp.concatenate([_dot(d, w_ref[s]) for s in range(NCHIP)], axis=1)
        gam = g_ref[...]
        u2, xh, rstd = _ln_fwd(u1_ref[...], gam, b_ref[...])
        sg = _sig(u2)
        du2 = du3 * (sg * (1.0 + u2 * (1.0 - sg)))
        du1 = _ln_bwd(du2, xh, rstd, gam)
        du1_ref[...] = du1
        _acc_rows(dg_ref, i == 0, _colsum(du2 * xh))
        _acc_rows(db_ref, i == 0, _colsum(du2))
        _acc_rows(dcb_ref, i == 0, _colsum(du1))

    row = pl.BlockSpec((tm, D), lambda i: (i, 0))
    vec = pl.BlockSpec((1, D), lambda i: (0, 0))
    return _pcall(
        body, name="convo_bwd", grid=(T // tm,),
        in_specs=[row, pl.BlockSpec((NCHIP, D, ks), lambda i: (0, 0, off // ks)), row, vec, vec],
        out_specs=[row, vec, vec, vec],
        out_shape=[jax.ShapeDtypeStruct((T, D), F32)] + [jax.ShapeDtypeStruct((1, D), F32)] * 3,
        compiler_params=_cp("arbitrary"))(dco, wall, u1, ln_g, ln_b)


def _conv_bwd(du1, proj, conv_k, dproj):
    T = du1.shape[0]
    tt = min(T, CONV_TT)
    nt = T // tt
    ca, cb = 6, 7

    def body(d_ref, dn_ref, a_ref, b_ref, pa_ref, pb_ref, k_ref, _, dp_ref, dk_ref, win_u, win_d, sh_u, sh_d):
        i = pl.program_id(0)
        a, sb = _glu(a_ref, b_ref)
        win_u[HALO:tt + HALO, :] = a * sb
        pa, psb = _glu(pa_ref, pb_ref, slice(tt - HALO, tt))
        win_u[0:HALO, :] = jnp.where(i > 0, pa * psb, 0.0)
        win_d[0:tt, :] = d_ref[...]
        win_d[tt:tt + HALO, :] = jnp.where(i < nt - 1, dn_ref[0:HALO, :], 0.0)
        _shift_copies(win_u, sh_u, tt + HALO)
        _shift_copies(win_d, sh_d, tt + HALO)

        @pl.when(i == 0)
        def _():
            dk_ref[...] = jnp.zeros_like(dk_ref)

        for c0 in range(0, D, CONV_CB):
            cs = slice(c0, c0 + CONV_CB)
            for r0 in range(0, tt, CONV_SB):
                acc = jnp.zeros((CONV_SB, CONV_CB), F32)
                for w in range(CONV_W):
                    st = r0 + (CONV_W - 1) - w
                    acc += _tap(win_d, sh_d, st, CONV_SB, cs) * k_ref[w:w + 1, cs]
                aa = a_ref[r0:r0 + CONV_SB, cs].astype(F32)
                ss = _sig(b_ref[r0:r0 + CONV_SB, cs].astype(F32))
                dp_ref[r0:r0 + CONV_SB, cs] = (acc * ss).astype(BF16)
                dp_ref[r0:r0 + CONV_SB, c0 + D:c0 + D + CONV_CB] = (acc * aa * ss * (1.0 - ss)).astype(BF16)
            for w in range(CONV_W):
                acc = jnp.zeros((CONV_SB, CONV_CB), F32)
                for r0 in range(0, tt, CONV_SB):
                    st = r0 + HALO - (CONV_W - 1) + w
                    acc += win_d[r0:r0 + CONV_SB, cs] * _tap(win_u, sh_u, st, CONV_SB, cs)
                dk_ref[w:w + 1, cs] += _colsum(acc)

    blk = lambda f, c: pl.BlockSpec((tt, D), lambda i: (f(i), c))
    cur = lambda i: i
    prv = lambda i: jnp.maximum(i - 1, 0)
    nxt = lambda i: jnp.minimum(i + 1, nt - 1)
    return _pcall(
        body, name="conv_bwd", grid=(nt,),
        in_specs=[blk(cur, 0), blk(nxt, 0), blk(cur, ca), blk(cur, cb), blk(prv, ca), blk(prv, cb),
                  pl.BlockSpec((CONV_W, D), lambda i: (0, 0)), pl.BlockSpec(memory_space=pl.ANY)],
        out_specs=[pl.BlockSpec((tt, 2 * D), lambda i: (i, 3)), pl.BlockSpec((HALO, D), lambda i: (0, 0))],
        out_shape=[jax.ShapeDtypeStruct((T, INW), BF16), jax.ShapeDtypeStruct((HALO, D), F32)],
        input_output_aliases={7: 0},
        scratch_shapes=[pltpu.VMEM((tt + HALO + CONV_PAD, D), F32), pltpu.VMEM((tt + HALO + CONV_PAD, D), F32),
                        pltpu.VMEM((SUB - 1, tt + HALO, D), F32), pltpu.VMEM((SUB - 1, tt + HALO, D), F32)],
        compiler_params=_cp("arbitrary"))(du1, du1, proj, proj, proj, proj, conv_k, dproj)


def _local_step(x, target, wts, sp, pos_c, pos_sc, adam):
    T = x.shape[0]
    cos_t, sin_t = _rope_tables(T)
    dm_t, xi_t, zeta_t, cds = _decay_tables()
    wa = lambda key: wts[LOC[key][0]]
    wo = lambda key: LOC[key][1]
    _ORDER.active, _ORDER.token = True, None

    xb, xt = _cast_t(x)
    a1, b1, h1 = _ffn_up(xb, wa("g1"), wo("g1"), wo("u1"), "ffn1_up")
    z1, x1, x1b, x1t = _proj_ln(h1, wa("d1"), wo("d1"), x, sp["ln1_g"], sp["ln1_b"], 0.5, "ffn1_down_ln")
    proj = _inproj(x1b, wa("w_in"), wo("w_in"), sp["b_in"], cos_t, sin_t)
    r, ret_in, states = _retention_fwd(proj, sp["ret_gn_g"], dm_t, xi_t, zeta_t, cds)
    u1, u3 = _conv_fwd(proj, sp["conv_k"], sp["conv_b"], sp["conv_ln_g"], sp["conv_ln_b"])
    ro, co, merged = _merge(ret_in, u3, proj, wa("w_ret_o"), wo("w_ret_o"), wo("w_conv_o"))
    z2, x2, x2b, x2t = _proj_ln(merged, wa("w_out"), wo("w_out"), x1, sp["ln2_g"], sp["ln2_b"], 1.0, "out_proj_ln")
    a2, b2, h2 = _ffn_up(x2b, wa("g2"), wo("g2"), wo("u2"), "ffn2_up")
    (z3,) = _proj_ln(h2, wa("d2"), wo("d2"), x2, sp["ln3_g"], sp["ln3_b"], 0.5, "ffn2_down", want_b=False)

    sg = {}
    rs = {}
    loss, df2b, df2t, dz3, sg["ln3_g"], sg["ln3_b"] = _loss_ln_bwd(z3, sp["ln3_g"], sp["ln3_b"], target, 0.5)
    da2, db2 = _ffn_bwd_h(df2b, wa("d2"), wo("d2"), a2, b2, "ffn2_bwd_h")
    g4 = _wgrad(df2t, h2, "d2", "wgrad_d2")
    g4 = _wgrad(x2t, da2, "g2", "wgrad_g2", g4)
    g4 = _wgrad(x2t, db2, "u2", "wgrad_u2", g4)
    rs[4] = _ReduceScatter(g4, 4, pos_c, pos_sc)
    dmb, dmt, dz2, sg["ln2_g"], sg["ln2_b"] = _dx_bwd(
        [da2, db2], [wo("g2"), wo("u2")], wa("g2"), dz3, "ffn2_dx_ln", ln=(z2, sp["ln2_g"], 1.0))
    rs[4].phase2()
    g3 = _wgrad(dmt, merged, "w_out", "wgrad_out")
    dro, drot, dco, dcot, dproj = _merge_bwd(dmb, wa("w_out"), wo("w_out"), proj, ro, co)
    g3 = _wgrad(drot, ret_in, "w_ret_o", "wgrad_ret_o", g3)
    g3 = _wgrad(dcot, u3, "w_conv_o", "wgrad_conv_o", g3)
    rs[3] = _ReduceScatter(g3, 3, pos_c, pos_sc)
    dr, sg["ret_gn_g"], dproj = _reto_bwd(dro, wa("w_ret_o"), wo("w_ret_o"), r, proj, sp["ret_gn_g"], dproj)
    rs[4].phase3()
    rs[3].phase2()
    dproj = _retention_bwd(proj, dr, states, cos_t, sin_t, dm_t, xi_t, zeta_t, cds, dproj)
    du1, sg["conv_ln_g"], sg["conv_ln_b"], sg["conv_b"] = _convo_bwd(
        dco, wa("w_conv_o"), wo("w_conv_o"), u1, sp["conv_ln_g"], sp["conv_ln_b"])
    dproj, dck = _conv_bwd(du1, proj, sp["conv_k"], dproj)
    sg["conv_k"] = dck[:CONV_W]
    adam(4, rs[4].result())
    rs[3].phase3()
    g2 = _wgrad(x1t, dproj, "w_in", "wgrad_in")
    rs[2] = _ReduceScatter(g2, 2, pos_c, pos_sc)
    df1b, df1t, dz1, sg["ln1_g"], sg["ln1_b"], sg["b_in"] = _dx_bwd(
        [dproj], [wo("w_in")], wa("w_in"), dz2, "mixer_dx_ln", ln=(z1, sp["ln1_g"], 0.5), colsum=True)
    adam(3, rs[3].result())
    rs[2].phase2()
    shapes = {n: sg[n].shape for n in SMALL + ["conv_k"]}
    small_parts = _all_gather_small(_pack_small(sg, loss, SMALL_ROWS), "gather_small")
    da1, db1 = _ffn_bwd_h(df1b, wa("d1"), wo("d1"), a1, b1, "ffn1_bwd_h")
    small_sum = _sum_devices(small_parts, "sum_small")
    g1 = _wgrad(df1t, h1, "d1", "wgrad_d1")
    rs[1] = _ReduceScatter(g1, 1, pos_c, pos_sc)
    g0 = _wgrad(xt, da1, "g1", "wgrad_g1")
    g0 = _wgrad(xt, db1, "u1", "wgrad_u1", g0)
    rs[0] = _ReduceScatter(g0, 0, pos_c, pos_sc)
    rs[2].phase3()
    rs[1].phase2()
    rs[0].phase2()
    (grad_x,) = _dx_bwd([da1, db1], [wo("g1"), wo("u1")], wa("g1"), dz1, "ffn1_dx")
    adam(2, rs[2].result())
    rs[1].phase3()
    rs[0].phase3()
    adam(1, rs[1].result())
    adam(0, rs[0].result())
    _ORDER.active = False
    return grad_x, small_sum, shapes


MESH = pl.DeviceIdType.MESH
ANY = pl.BlockSpec(memory_space=pl.ANY)
HALF = D // 2


def _place():
    x, y, c = lax.axis_index("x"), lax.axis_index("y"), lax.axis_index("c")
    chips = [(1 - x, y), (x, 1 - y), (1 - x, 1 - y)]
    return x, y, c, chips


GATHER_ID = 1


def _gather_weights(wloc, name):
    w_ref = jax.new_ref(wloc, memory_space=pltpu.MemorySpace.HBM)
    o_ref = jax.empty_ref(jax.ShapeDtypeStruct((NCHIP, 2, HALF, wloc.shape[-1]), BF16),
                          memory_space=pltpu.MemorySpace.HBM)
    dma = pltpu.SemaphoreType.DMA

    @pl.kernel(mesh=plsc.ScalarSubcoreMesh(axis_name="sc", num_cores=1), name=name,
               scratch_types=(dma(()), dma((2,)), dma((2,)), dma((3,)), dma((3,)), dma(()), dma(())),
               compiler_params=pltpu.CompilerParams(collective_id=GATHER_ID))
    def launch(lsem, s1, r1, s2, r2, s3, r3):
        x, y, c, _ = _place()
        me = 2 * x + y
        sib = (x, y, 1 - c)
        x_nbr, y_nbr = (1 - x, y, c), (x, 1 - y, c)
        x_chip, y_chip, d_chip = 2 * (1 - x) + y, 2 * x + (1 - y), 2 * (1 - x) + (1 - y)
        _handshake([sib, x_nbr, y_nbr])
        mine = pltpu.make_async_copy(w_ref, o_ref.at[me], lsem)
        mine.start()

        def rc(src, dst, ss, rs, dev):
            return pltpu.make_async_remote_copy(src_ref=src, dst_ref=dst, send_sem=ss, recv_sem=rs,
                                                device_id=dev, device_id_type=MESH)

        first = [rc(w_ref.at[c], o_ref.at[me, c], s1.at[0], r1.at[0], x_nbr),
                 rc(w_ref.at[c], o_ref.at[me, c], s1.at[1], r1.at[1], y_nbr)]
        for cp in first:
            cp.start()
        on_chip = c * x_chip + (1 - c) * y_chip
        other_chip = c * y_chip + (1 - c) * x_chip
        on_to = (c * x + (1 - c) * (1 - x), c * (1 - y) + (1 - c) * y, c)
        slot = o_ref.at[on_chip, c]
        rc(slot, slot, s1.at[1 - c], r1.at[1 - c], sib).wait_recv()
        onward = rc(slot, slot, s3, r3, on_to)
        onward.start()
        passed = [rc(slot, slot, s2.at[0], r2.at[0], sib)]
        passed[0].start()
        slot = o_ref.at[other_chip, c]
        rc(slot, slot, s1.at[c], r1.at[c], sib).wait_recv()
        passed.append(rc(slot, slot, s2.at[1], r2.at[1], sib))
        passed[1].start()
        slot = o_ref.at[d_chip, c]
        rc(slot, slot, s3, r3, sib).wait_recv()
        passed.append(rc(slot, slot, s2.at[2], r2.at[2], sib))
        passed[2].start()
        for j, chip in enumerate([other_chip, on_chip, d_chip]):
            slot = o_ref.at[chip, 1 - c]
            rc(slot, slot, s2.at[j], r2.at[j], sib).wait_recv()
        for cp in first + [onward] + passed:
            cp.wait_send()
        mine.wait()

    launch()
    return o_ref[...]


PAIR_ID = 2
CHIP_ID = 3
HBM = pltpu.MemorySpace.HBM


def _sequencer(name, collective_id, n_sems):
    dma = pltpu.SemaphoreType.DMA
    return pl.kernel(mesh=plsc.ScalarSubcoreMesh(axis_name="sc", num_cores=1), name=name,
                     scratch_types=(dma((n_sems,)), dma((n_sems,))),
                     compiler_params=pltpu.CompilerParams(collective_id=collective_id))


def _handshake(peers):
    barrier = pltpu.get_barrier_semaphore()
    for peer in peers:
        pl.semaphore_signal(barrier, inc=1, device_id=peer, device_id_type=MESH)
    pl.semaphore_wait(barrier, len(peers))


def _pair_exchange(g5, name):
    _, _, hr, cols = g5.shape
    g_ref = jax.new_ref(g5, memory_space=HBM)
    o_ref = jax.empty_ref(jax.ShapeDtypeStruct((NCHIP, hr, cols), g5.dtype), memory_space=HBM)

    @_sequencer(name, PAIR_ID, NCHIP)
    def launch(ss, rs):
        x, y, c, _ = _place()
        sib = (x, y, 1 - c)
        _handshake([sib])
        cps = [pltpu.make_async_remote_copy(src_ref=g_ref.at[j, 1 - c], dst_ref=o_ref.at[j], send_sem=ss.at[j],
                                            recv_sem=rs.at[j], device_id=sib, device_id_type=MESH)
               for j in range(NCHIP)]
        for cp in cps:
            cp.start()
        for cp in cps:
            cp.wait()

    launch()
    return o_ref[...]


RS_TR = 128


def _pair_sum(pos, g5, got, name):
    _, _, hr, cols = g5.shape

    def body(pos_ref, g_ref, r_ref, o_ref):
        o_ref[...] = (g_ref[...].astype(F32) + r_ref[...].astype(F32)).astype(BF16)

    return _pcall(
        body, name=name, scalar_prefetch=1, grid=(NCHIP, hr // RS_TR),
        in_specs=[pl.BlockSpec((None, None, RS_TR, cols), lambda j, i, p: (j, p[0], i, 0)),
                  pl.BlockSpec((None, RS_TR, cols), lambda j, i, p: (j, i, 0))],
        out_specs=pl.BlockSpec((None, RS_TR, cols), lambda j, i, p: (j, i, 0)),
        out_shape=jax.ShapeDtypeStruct((NCHIP, hr, cols), BF16),
        compiler_params=_cp("parallel", "parallel"))(pos, g5, got)


def _chip_exchange(pb, name):
    _, hr, cols = pb.shape
    p_ref = jax.new_ref(pb, memory_space=HBM)
    o_ref = jax.empty_ref(jax.ShapeDtypeStruct((3, hr, cols), BF16), memory_space=HBM)

    @_sequencer(name, CHIP_ID, 3)
    def launch(ss, rs):
        x, y, c, chips = _place()
        _handshake([(px, py, c) for px, py in chips])
        cps = [pltpu.make_async_remote_copy(src_ref=p_ref.at[2 * px + py], dst_ref=o_ref.at[j], send_sem=ss.at[j],
                                            recv_sem=rs.at[j], device_id=(px, py, c), device_id_type=MESH)
               for j, (px, py) in enumerate(chips)]
        for cp in cps:
            cp.start()
        for cp in cps:
            cp.wait()

    launch()
    return o_ref[...]


def _chip_sum(pos, g5, got, peers, name):
    _, _, hr, cols = g5.shape

    def body(pos_ref, g_ref, r_ref, p_ref, o_ref, t_ref):
        acc = g_ref[...].astype(F32) + r_ref[...].astype(F32)
        for j in range(3):
            acc += p_ref[j].astype(F32)
        o_ref[...] = acc
        t_ref[...] = jnp.zeros_like(t_ref)

    return _pcall(
        body, name=name, scalar_prefetch=1, grid=(hr // RS_TR,),
        in_specs=[pl.BlockSpec((None, None, RS_TR, cols), lambda i, p: (p[0], p[1], i, 0)),
                  pl.BlockSpec((None, RS_TR, cols), lambda i, p: (p[0], i, 0)),
                  pl.BlockSpec((3, RS_TR, cols), lambda i, p: (0, i, 0))],
        out_specs=[pl.BlockSpec((None, RS_TR, cols), lambda i, p: (p[1], i, 0)),
                   pl.BlockSpec((8, 128), lambda i, p: (0, 0))],
        out_shape=[jax.ShapeDtypeStruct((2, hr, cols), F32), jax.ShapeDtypeStruct((8, 128), F32)],
        compiler_params=_cp("arbitrary"))(pos, g5, got, peers)


def _pair_share(gsum, name):
    g_ref = jax.new_ref(gsum, memory_space=HBM)

    @_sequencer(name, PAIR_ID, 1)
    def launch(ss, rs):
        x, y, c, _ = _place()
        sib = (x, y, 1 - c)
        _handshake([sib])
        cp = pltpu.make_async_remote_copy(src_ref=g_ref.at[c], dst_ref=g_ref.at[c], send_sem=ss.at[0],
                                          recv_sem=rs.at[0], device_id=sib, device_id_type=MESH)
        cp.start()
        cp.wait_send()
        pltpu.make_async_remote_copy(src_ref=g_ref.at[1 - c], dst_ref=g_ref.at[1 - c], send_sem=ss.at[0],
                                     recv_sem=rs.at[0], device_id=sib, device_id_type=MESH).wait_recv()

    launch()
    return g_ref[...]


class _ReduceScatter:
    def __init__(self, g_arr, gi, pos_c, pos_sc):
        _, rows, cols = g_arr.shape
        self.g5 = g_arr.reshape(NCHIP, 2, rows // 2, cols)
        self.gi, self.pos_c, self.pos_sc = gi, pos_c, pos_sc
        self.got = _pair_exchange(self.g5, f"pair_exchange_{gi}")

    def phase2(self):
        pb = _pair_sum(self.pos_c, self.g5, self.got, f"pair_sum_{self.gi}")
        self.peers = _chip_exchange(pb, f"chip_exchange_{self.gi}")

    def phase3(self):
        gsum, _ = _chip_sum(self.pos_sc, self.g5, self.got, self.peers, f"chip_sum_{self.gi}")
        self.full = _pair_share(gsum, f"pair_share_{self.gi}")

    def result(self):
        _, hr, cols = self.full.shape
        return self.full.reshape(2 * hr, cols)


SMALL_ROWS = 56


ALL_ID = 4


def _all_gather_small(vec, name):
    v_ref = jax.new_ref(vec, memory_space=HBM)
    o_ref = jax.empty_ref(jax.ShapeDtypeStruct((8, SMALL_ROWS, D), F32), memory_space=HBM)

    @_sequencer(name, ALL_ID, 8)
    def launch(ss, rs):
        x, y, c, _ = _place()
        me = 4 * x + 2 * y + c
        flip = lambda v, bit: 1 - v if bit else v
        peers = [(flip(x, m >> 2), flip(y, (m >> 1) & 1), flip(c, m & 1)) for m in range(1, 8)]
        _handshake(peers)
        mine = pltpu.make_async_copy(v_ref, o_ref.at[me], ss.at[7])
        mine.start()
        cps = [pltpu.make_async_remote_copy(src_ref=v_ref, dst_ref=o_ref.at[me], send_sem=ss.at[k],
                                            recv_sem=rs.at[k], device_id=peer, device_id_type=MESH)
               for k, peer in enumerate(peers)]
        for cp in cps:
            cp.start()
        for cp in cps:
            cp.wait()
        mine.wait()

    launch()
    return o_ref[...]


def _sum_devices(parts, name):
    def body(p_ref, o_ref):
        acc = p_ref[0]
        for d in range(1, 8):
            acc += p_ref[d]
        o_ref[...] = acc

    return _pcall(
        body, name=name, grid=(SMALL_ROWS // 8,),
        in_specs=[pl.BlockSpec((8, 8, D), lambda i: (0, i, 0))],
        out_specs=pl.BlockSpec((8, D), lambda i: (i, 0)),
        out_shape=jax.ShapeDtypeStruct((SMALL_ROWS, D), F32),
        compiler_params=_cp("parallel"))(parts)


def _adamw_math(w, g, m, v):
    m2 = ADAM_B1 * m + (1.0 - ADAM_B1) * g
    v2 = ADAM_B2 * v + (1.0 - ADAM_B2) * (g * g)
    m_hat = m2 / (1.0 - ADAM_B1 ** ADAM_STEP)
    v_hat = v2 / (1.0 - ADAM_B2 ** ADAM_STEP)
    delta = -ADAM_LR * (m_hat / (jnp.sqrt(v_hat) + ADAM_EPS) + ADAM_WD * w)
    return delta, m2, v2


def _adamw(w, g, m, v, name, g_block=None):
    R, C = w.shape
    tr = R
    gw_hint = C if g_block is None else g_block[0]
    for cand in (512, 352, 256, 176, 128, 64, 32, 16, 8):
        if R % cand == 0 and cand * max(C, gw_hint) * 4 <= (2 << 20):
            tr = cand
            break
    gw, gi = (C, 0) if g_block is None else g_block

    def body(w_ref, g_ref, m_ref, v_ref, go_ref, d_ref, mo_ref, vo_ref):
        g = g_ref[:, 0:C]
        d, m2, v2 = _adamw_math(w_ref[...], g, m_ref[...], v_ref[...])
        go_ref[...] = g
        d_ref[...] = d
        mo_ref[...] = m2
        vo_ref[...] = v2

    spec = pl.BlockSpec((tr, C), lambda i: (i, 0))
    return _pcall(
        body, name=name, grid=(R // tr,),
        in_specs=[spec, pl.BlockSpec((tr, gw), lambda i: (i, gi)), spec, spec],
        out_specs=[spec] * 4, out_shape=[jax.ShapeDtypeStruct((R, C), F32)] * 4,
        compiler_params=_cp("parallel"))(w, g, m, v)


BIG = ["ffn1_w_gate", "ffn1_w_up", "ffn1_w_down", "w_in", "w_ret_o", "w_conv_o", "w_out",
       "ffn2_w_gate", "ffn2_w_up", "ffn2_w_down"]
SLAB = {"ffn1_w_gate": "g1", "ffn1_w_up": "u1", "ffn1_w_down": "d1", "w_in": "w_in", "w_ret_o": "w_ret_o",
        "w_conv_o": "w_conv_o", "w_out": "w_out", "ffn2_w_gate": "g2", "ffn2_w_up": "u2", "ffn2_w_down": "d2"}
TRANSPOSED = {"ffn1_w_down", "ffn2_w_down", "w_ret_o", "w_conv_o", "w_out"}
MINOR_ROWS = {"ffn1_w_gate", "ffn1_w_up", "ffn2_w_gate", "ffn2_w_up"}
SMALL = ["ln1_g", "ln1_b", "ln2_g", "ln2_b", "ln3_g", "ln3_b", "conv_ln_g", "conv_ln_b", "conv_b",
         "ret_gn_g", "b_in"]
ORDER = ["ffn1_w_gate", "ffn1_w_up", "ffn1_w_down", "ln1_g", "ln1_b", "w_in", "b_in", "ret_gn_g", "conv_k",
         "conv_b", "conv_ln_g", "conv_ln_b", "w_ret_o", "w_conv_o", "w_out", "ln2_g", "ln2_b",
         "ffn2_w_gate", "ffn2_w_up", "ffn2_w_down", "ln3_g", "ln3_b"]


def _slab_width(name):
    return WIDTH[SLAB[name]]


def _pack_group(weights, keys):
    by_key = {SLAB[n]: n for n in BIG}
    parts = []
    for key in keys:
        w = weights[by_key[key]]
        w = w.T if by_key[key] in TRANSPOSED else w
        parts.append(jnp.pad(w, ((0, 0), (0, WIDTH[key] - w.shape[1]))))
    return jnp.concatenate(parts, axis=1).astype(BF16)


def _pack_small(vals, loss, rows):
    flat = jnp.concatenate([vals[n].reshape(-1) for n in SMALL] + [vals["conv_k"].reshape(-1), loss.reshape(-1)])
    return jnp.pad(flat, (0, rows * D - flat.shape[0])).reshape(rows, D)


def _unpack_small(arr, shapes):
    flat = arr.reshape(-1)
    out, pos = {}, 0
    for n in SMALL + ["conv_k"]:
        size = int(np.prod(shapes[n]))
        out[n] = flat[pos:pos + size].reshape(shapes[n])
        pos += size
    return out, flat[pos]


def kernel(x, ffn1_w_gate, ffn1_w_up, ffn1_w_down, ln1_g, ln1_b, w_in, b_in, ret_gn_g, conv_k, conv_b, conv_ln_g, conv_ln_b, w_ret_o, w_conv_o, w_out, ln2_g, ln2_b, ffn2_w_gate, ffn2_w_up, ffn2_w_down, ln3_g, ln3_b, loss_target, m_ffn1_w_gate, m_ffn1_w_up, m_ffn1_w_down, m_ln1_g, m_ln1_b, m_w_in, m_b_in, m_ret_gn_g, m_conv_k, m_conv_b, m_conv_ln_g, m_conv_ln_b, m_w_ret_o, m_w_conv_o, m_w_out, m_ln2_g, m_ln2_b, m_ffn2_w_gate, m_ffn2_w_up, m_ffn2_w_down, m_ln3_g, m_ln3_b, v_ffn1_w_gate, v_ffn1_w_up, v_ffn1_w_down, v_ln1_g, v_ln1_b, v_w_in, v_b_in, v_ret_gn_g, v_conv_k, v_conv_b, v_conv_ln_g, v_conv_ln_b, v_w_ret_o, v_w_conv_o, v_w_out, v_ln2_g, v_ln2_b, v_ffn2_w_gate, v_ffn2_w_up, v_ffn2_w_down, v_ln3_g, v_ln3_b):
    args = dict(locals())
    w = {n: args[n] for n in ORDER}
    m = {n: args["m_" + n] for n in ORDER}
    v = {n: args["v_" + n] for n in ORDER}
    xi, yi, ci = lax.axis_index("x"), lax.axis_index("y"), lax.axis_index("c")
    chip = 2 * xi + yi

    shards = {n: w[n][0] for n in BIG}
    wts = []
    for gi, keys in enumerate(GROUPS):
        slab = _pack_group(shards, keys)
        cols = slab.shape[1]
        wts.append(_gather_weights(slab.reshape(2, HALF, cols), f"gather_{gi}").reshape(NCHIP, D, cols))

    sp = {n: w[n] for n in SMALL}
    sp["conv_k"] = None
    kfull_shape = (CONV_W, D)
    kpad = jnp.zeros(kfull_shape, F32)
    kpad = lax.dynamic_update_slice(kpad, w["conv_k"][0, :, 0, :] * jnp.where(ci == 0, 1.0, 0.0), (0, chip * (D // NCHIP)))
    kvec = jnp.pad(kpad.reshape(-1), (0, SMALL_ROWS * D - CONV_W * D)).reshape(SMALL_ROWS, D)
    kall = _sum_devices(_all_gather_small(kvec, "gather_conv_k"), "sum_conv_k")
    sp["conv_k"] = kall.reshape(-1)[:CONV_W * D].reshape(kfull_shape)
    pos_c = jnp.reshape(ci, (1,)).astype(jnp.int32)
    pos_sc = jnp.stack([chip, ci]).astype(jnp.int32)
    out = {}

    def adam(gi, slab):
        for n in BIG:
            (g_of, off), width = LOC[SLAB[n]], _slab_width(n)
            if g_of != gi:
                continue
            w2 = w[n][0]
            if n in TRANSPOSED:
                res = _adamw(w2, slab[:, off:off + w2.shape[0]].T, m[n][0], v[n][0], "adamw_" + n)
                out[n] = [r[None] for r in res]
            elif n in MINOR_ROWS:
                res = _adamw(w2.T, slab[:, off:off + w2.shape[1]].T, m[n][0].T, v[n][0].T, "adamw_" + n)
                out[n] = [r.T[None] for r in res]
            else:
                res = _adamw(w2, slab, m[n][0], v[n][0], "adamw_" + n, g_block=(width, off // width))
                out[n] = [r[None] for r in res]

    grad_x, small_sum, shapes = _local_step(x[0], loss_target[0], wts, sp, pos_c, pos_sc, adam)
    small, total = _unpack_small(small_sum, shapes)

    for n in SMALL:
        res = _adamw(w[n], small[n], m[n], v[n], "adamw_" + n)
        out[n] = list(res)
    gk = lax.dynamic_slice(small["conv_k"], (0, chip * (D // NCHIP)), (CONV_W, D // NCHIP))
    res = _adamw(w["conv_k"][0, :, 0, :], gk, m["conv_k"][0, :, 0, :], v["conv_k"][0, :, 0, :], "adamw_conv_k")
    out["conv_k"] = [r[None, :, None, :] for r in res]

    grads = [out[n][0] for n in ORDER]
    deltas = [out[n][1] for n in ORDER]
    new_m = [out[n][2] for n in ORDER]
    new_v = [out[n][3] for n in ORDER]
    return (total, grad_x[None], *grads, *deltas, *new_m, *new_v)
```

```python
import dataclasses
import functools

import numpy as np
import jax
import jax.numpy as jnp
from jax import lax
from jax.experimental import pallas as pl
from jax.experimental.pallas import tpu as pltpu
from jax.experimental.pallas import tpu_sc as plsc

F32 = jnp.float32
BF16 = jnp.bfloat16

D = 1024
FS = 704
FSP = 768
FP = 4 * FSP
H = 8
DK = 128
DV = 256
CH = 128
VW = H * DV
INW = 10240
INS = INW // 4
CONV_W = 31
HALO = 32
EPS = 1e-5
ALPHA = 2.0 ** 0.25
ROPE_BASE = 10000.0
NCHIP = 4

ADAM_LR, ADAM_B1, ADAM_B2, ADAM_EPS, ADAM_WD, ADAM_STEP = 0.001, 0.9, 0.999, 1e-08, 0.01, 10

OFF = {"w_in": 0, "w_ret_o": 2560, "g1": 3072, "u1": 3840, "d1": 4608,
       "g2": 5376, "u2": 6144, "d2": 6912, "w_conv_o": 7680, "w_out": 7936}
WCOLS = 8192
WIDTH = {"w_in": INS, "w_ret_o": VW // NCHIP, "w_conv_o": D // NCHIP, "w_out": D // NCHIP,
         "g1": FSP, "u1": FSP, "d1": FSP, "g2": FSP, "u2": FSP, "d2": FSP}
GROUPS = (("g1", "u1"), ("d1",), ("w_in",), ("w_ret_o", "w_conv_o", "w_out"), ("g2", "u2", "d2"))
LOC = {}
for _gi, _keys in enumerate(GROUPS):
    _off = 0
    for _k in _keys:
        LOC[_k] = (_gi, _off)
        _off += WIDTH[_k]
GCOLS = [sum(WIDTH[k] for k in keys) for keys in GROUPS]
VMEM_LIMIT = 56 << 20


def _cp(*sem, **kw):
    return pltpu.CompilerParams(dimension_semantics=sem, vmem_limit_bytes=VMEM_LIMIT, **kw)


class _ProgramOrder:
    def __init__(self):
        self.active = False
        self.token = None


_ORDER = _ProgramOrder()


def _pcall(body, *, in_specs, scalar_prefetch=0, **kw):
    def call(*args):
        dep = _ORDER.token if _ORDER.active else None
        specs, fn = list(in_specs), body
        if dep is not None:
            n = len(args)

            def fn(*refs):
                return body(*refs[:n], *refs[n + 1:])

            specs.append(pl.BlockSpec(memory_space=pl.ANY))
            args = (*args, dep)
        params = dict(kw)
        if scalar_prefetch:
            params["grid_spec"] = pltpu.PrefetchScalarGridSpec(
                num_scalar_prefetch=scalar_prefetch, grid=params.pop("grid"), in_specs=specs,
                out_specs=params.pop("out_specs"))
        else:
            params["in_specs"] = specs
        out = pl.pallas_call(fn, **params)(*args)
        if _ORDER.active:
            _ORDER.token = jax.tree.leaves(out)[-1]
        return out

    return call


def _resident(shape, col_block):
    lead = (0,) * (len(shape) - 1)
    return pl.BlockSpec(shape, lambda *_: (*lead, col_block), pipeline_mode=pl.Buffered(1))


def _sig(x):
    return 1.0 / (1.0 + jnp.exp(-x))


def _dot(a, b):
    return jnp.dot(a, b, preferred_element_type=F32)


def _dot_nt(a, b):
    return lax.dot_general(a, b, (((1,), (1,)), ((), ())), preferred_element_type=F32)


def _ln_fwd(z, g, b):
    mu = jnp.mean(z, axis=-1, keepdims=True)
    xc = z - mu
    var = jnp.mean(xc * xc, axis=-1, keepdims=True)
    rstd = lax.rsqrt(var + EPS)
    xh = xc * rstd
    return xh * g + b, xh, rstd


def _ln_bwd(dy, xh, rstd, g):
    dxh = dy * g
    m1 = jnp.mean(dxh, axis=-1, keepdims=True)
    m2 = jnp.mean(dxh * xh, axis=-1, keepdims=True)
    return rstd * (dxh - m1 - xh * m2)


def _colsum(x):
    return jnp.sum(x, axis=0, keepdims=True)


def _acc_rows(ref, first, val):
    @pl.when(first)
    def _():
        ref[...] = val

    @pl.when(jnp.logical_not(first))
    def _():
        ref[...] += val


def _rope_tables(T):
    half = DK // 2
    freqs = ROPE_BASE ** (-np.arange(half, dtype=np.float32) / half)
    ang = (np.arange(T, dtype=np.float32)[:, None] * freqs[None, :]).astype(np.float32)
    cos, sin = np.cos(ang), np.sin(ang)
    return (jnp.asarray(np.concatenate([cos, cos], 1), F32),
            jnp.asarray(np.concatenate([-sin, sin], 1), F32))


def _decay_tables():
    h = np.arange(H, dtype=np.float64)
    log_g = np.log(1.0 - np.exp2(-5.0 - h))
    idx = np.arange(CH, dtype=np.float64)
    diff = idx[:, None] - idx[None, :]
    dm = np.where(diff[None] >= 0, np.exp(np.maximum(diff, 0.0)[None] * log_g[:, None, None]), 0.0)
    xi = np.exp((idx[None, :] + 1.0) * log_g[:, None])
    zeta = np.exp((CH - 1.0 - idx)[None, :] * log_g[:, None])
    cd = np.exp(CH * log_g)
    xi_t = np.broadcast_to(xi[:, :, None], (H, CH, DV))
    zeta_t = np.broadcast_to(zeta[:, :, None], (H, CH, DK))
    return (jnp.asarray(dm, F32), jnp.asarray(xi_t, F32), jnp.asarray(zeta_t, F32),
            [float(v) for v in cd])


def _cast_t(x):
    T = x.shape[0]
    tm = min(T, 512)

    def body(x_ref, xb_ref, xt_ref):
        v = x_ref[...]
        xb_ref[...] = v.astype(BF16)
        xt_ref[...] = v.T.astype(BF16)

    return _pcall(
        body, name="cast_t", grid=(T // tm,),
        in_specs=[pl.BlockSpec((tm, D), lambda i: (i, 0))],
        out_specs=[pl.BlockSpec((tm, D), lambda i: (i, 0)), pl.BlockSpec((D, tm), lambda i: (0, i))],
        out_shape=[jax.ShapeDtypeStruct((T, D), BF16), jax.ShapeDtypeStruct((D, T), BF16)],
        compiler_params=_cp("parallel"))(x)


def _ffn_up(xb, wall, og, ou, name):
    T = xb.shape[0]
    tm = min(T, 512)
    assert ou == og + FSP

    def body(x_ref, w_ref, a_ref, b_ref, h_ref):
        x = x_ref[...]
        for s in range(NCHIP):
            cols = slice(s * FSP, (s + 1) * FSP)
            a = _dot(x, w_ref[s, :, 0:FSP])
            b = _dot(x, w_ref[s, :, FSP:2 * FSP])
            a_ref[:, cols] = a.astype(BF16)
            b_ref[:, cols] = b.astype(BF16)
            h_ref[:, cols] = (a * _sig(a) * b).astype(BF16)

    ospec = pl.BlockSpec((tm, FP), lambda i: (i, 0))
    return _pcall(
        body, name=name, grid=(T // tm,),
        in_specs=[pl.BlockSpec((tm, D), lambda i: (i, 0)), _resident((NCHIP, D, 2 * FSP), og // (2 * FSP))],
        out_specs=[ospec] * 3, out_shape=[jax.ShapeDtypeStruct((T, FP), BF16)] * 3,
        compiler_params=_cp("parallel"))(xb, wall)


def _proj_ln(hb, wall, off, res, g, b, coef, name, want_b=True):
    T, K = hb.shape
    ks = K // NCHIP
    tm = min(T, 256)

    def body(h_ref, w_ref, r_ref, g_ref, b_ref, z_ref, *rest):
        acc = _dot_nt(h_ref[:, 0:ks], w_ref[0])
        for s in range(1, NCHIP):
            acc += _dot_nt(h_ref[:, s * ks:(s + 1) * ks], w_ref[s])
        z = ALPHA * r_ref[...] + coef * acc
        z_ref[...] = z
        if want_b:
            y, _, _ = _ln_fwd(z, g_ref[...], b_ref[...])
            y_ref, yb_ref, yt_ref = rest
            y_ref[...] = y
            yb_ref[...] = y.astype(BF16)
            yt_ref[...] = y.T.astype(BF16)

    row = pl.BlockSpec((tm, D), lambda i: (i, 0))
    vec = pl.BlockSpec((1, D), lambda i: (0, 0))
    out_specs = [row]
    out_shape = [jax.ShapeDtypeStruct((T, D), F32)]
    if want_b:
        out_specs += [row, row, pl.BlockSpec((D, tm), lambda i: (0, i))]
        out_shape += [jax.ShapeDtypeStruct((T, D), F32), jax.ShapeDtypeStruct((T, D), BF16),
                      jax.ShapeDtypeStruct((D, T), BF16)]
    return _pcall(
        body, name=name, grid=(T // tm,),
        in_specs=[pl.BlockSpec((tm, K), lambda i: (i, 0)),
                  pl.BlockSpec((NCHIP, D, ks), lambda i: (0, 0, off // ks)), row, vec, vec],
        out_specs=out_specs, out_shape=out_shape,
        compiler_params=_cp("parallel"))(hb, wall, res, g, b)


def _inproj(xb, wall, off, b_in, cos_t, sin_t):
    T = xb.shape[0]
    tm, tn = min(T, 512), 512
    assert off == 0

    def body(x_ref, w_ref, bias_ref, cos_ref, sin_ref, o_ref):
        x = x_ref[...]
        c = cos_ref[...]
        s = sin_ref[...]
        for n0 in range(0, INW, tn):
            chip, c0 = divmod(n0, INS)
            acc = _dot(x, w_ref[chip, :, c0:c0 + tn]) + bias_ref[:, n0:n0 + tn]
            if n0 >= 2 * D:
                o_ref[:, n0:n0 + tn] = acc.astype(BF16)
                continue
            scale = DK ** -0.5 if n0 < D else 1.0
            for hh in range(tn // DK):
                xh = acc[:, hh * DK:(hh + 1) * DK]
                o = (xh * c + pltpu.roll(xh, DK // 2, 1) * s) * scale
                o_ref[:, n0 + hh * DK:n0 + (hh + 1) * DK] = o.astype(BF16)

    return _pcall(
        body, name="inproj", grid=(T // tm,),
        in_specs=[pl.BlockSpec((tm, D), lambda i: (i, 0)),
                  _resident((NCHIP, D, INS), 0),
                  pl.BlockSpec((1, INW), lambda i: (0, 0)),
                  pl.BlockSpec((tm, DK), lambda i: (i, 0)),
                  pl.BlockSpec((tm, DK), lambda i: (i, 0))],
        out_specs=pl.BlockSpec((tm, INW), lambda i: (i, 0)),
        out_shape=jax.ShapeDtypeStruct((T, INW), BF16),
        compiler_params=_cp("parallel"))(xb, wall, b_in, cos_t, sin_t)


def _retention_fwd(proj, gn_g, dm_t, xi_t, zeta_t, cds):
    T = proj.shape[0]
    n = T // CH

    def body(q_ref, k_ref, v_ref, g_ref, gn_ref, dm_ref, xi_ref, zt_ref, r_ref, ri_ref, st_ref, state):
        @pl.when(pl.program_id(0) == 0)
        def _():
            state[...] = jnp.zeros_like(state)

        for h in range(H):
            q = q_ref[:, h * DK:(h + 1) * DK]
            k = k_ref[:, h * DK:(h + 1) * DK]
            v = v_ref[:, h * DV:(h + 1) * DV]
            rows = slice(h * DK, (h + 1) * DK)
            s_prev = state[rows, :]
            s_b = s_prev.astype(BF16)
            st_ref[rows, :] = s_b
            sc = _dot_nt(q, k) * dm_ref[h]
            r = _dot(sc.astype(BF16), v) + _dot(q, s_b) * xi_ref[h]
            kz = k.astype(F32) * zt_ref[h]
            state[rows, :] = cds[h] * s_prev + _dot(kz.T.astype(BF16), v)
            cols = slice(h * DV, (h + 1) * DV)
            r_ref[:, cols] = r
            mu = jnp.mean(r, axis=-1, keepdims=True)
            xc = r - mu
            var = jnp.mean(xc * xc, axis=-1, keepdims=True)
            y = xc * lax.rsqrt(var + EPS) * gn_ref[:, cols]
            g = g_ref[:, cols].astype(F32)
            ri_ref[:, cols] = (g * _sig(g) * y).astype(BF16)

    full3 = lambda shp: pl.BlockSpec(shp, lambda c: (0, 0, 0))
    return _pcall(
        body, name="retention_fwd", grid=(n,),
        in_specs=[pl.BlockSpec((CH, D), lambda c: (c, 0)),
                  pl.BlockSpec((CH, D), lambda c: (c, 1)),
                  pl.BlockSpec((CH, VW), lambda c: (c, 1)),
                  pl.BlockSpec((CH, VW), lambda c: (c, 2)),
                  pl.BlockSpec((1, VW), lambda c: (0, 0)),
                  full3((H, CH, CH)), full3((H, CH, DV)), full3((H, CH, DK))],
        out_specs=[pl.BlockSpec((CH, VW), lambda c: (c, 0)), pl.BlockSpec((CH, VW), lambda c: (c, 0)),
                   pl.BlockSpec((None, H * DK, DV), lambda c: (c, 0, 0))],
        out_shape=[jax.ShapeDtypeStruct((T, VW), F32), jax.ShapeDtypeStruct((T, VW), BF16),
                   jax.ShapeDtypeStruct((n, H * DK, DV), BF16)],
        scratch_shapes=[pltpu.VMEM((H * DK, DV), F32)],
        compiler_params=_cp("arbitrary"))(proj, proj, proj, proj, gn_g, dm_t, xi_t, zeta_t)


CONV_TT = 256
CONV_SB = 64
CONV_CB = 256


SUB = 8
CONV_PAD = 8


def _glu(a_ref, b_ref, rows=slice(None)):
    a = a_ref[rows, :].astype(F32)
    sb = _sig(b_ref[rows, :].astype(F32))
    return a, sb


def _shift_copies(win, sh, rows):
    win[rows:rows + CONV_PAD, :] = jnp.zeros((CONV_PAD, D), F32)
    for b in range(1, SUB):
        sh[b - 1, :, :] = win[b:b + rows, :]


def _tap(win, sh, start, size, cs):
    a, b = divmod(start, SUB)
    src = win if b == 0 else sh.at[b - 1]
    return src[SUB * a:SUB * a + size, cs]


def _conv_fwd(proj, conv_k, conv_b, ln_g, ln_b):
    T = proj.shape[0]
    tt = min(T, CONV_TT)
    ca, cb = 6 * D // D, 7 * D // D

    def body(a_ref, b_ref, pa_ref, pb_ref, k_ref, cb_ref, g_ref, bb_ref, u1_ref, u3_ref, win, sh):
        i = pl.program_id(0)
        a, sb = _glu(a_ref, b_ref)
        win[HALO:tt + HALO, :] = a * sb
        pa, psb = _glu(pa_ref, pb_ref, slice(tt - HALO, tt))
        win[0:HALO, :] = jnp.where(i > 0, pa * psb, 0.0)
        _shift_copies(win, sh, tt + HALO)
        for c0 in range(0, D, CONV_CB):
            cs = slice(c0, c0 + CONV_CB)
            for r0 in range(0, tt, CONV_SB):
                acc = jnp.zeros((CONV_SB, CONV_CB), F32)
                for w in range(CONV_W):
                    st = r0 + HALO - (CONV_W - 1) + w
                    acc += _tap(win, sh, st, CONV_SB, cs) * k_ref[w:w + 1, cs]
                u1_ref[r0:r0 + CONV_SB, cs] = acc + cb_ref[:, cs]
        u2, _, _ = _ln_fwd(u1_ref[...], g_ref[...], bb_ref[...])
        u3_ref[...] = (u2 * _sig(u2)).astype(BF16)

    vec = pl.BlockSpec((1, D), lambda i: (0, 0))
    row = pl.BlockSpec((tt, D), lambda i: (i, 0))
    return _pcall(
        body, name="conv_fwd", grid=(T // tt,),
        in_specs=[pl.BlockSpec((tt, D), lambda i: (i, ca)), pl.BlockSpec((tt, D), lambda i: (i, cb)),
                  pl.BlockSpec((tt, D), lambda i: (jnp.maximum(i - 1, 0), ca)),
                  pl.BlockSpec((tt, D), lambda i: (jnp.maximum(i - 1, 0), cb)),
                  pl.BlockSpec((CONV_W, D), lambda i: (0, 0)), vec, vec, vec],
        out_specs=[row, row],
        out_shape=[jax.ShapeDtypeStruct((T, D), F32), jax.ShapeDtypeStruct((T, D), BF16)],
        scratch_shapes=[pltpu.VMEM((tt + HALO + CONV_PAD, D), F32), pltpu.VMEM((SUB - 1, tt + HALO, D), F32)],
        compiler_params=_cp("parallel"))(proj, proj, proj, proj, conv_k, conv_b, ln_g, ln_b)


def _merge(ret_in, u3, proj, wall, off_r, off_c):
    T = ret_in.shape[0]
    tm = min(T, 512)
    kr, kc = VW // NCHIP, D // NCHIP

    def body(ri_ref, u3_ref, gr_ref, gc_ref, wr_ref, wc_ref, ro_ref, co_ref, m_ref):
        ro = _dot_nt(ri_ref[:, 0:kr], wr_ref[0])
        co = _dot_nt(u3_ref[:, 0:kc], wc_ref[0])
        for s in range(1, NCHIP):
            ro += _dot_nt(ri_ref[:, s * kr:(s + 1) * kr], wr_ref[s])
            co += _dot_nt(u3_ref[:, s * kc:(s + 1) * kc], wc_ref[s])
        ro_ref[...] = ro.astype(BF16)
        co_ref[...] = co.astype(BF16)
        m = _sig(gr_ref[...].astype(F32)) * ro + _sig(gc_ref[...].astype(F32)) * co
        m_ref[...] = m.astype(BF16)

    row = pl.BlockSpec((tm, D), lambda i: (i, 0))
    return _pcall(
        body, name="merge", grid=(T // tm,),
        in_specs=[pl.BlockSpec((tm, VW), lambda i: (i, 0)), row,
                  pl.BlockSpec((tm, D), lambda i: (i, 8)), pl.BlockSpec((tm, D), lambda i: (i, 9)),
                  pl.BlockSpec((NCHIP, D, kr), lambda i: (0, 0, off_r // kr)),
                  pl.BlockSpec((NCHIP, D, kc), lambda i: (0, 0, off_c // kc))],
        out_specs=[row] * 3, out_shape=[jax.ShapeDtypeStruct((T, D), BF16)] * 3,
        compiler_params=_cp("parallel"))(ret_in, u3, proj, proj, wall, wall)


def _loss_ln_bwd(z, g, b, target, coef):
    T = z.shape[0]
    tm = min(T, 256)
    nt = T // tm

    def body(z_ref, g_ref, b_ref, t_ref, loss_ref, dzb_ref, dzt_ref, dz_ref, dg_ref, db_ref, lacc):
        i = pl.program_id(0)
        gam = g_ref[...]
        y, xh, rstd = _ln_fwd(z_ref[...], gam, b_ref[...])
        e = y - t_ref[...]
        part = _colsum(e * e)
        _acc_rows(lacc, i == 0, part)
        dy = e * (1.0 / D)
        dz = _ln_bwd(dy, xh, rstd, gam)
        dz_ref[...] = dz
        dzc = coef * dz
        dzb_ref[...] = dzc.astype(BF16)
        dzt_ref[...] = dzc.T.astype(BF16)
        _acc_rows(dg_ref, i == 0, _colsum(dy * xh))
        _acc_rows(db_ref, i == 0, _colsum(dy))

        @pl.when(i == nt - 1)
        def _():
            loss_ref[...] = (0.5 / D) * jnp.sum(lacc[...], axis=1, keepdims=True)

    row = pl.BlockSpec((tm, D), lambda i: (i, 0))
    vec = pl.BlockSpec((1, D), lambda i: (0, 0))
    return _pcall(
        body, name="loss_ln_bwd", grid=(nt,),
        in_specs=[row, vec, vec, row],
        out_specs=[pl.BlockSpec((1, 1), lambda i: (0, 0)), row, pl.BlockSpec((D, tm), lambda i: (0, i)),
                   row, vec, vec],
        out_shape=[jax.ShapeDtypeStruct((1, 1), F32), jax.ShapeDtypeStruct((T, D), BF16),
                   jax.ShapeDtypeStruct((D, T), BF16), jax.ShapeDtypeStruct((T, D), F32),
                   jax.ShapeDtypeStruct((1, D), F32), jax.ShapeDtypeStruct((1, D), F32)],
        scratch_shapes=[pltpu.VMEM((1, D), F32)],
        compiler_params=_cp("arbitrary"))(z, g, b, target)


def _ffn_bwd_h(dfb, wall, od, a, b, name):
    T = dfb.shape[0]
    tm = min(T, 512)

    def body(d_ref, w_ref, a_ref, b_ref, da_ref, db_ref):
        d = d_ref[...]
        for s in range(NCHIP):
            cols = slice(s * FSP, (s + 1) * FSP)
            dh = _dot(d, w_ref[s])
            a = a_ref[:, cols].astype(F32)
            sg = _sig(a)
            da_ref[:, cols] = (dh * b_ref[:, cols].astype(F32) * (sg * (1.0 + a * (1.0 - sg)))).astype(BF16)
            db_ref[:, cols] = (dh * a * sg).astype(BF16)

    ospec = pl.BlockSpec((tm, FP), lambda i: (i, 0))
    return _pcall(
        body, name=name, grid=(T // tm,),
        in_specs=[pl.BlockSpec((tm, D), lambda i: (i, 0)), _resident((NCHIP, D, FSP), od // FSP), ospec, ospec],
        out_specs=[ospec] * 2, out_shape=[jax.ShapeDtypeStruct((T, FP), BF16)] * 2,
        compiler_params=_cp("parallel"))(dfb, wall, a, b)


def _dx_bwd(lhs, offs, wall, dz_next, name, ln=None, colsum=False):
    T, K = lhs[0].shape
    ks = K // NCHIP
    nl = len(lhs)
    assert list(offs) == [l * ks for l in range(nl)]
    tm = min(T, 512 if K <= FP else 256)

    def body(*refs):
        l_refs = refs[:nl]
        w_ref = refs[nl]
        dzn_ref = refs[nl + 1]
        pos = nl + 2
        if ln is not None:
            z_ref, g_ref = refs[pos:pos + 2]
            pos += 2
        outs = refs[pos:]
        i = pl.program_id(0)
        acc = None
        for s in range(NCHIP):
            rows = slice(s * ks, (s + 1) * ks)
            for l in range(nl):
                part = _dot_nt(w_ref[s, :, l * ks:(l + 1) * ks], l_refs[l][:, rows])
                acc = part if acc is None else acc + part
            if colsum:
                _acc_rows(outs[-1].at[:, rows], i == 0, _colsum(l_refs[0][:, rows].astype(F32)))
        dy = acc.T + ALPHA * dzn_ref[...]
        if ln is None:
            outs[0][...] = dy
        else:
            gam = g_ref[...]
            _, xh, rstd = _ln_fwd(z_ref[...], gam, 0.0)
            dz = _ln_bwd(dy, xh, rstd, gam)
            dzc = ln[2] * dz
            outs[0][...] = dzc.astype(BF16)
            outs[1][...] = dzc.T.astype(BF16)
            outs[2][...] = dz
            _acc_rows(outs[3], i == 0, _colsum(dy * xh))
            _acc_rows(outs[4], i == 0, _colsum(dy))

    row = pl.BlockSpec((tm, D), lambda i: (i, 0))
    vec = pl.BlockSpec((1, D), lambda i: (0, 0))
    in_specs = [pl.BlockSpec((tm, K), lambda i: (i, 0))] * nl + [_resident((NCHIP, D, nl * ks), 0), row]
    args = list(lhs) + [wall, dz_next]
    if ln is None:
        out_specs = [row]
        out_shape = [jax.ShapeDtypeStruct((T, D), F32)]
    else:
        in_specs += [row, vec]
        args += [ln[0], ln[1]]
        out_specs = [row, pl.BlockSpec((D, tm), lambda i: (0, i)), row, vec, vec]
        out_shape = [jax.ShapeDtypeStruct((T, D), BF16), jax.ShapeDtypeStruct((D, T), BF16),
                     jax.ShapeDtypeStruct((T, D), F32), jax.ShapeDtypeStruct((1, D), F32),
                     jax.ShapeDtypeStruct((1, D), F32)]
    if colsum:
        out_specs += [pl.BlockSpec((1, K), lambda i: (0, 0))]
        out_shape += [jax.ShapeDtypeStruct((1, K), F32)]
    return _pcall(
        body, name=name, grid=(T // tm,), in_specs=in_specs, out_specs=out_specs, out_shape=out_shape,
        compiler_params=_cp("arbitrary"))(*args)


def _wgrad(lhs_t, rhs, key, name, g_all=None):
    T, N = rhs.shape
    tn = next(c for c in (768, 512, 256) if (N // NCHIP) % c == 0 and LOC[key][1] % c == 0)
    nps = N // NCHIP // tn
    off = LOC[key][1]
    cols = GCOLS[LOC[key][0]]

    def body(*refs):
        l_ref, r_ref, o_ref, t_ref = refs[0], refs[1], refs[-2], refs[-1]
        o_ref[...] = _dot(l_ref[...], r_ref[...]).astype(BF16)
        t_ref[...] = jnp.zeros_like(t_ref)

    in_specs = [_resident((D, T), 0), pl.BlockSpec((T, tn), lambda j: (0, j))]
    args = [lhs_t, rhs]
    aliases = {}
    if g_all is not None:
        in_specs.append(pl.BlockSpec(memory_space=pl.ANY))
        args.append(g_all)
        aliases = {2: 0}
    return _pcall(
        body, name=name, grid=(N // tn,), in_specs=in_specs,
        out_specs=[pl.BlockSpec((None, D, tn), lambda j: (j // nps, 0, off // tn + j % nps)),
                   pl.BlockSpec((8, 128), lambda j: (0, 0))],
        out_shape=[jax.ShapeDtypeStruct((NCHIP, D, cols), BF16), jax.ShapeDtypeStruct((8, 128), F32)],
        input_output_aliases=aliases,
        compiler_params=_cp("arbitrary"))(*args)[0]


def _merge_bwd(dmb, wall, off, proj, ro, co):
    T = dmb.shape[0]
    tm = min(T, 512)
    ks = D // NCHIP

    def body(d_ref, w_ref, gr_ref, gc_ref, ro_ref, co_ref, dro_ref, drot_ref, dco_ref, dcot_ref, dp_ref):
        d = d_ref[...]
        dmg = jnp.concatenate([_dot(d, w_ref[s]) for s in range(NCHIP)], axis=1)
        sr = _sig(gr_ref[...].astype(F32))
        sc = _sig(gc_ref[...].astype(F32))
        dro = dmg * sr
        dco = dmg * sc
        dro_ref[...] = dro.astype(BF16)
        drot_ref[...] = dro.T.astype(BF16)
        dco_ref[...] = dco.astype(BF16)
        dcot_ref[...] = dco.T.astype(BF16)
        dp_ref[:, 0:D] = (dmg * ro_ref[...].astype(F32) * sr * (1.0 - sr)).astype(BF16)
        dp_ref[:, D:2 * D] = (dmg * co_ref[...].astype(F32) * sc * (1.0 - sc)).astype(BF16)

    row = pl.BlockSpec((tm, D), lambda i: (i, 0))
    col = pl.BlockSpec((D, tm), lambda i: (0, i))
    return _pcall(
        body, name="merge_bwd", grid=(T // tm,),
        in_specs=[row, pl.BlockSpec((NCHIP, D, ks), lambda i: (0, 0, off // ks)),
                  pl.BlockSpec((tm, D), lambda i: (i, 8)), pl.BlockSpec((tm, D), lambda i: (i, 9)), row, row],
        out_specs=[row, col, row, col, pl.BlockSpec((tm, 2 * D), lambda i: (i, 4))],
        out_shape=[jax.ShapeDtypeStruct((T, D), BF16), jax.ShapeDtypeStruct((D, T), BF16),
                   jax.ShapeDtypeStruct((T, D), BF16), jax.ShapeDtypeStruct((D, T), BF16),
                   jax.ShapeDtypeStruct((T, INW), BF16)],
        compiler_params=_cp("parallel"))(dmb, wall, proj, proj, ro, co)


def _reto_bwd(dro, wall, off, r, proj, gn_g, dproj):
    T = dro.shape[0]
    tm = min(T, 512)
    hps = H // NCHIP

    def body(d_ref, w_ref, r_ref, g_ref, gn_ref, _, dr_ref, dgn_ref, dp_ref):
        i = pl.program_id(1)
        dri = _dot(d_ref[...], w_ref[...])
        rr = r_ref[...]
        mu = jnp.mean(rr, axis=-1, keepdims=True)
        xc = rr - mu
        var = jnp.mean(xc * xc, axis=-1, keepdims=True)
        rstd = lax.rsqrt(var + EPS)
        rn = xc * rstd
        gn = gn_ref[...]
        g = g_ref[...].astype(F32)
        sg = _sig(g)
        dy = dri * (g * sg)
        dp_ref[...] = (dri * (rn * gn) * (sg * (1.0 + g * (1.0 - sg)))).astype(BF16)
        _acc_rows(dgn_ref, i == 0, _colsum(dy * rn))
        dr_ref[...] = _ln_bwd(dy, rn, rstd, gn).astype(BF16)

    return _pcall(
        body, name="reto_bwd", grid=(H, T // tm),
        in_specs=[pl.BlockSpec((tm, D), lambda j, i: (i, 0)),
                  pl.BlockSpec((None, D, DV), lambda j, i: (j // hps, 0, off // DV + j % hps)),
                  pl.BlockSpec((tm, DV), lambda j, i: (i, j)),
                  pl.BlockSpec((tm, DV), lambda j, i: (i, 2 * VW // DV + j)),
                  pl.BlockSpec((1, DV), lambda j, i: (0, j)),
                  pl.BlockSpec(memory_space=pl.ANY)],
        out_specs=[pl.BlockSpec((tm, DV), lambda j, i: (i, j)), pl.BlockSpec((1, DV), lambda j, i: (0, j)),
                   pl.BlockSpec((tm, DV), lambda j, i: (i, 2 * VW // DV + j))],
        out_shape=[jax.ShapeDtypeStruct((T, VW), BF16), jax.ShapeDtypeStruct((1, VW), F32),
                   jax.ShapeDtypeStruct((T, INW), BF16)],
        input_output_aliases={5: 2},
        compiler_params=_cp("arbitrary", "arbitrary"))(dro, wall, r, proj, gn_g, dproj)


def _retention_bwd(proj, dr, states, cos_t, sin_t, dm_t, xi_t, zeta_t, cds, dproj):
    T = proj.shape[0]
    n = T // CH
    scale = DK ** -0.5

    def body(q_ref, k_ref, v_ref, dr_ref, st_ref, cos_ref, sin_ref, dm_ref, xi_ref, zt_ref, _, dp_ref, ds):
        @pl.when(pl.program_id(0) == 0)
        def _():
            ds[...] = jnp.zeros_like(ds)

        cos = cos_ref[...]
        sin = sin_ref[...]

        def unrope(d):
            return d * cos + pltpu.roll(d * sin, DK // 2, 1)

        for h in range(H):
            q = q_ref[:, h * DK:(h + 1) * DK]
            k = k_ref[:, h * DK:(h + 1) * DK]
            v = v_ref[:, h * DV:(h + 1) * DV]
            d_r = dr_ref[:, h * DV:(h + 1) * DV]
            rows = slice(h * DK, (h + 1) * DK)
            s_b = st_ref[rows, :]
            dm = dm_ref[h]
            zt = zt_ref[h]
            sc = _dot_nt(q, k) * dm
            dsc = _dot_nt(d_r, v) * dm
            drx = (d_r.astype(F32) * xi_ref[h]).astype(BF16)
            ds_prev = ds[rows, :]
            ds_b = ds_prev.astype(BF16)
            kz = (k.astype(F32) * zt).astype(BF16)
            dq = _dot(dsc.astype(BF16), k) + _dot_nt(drx, s_b)
            dk = _dot(dsc.T.astype(BF16), q) + _dot_nt(v, ds_b) * zt
            dv = _dot(sc.T.astype(BF16), d_r) + _dot(kz, ds_b)
            ds[rows, :] = cds[h] * ds_prev + _dot(q.astype(F32).T.astype(BF16), drx)
            dp_ref[:, h * DK:(h + 1) * DK] = unrope(dq * scale).astype(BF16)
            dp_ref[:, D + h * DK:D + (h + 1) * DK] = unrope(dk).astype(BF16)
            dp_ref[:, 2 * D + h * DV:2 * D + (h + 1) * DV] = dv.astype(BF16)

    rv = lambda c: n - 1 - c
    full3 = lambda shp: pl.BlockSpec(shp, lambda c: (0, 0, 0))
    return _pcall(
        body, name="retention_bwd", grid=(n,),
        in_specs=[pl.BlockSpec((CH, D), lambda c: (rv(c), 0)),
                  pl.BlockSpec((CH, D), lambda c: (rv(c), 1)),
                  pl.BlockSpec((CH, VW), lambda c: (rv(c), 1)),
                  pl.BlockSpec((CH, VW), lambda c: (rv(c), 0)),
                  pl.BlockSpec((None, H * DK, DV), lambda c: (rv(c), 0, 0)),
                  pl.BlockSpec((CH, DK), lambda c: (rv(c), 0)),
                  pl.BlockSpec((CH, DK), lambda c: (rv(c), 0)),
                  full3((H, CH, CH)), full3((H, CH, DV)), full3((H, CH, DK)),
                  pl.BlockSpec(memory_space=pl.ANY)],
        out_specs=pl.BlockSpec((CH, 2 * D + VW), lambda c: (rv(c), 0)),
        out_shape=jax.ShapeDtypeStruct((T, INW), BF16),
        input_output_aliases={10: 0},
        scratch_shapes=[pltpu.VMEM((H * DK, DV), F32)],
        compiler_params=_cp("arbitrary"))(proj, proj, proj, dr, states, cos_t, sin_t, dm_t, xi_t, zeta_t, dproj)


def _convo_bwd(dco, wall, off, u1, ln_g, ln_b):
    T = dco.shape[0]
    tm = min(T, 512)
    ks = D // NCHIP

    def body(d_ref, w_ref, u1_ref, g_ref, b_ref, du1_ref, dg_ref, db_ref, dcb_ref):
        i = pl.program_id(0)
        d = d_ref[...]
        du3 = jnp.concatenate([_dot(d, w_ref[s]) for s in range(NCHIP)], axis=1)
        gam = g_ref[...]
        u2, xh, rstd = _ln_fwd(u1_ref[...], gam, b_ref[...])
        sg = _sig(u2)
        du2 = du3 * (sg * (1.0 + u2 * (1.0 - sg)))
        du1 = _ln_bwd(du2, xh, rstd, gam)
        du1_ref[...] = du1
        _acc_rows(dg_ref, i == 0, _colsum(du2 * xh))
        _acc_rows(db_ref, i == 0, _colsum(du2))
        _acc_rows(dcb_ref, i == 0, _colsum(du1))

    row = pl.BlockSpec((tm, D), lambda i: (i, 0))
    vec = pl.BlockSpec((1, D), lambda i: (0, 0))
    return _pcall(
        body, name="convo_bwd", grid=(T // tm,),
        in_specs=[row, pl.BlockSpec((NCHIP, D, ks), lambda i: (0, 0, off // ks)), row, vec, vec],
        out_specs=[row, vec, vec, vec],
        out_shape=[jax.ShapeDtypeStruct((T, D), F32)] + [jax.ShapeDtypeStruct((1, D), F32)] * 3,
        compiler_params=_cp("arbitrary"))(dco, wall, u1, ln_g, ln_b)


def _conv_bwd(du1, proj, conv_k, dproj):
    T = du1.shape[0]
    tt = min(T, CONV_TT)
    nt = T // tt
    ca, cb = 6, 7

    def body(d_ref, dn_ref, a_ref, b_ref, pa_ref, pb_ref, k_ref, _, dp_ref, dk_ref, win_u, win_d, sh_u, sh_d):
        i = pl.program_id(0)
        a, sb = _glu(a_ref, b_ref)
        win_u[HALO:tt + HALO, :] = a * sb
        pa, psb = _glu(pa_ref, pb_ref, slice(tt - HALO, tt))
        win_u[0:HALO, :] = jnp.where(i > 0, pa * psb, 0.0)
        win_d[0:tt, :] = d_ref[...]
        win_d[tt:tt + HALO, :] = jnp.where(i < nt - 1, dn_ref[0:HALO, :], 0.0)
        _shift_copies(win_u, sh_u, tt + HALO)
        _shift_copies(win_d, sh_d, tt + HALO)

        @pl.when(i == 0)
        def _():
            dk_ref[...] = jnp.zeros_like(dk_ref)

        for c0 in range(0, D, CONV_CB):
            cs = slice(c0, c0 + CONV_CB)
            for r0 in range(0, tt, CONV_SB):
                acc = jnp.zeros((CONV_SB, CONV_CB), F32)
                for w in range(CONV_W):
                    st = r0 + (CONV_W - 1) - w
                    acc += _tap(win_d, sh_d, st, CONV_SB, cs) * k_ref[w:w + 1, cs]
                aa = a_ref[r0:r0 + CONV_SB, cs].astype(F32)
                ss = _sig(b_ref[r0:r0 + CONV_SB, cs].astype(F32))
                dp_ref[r0:r0 + CONV_SB, cs] = (acc * ss).astype(BF16)
                dp_ref[r0:r0 + CONV_SB, c0 + D:c0 + D + CONV_CB] = (acc * aa * ss * (1.0 - ss)).astype(BF16)
            for w in range(CONV_W):
                acc = jnp.zeros((CONV_SB, CONV_CB), F32)
                for r0 in range(0, tt, CONV_SB):
                    st = r0 + HALO - (CONV_W - 1) + w
                    acc += win_d[r0:r0 + CONV_SB, cs] * _tap(win_u, sh_u, st, CONV_SB, cs)
                dk_ref[w:w + 1, cs] += _colsum(acc)

    blk = lambda f, c: pl.BlockSpec((tt, D), lambda i: (f(i), c))
    cur = lambda i: i
    prv = lambda i: jnp.maximum(i - 1, 0)
    nxt = lambda i: jnp.minimum(i + 1, nt - 1)
    return _pcall(
        body, name="conv_bwd", grid=(nt,),
        in_specs=[blk(cur, 0), blk(nxt, 0), blk(cur, ca), blk(cur, cb), blk(prv, ca), blk(prv, cb),
                  pl.BlockSpec((CONV_W, D), lambda i: (0, 0)), pl.BlockSpec(memory_space=pl.ANY)],
        out_specs=[pl.BlockSpec((tt, 2 * D), lambda i: (i, 3)), pl.BlockSpec((HALO, D), lambda i: (0, 0))],
        out_shape=[jax.ShapeDtypeStruct((T, INW), BF16), jax.ShapeDtypeStruct((HALO, D), F32)],
        input_output_aliases={7: 0},
        scratch_shapes=[pltpu.VMEM((tt + HALO + CONV_PAD, D), F32), pltpu.VMEM((tt + HALO + CONV_PAD, D), F32),
                        pltpu.VMEM((SUB - 1, tt + HALO, D), F32), pltpu.VMEM((SUB - 1, tt + HALO, D), F32)],
        compiler_params=_cp("arbitrary"))(du1, du1, proj, proj, proj, proj, conv_k, dproj)


def _local_step(x, target, wts, sp, pos_c, pos_sc, adam):
    T = x.shape[0]
    cos_t, sin_t = _rope_tables(T)
    dm_t, xi_t, zeta_t, cds = _decay_tables()
    wa = lambda key: wts[LOC[key][0]]
    wo = lambda key: LOC[key][1]
    _ORDER.active, _ORDER.token = True, None

    xb, xt = _cast_t(x)
    a1, b1, h1 = _ffn_up(xb, wa("g1"), wo("g1"), wo("u1"), "ffn1_up")
    z1, x1, x1b, x1t = _proj_ln(h1, wa("d1"), wo("d1"), x, sp["ln1_g"], sp["ln1_b"], 0.5, "ffn1_down_ln")
    proj = _inproj(x1b, wa("w_in"), wo("w_in"), sp["b_in"], cos_t, sin_t)
    r, ret_in, states = _retention_fwd(proj, sp["ret_gn_g"], dm_t, xi_t, zeta_t, cds)
    u1, u3 = _conv_fwd(proj, sp["conv_k"], sp["conv_b"], sp["conv_ln_g"], sp["conv_ln_b"])
    ro, co, merged = _merge(ret_in, u3, proj, wa("w_ret_o"), wo("w_ret_o"), wo("w_conv_o"))
    z2, x2, x2b, x2t = _proj_ln(merged, wa("w_out"), wo("w_out"), x1, sp["ln2_g"], sp["ln2_b"], 1.0, "out_proj_ln")
    a2, b2, h2 = _ffn_up(x2b, wa("g2"), wo("g2"), wo("u2"), "ffn2_up")
    (z3,) = _proj_ln(h2, wa("d2"), wo("d2"), x2, sp["ln3_g"], sp["ln3_b"], 0.5, "ffn2_down", want_b=False)

    sg = {}
    rs = {}
    loss, df2b, df2t, dz3, sg["ln3_g"], sg["ln3_b"] = _loss_ln_bwd(z3, sp["ln3_g"], sp["ln3_b"], target, 0.5)
    da2, db2 = _ffn_bwd_h(df2b, wa("d2"), wo("d2"), a2, b2, "ffn2_bwd_h")
    g4 = _wgrad(df2t, h2, "d2", "wgrad_d2")
    g4 = _wgrad(x2t, da2, "g2", "wgrad_g2", g4)
    g4 = _wgrad(x2t, db2, "u2", "wgrad_u2", g4)
    rs[4] = _ReduceScatter(g4, 4, pos_c, pos_sc)
    dmb, dmt, dz2, sg["ln2_g"], sg["ln2_b"] = _dx_bwd(
        [da2, db2], [wo("g2"), wo("u2")], wa("g2"), dz3, "ffn2_dx_ln", ln=(z2, sp["ln2_g"], 1.0))
    rs[4].phase2()
    g3 = _wgrad(dmt, merged, "w_out", "wgrad_out")
    dro, drot, dco, dcot, dproj = _merge_bwd(dmb, wa("w_out"), wo("w_out"), proj, ro, co)
    g3 = _wgrad(drot, ret_in, "w_ret_o", "wgrad_ret_o", g3)
    g3 = _wgrad(dcot, u3, "w_conv_o", "wgrad_conv_o", g3)
    rs[3] = _ReduceScatter(g3, 3, pos_c, pos_sc)
    dr, sg["ret_gn_g"], dproj = _reto_bwd(dro, wa("w_ret_o"), wo("w_ret_o"), r, proj, sp["ret_gn_g"], dproj)
    rs[4].phase3()
    rs[3].phase2()
    dproj = _retention_bwd(proj, dr, states, cos_t, sin_t, dm_t, xi_t, zeta_t, cds, dproj)
    du1, sg["conv_ln_g"], sg["conv_ln_b"], sg["conv_b"] = _convo_bwd(
        dco, wa("w_conv_o"), wo("w_conv_o"), u1, sp["conv_ln_g"], sp["conv_ln_b"])
    dproj, dck = _conv_bwd(du1, proj, sp["conv_k"], dproj)
    sg["conv_k"] = dck[:CONV_W]
    adam(4, rs[4].result())
    rs[3].phase3()
    g2 = _wgrad(x1t, dproj, "w_in", "wgrad_in")
    rs[2] = _ReduceScatter(g2, 2, pos_c, pos_sc)
    df1b, df1t, dz1, sg["ln1_g"], sg["ln1_b"], sg["b_in"] = _dx_bwd(
        [dproj], [wo("w_in")], wa("w_in"), dz2, "mixer_dx_ln", ln=(z1, sp["ln1_g"], 0.5), colsum=True)
    adam(3, rs[3].result())
    rs[2].phase2()
    shapes = {n: sg[n].shape for n in SMALL + ["conv_k"]}
    small_parts = _all_gather_small(_pack_small(sg, loss, SMALL_ROWS), "gather_small")
    da1, db1 = _ffn_bwd_h(df1b, wa("d1"), wo("d1"), a1, b1, "ffn1_bwd_h")
    small_sum = _sum_devices(small_parts, "sum_small")
    g1 = _wgrad(df1t, h1, "d1", "wgrad_d1")
    rs[1] = _ReduceScatter(g1, 1, pos_c, pos_sc)
    g0 = _wgrad(xt, da1, "g1", "wgrad_g1")
    g0 = _wgrad(xt, db1, "u1", "wgrad_u1", g0)
    rs[0] = _ReduceScatter(g0, 0, pos_c, pos_sc)
    rs[2].phase3()
    rs[1].phase2()
    rs[0].phase2()
    (grad_x,) = _dx_bwd([da1, db1], [wo("g1"), wo("u1")], wa("g1"), dz1, "ffn1_dx")
    adam(2, rs[2].result())
    rs[1].phase3()
    rs[0].phase3()
    adam(1, rs[1].result())
    adam(0, rs[0].result())
    _ORDER.active = False
    return grad_x, small_sum, shapes


MESH = pl.DeviceIdType.MESH
ANY = pl.BlockSpec(memory_space=pl.ANY)
HALF = D // 2


def _place():
    x, y, c = lax.axis_index("x"), lax.axis_index("y"), lax.axis_index("c")
    chips = [(1 - x, y), (x, 1 - y), (1 - x, 1 - y)]
    return x, y, c, chips


GATHER_ID = 1


def _gather_weights(wloc, name):
    w_ref = jax.new_ref(wloc, memory_space=pltpu.MemorySpace.HBM)
    o_ref = jax.empty_ref(jax.ShapeDtypeStruct((NCHIP, 2, HALF, wloc.shape[-1]), BF16),
                          memory_space=pltpu.MemorySpace.HBM)
    dma = pltpu.SemaphoreType.DMA

    @pl.kernel(mesh=plsc.ScalarSubcoreMesh(axis_name="sc", num_cores=1), name=name,
               scratch_types=(dma(()), dma((2,)), dma((2,)), dma((3,)), dma((3,)), dma(()), dma(())),
               compiler_params=pltpu.CompilerParams(collective_id=GATHER_ID))
    def launch(lsem, s1, r1, s2, r2, s3, r3):
        x, y, c, _ = _place()
        me = 2 * x + y
        sib = (x, y, 1 - c)
        x_nbr, y_nbr = (1 - x, y, c), (x, 1 - y, c)
        x_chip, y_chip, d_chip = 2 * (1 - x) + y, 2 * x + (1 - y), 2 * (1 - x) + (1 - y)
        _handshake([sib, x_nbr, y_nbr])
        mine = pltpu.make_async_copy(w_ref, o_ref.at[me], lsem)
        mine.start()

        def rc(src, dst, ss, rs, dev):
            return pltpu.make_async_remote_copy(src_ref=src, dst_ref=dst, send_sem=ss, recv_sem=rs,
                                                device_id=dev, device_id_type=MESH)

        first = [rc(w_ref.at[c], o_ref.at[me, c], s1.at[0], r1.at[0], x_nbr),
                 rc(w_ref.at[c], o_ref.at[me, c], s1.at[1], r1.at[1], y_nbr)]
        for cp in first:
            cp.start()
        on_chip = c * x_chip + (1 - c) * y_chip
        other_chip = c * y_chip + (1 - c) * x_chip
        on_to = (c * x + (1 - c) * (1 - x), c * (1 - y) + (1 - c) * y, c)
        slot = o_ref.at[on_chip, c]
        rc(slot, slot, s1.at[1 - c], r1.at[1 - c], sib).wait_recv()
        onward = rc(slot, slot, s3, r3, on_to)
        onward.start()
        passed = [rc(slot, slot, s2.at[0], r2.at[0], sib)]
        passed[0].start()
        slot = o_ref.at[other_chip, c]
        rc(slot, slot, s1.at[c], r1.at[c], sib).wait_recv()
        passed.append(rc(slot, slot, s2.at[1], r2.at[1], sib))
        passed[1].start()
        slot = o_ref.at[d_chip, c]
        rc(slot, slot, s3, r3, sib).wait_recv()
        passed.append(rc(slot, slot, s2.at[2], r2.at[2], sib))
        passed[2].start()
        for j, chip in enumerate([other_chip, on_chip, d_chip]):
            slot = o_ref.at[chip, 1 - c]
            rc(slot, slot, s2.at[j], r2.at[j], sib).wait_recv()
        for cp in first + [onward] + passed:
            cp.wait_send()
        mine.wait()

    launch()
    return o_ref[...]


PAIR_ID = 2
CHIP_ID = 3
HBM = pltpu.MemorySpace.HBM


def _sequencer(name, collective_id, n_sems):
    dma = pltpu.SemaphoreType.DMA
    return pl.kernel(mesh=plsc.ScalarSubcoreMesh(axis_name="sc", num_cores=1), name=name,
                     scratch_types=(dma((n_sems,)), dma((n_sems,))),
                     compiler_params=pltpu.CompilerParams(collective_id=collective_id))


def _handshake(peers):
    barrier = pltpu.get_barrier_semaphore()
    for peer in peers:
        pl.semaphore_signal(barrier, inc=1, device_id=peer, device_id_type=MESH)
    pl.semaphore_wait(barrier, len(peers))


def _pair_exchange(g5, name):
    _, _, hr, cols = g5.shape
    g_ref = jax.new_ref(g5, memory_space=HBM)
    o_ref = jax.empty_ref(jax.ShapeDtypeStruct((NCHIP, hr, cols), g5.dtype), memory_space=HBM)

    @_sequencer(name, PAIR_ID, NCHIP)
    def launch(ss, rs):
        x, y, c, _ = _place()
        sib = (x, y, 1 - c)
        _handshake([sib])
        cps = [pltpu.make_async_remote_copy(src_ref=g_ref.at[j, 1 - c], dst_ref=o_ref.at[j], send_sem=ss.at[j],
                                            recv_sem=rs.at[j], device_id=sib, device_id_type=MESH)
               for j in range(NCHIP)]
        for cp in cps:
            cp.start()
        for cp in cps:
            cp.wait()

    launch()
    return o_ref[...]


RS_TR = 128


def _pair_sum(pos, g5, got, name):
    _, _, hr, cols = g5.shape

    def body(pos_ref, g_ref, r_ref, o_ref):
        o_ref[...] = (g_ref[...].astype(F32) + r_ref[...].astype(F32)).astype(BF16)

    return _pcall(
        body, name=name, scalar_prefetch=1, grid=(NCHIP, hr // RS_TR),
        in_specs=[pl.BlockSpec((None, None, RS_TR, cols), lambda j, i, p: (j, p[0], i, 0)),
                  pl.BlockSpec((None, RS_TR, cols), lambda j, i, p: (j, i, 0))],
        out_specs=pl.BlockSpec((None, RS_TR, cols), lambda j, i, p: (j, i, 0)),
        out_shape=jax.ShapeDtypeStruct((NCHIP, hr, cols), BF16),
        compiler_params=_cp("parallel", "parallel"))(pos, g5, got)


def _chip_exchange(pb, name):
    _, hr, cols = pb.shape
    p_ref = jax.new_ref(pb, memory_space=HBM)
    o_ref = jax.empty_ref(jax.ShapeDtypeStruct((3, hr, cols), BF16), memory_space=HBM)

    @_sequencer(name, CHIP_ID, 3)
    def launch(ss, rs):
        x, y, c, chips = _place()
        _handshake([(px, py, c) for px, py in chips])
        cps = [pltpu.make_async_remote_copy(src_ref=p_ref.at[2 * px + py], dst_ref=o_ref.at[j], send_sem=ss.at[j],
                                            recv_sem=rs.at[j], device_id=(px, py, c), device_id_type=MESH)
               for j, (px, py) in enumerate(chips)]
        for cp in cps:
            cp.start()
        for cp in cps:
            cp.wait()

    launch()
    return o_ref[...]


def _chip_sum(pos, g5, got, peers, name):
    _, _, hr, cols = g5.shape

    def body(pos_ref, g_ref, r_ref, p_ref, o_ref, t_ref):
        acc = g_ref[...].astype(F32) + r_ref[...].astype(F32)
        for j in range(3):
            acc += p_ref[j].astype(F32)
        o_ref[...] = acc
        t_ref[...] = jnp.zeros_like(t_ref)

    return _pcall(
        body, name=name, scalar_prefetch=1, grid=(hr // RS_TR,),
        in_specs=[pl.BlockSpec((None, None, RS_TR, cols), lambda i, p: (p[0], p[1], i, 0)),
                  pl.BlockSpec((None, RS_TR, cols), lambda i, p: (p[0], i, 0)),
                  pl.BlockSpec((3, RS_TR, cols), lambda i, p: (0, i, 0))],
        out_specs=[pl.BlockSpec((None, RS_TR, cols), lambda i, p: (p[1], i, 0)),
                   pl.BlockSpec((8, 128), lambda i, p: (0, 0))],
        out_shape=[jax.ShapeDtypeStruct((2, hr, cols), F32), jax.ShapeDtypeStruct((8, 128), F32)],
        compiler_params=_cp("arbitrary"))(pos, g5, got, peers)


def _pair_share(gsum, name):
    g_ref = jax.new_ref(gsum, memory_space=HBM)

    @_sequencer(name, PAIR_ID, 1)
    def launch(ss, rs):
        x, y, c, _ = _place()
        sib = (x, y, 1 - c)
        _handshake([sib])
        cp = pltpu.make_async_remote_copy(src_ref=g_ref.at[c], dst_ref=g_ref.at[c], send_sem=ss.at[0],
                                          recv_sem=rs.at[0], device_id=sib, device_id_type=MESH)
        cp.start()
        cp.wait_send()
        pltpu.make_async_remote_copy(src_ref=g_ref.at[1 - c], dst_ref=g_ref.at[1 - c], send_sem=ss.at[0],
                                     recv_sem=rs.at[0], device_id=sib, device_id_type=MESH).wait_recv()

    launch()
    return g_ref[...]


class _ReduceScatter:
    def __init__(self, g_arr, gi, pos_c, pos_sc):
        _, rows, cols = g_arr.shape
        self.g5 = g_arr.reshape(NCHIP, 2, rows // 2, cols)
        self.gi, self.pos_c, self.pos_sc = gi, pos_c, pos_sc
        self.got = _pair_exchange(self.g5, f"pair_exchange_{gi}")

    def phase2(self):
        pb = _pair_sum(self.pos_c, self.g5, self.got, f"pair_sum_{self.gi}")
        self.peers = _chip_exchange(pb, f"chip_exchange_{self.gi}")

    def phase3(self):
        gsum, _ = _chip_sum(self.pos_sc, self.g5, self.got, self.peers, f"chip_sum_{self.gi}")
        self.full = _pair_share(gsum, f"pair_share_{self.gi}")

    def result(self):
        _, hr, cols = self.full.shape
        return self.full.reshape(2 * hr, cols)


SMALL_ROWS = 56


ALL_ID = 4


def _all_gather_small(vec, name):
    v_ref = jax.new_ref(vec, memory_space=HBM)
    o_ref = jax.empty_ref(jax.ShapeDtypeStruct((8, SMALL_ROWS, D), F32), memory_space=HBM)

    @_sequencer(name, ALL_ID, 8)
    def launch(ss, rs):
        x, y, c, _ = _place()
        me = 4 * x + 2 * y + c
        flip = lambda v, bit: 1 - v if bit else v
        peers = [(flip(x, m >> 2), flip(y, (m >> 1) & 1), flip(c, m & 1)) for m in range(1, 8)]
        _handshake(peers)
        mine = pltpu.make_async_copy(v_ref, o_ref.at[me], ss.at[7])
        mine.start()
        cps = [pltpu.make_async_remote_copy(src_ref=v_ref, dst_ref=o_ref.at[me], send_sem=ss.at[k],
                                            recv_sem=rs.at[k], device_id=peer, device_id_type=MESH)
               for k, peer in enumerate(peers)]
        for cp in cps:
            cp.start()
        for cp in cps:
            cp.wait()
        mine.wait()

    launch()
    return o_ref[...]


def _sum_devices(parts, name):
    def body(p_ref, o_ref):
        acc = p_ref[0]
        for d in range(1, 8):
            acc += p_ref[d]
        o_ref[...] = acc

    return _pcall(
        body, name=name, grid=(SMALL_ROWS // 8,),
        in_specs=[pl.BlockSpec((8, 8, D), lambda i: (0, i, 0))],
        out_specs=pl.BlockSpec((8, D), lambda i: (i, 0)),
        out_shape=jax.ShapeDtypeStruct((SMALL_ROWS, D), F32),
        compiler_params=_cp("parallel"))(parts)


def _adamw_math(w, g, m, v):
    m2 = ADAM_B1 * m + (1.0 - ADAM_B1) * g
    v2 = ADAM_B2 * v + (1.0 - ADAM_B2) * (g * g)
    m_hat = m2 / (1.0 - ADAM_B1 ** ADAM_STEP)
    v_hat = v2 / (1.0 - ADAM_B2 ** ADAM_STEP)
    delta = -ADAM_LR * (m_hat / (jnp.sqrt(v_hat) + ADAM_EPS) + ADAM_WD * w)
    return delta, m2, v2


def _adamw(w, g, m, v, name, g_block=None):
    R, C = w.shape
    tr = R
    gw_hint = C if g_block is None else g_block[0]
    for cand in (512, 352, 256, 176, 128, 64, 32, 16, 8):
        if R % cand == 0 and cand * max(C, gw_hint) * 4 <= (2 << 20):
            tr = cand
            break
    gw, gi = (C, 0) if g_block is None else g_block

    def body(w_ref, g_ref, m_ref, v_ref, go_ref, d_ref, mo_ref, vo_ref):
        g = g_ref[:, 0:C]
        d, m2, v2 = _adamw_math(w_ref[...], g, m_ref[...], v_ref[...])
        go_ref[...] = g
        d_ref[...] = d
        mo_ref[...] = m2
        vo_ref[...] = v2

    spec = pl.BlockSpec((tr, C), lambda i: (i, 0))
    return _pcall(
        body, name=name, grid=(R // tr,),
        in_specs=[spec, pl.BlockSpec((tr, gw), lambda i: (i, gi)), spec, spec],
        out_specs=[spec] * 4, out_shape=[jax.ShapeDtypeStruct((R, C), F32)] * 4,
        compiler_params=_cp("parallel"))(w, g, m, v)


BIG = ["ffn1_w_gate", "ffn1_w_up", "ffn1_w_down", "w_in", "w_ret_o", "w_conv_o", "w_out",
       "ffn2_w_gate", "ffn2_w_up", "ffn2_w_down"]
SLAB = {"ffn1_w_gate": "g1", "ffn1_w_up": "u1", "ffn1_w_down": "d1", "w_in": "w_in", "w_ret_o": "w_ret_o",
        "w_conv_o": "w_conv_o", "w_out": "w_out", "ffn2_w_gate": "g2", "ffn2_w_up": "u2", "ffn2_w_down": "d2"}
TRANSPOSED = {"ffn1_w_down", "ffn2_w_down", "w_ret_o", "w_conv_o", "w_out"}
MINOR_ROWS = {"ffn1_w_gate", "ffn1_w_up", "ffn2_w_gate", "ffn2_w_up"}
SMALL = ["ln1_g", "ln1_b", "ln2_g", "ln2_b", "ln3_g", "ln3_b", "conv_ln_g", "conv_ln_b", "conv_b",
         "ret_gn_g", "b_in"]
ORDER = ["ffn1_w_gate", "ffn1_w_up", "ffn1_w_down", "ln1_g", "ln1_b", "w_in", "b_in", "ret_gn_g", "conv_k",
         "conv_b", "conv_ln_g", "conv_ln_b", "w_ret_o", "w_conv_o", "w_out", "ln2_g", "ln2_b",
         "ffn2_w_gate", "ffn2_w_up", "ffn2_w_down", "ln3_g", "ln3_b"]


def _slab_width(name):
    return WIDTH[SLAB[name]]


def _pack_group(weights, keys):
    by_key = {SLAB[n]: n for n in BIG}
    parts = []
    for key in keys:
        w = weights[by_key[key]]
        w = w.T if by_key[key] in TRANSPOSED else w
        parts.append(jnp.pad(w, ((0, 0), (0, WIDTH[key] - w.shape[1]))))
    return jnp.concatenate(parts, axis=1).astype(BF16)


def _pack_small(vals, loss, rows):
    flat = jnp.concatenate([vals[n].reshape(-1) for n in SMALL] + [vals["conv_k"].reshape(-1), loss.reshape(-1)])
    return jnp.pad(flat, (0, rows * D - flat.shape[0])).reshape(rows, D)


def _unpack_small(arr, shapes):
    flat = arr.reshape(-1)
    out, pos = {}, 0
    for n in SMALL + ["conv_k"]:
        size = int(np.prod(shapes[n]))
        out[n] = flat[pos:pos + size].reshape(shapes[n])
        pos += size
    return out, flat[pos]


def kernel(x, ffn1_w_gate, ffn1_w_up, ffn1_w_down, ln1_g, ln1_b, w_in, b_in, ret_gn_g, conv_k, conv_b, conv_ln_g, conv_ln_b, w_ret_o, w_conv_o, w_out, ln2_g, ln2_b, ffn2_w_gate, ffn2_w_up, ffn2_w_down, ln3_g, ln3_b, loss_target, m_ffn1_w_gate, m_ffn1_w_up, m_ffn1_w_down, m_ln1_g, m_ln1_b, m_w_in, m_b_in, m_ret_gn_g, m_conv_k, m_conv_b, m_conv_ln_g, m_conv_ln_b, m_w_ret_o, m_w_conv_o, m_w_out, m_ln2_g, m_ln2_b, m_ffn2_w_gate, m_ffn2_w_up, m_ffn2_w_down, m_ln3_g, m_ln3_b, v_ffn1_w_gate, v_ffn1_w_up, v_ffn1_w_down, v_ln1_g, v_ln1_b, v_w_in, v_b_in, v_ret_gn_g, v_conv_k, v_conv_b, v_conv_ln_g, v_conv_ln_b, v_w_ret_o, v_w_conv_o, v_w_out, v_ln2_g, v_ln2_b, v_ffn2_w_gate, v_ffn2_w_up, v_ffn2_w_down, v_ln3_g, v_ln3_b):
    args = dict(locals())
    w = {n: args[n] for n in ORDER}
    m = {n: args["m_" + n] for n in ORDER}
    v = {n: args["v_" + n] for n in ORDER}
    xi, yi, ci = lax.axis_index("x"), lax.axis_index("y"), lax.axis_index("c")
    chip = 2 * xi + yi

    shards = {n: w[n][0] for n in BIG}
    wts = []
    for gi, keys in enumerate(GROUPS):
        slab = _pack_group(shards, keys)
        cols = slab.shape[1]
        wts.append(_gather_weights(slab.reshape(2, HALF, cols), f"gather_{gi}").reshape(NCHIP, D, cols))

    sp = {n: w[n] for n in SMALL}
    sp["conv_k"] = None
    kfull_shape = (CONV_W, D)
    kpad = jnp.zeros(kfull_shape, F32)
    kpad = lax.dynamic_update_slice(kpad, w["conv_k"][0, :, 0, :] * jnp.where(ci == 0, 1.0, 0.0), (0, chip * (D // NCHIP)))
    kvec = jnp.pad(kpad.reshape(-1), (0, SMALL_ROWS * D - CONV_W * D)).reshape(SMALL_ROWS, D)
    kall = _sum_devices(_all_gather_small(kvec, "gather_conv_k"), "sum_conv_k")
    sp["conv_k"] = kall.reshape(-1)[:CONV_W * D].reshape(kfull_shape)
    pos_c = jnp.reshape(ci, (1,)).astype(jnp.int32)
    pos_sc = jnp.stack([chip, ci]).astype(jnp.int32)
    out = {}

    def adam(gi, slab):
        for n in BIG:
            (g_of, off), width = LOC[SLAB[n]], _slab_width(n)
            if g_of != gi:
                continue
            w2 = w[n][0]
            if n in TRANSPOSED:
                res = _adamw(w2, slab[:, off:off + w2.shape[0]].T, m[n][0], v[n][0], "adamw_" + n)
                out[n] = [r[None] for r in res]
            elif n in MINOR_ROWS:
                res = _adamw(w2.T, slab[:, off:off + w2.shape[1]].T, m[n][0].T, v[n][0].T, "adamw_" + n)
                out[n] = [r.T[None] for r in res]
            else:
                res = _adamw(w2, slab, m[n][0], v[n][0], "adamw_" + n, g_block=(width, off // width))
                out[n] = [r[None] for r in res]

    grad_x, small_sum, shapes = _local_step(x[0], loss_target[0], wts, sp, pos_c, pos_sc, adam)
    small, total = _unpack_small(small_sum, shapes)

    for n in SMALL:
        res = _adamw(w[n], small[n], m[n], v[n], "adamw_" + n)
        out[n] = list(res)
    gk = lax.dynamic_slice(small["conv_k"], (0, chip * (D // NCHIP)), (CONV_W, D // NCHIP))
    res = _adamw(w["conv_k"][0, :, 0, :], gk, m["conv_k"][0, :, 0, :], v["conv_k"][0, :, 0, :], "adamw_conv_k")
    out["conv_k"] = [r[None, :, None, :] for r in res]

    grads = [out[n][0] for n in ORDER]
    deltas = [out[n][1] for n in ORDER]
    new_m = [out[n][2] for n in ORDER]
    new_v = [out[n][3] for n in ORDER]
    return (total, grad_x[None], *grads, *deltas, *new_m, *new_v)
```

```python
import dataclasses
import functools

import numpy as np
import jax
import jax.numpy as jnp
from jax import lax
from jax.experimental import pallas as pl
from jax.experimental.pallas import tpu as pltpu
from jax.experimental.pallas import tpu_sc as plsc

F32 = jnp.float32
BF16 = jnp.bfloat16

D = 1024
FS = 704
FSP = 768
FP = 4 * FSP
H = 8
DK = 128
DV = 256
CH = 128
VW = H * DV
INW = 10240
INS = INW // 4
CONV_W = 31
HALO = 32
EPS = 1e-5
ALPHA = 2.0 ** 0.25
ROPE_BASE = 10000.0
NCHIP = 4

ADAM_LR, ADAM_B1, ADAM_B2, ADAM_EPS, ADAM_WD, ADAM_STEP = 0.001, 0.9, 0.999, 1e-08, 0.01, 10

OFF = {"w_in": 0, "w_ret_o": 2560, "g1": 3072, "u1": 3840, "d1": 4608,
       "g2": 5376, "u2": 6144, "d2": 6912, "w_conv_o": 7680, "w_out": 7936}
WCOLS = 8192
WIDTH = {"w_in": INS, "w_ret_o": VW // NCHIP, "w_conv_o": D // NCHIP, "w_out": D // NCHIP,
         "g1": FSP, "u1": FSP, "d1": FSP, "g2": FSP, "u2": FSP, "d2": FSP}
GROUPS = (("g1", "u1"), ("d1",), ("w_in",), ("w_ret_o", "w_conv_o", "w_out"), ("g2", "u2", "d2"))
LOC = {}
for _gi, _keys in enumerate(GROUPS):
    _off = 0
    for _k in _keys:
        LOC[_k] = (_gi, _off)
        _off += WIDTH[_k]
GCOLS = [sum(WIDTH[k] for k in keys) for keys in GROUPS]
VMEM_LIMIT = 56 << 20


def _cp(*sem, **kw):
    return pltpu.CompilerParams(dimension_semantics=sem, vmem_limit_bytes=VMEM_LIMIT, **kw)


class _ProgramOrder:
    def __init__(self):
        self.active = False
        self.token = None


_ORDER = _ProgramOrder()


def _pcall(body, *, in_specs, scalar_prefetch=0, **kw):
    def call(*args):
        dep = _ORDER.token if _ORDER.active else None
        specs, fn = list(in_specs), body
        if dep is not None:
            n = len(args)

            def fn(*refs):
                return body(*refs[:n], *refs[n + 1:])

            specs.append(pl.BlockSpec(memory_space=pl.ANY))
            args = (*args, dep)
        params = dict(kw)
        if scalar_prefetch:
            params["grid_spec"] = pltpu.PrefetchScalarGridSpec(
                num_scalar_prefetch=scalar_prefetch, grid=params.pop("grid"), in_specs=specs,
                out_specs=params.pop("out_specs"))
        else:
            params["in_specs"] = specs
        out = pl.pallas_call(fn, **params)(*args)
        if _ORDER.active:
            _ORDER.token = jax.tree.leaves(out)[-1]
        return out

    return call


def _resident(shape, col_block):
    lead = (0,) * (len(shape) - 1)
    return pl.BlockSpec(shape, lambda *_: (*lead, col_block), pipeline_mode=pl.Buffered(1))


def _sig(x):
    return 1.0 / (1.0 + jnp.exp(-x))


def _dot(a, b):
    return jnp.dot(a, b, preferred_element_type=F32)


def _dot_nt(a, b):
    return lax.dot_general(a, b, (((1,), (1,)), ((), ())), preferred_element_type=F32)


def _ln_fwd(z, g, b):
    mu = jnp.mean(z, axis=-1, keepdims=True)
    xc = z - mu
    var = jnp.mean(xc * xc, axis=-1, keepdims=True)
    rstd = lax.rsqrt(var + EPS)
    xh = xc * rstd
    return xh * g + b, xh, rstd


def _ln_bwd(dy, xh, rstd, g):
    dxh = dy * g
    m1 = jnp.mean(dxh, axis=-1, keepdims=True)
    m2 = jnp.mean(dxh * xh, axis=-1, keepdims=True)
    return rstd * (dxh - m1 - xh * m2)


def _colsum(x):
    return jnp.sum(x, axis=0, keepdims=True)


def _acc_rows(ref, first, val):
    @pl.when(first)
    def _():
        ref[...] = val

    @pl.when(jnp.logical_not(first))
    def _():
        ref[...] += val


def _rope_tables(T):
    half = DK // 2
    freqs = ROPE_BASE ** (-np.arange(half, dtype=np.float32) / half)
    ang = (np.arange(T, dtype=np.float32)[:, None] * freqs[None, :]).astype(np.float32)
    cos, sin = np.cos(ang), np.sin(ang)
    return (jnp.asarray(np.concatenate([cos, cos], 1), F32),
            jnp.asarray(np.concatenate([-sin, sin], 1), F32))


def _decay_tables():
    h = np.arange(H, dtype=np.float64)
    log_g = np.log(1.0 - np.exp2(-5.0 - h))
    idx = np.arange(CH, dtype=np.float64)
    diff = idx[:, None] - idx[None, :]
    dm = np.where(diff[None] >= 0, np.exp(np.maximum(diff, 0.0)[None] * log_g[:, None, None]), 0.0)
    xi = np.exp((idx[None, :] + 1.0) * log_g[:, None])
    zeta = np.exp((CH - 1.0 - idx)[None, :] * log_g[:, None])
    cd = np.exp(CH * log_g)
    xi_t = np.broadcast_to(xi[:, :, None], (H, CH, DV))
    zeta_t = np.broadcast_to(zeta[:, :, None], (H, CH, DK))
    return (jnp.asarray(dm, F32), jnp.asarray(xi_t, F32), jnp.asarray(zeta_t, F32),
            [float(v) for v in cd])


def _cast_t(x):
    T = x.shape[0]
    tm = min(T, 512)

    def body(x_ref, xb_ref, xt_ref):
        v = x_ref[...]
        xb_ref[...] = v.astype(BF16)
        xt_ref[...] = v.T.astype(BF16)

    return _pcall(
        body, name="cast_t", grid=(T // tm,),
        in_specs=[pl.BlockSpec((tm, D), lambda i: (i, 0))],
        out_specs=[pl.BlockSpec((tm, D), lambda i: (i, 0)), pl.BlockSpec((D, tm), lambda i: (0, i))],
        out_shape=[jax.ShapeDtypeStruct((T, D), BF16), jax.ShapeDtypeStruct((D, T), BF16)],
        compiler_params=_cp("parallel"))(x)


def _ffn_up(xb, wall, og, ou, name):
    T = xb.shape[0]
    tm = min(T, 512)
    assert ou == og + FSP

    def body(x_ref, w_ref, a_ref, b_ref, h_ref):
        x = x_ref[...]
        for s in range(NCHIP):
            cols = slice(s * FSP, (s + 1) * FSP)
            a = _dot(x, w_ref[s, :, 0:FSP])
            b = _dot(x, w_ref[s, :, FSP:2 * FSP])
            a_ref[:, cols] = a.astype(BF16)
            b_ref[:, cols] = b.astype(BF16)
            h_ref[:, cols] = (a * _sig(a) * b).astype(BF16)

    ospec = pl.BlockSpec((tm, FP), lambda i: (i, 0))
    return _pcall(
        body, name=name, grid=(T // tm,),
        in_specs=[pl.BlockSpec((tm, D), lambda i: (i, 0)), _resident((NCHIP, D, 2 * FSP), og // (2 * FSP))],
        out_specs=[ospec] * 3, out_shape=[jax.ShapeDtypeStruct((T, FP), BF16)] * 3,
        compiler_params=_cp("parallel"))(xb, wall)


def _proj_ln(hb, wall, off, res, g, b, coef, name, want_b=True):
    T, K = hb.shape
    ks = K // NCHIP
    tm = min(T, 256)

    def body(h_ref, w_ref, r_ref, g_ref, b_ref, z_ref, *rest):
        acc = _dot_nt(h_ref[:, 0:ks], w_ref[0])
        for s in range(1, NCHIP):
            acc += _dot_nt(h_ref[:, s * ks:(s + 1) * ks], w_ref[s])
        z = ALPHA * r_ref[...] + coef * acc
        z_ref[...] = z
        if want_b:
            y, _, _ = _ln_fwd(z, g_ref[...], b_ref[...])
            y_ref, yb_ref, yt_ref = rest
            y_ref[...] = y
            yb_ref[...] = y.astype(BF16)
            yt_ref[...] = y.T.astype(BF16)

    row = pl.BlockSpec((tm, D), lambda i: (i, 0))
    vec = pl.BlockSpec((1, D), lambda i: (0, 0))
    out_specs = [row]
    out_shape = [jax.ShapeDtypeStruct((T, D), F32)]
    if want_b:
        out_specs += [row, row, pl.BlockSpec((D, tm), lambda i: (0, i))]
        out_shape += [jax.ShapeDtypeStruct((T, D), F32), jax.ShapeDtypeStruct((T, D), BF16),
                      jax.ShapeDtypeStruct((D, T), BF16)]
    return _pcall(
        body, name=name, grid=(T // tm,),
        in_specs=[pl.BlockSpec((tm, K), lambda i: (i, 0)),
                  pl.BlockSpec((NCHIP, D, ks), lambda i: (0, 0, off // ks)), row, vec, vec],
        out_specs=out_specs, out_shape=out_shape,
        compiler_params=_cp("parallel"))(hb, wall, res, g, b)


def _inproj(xb, wall, off, b_in, cos_t, sin_t):
    T = xb.shape[0]
    tm, tn = min(T, 512), 512
    assert off == 0

    def body(x_ref, w_ref, bias_ref, cos_ref, sin_ref, o_ref):
        x = x_ref[...]
        c = cos_ref[...]
        s = sin_ref[...]
        for n0 in range(0, INW, tn):
            chip, c0 = divmod(n0, INS)
            acc = _dot(x, w_ref[chip, :, c0:c0 + tn]) + bias_ref[:, n0:n0 + tn]
            if n0 >= 2 * D:
                o_ref[:, n0:n0 + tn] = acc.astype(BF16)
                continue
            scale = DK ** -0.5 if n0 < D else 1.0
            for hh in range(tn // DK):
                xh = acc[:, hh * DK:(hh + 1) * DK]
                o = (xh * c + pltpu.roll(xh, DK // 2, 1) * s) * scale
                o_ref[:, n0 + hh * DK:n0 + (hh + 1) * DK] = o.astype(BF16)

    return _pcall(
        body, name="inproj", grid=(T // tm,),
        in_specs=[pl.BlockSpec((tm, D), lambda i: (i, 0)),
                  _resident((NCHIP, D, INS), 0),
                  pl.BlockSpec((1, INW), lambda i: (0, 0)),
                  pl.BlockSpec((tm, DK), lambda i: (i, 0)),
                  pl.BlockSpec((tm, DK), lambda i: (i, 0))],
        out_specs=pl.BlockSpec((tm, INW), lambda i: (i, 0)),
        out_shape=jax.ShapeDtypeStruct((T, INW), BF16),
        compiler_params=_cp("parallel"))(xb, wall, b_in, cos_t, sin_t)


RET_CPS = 2


def _retention_fwd(proj, gn_g, dm_t, xi_t, zeta_t, cds):
    T = proj.shape[0]
    n = T // CH
    tr = RET_CPS * CH

    def body(q_ref, k_ref, v_ref, g_ref, gn_ref, dm_ref, xi_ref, zt_ref, r_ref, ri_ref, st_ref, state):
        @pl.when(pl.program_id(0) == 0)
        def _():
            state[...] = jnp.zeros_like(state)

        for h in range(H):
            rows = slice(h * DK, (h + 1) * DK)
            cols = slice(h * DV, (h + 1) * DV)
            s_prev = state[rows, :]
            for j in range(RET_CPS):
                t = slice(j * CH, (j + 1) * CH)
                q = q_ref[t, h * DK:(h + 1) * DK]
                k = k_ref[t, h * DK:(h + 1) * DK]
                v = v_ref[t, cols]
                s_b = s_prev.astype(BF16)
                st_ref[j, rows, :] = s_b
                sc = _dot_nt(q, k) * dm_ref[h]
                r = _dot(sc.astype(BF16), v) + _dot(q, s_b) * xi_ref[h]
                kz = k.astype(F32) * zt_ref[h]
                s_prev = cds[h] * s_prev + _dot(kz.T.astype(BF16), v)
                r_ref[t, cols] = r
                mu = jnp.mean(r, axis=-1, keepdims=True)
                xc = r - mu
                var = jnp.mean(xc * xc, axis=-1, keepdims=True)
                y = xc * lax.rsqrt(var + EPS) * gn_ref[:, cols]
                g = g_ref[t, cols].astype(F32)
                ri_ref[t, cols] = (g * _sig(g) * y).astype(BF16)
            state[rows, :] = s_prev

    full3 = lambda shp: pl.BlockSpec(shp, lambda c: (0, 0, 0))
    return _pcall(
        body, name="retention_fwd", grid=(n // RET_CPS,),
        in_specs=[pl.BlockSpec((tr, D), lambda c: (c, 0)),
                  pl.BlockSpec((tr, D), lambda c: (c, 1)),
                  pl.BlockSpec((tr, VW), lambda c: (c, 1)),
                  pl.BlockSpec((tr, VW), lambda c: (c, 2)),
                  pl.BlockSpec((1, VW), lambda c: (0, 0)),
                  full3((H, CH, CH)), full3((H, CH, DV)), full3((H, CH, DK))],
        out_specs=[pl.BlockSpec((tr, VW), lambda c: (c, 0)), pl.BlockSpec((tr, VW), lambda c: (c, 0)),
                   pl.BlockSpec((RET_CPS, H * DK, DV), lambda c: (c, 0, 0))],
        out_shape=[jax.ShapeDtypeStruct((T, VW), F32), jax.ShapeDtypeStruct((T, VW), BF16),
                   jax.ShapeDtypeStruct((n, H * DK, DV), BF16)],
        scratch_shapes=[pltpu.VMEM((H * DK, DV), F32)],
        compiler_params=_cp("arbitrary"))(proj, proj, proj, proj, gn_g, dm_t, xi_t, zeta_t)


CONV_TT = 256
CONV_SB = 64
CONV_CB = 256


SUB = 8
CONV_PAD = 8


def _glu(a_ref, b_ref, rows=slice(None)):
    a = a_ref[rows, :].astype(F32)
    sb = _sig(b_ref[rows, :].astype(F32))
    return a, sb


def _shift_copies(win, sh, rows):
    win[rows:rows + CONV_PAD, :] = jnp.zeros((CONV_PAD, D), F32)
    for b in range(1, SUB):
        sh[b - 1, :, :] = win[b:b + rows, :]


def _tap(win, sh, start, size, cs):
    a, b = divmod(start, SUB)
    src = win if b == 0 else sh.at[b - 1]
    return src[SUB * a:SUB * a + size, cs]


def _conv_fwd(proj, conv_k, conv_b, ln_g, ln_b):
    T = proj.shape[0]
    tt = min(T, CONV_TT)
    ca, cb = 6 * D // D, 7 * D // D

    def body(a_ref, b_ref, pa_ref, pb_ref, k_ref, cb_ref, g_ref, bb_ref, u1_ref, u3_ref, win, sh):
        i = pl.program_id(0)
        a, sb = _glu(a_ref, b_ref)
        win[HALO:tt + HALO, :] = a * sb
        pa, psb = _glu(pa_ref, pb_ref, slice(tt - HALO, tt))
        win[0:HALO, :] = jnp.where(i > 0, pa * psb, 0.0)
        _shift_copies(win, sh, tt + HALO)
        for c0 in range(0, D, CONV_CB):
            cs = slice(c0, c0 + CONV_CB)
            for r0 in range(0, tt, CONV_SB):
                acc = jnp.zeros((CONV_SB, CONV_CB), F32)
                for w in range(CONV_W):
                    st = r0 + HALO - (CONV_W - 1) + w
                    acc += _tap(win, sh, st, CONV_SB, cs) * k_ref[w:w + 1, cs]
                u1_ref[r0:r0 + CONV_SB, cs] = acc + cb_ref[:, cs]
        u2, _, _ = _ln_fwd(u1_ref[...], g_ref[...], bb_ref[...])
        u3_ref[...] = (u2 * _sig(u2)).astype(BF16)

    vec = pl.BlockSpec((1, D), lambda i: (0, 0))
    row = pl.BlockSpec((tt, D), lambda i: (i, 0))
    return _pcall(
        body, name="conv_fwd", grid=(T // tt,),
        in_specs=[pl.BlockSpec((tt, D), lambda i: (i, ca)), pl.BlockSpec((tt, D), lambda i: (i, cb)),
                  pl.BlockSpec((tt, D), lambda i: (jnp.maximum(i - 1, 0), ca)),
                  pl.BlockSpec((tt, D), lambda i: (jnp.maximum(i - 1, 0), cb)),
                  pl.BlockSpec((CONV_W, D), lambda i: (0, 0)), vec, vec, vec],
        out_specs=[row, row],
        out_shape=[jax.ShapeDtypeStruct((T, D), F32), jax.ShapeDtypeStruct((T, D), BF16)],
        scratch_shapes=[pltpu.VMEM((tt + HALO + CONV_PAD, D), F32), pltpu.VMEM((SUB - 1, tt + HALO, D), F32)],
        compiler_params=_cp("parallel"))(proj, proj, proj, proj, conv_k, conv_b, ln_g, ln_b)


def _merge(ret_in, u3, proj, wall, off_r, off_c):
    T = ret_in.shape[0]
    tm = min(T, 512)
    kr, kc = VW // NCHIP, D // NCHIP

    def body(ri_ref, u3_ref, gr_ref, gc_ref, wr_ref, wc_ref, ro_ref, co_ref, m_ref):
        ro = _dot_nt(ri_ref[:, 0:kr], wr_ref[0])
        co = _dot_nt(u3_ref[:, 0:kc], wc_ref[0])
        for s in range(1, NCHIP):
            ro += _dot_nt(ri_ref[:, s * kr:(s + 1) * kr], wr_ref[s])
            co += _dot_nt(u3_ref[:, s * kc:(s + 1) * kc], wc_ref[s])
        ro_ref[...] = ro.astype(BF16)
        co_ref[...] = co.astype(BF16)
        m = _sig(gr_ref[...].astype(F32)) * ro + _sig(gc_ref[...].astype(F32)) * co
        m_ref[...] = m.astype(BF16)

    row = pl.BlockSpec((tm, D), lambda i: (i, 0))
    return _pcall(
        body, name="merge", grid=(T // tm,),
        in_specs=[pl.BlockSpec((tm, VW), lambda i: (i, 0)), row,
                  pl.BlockSpec((tm, D), lambda i: (i, 8)), pl.BlockSpec((tm, D), lambda i: (i, 9)),
                  pl.BlockSpec((NCHIP, D, kr), lambda i: (0, 0, off_r // kr)),
                  pl.BlockSpec((NCHIP, D, kc), lambda i: (0, 0, off_c // kc))],
        out_specs=[row] * 3, out_shape=[jax.ShapeDtypeStruct((T, D), BF16)] * 3,
        compiler_params=_cp("parallel"))(ret_in, u3, proj, proj, wall, wall)


def _loss_ln_bwd(z, g, b, target, coef):
    T = z.shape[0]
    tm = min(T, 256)
    nt = T // tm

    def body(z_ref, g_ref, b_ref, t_ref, loss_ref, dzb_ref, dzt_ref, dz_ref, dg_ref, db_ref, lacc):
        i = pl.program_id(0)
        gam = g_ref[...]
        y, xh, rstd = _ln_fwd(z_ref[...], gam, b_ref[...])
        e = y - t_ref[...]
        part = _colsum(e * e)
        _acc_rows(lacc, i == 0, part)
        dy = e * (1.0 / D)
        dz = _ln_bwd(dy, xh, rstd, gam)
        dz_ref[...] = dz
        dzc = coef * dz
        dzb_ref[...] = dzc.astype(BF16)
        dzt_ref[...] = dzc.T.astype(BF16)
        _acc_rows(dg_ref, i == 0, _colsum(dy * xh))
        _acc_rows(db_ref, i == 0, _colsum(dy))

        @pl.when(i == nt - 1)
        def _():
            loss_ref[...] = (0.5 / D) * jnp.sum(lacc[...], axis=1, keepdims=True)

    row = pl.BlockSpec((tm, D), lambda i: (i, 0))
    vec = pl.BlockSpec((1, D), lambda i: (0, 0))
    return _pcall(
        body, name="loss_ln_bwd", grid=(nt,),
        in_specs=[row, vec, vec, row],
        out_specs=[pl.BlockSpec((1, 1), lambda i: (0, 0)), row, pl.BlockSpec((D, tm), lambda i: (0, i)),
                   row, vec, vec],
        out_shape=[jax.ShapeDtypeStruct((1, 1), F32), jax.ShapeDtypeStruct((T, D), BF16),
                   jax.ShapeDtypeStruct((D, T), BF16), jax.ShapeDtypeStruct((T, D), F32),
                   jax.ShapeDtypeStruct((1, D), F32), jax.ShapeDtypeStruct((1, D), F32)],
        scratch_shapes=[pltpu.VMEM((1, D), F32)],
        compiler_params=_cp("arbitrary"))(z, g, b, target)


def _ffn_bwd_h(dfb, wall, od, a, b, name):
    T = dfb.shape[0]
    tm = min(T, 512)

    def body(d_ref, w_ref, a_ref, b_ref, da_ref, db_ref):
        d = d_ref[...]
        for s in range(NCHIP):
            cols = slice(s * FSP, (s + 1) * FSP)
            dh = _dot(d, w_ref[s])
            a = a_ref[:, cols].astype(F32)
            sg = _sig(a)
            da_ref[:, cols] = (dh * b_ref[:, cols].astype(F32) * (sg * (1.0 + a * (1.0 - sg)))).astype(BF16)
            db_ref[:, cols] = (dh * a * sg).astype(BF16)

    ospec = pl.BlockSpec((tm, FP), lambda i: (i, 0))
    return _pcall(
        body, name=name, grid=(T // tm,),
        in_specs=[pl.BlockSpec((tm, D), lambda i: (i, 0)), _resident((NCHIP, D, FSP), od // FSP), ospec, ospec],
        out_specs=[ospec] * 2, out_shape=[jax.ShapeDtypeStruct((T, FP), BF16)] * 2,
        compiler_params=_cp("parallel"))(dfb, wall, a, b)


DX_SUB = 256


def _dx_bwd(lhs, offs, wall, dz_next, name, ln=None, colsum=False):
    T, K = lhs[0].shape
    ks = K // NCHIP
    nl = len(lhs)
    assert list(offs) == [l * ks for l in range(nl)]
    tm = min(T, 512)
    lhs_mode = {"pipeline_mode": pl.Buffered(1)} if K > FP else {}

    def body(*refs):
        l_refs = refs[:nl]
        w_ref = refs[nl]
        dzn_ref = refs[nl + 1]
        pos = nl + 2
        if ln is not None:
            z_ref, g_ref = refs[pos:pos + 2]
            pos += 2
        outs = refs[pos:]
        sums = list(outs[3:]) if ln is not None else list(outs[1:])

        @pl.when(pl.program_id(0) == 0)
        def _():
            for ref in sums:
                ref[...] = jnp.zeros_like(ref)

        for r0 in range(0, tm, DX_SUB):
            r = slice(r0, r0 + DX_SUB)
            acc = None
            for s in range(NCHIP):
                rows = slice(s * ks, (s + 1) * ks)
                for l in range(nl):
                    part = _dot_nt(l_refs[l][r, rows], w_ref[s, :, l * ks:(l + 1) * ks])
                    acc = part if acc is None else acc + part
                if colsum:
                    outs[-1][:, rows] += _colsum(l_refs[0][r, rows].astype(F32))
            dy = acc + ALPHA * dzn_ref[r, :]
            if ln is None:
                outs[0][r, :] = dy
            else:
                gam = g_ref[...]
                _, xh, rstd = _ln_fwd(z_ref[r, :], gam, 0.0)
                dz = _ln_bwd(dy, xh, rstd, gam)
                dzc = ln[2] * dz
                outs[0][r, :] = dzc.astype(BF16)
                outs[1][:, r] = dzc.T.astype(BF16)
                outs[2][r, :] = dz
                outs[3][...] += _colsum(dy * xh)
                outs[4][...] += _colsum(dy)

    row = pl.BlockSpec((tm, D), lambda i: (i, 0))
    vec = pl.BlockSpec((1, D), lambda i: (0, 0))
    in_specs = [pl.BlockSpec((tm, K), lambda i: (i, 0), **lhs_mode)] * nl + [_resident((NCHIP, D, nl * ks), 0), row]
    args = list(lhs) + [wall, dz_next]
    if ln is None:
        out_specs = [row]
        out_shape = [jax.ShapeDtypeStruct((T, D), F32)]
    else:
        in_specs += [row, vec]
        args += [ln[0], ln[1]]
        out_specs = [row, pl.BlockSpec((D, tm), lambda i: (0, i)), row, vec, vec]
        out_shape = [jax.ShapeDtypeStruct((T, D), BF16), jax.ShapeDtypeStruct((D, T), BF16),
                     jax.ShapeDtypeStruct((T, D), F32), jax.ShapeDtypeStruct((1, D), F32),
                     jax.ShapeDtypeStruct((1, D), F32)]
    if colsum:
        out_specs += [pl.BlockSpec((1, K), lambda i: (0, 0))]
        out_shape += [jax.ShapeDtypeStruct((1, K), F32)]
    return _pcall(
        body, name=name, grid=(T // tm,), in_specs=in_specs, out_specs=out_specs, out_shape=out_shape,
        compiler_params=_cp("arbitrary"))(*args)


def _wgrad(lhs_t, rhs, key, name, g_all=None):
    T, N = rhs.shape
    tn = next(c for c in (768, 512, 256) if (N // NCHIP) % c == 0 and LOC[key][1] % c == 0)
    nps = N // NCHIP // tn
    off = LOC[key][1]
    cols = GCOLS[LOC[key][0]]

    def body(*refs):
        l_ref, r_ref, o_ref, t_ref = refs[0], refs[1], refs[-2], refs[-1]
        o_ref[...] = _dot(l_ref[...], r_ref[...]).astype(BF16)
        t_ref[...] = jnp.zeros_like(t_ref)

    in_specs = [_resident((D, T), 0), pl.BlockSpec((T, tn), lambda j: (0, j))]
    args = [lhs_t, rhs]
    aliases = {}
    if g_all is not None:
        in_specs.append(pl.BlockSpec(memory_space=pl.ANY))
        args.append(g_all)
        aliases = {2: 0}
    return _pcall(
        body, name=name, grid=(N // tn,), in_specs=in_specs,
        out_specs=[pl.BlockSpec((None, D, tn), lambda j: (j // nps, 0, off // tn + j % nps)),
                   pl.BlockSpec((8, 128), lambda j: (0, 0))],
        out_shape=[jax.ShapeDtypeStruct((NCHIP, D, cols), BF16), jax.ShapeDtypeStruct((8, 128), F32)],
        input_output_aliases=aliases,
        compiler_params=_cp("arbitrary"))(*args)[0]


def _merge_bwd(dmb, wall, off, proj, ro, co):
    T = dmb.shape[0]
    tm = min(T, 512)
    ks = D // NCHIP

    def body(d_ref, w_ref, gr_ref, gc_ref, ro_ref, co_ref, dro_ref, drot_ref, dco_ref, dcot_ref, dp_ref):
        d = d_ref[...]
        dmg = jnp.concatenate([_dot(d, w_ref[s]) for s in range(NCHIP)], axis=1)
        sr = _sig(gr_ref[...].astype(F32))
        sc = _sig(gc_ref[...].astype(F32))
        dro = dmg * sr
        dco = dmg * sc
        dro_ref[...] = dro.astype(BF16)
        drot_ref[...] = dro.T.astype(BF16)
        dco_ref[...] = dco.astype(BF16)
        dcot_ref[...] = dco.T.astype(BF16)
        dp_ref[:, 0:D] = (dmg * ro_ref[...].astype(F32) * sr * (1.0 - sr)).astype(BF16)
        dp_ref[:, D:2 * D] = (dmg * co_ref[...].astype(F32) * sc * (1.0 - sc)).astype(BF16)

    row = pl.BlockSpec((tm, D), lambda i: (i, 0))
    col = pl.BlockSpec((D, tm), lambda i: (0, i))
    return _pcall(
        body, name="merge_bwd", grid=(T // tm,),
        in_specs=[row, pl.BlockSpec((NCHIP, D, ks), lambda i: (0, 0, off // ks)),
                  pl.BlockSpec((tm, D), lambda i: (i, 8)), pl.BlockSpec((tm, D), lambda i: (i, 9)), row, row],
        out_specs=[row, col, row, col, pl.BlockSpec((tm, 2 * D), lambda i: (i, 4))],
        out_shape=[jax.ShapeDtypeStruct((T, D), BF16), jax.ShapeDtypeStruct((D, T), BF16),
                   jax.ShapeDtypeStruct((T, D), BF16), jax.ShapeDtypeStruct((D, T), BF16),
                   jax.ShapeDtypeStruct((T, INW), BF16)],
        compiler_params=_cp("parallel"))(dmb, wall, proj, proj, ro, co)


def _reto_bwd(dro, wall, off, r, proj, gn_g, dproj):
    T = dro.shape[0]
    tm = min(T, 512)
    hps = H // NCHIP

    def body(d_ref, w_ref, r_ref, g_ref, gn_ref, _, dr_ref, dgn_ref, dp_ref):
        i = pl.program_id(1)
        dri = _dot(d_ref[...], w_ref[...])
        rr = r_ref[...]
        mu = jnp.mean(rr, axis=-1, keepdims=True)
        xc = rr - mu
        var = jnp.mean(xc * xc, axis=-1, keepdims=True)
        rstd = lax.rsqrt(var + EPS)
        rn = xc * rstd
        gn = gn_ref[...]
        g = g_ref[...].astype(F32)
        sg = _sig(g)
        dy = dri * (g * sg)
        dp_ref[...] = (dri * (rn * gn) * (sg * (1.0 + g * (1.0 - sg)))).astype(BF16)
        _acc_rows(dgn_ref, i == 0, _colsum(dy * rn))
        dr_ref[...] = _ln_bwd(dy, rn, rstd, gn).astype(BF16)

    return _pcall(
        body, name="reto_bwd", grid=(H, T // tm),
        in_specs=[pl.BlockSpec((tm, D), lambda j, i: (i, 0)),
                  pl.BlockSpec((None, D, DV), lambda j, i: (j // hps, 0, off // DV + j % hps)),
                  pl.BlockSpec((tm, DV), lambda j, i: (i, j)),
                  pl.BlockSpec((tm, DV), lambda j, i: (i, 2 * VW // DV + j)),
                  pl.BlockSpec((1, DV), lambda j, i: (0, j)),
                  pl.BlockSpec(memory_space=pl.ANY)],
        out_specs=[pl.BlockSpec((tm, DV), lambda j, i: (i, j)), pl.BlockSpec((1, DV), lambda j, i: (0, j)),
                   pl.BlockSpec((tm, DV), lambda j, i: (i, 2 * VW // DV + j))],
        out_shape=[jax.ShapeDtypeStruct((T, VW), BF16), jax.ShapeDtypeStruct((1, VW), F32),
                   jax.ShapeDtypeStruct((T, INW), BF16)],
        input_output_aliases={5: 2},
        compiler_params=_cp("arbitrary", "arbitrary"))(dro, wall, r, proj, gn_g, dproj)


def _retention_bwd(proj, dr, states, cos_t, sin_t, dm_t, xi_t, zeta_t, cds, dproj):
    T = proj.shape[0]
    n = T // CH // RET_CPS
    tr = RET_CPS * CH
    scale = DK ** -0.5

    def body(q_ref, k_ref, v_ref, dr_ref, st_ref, cos_ref, sin_ref, dm_ref, xi_ref, zt_ref, _, dp_ref, ds):
        @pl.when(pl.program_id(0) == 0)
        def _():
            ds[...] = jnp.zeros_like(ds)

        def unrope(d, t):
            return d * cos_ref[t, :] + pltpu.roll(d * sin_ref[t, :], DK // 2, 1)

        for h in range(H):
            rows = slice(h * DK, (h + 1) * DK)
            dm = dm_ref[h]
            zt = zt_ref[h]
            ds_prev = ds[rows, :]
            for j in reversed(range(RET_CPS)):
                t = slice(j * CH, (j + 1) * CH)
                q = q_ref[t, h * DK:(h + 1) * DK]
                k = k_ref[t, h * DK:(h + 1) * DK]
                v = v_ref[t, h * DV:(h + 1) * DV]
                d_r = dr_ref[t, h * DV:(h + 1) * DV]
                s_b = st_ref[j, rows, :]
                sc = _dot_nt(q, k) * dm
                dsc = _dot_nt(d_r, v) * dm
                drx = (d_r.astype(F32) * xi_ref[h]).astype(BF16)
                ds_b = ds_prev.astype(BF16)
                kz = (k.astype(F32) * zt).astype(BF16)
                dq = _dot(dsc.astype(BF16), k) + _dot_nt(drx, s_b)
                dk = _dot(dsc.T.astype(BF16), q) + _dot_nt(v, ds_b) * zt
                dv = _dot(sc.T.astype(BF16), d_r) + _dot(kz, ds_b)
                ds_prev = cds[h] * ds_prev + _dot(q.astype(F32).T.astype(BF16), drx)
                dp_ref[t, h * DK:(h + 1) * DK] = unrope(dq * scale, t).astype(BF16)
                dp_ref[t, D + h * DK:D + (h + 1) * DK] = unrope(dk, t).astype(BF16)
                dp_ref[t, 2 * D + h * DV:2 * D + (h + 1) * DV] = dv.astype(BF16)
            ds[rows, :] = ds_prev

    rv = lambda c: n - 1 - c
    full3 = lambda shp: pl.BlockSpec(shp, lambda c: (0, 0, 0))
    return _pcall(
        body, name="retention_bwd", grid=(n,),
        in_specs=[pl.BlockSpec((tr, D), lambda c: (rv(c), 0)),
                  pl.BlockSpec((tr, D), lambda c: (rv(c), 1)),
                  pl.BlockSpec((tr, VW), lambda c: (rv(c), 1)),
                  pl.BlockSpec((tr, VW), lambda c: (rv(c), 0)),
                  pl.BlockSpec((RET_CPS, H * DK, DV), lambda c: (rv(c), 0, 0)),
                  pl.BlockSpec((tr, DK), lambda c: (rv(c), 0)),
                  pl.BlockSpec((tr, DK), lambda c: (rv(c), 0)),
                  full3((H, CH, CH)), full3((H, CH, DV)), full3((H, CH, DK)),
                  pl.BlockSpec(memory_space=pl.ANY)],
        out_specs=pl.BlockSpec((tr, 2 * D + VW), lambda c: (rv(c), 0)),
        out_shape=jax.ShapeDtypeStruct((T, INW), BF16),
        input_output_aliases={10: 0},
        scratch_shapes=[pltpu.VMEM((H * DK, DV), F32)],
        compiler_params=_cp("arbitrary"))(proj, proj, proj, dr, states, cos_t, sin_t, dm_t, xi_t, zeta_t, dproj)


def _convo_bwd(dco, wall, off, u1, ln_g, ln_b):
    T = dco.shape[0]
    tm = min(T, 512)
    ks = D // NCHIP

    def body(d_ref, w_ref, u1_ref, g_ref, b_ref, du1_ref, dg_ref, db_ref, dcb_ref):
        i = pl.program_id(0)
        d = d_ref[...]
        du3 = jnp.concatenate([_dot(d, w_ref[s]) for s in range(NCHIP)], axis=1)
        gam = g_ref[...]
        u2, xh, rstd = _ln_fwd(u1_ref[...], gam, b_ref[...])
        sg = _sig(u2)
        du2 = du3 * (sg * (1.0 + u2 * (1.0 - sg)))
        du1 = _ln_bwd(du2, xh, rstd, gam)
        du1_ref[...] = du1
        _acc_rows(dg_ref, i == 0, _colsum(du2 * xh))
        _acc_rows(db_ref, i == 0, _colsum(du2))
        _acc_rows(dcb_ref, i == 0, _colsum(du1))

    row = pl.BlockSpec((tm, D), lambda i: (i, 0))
    vec = pl.BlockSpec((1, D), lambda i: (0, 0))
    return _pcall(
        body, name="convo_bwd", grid=(T // tm,),
        in_specs=[row, pl.BlockSpec((NCHIP, D, ks), lambda i: (0, 0, off // ks)), row, vec, vec],
        out_specs=[row, vec, vec, vec],
        out_shape=[jax.ShapeDtypeStruct((T, D), F32)] + [jax.ShapeDtypeStruct((1, D), F32)] * 3,
        compiler_params=_cp("arbitrary"))(dco, wall, u1, ln_g, ln_b)


def _conv_bwd(du1, proj, conv_k, dproj):
    T = du1.shape[0]
    tt = min(T, CONV_TT)
    nt = T // tt
    ca, cb = 6, 7

    def body(d_ref, dn_ref, a_ref, b_ref, pa_ref, pb_ref, k_ref, _, dp_ref, dk_ref, win_u, win_d, sh_u, sh_d):
        i = pl.program_id(0)
        a, sb = _glu(a_ref, b_ref)
        win_u[HALO:tt + HALO, :] = a * sb
        pa, psb = _glu(pa_ref, pb_ref, slice(tt - HALO, tt))
        win_u[0:HALO, :] = jnp.where(i > 0, pa * psb, 0.0)
        win_d[0:tt, :] = d_ref[...]
        win_d[tt:tt + HALO, :] = jnp.where(i < nt - 1, dn_ref[0:HALO, :], 0.0)
        _shift_copies(win_u, sh_u, tt + HALO)
        _shift_copies(win_d, sh_d, tt + HALO)

        @pl.when(i == 0)
        def _():
            dk_ref[...] = jnp.zeros_like(dk_ref)

        for c0 in range(0, D, CONV_CB):
            cs = slice(c0, c0 + CONV_CB)
            for r0 in range(0, tt, CONV_SB):
                acc = jnp.zeros((CONV_SB, CONV_CB), F32)
                for w in range(CONV_W):
                    st = r0 + (CONV_W - 1) - w
                    acc += _tap(win_d, sh_d, st, CONV_SB, cs) * k_ref[w:w + 1, cs]
                aa = a_ref[r0:r0 + CONV_SB, cs].astype(F32)
                ss = _sig(b_ref[r0:r0 + CONV_SB, cs].astype(F32))
                dp_ref[r0:r0 + CONV_SB, cs] = (acc * ss).astype(BF16)
                dp_ref[r0:r0 + CONV_SB, c0 + D:c0 + D + CONV_CB] = (acc * aa * ss * (1.0 - ss)).astype(BF16)
            for w in range(CONV_W):
                acc = jnp.zeros((CONV_SB, CONV_CB), F32)
                for r0 in range(0, tt, CONV_SB):
                    st = r0 + HALO - (CONV_W - 1) + w
                    acc += win_d[r0:r0 + CONV_SB, cs] * _tap(win_u, sh_u, st, CONV_SB, cs)
                dk_ref[w:w + 1, cs] += _colsum(acc)

    blk = lambda f, c: pl.BlockSpec((tt, D), lambda i: (f(i), c))
    cur = lambda i: i
    prv = lambda i: jnp.maximum(i - 1, 0)
    nxt = lambda i: jnp.minimum(i + 1, nt - 1)
    return _pcall(
        body, name="conv_bwd", grid=(nt,),
        in_specs=[blk(cur, 0), blk(nxt, 0), blk(cur, ca), blk(cur, cb), blk(prv, ca), blk(prv, cb),
                  pl.BlockSpec((CONV_W, D), lambda i: (0, 0)), pl.BlockSpec(memory_space=pl.ANY)],
        out_specs=[pl.BlockSpec((tt, 2 * D), lambda i: (i, 3)), pl.BlockSpec((HALO, D), lambda i: (0, 0))],
        out_shape=[jax.ShapeDtypeStruct((T, INW), BF16), jax.ShapeDtypeStruct((HALO, D), F32)],
        input_output_aliases={7: 0},
        scratch_shapes=[pltpu.VMEM((tt + HALO + CONV_PAD, D), F32), pltpu.VMEM((tt + HALO + CONV_PAD, D), F32),
                        pltpu.VMEM((SUB - 1, tt + HALO, D), F32), pltpu.VMEM((SUB - 1, tt + HALO, D), F32)],
        compiler_params=_cp("arbitrary"))(du1, du1, proj, proj, proj, proj, conv_k, dproj)


def _local_step(x, target, wts, sp, pos_c, pos_sc, adam):
    T = x.shape[0]
    cos_t, sin_t = _rope_tables(T)
    dm_t, xi_t, zeta_t, cds = _decay_tables()
    wa = lambda key: wts[LOC[key][0]]
    wo = lambda key: LOC[key][1]
    _ORDER.active, _ORDER.token = True, None

    xb, xt = _cast_t(x)
    a1, b1, h1 = _ffn_up(xb, wa("g1"), wo("g1"), wo("u1"), "ffn1_up")
    z1, x1, x1b, x1t = _proj_ln(h1, wa("d1"), wo("d1"), x, sp["ln1_g"], sp["ln1_b"], 0.5, "ffn1_down_ln")
    proj = _inproj(x1b, wa("w_in"), wo("w_in"), sp["b_in"], cos_t, sin_t)
    r, ret_in, states = _retention_fwd(proj, sp["ret_gn_g"], dm_t, xi_t, zeta_t, cds)
    u1, u3 = _conv_fwd(proj, sp["conv_k"], sp["conv_b"], sp["conv_ln_g"], sp["conv_ln_b"])
    ro, co, merged = _merge(ret_in, u3, proj, wa("w_ret_o"), wo("w_ret_o"), wo("w_conv_o"))
    z2, x2, x2b, x2t = _proj_ln(merged, wa("w_out"), wo("w_out"), x1, sp["ln2_g"], sp["ln2_b"], 1.0, "out_proj_ln")
    a2, b2, h2 = _ffn_up(x2b, wa("g2"), wo("g2"), wo("u2"), "ffn2_up")
    (z3,) = _proj_ln(h2, wa("d2"), wo("d2"), x2, sp["ln3_g"], sp["ln3_b"], 0.5, "ffn2_down", want_b=False)

    sg = {}
    rs = {}
    loss, df2b, df2t, dz3, sg["ln3_g"], sg["ln3_b"] = _loss_ln_bwd(z3, sp["ln3_g"], sp["ln3_b"], target, 0.5)
    da2, db2 = _ffn_bwd_h(df2b, wa("d2"), wo("d2"), a2, b2, "ffn2_bwd_h")
    g4 = _wgrad(df2t, h2, "d2", "wgrad_d2")
    g4 = _wgrad(x2t, da2, "g2", "wgrad_g2", g4)
    g4 = _wgrad(x2t, db2, "u2", "wgrad_u2", g4)
    rs[4] = _ReduceScatter(g4, 4, pos_c, pos_sc)
    dmb, dmt, dz2, sg["ln2_g"], sg["ln2_b"] = _dx_bwd(
        [da2, db2], [wo("g2"), wo("u2")], wa("g2"), dz3, "ffn2_dx_ln", ln=(z2, sp["ln2_g"], 1.0))
    rs[4].phase2()
    g3 = _wgrad(dmt, merged, "w_out", "wgrad_out")
    dro, drot, dco, dcot, dproj = _merge_bwd(dmb, wa("w_out"), wo("w_out"), proj, ro, co)
    g3 = _wgrad(drot, ret_in, "w_ret_o", "wgrad_ret_o", g3)
    g3 = _wgrad(dcot, u3, "w_conv_o", "wgrad_conv_o", g3)
    rs[3] = _ReduceScatter(g3, 3, pos_c, pos_sc)
    dr, sg["ret_gn_g"], dproj = _reto_bwd(dro, wa("w_ret_o"), wo("w_ret_o"), r, proj, sp["ret_gn_g"], dproj)
    rs[4].phase3()
    rs[3].phase2()
    dproj = _retention_bwd(proj, dr, states, cos_t, sin_t, dm_t, xi_t, zeta_t, cds, dproj)
    du1, sg["conv_ln_g"], sg["conv_ln_b"], sg["conv_b"] = _convo_bwd(
        dco, wa("w_conv_o"), wo("w_conv_o"), u1, sp["conv_ln_g"], sp["conv_ln_b"])
    dproj, dck = _conv_bwd(du1, proj, sp["conv_k"], dproj)
    sg["conv_k"] = dck[:CONV_W]
    adam(4, rs[4].result())
    rs[3].phase3()
    g2 = _wgrad(x1t, dproj, "w_in", "wgrad_in")
    rs[2] = _ReduceScatter(g2, 2, pos_c, pos_sc)
    df1b, df1t, dz1, sg["ln1_g"], sg["ln1_b"], sg["b_in"] = _dx_bwd(
        [dproj], [wo("w_in")], wa("w_in"), dz2, "mixer_dx_ln", ln=(z1, sp["ln1_g"], 0.5), colsum=True)
    adam(3, rs[3].result())
    rs[2].phase2()
    shapes = {n: sg[n].shape for n in SMALL + ["conv_k"]}
    small_parts = _all_gather_small(_pack_small(sg, loss, SMALL_ROWS), "gather_small")
    da1, db1 = _ffn_bwd_h(df1b, wa("d1"), wo("d1"), a1, b1, "ffn1_bwd_h")
    small_sum = _sum_devices(small_parts, "sum_small")
    g1 = _wgrad(df1t, h1, "d1", "wgrad_d1")
    rs[1] = _ReduceScatter(g1, 1, pos_c, pos_sc)
    g0 = _wgrad(xt, da1, "g1", "wgrad_g1")
    g0 = _wgrad(xt, db1, "u1", "wgrad_u1", g0)
    rs[0] = _ReduceScatter(g0, 0, pos_c, pos_sc)
    rs[2].phase3()
    rs[1].phase2()
    rs[0].phase2()
    (grad_x,) = _dx_bwd([da1, db1], [wo("g1"), wo("u1")], wa("g1"), dz1, "ffn1_dx")
    adam(2, rs[2].result())
    rs[1].phase3()
    rs[0].phase3()
    adam(1, rs[1].result())
    adam(0, rs[0].result())
    _ORDER.active = False
    return grad_x, small_sum, shapes


MESH = pl.DeviceIdType.MESH
ANY = pl.BlockSpec(memory_space=pl.ANY)
HALF = D // 2


def _place():
    x, y, c = lax.axis_index("x"), lax.axis_index("y"), lax.axis_index("c")
    chips = [(1 - x, y), (x, 1 - y), (1 - x, 1 - y)]
    return x, y, c, chips


GATHER_ID = 1


def _gather_weights(wloc, name):
    w_ref = jax.new_ref(wloc, memory_space=pltpu.MemorySpace.HBM)
    o_ref = jax.empty_ref(jax.ShapeDtypeStruct((NCHIP, 2, HALF, wloc.shape[-1]), BF16),
                          memory_space=pltpu.MemorySpace.HBM)
    dma = pltpu.SemaphoreType.DMA

    @pl.kernel(mesh=plsc.ScalarSubcoreMesh(axis_name="sc", num_cores=1), name=name,
               scratch_types=(dma(()), dma((2,)), dma((2,)), dma((3,)), dma((3,)), dma(()), dma(())),
               compiler_params=pltpu.CompilerParams(collective_id=GATHER_ID))
    def launch(lsem, s1, r1, s2, r2, s3, r3):
        x, y, c, _ = _place()
        me = 2 * x + y
        sib = (x, y, 1 - c)
        x_nbr, y_nbr = (1 - x, y, c), (x, 1 - y, c)
        x_chip, y_chip, d_chip = 2 * (1 - x) + y, 2 * x + (1 - y), 2 * (1 - x) + (1 - y)
        _handshake([sib, x_nbr, y_nbr])
        mine = pltpu.make_async_copy(w_ref, o_ref.at[me], lsem)
        mine.start()

        def rc(src, dst, ss, rs, dev):
            return pltpu.make_async_remote_copy(src_ref=src, dst_ref=dst, send_sem=ss, recv_sem=rs,
                                                device_id=dev, device_id_type=MESH)

        first = [rc(w_ref.at[c], o_ref.at[me, c], s1.at[0], r1.at[0], x_nbr),
                 rc(w_ref.at[c], o_ref.at[me, c], s1.at[1], r1.at[1], y_nbr)]
        for cp in first:
            cp.start()
        on_chip = c * x_chip + (1 - c) * y_chip
        other_chip = c * y_chip + (1 - c) * x_chip
        on_to = (c * x + (1 - c) * (1 - x), c * (1 - y) + (1 - c) * y, c)
        slot = o_ref.at[on_chip, c]
        rc(slot, slot, s1.at[1 - c], r1.at[1 - c], sib).wait_recv()
        onward = rc(slot, slot, s3, r3, on_to)
        onward.start()
        passed = [rc(slot, slot, s2.at[0], r2.at[0], sib)]
        passed[0].start()
        slot = o_ref.at[other_chip, c]
        rc(slot, slot, s1.at[c], r1.at[c], sib).wait_recv()
        passed.append(rc(slot, slot, s2.at[1], r2.at[1], sib))
        passed[1].start()
        slot = o_ref.at[d_chip, c]
        rc(slot, slot, s3, r3, sib).wait_recv()
        passed.append(rc(slot, slot, s2.at[2], r2.at[2], sib))
        passed[2].start()
        for j, chip in enumerate([other_chip, on_chip, d_chip]):
            slot = o_ref.at[chip, 1 - c]
            rc(slot, slot, s2.at[j], r2.at[j], sib).wait_recv()
        for cp in first + [onward] + passed:
            cp.wait_send()
        mine.wait()

    launch()
    return o_ref[...]


PAIR_ID = 2
CHIP_ID = 3
HBM = pltpu.MemorySpace.HBM


def _sequencer(name, collective_id, n_sems):
    dma = pltpu.SemaphoreType.DMA
    return pl.kernel(mesh=plsc.ScalarSubcoreMesh(axis_name="sc", num_cores=1), name=name,
                     scratch_types=(dma((n_sems,)), dma((n_sems,))),
                     compiler_params=pltpu.CompilerParams(collective_id=collective_id))


def _handshake(peers):
    barrier = pltpu.get_barrier_semaphore()
    for peer in peers:
        pl.semaphore_signal(barrier, inc=1, device_id=peer, device_id_type=MESH)
    pl.semaphore_wait(barrier, len(peers))


def _pair_exchange(g5, name):
    _, _, hr, cols = g5.shape
    g_ref = jax.new_ref(g5, memory_space=HBM)
    o_ref = jax.empty_ref(jax.ShapeDtypeStruct((NCHIP, hr, cols), g5.dtype), memory_space=HBM)

    @_sequencer(name, PAIR_ID, NCHIP)
    def launch(ss, rs):
        x, y, c, _ = _place()
        sib = (x, y, 1 - c)
        _handshake([sib])
        cps = [pltpu.make_async_remote_copy(src_ref=g_ref.at[j, 1 - c], dst_ref=o_ref.at[j], send_sem=ss.at[j],
                                            recv_sem=rs.at[j], device_id=sib, device_id_type=MESH)
               for j in range(NCHIP)]
        for cp in cps:
            cp.start()
        for cp in cps:
            cp.wait()

    launch()
    return o_ref[...]


RS_TR = 128


def _pair_sum(pos, g5, got, name):
    _, _, hr, cols = g5.shape

    def body(pos_ref, g_ref, r_ref, o_ref):
        o_ref[...] = (g_ref[...].astype(F32) + r_ref[...].astype(F32)).astype(BF16)

    return _pcall(
        body, name=name, scalar_prefetch=1, grid=(NCHIP, hr // RS_TR),
        in_specs=[pl.BlockSpec((None, None, RS_TR, cols), lambda j, i, p: (j, p[0], i, 0)),
                  pl.BlockSpec((None, RS_TR, cols), lambda j, i, p: (j, i, 0))],
        out_specs=pl.BlockSpec((None, RS_TR, cols), lambda j, i, p: (j, i, 0)),
        out_shape=jax.ShapeDtypeStruct((NCHIP, hr, cols), BF16),
        compiler_params=_cp("parallel", "parallel"))(pos, g5, got)


def _chip_exchange(pb, name):
    _, hr, cols = pb.shape
    p_ref = jax.new_ref(pb, memory_space=HBM)
    o_ref = jax.empty_ref(jax.ShapeDtypeStruct((3, hr, cols), BF16), memory_space=HBM)

    @_sequencer(name, CHIP_ID, 3)
    def launch(ss, rs):
        x, y, c, chips = _place()
        _handshake([(px, py, c) for px, py in chips])
        cps = [pltpu.make_async_remote_copy(src_ref=p_ref.at[2 * px + py], dst_ref=o_ref.at[j], send_sem=ss.at[j],
                                            recv_sem=rs.at[j], device_id=(px, py, c), device_id_type=MESH)
               for j, (px, py) in enumerate(chips)]
        for cp in cps:
            cp.start()
        for cp in cps:
            cp.wait()

    launch()
    return o_ref[...]


def _chip_sum(pos, g5, got, peers, name):
    _, _, hr, cols = g5.shape

    def body(pos_ref, g_ref, r_ref, p_ref, o_ref, t_ref):
        acc = g_ref[...].astype(F32) + r_ref[...].astype(F32)
        for j in range(3):
            acc += p_ref[j].astype(F32)
        o_ref[...] = acc
        t_ref[...] = jnp.zeros_like(t_ref)

    return _pcall(
        body, name=name, scalar_prefetch=1, grid=(hr // RS_TR,),
        in_specs=[pl.BlockSpec((None, None, RS_TR, cols), lambda i, p: (p[0], p[1], i, 0)),
                  pl.BlockSpec((None, RS_TR, cols), lambda i, p: (p[0], i, 0)),
                  pl.BlockSpec((3, RS_TR, cols), lambda i, p: (0, i, 0))],
        out_specs=[pl.BlockSpec((None, RS_TR, cols), lambda i, p: (p[1], i, 0)),
                   pl.BlockSpec((8, 128), lambda i, p: (0, 0))],
        out_shape=[jax.ShapeDtypeStruct((2, hr, cols), F32), jax.ShapeDtypeStruct((8, 128), F32)],
        compiler_params=_cp("arbitrary"))(pos, g5, got, peers)


def _pair_share(gsum, name):
    g_ref = jax.new_ref(gsum, memory_space=HBM)

    @_sequencer(name, PAIR_ID, 1)
    def launch(ss, rs):
        x, y, c, _ = _place()
        sib = (x, y, 1 - c)
        _handshake([sib])
        cp = pltpu.make_async_remote_copy(src_ref=g_ref.at[c], dst_ref=g_ref.at[c], send_sem=ss.at[0],
                                          recv_sem=rs.at[0], device_id=sib, device_id_type=MESH)
        cp.start()
        cp.wait_send()
        pltpu.make_async_remote_copy(src_ref=g_ref.at[1 - c], dst_ref=g_ref.at[1 - c], send_sem=ss.at[0],
                                     recv_sem=rs.at[0], device_id=sib, device_id_type=MESH).wait_recv()

    launch()
    return g_ref[...]


class _ReduceScatter:
    def __init__(self, g_arr, gi, pos_c, pos_sc):
        _, rows, cols = g_arr.shape
        self.g5 = g_arr.reshape(NCHIP, 2, rows // 2, cols)
        self.gi, self.pos_c, self.pos_sc = gi, pos_c, pos_sc
        self.got = _pair_exchange(self.g5, f"pair_exchange_{gi}")

    def phase2(self):
        pb = _pair_sum(self.pos_c, self.g5, self.got, f"pair_sum_{self.gi}")
        self.peers = _chip_exchange(pb, f"chip_exchange_{self.gi}")

    def phase3(self):
        gsum, _ = _chip_sum(self.pos_sc, self.g5, self.got, self.peers, f"chip_sum_{self.gi}")
        self.full = _pair_share(gsum, f"pair_share_{self.gi}")

    def result(self):
        _, hr, cols = self.full.shape
        return self.full.reshape(2 * hr, cols)


SMALL_ROWS = 56


ALL_ID = 4


def _all_gather_small(vec, name):
    v_ref = jax.new_ref(vec, memory_space=HBM)
    o_ref = jax.empty_ref(jax.ShapeDtypeStruct((8, SMALL_ROWS, D), F32), memory_space=HBM)

    @_sequencer(name, ALL_ID, 8)
    def launch(ss, rs):
        x, y, c, _ = _place()
        me = 4 * x + 2 * y + c
        flip = lambda v, bit: 1 - v if bit else v
        peers = [(flip(x, m >> 2), flip(y, (m >> 1) & 1), flip(c, m & 1)) for m in range(1, 8)]
        _handshake(peers)
        mine = pltpu.make_async_copy(v_ref, o_ref.at[me], ss.at[7])
        mine.start()
        cps = [pltpu.make_async_remote_copy(src_ref=v_ref, dst_ref=o_ref.at[me], send_sem=ss.at[k],
                                            recv_sem=rs.at[k], device_id=peer, device_id_type=MESH)
               for k, peer in enumerate(peers)]
        for cp in cps:
            cp.start()
        for cp in cps:
            cp.wait()
        mine.wait()

    launch()
    return o_ref[...]


def _sum_devices(parts, name):
    def body(p_ref, o_ref):
        acc = p_ref[0]
        for d in range(1, 8):
            acc += p_ref[d]
        o_ref[...] = acc

    return _pcall(
        body, name=name, grid=(SMALL_ROWS // 8,),
        in_specs=[pl.BlockSpec((8, 8, D), lambda i: (0, i, 0))],
        out_specs=pl.BlockSpec((8, D), lambda i: (i, 0)),
        out_shape=jax.ShapeDtypeStruct((SMALL_ROWS, D), F32),
        compiler_params=_cp("parallel"))(parts)


def _adamw_math(w, g, m, v):
    m2 = ADAM_B1 * m + (1.0 - ADAM_B1) * g
    v2 = ADAM_B2 * v + (1.0 - ADAM_B2) * (g * g)
    m_hat = m2 / (1.0 - ADAM_B1 ** ADAM_STEP)
    v_hat = v2 / (1.0 - ADAM_B2 ** ADAM_STEP)
    delta = -ADAM_LR * (m_hat / (jnp.sqrt(v_hat) + ADAM_EPS) + ADAM_WD * w)
    return delta, m2, v2


def _adamw(w, g, m, v, name, g_block=None):
    R, C = w.shape
    tr = R
    gw_hint = C if g_block is None else g_block[0]
    for cand in (512, 352, 256, 176, 128, 64, 32, 16, 8):
        if R % cand == 0 and cand * max(C, gw_hint) * 4 <= (2 << 20):
            tr = cand
            break
    gw, gi = (C, 0) if g_block is None else g_block

    def body(w_ref, g_ref, m_ref, v_ref, go_ref, d_ref, mo_ref, vo_ref):
        g = g_ref[:, 0:C]
        d, m2, v2 = _adamw_math(w_ref[...], g, m_ref[...], v_ref[...])
        go_ref[...] = g
        d_ref[...] = d
        mo_ref[...] = m2
        vo_ref[...] = v2

    spec = pl.BlockSpec((tr, C), lambda i: (i, 0))
    return _pcall(
        body, name=name, grid=(R // tr,),
        in_specs=[spec, pl.BlockSpec((tr, gw), lambda i: (i, gi)), spec, spec],
        out_specs=[spec] * 4, out_shape=[jax.ShapeDtypeStruct((R, C), F32)] * 4,
        compiler_params=_cp("parallel"))(w, g, m, v)


BIG = ["ffn1_w_gate", "ffn1_w_up", "ffn1_w_down", "w_in", "w_ret_o", "w_conv_o", "w_out",
       "ffn2_w_gate", "ffn2_w_up", "ffn2_w_down"]
SLAB = {"ffn1_w_gate": "g1", "ffn1_w_up": "u1", "ffn1_w_down": "d1", "w_in": "w_in", "w_ret_o": "w_ret_o",
        "w_conv_o": "w_conv_o", "w_out": "w_out", "ffn2_w_gate": "g2", "ffn2_w_up": "u2", "ffn2_w_down": "d2"}
TRANSPOSED = {"ffn1_w_down", "ffn2_w_down", "w_ret_o", "w_conv_o", "w_out"}
MINOR_ROWS = {"ffn1_w_gate", "ffn1_w_up", "ffn2_w_gate", "ffn2_w_up"}
SMALL = ["ln1_g", "ln1_b", "ln2_g", "ln2_b", "ln3_g", "ln3_b", "conv_ln_g", "conv_ln_b", "conv_b",
         "ret_gn_g", "b_in"]
ORDER = ["ffn1_w_gate", "ffn1_w_up", "ffn1_w_down", "ln1_g", "ln1_b", "w_in", "b_in", "ret_gn_g", "conv_k",
         "conv_b", "conv_ln_g", "conv_ln_b", "w_ret_o", "w_conv_o", "w_out", "ln2_g", "ln2_b",
         "ffn2_w_gate", "ffn2_w_up", "ffn2_w_down", "ln3_g", "ln3_b"]


def _slab_width(name):
    return WIDTH[SLAB[name]]


def _pack_group(weights, keys):
    by_key = {SLAB[n]: n for n in BIG}
    parts = []
    for key in keys:
        w = weights[by_key[key]]
        w = w.T if by_key[key] in TRANSPOSED else w
        parts.append(jnp.pad(w, ((0, 0), (0, WIDTH[key] - w.shape[1]))))
    return jnp.concatenate(parts, axis=1).astype(BF16)


def _pack_small(vals, loss, rows):
    flat = jnp.concatenate([vals[n].reshape(-1) for n in SMALL] + [vals["conv_k"].reshape(-1), loss.reshape(-1)])
    return jnp.pad(flat, (0, rows * D - flat.shape[0])).reshape(rows, D)


def _unpack_small(arr, shapes):
    flat = arr.reshape(-1)
    out, pos = {}, 0
    for n in SMALL + ["conv_k"]:
        size = int(np.prod(shapes[n]))
        out[n] = flat[pos:pos + size].reshape(shapes[n])
        pos += size
    return out, flat[pos]


def kernel(x, ffn1_w_gate, ffn1_w_up, ffn1_w_down, ln1_g, ln1_b, w_in, b_in, ret_gn_g, conv_k, conv_b, conv_ln_g, conv_ln_b, w_ret_o, w_conv_o, w_out, ln2_g, ln2_b, ffn2_w_gate, ffn2_w_up, ffn2_w_down, ln3_g, ln3_b, loss_target, m_ffn1_w_gate, m_ffn1_w_up, m_ffn1_w_down, m_ln1_g, m_ln1_b, m_w_in, m_b_in, m_ret_gn_g, m_conv_k, m_conv_b, m_conv_ln_g, m_conv_ln_b, m_w_ret_o, m_w_conv_o, m_w_out, m_ln2_g, m_ln2_b, m_ffn2_w_gate, m_ffn2_w_up, m_ffn2_w_down, m_ln3_g, m_ln3_b, v_ffn1_w_gate, v_ffn1_w_up, v_ffn1_w_down, v_ln1_g, v_ln1_b, v_w_in, v_b_in, v_ret_gn_g, v_conv_k, v_conv_b, v_conv_ln_g, v_conv_ln_b, v_w_ret_o, v_w_conv_o, v_w_out, v_ln2_g, v_ln2_b, v_ffn2_w_gate, v_ffn2_w_up, v_ffn2_w_down, v_ln3_g, v_ln3_b):
    args = dict(locals())
    w = {n: args[n] for n in ORDER}
    m = {n: args["m_" + n] for n in ORDER}
    v = {n: args["v_" + n] for n in ORDER}
    xi, yi, ci = lax.axis_index("x"), lax.axis_index("y"), lax.axis_index("c")
    chip = 2 * xi + yi

    shards = {n: w[n][0] for n in BIG}
    wts = []
    for gi, keys in enumerate(GROUPS):
        slab = _pack_group(shards, keys)
        cols = slab.shape[1]
        wts.append(_gather_weights(slab.reshape(2, HALF, cols), f"gather_{gi}").reshape(NCHIP, D, cols))

    sp = {n: w[n] for n in SMALL}
    sp["conv_k"] = None
    kfull_shape = (CONV_W, D)
    kpad = jnp.zeros(kfull_shape, F32)
    kpad = lax.dynamic_update_slice(kpad, w["conv_k"][0, :, 0, :] * jnp.where(ci == 0, 1.0, 0.0), (0, chip * (D // NCHIP)))
    kvec = jnp.pad(kpad.reshape(-1), (0, SMALL_ROWS * D - CONV_W * D)).reshape(SMALL_ROWS, D)
    kall = _sum_devices(_all_gather_small(kvec, "gather_conv_k"), "sum_conv_k")
    sp["conv_k"] = kall.reshape(-1)[:CONV_W * D].reshape(kfull_shape)
    pos_c = jnp.reshape(ci, (1,)).astype(jnp.int32)
    pos_sc = jnp.stack([chip, ci]).astype(jnp.int32)
    out = {}

    def adam(gi, slab):
        for n in BIG:
            (g_of, off), width = LOC[SLAB[n]], _slab_width(n)
            if g_of != gi:
                continue
            w2 = w[n][0]
            if n in TRANSPOSED:
                res = _adamw(w2, slab[:, off:off + w2.shape[0]].T, m[n][0], v[n][0], "adamw_" + n)
                out[n] = [r[None] for r in res]
            elif n in MINOR_ROWS:
                res = _adamw(w2.T, slab[:, off:off + w2.shape[1]].T, m[n][0].T, v[n][0].T, "adamw_" + n)
                out[n] = [r.T[None] for r in res]
            else:
                res = _adamw(w2, slab, m[n][0], v[n][0], "adamw_" + n, g_block=(width, off // width))
                out[n] = [r[None] for r in res]

    grad_x, small_sum, shapes = _local_step(x[0], loss_target[0], wts, sp, pos_c, pos_sc, adam)
    small, total = _unpack_small(small_sum, shapes)

    for n in SMALL:
        res = _adamw(w[n], small[n], m[n], v[n], "adamw_" + n)
        out[n] = list(res)
    gk = lax.dynamic_slice(small["conv_k"], (0, chip * (D // NCHIP)), (CONV_W, D // NCHIP))
    res = _adamw(w["conv_k"][0, :, 0, :], gk, m["conv_k"][0, :, 0, :], v["conv_k"][0, :, 0, :], "adamw_conv_k")
    out["conv_k"] = [r[None, :, None, :] for r in res]

    grads = [out[n][0] for n in ORDER]
    deltas = [out[n][1] for n in ORDER]
    new_m = [out[n][2] for n in ORDER]
    new_v = [out[n][3] for n in ORDER]
    return (total, grad_x[None], *grads, *deltas, *new_m, *new_v)
```

```python
import dataclasses
import functools

import numpy as np
import jax
import jax.numpy as jnp
from jax import lax
from jax.experimental import pallas as pl
from jax.experimental.pallas import tpu as pltpu
from jax.experimental.pallas import tpu_sc as plsc

F32 = jnp.float32
BF16 = jnp.bfloat16

D = 1024
FS = 704
FSP = 768
FP = 4 * FSP
H = 8
DK = 128
DV = 256
CH = 128
VW = H * DV
INW = 10240
INS = INW // 4
CONV_W = 31
HALO = 32
EPS = 1e-5
ALPHA = 2.0 ** 0.25
ROPE_BASE = 10000.0
NCHIP = 4

ADAM_LR, ADAM_B1, ADAM_B2, ADAM_EPS, ADAM_WD, ADAM_STEP = 0.001, 0.9, 0.999, 1e-08, 0.01, 10

OFF = {"w_in": 0, "w_ret_o": 2560, "g1": 3072, "u1": 3840, "d1": 4608,
       "g2": 5376, "u2": 6144, "d2": 6912, "w_conv_o": 7680, "w_out": 7936}
WCOLS = 8192
WIDTH = {"w_in": INS, "w_ret_o": VW // NCHIP, "w_conv_o": D // NCHIP, "w_out": D // NCHIP,
         "g1": FSP, "u1": FSP, "d1": FSP, "g2": FSP, "u2": FSP, "d2": FSP}
GROUPS = (("g1", "u1"), ("d1",), ("w_in",), ("w_ret_o", "w_conv_o", "w_out"), ("g2", "u2", "d2"))
LOC = {}
for _gi, _keys in enumerate(GROUPS):
    _off = 0
    for _k in _keys:
        LOC[_k] = (_gi, _off)
        _off += WIDTH[_k]
GCOLS = [sum(WIDTH[k] for k in keys) for keys in GROUPS]
VMEM_LIMIT = 56 << 20


def _cp(*sem, **kw):
    return pltpu.CompilerParams(dimension_semantics=sem, vmem_limit_bytes=VMEM_LIMIT, **kw)


class _ProgramOrder:
    def __init__(self):
        self.active = False
        self.token = None


_ORDER = _ProgramOrder()


def _pcall(body, *, in_specs, scalar_prefetch=0, **kw):
    def call(*args):
        dep = _ORDER.token if _ORDER.active else None
        specs, fn = list(in_specs), body
        if dep is not None:
            n = len(args)

            def fn(*refs):
                return body(*refs[:n], *refs[n + 1:])

            specs.append(pl.BlockSpec(memory_space=pl.ANY))
            args = (*args, dep)
        params = dict(kw)
        if scalar_prefetch:
            params["grid_spec"] = pltpu.PrefetchScalarGridSpec(
                num_scalar_prefetch=scalar_prefetch, grid=params.pop("grid"), in_specs=specs,
                out_specs=params.pop("out_specs"))
        else:
            params["in_specs"] = specs
        out = pl.pallas_call(fn, **params)(*args)
        if _ORDER.active:
            _ORDER.token = jax.tree.leaves(out)[-1]
        return out

    return call


def _resident(shape, col_block):
    lead = (0,) * (len(shape) - 1)
    return pl.BlockSpec(shape, lambda *_: (*lead, col_block), pipeline_mode=pl.Buffered(1))


def _sig(x):
    return 1.0 / (1.0 + jnp.exp(-x))


def _dot(a, b):
    return jnp.dot(a, b, preferred_element_type=F32)


def _dot_nt(a, b):
    return lax.dot_general(a, b, (((1,), (1,)), ((), ())), preferred_element_type=F32)


def _ln_fwd(z, g, b):
    mu = jnp.mean(z, axis=-1, keepdims=True)
    xc = z - mu
    var = jnp.mean(xc * xc, axis=-1, keepdims=True)
    rstd = lax.rsqrt(var + EPS)
    xh = xc * rstd
    return xh * g + b, xh, rstd


def _ln_bwd(dy, xh, rstd, g):
    dxh = dy * g
    m1 = jnp.mean(dxh, axis=-1, keepdims=True)
    m2 = jnp.mean(dxh * xh, axis=-1, keepdims=True)
    return rstd * (dxh - m1 - xh * m2)


def _colsum(x):
    return jnp.sum(x, axis=0, keepdims=True)


def _acc_rows(ref, first, val):
    @pl.when(first)
    def _():
        ref[...] = val

    @pl.when(jnp.logical_not(first))
    def _():
        ref[...] += val


def _rope_tables(T):
    half = DK // 2
    freqs = ROPE_BASE ** (-np.arange(half, dtype=np.float32) / half)
    ang = (np.arange(T, dtype=np.float32)[:, None] * freqs[None, :]).astype(np.float32)
    cos, sin = np.cos(ang), np.sin(ang)
    return (jnp.asarray(np.concatenate([cos, cos], 1), F32),
            jnp.asarray(np.concatenate([-sin, sin], 1), F32))


def _decay_tables():
    h = np.arange(H, dtype=np.float64)
    log_g = np.log(1.0 - np.exp2(-5.0 - h))
    idx = np.arange(CH, dtype=np.float64)
    diff = idx[:, None] - idx[None, :]
    dm = np.where(diff[None] >= 0, np.exp(np.maximum(diff, 0.0)[None] * log_g[:, None, None]), 0.0)
    xi = np.exp((idx[None, :] + 1.0) * log_g[:, None])
    zeta = np.exp((CH - 1.0 - idx)[None, :] * log_g[:, None])
    cd = np.exp(CH * log_g)
    xi_t = np.broadcast_to(xi[:, :, None], (H, CH, DV))
    zeta_t = np.broadcast_to(zeta[:, :, None], (H, CH, DK))
    return (jnp.asarray(dm, F32), jnp.asarray(xi_t, F32), jnp.asarray(zeta_t, F32),
            [float(v) for v in cd])


def _cast_t(x):
    T = x.shape[0]
    tm = min(T, 512)

    def body(x_ref, xb_ref, xt_ref):
        v = x_ref[...]
        xb_ref[...] = v.astype(BF16)
        xt_ref[...] = v.T.astype(BF16)

    return _pcall(
        body, name="cast_t", grid=(T // tm,),
        in_specs=[pl.BlockSpec((tm, D), lambda i: (i, 0))],
        out_specs=[pl.BlockSpec((tm, D), lambda i: (i, 0)), pl.BlockSpec((D, tm), lambda i: (0, i))],
        out_shape=[jax.ShapeDtypeStruct((T, D), BF16), jax.ShapeDtypeStruct((D, T), BF16)],
        compiler_params=_cp("parallel"))(x)


def _ffn_up(xb, wall, og, ou, name):
    T = xb.shape[0]
    tm = min(T, 512)
    assert ou == og + FSP

    def body(x_ref, w_ref, a_ref, b_ref, h_ref):
        x = x_ref[...]
        for s in range(NCHIP):
            cols = slice(s * FSP, (s + 1) * FSP)
            a = _dot(x, w_ref[s, :, 0:FSP])
            b = _dot(x, w_ref[s, :, FSP:2 * FSP])
            a_ref[:, cols] = a.astype(BF16)
            b_ref[:, cols] = b.astype(BF16)
            h_ref[:, cols] = (a * _sig(a) * b).astype(BF16)

    ospec = pl.BlockSpec((tm, FP), lambda i: (i, 0))
    return _pcall(
        body, name=name, grid=(T // tm,),
        in_specs=[pl.BlockSpec((tm, D), lambda i: (i, 0)), _resident((NCHIP, D, 2 * FSP), og // (2 * FSP))],
        out_specs=[ospec] * 3, out_shape=[jax.ShapeDtypeStruct((T, FP), BF16)] * 3,
        compiler_params=_cp("parallel"))(xb, wall)


def _proj_ln(hb, wall, off, res, g, b, coef, name, want_b=True):
    T, K = hb.shape
    ks = K // NCHIP
    tm = min(T, 256)

    def body(h_ref, w_ref, r_ref, g_ref, b_ref, z_ref, *rest):
        acc = _dot_nt(h_ref[:, 0:ks], w_ref[0])
        for s in range(1, NCHIP):
            acc += _dot_nt(h_ref[:, s * ks:(s + 1) * ks], w_ref[s])
        z = ALPHA * r_ref[...] + coef * acc
        z_ref[...] = z
        if want_b:
            y, _, _ = _ln_fwd(z, g_ref[...], b_ref[...])
            y_ref, yb_ref, yt_ref = rest
            y_ref[...] = y
            yb_ref[...] = y.astype(BF16)
            yt_ref[...] = y.T.astype(BF16)

    row = pl.BlockSpec((tm, D), lambda i: (i, 0))
    vec = pl.BlockSpec((1, D), lambda i: (0, 0))
    out_specs = [row]
    out_shape = [jax.ShapeDtypeStruct((T, D), F32)]
    if want_b:
        out_specs += [row, row, pl.BlockSpec((D, tm), lambda i: (0, i))]
        out_shape += [jax.ShapeDtypeStruct((T, D), F32), jax.ShapeDtypeStruct((T, D), BF16),
                      jax.ShapeDtypeStruct((D, T), BF16)]
    return _pcall(
        body, name=name, grid=(T // tm,),
        in_specs=[pl.BlockSpec((tm, K), lambda i: (i, 0)),
                  pl.BlockSpec((NCHIP, D, ks), lambda i: (0, 0, off // ks)), row, vec, vec],
        out_specs=out_specs, out_shape=out_shape,
        compiler_params=_cp("parallel"))(hb, wall, res, g, b)


def _inproj(xb, wall, off, b_in, cos_t, sin_t):
    T = xb.shape[0]
    tm, tn = min(T, 512), 512
    assert off == 0

    def body(x_ref, w_ref, bias_ref, cos_ref, sin_ref, o_ref):
        x = x_ref[...]
        c = cos_ref[...]
        s = sin_ref[...]
        for n0 in range(0, INW, tn):
            chip, c0 = divmod(n0, INS)
            acc = _dot(x, w_ref[chip, :, c0:c0 + tn]) + bias_ref[:, n0:n0 + tn]
            if n0 >= 2 * D:
                o_ref[:, n0:n0 + tn] = acc.astype(BF16)
                continue
            scale = DK ** -0.5 if n0 < D else 1.0
            for hh in range(tn // DK):
                xh = acc[:, hh * DK:(hh + 1) * DK]
                o = (xh * c + pltpu.roll(xh, DK // 2, 1) * s) * scale
                o_ref[:, n0 + hh * DK:n0 + (hh + 1) * DK] = o.astype(BF16)

    return _pcall(
        body, name="inproj", grid=(T // tm,),
        in_specs=[pl.BlockSpec((tm, D), lambda i: (i, 0)),
                  _resident((NCHIP, D, INS), 0),
                  pl.BlockSpec((1, INW), lambda i: (0, 0)),
                  pl.BlockSpec((tm, DK), lambda i: (i, 0)),
                  pl.BlockSpec((tm, DK), lambda i: (i, 0))],
        out_specs=pl.BlockSpec((tm, INW), lambda i: (i, 0)),
        out_shape=jax.ShapeDtypeStruct((T, INW), BF16),
        compiler_params=_cp("parallel"))(xb, wall, b_in, cos_t, sin_t)


RET_CPS = 2


def _retention_fwd(proj, gn_g, dm_t, xi_t, zeta_t, cds):
    T = proj.shape[0]
    n = T // CH
    tr = RET_CPS * CH

    def body(q_ref, k_ref, v_ref, g_ref, gn_ref, dm_ref, xi_ref, zt_ref, r_ref, ri_ref, st_ref, state):
        @pl.when(pl.program_id(0) == 0)
        def _():
            state[...] = jnp.zeros_like(state)

        for h in range(H):
            rows = slice(h * DK, (h + 1) * DK)
            cols = slice(h * DV, (h + 1) * DV)
            s_prev = state[rows, :]
            for j in range(RET_CPS):
                t = slice(j * CH, (j + 1) * CH)
                q = q_ref[t, h * DK:(h + 1) * DK]
                k = k_ref[t, h * DK:(h + 1) * DK]
                v = v_ref[t, cols]
                s_b = s_prev.astype(BF16)
                st_ref[j, rows, :] = s_b
                sc = _dot_nt(q, k) * dm_ref[h]
                r = _dot(sc.astype(BF16), v) + _dot(q, s_b) * xi_ref[h]
                kz = k.astype(F32) * zt_ref[h]
                s_prev = cds[h] * s_prev + _dot(kz.T.astype(BF16), v)
                r_ref[t, cols] = r
                mu = jnp.mean(r, axis=-1, keepdims=True)
                xc = r - mu
                var = jnp.mean(xc * xc, axis=-1, keepdims=True)
                y = xc * lax.rsqrt(var + EPS) * gn_ref[:, cols]
                g = g_ref[t, cols].astype(F32)
                ri_ref[t, cols] = (g * _sig(g) * y).astype(BF16)
            state[rows, :] = s_prev

    full3 = lambda shp: pl.BlockSpec(shp, lambda c: (0, 0, 0))
    return _pcall(
        body, name="retention_fwd", grid=(n // RET_CPS,),
        in_specs=[pl.BlockSpec((tr, D), lambda c: (c, 0)),
                  pl.BlockSpec((tr, D), lambda c: (c, 1)),
                  pl.BlockSpec((tr, VW), lambda c: (c, 1)),
                  pl.BlockSpec((tr, VW), lambda c: (c, 2)),
                  pl.BlockSpec((1, VW), lambda c: (0, 0)),
                  full3((H, CH, CH)), full3((H, CH, DV)), full3((H, CH, DK))],
        out_specs=[pl.BlockSpec((tr, VW), lambda c: (c, 0)), pl.BlockSpec((tr, VW), lambda c: (c, 0)),
                   pl.BlockSpec((RET_CPS, H * DK, DV), lambda c: (c, 0, 0))],
        out_shape=[jax.ShapeDtypeStruct((T, VW), F32), jax.ShapeDtypeStruct((T, VW), BF16),
                   jax.ShapeDtypeStruct((n, H * DK, DV), BF16)],
        scratch_shapes=[pltpu.VMEM((H * DK, DV), F32)],
        compiler_params=_cp("arbitrary"))(proj, proj, proj, proj, gn_g, dm_t, xi_t, zeta_t)


CONV_TT = 256
CONV_SB = 64
CONV_CB = 256


SUB = 8
CONV_PAD = 8


def _glu(a_ref, b_ref, rows=slice(None)):
    a = a_ref[rows, :].astype(F32)
    sb = _sig(b_ref[rows, :].astype(F32))
    return a, sb


def _shift_copies(win, sh, rows):
    win[rows:rows + CONV_PAD, :] = jnp.zeros((CONV_PAD, D), F32)
    for b in range(1, SUB):
        sh[b - 1, :, :] = win[b:b + rows, :]


def _tap(win, sh, start, size, cs):
    a, b = divmod(start, SUB)
    src = win if b == 0 else sh.at[b - 1]
    return src[SUB * a:SUB * a + size, cs]


def _conv_fwd(proj, conv_k, conv_b, ln_g, ln_b):
    T = proj.shape[0]
    tt = min(T, CONV_TT)
    ca, cb = 6 * D // D, 7 * D // D

    def body(a_ref, b_ref, pa_ref, pb_ref, k_ref, cb_ref, g_ref, bb_ref, u1_ref, u3_ref, win, sh):
        i = pl.program_id(0)
        a, sb = _glu(a_ref, b_ref)
        win[HALO:tt + HALO, :] = a * sb
        pa, psb = _glu(pa_ref, pb_ref, slice(tt - HALO, tt))
        win[0:HALO, :] = jnp.where(i > 0, pa * psb, 0.0)
        _shift_copies(win, sh, tt + HALO)
        for c0 in range(0, D, CONV_CB):
            cs = slice(c0, c0 + CONV_CB)
            for r0 in range(0, tt, CONV_SB):
                acc = jnp.zeros((CONV_SB, CONV_CB), F32)
                for w in range(CONV_W):
                    st = r0 + HALO - (CONV_W - 1) + w
                    acc += _tap(win, sh, st, CONV_SB, cs) * k_ref[w:w + 1, cs]
                u1_ref[r0:r0 + CONV_SB, cs] = acc + cb_ref[:, cs]
        u2, _, _ = _ln_fwd(u1_ref[...], g_ref[...], bb_ref[...])
        u3_ref[...] = (u2 * _sig(u2)).astype(BF16)

    vec = pl.BlockSpec((1, D), lambda i: (0, 0))
    row = pl.BlockSpec((tt, D), lambda i: (i, 0))
    return _pcall(
        body, name="conv_fwd", grid=(T // tt,),
        in_specs=[pl.BlockSpec((tt, D), lambda i: (i, ca)), pl.BlockSpec((tt, D), lambda i: (i, cb)),
                  pl.BlockSpec((tt, D), lambda i: (jnp.maximum(i - 1, 0), ca)),
                  pl.BlockSpec((tt, D), lambda i: (jnp.maximum(i - 1, 0), cb)),
                  pl.BlockSpec((CONV_W, D), lambda i: (0, 0)), vec, vec, vec],
        out_specs=[row, row],
        out_shape=[jax.ShapeDtypeStruct((T, D), F32), jax.ShapeDtypeStruct((T, D), BF16)],
        scratch_shapes=[pltpu.VMEM((tt + HALO + CONV_PAD, D), F32), pltpu.VMEM((SUB - 1, tt + HALO, D), F32)],
        compiler_params=_cp("parallel"))(proj, proj, proj, proj, conv_k, conv_b, ln_g, ln_b)


def _merge(ret_in, u3, proj, wall, off_r, off_c):
    T = ret_in.shape[0]
    tm = min(T, 512)
    kr, kc = VW // NCHIP, D // NCHIP

    def body(ri_ref, u3_ref, gr_ref, gc_ref, wr_ref, wc_ref, ro_ref, co_ref, m_ref):
        ro = _dot_nt(ri_ref[:, 0:kr], wr_ref[0])
        co = _dot_nt(u3_ref[:, 0:kc], wc_ref[0])
        for s in range(1, NCHIP):
            ro += _dot_nt(ri_ref[:, s * kr:(s + 1) * kr], wr_ref[s])
            co += _dot_nt(u3_ref[:, s * kc:(s + 1) * kc], wc_ref[s])
        ro_ref[...] = ro.astype(BF16)
        co_ref[...] = co.astype(BF16)
        m = _sig(gr_ref[...].astype(F32)) * ro + _sig(gc_ref[...].astype(F32)) * co
        m_ref[...] = m.astype(BF16)

    row = pl.BlockSpec((tm, D), lambda i: (i, 0))
    return _pcall(
        body, name="merge", grid=(T // tm,),
        in_specs=[pl.BlockSpec((tm, VW), lambda i: (i, 0)), row,
                  pl.BlockSpec((tm, D), lambda i: (i, 8)), pl.BlockSpec((tm, D), lambda i: (i, 9)),
                  pl.BlockSpec((NCHIP, D, kr), lambda i: (0, 0, off_r // kr)),
                  pl.BlockSpec((NCHIP, D, kc), lambda i: (0, 0, off_c // kc))],
        out_specs=[row] * 3, out_shape=[jax.ShapeDtypeStruct((T, D), BF16)] * 3,
        compiler_params=_cp("parallel"))(ret_in, u3, proj, proj, wall, wall)


def _loss_ln_bwd(z, g, b, target, coef):
    T = z.shape[0]
    tm = min(T, 256)
    nt = T // tm

    def body(z_ref, g_ref, b_ref, t_ref, loss_ref, dzb_ref, dzt_ref, dz_ref, dg_ref, db_ref, lacc):
        i = pl.program_id(0)
        gam = g_ref[...]
        y, xh, rstd = _ln_fwd(z_ref[...], gam, b_ref[...])
        e = y - t_ref[...]
        part = _colsum(e * e)
        _acc_rows(lacc, i == 0, part)
        dy = e * (1.0 / D)
        dz = _ln_bwd(dy, xh, rstd, gam)
        dz_ref[...] = dz
        dzc = coef * dz
        dzb_ref[...] = dzc.astype(BF16)
        dzt_ref[...] = dzc.T.astype(BF16)
        _acc_rows(dg_ref, i == 0, _colsum(dy * xh))
        _acc_rows(db_ref, i == 0, _colsum(dy))

        @pl.when(i == nt - 1)
        def _():
            loss_ref[...] = (0.5 / D) * jnp.sum(lacc[...], axis=1, keepdims=True)

    row = pl.BlockSpec((tm, D), lambda i: (i, 0))
    vec = pl.BlockSpec((1, D), lambda i: (0, 0))
    return _pcall(
        body, name="loss_ln_bwd", grid=(nt,),
        in_specs=[row, vec, vec, row],
        out_specs=[pl.BlockSpec((1, 1), lambda i: (0, 0)), row, pl.BlockSpec((D, tm), lambda i: (0, i)),
                   row, vec, vec],
        out_shape=[jax.ShapeDtypeStruct((1, 1), F32), jax.ShapeDtypeStruct((T, D), BF16),
                   jax.ShapeDtypeStruct((D, T), BF16), jax.ShapeDtypeStruct((T, D), F32),
                   jax.ShapeDtypeStruct((1, D), F32), jax.ShapeDtypeStruct((1, D), F32)],
        scratch_shapes=[pltpu.VMEM((1, D), F32)],
        compiler_params=_cp("arbitrary"))(z, g, b, target)


def _ffn_bwd_h(dfb, wall, od, a, b, name):
    T = dfb.shape[0]
    tm = min(T, 512)

    def body(d_ref, w_ref, a_ref, b_ref, da_ref, db_ref):
        d = d_ref[...]
        for s in range(NCHIP):
            cols = slice(s * FSP, (s + 1) * FSP)
            dh = _dot(d, w_ref[s])
            a = a_ref[:, cols].astype(F32)
            sg = _sig(a)
            da_ref[:, cols] = (dh * b_ref[:, cols].astype(F32) * (sg * (1.0 + a * (1.0 - sg)))).astype(BF16)
            db_ref[:, cols] = (dh * a * sg).astype(BF16)

    ospec = pl.BlockSpec((tm, FP), lambda i: (i, 0))
    return _pcall(
        body, name=name, grid=(T // tm,),
        in_specs=[pl.BlockSpec((tm, D), lambda i: (i, 0)), _resident((NCHIP, D, FSP), od // FSP), ospec, ospec],
        out_specs=[ospec] * 2, out_shape=[jax.ShapeDtypeStruct((T, FP), BF16)] * 2,
        compiler_params=_cp("parallel"))(dfb, wall, a, b)


DX_SUB = 256


def _dx_partial(lhs, wall, chips, name):
    T, K = lhs.shape
    ks = K // NCHIP
    nc = len(chips)
    assert list(chips) == list(range(chips[0], chips[0] + nc)) and chips[0] % nc == 0
    tm = min(T, 512)

    def body(l_ref, w_ref, o_ref):
        for r0 in range(0, tm, DX_SUB):
            r = slice(r0, r0 + DX_SUB)
            acc = _dot_nt(l_ref[r, 0:ks], w_ref[0])
            for s in range(1, nc):
                acc += _dot_nt(l_ref[r, s * ks:(s + 1) * ks], w_ref[s])
            o_ref[r, :] = acc

    blk = chips[0] // nc
    return _pcall(
        body, name=name, grid=(T // tm,),
        in_specs=[pl.BlockSpec((tm, nc * ks), lambda i: (i, blk)),
                  pl.BlockSpec((nc, D, ks), lambda i: (blk, 0, 0), pipeline_mode=pl.Buffered(1))],
        out_specs=pl.BlockSpec((tm, D), lambda i: (i, 0)),
        out_shape=jax.ShapeDtypeStruct((T, D), F32),
        compiler_params=_cp("parallel"))(lhs, wall)


def _dx_bwd(lhs, offs, wall, dz_next, name, ln=None, chips=tuple(range(NCHIP)), partial=None):
    T, K = lhs[0].shape
    ks = K // NCHIP
    nl = len(lhs)
    nc = len(chips)
    assert list(offs) == [l * ks for l in range(nl)]
    assert list(chips) == list(range(chips[0], chips[0] + nc)) and chips[0] % nc == 0
    blk = chips[0] // nc
    tm = min(T, 512)

    def body(*refs):
        l_refs = refs[:nl]
        w_ref = refs[nl]
        dzn_ref = refs[nl + 1]
        pos = nl + 2
        if partial is not None:
            p_ref = refs[pos]
            pos += 1
        if ln is not None:
            z_ref, g_ref = refs[pos:pos + 2]
            pos += 2
        outs = refs[pos:]
        sums = list(outs[3:]) if ln is not None else list(outs[1:])

        @pl.when(pl.program_id(0) == 0)
        def _():
            for ref in sums:
                ref[...] = jnp.zeros_like(ref)

        for r0 in range(0, tm, DX_SUB):
            r = slice(r0, r0 + DX_SUB)
            acc = None if partial is None else p_ref[r, :]
            for s in range(nc):
                rows = slice(s * ks, (s + 1) * ks)
                for l in range(nl):
                    part = _dot_nt(l_refs[l][r, rows], w_ref[s, :, l * ks:(l + 1) * ks])
                    acc = part if acc is None else acc + part
            dy = acc + ALPHA * dzn_ref[r, :]
            if ln is None:
                outs[0][r, :] = dy
            else:
                gam = g_ref[...]
                _, xh, rstd = _ln_fwd(z_ref[r, :], gam, 0.0)
                dz = _ln_bwd(dy, xh, rstd, gam)
                dzc = ln[2] * dz
                outs[0][r, :] = dzc.astype(BF16)
                outs[1][:, r] = dzc.T.astype(BF16)
                outs[2][r, :] = dz
                outs[3][...] += _colsum(dy * xh)
                outs[4][...] += _colsum(dy)

    row = pl.BlockSpec((tm, D), lambda i: (i, 0))
    vec = pl.BlockSpec((1, D), lambda i: (0, 0))
    in_specs = [pl.BlockSpec((tm, nc * ks), lambda i: (i, blk))] * nl
    in_specs += [pl.BlockSpec((nc, D, nl * ks), lambda i: (blk, 0, 0), pipeline_mode=pl.Buffered(1)), row]
    args = list(lhs) + [wall, dz_next]
    if partial is not None:
        in_specs.append(row)
        args.append(partial)
    if ln is None:
        out_specs = [row]
        out_shape = [jax.ShapeDtypeStruct((T, D), F32)]
    else:
        in_specs += [row, vec]
        args += [ln[0], ln[1]]
        out_specs = [row, pl.BlockSpec((D, tm), lambda i: (0, i)), row, vec, vec]
        out_shape = [jax.ShapeDtypeStruct((T, D), BF16), jax.ShapeDtypeStruct((D, T), BF16),
                     jax.ShapeDtypeStruct((T, D), F32), jax.ShapeDtypeStruct((1, D), F32),
                     jax.ShapeDtypeStruct((1, D), F32)]
    return _pcall(
        body, name=name, grid=(T // tm,), in_specs=in_specs, out_specs=out_specs, out_shape=out_shape,
        compiler_params=_cp("arbitrary"))(*args)


def _wgrad(lhs_t, rhs, key, name, g_all=None, colsum=False):
    T, N = rhs.shape
    tn = next(c for c in (768, 512, 256) if (N // NCHIP) % c == 0 and LOC[key][1] % c == 0)
    nps = N // NCHIP // tn
    off = LOC[key][1]
    cols = GCOLS[LOC[key][0]]

    def body(*refs):
        l_ref, r_ref = refs[0], refs[1]
        o_ref, t_ref = refs[-2 - colsum], refs[-1]
        o_ref[...] = _dot(l_ref[...], r_ref[...]).astype(BF16)
        if colsum:
            refs[-2][...] = _colsum(r_ref[...].astype(F32))
        t_ref[...] = jnp.zeros_like(t_ref)

    in_specs = [_resident((D, T), 0), pl.BlockSpec((T, tn), lambda j: (0, j))]
    args = [lhs_t, rhs]
    aliases = {}
    if g_all is not None:
        in_specs.append(pl.BlockSpec(memory_space=pl.ANY))
        args.append(g_all)
        aliases = {2: 0}
    out_specs = [pl.BlockSpec((None, D, tn), lambda j: (j // nps, 0, off // tn + j % nps))]
    out_shape = [jax.ShapeDtypeStruct((NCHIP, D, cols), BF16)]
    if colsum:
        out_specs.append(pl.BlockSpec((1, tn), lambda j: (0, j)))
        out_shape.append(jax.ShapeDtypeStruct((1, N), F32))
    out_specs.append(pl.BlockSpec((8, 128), lambda j: (0, 0)))
    out_shape.append(jax.ShapeDtypeStruct((8, 128), F32))
    res = _pcall(
        body, name=name, grid=(N // tn,), in_specs=in_specs, out_specs=out_specs, out_shape=out_shape,
        input_output_aliases=aliases,
        compiler_params=_cp("arbitrary"))(*args)
    return (res[0], res[1]) if colsum else res[0]


def _merge_bwd(dmb, wall, off, proj, ro, co):
    T = dmb.shape[0]
    tm = min(T, 512)
    ks = D // NCHIP

    def body(d_ref, w_ref, gr_ref, gc_ref, ro_ref, co_ref, dro_ref, drot_ref, dco_ref, dcot_ref, dp_ref):
        d = d_ref[...]
        dmg = jnp.concatenate([_dot(d, w_ref[s]) for s in range(NCHIP)], axis=1)
        sr = _sig(gr_ref[...].astype(F32))
        sc = _sig(gc_ref[...].astype(F32))
        dro = dmg * sr
        dco = dmg * sc
        dro_ref[...] = dro.astype(BF16)
        drot_ref[...] = dro.T.astype(BF16)
        dco_ref[...] = dco.astype(BF16)
        dcot_ref[...] = dco.T.astype(BF16)
        dp_ref[:, 0:D] = (dmg * ro_ref[...].astype(F32) * sr * (1.0 - sr)).astype(BF16)
        dp_ref[:, D:2 * D] = (dmg * co_ref[...].astype(F32) * sc * (1.0 - sc)).astype(BF16)

    row = pl.BlockSpec((tm, D), lambda i: (i, 0))
    col = pl.BlockSpec((D, tm), lambda i: (0, i))
    return _pcall(
        body, name="merge_bwd", grid=(T // tm,),
        in_specs=[row, pl.BlockSpec((NCHIP, D, ks), lambda i: (0, 0, off // ks)),
                  pl.BlockSpec((tm, D), lambda i: (i, 8)), pl.BlockSpec((tm, D), lambda i: (i, 9)), row, row],
        out_specs=[row, col, row, col, pl.BlockSpec((tm, 2 * D), lambda i: (i, 4))],
        out_shape=[jax.ShapeDtypeStruct((T, D), BF16), jax.ShapeDtypeStruct((D, T), BF16),
                   jax.ShapeDtypeStruct((T, D), BF16), jax.ShapeDtypeStruct((D, T), BF16),
                   jax.ShapeDtypeStruct((T, INW), BF16)],
        compiler_params=_cp("parallel"))(dmb, wall, proj, proj, ro, co)


def _reto_bwd(dro, wall, off, r, proj, gn_g, dproj):
    T = dro.shape[0]
    tm = min(T, 512)
    hps = H // NCHIP

    def body(d_ref, w_ref, r_ref, g_ref, gn_ref, _, dr_ref, dgn_ref, dp_ref):
        i = pl.program_id(1)
        dri = _dot(d_ref[...], w_ref[...])
        rr = r_ref[...]
        mu = jnp.mean(rr, axis=-1, keepdims=True)
        xc = rr - mu
        var = jnp.mean(xc * xc, axis=-1, keepdims=True)
        rstd = lax.rsqrt(var + EPS)
        rn = xc * rstd
        gn = gn_ref[...]
        g = g_ref[...].astype(F32)
        sg = _sig(g)
        dy = dri * (g * sg)
        dp_ref[...] = (dri * (rn * gn) * (sg * (1.0 + g * (1.0 - sg)))).astype(BF16)
        _acc_rows(dgn_ref, i == 0, _colsum(dy * rn))
        dr_ref[...] = _ln_bwd(dy, rn, rstd, gn).astype(BF16)

    return _pcall(
        body, name="reto_bwd", grid=(H, T // tm),
        in_specs=[pl.BlockSpec((tm, D), lambda j, i: (i, 0)),
                  pl.BlockSpec((None, D, DV), lambda j, i: (j // hps, 0, off // DV + j % hps)),
                  pl.BlockSpec((tm, DV), lambda j, i: (i, j)),
                  pl.BlockSpec((tm, DV), lambda j, i: (i, 2 * VW // DV + j)),
                  pl.BlockSpec((1, DV), lambda j, i: (0, j)),
                  pl.BlockSpec(memory_space=pl.ANY)],
        out_specs=[pl.BlockSpec((tm, DV), lambda j, i: (i, j)), pl.BlockSpec((1, DV), lambda j, i: (0, j)),
                   pl.BlockSpec((tm, DV), lambda j, i: (i, 2 * VW // DV + j))],
        out_shape=[jax.ShapeDtypeStruct((T, VW), BF16), jax.ShapeDtypeStruct((1, VW), F32),
                   jax.ShapeDtypeStruct((T, INW), BF16)],
        input_output_aliases={5: 2},
        compiler_params=_cp("arbitrary", "arbitrary"))(dro, wall, r, proj, gn_g, dproj)


def _retention_bwd(proj, dr, states, cos_t, sin_t, dm_t, xi_t, zeta_t, cds, dproj):
    T = proj.shape[0]
    n = T // CH // RET_CPS
    tr = RET_CPS * CH
    scale = DK ** -0.5

    def body(q_ref, k_ref, v_ref, dr_ref, st_ref, cos_ref, sin_ref, dm_ref, xi_ref, zt_ref, _, dp_ref, ds):
        @pl.when(pl.program_id(0) == 0)
        def _():
            ds[...] = jnp.zeros_like(ds)

        def unrope(d, t):
            return d * cos_ref[t, :] + pltpu.roll(d * sin_ref[t, :], DK // 2, 1)

        for h in range(H):
            rows = slice(h * DK, (h + 1) * DK)
            dm = dm_ref[h]
            zt = zt_ref[h]
            ds_prev = ds[rows, :]
            for j in reversed(range(RET_CPS)):
                t = slice(j * CH, (j + 1) * CH)
                q = q_ref[t, h * DK:(h + 1) * DK]
                k = k_ref[t, h * DK:(h + 1) * DK]
                v = v_ref[t, h * DV:(h + 1) * DV]
                d_r = dr_ref[t, h * DV:(h + 1) * DV]
                s_b = st_ref[j, rows, :]
                sc = _dot_nt(q, k) * dm
                dsc = _dot_nt(d_r, v) * dm
                drx = (d_r.astype(F32) * xi_ref[h]).astype(BF16)
                ds_b = ds_prev.astype(BF16)
                kz = (k.astype(F32) * zt).astype(BF16)
                dq = _dot(dsc.astype(BF16), k) + _dot_nt(drx, s_b)
                dk = _dot(dsc.T.astype(BF16), q) + _dot_nt(v, ds_b) * zt
                dv = _dot(sc.T.astype(BF16), d_r) + _dot(kz, ds_b)
                ds_prev = cds[h] * ds_prev + _dot(q.astype(F32).T.astype(BF16), drx)
                dp_ref[t, h * DK:(h + 1) * DK] = unrope(dq * scale, t).astype(BF16)
                dp_ref[t, D + h * DK:D + (h + 1) * DK] = unrope(dk, t).astype(BF16)
                dp_ref[t, 2 * D + h * DV:2 * D + (h + 1) * DV] = dv.astype(BF16)
            ds[rows, :] = ds_prev

    rv = lambda c: n - 1 - c
    full3 = lambda shp: pl.BlockSpec(shp, lambda c: (0, 0, 0))
    return _pcall(
        body, name="retention_bwd", grid=(n,),
        in_specs=[pl.BlockSpec((tr, D), lambda c: (rv(c), 0)),
                  pl.BlockSpec((tr, D), lambda c: (rv(c), 1)),
                  pl.BlockSpec((tr, VW), lambda c: (rv(c), 1)),
                  pl.BlockSpec((tr, VW), lambda c: (rv(c), 0)),
                  pl.BlockSpec((RET_CPS, H * DK, DV), lambda c: (rv(c), 0, 0)),
                  pl.BlockSpec((tr, DK), lambda c: (rv(c), 0)),
                  pl.BlockSpec((tr, DK), lambda c: (rv(c), 0)),
                  full3((H, CH, CH)), full3((H, CH, DV)), full3((H, CH, DK)),
                  pl.BlockSpec(memory_space=pl.ANY)],
        out_specs=pl.BlockSpec((tr, 2 * D + VW), lambda c: (rv(c), 0)),
        out_shape=jax.ShapeDtypeStruct((T, INW), BF16),
        input_output_aliases={10: 0},
        scratch_shapes=[pltpu.VMEM((H * DK, DV), F32)],
        compiler_params=_cp("arbitrary"))(proj, proj, proj, dr, states, cos_t, sin_t, dm_t, xi_t, zeta_t, dproj)


def _convo_bwd(dco, wall, off, u1, ln_g, ln_b):
    T = dco.shape[0]
    tm = min(T, 512)
    ks = D // NCHIP

    def body(d_ref, w_ref, u1_ref, g_ref, b_ref, du1_ref, dg_ref, db_ref, dcb_ref):
        i = pl.program_id(0)
        d = d_ref[...]
        du3 = jnp.concatenate([_dot(d, w_ref[s]) for s in range(NCHIP)], axis=1)
        gam = g_ref[...]
        u2, xh, rstd = _ln_fwd(u1_ref[...], gam, b_ref[...])
        sg = _sig(u2)
        du2 = du3 * (sg * (1.0 + u2 * (1.0 - sg)))
        du1 = _ln_bwd(du2, xh, rstd, gam)
        du1_ref[...] = du1
        _acc_rows(dg_ref, i == 0, _colsum(du2 * xh))
        _acc_rows(db_ref, i == 0, _colsum(du2))
        _acc_rows(dcb_ref, i == 0, _colsum(du1))

    row = pl.BlockSpec((tm, D), lambda i: (i, 0))
    vec = pl.BlockSpec((1, D), lambda i: (0, 0))
    return _pcall(
        body, name="convo_bwd", grid=(T // tm,),
        in_specs=[row, pl.BlockSpec((NCHIP, D, ks), lambda i: (0, 0, off // ks)), row, vec, vec],
        out_specs=[row, vec, vec, vec],
        out_shape=[jax.ShapeDtypeStruct((T, D), F32)] + [jax.ShapeDtypeStruct((1, D), F32)] * 3,
        compiler_params=_cp("arbitrary"))(dco, wall, u1, ln_g, ln_b)


def _conv_bwd(du1, proj, conv_k, dproj):
    T = du1.shape[0]
    tt = min(T, CONV_TT)
    nt = T // tt
    ca, cb = 6, 7

    def body(d_ref, dn_ref, a_ref, b_ref, pa_ref, pb_ref, k_ref, _, dp_ref, dk_ref, win_u, win_d, sh_u, sh_d):
        i = pl.program_id(0)
        a, sb = _glu(a_ref, b_ref)
        win_u[HALO:tt + HALO, :] = a * sb
        pa, psb = _glu(pa_ref, pb_ref, slice(tt - HALO, tt))
        win_u[0:HALO, :] = jnp.where(i > 0, pa * psb, 0.0)
        win_d[0:tt, :] = d_ref[...]
        win_d[tt:tt + HALO, :] = jnp.where(i < nt - 1, dn_ref[0:HALO, :], 0.0)
        _shift_copies(win_u, sh_u, tt + HALO)
        _shift_copies(win_d, sh_d, tt + HALO)

        @pl.when(i == 0)
        def _():
            dk_ref[...] = jnp.zeros_like(dk_ref)

        for c0 in range(0, D, CONV_CB):
            cs = slice(c0, c0 + CONV_CB)
            for r0 in range(0, tt, CONV_SB):
                acc = jnp.zeros((CONV_SB, CONV_CB), F32)
                for w in range(CONV_W):
                    st = r0 + (CONV_W - 1) - w
                    acc += _tap(win_d, sh_d, st, CONV_SB, cs) * k_ref[w:w + 1, cs]
                aa = a_ref[r0:r0 + CONV_SB, cs].astype(F32)
                ss = _sig(b_ref[r0:r0 + CONV_SB, cs].astype(F32))
                dp_ref[r0:r0 + CONV_SB, cs] = (acc * ss).astype(BF16)
                dp_ref[r0:r0 + CONV_SB, c0 + D:c0 + D + CONV_CB] = (acc * aa * ss * (1.0 - ss)).astype(BF16)
            for w in range(CONV_W):
                acc = jnp.zeros((CONV_SB, CONV_CB), F32)
                for r0 in range(0, tt, CONV_SB):
                    st = r0 + HALO - (CONV_W - 1) + w
                    acc += win_d[r0:r0 + CONV_SB, cs] * _tap(win_u, sh_u, st, CONV_SB, cs)
                dk_ref[w:w + 1, cs] += _colsum(acc)

    blk = lambda f, c: pl.BlockSpec((tt, D), lambda i: (f(i), c))
    cur = lambda i: i
    prv = lambda i: jnp.maximum(i - 1, 0)
    nxt = lambda i: jnp.minimum(i + 1, nt - 1)
    return _pcall(
        body, name="conv_bwd", grid=(nt,),
        in_specs=[blk(cur, 0), blk(nxt, 0), blk(cur, ca), blk(cur, cb), blk(prv, ca), blk(prv, cb),
                  pl.BlockSpec((CONV_W, D), lambda i: (0, 0)), pl.BlockSpec(memory_space=pl.ANY)],
        out_specs=[pl.BlockSpec((tt, 2 * D), lambda i: (i, 3)), pl.BlockSpec((HALO, D), lambda i: (0, 0))],
        out_shape=[jax.ShapeDtypeStruct((T, INW), BF16), jax.ShapeDtypeStruct((HALO, D), F32)],
        input_output_aliases={7: 0},
        scratch_shapes=[pltpu.VMEM((tt + HALO + CONV_PAD, D), F32), pltpu.VMEM((tt + HALO + CONV_PAD, D), F32),
                        pltpu.VMEM((SUB - 1, tt + HALO, D), F32), pltpu.VMEM((SUB - 1, tt + HALO, D), F32)],
        compiler_params=_cp("arbitrary"))(du1, du1, proj, proj, proj, proj, conv_k, dproj)


def _local_step(x, target, wts, sp, pos_c, pos_sc, adam):
    T = x.shape[0]
    cos_t, sin_t = _rope_tables(T)
    dm_t, xi_t, zeta_t, cds = _decay_tables()
    wa = lambda key: wts[LOC[key][0]]
    wo = lambda key: LOC[key][1]
    _ORDER.active, _ORDER.token = True, None

    xb, xt = _cast_t(x)
    a1, b1, h1 = _ffn_up(xb, wa("g1"), wo("g1"), wo("u1"), "ffn1_up")
    z1, x1, x1b, x1t = _proj_ln(h1, wa("d1"), wo("d1"), x, sp["ln1_g"], sp["ln1_b"], 0.5, "ffn1_down_ln")
    proj = _inproj(x1b, wa("w_in"), wo("w_in"), sp["b_in"], cos_t, sin_t)
    r, ret_in, states = _retention_fwd(proj, sp["ret_gn_g"], dm_t, xi_t, zeta_t, cds)
    u1, u3 = _conv_fwd(proj, sp["conv_k"], sp["conv_b"], sp["conv_ln_g"], sp["conv_ln_b"])
    ro, co, merged = _merge(ret_in, u3, proj, wa("w_ret_o"), wo("w_ret_o"), wo("w_conv_o"))
    z2, x2, x2b, x2t = _proj_ln(merged, wa("w_out"), wo("w_out"), x1, sp["ln2_g"], sp["ln2_b"], 1.0, "out_proj_ln")
    a2, b2, h2 = _ffn_up(x2b, wa("g2"), wo("g2"), wo("u2"), "ffn2_up")
    (z3,) = _proj_ln(h2, wa("d2"), wo("d2"), x2, sp["ln3_g"], sp["ln3_b"], 0.5, "ffn2_down", want_b=False)

    sg = {}
    rs = {}
    loss, df2b, df2t, dz3, sg["ln3_g"], sg["ln3_b"] = _loss_ln_bwd(z3, sp["ln3_g"], sp["ln3_b"], target, 0.5)
    da2, db2 = _ffn_bwd_h(df2b, wa("d2"), wo("d2"), a2, b2, "ffn2_bwd_h")
    g4 = _wgrad(df2t, h2, "d2", "wgrad_d2")
    g4 = _wgrad(x2t, da2, "g2", "wgrad_g2", g4)
    g4 = _wgrad(x2t, db2, "u2", "wgrad_u2", g4)
    rs[4] = _ReduceScatter(g4, 4, pos_c, pos_sc)
    dmb, dmt, dz2, sg["ln2_g"], sg["ln2_b"] = _dx_bwd(
        [da2, db2], [wo("g2"), wo("u2")], wa("g2"), dz3, "ffn2_dx_ln", ln=(z2, sp["ln2_g"], 1.0))
    rs[4].phase2()
    g3 = _wgrad(dmt, merged, "w_out", "wgrad_out")
    dro, drot, dco, dcot, dproj = _merge_bwd(dmb, wa("w_out"), wo("w_out"), proj, ro, co)
    g3 = _wgrad(drot, ret_in, "w_ret_o", "wgrad_ret_o", g3)
    g3 = _wgrad(dcot, u3, "w_conv_o", "wgrad_conv_o", g3)
    rs[3] = _ReduceScatter(g3, 3, pos_c, pos_sc)
    dr, sg["ret_gn_g"], dproj = _reto_bwd(dro, wa("w_ret_o"), wo("w_ret_o"), r, proj, sp["ret_gn_g"], dproj)
    rs[4].phase3()
    rs[3].phase2()
    dproj = _retention_bwd(proj, dr, states, cos_t, sin_t, dm_t, xi_t, zeta_t, cds, dproj)
    du1, sg["conv_ln_g"], sg["conv_ln_b"], sg["conv_b"] = _convo_bwd(
        dco, wa("w_conv_o"), wo("w_conv_o"), u1, sp["conv_ln_g"], sp["conv_ln_b"])
    dproj, dck = _conv_bwd(du1, proj, sp["conv_k"], dproj)
    sg["conv_k"] = dck[:CONV_W]
    adam(4, rs[4].result())
    rs[3].phase3()
    g2, sg["b_in"] = _wgrad(x1t, dproj, "w_in", "wgrad_in", colsum=True)
    rs[2] = _ReduceScatter(g2, 2, pos_c, pos_sc)
    dx_part = _dx_partial(dproj, wa("w_in"), (0, 1), "mixer_dx_part")
    df1b, df1t, dz1, sg["ln1_g"], sg["ln1_b"] = _dx_bwd(
        [dproj], [wo("w_in")], wa("w_in"), dz2, "mixer_dx_ln", ln=(z1, sp["ln1_g"], 0.5), chips=(2, 3),
        partial=dx_part)
    adam(3, rs[3].result())
    rs[2].phase2()
    shapes = {n: sg[n].shape for n in SMALL + ["conv_k"]}
    small_parts = _all_gather_small(_pack_small(sg, loss, SMALL_ROWS), "gather_small")
    da1, db1 = _ffn_bwd_h(df1b, wa("d1"), wo("d1"), a1, b1, "ffn1_bwd_h")
    small_sum = _sum_devices(small_parts, "sum_small")
    g1 = _wgrad(df1t, h1, "d1", "wgrad_d1")
    rs[1] = _ReduceScatter(g1, 1, pos_c, pos_sc)
    g0 = _wgrad(xt, da1, "g1", "wgrad_g1")
    g0 = _wgrad(xt, db1, "u1", "wgrad_u1", g0)
    rs[0] = _ReduceScatter(g0, 0, pos_c, pos_sc)
    rs[2].phase3()
    rs[1].phase2()
    rs[0].phase2()
    (grad_x,) = _dx_bwd([da1, db1], [wo("g1"), wo("u1")], wa("g1"), dz1, "ffn1_dx")
    adam(2, rs[2].result())
    rs[1].phase3()
    rs[0].phase3()
    adam(1, rs[1].result())
    adam(0, rs[0].result())
    _ORDER.active = False
    return grad_x, small_sum, shapes


MESH = pl.DeviceIdType.MESH
ANY = pl.BlockSpec(memory_space=pl.ANY)
HALF = D // 2


def _place():
    x, y, c = lax.axis_index("x"), lax.axis_index("y"), lax.axis_index("c")
    chips = [(1 - x, y), (x, 1 - y), (1 - x, 1 - y)]
    return x, y, c, chips


GATHER_ID = 1


def _gather_weights(wloc, name):
    w_ref = jax.new_ref(wloc, memory_space=pltpu.MemorySpace.HBM)
    o_ref = jax.empty_ref(jax.ShapeDtypeStruct((NCHIP, 2, HALF, wloc.shape[-1]), BF16),
                          memory_space=pltpu.MemorySpace.HBM)
    dma = pltpu.SemaphoreType.DMA

    @pl.kernel(mesh=plsc.ScalarSubcoreMesh(axis_name="sc", num_cores=1), name=name,
               scratch_types=(dma(()), dma((2,)), dma((2,)), dma((3,)), dma((3,)), dma(()), dma(())),
               compiler_params=pltpu.CompilerParams(collective_id=GATHER_ID))
    def launch(lsem, s1, r1, s2, r2, s3, r3):
        x, y, c, _ = _place()
        me = 2 * x + y
        sib = (x, y, 1 - c)
        x_nbr, y_nbr = (1 - x, y, c), (x, 1 - y, c)
        x_chip, y_chip, d_chip = 2 * (1 - x) + y, 2 * x + (1 - y), 2 * (1 - x) + (1 - y)
        _handshake([sib, x_nbr, y_nbr])
        mine = pltpu.make_async_copy(w_ref, o_ref.at[me], lsem)
        mine.start()

        def rc(src, dst, ss, rs, dev):
            return pltpu.make_async_remote_copy(src_ref=src, dst_ref=dst, send_sem=ss, recv_sem=rs,
                                                device_id=dev, device_id_type=MESH)

        first = [rc(w_ref.at[c], o_ref.at[me, c], s1.at[0], r1.at[0], x_nbr),
                 rc(w_ref.at[c], o_ref.at[me, c], s1.at[1], r1.at[1], y_nbr)]
        for cp in first:
            cp.start()
        on_chip = c * x_chip + (1 - c) * y_chip
        other_chip = c * y_chip + (1 - c) * x_chip
        on_to = (c * x + (1 - c) * (1 - x), c * (1 - y) + (1 - c) * y, c)
        slot = o_ref.at[on_chip, c]
        rc(slot, slot, s1.at[1 - c], r1.at[1 - c], sib).wait_recv()
        onward = rc(slot, slot, s3, r3, on_to)
        onward.start()
        passed = [rc(slot, slot, s2.at[0], r2.at[0], sib)]
        passed[0].start()
        slot = o_ref.at[other_chip, c]
        rc(slot, slot, s1.at[c], r1.at[c], sib).wait_recv()
        passed.append(rc(slot, slot, s2.at[1], r2.at[1], sib))
        passed[1].start()
        slot = o_ref.at[d_chip, c]
        rc(slot, slot, s3, r3, sib).wait_recv()
        passed.append(rc(slot, slot, s2.at[2], r2.at[2], sib))
        passed[2].start()
        for j, chip in enumerate([other_chip, on_chip, d_chip]):
            slot = o_ref.at[chip, 1 - c]
            rc(slot, slot, s2.at[j], r2.at[j], sib).wait_recv()
        for cp in first + [onward] + passed:
            cp.wait_send()
        mine.wait()

    launch()
    return o_ref[...]


PAIR_ID = 2
CHIP_ID = 3
HBM = pltpu.MemorySpace.HBM


def _sequencer(name, collective_id, n_sems):
    dma = pltpu.SemaphoreType.DMA
    return pl.kernel(mesh=plsc.ScalarSubcoreMesh(axis_name="sc", num_cores=1), name=name,
                     scratch_types=(dma((n_sems,)), dma((n_sems,))),
                     compiler_params=pltpu.CompilerParams(collective_id=collective_id))


def _handshake(peers):
    barrier = pltpu.get_barrier_semaphore()
    for peer in peers:
        pl.semaphore_signal(barrier, inc=1, device_id=peer, device_id_type=MESH)
    pl.semaphore_wait(barrier, len(peers))


def _pair_exchange(g5, name):
    _, _, hr, cols = g5.shape
    g_ref = jax.new_ref(g5, memory_space=HBM)
    o_ref = jax.empty_ref(jax.ShapeDtypeStruct((NCHIP, hr, cols), g5.dtype), memory_space=HBM)

    @_sequencer(name, PAIR_ID, NCHIP)
    def launch(ss, rs):
        x, y, c, _ = _place()
        sib = (x, y, 1 - c)
        _handshake([sib])
        cps = [pltpu.make_async_remote_copy(src_ref=g_ref.at[j, 1 - c], dst_ref=o_ref.at[j], send_sem=ss.at[j],
                                            recv_sem=rs.at[j], device_id=sib, device_id_type=MESH)
               for j in range(NCHIP)]
        for cp in cps:
            cp.start()
        for cp in cps:
            cp.wait()

    launch()
    return o_ref[...]


RS_TR = 128


def _pair_sum(pos, g5, got, name):
    _, _, hr, cols = g5.shape

    def body(pos_ref, g_ref, r_ref, o_ref):
        o_ref[...] = (g_ref[...].astype(F32) + r_ref[...].astype(F32)).astype(BF16)

    return _pcall(
        body, name=name, scalar_prefetch=1, grid=(NCHIP, hr // RS_TR),
        in_specs=[pl.BlockSpec((None, None, RS_TR, cols), lambda j, i, p: (j, p[0], i, 0)),
                  pl.BlockSpec((None, RS_TR, cols), lambda j, i, p: (j, i, 0))],
        out_specs=pl.BlockSpec((None, RS_TR, cols), lambda j, i, p: (j, i, 0)),
        out_shape=jax.ShapeDtypeStruct((NCHIP, hr, cols), BF16),
        compiler_params=_cp("parallel", "parallel"))(pos, g5, got)


def _chip_exchange(pb, name):
    _, hr, cols = pb.shape
    p_ref = jax.new_ref(pb, memory_space=HBM)
    o_ref = jax.empty_ref(jax.ShapeDtypeStruct((3, hr, cols), BF16), memory_space=HBM)

    @_sequencer(name, CHIP_ID, 3)
    def launch(ss, rs):
        x, y, c, chips = _place()
        _handshake([(px, py, c) for px, py in chips])
        cps = [pltpu.make_async_remote_copy(src_ref=p_ref.at[2 * px + py], dst_ref=o_ref.at[j], send_sem=ss.at[j],
                                            recv_sem=rs.at[j], device_id=(px, py, c), device_id_type=MESH)
               for j, (px, py) in enumerate(chips)]
        for cp in cps:
            cp.start()
        for cp in cps:
            cp.wait()

    launch()
    return o_ref[...]


def _chip_sum(pos, g5, got, peers, name):
    _, _, hr, cols = g5.shape

    def body(pos_ref, g_ref, r_ref, p_ref, o_ref, t_ref):
        acc = g_ref[...].astype(F32) + r_ref[...].astype(F32)
        for j in range(3):
            acc += p_ref[j].astype(F32)
        o_ref[...] = acc
        t_ref[...] = jnp.zeros_like(t_ref)

    return _pcall(
        body, name=name, scalar_prefetch=1, grid=(hr // RS_TR,),
        in_specs=[pl.BlockSpec((None, None, RS_TR, cols), lambda i, p: (p[0], p[1], i, 0)),
                  pl.BlockSpec((None, RS_TR, cols), lambda i, p: (p[0], i, 0)),
                  pl.BlockSpec((3, RS_TR, cols), lambda i, p: (0, i, 0))],
        out_specs=[pl.BlockSpec((None, RS_TR, cols), lambda i, p: (p[1], i, 0)),
                   pl.BlockSpec((8, 128), lambda i, p: (0, 0))],
        out_shape=[jax.ShapeDtypeStruct((2, hr, cols), F32), jax.ShapeDtypeStruct((8, 128), F32)],
        compiler_params=_cp("arbitrary"))(pos, g5, got, peers)


def _pair_share(gsum, name):
    g_ref = jax.new_ref(gsum, memory_space=HBM)

    @_sequencer(name, PAIR_ID, 1)
    def launch(ss, rs):
        x, y, c, _ = _place()
        sib = (x, y, 1 - c)
        _handshake([sib])
        cp = pltpu.make_async_remote_copy(src_ref=g_ref.at[c], dst_ref=g_ref.at[c], send_sem=ss.at[0],
                                          recv_sem=rs.at[0], device_id=sib, device_id_type=MESH)
        cp.start()
        cp.wait_send()
        pltpu.make_async_remote_copy(src_ref=g_ref.at[1 - c], dst_ref=g_ref.at[1 - c], send_sem=ss.at[0],
                                     recv_sem=rs.at[0], device_id=sib, device_id_type=MESH).wait_recv()

    launch()
    return g_ref[...]


class _ReduceScatter:
    def __init__(self, g_arr, gi, pos_c, pos_sc):
        _, rows, cols = g_arr.shape
        self.g5 = g_arr.reshape(NCHIP, 2, rows // 2, cols)
        self.gi, self.pos_c, self.pos_sc = gi, pos_c, pos_sc
        self.got = _pair_exchange(self.g5, f"pair_exchange_{gi}")

    def phase2(self):
        pb = _pair_sum(self.pos_c, self.g5, self.got, f"pair_sum_{self.gi}")
        self.peers = _chip_exchange(pb, f"chip_exchange_{self.gi}")

    def phase3(self):
        gsum, _ = _chip_sum(self.pos_sc, self.g5, self.got, self.peers, f"chip_sum_{self.gi}")
        self.full = _pair_share(gsum, f"pair_share_{self.gi}")

    def result(self):
        _, hr, cols = self.full.shape
        return self.full.reshape(2 * hr, cols)


SMALL_ROWS = 56


ALL_ID = 4


def _all_gather_small(vec, name):
    v_ref = jax.new_ref(vec, memory_space=HBM)
    o_ref = jax.empty_ref(jax.ShapeDtypeStruct((8, SMALL_ROWS, D), F32), memory_space=HBM)

    @_sequencer(name, ALL_ID, 8)
    def launch(ss, rs):
        x, y, c, _ = _place()
        me = 4 * x + 2 * y + c
        flip = lambda v, bit: 1 - v if bit else v
        peers = [(flip(x, m >> 2), flip(y, (m >> 1) & 1), flip(c, m & 1)) for m in range(1, 8)]
        _handshake(peers)
        mine = pltpu.make_async_copy(v_ref, o_ref.at[me], ss.at[7])
        mine.start()
        cps = [pltpu.make_async_remote_copy(src_ref=v_ref, dst_ref=o_ref.at[me], send_sem=ss.at[k],
                                            recv_sem=rs.at[k], device_id=peer, device_id_type=MESH)
               for k, peer in enumerate(peers)]
        for cp in cps:
            cp.start()
        for cp in cps:
            cp.wait()
        mine.wait()

    launch()
    return o_ref[...]


def _sum_devices(parts, name):
    def body(p_ref, o_ref):
        acc = p_ref[0]
        for d in range(1, 8):
            acc += p_ref[d]
        o_ref[...] = acc

    return _pcall(
        body, name=name, grid=(SMALL_ROWS // 8,),
        in_specs=[pl.BlockSpec((8, 8, D), lambda i: (0, i, 0))],
        out_specs=pl.BlockSpec((8, D), lambda i: (i, 0)),
        out_shape=jax.ShapeDtypeStruct((SMALL_ROWS, D), F32),
        compiler_params=_cp("parallel"))(parts)


def _adamw_math(w, g, m, v):
    m2 = ADAM_B1 * m + (1.0 - ADAM_B1) * g
    v2 = ADAM_B2 * v + (1.0 - ADAM_B2) * (g * g)
    m_hat = m2 / (1.0 - ADAM_B1 ** ADAM_STEP)
    v_hat = v2 / (1.0 - ADAM_B2 ** ADAM_STEP)
    delta = -ADAM_LR * (m_hat / (jnp.sqrt(v_hat) + ADAM_EPS) + ADAM_WD * w)
    return delta, m2, v2


def _adamw(w, g, m, v, name, g_block=None):
    R, C = w.shape
    tr = R
    gw_hint = C if g_block is None else g_block[0]
    for cand in (512, 352, 256, 176, 128, 64, 32, 16, 8):
        if R % cand == 0 and cand * max(C, gw_hint) * 4 <= (2 << 20):
            tr = cand
            break
    gw, gi = (C, 0) if g_block is None else g_block

    def body(w_ref, g_ref, m_ref, v_ref, go_ref, d_ref, mo_ref, vo_ref):
        g = g_ref[:, 0:C]
        d, m2, v2 = _adamw_math(w_ref[...], g, m_ref[...], v_ref[...])
        go_ref[...] = g
        d_ref[...] = d
        mo_ref[...] = m2
        vo_ref[...] = v2

    spec = pl.BlockSpec((tr, C), lambda i: (i, 0))
    return _pcall(
        body, name=name, grid=(R // tr,),
        in_specs=[spec, pl.BlockSpec((tr, gw), lambda i: (i, gi)), spec, spec],
        out_specs=[spec] * 4, out_shape=[jax.ShapeDtypeStruct((R, C), F32)] * 4,
        compiler_params=_cp("parallel"))(w, g, m, v)


BIG = ["ffn1_w_gate", "ffn1_w_up", "ffn1_w_down", "w_in", "w_ret_o", "w_conv_o", "w_out",
       "ffn2_w_gate", "ffn2_w_up", "ffn2_w_down"]
SLAB = {"ffn1_w_gate": "g1", "ffn1_w_up": "u1", "ffn1_w_down": "d1", "w_in": "w_in", "w_ret_o": "w_ret_o",
        "w_conv_o": "w_conv_o", "w_out": "w_out", "ffn2_w_gate": "g2", "ffn2_w_up": "u2", "ffn2_w_down": "d2"}
TRANSPOSED = {"ffn1_w_down", "ffn2_w_down", "w_ret_o", "w_conv_o", "w_out"}
MINOR_ROWS = {"ffn1_w_gate", "ffn1_w_up", "ffn2_w_gate", "ffn2_w_up"}
SMALL = ["ln1_g", "ln1_b", "ln2_g", "ln2_b", "ln3_g", "ln3_b", "conv_ln_g", "conv_ln_b", "conv_b",
         "ret_gn_g", "b_in"]
ORDER = ["ffn1_w_gate", "ffn1_w_up", "ffn1_w_down", "ln1_g", "ln1_b", "w_in", "b_in", "ret_gn_g", "conv_k",
         "conv_b", "conv_ln_g", "conv_ln_b", "w_ret_o", "w_conv_o", "w_out", "ln2_g", "ln2_b",
         "ffn2_w_gate", "ffn2_w_up", "ffn2_w_down", "ln3_g", "ln3_b"]


def _slab_width(name):
    return WIDTH[SLAB[name]]


def _pack_group(weights, keys):
    by_key = {SLAB[n]: n for n in BIG}
    parts = []
    for key in keys:
        w = weights[by_key[key]]
        w = w.T if by_key[key] in TRANSPOSED else w
        parts.append(jnp.pad(w, ((0, 0), (0, WIDTH[key] - w.shape[1]))))
    return jnp.concatenate(parts, axis=1).astype(BF16)


def _pack_small(vals, loss, rows):
    flat = jnp.concatenate([vals[n].reshape(-1) for n in SMALL] + [vals["conv_k"].reshape(-1), loss.reshape(-1)])
    return jnp.pad(flat, (0, rows * D - flat.shape[0])).reshape(rows, D)


def _unpack_small(arr, shapes):
    flat = arr.reshape(-1)
    out, pos = {}, 0
    for n in SMALL + ["conv_k"]:
        size = int(np.prod(shapes[n]))
        out[n] = flat[pos:pos + size].reshape(shapes[n])
        pos += size
    return out, flat[pos]


def kernel(x, ffn1_w_gate, ffn1_w_up, ffn1_w_down, ln1_g, ln1_b, w_in, b_in, ret_gn_g, conv_k, conv_b, conv_ln_g, conv_ln_b, w_ret_o, w_conv_o, w_out, ln2_g, ln2_b, ffn2_w_gate, ffn2_w_up, ffn2_w_down, ln3_g, ln3_b, loss_target, m_ffn1_w_gate, m_ffn1_w_up, m_ffn1_w_down, m_ln1_g, m_ln1_b, m_w_in, m_b_in, m_ret_gn_g, m_conv_k, m_conv_b, m_conv_ln_g, m_conv_ln_b, m_w_ret_o, m_w_conv_o, m_w_out, m_ln2_g, m_ln2_b, m_ffn2_w_gate, m_ffn2_w_up, m_ffn2_w_down, m_ln3_g, m_ln3_b, v_ffn1_w_gate, v_ffn1_w_up, v_ffn1_w_down, v_ln1_g, v_ln1_b, v_w_in, v_b_in, v_ret_gn_g, v_conv_k, v_conv_b, v_conv_ln_g, v_conv_ln_b, v_w_ret_o, v_w_conv_o, v_w_out, v_ln2_g, v_ln2_b, v_ffn2_w_gate, v_ffn2_w_up, v_ffn2_w_down, v_ln3_g, v_ln3_b):
    args = dict(locals())
    w = {n: args[n] for n in ORDER}
    m = {n: args["m_" + n] for n in ORDER}
    v = {n: args["v_" + n] for n in ORDER}
    xi, yi, ci = lax.axis_index("x"), lax.axis_index("y"), lax.axis_index("c")
    chip = 2 * xi + yi

    shards = {n: w[n][0] for n in BIG}
    wts = []
    for gi, keys in enumerate(GROUPS):
        slab = _pack_group(shards, keys)
        cols = slab.shape[1]
        wts.append(_gather_weights(slab.reshape(2, HALF, cols), f"gather_{gi}").reshape(NCHIP, D, cols))

    sp = {n: w[n] for n in SMALL}
    sp["conv_k"] = None
    kfull_shape = (CONV_W, D)
    kpad = jnp.zeros(kfull_shape, F32)
    kpad = lax.dynamic_update_slice(kpad, w["conv_k"][0, :, 0, :] * jnp.where(ci == 0, 1.0, 0.0), (0, chip * (D // NCHIP)))
    kvec = jnp.pad(kpad.reshape(-1), (0, SMALL_ROWS * D - CONV_W * D)).reshape(SMALL_ROWS, D)
    kall = _sum_devices(_all_gather_small(kvec, "gather_conv_k"), "sum_conv_k")
    sp["conv_k"] = kall.reshape(-1)[:CONV_W * D].reshape(kfull_shape)
    pos_c = jnp.reshape(ci, (1,)).astype(jnp.int32)
    pos_sc = jnp.stack([chip, ci]).astype(jnp.int32)
    out = {}

    def adam(gi, slab):
        for n in BIG:
            (g_of, off), width = LOC[SLAB[n]], _slab_width(n)
            if g_of != gi:
                continue
            w2 = w[n][0]
            if n in TRANSPOSED:
                res = _adamw(w2, slab[:, off:off + w2.shape[0]].T, m[n][0], v[n][0], "adamw_" + n)
                out[n] = [r[None] for r in res]
            elif n in MINOR_ROWS:
                res = _adamw(w2.T, slab[:, off:off + w2.shape[1]].T, m[n][0].T, v[n][0].T, "adamw_" + n)
                out[n] = [r.T[None] for r in res]
            else:
                res = _adamw(w2, slab, m[n][0], v[n][0], "adamw_" + n, g_block=(width, off // width))
                out[n] = [r[None] for r in res]

    grad_x, small_sum, shapes = _local_step(x[0], loss_target[0], wts, sp, pos_c, pos_sc, adam)
    small, total = _unpack_small(small_sum, shapes)

    for n in SMALL:
        res = _adamw(w[n], small[n], m[n], v[n], "adamw_" + n)
        out[n] = list(res)
    gk = lax.dynamic_slice(small["conv_k"], (0, chip * (D // NCHIP)), (CONV_W, D // NCHIP))
    res = _adamw(w["conv_k"][0, :, 0, :], gk, m["conv_k"][0, :, 0, :], v["conv_k"][0, :, 0, :], "adamw_conv_k")
    out["conv_k"] = [r[None, :, None, :] for r in res]

    grads = [out[n][0] for n in ORDER]
    deltas = [out[n][1] for n in ORDER]
    new_m = [out[n][2] for n in ORDER]
    new_v = [out[n][3] for n in ORDER]
    return (total, grad_x[None], *grads, *deltas, *new_m, *new_v)
```

```python
import dataclasses
import functools

import numpy as np
import jax
import jax.numpy as jnp
from jax import lax
from jax.experimental import pallas as pl
from jax.experimental.pallas import tpu as pltpu
from jax.experimental.pallas import tpu_sc as plsc

F32 = jnp.float32
BF16 = jnp.bfloat16

D = 1024
FS = 704
FSP = 768
FP = 4 * FSP
H = 8
DK = 128
DV = 256
CH = 128
VW = H * DV
INW = 10240
INS = INW // 4
CONV_W = 31
HALO = 32
EPS = 1e-5
ALPHA = 2.0 ** 0.25
ROPE_BASE = 10000.0
NCHIP = 4

ADAM_LR, ADAM_B1, ADAM_B2, ADAM_EPS, ADAM_WD, ADAM_STEP = 0.001, 0.9, 0.999, 1e-08, 0.01, 10

OFF = {"w_in": 0, "w_ret_o": 2560, "g1": 3072, "u1": 3840, "d1": 4608,
       "g2": 5376, "u2": 6144, "d2": 6912, "w_conv_o": 7680, "w_out": 7936}
WCOLS = 8192
WIDTH = {"w_in": INS, "w_ret_o": VW // NCHIP, "w_conv_o": D // NCHIP, "w_out": D // NCHIP,
         "g1": FSP, "u1": FSP, "d1": FSP, "g2": FSP, "u2": FSP, "d2": FSP}
GROUPS = (("g1", "u1"), ("d1",), ("w_in",), ("w_ret_o", "w_conv_o", "w_out"), ("g2", "u2", "d2"))
LOC = {}
for _gi, _keys in enumerate(GROUPS):
    _off = 0
    for _k in _keys:
        LOC[_k] = (_gi, _off)
        _off += WIDTH[_k]
GCOLS = [sum(WIDTH[k] for k in keys) for keys in GROUPS]
VMEM_LIMIT = 56 << 20


def _cp(*sem, **kw):
    return pltpu.CompilerParams(dimension_semantics=sem, vmem_limit_bytes=VMEM_LIMIT, **kw)


class _ProgramOrder:
    def __init__(self):
        self.active = False
        self.token = None


_ORDER = _ProgramOrder()


def _pcall(body, *, in_specs, scalar_prefetch=0, **kw):
    def call(*args):
        dep = _ORDER.token if _ORDER.active else None
        specs, fn = list(in_specs), body
        if dep is not None:
            n = len(args)

            def fn(*refs):
                return body(*refs[:n], *refs[n + 1:])

            specs.append(pl.BlockSpec(memory_space=pl.ANY))
            args = (*args, dep)
        params = dict(kw)
        if scalar_prefetch:
            params["grid_spec"] = pltpu.PrefetchScalarGridSpec(
                num_scalar_prefetch=scalar_prefetch, grid=params.pop("grid"), in_specs=specs,
                out_specs=params.pop("out_specs"))
        else:
            params["in_specs"] = specs
        out = pl.pallas_call(fn, **params)(*args)
        if _ORDER.active:
            _ORDER.token = jax.tree.leaves(out)[-1]
        return out

    return call


def _resident(shape, col_block):
    lead = (0,) * (len(shape) - 1)
    return pl.BlockSpec(shape, lambda *_: (*lead, col_block), pipeline_mode=pl.Buffered(1))


def _sig(x):
    return 1.0 / (1.0 + jnp.exp(-x))


def _dot(a, b):
    return jnp.dot(a, b, preferred_element_type=F32)


def _dot_nt(a, b):
    return lax.dot_general(a, b, (((1,), (1,)), ((), ())), preferred_element_type=F32)


def _ln_fwd(z, g, b):
    mu = jnp.mean(z, axis=-1, keepdims=True)
    xc = z - mu
    var = jnp.mean(xc * xc, axis=-1, keepdims=True)
    rstd = lax.rsqrt(var + EPS)
    xh = xc * rstd
    return xh * g + b, xh, rstd


def _ln_bwd(dy, xh, rstd, g):
    dxh = dy * g
    m1 = jnp.mean(dxh, axis=-1, keepdims=True)
    m2 = jnp.mean(dxh * xh, axis=-1, keepdims=True)
    return rstd * (dxh - m1 - xh * m2)


def _colsum(x):
    return jnp.sum(x, axis=0, keepdims=True)


def _zero_first(first, *refs):
    @pl.when(first)
    def _():
        for ref in refs:
            ref[...] = jnp.zeros_like(ref)


def _rope_tables(T):
    half = DK // 2
    freqs = ROPE_BASE ** (-np.arange(half, dtype=np.float32) / half)
    ang = (np.arange(T, dtype=np.float32)[:, None] * freqs[None, :]).astype(np.float32)
    cos, sin = np.cos(ang), np.sin(ang)
    return (jnp.asarray(np.concatenate([cos, cos], 1), F32),
            jnp.asarray(np.concatenate([-sin, sin], 1), F32))


def _decay_tables():
    h = np.arange(H, dtype=np.float64)
    log_g = np.log(1.0 - np.exp2(-5.0 - h))
    idx = np.arange(CH, dtype=np.float64)
    diff = idx[:, None] - idx[None, :]
    dm = np.where(diff[None] >= 0, np.exp(np.maximum(diff, 0.0)[None] * log_g[:, None, None]), 0.0)
    xi = np.exp((idx[None, :] + 1.0) * log_g[:, None])
    zeta = np.exp((CH - 1.0 - idx)[None, :] * log_g[:, None])
    cd = np.exp(CH * log_g)
    xi_t = np.broadcast_to(xi[:, :, None], (H, CH, DV))
    zeta_t = np.broadcast_to(zeta[:, :, None], (H, CH, DK))
    return (jnp.asarray(dm, F32), jnp.asarray(xi_t, F32), jnp.asarray(zeta_t, F32),
            [float(v) for v in cd])


def _cast_t(x):
    T = x.shape[0]
    tm = min(T, 512)

    def body(x_ref, xb_ref, xt_ref):
        v = x_ref[...]
        xb_ref[...] = v.astype(BF16)
        xt_ref[...] = v.T.astype(BF16)

    return _pcall(
        body, name="cast_t", grid=(T // tm,),
        in_specs=[pl.BlockSpec((tm, D), lambda i: (i, 0))],
        out_specs=[pl.BlockSpec((tm, D), lambda i: (i, 0)), pl.BlockSpec((D, tm), lambda i: (0, i))],
        out_shape=[jax.ShapeDtypeStruct((T, D), BF16), jax.ShapeDtypeStruct((D, T), BF16)],
        compiler_params=_cp("parallel"))(x)


def _ffn_up(xb, wall, og, ou, name):
    T = xb.shape[0]
    tm = min(T, 512)
    assert ou == og + FSP

    def body(x_ref, w_ref, a_ref, b_ref, h_ref):
        x = x_ref[...]
        for s in range(NCHIP):
            cols = slice(s * FSP, (s + 1) * FSP)
            a = _dot(x, w_ref[s, :, 0:FSP])
            b = _dot(x, w_ref[s, :, FSP:2 * FSP])
            a_ref[:, cols] = a.astype(BF16)
            b_ref[:, cols] = b.astype(BF16)
            h_ref[:, cols] = (a * _sig(a) * b).astype(BF16)

    ospec = pl.BlockSpec((tm, FP), lambda i: (i, 0))
    return _pcall(
        body, name=name, grid=(T // tm,),
        in_specs=[pl.BlockSpec((tm, D), lambda i: (i, 0)), _resident((NCHIP, D, 2 * FSP), og // (2 * FSP))],
        out_specs=[ospec] * 3, out_shape=[jax.ShapeDtypeStruct((T, FP), BF16)] * 3,
        compiler_params=_cp("parallel"))(xb, wall)


def _proj_ln(hb, wall, off, res, g, b, coef, name, want_b=True):
    T, K = hb.shape
    ks = K // NCHIP
    tm = min(T, 512)
    sub = min(tm, 256)

    def body(h_ref, w_ref, r_ref, g_ref, b_ref, z_ref, *rest):
        for r0 in range(0, tm, sub):
            r = slice(r0, r0 + sub)
            acc = _dot_nt(h_ref[r, 0:ks], w_ref[0])
            for s in range(1, NCHIP):
                acc += _dot_nt(h_ref[r, s * ks:(s + 1) * ks], w_ref[s])
            z = ALPHA * r_ref[r, :] + coef * acc
            z_ref[r, :] = z
            if want_b:
                y, _, _ = _ln_fwd(z, g_ref[...], b_ref[...])
                y_ref, yb_ref, yt_ref = rest
                y_ref[r, :] = y
                yb_ref[r, :] = y.astype(BF16)
                yt_ref[:, r] = y.T.astype(BF16)

    row = pl.BlockSpec((tm, D), lambda i: (i, 0))
    vec = pl.BlockSpec((1, D), lambda i: (0, 0))
    out_specs = [row]
    out_shape = [jax.ShapeDtypeStruct((T, D), F32)]
    if want_b:
        out_specs += [row, row, pl.BlockSpec((D, tm), lambda i: (0, i))]
        out_shape += [jax.ShapeDtypeStruct((T, D), F32), jax.ShapeDtypeStruct((T, D), BF16),
                      jax.ShapeDtypeStruct((D, T), BF16)]
    return _pcall(
        body, name=name, grid=(T // tm,),
        in_specs=[pl.BlockSpec((tm, K), lambda i: (i, 0)),
                  _resident((NCHIP, D, ks), off // ks), row, vec, vec],
        out_specs=out_specs, out_shape=out_shape,
        compiler_params=_cp("parallel"))(hb, wall, res, g, b)


def _inproj(xb, wall, off, b_in, cos_t, sin_t):
    T = xb.shape[0]
    tm, tn = min(T, 512), 512
    assert off == 0

    def body(x_ref, w_ref, bias_ref, cos_ref, sin_ref, o_ref):
        x = x_ref[...]
        c = cos_ref[...]
        s = sin_ref[...]
        for n0 in range(0, INW, tn):
            chip, c0 = divmod(n0, INS)
            acc = _dot(x, w_ref[chip, :, c0:c0 + tn]) + bias_ref[:, n0:n0 + tn]
            if n0 >= 2 * D:
                o_ref[:, n0:n0 + tn] = acc.astype(BF16)
                continue
            scale = DK ** -0.5 if n0 < D else 1.0
            for hh in range(tn // DK):
                xh = acc[:, hh * DK:(hh + 1) * DK]
                o = (xh * c + pltpu.roll(xh, DK // 2, 1) * s) * scale
                o_ref[:, n0 + hh * DK:n0 + (hh + 1) * DK] = o.astype(BF16)

    return _pcall(
        body, name="inproj", grid=(T // tm,),
        in_specs=[pl.BlockSpec((tm, D), lambda i: (i, 0)),
                  _resident((NCHIP, D, INS), 0),
                  pl.BlockSpec((1, INW), lambda i: (0, 0)),
                  pl.BlockSpec((tm, DK), lambda i: (i, 0)),
                  pl.BlockSpec((tm, DK), lambda i: (i, 0))],
        out_specs=pl.BlockSpec((tm, INW), lambda i: (i, 0)),
        out_shape=jax.ShapeDtypeStruct((T, INW), BF16),
        compiler_params=_cp("parallel"))(xb, wall, b_in, cos_t, sin_t)


RET_CPS = 2


def _retention_fwd(proj, gn_g, dm_t, xi_t, zeta_t, cds):
    T = proj.shape[0]
    n = T // CH
    tr = RET_CPS * CH

    def body(q_ref, k_ref, v_ref, g_ref, gn_ref, dm_ref, xi_ref, zt_ref, r_ref, ri_ref, st_ref, state):
        @pl.when(pl.program_id(0) == 0)
        def _():
            state[...] = jnp.zeros_like(state)

        for h in range(H):
            rows = slice(h * DK, (h + 1) * DK)
            cols = slice(h * DV, (h + 1) * DV)
            s_prev = state[rows, :]
            for j in range(RET_CPS):
                t = slice(j * CH, (j + 1) * CH)
                q = q_ref[t, h * DK:(h + 1) * DK]
                k = k_ref[t, h * DK:(h + 1) * DK]
                v = v_ref[t, cols]
                s_b = s_prev.astype(BF16)
                st_ref[j, rows, :] = s_b
                sc = _dot_nt(q, k) * dm_ref[h]
                r = _dot(sc.astype(BF16), v) + _dot(q, s_b) * xi_ref[h]
                kz = k.astype(F32) * zt_ref[h]
                s_prev = cds[h] * s_prev + _dot(kz.T.astype(BF16), v)
                r_ref[t, cols] = r
                mu = jnp.mean(r, axis=-1, keepdims=True)
                xc = r - mu
                var = jnp.mean(xc * xc, axis=-1, keepdims=True)
                y = xc * lax.rsqrt(var + EPS) * gn_ref[:, cols]
                g = g_ref[t, cols].astype(F32)
                ri_ref[t, cols] = (g * _sig(g) * y).astype(BF16)
            state[rows, :] = s_prev

    full3 = lambda shp: pl.BlockSpec(shp, lambda c: (0, 0, 0))
    return _pcall(
        body, name="retention_fwd", grid=(n // RET_CPS,),
        in_specs=[pl.BlockSpec((tr, D), lambda c: (c, 0)),
                  pl.BlockSpec((tr, D), lambda c: (c, 1)),
                  pl.BlockSpec((tr, VW), lambda c: (c, 1)),
                  pl.BlockSpec((tr, VW), lambda c: (c, 2)),
                  pl.BlockSpec((1, VW), lambda c: (0, 0)),
                  full3((H, CH, CH)), full3((H, CH, DV)), full3((H, CH, DK))],
        out_specs=[pl.BlockSpec((tr, VW), lambda c: (c, 0)), pl.BlockSpec((tr, VW), lambda c: (c, 0)),
                   pl.BlockSpec((RET_CPS, H * DK, DV), lambda c: (c, 0, 0))],
        out_shape=[jax.ShapeDtypeStruct((T, VW), F32), jax.ShapeDtypeStruct((T, VW), BF16),
                   jax.ShapeDtypeStruct((n, H * DK, DV), BF16)],
        scratch_shapes=[pltpu.VMEM((H * DK, DV), F32)],
        compiler_params=_cp("arbitrary"))(proj, proj, proj, proj, gn_g, dm_t, xi_t, zeta_t)


CONV_TT = 256
CONV_SB = 64
CONV_CB = 256


SUB = 8
CONV_PAD = 8


def _glu(a_ref, b_ref, rows=slice(None)):
    a = a_ref[rows, :].astype(F32)
    sb = _sig(b_ref[rows, :].astype(F32))
    return a, sb


def _shift_copies(win, sh, rows):
    win[rows:rows + CONV_PAD, :] = jnp.zeros((CONV_PAD, D), F32)
    for b in range(1, SUB):
        sh[b - 1, :, :] = win[b:b + rows, :]


def _tap(win, sh, start, size, cs):
    a, b = divmod(start, SUB)
    src = win if b == 0 else sh.at[b - 1]
    return src[SUB * a:SUB * a + size, cs]


def _conv_fwd(proj, conv_k, conv_b, ln_g, ln_b):
    T = proj.shape[0]
    tt = min(T, CONV_TT)
    ca, cb = 6 * D // D, 7 * D // D

    def body(a_ref, b_ref, pa_ref, pb_ref, k_ref, cb_ref, g_ref, bb_ref, u1_ref, u3_ref, win, sh):
        i = pl.program_id(0)
        a, sb = _glu(a_ref, b_ref)
        win[HALO:tt + HALO, :] = a * sb
        pa, psb = _glu(pa_ref, pb_ref, slice(tt - HALO, tt))
        win[0:HALO, :] = jnp.where(i > 0, pa * psb, 0.0)
        _shift_copies(win, sh, tt + HALO)
        for c0 in range(0, D, CONV_CB):
            cs = slice(c0, c0 + CONV_CB)
            for r0 in range(0, tt, CONV_SB):
                acc = jnp.zeros((CONV_SB, CONV_CB), F32)
                for w in range(CONV_W):
                    st = r0 + HALO - (CONV_W - 1) + w
                    acc += _tap(win, sh, st, CONV_SB, cs) * k_ref[w:w + 1, cs]
                u1_ref[r0:r0 + CONV_SB, cs] = acc + cb_ref[:, cs]
        u2, _, _ = _ln_fwd(u1_ref[...], g_ref[...], bb_ref[...])
        u3_ref[...] = (u2 * _sig(u2)).astype(BF16)

    vec = pl.BlockSpec((1, D), lambda i: (0, 0))
    row = pl.BlockSpec((tt, D), lambda i: (i, 0))
    return _pcall(
        body, name="conv_fwd", grid=(T // tt,),
        in_specs=[pl.BlockSpec((tt, D), lambda i: (i, ca)), pl.BlockSpec((tt, D), lambda i: (i, cb)),
                  pl.BlockSpec((tt, D), lambda i: (jnp.maximum(i - 1, 0), ca)),
                  pl.BlockSpec((tt, D), lambda i: (jnp.maximum(i - 1, 0), cb)),
                  pl.BlockSpec((CONV_W, D), lambda i: (0, 0)), vec, vec, vec],
        out_specs=[row, row],
        out_shape=[jax.ShapeDtypeStruct((T, D), F32), jax.ShapeDtypeStruct((T, D), BF16)],
        scratch_shapes=[pltpu.VMEM((tt + HALO + CONV_PAD, D), F32), pltpu.VMEM((SUB - 1, tt + HALO, D), F32)],
        compiler_params=_cp("parallel"))(proj, proj, proj, proj, conv_k, conv_b, ln_g, ln_b)


def _merge(ret_in, u3, proj, wall, off_r, off_c):
    T = ret_in.shape[0]
    tm = min(T, 512)
    kr, kc = VW // NCHIP, D // NCHIP

    def body(ri_ref, u3_ref, gr_ref, gc_ref, wr_ref, wc_ref, ro_ref, co_ref, m_ref):
        ro = _dot_nt(ri_ref[:, 0:kr], wr_ref[0])
        co = _dot_nt(u3_ref[:, 0:kc], wc_ref[0])
        for s in range(1, NCHIP):
            ro += _dot_nt(ri_ref[:, s * kr:(s + 1) * kr], wr_ref[s])
            co += _dot_nt(u3_ref[:, s * kc:(s + 1) * kc], wc_ref[s])
        ro_ref[...] = ro.astype(BF16)
        co_ref[...] = co.astype(BF16)
        m = _sig(gr_ref[...].astype(F32)) * ro + _sig(gc_ref[...].astype(F32)) * co
        m_ref[...] = m.astype(BF16)

    row = pl.BlockSpec((tm, D), lambda i: (i, 0))
    return _pcall(
        body, name="merge", grid=(T // tm,),
        in_specs=[pl.BlockSpec((tm, VW), lambda i: (i, 0)), row,
                  pl.BlockSpec((tm, D), lambda i: (i, 8)), pl.BlockSpec((tm, D), lambda i: (i, 9)),
                  pl.BlockSpec((NCHIP, D, kr), lambda i: (0, 0, off_r // kr)),
                  pl.BlockSpec((NCHIP, D, kc), lambda i: (0, 0, off_c // kc))],
        out_specs=[row] * 3, out_shape=[jax.ShapeDtypeStruct((T, D), BF16)] * 3,
        compiler_params=_cp("parallel"))(ret_in, u3, proj, proj, wall, wall)


def _loss_ln_bwd(z, g, b, target, coef):
    T = z.shape[0]
    tm = min(T, 256)
    nt = T // tm

    def body(z_ref, g_ref, b_ref, t_ref, loss_ref, dzb_ref, dzt_ref, dz_ref, dg_ref, db_ref, lacc):
        i = pl.program_id(0)
        _zero_first(i == 0, lacc, dg_ref, db_ref)
        gam = g_ref[...]
        y, xh, rstd = _ln_fwd(z_ref[...], gam, b_ref[...])
        e = y - t_ref[...]
        lacc[...] += _colsum(e * e)
        dy = e * (1.0 / D)
        dz = _ln_bwd(dy, xh, rstd, gam)
        dz_ref[...] = dz
        dzc = coef * dz
        dzb_ref[...] = dzc.astype(BF16)
        dzt_ref[...] = dzc.T.astype(BF16)
        dg_ref[...] += _colsum(dy * xh)
        db_ref[...] += _colsum(dy)

        @pl.when(i == nt - 1)
        def _():
            loss_ref[...] = (0.5 / D) * jnp.sum(lacc[...], axis=1, keepdims=True)

    row = pl.BlockSpec((tm, D), lambda i: (i, 0))
    vec = pl.BlockSpec((1, D), lambda i: (0, 0))
    return _pcall(
        body, name="loss_ln_bwd", grid=(nt,),
        in_specs=[row, vec, vec, row],
        out_specs=[pl.BlockSpec((1, 1), lambda i: (0, 0)), row, pl.BlockSpec((D, tm), lambda i: (0, i)),
                   row, vec, vec],
        out_shape=[jax.ShapeDtypeStruct((1, 1), F32), jax.ShapeDtypeStruct((T, D), BF16),
                   jax.ShapeDtypeStruct((D, T), BF16), jax.ShapeDtypeStruct((T, D), F32),
                   jax.ShapeDtypeStruct((1, D), F32), jax.ShapeDtypeStruct((1, D), F32)],
        scratch_shapes=[pltpu.VMEM((1, D), F32)],
        compiler_params=_cp("arbitrary"))(z, g, b, target)


def _ffn_bwd_h(dfb, wall, od, a, b, name):
    T = dfb.shape[0]
    tm = min(T, 512)

    def body(d_ref, w_ref, a_ref, b_ref, da_ref, db_ref):
        d = d_ref[...]
        for s in range(NCHIP):
            cols = slice(s * FSP, (s + 1) * FSP)
            dh = _dot(d, w_ref[s])
            a = a_ref[:, cols].astype(F32)
            sg = _sig(a)
            da_ref[:, cols] = (dh * b_ref[:, cols].astype(F32) * (sg * (1.0 + a * (1.0 - sg)))).astype(BF16)
            db_ref[:, cols] = (dh * a * sg).astype(BF16)

    ospec = pl.BlockSpec((tm, FP), lambda i: (i, 0))
    return _pcall(
        body, name=name, grid=(T // tm,),
        in_specs=[pl.BlockSpec((tm, D), lambda i: (i, 0)), _resident((NCHIP, D, FSP), od // FSP), ospec, ospec],
        out_specs=[ospec] * 2, out_shape=[jax.ShapeDtypeStruct((T, FP), BF16)] * 2,
        compiler_params=_cp("parallel"))(dfb, wall, a, b)


DX_SUB = 256


def _dx_partial(lhs, wall, chips, name):
    T, K = lhs.shape
    ks = K // NCHIP
    nc = len(chips)
    assert list(chips) == list(range(chips[0], chips[0] + nc)) and chips[0] % nc == 0
    tm = min(T, 512)

    def body(l_ref, w_ref, o_ref):
        for r0 in range(0, tm, DX_SUB):
            r = slice(r0, r0 + DX_SUB)
            acc = _dot_nt(l_ref[r, 0:ks], w_ref[0])
            for s in range(1, nc):
                acc += _dot_nt(l_ref[r, s * ks:(s + 1) * ks], w_ref[s])
            o_ref[r, :] = acc

    blk = chips[0] // nc
    return _pcall(
        body, name=name, grid=(T // tm,),
        in_specs=[pl.BlockSpec((tm, nc * ks), lambda i: (i, blk)),
                  pl.BlockSpec((nc, D, ks), lambda i: (blk, 0, 0), pipeline_mode=pl.Buffered(1))],
        out_specs=pl.BlockSpec((tm, D), lambda i: (i, 0)),
        out_shape=jax.ShapeDtypeStruct((T, D), F32),
        compiler_params=_cp("parallel"))(lhs, wall)


def _dx_bwd(lhs, offs, wall, dz_next, name, ln=None, chips=tuple(range(NCHIP)), partial=None):
    T, K = lhs[0].shape
    ks = K // NCHIP
    nl = len(lhs)
    nc = len(chips)
    assert list(offs) == [l * ks for l in range(nl)]
    assert list(chips) == list(range(chips[0], chips[0] + nc)) and chips[0] % nc == 0
    blk = chips[0] // nc
    tm = min(T, 512)

    def body(*refs):
        l_refs = refs[:nl]
        w_ref = refs[nl]
        dzn_ref = refs[nl + 1]
        pos = nl + 2
        if partial is not None:
            p_ref = refs[pos]
            pos += 1
        if ln is not None:
            z_ref, g_ref = refs[pos:pos + 2]
            pos += 2
        outs = refs[pos:]
        sums = list(outs[3:]) if ln is not None else list(outs[1:])

        _zero_first(pl.program_id(0) == 0, *sums)

        for r0 in range(0, tm, DX_SUB):
            r = slice(r0, r0 + DX_SUB)
            acc = None if partial is None else p_ref[r, :]
            for s in range(nc):
                rows = slice(s * ks, (s + 1) * ks)
                for l in range(nl):
                    part = _dot_nt(l_refs[l][r, rows], w_ref[s, :, l * ks:(l + 1) * ks])
                    acc = part if acc is None else acc + part
            dy = acc + ALPHA * dzn_ref[r, :]
            if ln is None:
                outs[0][r, :] = dy
            else:
                gam = g_ref[...]
                _, xh, rstd = _ln_fwd(z_ref[r, :], gam, 0.0)
                dz = _ln_bwd(dy, xh, rstd, gam)
                dzc = ln[2] * dz
                outs[0][r, :] = dzc.astype(BF16)
                outs[1][:, r] = dzc.T.astype(BF16)
                outs[2][r, :] = dz
                outs[3][...] += _colsum(dy * xh)
                outs[4][...] += _colsum(dy)

    row = pl.BlockSpec((tm, D), lambda i: (i, 0))
    vec = pl.BlockSpec((1, D), lambda i: (0, 0))
    in_specs = [pl.BlockSpec((tm, nc * ks), lambda i: (i, blk))] * nl
    in_specs += [pl.BlockSpec((nc, D, nl * ks), lambda i: (blk, 0, 0), pipeline_mode=pl.Buffered(1)), row]
    args = list(lhs) + [wall, dz_next]
    if partial is not None:
        in_specs.append(row)
        args.append(partial)
    if ln is None:
        out_specs = [row]
        out_shape = [jax.ShapeDtypeStruct((T, D), F32)]
    else:
        in_specs += [row, vec]
        args += [ln[0], ln[1]]
        out_specs = [row, pl.BlockSpec((D, tm), lambda i: (0, i)), row, vec, vec]
        out_shape = [jax.ShapeDtypeStruct((T, D), BF16), jax.ShapeDtypeStruct((D, T), BF16),
                     jax.ShapeDtypeStruct((T, D), F32), jax.ShapeDtypeStruct((1, D), F32),
                     jax.ShapeDtypeStruct((1, D), F32)]
    return _pcall(
        body, name=name, grid=(T // tm,), in_specs=in_specs, out_specs=out_specs, out_shape=out_shape,
        compiler_params=_cp("arbitrary"))(*args)


def _wgrad(lhs_t, rhs, key, name, g_all=None, colsum=False):
    T, N = rhs.shape
    tn = next(c for c in (768, 512, 256) if (N // NCHIP) % c == 0 and LOC[key][1] % c == 0)
    nps = N // NCHIP // tn
    off = LOC[key][1]
    cols = GCOLS[LOC[key][0]]

    def body(*refs):
        l_ref, r_ref = refs[0], refs[1]
        o_ref, t_ref = refs[-2 - colsum], refs[-1]
        o_ref[...] = _dot(l_ref[...], r_ref[...]).astype(BF16)
        if colsum:
            refs[-2][...] = _colsum(r_ref[...].astype(F32))
        t_ref[...] = jnp.zeros_like(t_ref)

    in_specs = [_resident((D, T), 0), pl.BlockSpec((T, tn), lambda j: (0, j))]
    args = [lhs_t, rhs]
    aliases = {}
    if g_all is not None:
        in_specs.append(pl.BlockSpec(memory_space=pl.ANY))
        args.append(g_all)
        aliases = {2: 0}
    out_specs = [pl.BlockSpec((None, D, tn), lambda j: (j // nps, 0, off // tn + j % nps))]
    out_shape = [jax.ShapeDtypeStruct((NCHIP, D, cols), BF16)]
    if colsum:
        out_specs.append(pl.BlockSpec((1, tn), lambda j: (0, j)))
        out_shape.append(jax.ShapeDtypeStruct((1, N), F32))
    out_specs.append(pl.BlockSpec((8, 128), lambda j: (0, 0)))
    out_shape.append(jax.ShapeDtypeStruct((8, 128), F32))
    res = _pcall(
        body, name=name, grid=(N // tn,), in_specs=in_specs, out_specs=out_specs, out_shape=out_shape,
        input_output_aliases=aliases,
        compiler_params=_cp("arbitrary"))(*args)
    return (res[0], res[1]) if colsum else res[0]


def _merge_bwd(dmb, wall, off, proj, ro, co):
    T = dmb.shape[0]
    tm = min(T, 512)
    ks = D // NCHIP

    def body(d_ref, w_ref, gr_ref, gc_ref, ro_ref, co_ref, dro_ref, drot_ref, dco_ref, dcot_ref, dp_ref):
        d = d_ref[...]
        dmg = jnp.concatenate([_dot(d, w_ref[s]) for s in range(NCHIP)], axis=1)
        sr = _sig(gr_ref[...].astype(F32))
        sc = _sig(gc_ref[...].astype(F32))
        dro = dmg * sr
        dco = dmg * sc
        dro_ref[...] = dro.astype(BF16)
        drot_ref[...] = dro.T.astype(BF16)
        dco_ref[...] = dco.astype(BF16)
        dcot_ref[...] = dco.T.astype(BF16)
        dp_ref[:, 0:D] = (dmg * ro_ref[...].astype(F32) * sr * (1.0 - sr)).astype(BF16)
        dp_ref[:, D:2 * D] = (dmg * co_ref[...].astype(F32) * sc * (1.0 - sc)).astype(BF16)

    row = pl.BlockSpec((tm, D), lambda i: (i, 0))
    col = pl.BlockSpec((D, tm), lambda i: (0, i))
    return _pcall(
        body, name="merge_bwd", grid=(T // tm,),
        in_specs=[row, pl.BlockSpec((NCHIP, D, ks), lambda i: (0, 0, off // ks)),
                  pl.BlockSpec((tm, D), lambda i: (i, 8)), pl.BlockSpec((tm, D), lambda i: (i, 9)), row, row],
        out_specs=[row, col, row, col, pl.BlockSpec((tm, 2 * D), lambda i: (i, 4))],
        out_shape=[jax.ShapeDtypeStruct((T, D), BF16), jax.ShapeDtypeStruct((D, T), BF16),
                   jax.ShapeDtypeStruct((T, D), BF16), jax.ShapeDtypeStruct((D, T), BF16),
                   jax.ShapeDtypeStruct((T, INW), BF16)],
        compiler_params=_cp("parallel"))(dmb, wall, proj, proj, ro, co)


def _reto_bwd(dro, wall, off, r, proj, gn_g, dproj):
    T = dro.shape[0]
    tm = min(T, 512)
    hps = H // NCHIP

    def body(d_ref, w_ref, r_ref, g_ref, gn_ref, _, dr_ref, dgn_ref, dp_ref):
        _zero_first(pl.program_id(1) == 0, dgn_ref)
        dri = _dot(d_ref[...], w_ref[...])
        rr = r_ref[...]
        mu = jnp.mean(rr, axis=-1, keepdims=True)
        xc = rr - mu
        var = jnp.mean(xc * xc, axis=-1, keepdims=True)
        rstd = lax.rsqrt(var + EPS)
        rn = xc * rstd
        gn = gn_ref[...]
        g = g_ref[...].astype(F32)
        sg = _sig(g)
        dy = dri * (g * sg)
        dp_ref[...] = (dri * (rn * gn) * (sg * (1.0 + g * (1.0 - sg)))).astype(BF16)
        dgn_ref[...] += _colsum(dy * rn)
        dr_ref[...] = _ln_bwd(dy, rn, rstd, gn).astype(BF16)

    return _pcall(
        body, name="reto_bwd", grid=(H, T // tm),
        in_specs=[pl.BlockSpec((tm, D), lambda j, i: (i, 0)),
                  pl.BlockSpec((None, D, DV), lambda j, i: (j // hps, 0, off // DV + j % hps)),
                  pl.BlockSpec((tm, DV), lambda j, i: (i, j)),
                  pl.BlockSpec((tm, DV), lambda j, i: (i, 2 * VW // DV + j)),
                  pl.BlockSpec((1, DV), lambda j, i: (0, j)),
                  pl.BlockSpec(memory_space=pl.ANY)],
        out_specs=[pl.BlockSpec((tm, DV), lambda j, i: (i, j)), pl.BlockSpec((1, DV), lambda j, i: (0, j)),
                   pl.BlockSpec((tm, DV), lambda j, i: (i, 2 * VW // DV + j))],
        out_shape=[jax.ShapeDtypeStruct((T, VW), BF16), jax.ShapeDtypeStruct((1, VW), F32),
                   jax.ShapeDtypeStruct((T, INW), BF16)],
        input_output_aliases={5: 2},
        compiler_params=_cp("arbitrary", "arbitrary"))(dro, wall, r, proj, gn_g, dproj)


def _retention_bwd(proj, dr, states, cos_t, sin_t, dm_t, xi_t, zeta_t, cds, dproj):
    T = proj.shape[0]
    n = T // CH // RET_CPS
    tr = RET_CPS * CH
    scale = DK ** -0.5

    def body(q_ref, k_ref, v_ref, dr_ref, st_ref, cos_ref, sin_ref, dm_ref, xi_ref, zt_ref, _, dp_ref, ds):
        @pl.when(pl.program_id(0) == 0)
        def _():
            ds[...] = jnp.zeros_like(ds)

        def unrope(d, t):
            return d * cos_ref[t, :] + pltpu.roll(d * sin_ref[t, :], DK // 2, 1)

        for h in range(H):
            rows = slice(h * DK, (h + 1) * DK)
            dm = dm_ref[h]
            zt = zt_ref[h]
            ds_prev = ds[rows, :]
            for j in reversed(range(RET_CPS)):
                t = slice(j * CH, (j + 1) * CH)
                q = q_ref[t, h * DK:(h + 1) * DK]
                k = k_ref[t, h * DK:(h + 1) * DK]
                v = v_ref[t, h * DV:(h + 1) * DV]
                d_r = dr_ref[t, h * DV:(h + 1) * DV]
                s_b = st_ref[j, rows, :]
                sc = _dot_nt(q, k) * dm
                dsc = _dot_nt(d_r, v) * dm
                drx = (d_r.astype(F32) * xi_ref[h]).astype(BF16)
                ds_b = ds_prev.astype(BF16)
                kz = (k.astype(F32) * zt).astype(BF16)
                dq = _dot(dsc.astype(BF16), k) + _dot_nt(drx, s_b)
                dk = _dot(dsc.T.astype(BF16), q) + _dot_nt(v, ds_b) * zt
                dv = _dot(sc.T.astype(BF16), d_r) + _dot(kz, ds_b)
                ds_prev = cds[h] * ds_prev + _dot(q.astype(F32).T.astype(BF16), drx)
                dp_ref[t, h * DK:(h + 1) * DK] = unrope(dq * scale, t).astype(BF16)
                dp_ref[t, D + h * DK:D + (h + 1) * DK] = unrope(dk, t).astype(BF16)
                dp_ref[t, 2 * D + h * DV:2 * D + (h + 1) * DV] = dv.astype(BF16)
            ds[rows, :] = ds_prev

    rv = lambda c: n - 1 - c
    full3 = lambda shp: pl.BlockSpec(shp, lambda c: (0, 0, 0))
    return _pcall(
        body, name="retention_bwd", grid=(n,),
        in_specs=[pl.BlockSpec((tr, D), lambda c: (rv(c), 0)),
                  pl.BlockSpec((tr, D), lambda c: (rv(c), 1)),
                  pl.BlockSpec((tr, VW), lambda c: (rv(c), 1)),
                  pl.BlockSpec((tr, VW), lambda c: (rv(c), 0)),
                  pl.BlockSpec((RET_CPS, H * DK, DV), lambda c: (rv(c), 0, 0)),
                  pl.BlockSpec((tr, DK), lambda c: (rv(c), 0)),
                  pl.BlockSpec((tr, DK), lambda c: (rv(c), 0)),
                  full3((H, CH, CH)), full3((H, CH, DV)), full3((H, CH, DK)),
                  pl.BlockSpec(memory_space=pl.ANY)],
        out_specs=pl.BlockSpec((tr, 2 * D + VW), lambda c: (rv(c), 0)),
        out_shape=jax.ShapeDtypeStruct((T, INW), BF16),
        input_output_aliases={10: 0},
        scratch_shapes=[pltpu.VMEM((H * DK, DV), F32)],
        compiler_params=_cp("arbitrary"))(proj, proj, proj, dr, states, cos_t, sin_t, dm_t, xi_t, zeta_t, dproj)


def _convo_bwd(dco, wall, off, u1, ln_g, ln_b):
    T = dco.shape[0]
    tm = min(T, 512)
    ks = D // NCHIP

    def body(d_ref, w_ref, u1_ref, g_ref, b_ref, du1_ref, dg_ref, db_ref, dcb_ref):
        _zero_first(pl.program_id(0) == 0, dg_ref, db_ref, dcb_ref)
        d = d_ref[...]
        du3 = jnp.concatenate([_dot(d, w_ref[s]) for s in range(NCHIP)], axis=1)
        gam = g_ref[...]
        u2, xh, rstd = _ln_fwd(u1_ref[...], gam, b_ref[...])
        sg = _sig(u2)
        du2 = du3 * (sg * (1.0 + u2 * (1.0 - sg)))
        du1 = _ln_bwd(du2, xh, rstd, gam)
        du1_ref[...] = du1
        dg_ref[...] += _colsum(du2 * xh)
        db_ref[...] += _colsum(du2)
        dcb_ref[...] += _colsum(du1)

    row = pl.BlockSpec((tm, D), lambda i: (i, 0))
    vec = pl.BlockSpec((1, D), lambda i: (0, 0))
    return _pcall(
        body, name="convo_bwd", grid=(T // tm,),
        in_specs=[row, pl.BlockSpec((NCHIP, D, ks), lambda i: (0, 0, off // ks)), row, vec, vec],
        out_specs=[row, vec, vec, vec],
        out_shape=[jax.ShapeDtypeStruct((T, D), F32)] + [jax.ShapeDtypeStruct((1, D), F32)] * 3,
        compiler_params=_cp("arbitrary"))(dco, wall, u1, ln_g, ln_b)


def _conv_bwd(du1, proj, conv_k, dproj):
    T = du1.shape[0]
    tt = min(T, CONV_TT)
    nt = T // tt
    ca, cb = 6, 7

    def body(d_ref, dn_ref, a_ref, b_ref, pa_ref, pb_ref, k_ref, _, dp_ref, dk_ref, win_u, win_d, sh_u, sh_d):
        i = pl.program_id(0)
        a, sb = _glu(a_ref, b_ref)
        win_u[HALO:tt + HALO, :] = a * sb
        pa, psb = _glu(pa_ref, pb_ref, slice(tt - HALO, tt))
        win_u[0:HALO, :] = jnp.where(i > 0, pa * psb, 0.0)
        win_d[0:tt, :] = d_ref[...]
        win_d[tt:tt + HALO, :] = jnp.where(i < nt - 1, dn_ref[0:HALO, :], 0.0)
        _shift_copies(win_u, sh_u, tt + HALO)
        _shift_copies(win_d, sh_d, tt + HALO)

        @pl.when(i == 0)
        def _():
            dk_ref[...] = jnp.zeros_like(dk_ref)

        for c0 in range(0, D, CONV_CB):
            cs = slice(c0, c0 + CONV_CB)
            for r0 in range(0, tt, CONV_SB):
                acc = jnp.zeros((CONV_SB, CONV_CB), F32)
                for w in range(CONV_W):
                    st = r0 + (CONV_W - 1) - w
                    acc += _tap(win_d, sh_d, st, CONV_SB, cs) * k_ref[w:w + 1, cs]
                aa = a_ref[r0:r0 + CONV_SB, cs].astype(F32)
                ss = _sig(b_ref[r0:r0 + CONV_SB, cs].astype(F32))
                dp_ref[r0:r0 + CONV_SB, cs] = (acc * ss).astype(BF16)
                dp_ref[r0:r0 + CONV_SB, c0 + D:c0 + D + CONV_CB] = (acc * aa * ss * (1.0 - ss)).astype(BF16)
        for c0 in range(0, D, CONV_CB):
            cs = slice(c0, c0 + CONV_CB)
            for w in range(CONV_W):
                acc = jnp.zeros((CONV_SB, CONV_CB), F32)
                for r0 in range(0, tt, CONV_SB):
                    st = r0 + HALO - (CONV_W - 1) + w
                    acc += win_d[r0:r0 + CONV_SB, cs] * _tap(win_u, sh_u, st, CONV_SB, cs)
                dk_ref[w:w + 1, cs] += _colsum(acc)

    blk = lambda f, c: pl.BlockSpec((tt, D), lambda i: (f(i), c))
    cur = lambda i: i
    prv = lambda i: jnp.maximum(i - 1, 0)
    nxt = lambda i: jnp.minimum(i + 1, nt - 1)
    return _pcall(
        body, name="conv_bwd", grid=(nt,),
        in_specs=[blk(cur, 0), blk(nxt, 0), blk(cur, ca), blk(cur, cb), blk(prv, ca), blk(prv, cb),
                  pl.BlockSpec((CONV_W, D), lambda i: (0, 0)), pl.BlockSpec(memory_space=pl.ANY)],
        out_specs=[pl.BlockSpec((tt, 2 * D), lambda i: (i, 3)), pl.BlockSpec((HALO, D), lambda i: (0, 0))],
        out_shape=[jax.ShapeDtypeStruct((T, INW), BF16), jax.ShapeDtypeStruct((HALO, D), F32)],
        input_output_aliases={7: 0},
        scratch_shapes=[pltpu.VMEM((tt + HALO + CONV_PAD, D), F32), pltpu.VMEM((tt + HALO + CONV_PAD, D), F32),
                        pltpu.VMEM((SUB - 1, tt + HALO, D), F32), pltpu.VMEM((SUB - 1, tt + HALO, D), F32)],
        compiler_params=_cp("arbitrary"))(du1, du1, proj, proj, proj, proj, conv_k, dproj)


def _local_step(x, target, wts, sp, kvec, pos_c, pos_sc, adam):
    T = x.shape[0]
    cos_t, sin_t = _rope_tables(T)
    dm_t, xi_t, zeta_t, cds = _decay_tables()
    wa = lambda key: wts[LOC[key][0]]
    wo = lambda key: LOC[key][1]
    _ORDER.active, _ORDER.token = True, None

    xb, xt = _cast_t(x)
    a1, b1, h1 = _ffn_up(xb, wa("g1"), wo("g1"), wo("u1"), "ffn1_up")
    z1, x1, x1b, x1t = _proj_ln(h1, wa("d1"), wo("d1"), x, sp["ln1_g"], sp["ln1_b"], 0.5, "ffn1_down_ln")
    proj = _inproj(x1b, wa("w_in"), wo("w_in"), sp["b_in"], cos_t, sin_t)
    r, ret_in, states = _retention_fwd(proj, sp["ret_gn_g"], dm_t, xi_t, zeta_t, cds)
    kall = _sum_devices(_all_gather_small(kvec, "gather_conv_k"), "sum_conv_k")
    sp = dict(sp, conv_k=kall.reshape(-1)[:CONV_W * D].reshape(CONV_W, D))
    u1, u3 = _conv_fwd(proj, sp["conv_k"], sp["conv_b"], sp["conv_ln_g"], sp["conv_ln_b"])
    ro, co, merged = _merge(ret_in, u3, proj, wa("w_ret_o"), wo("w_ret_o"), wo("w_conv_o"))
    z2, x2, x2b, x2t = _proj_ln(merged, wa("w_out"), wo("w_out"), x1, sp["ln2_g"], sp["ln2_b"], 1.0, "out_proj_ln")
    a2, b2, h2 = _ffn_up(x2b, wa("g2"), wo("g2"), wo("u2"), "ffn2_up")
    (z3,) = _proj_ln(h2, wa("d2"), wo("d2"), x2, sp["ln3_g"], sp["ln3_b"], 0.5, "ffn2_down", want_b=False)

    sg = {}
    rs = {}
    loss, df2b, df2t, dz3, sg["ln3_g"], sg["ln3_b"] = _loss_ln_bwd(z3, sp["ln3_g"], sp["ln3_b"], target, 0.5)
    da2, db2 = _ffn_bwd_h(df2b, wa("d2"), wo("d2"), a2, b2, "ffn2_bwd_h")
    g4 = _wgrad(df2t, h2, "d2", "wgrad_d2")
    g4 = _wgrad(x2t, da2, "g2", "wgrad_g2", g4)
    g4 = _wgrad(x2t, db2, "u2", "wgrad_u2", g4)
    rs[4] = _ReduceScatter(g4, 4, pos_c, pos_sc)
    dmb, dmt, dz2, sg["ln2_g"], sg["ln2_b"] = _dx_bwd(
        [da2, db2], [wo("g2"), wo("u2")], wa("g2"), dz3, "ffn2_dx_ln", ln=(z2, sp["ln2_g"], 1.0))
    rs[4].phase2()
    g3 = _wgrad(dmt, merged, "w_out", "wgrad_out")
    dro, drot, dco, dcot, dproj = _merge_bwd(dmb, wa("w_out"), wo("w_out"), proj, ro, co)
    g3 = _wgrad(drot, ret_in, "w_ret_o", "wgrad_ret_o", g3)
    g3 = _wgrad(dcot, u3, "w_conv_o", "wgrad_conv_o", g3)
    rs[3] = _ReduceScatter(g3, 3, pos_c, pos_sc)
    dr, sg["ret_gn_g"], dproj = _reto_bwd(dro, wa("w_ret_o"), wo("w_ret_o"), r, proj, sp["ret_gn_g"], dproj)
    rs[4].phase3()
    rs[3].phase2()
    dproj = _retention_bwd(proj, dr, states, cos_t, sin_t, dm_t, xi_t, zeta_t, cds, dproj)
    du1, sg["conv_ln_g"], sg["conv_ln_b"], sg["conv_b"] = _convo_bwd(
        dco, wa("w_conv_o"), wo("w_conv_o"), u1, sp["conv_ln_g"], sp["conv_ln_b"])
    dproj, dck = _conv_bwd(du1, proj, sp["conv_k"], dproj)
    sg["conv_k"] = dck[:CONV_W]
    adam(4, rs[4].result())
    rs[3].phase3()
    g2, sg["b_in"] = _wgrad(x1t, dproj, "w_in", "wgrad_in", colsum=True)
    rs[2] = _ReduceScatter(g2, 2, pos_c, pos_sc)
    dx_part = _dx_partial(dproj, wa("w_in"), (0, 1), "mixer_dx_part")
    df1b, df1t, dz1, sg["ln1_g"], sg["ln1_b"] = _dx_bwd(
        [dproj], [wo("w_in")], wa("w_in"), dz2, "mixer_dx_ln", ln=(z1, sp["ln1_g"], 0.5), chips=(2, 3),
        partial=dx_part)
    adam(3, rs[3].result())
    rs[2].phase2()
    shapes = {n: sg[n].shape for n in SMALL + ["conv_k"]}
    small_parts = _all_gather_small(_pack_small(sg, loss, SMALL_ROWS), "gather_small")
    da1, db1 = _ffn_bwd_h(df1b, wa("d1"), wo("d1"), a1, b1, "ffn1_bwd_h")
    small_sum = _sum_devices(small_parts, "sum_small")
    g1 = _wgrad(df1t, h1, "d1", "wgrad_d1")
    rs[1] = _ReduceScatter(g1, 1, pos_c, pos_sc)
    g0 = _wgrad(xt, da1, "g1", "wgrad_g1")
    g0 = _wgrad(xt, db1, "u1", "wgrad_u1", g0)
    rs[0] = _ReduceScatter(g0, 0, pos_c, pos_sc)
    rs[2].phase3()
    rs[1].phase2()
    rs[0].phase2()
    (grad_x,) = _dx_bwd([da1, db1], [wo("g1"), wo("u1")], wa("g1"), dz1, "ffn1_dx")
    adam(2, rs[2].result())
    rs[1].phase3()
    rs[0].phase3()
    adam(1, rs[1].result())
    adam(0, rs[0].result())
    _ORDER.active = False
    return grad_x, small_sum, shapes


MESH = pl.DeviceIdType.MESH
ANY = pl.BlockSpec(memory_space=pl.ANY)
HALF = D // 2


def _place():
    x, y, c = lax.axis_index("x"), lax.axis_index("y"), lax.axis_index("c")
    chips = [(1 - x, y), (x, 1 - y), (1 - x, 1 - y)]
    return x, y, c, chips


GATHER_ID = 1


def _gather_weights(wloc, name):
    w_ref = jax.new_ref(wloc, memory_space=pltpu.MemorySpace.HBM)
    o_ref = jax.empty_ref(jax.ShapeDtypeStruct((NCHIP, 2, HALF, wloc.shape[-1]), BF16),
                          memory_space=pltpu.MemorySpace.HBM)
    dma = pltpu.SemaphoreType.DMA

    @pl.kernel(mesh=plsc.ScalarSubcoreMesh(axis_name="sc", num_cores=1), name=name,
               scratch_types=(dma(()), dma((2,)), dma((2,)), dma((3,)), dma((3,)), dma(()), dma(())),
               compiler_params=pltpu.CompilerParams(collective_id=GATHER_ID))
    def launch(lsem, s1, r1, s2, r2, s3, r3):
        x, y, c, _ = _place()
        me = 2 * x + y
        sib = (x, y, 1 - c)
        x_nbr, y_nbr = (1 - x, y, c), (x, 1 - y, c)
        x_chip, y_chip, d_chip = 2 * (1 - x) + y, 2 * x + (1 - y), 2 * (1 - x) + (1 - y)
        _handshake([sib, x_nbr, y_nbr])
        mine = pltpu.make_async_copy(w_ref, o_ref.at[me], lsem)
        mine.start()

        def rc(src, dst, ss, rs, dev):
            return pltpu.make_async_remote_copy(src_ref=src, dst_ref=dst, send_sem=ss, recv_sem=rs,
                                                device_id=dev, device_id_type=MESH)

        first = [rc(w_ref.at[c], o_ref.at[me, c], s1.at[0], r1.at[0], x_nbr),
                 rc(w_ref.at[c], o_ref.at[me, c], s1.at[1], r1.at[1], y_nbr)]
        for cp in first:
            cp.start()
        on_chip = c * x_chip + (1 - c) * y_chip
        other_chip = c * y_chip + (1 - c) * x_chip
        on_to = (c * x + (1 - c) * (1 - x), c * (1 - y) + (1 - c) * y, c)
        slot = o_ref.at[on_chip, c]
        rc(slot, slot, s1.at[1 - c], r1.at[1 - c], sib).wait_recv()
        onward = rc(slot, slot, s3, r3, on_to)
        onward.start()
        passed = [rc(slot, slot, s2.at[0], r2.at[0], sib)]
        passed[0].start()
        slot = o_ref.at[other_chip, c]
        rc(slot, slot, s1.at[c], r1.at[c], sib).wait_recv()
        passed.append(rc(slot, slot, s2.at[1], r2.at[1], sib))
        passed[1].start()
        slot = o_ref.at[d_chip, c]
        rc(slot, slot, s3, r3, sib).wait_recv()
        passed.append(rc(slot, slot, s2.at[2], r2.at[2], sib))
        passed[2].start()
        for j, chip in enumerate([other_chip, on_chip, d_chip]):
            slot = o_ref.at[chip, 1 - c]
            rc(slot, slot, s2.at[j], r2.at[j], sib).wait_recv()
        for cp in first + [onward] + passed:
            cp.wait_send()
        mine.wait()

    launch()
    return o_ref[...]


PAIR_ID = 2
CHIP_ID = 3
HBM = pltpu.MemorySpace.HBM


def _sequencer(name, collective_id, n_sems):
    dma = pltpu.SemaphoreType.DMA
    return pl.kernel(mesh=plsc.ScalarSubcoreMesh(axis_name="sc", num_cores=1), name=name,
                     scratch_types=(dma((n_sems,)), dma((n_sems,))),
                     compiler_params=pltpu.CompilerParams(collective_id=collective_id))


def _handshake(peers):
    barrier = pltpu.get_barrier_semaphore()
    for peer in peers:
        pl.semaphore_signal(barrier, inc=1, device_id=peer, device_id_type=MESH)
    pl.semaphore_wait(barrier, len(peers))


def _pair_exchange(g5, name):
    _, _, hr, cols = g5.shape
    g_ref = jax.new_ref(g5, memory_space=HBM)
    o_ref = jax.empty_ref(jax.ShapeDtypeStruct((NCHIP, hr, cols), g5.dtype), memory_space=HBM)

    @_sequencer(name, PAIR_ID, NCHIP)
    def launch(ss, rs):
        x, y, c, _ = _place()
        sib = (x, y, 1 - c)
        _handshake([sib])
        cps = [pltpu.make_async_remote_copy(src_ref=g_ref.at[j, 1 - c], dst_ref=o_ref.at[j], send_sem=ss.at[j],
                                            recv_sem=rs.at[j], device_id=sib, device_id_type=MESH)
               for j in range(NCHIP)]
        for cp in cps:
            cp.start()
        for cp in cps:
            cp.wait()

    launch()
    return o_ref[...]


RS_TR = 128


def _pair_sum(pos, g5, got, name):
    _, _, hr, cols = g5.shape

    def body(pos_ref, g_ref, r_ref, o_ref):
        o_ref[...] = (g_ref[...].astype(F32) + r_ref[...].astype(F32)).astype(BF16)

    return _pcall(
        body, name=name, scalar_prefetch=1, grid=(NCHIP, hr // RS_TR),
        in_specs=[pl.BlockSpec((None, None, RS_TR, cols), lambda j, i, p: (j, p[0], i, 0)),
                  pl.BlockSpec((None, RS_TR, cols), lambda j, i, p: (j, i, 0))],
        out_specs=pl.BlockSpec((None, RS_TR, cols), lambda j, i, p: (j, i, 0)),
        out_shape=jax.ShapeDtypeStruct((NCHIP, hr, cols), BF16),
        compiler_params=_cp("parallel", "parallel"))(pos, g5, got)


def _chip_exchange(pb, name):
    _, hr, cols = pb.shape
    p_ref = jax.new_ref(pb, memory_space=HBM)
    o_ref = jax.empty_ref(jax.ShapeDtypeStruct((3, hr, cols), BF16), memory_space=HBM)

    @_sequencer(name, CHIP_ID, 3)
    def launch(ss, rs):
        x, y, c, chips = _place()
        _handshake([(px, py, c) for px, py in chips])
        cps = [pltpu.make_async_remote_copy(src_ref=p_ref.at[2 * px + py], dst_ref=o_ref.at[j], send_sem=ss.at[j],
                                            recv_sem=rs.at[j], device_id=(px, py, c), device_id_type=MESH)
               for j, (px, py) in enumerate(chips)]
        for cp in cps:
            cp.start()
        for cp in cps:
            cp.wait()

    launch()
    return o_ref[...]


def _chip_sum(pos, g5, got, peers, name):
    _, _, hr, cols = g5.shape

    def body(pos_ref, g_ref, r_ref, p_ref, o_ref, t_ref):
        acc = g_ref[...].astype(F32) + r_ref[...].astype(F32)
        for j in range(3):
            acc += p_ref[j].astype(F32)
        o_ref[...] = acc
        t_ref[...] = jnp.zeros_like(t_ref)

    return _pcall(
        body, name=name, scalar_prefetch=1, grid=(hr // RS_TR,),
        in_specs=[pl.BlockSpec((None, None, RS_TR, cols), lambda i, p: (p[0], p[1], i, 0)),
                  pl.BlockSpec((None, RS_TR, cols), lambda i, p: (p[0], i, 0)),
                  pl.BlockSpec((3, RS_TR, cols), lambda i, p: (0, i, 0))],
        out_specs=[pl.BlockSpec((None, RS_TR, cols), lambda i, p: (p[1], i, 0)),
                   pl.BlockSpec((8, 128), lambda i, p: (0, 0))],
        out_shape=[jax.ShapeDtypeStruct((2, hr, cols), F32), jax.ShapeDtypeStruct((8, 128), F32)],
        compiler_params=_cp("arbitrary"))(pos, g5, got, peers)


def _pair_share(gsum, name):
    g_ref = jax.new_ref(gsum, memory_space=HBM)

    @_sequencer(name, PAIR_ID, 1)
    def launch(ss, rs):
        x, y, c, _ = _place()
        sib = (x, y, 1 - c)
        _handshake([sib])
        cp = pltpu.make_async_remote_copy(src_ref=g_ref.at[c], dst_ref=g_ref.at[c], send_sem=ss.at[0],
                                          recv_sem=rs.at[0], device_id=sib, device_id_type=MESH)
        cp.start()
        cp.wait_send()
        pltpu.make_async_remote_copy(src_ref=g_ref.at[1 - c], dst_ref=g_ref.at[1 - c], send_sem=ss.at[0],
                                     recv_sem=rs.at[0], device_id=sib, device_id_type=MESH).wait_recv()

    launch()
    return g_ref[...]


class _ReduceScatter:
    def __init__(self, g_arr, gi, pos_c, pos_sc):
        _, rows, cols = g_arr.shape
        self.g5 = g_arr.reshape(NCHIP, 2, rows // 2, cols)
        self.gi, self.pos_c, self.pos_sc = gi, pos_c, pos_sc
        self.got = _pair_exchange(self.g5, f"pair_exchange_{gi}")

    def phase2(self):
        pb = _pair_sum(self.pos_c, self.g5, self.got, f"pair_sum_{self.gi}")
        self.peers = _chip_exchange(pb, f"chip_exchange_{self.gi}")

    def phase3(self):
        gsum, _ = _chip_sum(self.pos_sc, self.g5, self.got, self.peers, f"chip_sum_{self.gi}")
        self.full = _pair_share(gsum, f"pair_share_{self.gi}")

    def result(self):
        _, hr, cols = self.full.shape
        return self.full.reshape(2 * hr, cols)


SMALL_ROWS = 56


ALL_ID = 4


def _all_gather_small(vec, name):
    v_ref = jax.new_ref(vec, memory_space=HBM)
    o_ref = jax.empty_ref(jax.ShapeDtypeStruct((8, SMALL_ROWS, D), F32), memory_space=HBM)

    @_sequencer(name, ALL_ID, 8)
    def launch(ss, rs):
        x, y, c, _ = _place()
        me = 4 * x + 2 * y + c
        flip = lambda v, bit: 1 - v if bit else v
        peers = [(flip(x, m >> 2), flip(y, (m >> 1) & 1), flip(c, m & 1)) for m in range(1, 8)]
        _handshake(peers)
        mine = pltpu.make_async_copy(v_ref, o_ref.at[me], ss.at[7])
        mine.start()
        cps = [pltpu.make_async_remote_copy(src_ref=v_ref, dst_ref=o_ref.at[me], send_sem=ss.at[k],
                                            recv_sem=rs.at[k], device_id=peer, device_id_type=MESH)
               for k, peer in enumerate(peers)]
        for cp in cps:
            cp.start()
        for cp in cps:
            cp.wait()
        mine.wait()

    launch()
    return o_ref[...]


def _sum_devices(parts, name):
    def body(p_ref, o_ref):
        acc = p_ref[0]
        for d in range(1, 8):
            acc += p_ref[d]
        o_ref[...] = acc

    return _pcall(
        body, name=name, grid=(SMALL_ROWS // 8,),
        in_specs=[pl.BlockSpec((8, 8, D), lambda i: (0, i, 0))],
        out_specs=pl.BlockSpec((8, D), lambda i: (i, 0)),
        out_shape=jax.ShapeDtypeStruct((SMALL_ROWS, D), F32),
        compiler_params=_cp("parallel"))(parts)


def _adamw_math(w, g, m, v):
    m2 = ADAM_B1 * m + (1.0 - ADAM_B1) * g
    v2 = ADAM_B2 * v + (1.0 - ADAM_B2) * (g * g)
    m_hat = m2 / (1.0 - ADAM_B1 ** ADAM_STEP)
    v_hat = v2 / (1.0 - ADAM_B2 ** ADAM_STEP)
    delta = -ADAM_LR * (m_hat / (jnp.sqrt(v_hat) + ADAM_EPS) + ADAM_WD * w)
    return delta, m2, v2


def _adamw(w, g, m, v, name, g_block=None):
    R, C = w.shape
    tr = R
    gw_hint = C if g_block is None else g_block[0]
    for cand in (512, 352, 256, 176, 128, 64, 32, 16, 8):
        if R % cand == 0 and cand * max(C, gw_hint) * 4 <= (2 << 20):
            tr = cand
            break
    gw, gi = (C, 0) if g_block is None else g_block

    def body(w_ref, g_ref, m_ref, v_ref, go_ref, d_ref, mo_ref, vo_ref):
        g = g_ref[:, 0:C]
        d, m2, v2 = _adamw_math(w_ref[...], g, m_ref[...], v_ref[...])
        go_ref[...] = g
        d_ref[...] = d
        mo_ref[...] = m2
        vo_ref[...] = v2

    spec = pl.BlockSpec((tr, C), lambda i: (i, 0))
    return _pcall(
        body, name=name, grid=(R // tr,),
        in_specs=[spec, pl.BlockSpec((tr, gw), lambda i: (i, gi)), spec, spec],
        out_specs=[spec] * 4, out_shape=[jax.ShapeDtypeStruct((R, C), F32)] * 4,
        compiler_params=_cp("parallel"))(w, g, m, v)


BIG = ["ffn1_w_gate", "ffn1_w_up", "ffn1_w_down", "w_in", "w_ret_o", "w_conv_o", "w_out",
       "ffn2_w_gate", "ffn2_w_up", "ffn2_w_down"]
SLAB = {"ffn1_w_gate": "g1", "ffn1_w_up": "u1", "ffn1_w_down": "d1", "w_in": "w_in", "w_ret_o": "w_ret_o",
        "w_conv_o": "w_conv_o", "w_out": "w_out", "ffn2_w_gate": "g2", "ffn2_w_up": "u2", "ffn2_w_down": "d2"}
TRANSPOSED = {"ffn1_w_down", "ffn2_w_down", "w_ret_o", "w_conv_o", "w_out"}
MINOR_ROWS = {"ffn1_w_gate", "ffn1_w_up", "ffn2_w_gate", "ffn2_w_up"}
SMALL = ["ln1_g", "ln1_b", "ln2_g", "ln2_b", "ln3_g", "ln3_b", "conv_ln_g", "conv_ln_b", "conv_b",
         "ret_gn_g", "b_in"]
ORDER = ["ffn1_w_gate", "ffn1_w_up", "ffn1_w_down", "ln1_g", "ln1_b", "w_in", "b_in", "ret_gn_g", "conv_k",
         "conv_b", "conv_ln_g", "conv_ln_b", "w_ret_o", "w_conv_o", "w_out", "ln2_g", "ln2_b",
         "ffn2_w_gate", "ffn2_w_up", "ffn2_w_down", "ln3_g", "ln3_b"]


def _slab_width(name):
    return WIDTH[SLAB[name]]


def _pack_group(weights, keys):
    by_key = {SLAB[n]: n for n in BIG}
    parts = []
    for key in keys:
        w = weights[by_key[key]]
        w = w.T if by_key[key] in TRANSPOSED else w
        parts.append(jnp.pad(w, ((0, 0), (0, WIDTH[key] - w.shape[1]))))
    return jnp.concatenate(parts, axis=1).astype(BF16)


def _pack_small(vals, loss, rows):
    flat = jnp.concatenate([vals[n].reshape(-1) for n in SMALL] + [vals["conv_k"].reshape(-1), loss.reshape(-1)])
    return jnp.pad(flat, (0, rows * D - flat.shape[0])).reshape(rows, D)


def _unpack_small(arr, shapes):
    flat = arr.reshape(-1)
    out, pos = {}, 0
    for n in SMALL + ["conv_k"]:
        size = int(np.prod(shapes[n]))
        out[n] = flat[pos:pos + size].reshape(shapes[n])
        pos += size
    return out, flat[pos]


def kernel(x, ffn1_w_gate, ffn1_w_up, ffn1_w_down, ln1_g, ln1_b, w_in, b_in, ret_gn_g, conv_k, conv_b, conv_ln_g, conv_ln_b, w_ret_o, w_conv_o, w_out, ln2_g, ln2_b, ffn2_w_gate, ffn2_w_up, ffn2_w_down, ln3_g, ln3_b, loss_target, m_ffn1_w_gate, m_ffn1_w_up, m_ffn1_w_down, m_ln1_g, m_ln1_b, m_w_in, m_b_in, m_ret_gn_g, m_conv_k, m_conv_b, m_conv_ln_g, m_conv_ln_b, m_w_ret_o, m_w_conv_o, m_w_out, m_ln2_g, m_ln2_b, m_ffn2_w_gate, m_ffn2_w_up, m_ffn2_w_down, m_ln3_g, m_ln3_b, v_ffn1_w_gate, v_ffn1_w_up, v_ffn1_w_down, v_ln1_g, v_ln1_b, v_w_in, v_b_in, v_ret_gn_g, v_conv_k, v_conv_b, v_conv_ln_g, v_conv_ln_b, v_w_ret_o, v_w_conv_o, v_w_out, v_ln2_g, v_ln2_b, v_ffn2_w_gate, v_ffn2_w_up, v_ffn2_w_down, v_ln3_g, v_ln3_b):
    args = dict(locals())
    w = {n: args[n] for n in ORDER}
    m = {n: args["m_" + n] for n in ORDER}
    v = {n: args["v_" + n] for n in ORDER}
    xi, yi, ci = lax.axis_index("x"), lax.axis_index("y"), lax.axis_index("c")
    chip = 2 * xi + yi

    shards = {n: w[n][0] for n in BIG}
    wts = []
    for gi, keys in enumerate(GROUPS):
        slab = _pack_group(shards, keys)
        cols = slab.shape[1]
        wts.append(_gather_weights(slab.reshape(2, HALF, cols), f"gather_{gi}").reshape(NCHIP, D, cols))

    sp = {n: w[n] for n in SMALL}
    kpad = jnp.zeros((CONV_W, D), F32)
    kpad = lax.dynamic_update_slice(kpad, w["conv_k"][0, :, 0, :] * jnp.where(ci == 0, 1.0, 0.0), (0, chip * (D // NCHIP)))
    kvec = jnp.pad(kpad.reshape(-1), (0, SMALL_ROWS * D - CONV_W * D)).reshape(SMALL_ROWS, D)
    pos_c = jnp.reshape(ci, (1,)).astype(jnp.int32)
    pos_sc = jnp.stack([chip, ci]).astype(jnp.int32)
    out = {}

    def adam(gi, slab):
        for n in BIG:
            (g_of, off), width = LOC[SLAB[n]], _slab_width(n)
            if g_of != gi:
                continue
            w2 = w[n][0]
            if n in TRANSPOSED:
                res = _adamw(w2, slab[:, off:off + w2.shape[0]].T, m[n][0], v[n][0], "adamw_" + n)
                out[n] = [r[None] for r in res]
            elif n in MINOR_ROWS:
                res = _adamw(w2.T, slab[:, off:off + w2.shape[1]].T, m[n][0].T, v[n][0].T, "adamw_" + n)
                out[n] = [r.T[None] for r in res]
            else:
                res = _adamw(w2, slab, m[n][0], v[n][0], "adamw_" + n, g_block=(width, off // width))
                out[n] = [r[None] for r in res]

    grad_x, small_sum, shapes = _local_step(x[0], loss_target[0], wts, sp, kvec, pos_c, pos_sc, adam)
    small, total = _unpack_small(small_sum, shapes)

    for n in SMALL:
        res = _adamw(w[n], small[n], m[n], v[n], "adamw_" + n)
        out[n] = list(res)
    gk = lax.dynamic_slice(small["conv_k"], (0, chip * (D // NCHIP)), (CONV_W, D // NCHIP))
    res = _adamw(w["conv_k"][0, :, 0, :], gk, m["conv_k"][0, :, 0, :], v["conv_k"][0, :, 0, :], "adamw_conv_k")
    out["conv_k"] = [r[None, :, None, :] for r in res]

    grads = [out[n][0] for n in ORDER]
    deltas = [out[n][1] for n in ORDER]
    new_m = [out[n][2] for n in ORDER]
    new_v = [out[n][3] for n in ORDER]
    return (total, grad_x[None], *grads, *deltas, *new_m, *new_v)
```

```python
import dataclasses
import functools

import numpy as np
import jax
import jax.numpy as jnp
from jax import lax
from jax.experimental import pallas as pl
from jax.experimental.pallas import tpu as pltpu
from jax.experimental.pallas import tpu_sc as plsc

F32 = jnp.float32
BF16 = jnp.bfloat16

D = 1024
FS = 704
FSP = 768
FP = 4 * FSP
H = 8
DK = 128
DV = 256
CH = 128
VW = H * DV
INW = 10240
INS = INW // 4
CONV_W = 31
HALO = 32
EPS = 1e-5
ALPHA = 2.0 ** 0.25
ROPE_BASE = 10000.0
NCHIP = 4

ADAM_LR, ADAM_B1, ADAM_B2, ADAM_EPS, ADAM_WD, ADAM_STEP = 0.001, 0.9, 0.999, 1e-08, 0.01, 10

OFF = {"w_in": 0, "w_ret_o": 2560, "g1": 3072, "u1": 3840, "d1": 4608,
       "g2": 5376, "u2": 6144, "d2": 6912, "w_conv_o": 7680, "w_out": 7936}
WCOLS = 8192
WIDTH = {"w_in": INS, "w_ret_o": VW // NCHIP, "w_conv_o": D // NCHIP, "w_out": D // NCHIP,
         "g1": FSP, "u1": FSP, "d1": FSP, "g2": FSP, "u2": FSP, "d2": FSP}
GROUPS = (("g1", "u1"), ("d1",), ("w_in",), ("w_ret_o", "w_conv_o", "w_out"), ("g2", "u2", "d2"))
LOC = {}
for _gi, _keys in enumerate(GROUPS):
    _off = 0
    for _k in _keys:
        LOC[_k] = (_gi, _off)
        _off += WIDTH[_k]
GCOLS = [sum(WIDTH[k] for k in keys) for keys in GROUPS]
VMEM_LIMIT = 56 << 20


def _cp(*sem, **kw):
    return pltpu.CompilerParams(dimension_semantics=sem, vmem_limit_bytes=VMEM_LIMIT, **kw)


class _ProgramOrder:
    def __init__(self):
        self.active = False
        self.token = None


_ORDER = _ProgramOrder()


def _pcall(body, *, in_specs, scalar_prefetch=0, **kw):
    def call(*args):
        dep = _ORDER.token if _ORDER.active else None
        specs, fn = list(in_specs), body
        if dep is not None:
            n = len(args)

            def fn(*refs):
                return body(*refs[:n], *refs[n + 1:])

            specs.append(pl.BlockSpec(memory_space=pl.ANY))
            args = (*args, dep)
        params = dict(kw)
        if scalar_prefetch:
            params["grid_spec"] = pltpu.PrefetchScalarGridSpec(
                num_scalar_prefetch=scalar_prefetch, grid=params.pop("grid"), in_specs=specs,
                out_specs=params.pop("out_specs"))
        else:
            params["in_specs"] = specs
        out = pl.pallas_call(fn, **params)(*args)
        if _ORDER.active:
            _ORDER.token = jax.tree.leaves(out)[-1]
        return out

    return call


def _resident(shape, col_block):
    lead = (0,) * (len(shape) - 1)
    return pl.BlockSpec(shape, lambda *_: (*lead, col_block), pipeline_mode=pl.Buffered(1))


def _sig(x):
    return 1.0 / (1.0 + jnp.exp(-x))


def _dot(a, b):
    return jnp.dot(a, b, preferred_element_type=F32)


def _dot_nt(a, b):
    return lax.dot_general(a, b, (((1,), (1,)), ((), ())), preferred_element_type=F32)


def _ln_fwd(z, g, b):
    mu = jnp.mean(z, axis=-1, keepdims=True)
    xc = z - mu
    var = jnp.mean(xc * xc, axis=-1, keepdims=True)
    rstd = lax.rsqrt(var + EPS)
    xh = xc * rstd
    return xh * g + b, xh, rstd


def _ln_bwd(dy, xh, rstd, g):
    dxh = dy * g
    m1 = jnp.mean(dxh, axis=-1, keepdims=True)
    m2 = jnp.mean(dxh * xh, axis=-1, keepdims=True)
    return rstd * (dxh - m1 - xh * m2)


def _colsum(x):
    return jnp.sum(x, axis=0, keepdims=True)


def _zero_first(first, *refs):
    @pl.when(first)
    def _():
        for ref in refs:
            ref[...] = jnp.zeros_like(ref)


def _rope_tables(T):
    half = DK // 2
    freqs = ROPE_BASE ** (-np.arange(half, dtype=np.float32) / half)
    ang = (np.arange(T, dtype=np.float32)[:, None] * freqs[None, :]).astype(np.float32)
    cos, sin = np.cos(ang), np.sin(ang)
    return (jnp.asarray(np.concatenate([cos, cos], 1), F32),
            jnp.asarray(np.concatenate([-sin, sin], 1), F32))


def _decay_tables():
    h = np.arange(H, dtype=np.float64)
    log_g = np.log(1.0 - np.exp2(-5.0 - h))
    idx = np.arange(CH, dtype=np.float64)
    diff = idx[:, None] - idx[None, :]
    dm = np.where(diff[None] >= 0, np.exp(np.maximum(diff, 0.0)[None] * log_g[:, None, None]), 0.0)
    xi = np.exp((idx[None, :] + 1.0) * log_g[:, None])
    zeta = np.exp((CH - 1.0 - idx)[None, :] * log_g[:, None])
    cd = np.exp(CH * log_g)
    xi_t = np.broadcast_to(xi[:, :, None], (H, CH, DV))
    zeta_t = np.broadcast_to(zeta[:, :, None], (H, CH, DK))
    return (jnp.asarray(dm, F32), jnp.asarray(xi_t, F32), jnp.asarray(zeta_t, F32),
            [float(v) for v in cd])


def _cast_t(x):
    T = x.shape[0]
    tm = min(T, 512)

    def body(x_ref, xb_ref, xt_ref):
        v = x_ref[...]
        xb_ref[...] = v.astype(BF16)
        xt_ref[...] = v.T.astype(BF16)

    return _pcall(
        body, name="cast_t", grid=(T // tm,),
        in_specs=[pl.BlockSpec((tm, D), lambda i: (i, 0))],
        out_specs=[pl.BlockSpec((tm, D), lambda i: (i, 0)), pl.BlockSpec((D, tm), lambda i: (0, i))],
        out_shape=[jax.ShapeDtypeStruct((T, D), BF16), jax.ShapeDtypeStruct((D, T), BF16)],
        compiler_params=_cp("parallel"))(x)


def _ffn_up(xb, wall, og, ou, name):
    T = xb.shape[0]
    tm = min(T, 512)
    assert ou == og + FSP

    def body(x_ref, w_ref, a_ref, b_ref, h_ref):
        x = x_ref[...]
        for s in range(NCHIP):
            cols = slice(s * FSP, (s + 1) * FSP)
            a = _dot(x, w_ref[s, :, 0:FSP])
            b = _dot(x, w_ref[s, :, FSP:2 * FSP])
            a_ref[:, cols] = a.astype(BF16)
            b_ref[:, cols] = b.astype(BF16)
            h_ref[:, cols] = (a * _sig(a) * b).astype(BF16)

    ospec = pl.BlockSpec((tm, FP), lambda i: (i, 0))
    return _pcall(
        body, name=name, grid=(T // tm,),
        in_specs=[pl.BlockSpec((tm, D), lambda i: (i, 0)), _resident((NCHIP, D, 2 * FSP), og // (2 * FSP))],
        out_specs=[ospec] * 3, out_shape=[jax.ShapeDtypeStruct((T, FP), BF16)] * 3,
        compiler_params=_cp("parallel"))(xb, wall)


def _proj_ln(hb, wall, off, res, g, b, coef, name, want_b=True):
    T, K = hb.shape
    ks = K // NCHIP
    tm = min(T, 512)
    sub = min(tm, 256)

    def body(h_ref, w_ref, r_ref, g_ref, b_ref, z_ref, *rest):
        for r0 in range(0, tm, sub):
            r = slice(r0, r0 + sub)
            acc = _dot_nt(h_ref[r, 0:ks], w_ref[0])
            for s in range(1, NCHIP):
                acc += _dot_nt(h_ref[r, s * ks:(s + 1) * ks], w_ref[s])
            z = ALPHA * r_ref[r, :] + coef * acc
            z_ref[r, :] = z
            if want_b:
                y, _, _ = _ln_fwd(z, g_ref[...], b_ref[...])
                y_ref, yb_ref, yt_ref = rest
                y_ref[r, :] = y
                yb_ref[r, :] = y.astype(BF16)
                yt_ref[:, r] = y.T.astype(BF16)

    row = pl.BlockSpec((tm, D), lambda i: (i, 0))
    vec = pl.BlockSpec((1, D), lambda i: (0, 0))
    out_specs = [row]
    out_shape = [jax.ShapeDtypeStruct((T, D), F32)]
    if want_b:
        out_specs += [row, row, pl.BlockSpec((D, tm), lambda i: (0, i))]
        out_shape += [jax.ShapeDtypeStruct((T, D), F32), jax.ShapeDtypeStruct((T, D), BF16),
                      jax.ShapeDtypeStruct((D, T), BF16)]
    return _pcall(
        body, name=name, grid=(T // tm,),
        in_specs=[pl.BlockSpec((tm, K), lambda i: (i, 0)),
                  _resident((NCHIP, D, ks), off // ks), row, vec, vec],
        out_specs=out_specs, out_shape=out_shape,
        compiler_params=_cp("parallel"))(hb, wall, res, g, b)


def _inproj(xb, wall, off, b_in, cos_t, sin_t):
    T = xb.shape[0]
    tm, tn = min(T, 512), 512
    assert off == 0

    def body(x_ref, w_ref, bias_ref, cos_ref, sin_ref, o_ref):
        x = x_ref[...]
        c = cos_ref[...]
        s = sin_ref[...]
        for n0 in range(0, INW, tn):
            chip, c0 = divmod(n0, INS)
            acc = _dot(x, w_ref[chip, :, c0:c0 + tn]) + bias_ref[:, n0:n0 + tn]
            if n0 >= 2 * D:
                o_ref[:, n0:n0 + tn] = acc.astype(BF16)
                continue
            scale = DK ** -0.5 if n0 < D else 1.0
            for hh in range(tn // DK):
                xh = acc[:, hh * DK:(hh + 1) * DK]
                o = (xh * c + pltpu.roll(xh, DK // 2, 1) * s) * scale
                o_ref[:, n0 + hh * DK:n0 + (hh + 1) * DK] = o.astype(BF16)

    return _pcall(
        body, name="inproj", grid=(T // tm,),
        in_specs=[pl.BlockSpec((tm, D), lambda i: (i, 0)),
                  _resident((NCHIP, D, INS), 0),
                  pl.BlockSpec((1, INW), lambda i: (0, 0)),
                  pl.BlockSpec((tm, DK), lambda i: (i, 0)),
                  pl.BlockSpec((tm, DK), lambda i: (i, 0))],
        out_specs=pl.BlockSpec((tm, INW), lambda i: (i, 0)),
        out_shape=jax.ShapeDtypeStruct((T, INW), BF16),
        compiler_params=_cp("parallel"))(xb, wall, b_in, cos_t, sin_t)


RET_CPS = 2


def _retention_fwd(proj, gn_g, dm_t, xi_t, zeta_t, cds):
    T = proj.shape[0]
    n = T // CH
    tr = RET_CPS * CH

    def body(q_ref, k_ref, v_ref, g_ref, gn_ref, dm_ref, xi_ref, zt_ref, r_ref, ri_ref, st_ref, state):
        @pl.when(pl.program_id(0) == 0)
        def _():
            state[...] = jnp.zeros_like(state)

        for h in range(H):
            rows = slice(h * DK, (h + 1) * DK)
            cols = slice(h * DV, (h + 1) * DV)
            s_prev = state[rows, :]
            for j in range(RET_CPS):
                t = slice(j * CH, (j + 1) * CH)
                q = q_ref[t, h * DK:(h + 1) * DK]
                k = k_ref[t, h * DK:(h + 1) * DK]
                v = v_ref[t, cols]
                s_b = s_prev.astype(BF16)
                st_ref[j, rows, :] = s_b
                sc = _dot_nt(q, k) * dm_ref[h]
                r = _dot(sc.astype(BF16), v) + _dot(q, s_b) * xi_ref[h]
                kz = k.astype(F32) * zt_ref[h]
                s_prev = cds[h] * s_prev + _dot(kz.T.astype(BF16), v)
                r_ref[t, cols] = r
                mu = jnp.mean(r, axis=-1, keepdims=True)
                xc = r - mu
                var = jnp.mean(xc * xc, axis=-1, keepdims=True)
                y = xc * lax.rsqrt(var + EPS) * gn_ref[:, cols]
                g = g_ref[t, cols].astype(F32)
                ri_ref[t, cols] = (g * _sig(g) * y).astype(BF16)
            state[rows, :] = s_prev

    full3 = lambda shp: pl.BlockSpec(shp, lambda c: (0, 0, 0))
    return _pcall(
        body, name="retention_fwd", grid=(n // RET_CPS,),
        in_specs=[pl.BlockSpec((tr, D), lambda c: (c, 0)),
                  pl.BlockSpec((tr, D), lambda c: (c, 1)),
                  pl.BlockSpec((tr, VW), lambda c: (c, 1)),
                  pl.BlockSpec((tr, VW), lambda c: (c, 2)),
                  pl.BlockSpec((1, VW), lambda c: (0, 0)),
                  full3((H, CH, CH)), full3((H, CH, DV)), full3((H, CH, DK))],
        out_specs=[pl.BlockSpec((tr, VW), lambda c: (c, 0)), pl.BlockSpec((tr, VW), lambda c: (c, 0)),
                   pl.BlockSpec((RET_CPS, H * DK, DV), lambda c: (c, 0, 0))],
        out_shape=[jax.ShapeDtypeStruct((T, VW), F32), jax.ShapeDtypeStruct((T, VW), BF16),
                   jax.ShapeDtypeStruct((n, H * DK, DV), BF16)],
        scratch_shapes=[pltpu.VMEM((H * DK, DV), F32)],
        compiler_params=_cp("arbitrary"))(proj, proj, proj, proj, gn_g, dm_t, xi_t, zeta_t)


CONV_TT = 256
CONV_SB = 64
CONV_CB = 256


SUB = 8
CONV_PAD = 8


def _glu(a_ref, b_ref, rows=slice(None)):
    a = a_ref[rows, :].astype(F32)
    sb = _sig(b_ref[rows, :].astype(F32))
    return a, sb


def _shift_copies(win, sh, rows):
    win[rows:rows + CONV_PAD, :] = jnp.zeros((CONV_PAD, D), F32)
    for b in range(1, SUB):
        sh[b - 1, :, :] = win[b:b + rows, :]


def _tap(win, sh, start, size, cs):
    a, b = divmod(start, SUB)
    src = win if b == 0 else sh.at[b - 1]
    return src[SUB * a:SUB * a + size, cs]


def _conv_fwd(proj, conv_k, conv_b, ln_g, ln_b):
    T = proj.shape[0]
    tt = min(T, CONV_TT)
    ca, cb = 6 * D // D, 7 * D // D

    def body(a_ref, b_ref, pa_ref, pb_ref, k_ref, cb_ref, g_ref, bb_ref, u1_ref, u3_ref, win, sh):
        i = pl.program_id(0)
        a, sb = _glu(a_ref, b_ref)
        win[HALO:tt + HALO, :] = a * sb
        pa, psb = _glu(pa_ref, pb_ref, slice(tt - HALO, tt))
        win[0:HALO, :] = jnp.where(i > 0, pa * psb, 0.0)
        _shift_copies(win, sh, tt + HALO)
        for c0 in range(0, D, CONV_CB):
            cs = slice(c0, c0 + CONV_CB)
            for r0 in range(0, tt, CONV_SB):
                acc = jnp.zeros((CONV_SB, CONV_CB), F32)
                for w in range(CONV_W):
                    st = r0 + HALO - (CONV_W - 1) + w
                    acc += _tap(win, sh, st, CONV_SB, cs) * k_ref[w:w + 1, cs]
                u1_ref[r0:r0 + CONV_SB, cs] = acc + cb_ref[:, cs]
        u2, _, _ = _ln_fwd(u1_ref[...], g_ref[...], bb_ref[...])
        u3_ref[...] = (u2 * _sig(u2)).astype(BF16)

    vec = pl.BlockSpec((1, D), lambda i: (0, 0))
    row = pl.BlockSpec((tt, D), lambda i: (i, 0))
    return _pcall(
        body, name="conv_fwd", grid=(T // tt,),
        in_specs=[pl.BlockSpec((tt, D), lambda i: (i, ca)), pl.BlockSpec((tt, D), lambda i: (i, cb)),
                  pl.BlockSpec((tt, D), lambda i: (jnp.maximum(i - 1, 0), ca)),
                  pl.BlockSpec((tt, D), lambda i: (jnp.maximum(i - 1, 0), cb)),
                  pl.BlockSpec((CONV_W, D), lambda i: (0, 0)), vec, vec, vec],
        out_specs=[row, row],
        out_shape=[jax.ShapeDtypeStruct((T, D), F32), jax.ShapeDtypeStruct((T, D), BF16)],
        scratch_shapes=[pltpu.VMEM((tt + HALO + CONV_PAD, D), F32), pltpu.VMEM((SUB - 1, tt + HALO, D), F32)],
        compiler_params=_cp("parallel"))(proj, proj, proj, proj, conv_k, conv_b, ln_g, ln_b)


def _merge(ret_in, u3, proj, wall, off_r, off_c):
    T = ret_in.shape[0]
    tm = min(T, 512)
    kr, kc = VW // NCHIP, D // NCHIP

    def body(ri_ref, u3_ref, gr_ref, gc_ref, wr_ref, wc_ref, ro_ref, co_ref, m_ref):
        ro = _dot_nt(ri_ref[:, 0:kr], wr_ref[0])
        co = _dot_nt(u3_ref[:, 0:kc], wc_ref[0])
        for s in range(1, NCHIP):
            ro += _dot_nt(ri_ref[:, s * kr:(s + 1) * kr], wr_ref[s])
            co += _dot_nt(u3_ref[:, s * kc:(s + 1) * kc], wc_ref[s])
        ro_ref[...] = ro.astype(BF16)
        co_ref[...] = co.astype(BF16)
        m = _sig(gr_ref[...].astype(F32)) * ro + _sig(gc_ref[...].astype(F32)) * co
        m_ref[...] = m.astype(BF16)

    row = pl.BlockSpec((tm, D), lambda i: (i, 0))
    return _pcall(
        body, name="merge", grid=(T // tm,),
        in_specs=[pl.BlockSpec((tm, VW), lambda i: (i, 0)), row,
                  pl.BlockSpec((tm, D), lambda i: (i, 8)), pl.BlockSpec((tm, D), lambda i: (i, 9)),
                  pl.BlockSpec((NCHIP, D, kr), lambda i: (0, 0, off_r // kr)),
                  pl.BlockSpec((NCHIP, D, kc), lambda i: (0, 0, off_c // kc))],
        out_specs=[row] * 3, out_shape=[jax.ShapeDtypeStruct((T, D), BF16)] * 3,
        compiler_params=_cp("parallel"))(ret_in, u3, proj, proj, wall, wall)


def _loss_ln_bwd(z, g, b, target, coef):
    T = z.shape[0]
    tm = min(T, 256)
    nt = T // tm

    def body(z_ref, g_ref, b_ref, t_ref, loss_ref, dzb_ref, dzt_ref, dz_ref, dg_ref, db_ref, lacc):
        i = pl.program_id(0)
        _zero_first(i == 0, lacc, dg_ref, db_ref)
        gam = g_ref[...]
        y, xh, rstd = _ln_fwd(z_ref[...], gam, b_ref[...])
        e = y - t_ref[...]
        lacc[...] += _colsum(e * e)
        dy = e * (1.0 / D)
        dz = _ln_bwd(dy, xh, rstd, gam)
        dz_ref[...] = dz
        dzc = coef * dz
        dzb_ref[...] = dzc.astype(BF16)
        dzt_ref[...] = dzc.T.astype(BF16)
        dg_ref[...] += _colsum(dy * xh)
        db_ref[...] += _colsum(dy)

        @pl.when(i == nt - 1)
        def _():
            loss_ref[...] = (0.5 / D) * jnp.sum(lacc[...], axis=1, keepdims=True)

    row = pl.BlockSpec((tm, D), lambda i: (i, 0))
    vec = pl.BlockSpec((1, D), lambda i: (0, 0))
    return _pcall(
        body, name="loss_ln_bwd", grid=(nt,),
        in_specs=[row, vec, vec, row],
        out_specs=[pl.BlockSpec((1, 1), lambda i: (0, 0)), row, pl.BlockSpec((D, tm), lambda i: (0, i)),
                   row, vec, vec],
        out_shape=[jax.ShapeDtypeStruct((1, 1), F32), jax.ShapeDtypeStruct((T, D), BF16),
                   jax.ShapeDtypeStruct((D, T), BF16), jax.ShapeDtypeStruct((T, D), F32),
                   jax.ShapeDtypeStruct((1, D), F32), jax.ShapeDtypeStruct((1, D), F32)],
        scratch_shapes=[pltpu.VMEM((1, D), F32)],
        compiler_params=_cp("arbitrary"))(z, g, b, target)


def _ffn_bwd_h(dfb, wall, od, a, b, name):
    T = dfb.shape[0]
    tm = min(T, 512)

    def body(d_ref, w_ref, a_ref, b_ref, da_ref, db_ref):
        d = d_ref[...]
        for s in range(NCHIP):
            cols = slice(s * FSP, (s + 1) * FSP)
            dh = _dot(d, w_ref[s])
            a = a_ref[:, cols].astype(F32)
            sg = _sig(a)
            da_ref[:, cols] = (dh * b_ref[:, cols].astype(F32) * (sg * (1.0 + a * (1.0 - sg)))).astype(BF16)
            db_ref[:, cols] = (dh * a * sg).astype(BF16)

    ospec = pl.BlockSpec((tm, FP), lambda i: (i, 0))
    return _pcall(
        body, name=name, grid=(T // tm,),
        in_specs=[pl.BlockSpec((tm, D), lambda i: (i, 0)), _resident((NCHIP, D, FSP), od // FSP), ospec, ospec],
        out_specs=[ospec] * 2, out_shape=[jax.ShapeDtypeStruct((T, FP), BF16)] * 2,
        compiler_params=_cp("parallel"))(dfb, wall, a, b)


DX_SUB = 256


def _dx_partial(lhs, wall, chips, name):
    T, K = lhs.shape
    ks = K // NCHIP
    nc = len(chips)
    assert list(chips) == list(range(chips[0], chips[0] + nc)) and chips[0] % nc == 0
    tm = min(T, 512)

    def body(l_ref, w_ref, o_ref):
        for r0 in range(0, tm, DX_SUB):
            r = slice(r0, r0 + DX_SUB)
            acc = _dot_nt(l_ref[r, 0:ks], w_ref[0])
            for s in range(1, nc):
                acc += _dot_nt(l_ref[r, s * ks:(s + 1) * ks], w_ref[s])
            o_ref[r, :] = acc

    blk = chips[0] // nc
    return _pcall(
        body, name=name, grid=(T // tm,),
        in_specs=[pl.BlockSpec((tm, nc * ks), lambda i: (i, blk)),
                  pl.BlockSpec((nc, D, ks), lambda i: (blk, 0, 0), pipeline_mode=pl.Buffered(1))],
        out_specs=pl.BlockSpec((tm, D), lambda i: (i, 0)),
        out_shape=jax.ShapeDtypeStruct((T, D), F32),
        compiler_params=_cp("parallel"))(lhs, wall)


def _dx_bwd(lhs, offs, wall, dz_next, name, ln=None, chips=tuple(range(NCHIP)), partial=None):
    T, K = lhs[0].shape
    ks = K // NCHIP
    nl = len(lhs)
    nc = len(chips)
    assert list(offs) == [l * ks for l in range(nl)]
    assert list(chips) == list(range(chips[0], chips[0] + nc)) and chips[0] % nc == 0
    blk = chips[0] // nc
    tm = min(T, 512)

    def body(*refs):
        l_refs = refs[:nl]
        w_ref = refs[nl]
        dzn_ref = refs[nl + 1]
        pos = nl + 2
        if partial is not None:
            p_ref = refs[pos]
            pos += 1
        if ln is not None:
            z_ref, g_ref = refs[pos:pos + 2]
            pos += 2
        outs = refs[pos:]
        sums = list(outs[3:]) if ln is not None else list(outs[1:])

        _zero_first(pl.program_id(0) == 0, *sums)

        for r0 in range(0, tm, DX_SUB):
            r = slice(r0, r0 + DX_SUB)
            acc = None if partial is None else p_ref[r, :]
            for s in range(nc):
                rows = slice(s * ks, (s + 1) * ks)
                for l in range(nl):
                    part = _dot_nt(l_refs[l][r, rows], w_ref[s, :, l * ks:(l + 1) * ks])
                    acc = part if acc is None else acc + part
            dy = acc + ALPHA * dzn_ref[r, :]
            if ln is None:
                outs[0][r, :] = dy
            else:
                gam = g_ref[...]
                _, xh, rstd = _ln_fwd(z_ref[r, :], gam, 0.0)
                dz = _ln_bwd(dy, xh, rstd, gam)
                dzc = ln[2] * dz
                outs[0][r, :] = dzc.astype(BF16)
                outs[1][:, r] = dzc.T.astype(BF16)
                outs[2][r, :] = dz
                outs[3][...] += _colsum(dy * xh)
                outs[4][...] += _colsum(dy)

    row = pl.BlockSpec((tm, D), lambda i: (i, 0))
    vec = pl.BlockSpec((1, D), lambda i: (0, 0))
    in_specs = [pl.BlockSpec((tm, nc * ks), lambda i: (i, blk))] * nl
    in_specs += [pl.BlockSpec((nc, D, nl * ks), lambda i: (blk, 0, 0), pipeline_mode=pl.Buffered(1)), row]
    args = list(lhs) + [wall, dz_next]
    if partial is not None:
        in_specs.append(row)
        args.append(partial)
    if ln is None:
        out_specs = [row]
        out_shape = [jax.ShapeDtypeStruct((T, D), F32)]
    else:
        in_specs += [row, vec]
        args += [ln[0], ln[1]]
        out_specs = [row, pl.BlockSpec((D, tm), lambda i: (0, i)), row, vec, vec]
        out_shape = [jax.ShapeDtypeStruct((T, D), BF16), jax.ShapeDtypeStruct((D, T), BF16),
                     jax.ShapeDtypeStruct((T, D), F32), jax.ShapeDtypeStruct((1, D), F32),
                     jax.ShapeDtypeStruct((1, D), F32)]
    return _pcall(
        body, name=name, grid=(T // tm,), in_specs=in_specs, out_specs=out_specs, out_shape=out_shape,
        compiler_params=_cp("arbitrary"))(*args)


def _wgrad(lhs_t, rhs, key, name, g_all=None, colsum=False):
    T, N = rhs.shape
    tn = next(c for c in (768, 512, 256) if (N // NCHIP) % c == 0 and LOC[key][1] % c == 0)
    nps = N // NCHIP // tn
    off = LOC[key][1]
    cols = GCOLS[LOC[key][0]]

    def body(*refs):
        l_ref, r_ref = refs[0], refs[1]
        o_ref, t_ref = refs[-2 - colsum], refs[-1]
        o_ref[...] = _dot(l_ref[...], r_ref[...]).astype(BF16)
        if colsum:
            refs[-2][...] = _colsum(r_ref[...].astype(F32))
        t_ref[...] = jnp.zeros_like(t_ref)

    in_specs = [_resident((D, T), 0), pl.BlockSpec((T, tn), lambda j: (0, j))]
    args = [lhs_t, rhs]
    aliases = {}
    if g_all is not None:
        in_specs.append(pl.BlockSpec(memory_space=pl.ANY))
        args.append(g_all)
        aliases = {2: 0}
    out_specs = [pl.BlockSpec((None, D, tn), lambda j: (j // nps, 0, off // tn + j % nps))]
    out_shape = [jax.ShapeDtypeStruct((NCHIP, D, cols), BF16)]
    if colsum:
        out_specs.append(pl.BlockSpec((1, tn), lambda j: (0, j)))
        out_shape.append(jax.ShapeDtypeStruct((1, N), F32))
    out_specs.append(pl.BlockSpec((8, 128), lambda j: (0, 0)))
    out_shape.append(jax.ShapeDtypeStruct((8, 128), F32))
    res = _pcall(
        body, name=name, grid=(N // tn,), in_specs=in_specs, out_specs=out_specs, out_shape=out_shape,
        input_output_aliases=aliases,
        compiler_params=_cp("arbitrary"))(*args)
    return (res[0], res[1]) if colsum else res[0]


def _merge_bwd(dmb, wall, off, proj, ro, co):
    T = dmb.shape[0]
    tm = min(T, 512)
    ks = D // NCHIP

    def body(d_ref, w_ref, gr_ref, gc_ref, ro_ref, co_ref, dro_ref, drot_ref, dco_ref, dcot_ref, dp_ref):
        d = d_ref[...]
        dmg = jnp.concatenate([_dot(d, w_ref[s]) for s in range(NCHIP)], axis=1)
        sr = _sig(gr_ref[...].astype(F32))
        sc = _sig(gc_ref[...].astype(F32))
        dro = dmg * sr
        dco = dmg * sc
        dro_ref[...] = dro.astype(BF16)
        drot_ref[...] = dro.T.astype(BF16)
        dco_ref[...] = dco.astype(BF16)
        dcot_ref[...] = dco.T.astype(BF16)
        dp_ref[:, 0:D] = (dmg * ro_ref[...].astype(F32) * sr * (1.0 - sr)).astype(BF16)
        dp_ref[:, D:2 * D] = (dmg * co_ref[...].astype(F32) * sc * (1.0 - sc)).astype(BF16)

    row = pl.BlockSpec((tm, D), lambda i: (i, 0))
    col = pl.BlockSpec((D, tm), lambda i: (0, i))
    return _pcall(
        body, name="merge_bwd", grid=(T // tm,),
        in_specs=[row, pl.BlockSpec((NCHIP, D, ks), lambda i: (0, 0, off // ks)),
                  pl.BlockSpec((tm, D), lambda i: (i, 8)), pl.BlockSpec((tm, D), lambda i: (i, 9)), row, row],
        out_specs=[row, col, row, col, pl.BlockSpec((tm, 2 * D), lambda i: (i, 4))],
        out_shape=[jax.ShapeDtypeStruct((T, D), BF16), jax.ShapeDtypeStruct((D, T), BF16),
                   jax.ShapeDtypeStruct((T, D), BF16), jax.ShapeDtypeStruct((D, T), BF16),
                   jax.ShapeDtypeStruct((T, INW), BF16)],
        compiler_params=_cp("parallel"))(dmb, wall, proj, proj, ro, co)


def _reto_bwd(dro, wall, off, r, proj, gn_g, dproj):
    T = dro.shape[0]
    tm = min(T, 512)
    hps = H // NCHIP

    def body(d_ref, w_ref, r_ref, g_ref, gn_ref, _, dr_ref, dgn_ref, dp_ref):
        _zero_first(pl.program_id(1) == 0, dgn_ref)
        dri = _dot(d_ref[...], w_ref[...])
        rr = r_ref[...]
        mu = jnp.mean(rr, axis=-1, keepdims=True)
        xc = rr - mu
        var = jnp.mean(xc * xc, axis=-1, keepdims=True)
        rstd = lax.rsqrt(var + EPS)
        rn = xc * rstd
        gn = gn_ref[...]
        g = g_ref[...].astype(F32)
        sg = _sig(g)
        dy = dri * (g * sg)
        dp_ref[...] = (dri * (rn * gn) * (sg * (1.0 + g * (1.0 - sg)))).astype(BF16)
        dgn_ref[...] += _colsum(dy * rn)
        dr_ref[...] = _ln_bwd(dy, rn, rstd, gn).astype(BF16)

    return _pcall(
        body, name="reto_bwd", grid=(H, T // tm),
        in_specs=[pl.BlockSpec((tm, D), lambda j, i: (i, 0)),
                  pl.BlockSpec((None, D, DV), lambda j, i: (j // hps, 0, off // DV + j % hps)),
                  pl.BlockSpec((tm, DV), lambda j, i: (i, j)),
                  pl.BlockSpec((tm, DV), lambda j, i: (i, 2 * VW // DV + j)),
                  pl.BlockSpec((1, DV), lambda j, i: (0, j)),
                  pl.BlockSpec(memory_space=pl.ANY)],
        out_specs=[pl.BlockSpec((tm, DV), lambda j, i: (i, j)), pl.BlockSpec((1, DV), lambda j, i: (0, j)),
                   pl.BlockSpec((tm, DV), lambda j, i: (i, 2 * VW // DV + j))],
        out_shape=[jax.ShapeDtypeStruct((T, VW), BF16), jax.ShapeDtypeStruct((1, VW), F32),
                   jax.ShapeDtypeStruct((T, INW), BF16)],
        input_output_aliases={5: 2},
        compiler_params=_cp("arbitrary", "arbitrary"))(dro, wall, r, proj, gn_g, dproj)


def _retention_bwd(proj, dr, states, cos_t, sin_t, dm_t, xi_t, zeta_t, cds, dproj):
    T = proj.shape[0]
    n = T // CH // RET_CPS
    tr = RET_CPS * CH
    scale = DK ** -0.5

    def body(q_ref, k_ref, v_ref, dr_ref, st_ref, cos_ref, sin_ref, dm_ref, xi_ref, zt_ref, _, dp_ref, ds):
        @pl.when(pl.program_id(0) == 0)
        def _():
            ds[...] = jnp.zeros_like(ds)

        def unrope(d, t):
            return d * cos_ref[t, :] + pltpu.roll(d * sin_ref[t, :], DK // 2, 1)

        for h in range(H):
            rows = slice(h * DK, (h + 1) * DK)
            dm = dm_ref[h]
            zt = zt_ref[h]
            ds_prev = ds[rows, :]
            for j in reversed(range(RET_CPS)):
                t = slice(j * CH, (j + 1) * CH)
                q = q_ref[t, h * DK:(h + 1) * DK]
                k = k_ref[t, h * DK:(h + 1) * DK]
                v = v_ref[t, h * DV:(h + 1) * DV]
                d_r = dr_ref[t, h * DV:(h + 1) * DV]
                s_b = st_ref[j, rows, :]
                sc = _dot_nt(q, k) * dm
                dsc = _dot_nt(d_r, v) * dm
                drx = (d_r.astype(F32) * xi_ref[h]).astype(BF16)
                ds_b = ds_prev.astype(BF16)
                kz = (k.astype(F32) * zt).astype(BF16)
                dq = _dot(dsc.astype(BF16), k) + _dot_nt(drx, s_b)
                dk = _dot(dsc.T.astype(BF16), q) + _dot_nt(v, ds_b) * zt
                dv = _dot(sc.T.astype(BF16), d_r) + _dot(kz, ds_b)
                ds_prev = cds[h] * ds_prev + _dot(q.astype(F32).T.astype(BF16), drx)
                dp_ref[t, h * DK:(h + 1) * DK] = unrope(dq * scale, t).astype(BF16)
                dp_ref[t, D + h * DK:D + (h + 1) * DK] = unrope(dk, t).astype(BF16)
                dp_ref[t, 2 * D + h * DV:2 * D + (h + 1) * DV] = dv.astype(BF16)
            ds[rows, :] = ds_prev

    rv = lambda c: n - 1 - c
    full3 = lambda shp: pl.BlockSpec(shp, lambda c: (0, 0, 0))
    return _pcall(
        body, name="retention_bwd", grid=(n,),
        in_specs=[pl.BlockSpec((tr, D), lambda c: (rv(c), 0)),
                  pl.BlockSpec((tr, D), lambda c: (rv(c), 1)),
                  pl.BlockSpec((tr, VW), lambda c: (rv(c), 1)),
                  pl.BlockSpec((tr, VW), lambda c: (rv(c), 0)),
                  pl.BlockSpec((RET_CPS, H * DK, DV), lambda c: (rv(c), 0, 0)),
                  pl.BlockSpec((tr, DK), lambda c: (rv(c), 0)),
                  pl.BlockSpec((tr, DK), lambda c: (rv(c), 0)),
                  full3((H, CH, CH)), full3((H, CH, DV)), full3((H, CH, DK)),
                  pl.BlockSpec(memory_space=pl.ANY)],
        out_specs=pl.BlockSpec((tr, 2 * D + VW), lambda c: (rv(c), 0)),
        out_shape=jax.ShapeDtypeStruct((T, INW), BF16),
        input_output_aliases={10: 0},
        scratch_shapes=[pltpu.VMEM((H * DK, DV), F32)],
        compiler_params=_cp("arbitrary"))(proj, proj, proj, dr, states, cos_t, sin_t, dm_t, xi_t, zeta_t, dproj)


def _convo_bwd(dco, wall, off, u1, ln_g, ln_b):
    T = dco.shape[0]
    tm = min(T, 512)
    ks = D // NCHIP

    def body(d_ref, w_ref, u1_ref, g_ref, b_ref, du1_ref, dg_ref, db_ref, dcb_ref):
        _zero_first(pl.program_id(0) == 0, dg_ref, db_ref, dcb_ref)
        d = d_ref[...]
        du3 = jnp.concatenate([_dot(d, w_ref[s]) for s in range(NCHIP)], axis=1)
        gam = g_ref[...]
        u2, xh, rstd = _ln_fwd(u1_ref[...], gam, b_ref[...])
        sg = _sig(u2)
        du2 = du3 * (sg * (1.0 + u2 * (1.0 - sg)))
        du1 = _ln_bwd(du2, xh, rstd, gam)
        du1_ref[...] = du1
        dg_ref[...] += _colsum(du2 * xh)
        db_ref[...] += _colsum(du2)
        dcb_ref[...] += _colsum(du1)

    row = pl.BlockSpec((tm, D), lambda i: (i, 0))
    vec = pl.BlockSpec((1, D), lambda i: (0, 0))
    return _pcall(
        body, name="convo_bwd", grid=(T // tm,),
        in_specs=[row, pl.BlockSpec((NCHIP, D, ks), lambda i: (0, 0, off // ks)), row, vec, vec],
        out_specs=[row, vec, vec, vec],
        out_shape=[jax.ShapeDtypeStruct((T, D), F32)] + [jax.ShapeDtypeStruct((1, D), F32)] * 3,
        compiler_params=_cp("arbitrary"))(dco, wall, u1, ln_g, ln_b)


def _conv_bwd(du1, proj, conv_k, dproj):
    T = du1.shape[0]
    tt = min(T, CONV_TT)
    nt = T // tt
    ca, cb = 6, 7

    def body(d_ref, dn_ref, a_ref, b_ref, pa_ref, pb_ref, k_ref, _, dp_ref, dk_ref, win_u, win_d, sh_u, sh_d):
        i = pl.program_id(0)
        a, sb = _glu(a_ref, b_ref)
        win_u[HALO:tt + HALO, :] = a * sb
        pa, psb = _glu(pa_ref, pb_ref, slice(tt - HALO, tt))
        win_u[0:HALO, :] = jnp.where(i > 0, pa * psb, 0.0)
        win_d[0:tt, :] = d_ref[...]
        win_d[tt:tt + HALO, :] = jnp.where(i < nt - 1, dn_ref[0:HALO, :], 0.0)
        _shift_copies(win_u, sh_u, tt + HALO)
        _shift_copies(win_d, sh_d, tt + HALO)

        @pl.when(i == 0)
        def _():
            dk_ref[...] = jnp.zeros_like(dk_ref)

        for c0 in range(0, D, CONV_CB):
            cs = slice(c0, c0 + CONV_CB)
            for r0 in range(0, tt, CONV_SB):
                acc = jnp.zeros((CONV_SB, CONV_CB), F32)
                for w in range(CONV_W):
                    st = r0 + (CONV_W - 1) - w
                    acc += _tap(win_d, sh_d, st, CONV_SB, cs) * k_ref[w:w + 1, cs]
                aa = a_ref[r0:r0 + CONV_SB, cs].astype(F32)
                ss = _sig(b_ref[r0:r0 + CONV_SB, cs].astype(F32))
                dp_ref[r0:r0 + CONV_SB, cs] = (acc * ss).astype(BF16)
                dp_ref[r0:r0 + CONV_SB, c0 + D:c0 + D + CONV_CB] = (acc * aa * ss * (1.0 - ss)).astype(BF16)
        for c0 in range(0, D, CONV_CB):
            cs = slice(c0, c0 + CONV_CB)
            for w in range(CONV_W):
                acc = jnp.zeros((CONV_SB, CONV_CB), F32)
                for r0 in range(0, tt, CONV_SB):
                    st = r0 + HALO - (CONV_W - 1) + w
                    acc += win_d[r0:r0 + CONV_SB, cs] * _tap(win_u, sh_u, st, CONV_SB, cs)
                dk_ref[w:w + 1, cs] += _colsum(acc)

    blk = lambda f, c: pl.BlockSpec((tt, D), lambda i: (f(i), c))
    cur = lambda i: i
    prv = lambda i: jnp.maximum(i - 1, 0)
    nxt = lambda i: jnp.minimum(i + 1, nt - 1)
    return _pcall(
        body, name="conv_bwd", grid=(nt,),
        in_specs=[blk(cur, 0), blk(nxt, 0), blk(cur, ca), blk(cur, cb), blk(prv, ca), blk(prv, cb),
                  pl.BlockSpec((CONV_W, D), lambda i: (0, 0)), pl.BlockSpec(memory_space=pl.ANY)],
        out_specs=[pl.BlockSpec((tt, 2 * D), lambda i: (i, 3)), pl.BlockSpec((HALO, D), lambda i: (0, 0))],
        out_shape=[jax.ShapeDtypeStruct((T, INW), BF16), jax.ShapeDtypeStruct((HALO, D), F32)],
        input_output_aliases={7: 0},
        scratch_shapes=[pltpu.VMEM((tt + HALO + CONV_PAD, D), F32), pltpu.VMEM((tt + HALO + CONV_PAD, D), F32),
                        pltpu.VMEM((SUB - 1, tt + HALO, D), F32), pltpu.VMEM((SUB - 1, tt + HALO, D), F32)],
        compiler_params=_cp("arbitrary"))(du1, du1, proj, proj, proj, proj, conv_k, dproj)


def _local_step(x, target, wts, sp, kvec, pos_c, pos_sc, adam):
    T = x.shape[0]
    cos_t, sin_t = _rope_tables(T)
    dm_t, xi_t, zeta_t, cds = _decay_tables()
    wa = lambda key: wts[LOC[key][0]]
    wo = lambda key: LOC[key][1]
    _ORDER.active, _ORDER.token = True, None

    xb, xt = _cast_t(x)
    a1, b1, h1 = _ffn_up(xb, wa("g1"), wo("g1"), wo("u1"), "ffn1_up")
    z1, x1, x1b, x1t = _proj_ln(h1, wa("d1"), wo("d1"), x, sp["ln1_g"], sp["ln1_b"], 0.5, "ffn1_down_ln")
    proj = _inproj(x1b, wa("w_in"), wo("w_in"), sp["b_in"], cos_t, sin_t)
    r, ret_in, states = _retention_fwd(proj, sp["ret_gn_g"], dm_t, xi_t, zeta_t, cds)
    kall = _sum_devices(_all_gather_small(kvec, "gather_conv_k"), "sum_conv_k")
    sp = dict(sp, conv_k=kall.reshape(-1)[:CONV_W * D].reshape(CONV_W, D))
    u1, u3 = _conv_fwd(proj, sp["conv_k"], sp["conv_b"], sp["conv_ln_g"], sp["conv_ln_b"])
    ro, co, merged = _merge(ret_in, u3, proj, wa("w_ret_o"), wo("w_ret_o"), wo("w_conv_o"))
    z2, x2, x2b, x2t = _proj_ln(merged, wa("w_out"), wo("w_out"), x1, sp["ln2_g"], sp["ln2_b"], 1.0, "out_proj_ln")
    a2, b2, h2 = _ffn_up(x2b, wa("g2"), wo("g2"), wo("u2"), "ffn2_up")
    (z3,) = _proj_ln(h2, wa("d2"), wo("d2"), x2, sp["ln3_g"], sp["ln3_b"], 0.5, "ffn2_down", want_b=False)

    sg = {}
    rs = {}
    loss, df2b, df2t, dz3, sg["ln3_g"], sg["ln3_b"] = _loss_ln_bwd(z3, sp["ln3_g"], sp["ln3_b"], target, 0.5)
    da2, db2 = _ffn_bwd_h(df2b, wa("d2"), wo("d2"), a2, b2, "ffn2_bwd_h")
    g4 = _wgrad(df2t, h2, "d2", "wgrad_d2")
    g4 = _wgrad(x2t, da2, "g2", "wgrad_g2", g4)
    g4 = _wgrad(x2t, db2, "u2", "wgrad_u2", g4)
    rs[4] = _ReduceScatter(g4, 4, pos_c, pos_sc)
    dmb, dmt, dz2, sg["ln2_g"], sg["ln2_b"] = _dx_bwd(
        [da2, db2], [wo("g2"), wo("u2")], wa("g2"), dz3, "ffn2_dx_ln", ln=(z2, sp["ln2_g"], 1.0))
    rs[4].phase2()
    g3 = _wgrad(dmt, merged, "w_out", "wgrad_out")
    dro, drot, dco, dcot, dproj = _merge_bwd(dmb, wa("w_out"), wo("w_out"), proj, ro, co)
    g3 = _wgrad(drot, ret_in, "w_ret_o", "wgrad_ret_o", g3)
    g3 = _wgrad(dcot, u3, "w_conv_o", "wgrad_conv_o", g3)
    rs[3] = _ReduceScatter(g3, 3, pos_c, pos_sc)
    dr, sg["ret_gn_g"], dproj = _reto_bwd(dro, wa("w_ret_o"), wo("w_ret_o"), r, proj, sp["ret_gn_g"], dproj)
    rs[4].phase3()
    rs[3].phase2()
    dproj = _retention_bwd(proj, dr, states, cos_t, sin_t, dm_t, xi_t, zeta_t, cds, dproj)
    du1, sg["conv_ln_g"], sg["conv_ln_b"], sg["conv_b"] = _convo_bwd(
        dco, wa("w_conv_o"), wo("w_conv_o"), u1, sp["conv_ln_g"], sp["conv_ln_b"])
    dproj, dck = _conv_bwd(du1, proj, sp["conv_k"], dproj)
    sg["conv_k"] = dck[:CONV_W]
    adam(4, rs[4].result())
    rs[3].phase3()
    g2, sg["b_in"] = _wgrad(x1t, dproj, "w_in", "wgrad_in", colsum=True)
    rs[2] = _ReduceScatter(g2, 2, pos_c, pos_sc)
    dx_part = _dx_partial(dproj, wa("w_in"), (0, 1), "mixer_dx_part")
    df1b, df1t, dz1, sg["ln1_g"], sg["ln1_b"] = _dx_bwd(
        [dproj], [wo("w_in")], wa("w_in"), dz2, "mixer_dx_ln", ln=(z1, sp["ln1_g"], 0.5), chips=(2, 3),
        partial=dx_part)
    adam(3, rs[3].result())
    rs[2].phase2()
    shapes = {n: sg[n].shape for n in SMALL + ["conv_k"]}
    small_parts = _all_gather_small(_pack_small(sg, loss, SMALL_ROWS), "gather_small")
    da1, db1 = _ffn_bwd_h(df1b, wa("d1"), wo("d1"), a1, b1, "ffn1_bwd_h")
    small_sum = _sum_devices(small_parts, "sum_small")
    g1 = _wgrad(df1t, h1, "d1", "wgrad_d1")
    rs[1] = _ReduceScatter(g1, 1, pos_c, pos_sc)
    g0 = _wgrad(xt, da1, "g1", "wgrad_g1")
    g0 = _wgrad(xt, db1, "u1", "wgrad_u1", g0)
    rs[0] = _ReduceScatter(g0, 0, pos_c, pos_sc)
    rs[2].phase3()
    rs[1].phase2()
    rs[0].phase2()
    (grad_x,) = _dx_bwd([da1, db1], [wo("g1"), wo("u1")], wa("g1"), dz1, "ffn1_dx")
    adam(2, rs[2].result())
    rs[1].phase3()
    rs[0].phase3()
    adam(1, rs[1].result())
    adam(0, rs[0].result())
    _ORDER.active = False
    return grad_x, small_sum, shapes


MESH = pl.DeviceIdType.MESH
ANY = pl.BlockSpec(memory_space=pl.ANY)
HALF = D // 2


def _place():
    x, y, c = lax.axis_index("x"), lax.axis_index("y"), lax.axis_index("c")
    chips = [(1 - x, y), (x, 1 - y), (1 - x, 1 - y)]
    return x, y, c, chips


GATHER_ID = 1


def _gather_weights(wloc, name):
    w_ref = jax.new_ref(wloc, memory_space=pltpu.MemorySpace.HBM)
    o_ref = jax.empty_ref(jax.ShapeDtypeStruct((NCHIP, 2, HALF, wloc.shape[-1]), BF16),
                          memory_space=pltpu.MemorySpace.HBM)
    dma = pltpu.SemaphoreType.DMA

    @pl.kernel(mesh=plsc.ScalarSubcoreMesh(axis_name="sc", num_cores=1), name=name,
               scratch_types=(dma(()), dma((2,)), dma((2,)), dma((3,)), dma((3,)), dma(()), dma(())),
               compiler_params=pltpu.CompilerParams(collective_id=GATHER_ID))
    def launch(lsem, s1, r1, s2, r2, s3, r3):
        x, y, c, _ = _place()
        me = 2 * x + y
        sib = (x, y, 1 - c)
        x_nbr, y_nbr = (1 - x, y, c), (x, 1 - y, c)
        x_chip, y_chip, d_chip = 2 * (1 - x) + y, 2 * x + (1 - y), 2 * (1 - x) + (1 - y)
        _handshake([sib, x_nbr, y_nbr])
        mine = pltpu.make_async_copy(w_ref, o_ref.at[me], lsem)
        mine.start()

        def rc(src, dst, ss, rs, dev):
            return pltpu.make_async_remote_copy(src_ref=src, dst_ref=dst, send_sem=ss, recv_sem=rs,
                                                device_id=dev, device_id_type=MESH)

        first = [rc(w_ref.at[c], o_ref.at[me, c], s1.at[0], r1.at[0], x_nbr),
                 rc(w_ref.at[c], o_ref.at[me, c], s1.at[1], r1.at[1], y_nbr)]
        for cp in first:
            cp.start()
        on_chip = c * x_chip + (1 - c) * y_chip
        other_chip = c * y_chip + (1 - c) * x_chip
        on_to = (c * x + (1 - c) * (1 - x), c * (1 - y) + (1 - c) * y, c)
        slot = o_ref.at[on_chip, c]
        rc(slot, slot, s1.at[1 - c], r1.at[1 - c], sib).wait_recv()
        onward = rc(slot, slot, s3, r3, on_to)
        onward.start()
        passed = [rc(slot, slot, s2.at[0], r2.at[0], sib)]
        passed[0].start()
        slot = o_ref.at[other_chip, c]
        rc(slot, slot, s1.at[c], r1.at[c], sib).wait_recv()
        passed.append(rc(slot, slot, s2.at[1], r2.at[1], sib))
        passed[1].start()
        slot = o_ref.at[d_chip, c]
        rc(slot, slot, s3, r3, sib).wait_recv()
        passed.append(rc(slot, slot, s2.at[2], r2.at[2], sib))
        passed[2].start()
        for j, chip in enumerate([other_chip, on_chip, d_chip]):
            slot = o_ref.at[chip, 1 - c]
            rc(slot, slot, s2.at[j], r2.at[j], sib).wait_recv()
        for cp in first + [onward] + passed:
            cp.wait_send()
        mine.wait()

    launch()
    return o_ref[...]


PAIR_ID = 2
CHIP_ID = 3
HBM = pltpu.MemorySpace.HBM


def _sequencer(name, collective_id, n_sems):
    dma = pltpu.SemaphoreType.DMA
    return pl.kernel(mesh=plsc.ScalarSubcoreMesh(axis_name="sc", num_cores=1), name=name,
                     scratch_types=(dma((n_sems,)), dma((n_sems,))),
                     compiler_params=pltpu.CompilerParams(collective_id=collective_id))


def _handshake(peers):
    barrier = pltpu.get_barrier_semaphore()
    for peer in peers:
        pl.semaphore_signal(barrier, inc=1, device_id=peer, device_id_type=MESH)
    pl.semaphore_wait(barrier, len(peers))


def _pair_exchange(g5, name):
    _, _, hr, cols = g5.shape
    g_ref = jax.new_ref(g5, memory_space=HBM)
    o_ref = jax.empty_ref(jax.ShapeDtypeStruct((NCHIP, hr, cols), g5.dtype), memory_space=HBM)

    @_sequencer(name, PAIR_ID, NCHIP)
    def launch(ss, rs):
        x, y, c, _ = _place()
        sib = (x, y, 1 - c)
        _handshake([sib])
        cps = [pltpu.make_async_remote_copy(src_ref=g_ref.at[j, 1 - c], dst_ref=o_ref.at[j], send_sem=ss.at[j],
                                            recv_sem=rs.at[j], device_id=sib, device_id_type=MESH)
               for j in range(NCHIP)]
        for cp in cps:
            cp.start()
        for cp in cps:
            cp.wait()

    launch()
    return o_ref[...]


RS_TR = 256


def _pair_sum(pos, g5, got, name):
    _, _, hr, cols = g5.shape

    def body(pos_ref, g_ref, r_ref, o_ref):
        o_ref[...] = (g_ref[...].astype(F32) + r_ref[...].astype(F32)).astype(BF16)

    return _pcall(
        body, name=name, scalar_prefetch=1, grid=(NCHIP, hr // RS_TR),
        in_specs=[pl.BlockSpec((None, None, RS_TR, cols), lambda j, i, p: (j, p[0], i, 0)),
                  pl.BlockSpec((None, RS_TR, cols), lambda j, i, p: (j, i, 0))],
        out_specs=pl.BlockSpec((None, RS_TR, cols), lambda j, i, p: (j, i, 0)),
        out_shape=jax.ShapeDtypeStruct((NCHIP, hr, cols), BF16),
        compiler_params=_cp("parallel", "parallel"))(pos, g5, got)


def _chip_exchange(pb, name):
    _, hr, cols = pb.shape
    p_ref = jax.new_ref(pb, memory_space=HBM)
    o_ref = jax.empty_ref(jax.ShapeDtypeStruct((3, hr, cols), BF16), memory_space=HBM)

    @_sequencer(name, CHIP_ID, 3)
    def launch(ss, rs):
        x, y, c, chips = _place()
        _handshake([(px, py, c) for px, py in chips])
        cps = [pltpu.make_async_remote_copy(src_ref=p_ref.at[2 * px + py], dst_ref=o_ref.at[j], send_sem=ss.at[j],
                                            recv_sem=rs.at[j], device_id=(px, py, c), device_id_type=MESH)
               for j, (px, py) in enumerate(chips)]
        for cp in cps:
            cp.start()
        for cp in cps:
            cp.wait()

    launch()
    return o_ref[...]


def _chip_sum(pos, g5, got, peers, name):
    _, _, hr, cols = g5.shape

    def body(pos_ref, g_ref, r_ref, p_ref, o_ref, t_ref):
        acc = g_ref[...].astype(F32) + r_ref[...].astype(F32)
        for j in range(3):
            acc += p_ref[j].astype(F32)
        o_ref[...] = acc
        t_ref[...] = jnp.zeros_like(t_ref)

    return _pcall(
        body, name=name, scalar_prefetch=1, grid=(hr // RS_TR,),
        in_specs=[pl.BlockSpec((None, None, RS_TR, cols), lambda i, p: (p[0], p[1], i, 0)),
                  pl.BlockSpec((None, RS_TR, cols), lambda i, p: (p[0], i, 0)),
                  pl.BlockSpec((3, RS_TR, cols), lambda i, p: (0, i, 0))],
        out_specs=[pl.BlockSpec((None, RS_TR, cols), lambda i, p: (p[1], i, 0)),
                   pl.BlockSpec((8, 128), lambda i, p: (0, 0))],
        out_shape=[jax.ShapeDtypeStruct((2, hr, cols), F32), jax.ShapeDtypeStruct((8, 128), F32)],
        compiler_params=_cp("arbitrary"))(pos, g5, got, peers)


def _pair_share(gsum, name):
    g_ref = jax.new_ref(gsum, memory_space=HBM)

    @_sequencer(name, PAIR_ID, 1)
    def launch(ss, rs):
        x, y, c, _ = _place()
        sib = (x, y, 1 - c)
        _handshake([sib])
        cp = pltpu.make_async_remote_copy(src_ref=g_ref.at[c], dst_ref=g_ref.at[c], send_sem=ss.at[0],
                                          recv_sem=rs.at[0], device_id=sib, device_id_type=MESH)
        cp.start()
        cp.wait_send()
        pltpu.make_async_remote_copy(src_ref=g_ref.at[1 - c], dst_ref=g_ref.at[1 - c], send_sem=ss.at[0],
                                     recv_sem=rs.at[0], device_id=sib, device_id_type=MESH).wait_recv()

    launch()
    return g_ref[...]


class _ReduceScatter:
    def __init__(self, g_arr, gi, pos_c, pos_sc):
        _, rows, cols = g_arr.shape
        self.g5 = g_arr.reshape(NCHIP, 2, rows // 2, cols)
        self.gi, self.pos_c, self.pos_sc = gi, pos_c, pos_sc
        self.got = _pair_exchange(self.g5, f"pair_exchange_{gi}")

    def phase2(self):
        pb = _pair_sum(self.pos_c, self.g5, self.got, f"pair_sum_{self.gi}")
        self.peers = _chip_exchange(pb, f"chip_exchange_{self.gi}")

    def phase3(self):
        gsum, _ = _chip_sum(self.pos_sc, self.g5, self.got, self.peers, f"chip_sum_{self.gi}")
        self.full = _pair_share(gsum, f"pair_share_{self.gi}")

    def result(self):
        _, hr, cols = self.full.shape
        return self.full.reshape(2 * hr, cols)


SMALL_ROWS = 56


ALL_ID = 4


def _all_gather_small(vec, name):
    v_ref = jax.new_ref(vec, memory_space=HBM)
    o_ref = jax.empty_ref(jax.ShapeDtypeStruct((8, SMALL_ROWS, D), F32), memory_space=HBM)

    @_sequencer(name, ALL_ID, 8)
    def launch(ss, rs):
        x, y, c, _ = _place()
        me = 4 * x + 2 * y + c
        flip = lambda v, bit: 1 - v if bit else v
        peers = [(flip(x, m >> 2), flip(y, (m >> 1) & 1), flip(c, m & 1)) for m in range(1, 8)]
        _handshake(peers)
        mine = pltpu.make_async_copy(v_ref, o_ref.at[me], ss.at[7])
        mine.start()
        cps = [pltpu.make_async_remote_copy(src_ref=v_ref, dst_ref=o_ref.at[me], send_sem=ss.at[k],
                                            recv_sem=rs.at[k], device_id=peer, device_id_type=MESH)
               for k, peer in enumerate(peers)]
        for cp in cps:
            cp.start()
        for cp in cps:
            cp.wait()
        mine.wait()

    launch()
    return o_ref[...]


def _sum_devices(parts, name):
    def body(p_ref, o_ref):
        acc = p_ref[0]
        for d in range(1, 8):
            acc += p_ref[d]
        o_ref[...] = acc

    return _pcall(
        body, name=name, grid=(SMALL_ROWS // 8,),
        in_specs=[pl.BlockSpec((8, 8, D), lambda i: (0, i, 0))],
        out_specs=pl.BlockSpec((8, D), lambda i: (i, 0)),
        out_shape=jax.ShapeDtypeStruct((SMALL_ROWS, D), F32),
        compiler_params=_cp("parallel"))(parts)


def _adamw_math(w, g, m, v):
    m2 = ADAM_B1 * m + (1.0 - ADAM_B1) * g
    v2 = ADAM_B2 * v + (1.0 - ADAM_B2) * (g * g)
    m_hat = m2 / (1.0 - ADAM_B1 ** ADAM_STEP)
    v_hat = v2 / (1.0 - ADAM_B2 ** ADAM_STEP)
    delta = -ADAM_LR * (m_hat / (jnp.sqrt(v_hat) + ADAM_EPS) + ADAM_WD * w)
    return delta, m2, v2


def _adamw(w, g, m, v, name, g_block=None):
    R, C = w.shape
    tr = R
    gw_hint = C if g_block is None else g_block[0]
    for cand in (512, 352, 256, 176, 128, 64, 32, 16, 8):
        if R % cand == 0 and cand * max(C, gw_hint) * 4 <= (2 << 20):
            tr = cand
            break
    gw, gi = (C, 0) if g_block is None else g_block

    def body(w_ref, g_ref, m_ref, v_ref, go_ref, d_ref, mo_ref, vo_ref):
        g = g_ref[:, 0:C]
        d, m2, v2 = _adamw_math(w_ref[...], g, m_ref[...], v_ref[...])
        go_ref[...] = g
        d_ref[...] = d
        mo_ref[...] = m2
        vo_ref[...] = v2

    spec = pl.BlockSpec((tr, C), lambda i: (i, 0))
    return _pcall(
        body, name=name, grid=(R // tr,),
        in_specs=[spec, pl.BlockSpec((tr, gw), lambda i: (i, gi)), spec, spec],
        out_specs=[spec] * 4, out_shape=[jax.ShapeDtypeStruct((R, C), F32)] * 4,
        compiler_params=_cp("parallel"))(w, g, m, v)


BIG = ["ffn1_w_gate", "ffn1_w_up", "ffn1_w_down", "w_in", "w_ret_o", "w_conv_o", "w_out",
       "ffn2_w_gate", "ffn2_w_up", "ffn2_w_down"]
SLAB = {"ffn1_w_gate": "g1", "ffn1_w_up": "u1", "ffn1_w_down": "d1", "w_in": "w_in", "w_ret_o": "w_ret_o",
        "w_conv_o": "w_conv_o", "w_out": "w_out", "ffn2_w_gate": "g2", "ffn2_w_up": "u2", "ffn2_w_down": "d2"}
TRANSPOSED = {"ffn1_w_down", "ffn2_w_down", "w_ret_o", "w_conv_o", "w_out"}
MINOR_ROWS = {"ffn1_w_gate", "ffn1_w_up", "ffn2_w_gate", "ffn2_w_up"}
SMALL = ["ln1_g", "ln1_b", "ln2_g", "ln2_b", "ln3_g", "ln3_b", "conv_ln_g", "conv_ln_b", "conv_b",
         "ret_gn_g", "b_in"]
ORDER = ["ffn1_w_gate", "ffn1_w_up", "ffn1_w_down", "ln1_g", "ln1_b", "w_in", "b_in", "ret_gn_g", "conv_k",
         "conv_b", "conv_ln_g", "conv_ln_b", "w_ret_o", "w_conv_o", "w_out", "ln2_g", "ln2_b",
         "ffn2_w_gate", "ffn2_w_up", "ffn2_w_down", "ln3_g", "ln3_b"]


def _slab_width(name):
    return WIDTH[SLAB[name]]


def _pack_group(weights, keys):
    by_key = {SLAB[n]: n for n in BIG}
    parts = []
    for key in keys:
        w = weights[by_key[key]]
        w = w.T if by_key[key] in TRANSPOSED else w
        parts.append(jnp.pad(w, ((0, 0), (0, WIDTH[key] - w.shape[1]))))
    return jnp.concatenate(parts, axis=1).astype(BF16)


def _pack_small(vals, loss, rows):
    flat = jnp.concatenate([vals[n].reshape(-1) for n in SMALL] + [vals["conv_k"].reshape(-1), loss.reshape(-1)])
    return jnp.pad(flat, (0, rows * D - flat.shape[0])).reshape(rows, D)


def _unpack_small(arr, shapes):
    flat = arr.reshape(-1)
    out, pos = {}, 0
    for n in SMALL + ["conv_k"]:
        size = int(np.prod(shapes[n]))
        out[n] = flat[pos:pos + size].reshape(shapes[n])
        pos += size
    return out, flat[pos]


def kernel(x, ffn1_w_gate, ffn1_w_up, ffn1_w_down, ln1_g, ln1_b, w_in, b_in, ret_gn_g, conv_k, conv_b, conv_ln_g, conv_ln_b, w_ret_o, w_conv_o, w_out, ln2_g, ln2_b, ffn2_w_gate, ffn2_w_up, ffn2_w_down, ln3_g, ln3_b, loss_target, m_ffn1_w_gate, m_ffn1_w_up, m_ffn1_w_down, m_ln1_g, m_ln1_b, m_w_in, m_b_in, m_ret_gn_g, m_conv_k, m_conv_b, m_conv_ln_g, m_conv_ln_b, m_w_ret_o, m_w_conv_o, m_w_out, m_ln2_g, m_ln2_b, m_ffn2_w_gate, m_ffn2_w_up, m_ffn2_w_down, m_ln3_g, m_ln3_b, v_ffn1_w_gate, v_ffn1_w_up, v_ffn1_w_down, v_ln1_g, v_ln1_b, v_w_in, v_b_in, v_ret_gn_g, v_conv_k, v_conv_b, v_conv_ln_g, v_conv_ln_b, v_w_ret_o, v_w_conv_o, v_w_out, v_ln2_g, v_ln2_b, v_ffn2_w_gate, v_ffn2_w_up, v_ffn2_w_down, v_ln3_g, v_ln3_b):
    args = dict(locals())
    w = {n: args[n] for n in ORDER}
    m = {n: args["m_" + n] for n in ORDER}
    v = {n: args["v_" + n] for n in ORDER}
    xi, yi, ci = lax.axis_index("x"), lax.axis_index("y"), lax.axis_index("c")
    chip = 2 * xi + yi

    shards = {n: w[n][0] for n in BIG}
    wts = []
    for gi, keys in enumerate(GROUPS):
        slab = _pack_group(shards, keys)
        cols = slab.shape[1]
        wts.append(_gather_weights(slab.reshape(2, HALF, cols), f"gather_{gi}").reshape(NCHIP, D, cols))

    sp = {n: w[n] for n in SMALL}
    kpad = jnp.zeros((CONV_W, D), F32)
    kpad = lax.dynamic_update_slice(kpad, w["conv_k"][0, :, 0, :] * jnp.where(ci == 0, 1.0, 0.0), (0, chip * (D // NCHIP)))
    kvec = jnp.pad(kpad.reshape(-1), (0, SMALL_ROWS * D - CONV_W * D)).reshape(SMALL_ROWS, D)
    pos_c = jnp.reshape(ci, (1,)).astype(jnp.int32)
    pos_sc = jnp.stack([chip, ci]).astype(jnp.int32)
    out = {}

    def adam(gi, slab):
        for n in BIG:
            (g_of, off), width = LOC[SLAB[n]], _slab_width(n)
            if g_of != gi:
                continue
            w2 = w[n][0]
            if n in TRANSPOSED:
                res = _adamw(w2, slab[:, off:off + w2.shape[0]].T, m[n][0], v[n][0], "adamw_" + n)
                out[n] = [r[None] for r in res]
            elif n in MINOR_ROWS:
                res = _adamw(w2.T, slab[:, off:off + w2.shape[1]].T, m[n][0].T, v[n][0].T, "adamw_" + n)
                out[n] = [r.T[None] for r in res]
            else:
                res = _adamw(w2, slab, m[n][0], v[n][0], "adamw_" + n, g_block=(width, off // width))
                out[n] = [r[None] for r in res]

    grad_x, small_sum, shapes = _local_step(x[0], loss_target[0], wts, sp, kvec, pos_c, pos_sc, adam)
    small, total = _unpack_small(small_sum, shapes)

    for n in SMALL:
        res = _adamw(w[n], small[n], m[n], v[n], "adamw_" + n)
        out[n] = list(res)
    gk = lax.dynamic_slice(small["conv_k"], (0, chip * (D // NCHIP)), (CONV_W, D // NCHIP))
    res = _adamw(w["conv_k"][0, :, 0, :], gk, m["conv_k"][0, :, 0, :], v["conv_k"][0, :, 0, :], "adamw_conv_k")
    out["conv_k"] = [r[None, :, None, :] for r in res]

    grads = [out[n][0] for n in ORDER]
    deltas = [out[n][1] for n in ORDER]
    new_m = [out[n][2] for n in ORDER]
    new_v = [out[n][3] for n in ORDER]
    return (total, grad_x[None], *grads, *deltas, *new_m, *new_v)
```

```python
import dataclasses
import functools

import numpy as np
import jax
import jax.numpy as jnp
from jax import lax
from jax.experimental import pallas as pl
from jax.experimental.pallas import tpu as pltpu
from jax.experimental.pallas import tpu_sc as plsc

F32 = jnp.float32
BF16 = jnp.bfloat16

D = 1024
FS = 704
FSP = 768
FP = 4 * FSP
H = 8
DK = 128
DV = 256
CH = 128
VW = H * DV
INW = 10240
INS = INW // 4
CONV_W = 31
HALO = 32
EPS = 1e-5
ALPHA = 2.0 ** 0.25
ROPE_BASE = 10000.0
NCHIP = 4

ADAM_LR, ADAM_B1, ADAM_B2, ADAM_EPS, ADAM_WD, ADAM_STEP = 0.001, 0.9, 0.999, 1e-08, 0.01, 10

OFF = {"w_in": 0, "w_ret_o": 2560, "g1": 3072, "u1": 3840, "d1": 4608,
       "g2": 5376, "u2": 6144, "d2": 6912, "w_conv_o": 7680, "w_out": 7936}
WCOLS = 8192
WIDTH = {"w_in": INS, "w_ret_o": VW // NCHIP, "w_conv_o": D // NCHIP, "w_out": D // NCHIP,
         "g1": FSP, "u1": FSP, "d1": FSP, "g2": FSP, "u2": FSP, "d2": FSP}
GROUPS = (("g1", "u1"), ("d1",), ("w_in",), ("w_ret_o", "w_conv_o", "w_out"), ("g2", "u2", "d2"))
GATHER_GROUPS = (("g1", "u1", "d1"), ("w_in",), ("w_ret_o", "w_conv_o", "w_out"), ("g2", "u2", "d2"))


def _locate(groups):
    loc = {}
    for gi, keys in enumerate(groups):
        off = 0
        for k in keys:
            loc[k] = (gi, off)
            off += WIDTH[k]
    return loc


LOC = _locate(GROUPS)
LOC_W = _locate(GATHER_GROUPS)
GCOLS = [sum(WIDTH[k] for k in keys) for keys in GROUPS]
VMEM_LIMIT = 56 << 20


def _cp(*sem, **kw):
    return pltpu.CompilerParams(dimension_semantics=sem, vmem_limit_bytes=VMEM_LIMIT, **kw)


class _ProgramOrder:
    def __init__(self):
        self.active = False
        self.token = None


_ORDER = _ProgramOrder()


def _pcall(body, *, in_specs, scalar_prefetch=0, **kw):
    def call(*args):
        dep = _ORDER.token if _ORDER.active else None
        specs, fn = list(in_specs), body
        if dep is not None:
            n = len(args)

            def fn(*refs):
                return body(*refs[:n], *refs[n + 1:])

            specs.append(pl.BlockSpec(memory_space=pl.ANY))
            args = (*args, dep)
        params = dict(kw)
        if scalar_prefetch:
            params["grid_spec"] = pltpu.PrefetchScalarGridSpec(
                num_scalar_prefetch=scalar_prefetch, grid=params.pop("grid"), in_specs=specs,
                out_specs=params.pop("out_specs"))
        else:
            params["in_specs"] = specs
        out = pl.pallas_call(fn, **params)(*args)
        if _ORDER.active:
            _ORDER.token = jax.tree.leaves(out)[-1]
        return out

    return call


def _resident(shape, col_block):
    lead = (0,) * (len(shape) - 1)
    return pl.BlockSpec(shape, lambda *_: (*lead, col_block), pipeline_mode=pl.Buffered(1))


def _sig(x):
    return 1.0 / (1.0 + jnp.exp(-x))


def _dot(a, b):
    return jnp.dot(a, b, preferred_element_type=F32)


def _dot_nt(a, b):
    return lax.dot_general(a, b, (((1,), (1,)), ((), ())), preferred_element_type=F32)


def _ln_fwd(z, g, b):
    mu = jnp.mean(z, axis=-1, keepdims=True)
    xc = z - mu
    var = jnp.mean(xc * xc, axis=-1, keepdims=True)
    rstd = lax.rsqrt(var + EPS)
    xh = xc * rstd
    return xh * g + b, xh, rstd


def _ln_bwd(dy, xh, rstd, g):
    dxh = dy * g
    m1 = jnp.mean(dxh, axis=-1, keepdims=True)
    m2 = jnp.mean(dxh * xh, axis=-1, keepdims=True)
    return rstd * (dxh - m1 - xh * m2)


def _colsum(x):
    return jnp.sum(x, axis=0, keepdims=True)


def _zero_first(first, *refs):
    @pl.when(first)
    def _():
        for ref in refs:
            ref[...] = jnp.zeros_like(ref)


def _rope_tables(T):
    half = DK // 2
    freqs = ROPE_BASE ** (-np.arange(half, dtype=np.float32) / half)
    ang = (np.arange(T, dtype=np.float32)[:, None] * freqs[None, :]).astype(np.float32)
    cos, sin = np.cos(ang), np.sin(ang)
    return (jnp.asarray(np.concatenate([cos, cos], 1), F32),
            jnp.asarray(np.concatenate([-sin, sin], 1), F32))


def _decay_tables():
    h = np.arange(H, dtype=np.float64)
    log_g = np.log(1.0 - np.exp2(-5.0 - h))
    idx = np.arange(CH, dtype=np.float64)
    diff = idx[:, None] - idx[None, :]
    dm = np.where(diff[None] >= 0, np.exp(np.maximum(diff, 0.0)[None] * log_g[:, None, None]), 0.0)
    xi = np.exp((idx[None, :] + 1.0) * log_g[:, None])
    zeta = np.exp((CH - 1.0 - idx)[None, :] * log_g[:, None])
    cd = np.exp(CH * log_g)
    xi_t = np.broadcast_to(xi[:, :, None], (H, CH, DV))
    zeta_t = np.broadcast_to(zeta[:, :, None], (H, CH, DK))
    return (jnp.asarray(dm, F32), jnp.asarray(xi_t, F32), jnp.asarray(zeta_t, F32),
            [float(v) for v in cd])


def _cast_t(x):
    T = x.shape[0]
    tm = min(T, 512)

    def body(x_ref, xb_ref, xt_ref):
        v = x_ref[...]
        xb_ref[...] = v.astype(BF16)
        xt_ref[...] = v.T.astype(BF16)

    return _pcall(
        body, name="cast_t", grid=(T // tm,),
        in_specs=[pl.BlockSpec((tm, D), lambda i: (i, 0))],
        out_specs=[pl.BlockSpec((tm, D), lambda i: (i, 0)), pl.BlockSpec((D, tm), lambda i: (0, i))],
        out_shape=[jax.ShapeDtypeStruct((T, D), BF16), jax.ShapeDtypeStruct((D, T), BF16)],
        compiler_params=_cp("parallel"))(x)


def _ffn_up(xb, wall, og, ou, name):
    T = xb.shape[0]
    tm = min(T, 512)
    assert ou == og + FSP

    def body(x_ref, w_ref, a_ref, b_ref, h_ref):
        x = x_ref[...]
        for s in range(NCHIP):
            cols = slice(s * FSP, (s + 1) * FSP)
            a = _dot(x, w_ref[s, :, 0:FSP])
            b = _dot(x, w_ref[s, :, FSP:2 * FSP])
            a_ref[:, cols] = a.astype(BF16)
            b_ref[:, cols] = b.astype(BF16)
            h_ref[:, cols] = (a * _sig(a) * b).astype(BF16)

    ospec = pl.BlockSpec((tm, FP), lambda i: (i, 0))
    return _pcall(
        body, name=name, grid=(T // tm,),
        in_specs=[pl.BlockSpec((tm, D), lambda i: (i, 0)), _resident((NCHIP, D, 2 * FSP), og // (2 * FSP))],
        out_specs=[ospec] * 3, out_shape=[jax.ShapeDtypeStruct((T, FP), BF16)] * 3,
        compiler_params=_cp("parallel"))(xb, wall)


def _proj_ln(hb, wall, off, res, g, b, coef, name, want_b=True):
    T, K = hb.shape
    ks = K // NCHIP
    tm = min(T, 512)
    sub = min(tm, 256)

    def body(h_ref, w_ref, r_ref, g_ref, b_ref, z_ref, *rest):
        for r0 in range(0, tm, sub):
            r = slice(r0, r0 + sub)
            acc = _dot_nt(h_ref[r, 0:ks], w_ref[0])
            for s in range(1, NCHIP):
                acc += _dot_nt(h_ref[r, s * ks:(s + 1) * ks], w_ref[s])
            z = ALPHA * r_ref[r, :] + coef * acc
            z_ref[r, :] = z
            if want_b:
                y, _, _ = _ln_fwd(z, g_ref[...], b_ref[...])
                y_ref, yb_ref, yt_ref = rest
                y_ref[r, :] = y
                yb_ref[r, :] = y.astype(BF16)
                yt_ref[:, r] = y.T.astype(BF16)

    row = pl.BlockSpec((tm, D), lambda i: (i, 0))
    vec = pl.BlockSpec((1, D), lambda i: (0, 0))
    out_specs = [row]
    out_shape = [jax.ShapeDtypeStruct((T, D), F32)]
    if want_b:
        out_specs += [row, row, pl.BlockSpec((D, tm), lambda i: (0, i))]
        out_shape += [jax.ShapeDtypeStruct((T, D), F32), jax.ShapeDtypeStruct((T, D), BF16),
                      jax.ShapeDtypeStruct((D, T), BF16)]
    return _pcall(
        body, name=name, grid=(T // tm,),
        in_specs=[pl.BlockSpec((tm, K), lambda i: (i, 0)),
                  _resident((NCHIP, D, ks), off // ks), row, vec, vec],
        out_specs=out_specs, out_shape=out_shape,
        compiler_params=_cp("parallel"))(hb, wall, res, g, b)


def _inproj(xb, wall, off, b_in, cos_t, sin_t):
    T = xb.shape[0]
    tm, tn = min(T, 512), 512
    assert off == 0

    def body(x_ref, w_ref, bias_ref, cos_ref, sin_ref, o_ref):
        x = x_ref[...]
        c = cos_ref[...]
        s = sin_ref[...]
        for n0 in range(0, INW, tn):
            chip, c0 = divmod(n0, INS)
            acc = _dot(x, w_ref[chip, :, c0:c0 + tn]) + bias_ref[:, n0:n0 + tn]
            if n0 >= 2 * D:
                o_ref[:, n0:n0 + tn] = acc.astype(BF16)
                continue
            scale = DK ** -0.5 if n0 < D else 1.0
            for hh in range(tn // DK):
                xh = acc[:, hh * DK:(hh + 1) * DK]
                o = (xh * c + pltpu.roll(xh, DK // 2, 1) * s) * scale
                o_ref[:, n0 + hh * DK:n0 + (hh + 1) * DK] = o.astype(BF16)

    return _pcall(
        body, name="inproj", grid=(T // tm,),
        in_specs=[pl.BlockSpec((tm, D), lambda i: (i, 0)),
                  _resident((NCHIP, D, INS), 0),
                  pl.BlockSpec((1, INW), lambda i: (0, 0)),
                  pl.BlockSpec((tm, DK), lambda i: (i, 0)),
                  pl.BlockSpec((tm, DK), lambda i: (i, 0))],
        out_specs=pl.BlockSpec((tm, INW), lambda i: (i, 0)),
        out_shape=jax.ShapeDtypeStruct((T, INW), BF16),
        compiler_params=_cp("parallel"))(xb, wall, b_in, cos_t, sin_t)


RET_CPS = 2


def _retention_fwd(proj, gn_g, dm_t, xi_t, zeta_t, cds):
    T = proj.shape[0]
    n = T // CH
    tr = RET_CPS * CH

    def body(q_ref, k_ref, v_ref, g_ref, gn_ref, dm_ref, xi_ref, zt_ref, r_ref, ri_ref, st_ref, state):
        @pl.when(pl.program_id(0) == 0)
        def _():
            state[...] = jnp.zeros_like(state)

        for h in range(H):
            rows = slice(h * DK, (h + 1) * DK)
            cols = slice(h * DV, (h + 1) * DV)
            s_prev = state[rows, :]
            for j in range(RET_CPS):
                t = slice(j * CH, (j + 1) * CH)
                q = q_ref[t, h * DK:(h + 1) * DK]
                k = k_ref[t, h * DK:(h + 1) * DK]
                v = v_ref[t, cols]
                s_b = s_prev.astype(BF16)
                st_ref[j, rows, :] = s_b
                sc = _dot_nt(q, k) * dm_ref[h]
                r = _dot(sc.astype(BF16), v) + _dot(q, s_b) * xi_ref[h]
                kz = k.astype(F32) * zt_ref[h]
                s_prev = cds[h] * s_prev + _dot(kz.T.astype(BF16), v)
                r_ref[t, cols] = r
                mu = jnp.mean(r, axis=-1, keepdims=True)
                xc = r - mu
                var = jnp.mean(xc * xc, axis=-1, keepdims=True)
                y = xc * lax.rsqrt(var + EPS) * gn_ref[:, cols]
                g = g_ref[t, cols].astype(F32)
                ri_ref[t, cols] = (g * _sig(g) * y).astype(BF16)
            state[rows, :] = s_prev

    full3 = lambda shp: pl.BlockSpec(shp, lambda c: (0, 0, 0))
    return _pcall(
        body, name="retention_fwd", grid=(n // RET_CPS,),
        in_specs=[pl.BlockSpec((tr, D), lambda c: (c, 0)),
                  pl.BlockSpec((tr, D), lambda c: (c, 1)),
                  pl.BlockSpec((tr, VW), lambda c: (c, 1)),
                  pl.BlockSpec((tr, VW), lambda c: (c, 2)),
                  pl.BlockSpec((1, VW), lambda c: (0, 0)),
                  full3((H, CH, CH)), full3((H, CH, DV)), full3((H, CH, DK))],
        out_specs=[pl.BlockSpec((tr, VW), lambda c: (c, 0)), pl.BlockSpec((tr, VW), lambda c: (c, 0)),
                   pl.BlockSpec((RET_CPS, H * DK, DV), lambda c: (c, 0, 0))],
        out_shape=[jax.ShapeDtypeStruct((T, VW), F32), jax.ShapeDtypeStruct((T, VW), BF16),
                   jax.ShapeDtypeStruct((n, H * DK, DV), BF16)],
        scratch_shapes=[pltpu.VMEM((H * DK, DV), F32)],
        compiler_params=_cp("arbitrary"))(proj, proj, proj, proj, gn_g, dm_t, xi_t, zeta_t)


CONV_TT = 256
CONV_SB = 64
CONV_CB = 256


SUB = 8
CONV_PAD = 8


def _glu(a_ref, b_ref, rows=slice(None)):
    a = a_ref[rows, :].astype(F32)
    sb = _sig(b_ref[rows, :].astype(F32))
    return a, sb


def _shift_copies(win, sh, rows):
    win[rows:rows + CONV_PAD, :] = jnp.zeros((CONV_PAD, D), F32)
    for b in range(1, SUB):
        sh[b - 1, :, :] = win[b:b + rows, :]


def _tap(win, sh, start, size, cs):
    a, b = divmod(start, SUB)
    src = win if b == 0 else sh.at[b - 1]
    return src[SUB * a:SUB * a + size, cs]


def _conv_fwd(proj, conv_k, conv_b, ln_g, ln_b):
    T = proj.shape[0]
    tt = min(T, CONV_TT)
    ca, cb = 6 * D // D, 7 * D // D

    def body(a_ref, b_ref, pa_ref, pb_ref, k_ref, cb_ref, g_ref, bb_ref, u1_ref, u3_ref, win, sh):
        i = pl.program_id(0)
        a, sb = _glu(a_ref, b_ref)
        win[HALO:tt + HALO, :] = a * sb
        pa, psb = _glu(pa_ref, pb_ref, slice(tt - HALO, tt))
        win[0:HALO, :] = jnp.where(i > 0, pa * psb, 0.0)
        _shift_copies(win, sh, tt + HALO)
        for c0 in range(0, D, CONV_CB):
            cs = slice(c0, c0 + CONV_CB)
            for r0 in range(0, tt, CONV_SB):
                acc = jnp.zeros((CONV_SB, CONV_CB), F32)
                for w in range(CONV_W):
                    st = r0 + HALO - (CONV_W - 1) + w
                    acc += _tap(win, sh, st, CONV_SB, cs) * k_ref[w:w + 1, cs]
                u1_ref[r0:r0 + CONV_SB, cs] = acc + cb_ref[:, cs]
        u2, _, _ = _ln_fwd(u1_ref[...], g_ref[...], bb_ref[...])
        u3_ref[...] = (u2 * _sig(u2)).astype(BF16)

    vec = pl.BlockSpec((1, D), lambda i: (0, 0))
    row = pl.BlockSpec((tt, D), lambda i: (i, 0))
    return _pcall(
        body, name="conv_fwd", grid=(T // tt,),
        in_specs=[pl.BlockSpec((tt, D), lambda i: (i, ca)), pl.BlockSpec((tt, D), lambda i: (i, cb)),
                  pl.BlockSpec((tt, D), lambda i: (jnp.maximum(i - 1, 0), ca)),
                  pl.BlockSpec((tt, D), lambda i: (jnp.maximum(i - 1, 0), cb)),
                  pl.BlockSpec((CONV_W, D), lambda i: (0, 0)), vec, vec, vec],
        out_specs=[row, row],
        out_shape=[jax.ShapeDtypeStruct((T, D), F32), jax.ShapeDtypeStruct((T, D), BF16)],
        scratch_shapes=[pltpu.VMEM((tt + HALO + CONV_PAD, D), F32), pltpu.VMEM((SUB - 1, tt + HALO, D), F32)],
        compiler_params=_cp("parallel"))(proj, proj, proj, proj, conv_k, conv_b, ln_g, ln_b)


def _merge(ret_in, u3, proj, wall, off_r, off_c):
    T = ret_in.shape[0]
    tm = min(T, 512)
    kr, kc = VW // NCHIP, D // NCHIP

    def body(ri_ref, u3_ref, gr_ref, gc_ref, wr_ref, wc_ref, ro_ref, co_ref, m_ref):
        ro = _dot_nt(ri_ref[:, 0:kr], wr_ref[0])
        co = _dot_nt(u3_ref[:, 0:kc], wc_ref[0])
        for s in range(1, NCHIP):
            ro += _dot_nt(ri_ref[:, s * kr:(s + 1) * kr], wr_ref[s])
            co += _dot_nt(u3_ref[:, s * kc:(s + 1) * kc], wc_ref[s])
        ro_ref[...] = ro.astype(BF16)
        co_ref[...] = co.astype(BF16)
        m = _sig(gr_ref[...].astype(F32)) * ro + _sig(gc_ref[...].astype(F32)) * co
        m_ref[...] = m.astype(BF16)

    row = pl.BlockSpec((tm, D), lambda i: (i, 0))
    return _pcall(
        body, name="merge", grid=(T // tm,),
        in_specs=[pl.BlockSpec((tm, VW), lambda i: (i, 0)), row,
                  pl.BlockSpec((tm, D), lambda i: (i, 8)), pl.BlockSpec((tm, D), lambda i: (i, 9)),
                  pl.BlockSpec((NCHIP, D, kr), lambda i: (0, 0, off_r // kr)),
                  pl.BlockSpec((NCHIP, D, kc), lambda i: (0, 0, off_c // kc))],
        out_specs=[row] * 3, out_shape=[jax.ShapeDtypeStruct((T, D), BF16)] * 3,
        compiler_params=_cp("parallel"))(ret_in, u3, proj, proj, wall, wall)


def _loss_ln_bwd(z, g, b, target, coef):
    T = z.shape[0]
    tm = min(T, 256)
    nt = T // tm

    def body(z_ref, g_ref, b_ref, t_ref, loss_ref, dzb_ref, dzt_ref, dz_ref, dg_ref, db_ref, lacc):
        i = pl.program_id(0)
        _zero_first(i == 0, lacc, dg_ref, db_ref)
        gam = g_ref[...]
        y, xh, rstd = _ln_fwd(z_ref[...], gam, b_ref[...])
        e = y - t_ref[...]
        lacc[...] += _colsum(e * e)
        dy = e * (1.0 / D)
        dz = _ln_bwd(dy, xh, rstd, gam)
        dz_ref[...] = dz
        dzc = coef * dz
        dzb_ref[...] = dzc.astype(BF16)
        dzt_ref[...] = dzc.T.astype(BF16)
        dg_ref[...] += _colsum(dy * xh)
        db_ref[...] += _colsum(dy)

        @pl.when(i == nt - 1)
        def _():
            loss_ref[...] = (0.5 / D) * jnp.sum(lacc[...], axis=1, keepdims=True)

    row = pl.BlockSpec((tm, D), lambda i: (i, 0))
    vec = pl.BlockSpec((1, D), lambda i: (0, 0))
    return _pcall(
        body, name="loss_ln_bwd", grid=(nt,),
        in_specs=[row, vec, vec, row],
        out_specs=[pl.BlockSpec((1, 1), lambda i: (0, 0)), row, pl.BlockSpec((D, tm), lambda i: (0, i)),
                   row, vec, vec],
        out_shape=[jax.ShapeDtypeStruct((1, 1), F32), jax.ShapeDtypeStruct((T, D), BF16),
                   jax.ShapeDtypeStruct((D, T), BF16), jax.ShapeDtypeStruct((T, D), F32),
                   jax.ShapeDtypeStruct((1, D), F32), jax.ShapeDtypeStruct((1, D), F32)],
        scratch_shapes=[pltpu.VMEM((1, D), F32)],
        compiler_params=_cp("arbitrary"))(z, g, b, target)


def _ffn_bwd_h(dfb, wall, od, a, b, name):
    T = dfb.shape[0]
    tm = min(T, 512)

    def body(d_ref, w_ref, a_ref, b_ref, da_ref, db_ref):
        d = d_ref[...]
        for s in range(NCHIP):
            cols = slice(s * FSP, (s + 1) * FSP)
            dh = _dot(d, w_ref[s])
            a = a_ref[:, cols].astype(F32)
            sg = _sig(a)
            da_ref[:, cols] = (dh * b_ref[:, cols].astype(F32) * (sg * (1.0 + a * (1.0 - sg)))).astype(BF16)
            db_ref[:, cols] = (dh * a * sg).astype(BF16)

    ospec = pl.BlockSpec((tm, FP), lambda i: (i, 0))
    return _pcall(
        body, name=name, grid=(T // tm,),
        in_specs=[pl.BlockSpec((tm, D), lambda i: (i, 0)), _resident((NCHIP, D, FSP), od // FSP), ospec, ospec],
        out_specs=[ospec] * 2, out_shape=[jax.ShapeDtypeStruct((T, FP), BF16)] * 2,
        compiler_params=_cp("parallel"))(dfb, wall, a, b)


DX_SUB = 256


def _dx_partial(lhs, wall, chips, name):
    T, K = lhs.shape
    ks = K // NCHIP
    nc = len(chips)
    assert list(chips) == list(range(chips[0], chips[0] + nc)) and chips[0] % nc == 0
    tm = min(T, 512)

    def body(l_ref, w_ref, o_ref):
        for r0 in range(0, tm, DX_SUB):
            r = slice(r0, r0 + DX_SUB)
            acc = _dot_nt(l_ref[r, 0:ks], w_ref[0])
            for s in range(1, nc):
                acc += _dot_nt(l_ref[r, s * ks:(s + 1) * ks], w_ref[s])
            o_ref[r, :] = acc

    blk = chips[0] // nc
    return _pcall(
        body, name=name, grid=(T // tm,),
        in_specs=[pl.BlockSpec((tm, nc * ks), lambda i: (i, blk)),
                  pl.BlockSpec((nc, D, ks), lambda i: (blk, 0, 0), pipeline_mode=pl.Buffered(1))],
        out_specs=pl.BlockSpec((tm, D), lambda i: (i, 0)),
        out_shape=jax.ShapeDtypeStruct((T, D), F32),
        compiler_params=_cp("parallel"))(lhs, wall)


def _dx_bwd(lhs, offs, wall, dz_next, name, ln=None, chips=tuple(range(NCHIP)), partial=None):
    T, K = lhs[0].shape
    ks = K // NCHIP
    nl = len(lhs)
    nc = len(chips)
    assert list(offs) == [l * ks for l in range(nl)]
    assert list(chips) == list(range(chips[0], chips[0] + nc)) and chips[0] % nc == 0
    blk = chips[0] // nc
    tm = min(T, 512)

    def body(*refs):
        l_refs = refs[:nl]
        w_ref = refs[nl]
        dzn_ref = refs[nl + 1]
        pos = nl + 2
        if partial is not None:
            p_ref = refs[pos]
            pos += 1
        if ln is not None:
            z_ref, g_ref = refs[pos:pos + 2]
            pos += 2
        outs = refs[pos:]
        sums = list(outs[3:]) if ln is not None else list(outs[1:])

        _zero_first(pl.program_id(0) == 0, *sums)

        for r0 in range(0, tm, DX_SUB):
            r = slice(r0, r0 + DX_SUB)
            acc = None if partial is None else p_ref[r, :]
            for s in range(nc):
                rows = slice(s * ks, (s + 1) * ks)
                for l in range(nl):
                    part = _dot_nt(l_refs[l][r, rows], w_ref[s, :, l * ks:(l + 1) * ks])
                    acc = part if acc is None else acc + part
            dy = acc + ALPHA * dzn_ref[r, :]
            if ln is None:
                outs[0][r, :] = dy
            else:
                gam = g_ref[...]
                _, xh, rstd = _ln_fwd(z_ref[r, :], gam, 0.0)
                dz = _ln_bwd(dy, xh, rstd, gam)
                dzc = ln[2] * dz
                outs[0][r, :] = dzc.astype(BF16)
                outs[1][:, r] = dzc.T.astype(BF16)
                outs[2][r, :] = dz
                outs[3][...] += _colsum(dy * xh)
                outs[4][...] += _colsum(dy)

    row = pl.BlockSpec((tm, D), lambda i: (i, 0))
    vec = pl.BlockSpec((1, D), lambda i: (0, 0))
    in_specs = [pl.BlockSpec((tm, nc * ks), lambda i: (i, blk))] * nl
    in_specs += [pl.BlockSpec((nc, D, nl * ks), lambda i: (blk, 0, 0), pipeline_mode=pl.Buffered(1)), row]
    args = list(lhs) + [wall, dz_next]
    if partial is not None:
        in_specs.append(row)
        args.append(partial)
    if ln is None:
        out_specs = [row]
        out_shape = [jax.ShapeDtypeStruct((T, D), F32)]
    else:
        in_specs += [row, vec]
        args += [ln[0], ln[1]]
        out_specs = [row, pl.BlockSpec((D, tm), lambda i: (0, i)), row, vec, vec]
        out_shape = [jax.ShapeDtypeStruct((T, D), BF16), jax.ShapeDtypeStruct((D, T), BF16),
                     jax.ShapeDtypeStruct((T, D), F32), jax.ShapeDtypeStruct((1, D), F32),
                     jax.ShapeDtypeStruct((1, D), F32)]
    return _pcall(
        body, name=name, grid=(T // tm,), in_specs=in_specs, out_specs=out_specs, out_shape=out_shape,
        compiler_params=_cp("arbitrary"))(*args)


def _wgrad(lhs_t, rhs, key, name, g_all=None, colsum=False):
    T, N = rhs.shape
    tn = next(c for c in (768, 512, 256) if (N // NCHIP) % c == 0 and LOC[key][1] % c == 0)
    nps = N // NCHIP // tn
    off = LOC[key][1]
    cols = GCOLS[LOC[key][0]]

    def body(*refs):
        l_ref, r_ref = refs[0], refs[1]
        o_ref, t_ref = refs[-2 - colsum], refs[-1]
        o_ref[...] = _dot(l_ref[...], r_ref[...]).astype(BF16)
        if colsum:
            refs[-2][...] = _colsum(r_ref[...].astype(F32))
        t_ref[...] = jnp.zeros_like(t_ref)

    in_specs = [_resident((D, T), 0), pl.BlockSpec((T, tn), lambda j: (0, j))]
    args = [lhs_t, rhs]
    aliases = {}
    if g_all is not None:
        in_specs.append(pl.BlockSpec(memory_space=pl.ANY))
        args.append(g_all)
        aliases = {2: 0}
    out_specs = [pl.BlockSpec((None, D, tn), lambda j: (j // nps, 0, off // tn + j % nps))]
    out_shape = [jax.ShapeDtypeStruct((NCHIP, D, cols), BF16)]
    if colsum:
        out_specs.append(pl.BlockSpec((1, tn), lambda j: (0, j)))
        out_shape.append(jax.ShapeDtypeStruct((1, N), F32))
    out_specs.append(pl.BlockSpec((8, 128), lambda j: (0, 0)))
    out_shape.append(jax.ShapeDtypeStruct((8, 128), F32))
    res = _pcall(
        body, name=name, grid=(N // tn,), in_specs=in_specs, out_specs=out_specs, out_shape=out_shape,
        input_output_aliases=aliases,
        compiler_params=_cp("arbitrary"))(*args)
    return (res[0], res[1]) if colsum else res[0]


def _merge_bwd(dmb, wall, off, proj, ro, co):
    T = dmb.shape[0]
    tm = min(T, 512)
    ks = D // NCHIP

    def body(d_ref, w_ref, gr_ref, gc_ref, ro_ref, co_ref, dro_ref, drot_ref, dco_ref, dcot_ref, dp_ref):
        d = d_ref[...]
        dmg = jnp.concatenate([_dot(d, w_ref[s]) for s in range(NCHIP)], axis=1)
        sr = _sig(gr_ref[...].astype(F32))
        sc = _sig(gc_ref[...].astype(F32))
        dro = dmg * sr
        dco = dmg * sc
        dro_ref[...] = dro.astype(BF16)
        drot_ref[...] = dro.T.astype(BF16)
        dco_ref[...] = dco.astype(BF16)
        dcot_ref[...] = dco.T.astype(BF16)
        dp_ref[:, 0:D] = (dmg * ro_ref[...].astype(F32) * sr * (1.0 - sr)).astype(BF16)
        dp_ref[:, D:2 * D] = (dmg * co_ref[...].astype(F32) * sc * (1.0 - sc)).astype(BF16)

    row = pl.BlockSpec((tm, D), lambda i: (i, 0))
    col = pl.BlockSpec((D, tm), lambda i: (0, i))
    return _pcall(
        body, name="merge_bwd", grid=(T // tm,),
        in_specs=[row, pl.BlockSpec((NCHIP, D, ks), lambda i: (0, 0, off // ks)),
                  pl.BlockSpec((tm, D), lambda i: (i, 8)), pl.BlockSpec((tm, D), lambda i: (i, 9)), row, row],
        out_specs=[row, col, row, col, pl.BlockSpec((tm, 2 * D), lambda i: (i, 4))],
        out_shape=[jax.ShapeDtypeStruct((T, D), BF16), jax.ShapeDtypeStruct((D, T), BF16),
                   jax.ShapeDtypeStruct((T, D), BF16), jax.ShapeDtypeStruct((D, T), BF16),
                   jax.ShapeDtypeStruct((T, INW), BF16)],
        compiler_params=_cp("parallel"))(dmb, wall, proj, proj, ro, co)


def _reto_bwd(dro, wall, off, r, proj, gn_g, dproj):
    T = dro.shape[0]
    tm = min(T, 512)
    hps = H // NCHIP

    def body(d_ref, w_ref, r_ref, g_ref, gn_ref, _, dr_ref, dgn_ref, dp_ref):
        _zero_first(pl.program_id(1) == 0, dgn_ref)
        dri = _dot(d_ref[...], w_ref[...])
        rr = r_ref[...]
        mu = jnp.mean(rr, axis=-1, keepdims=True)
        xc = rr - mu
        var = jnp.mean(xc * xc, axis=-1, keepdims=True)
        rstd = lax.rsqrt(var + EPS)
        rn = xc * rstd
        gn = gn_ref[...]
        g = g_ref[...].astype(F32)
        sg = _sig(g)
        dy = dri * (g * sg)
        dp_ref[...] = (dri * (rn * gn) * (sg * (1.0 + g * (1.0 - sg)))).astype(BF16)
        dgn_ref[...] += _colsum(dy * rn)
        dr_ref[...] = _ln_bwd(dy, rn, rstd, gn).astype(BF16)

    return _pcall(
        body, name="reto_bwd", grid=(H, T // tm),
        in_specs=[pl.BlockSpec((tm, D), lambda j, i: (i, 0)),
                  pl.BlockSpec((None, D, DV), lambda j, i: (j // hps, 0, off // DV + j % hps)),
                  pl.BlockSpec((tm, DV), lambda j, i: (i, j)),
                  pl.BlockSpec((tm, DV), lambda j, i: (i, 2 * VW // DV + j)),
                  pl.BlockSpec((1, DV), lambda j, i: (0, j)),
                  pl.BlockSpec(memory_space=pl.ANY)],
        out_specs=[pl.BlockSpec((tm, DV), lambda j, i: (i, j)), pl.BlockSpec((1, DV), lambda j, i: (0, j)),
                   pl.BlockSpec((tm, DV), lambda j, i: (i, 2 * VW // DV + j))],
        out_shape=[jax.ShapeDtypeStruct((T, VW), BF16), jax.ShapeDtypeStruct((1, VW), F32),
                   jax.ShapeDtypeStruct((T, INW), BF16)],
        input_output_aliases={5: 2},
        compiler_params=_cp("arbitrary", "arbitrary"))(dro, wall, r, proj, gn_g, dproj)


def _retention_bwd(proj, dr, states, cos_t, sin_t, dm_t, xi_t, zeta_t, cds, dproj):
    T = proj.shape[0]
    n = T // CH // RET_CPS
    tr = RET_CPS * CH
    scale = DK ** -0.5

    def body(q_ref, k_ref, v_ref, dr_ref, st_ref, cos_ref, sin_ref, dm_ref, xi_ref, zt_ref, _, dp_ref, ds):
        @pl.when(pl.program_id(0) == 0)
        def _():
            ds[...] = jnp.zeros_like(ds)

        def unrope(d, t):
            return d * cos_ref[t, :] + pltpu.roll(d * sin_ref[t, :], DK // 2, 1)

        for h in range(H):
            rows = slice(h * DK, (h + 1) * DK)
            dm = dm_ref[h]
            zt = zt_ref[h]
            ds_prev = ds[rows, :]
            for j in reversed(range(RET_CPS)):
                t = slice(j * CH, (j + 1) * CH)
                q = q_ref[t, h * DK:(h + 1) * DK]
                k = k_ref[t, h * DK:(h + 1) * DK]
                v = v_ref[t, h * DV:(h + 1) * DV]
                d_r = dr_ref[t, h * DV:(h + 1) * DV]
                s_b = st_ref[j, rows, :]
                sc = _dot_nt(q, k) * dm
                dsc = _dot_nt(d_r, v) * dm
                drx = (d_r.astype(F32) * xi_ref[h]).astype(BF16)
                ds_b = ds_prev.astype(BF16)
                kz = (k.astype(F32) * zt).astype(BF16)
                dq = _dot(dsc.astype(BF16), k) + _dot_nt(drx, s_b)
                dk = _dot(dsc.T.astype(BF16), q) + _dot_nt(v, ds_b) * zt
                dv = _dot(sc.T.astype(BF16), d_r) + _dot(kz, ds_b)
                ds_prev = cds[h] * ds_prev + _dot(q.astype(F32).T.astype(BF16), drx)
                dp_ref[t, h * DK:(h + 1) * DK] = unrope(dq * scale, t).astype(BF16)
                dp_ref[t, D + h * DK:D + (h + 1) * DK] = unrope(dk, t).astype(BF16)
                dp_ref[t, 2 * D + h * DV:2 * D + (h + 1) * DV] = dv.astype(BF16)
            ds[rows, :] = ds_prev

    rv = lambda c: n - 1 - c
    full3 = lambda shp: pl.BlockSpec(shp, lambda c: (0, 0, 0))
    return _pcall(
        body, name="retention_bwd", grid=(n,),
        in_specs=[pl.BlockSpec((tr, D), lambda c: (rv(c), 0)),
                  pl.BlockSpec((tr, D), lambda c: (rv(c), 1)),
                  pl.BlockSpec((tr, VW), lambda c: (rv(c), 1)),
                  pl.BlockSpec((tr, VW), lambda c: (rv(c), 0)),
                  pl.BlockSpec((RET_CPS, H * DK, DV), lambda c: (rv(c), 0, 0)),
                  pl.BlockSpec((tr, DK), lambda c: (rv(c), 0)),
                  pl.BlockSpec((tr, DK), lambda c: (rv(c), 0)),
                  full3((H, CH, CH)), full3((H, CH, DV)), full3((H, CH, DK)),
                  pl.BlockSpec(memory_space=pl.ANY)],
        out_specs=pl.BlockSpec((tr, 2 * D + VW), lambda c: (rv(c), 0)),
        out_shape=jax.ShapeDtypeStruct((T, INW), BF16),
        input_output_aliases={10: 0},
        scratch_shapes=[pltpu.VMEM((H * DK, DV), F32)],
        compiler_params=_cp("arbitrary"))(proj, proj, proj, dr, states, cos_t, sin_t, dm_t, xi_t, zeta_t, dproj)


def _convo_bwd(dco, wall, off, u1, ln_g, ln_b):
    T = dco.shape[0]
    tm = min(T, 512)
    ks = D // NCHIP

    def body(d_ref, w_ref, u1_ref, g_ref, b_ref, du1_ref, dg_ref, db_ref, dcb_ref):
        _zero_first(pl.program_id(0) == 0, dg_ref, db_ref, dcb_ref)
        d = d_ref[...]
        du3 = jnp.concatenate([_dot(d, w_ref[s]) for s in range(NCHIP)], axis=1)
        gam = g_ref[...]
        u2, xh, rstd = _ln_fwd(u1_ref[...], gam, b_ref[...])
        sg = _sig(u2)
        du2 = du3 * (sg * (1.0 + u2 * (1.0 - sg)))
        du1 = _ln_bwd(du2, xh, rstd, gam)
        du1_ref[...] = du1
        dg_ref[...] += _colsum(du2 * xh)
        db_ref[...] += _colsum(du2)
        dcb_ref[...] += _colsum(du1)

    row = pl.BlockSpec((tm, D), lambda i: (i, 0))
    vec = pl.BlockSpec((1, D), lambda i: (0, 0))
    return _pcall(
        body, name="convo_bwd", grid=(T // tm,),
        in_specs=[row, pl.BlockSpec((NCHIP, D, ks), lambda i: (0, 0, off // ks)), row, vec, vec],
        out_specs=[row, vec, vec, vec],
        out_shape=[jax.ShapeDtypeStruct((T, D), F32)] + [jax.ShapeDtypeStruct((1, D), F32)] * 3,
        compiler_params=_cp("arbitrary"))(dco, wall, u1, ln_g, ln_b)


def _conv_bwd(du1, proj, conv_k, dproj):
    T = du1.shape[0]
    tt = min(T, CONV_TT)
    nt = T // tt
    ca, cb = 6, 7

    def body(d_ref, dn_ref, a_ref, b_ref, pa_ref, pb_ref, k_ref, _, dp_ref, dk_ref, win_u, win_d, sh_u, sh_d):
        i = pl.program_id(0)
        a, sb = _glu(a_ref, b_ref)
        win_u[HALO:tt + HALO, :] = a * sb
        pa, psb = _glu(pa_ref, pb_ref, slice(tt - HALO, tt))
        win_u[0:HALO, :] = jnp.where(i > 0, pa * psb, 0.0)
        win_d[0:tt, :] = d_ref[...]
        win_d[tt:tt + HALO, :] = jnp.where(i < nt - 1, dn_ref[0:HALO, :], 0.0)
        _shift_copies(win_u, sh_u, tt + HALO)
        _shift_copies(win_d, sh_d, tt + HALO)

        @pl.when(i == 0)
        def _():
            dk_ref[...] = jnp.zeros_like(dk_ref)

        for c0 in range(0, D, CONV_CB):
            cs = slice(c0, c0 + CONV_CB)
            for r0 in range(0, tt, CONV_SB):
                acc = jnp.zeros((CONV_SB, CONV_CB), F32)
                for w in range(CONV_W):
                    st = r0 + (CONV_W - 1) - w
                    acc += _tap(win_d, sh_d, st, CONV_SB, cs) * k_ref[w:w + 1, cs]
                aa = a_ref[r0:r0 + CONV_SB, cs].astype(F32)
                ss = _sig(b_ref[r0:r0 + CONV_SB, cs].astype(F32))
                dp_ref[r0:r0 + CONV_SB, cs] = (acc * ss).astype(BF16)
                dp_ref[r0:r0 + CONV_SB, c0 + D:c0 + D + CONV_CB] = (acc * aa * ss * (1.0 - ss)).astype(BF16)
        for c0 in range(0, D, CONV_CB):
            cs = slice(c0, c0 + CONV_CB)
            for w in range(CONV_W):
                acc = jnp.zeros((CONV_SB, CONV_CB), F32)
                for r0 in range(0, tt, CONV_SB):
                    st = r0 + HALO - (CONV_W - 1) + w
                    acc += win_d[r0:r0 + CONV_SB, cs] * _tap(win_u, sh_u, st, CONV_SB, cs)
                dk_ref[w:w + 1, cs] += _colsum(acc)

    blk = lambda f, c: pl.BlockSpec((tt, D), lambda i: (f(i), c))
    cur = lambda i: i
    prv = lambda i: jnp.maximum(i - 1, 0)
    nxt = lambda i: jnp.minimum(i + 1, nt - 1)
    return _pcall(
        body, name="conv_bwd", grid=(nt,),
        in_specs=[blk(cur, 0), blk(nxt, 0), blk(cur, ca), blk(cur, cb), blk(prv, ca), blk(prv, cb),
                  pl.BlockSpec((CONV_W, D), lambda i: (0, 0)), pl.BlockSpec(memory_space=pl.ANY)],
        out_specs=[pl.BlockSpec((tt, 2 * D), lambda i: (i, 3)), pl.BlockSpec((HALO, D), lambda i: (0, 0))],
        out_shape=[jax.ShapeDtypeStruct((T, INW), BF16), jax.ShapeDtypeStruct((HALO, D), F32)],
        input_output_aliases={7: 0},
        scratch_shapes=[pltpu.VMEM((tt + HALO + CONV_PAD, D), F32), pltpu.VMEM((tt + HALO + CONV_PAD, D), F32),
                        pltpu.VMEM((SUB - 1, tt + HALO, D), F32), pltpu.VMEM((SUB - 1, tt + HALO, D), F32)],
        compiler_params=_cp("arbitrary"))(du1, du1, proj, proj, proj, proj, conv_k, dproj)


def _local_step(x, target, wts, sp, kvec, pos_c, pos_sc, adam):
    T = x.shape[0]
    cos_t, sin_t = _rope_tables(T)
    dm_t, xi_t, zeta_t, cds = _decay_tables()
    wa = lambda key: wts[LOC_W[key][0]]
    wo = lambda key: LOC_W[key][1]
    _ORDER.active, _ORDER.token = True, None

    xb, xt = _cast_t(x)
    a1, b1, h1 = _ffn_up(xb, wa("g1"), wo("g1"), wo("u1"), "ffn1_up")
    z1, x1, x1b, x1t = _proj_ln(h1, wa("d1"), wo("d1"), x, sp["ln1_g"], sp["ln1_b"], 0.5, "ffn1_down_ln")
    proj = _inproj(x1b, wa("w_in"), wo("w_in"), sp["b_in"], cos_t, sin_t)
    r, ret_in, states = _retention_fwd(proj, sp["ret_gn_g"], dm_t, xi_t, zeta_t, cds)
    kall = _sum_devices(_all_gather_small(kvec, "gather_conv_k"), "sum_conv_k")
    sp = dict(sp, conv_k=kall.reshape(-1)[:CONV_W * D].reshape(CONV_W, D))
    u1, u3 = _conv_fwd(proj, sp["conv_k"], sp["conv_b"], sp["conv_ln_g"], sp["conv_ln_b"])
    ro, co, merged = _merge(ret_in, u3, proj, wa("w_ret_o"), wo("w_ret_o"), wo("w_conv_o"))
    z2, x2, x2b, x2t = _proj_ln(merged, wa("w_out"), wo("w_out"), x1, sp["ln2_g"], sp["ln2_b"], 1.0, "out_proj_ln")
    a2, b2, h2 = _ffn_up(x2b, wa("g2"), wo("g2"), wo("u2"), "ffn2_up")
    (z3,) = _proj_ln(h2, wa("d2"), wo("d2"), x2, sp["ln3_g"], sp["ln3_b"], 0.5, "ffn2_down", want_b=False)

    sg = {}
    rs = {}
    loss, df2b, df2t, dz3, sg["ln3_g"], sg["ln3_b"] = _loss_ln_bwd(z3, sp["ln3_g"], sp["ln3_b"], target, 0.5)
    da2, db2 = _ffn_bwd_h(df2b, wa("d2"), wo("d2"), a2, b2, "ffn2_bwd_h")
    g4 = _wgrad(df2t, h2, "d2", "wgrad_d2")
    g4 = _wgrad(x2t, da2, "g2", "wgrad_g2", g4)
    g4 = _wgrad(x2t, db2, "u2", "wgrad_u2", g4)
    rs[4] = _ReduceScatter(g4, 4, pos_c, pos_sc)
    dmb, dmt, dz2, sg["ln2_g"], sg["ln2_b"] = _dx_bwd(
        [da2, db2], [wo("g2"), wo("u2")], wa("g2"), dz3, "ffn2_dx_ln", ln=(z2, sp["ln2_g"], 1.0))
    rs[4].phase2()
    g3 = _wgrad(dmt, merged, "w_out", "wgrad_out")
    dro, drot, dco, dcot, dproj = _merge_bwd(dmb, wa("w_out"), wo("w_out"), proj, ro, co)
    g3 = _wgrad(drot, ret_in, "w_ret_o", "wgrad_ret_o", g3)
    g3 = _wgrad(dcot, u3, "w_conv_o", "wgrad_conv_o", g3)
    rs[3] = _ReduceScatter(g3, 3, pos_c, pos_sc)
    dr, sg["ret_gn_g"], dproj = _reto_bwd(dro, wa("w_ret_o"), wo("w_ret_o"), r, proj, sp["ret_gn_g"], dproj)
    rs[4].phase3()
    rs[3].phase2()
    dproj = _retention_bwd(proj, dr, states, cos_t, sin_t, dm_t, xi_t, zeta_t, cds, dproj)
    du1, sg["conv_ln_g"], sg["conv_ln_b"], sg["conv_b"] = _convo_bwd(
        dco, wa("w_conv_o"), wo("w_conv_o"), u1, sp["conv_ln_g"], sp["conv_ln_b"])
    dproj, dck = _conv_bwd(du1, proj, sp["conv_k"], dproj)
    sg["conv_k"] = dck[:CONV_W]
    adam(4, rs[4].result())
    rs[3].phase3()
    g2, sg["b_in"] = _wgrad(x1t, dproj, "w_in", "wgrad_in", colsum=True)
    rs[2] = _ReduceScatter(g2, 2, pos_c, pos_sc)
    dx_part = _dx_partial(dproj, wa("w_in"), (0, 1), "mixer_dx_part")
    df1b, df1t, dz1, sg["ln1_g"], sg["ln1_b"] = _dx_bwd(
        [dproj], [wo("w_in")], wa("w_in"), dz2, "mixer_dx_ln", ln=(z1, sp["ln1_g"], 0.5), chips=(2, 3),
        partial=dx_part)
    adam(3, rs[3].result())
    rs[2].phase2()
    shapes = {n: sg[n].shape for n in SMALL + ["conv_k"]}
    small_parts = _all_gather_small(_pack_small(sg, loss, SMALL_ROWS), "gather_small")
    da1, db1 = _ffn_bwd_h(df1b, wa("d1"), wo("d1"), a1, b1, "ffn1_bwd_h")
    small_sum = _sum_devices(small_parts, "sum_small")
    g1 = _wgrad(df1t, h1, "d1", "wgrad_d1")
    rs[1] = _ReduceScatter(g1, 1, pos_c, pos_sc)
    g0 = _wgrad(xt, da1, "g1", "wgrad_g1")
    g0 = _wgrad(xt, db1, "u1", "wgrad_u1", g0)
    rs[0] = _ReduceScatter(g0, 0, pos_c, pos_sc)
    rs[2].phase3()
    rs[1].phase2()
    rs[0].phase2()
    (grad_x,) = _dx_bwd([da1, db1], [wo("g1"), wo("u1")], wa("g1"), dz1, "ffn1_dx")
    adam(2, rs[2].result())
    rs[1].phase3()
    rs[0].phase3()
    adam(1, rs[1].result())
    adam(0, rs[0].result())
    _ORDER.active = False
    return grad_x, small_sum, shapes


MESH = pl.DeviceIdType.MESH
ANY = pl.BlockSpec(memory_space=pl.ANY)
HALF = D // 2


def _place():
    x, y, c = lax.axis_index("x"), lax.axis_index("y"), lax.axis_index("c")
    chips = [(1 - x, y), (x, 1 - y), (1 - x, 1 - y)]
    return x, y, c, chips


GATHER_ID = 1


def _gather_weights(wloc, name):
    w_ref = jax.new_ref(wloc, memory_space=pltpu.MemorySpace.HBM)
    o_ref = jax.empty_ref(jax.ShapeDtypeStruct((NCHIP, 2, HALF, wloc.shape[-1]), BF16),
                          memory_space=pltpu.MemorySpace.HBM)
    dma = pltpu.SemaphoreType.DMA

    @pl.kernel(mesh=plsc.ScalarSubcoreMesh(axis_name="sc", num_cores=1), name=name,
               scratch_types=(dma(()), dma((2,)), dma((2,)), dma((3,)), dma((3,)), dma(()), dma(())),
               compiler_params=pltpu.CompilerParams(collective_id=GATHER_ID))
    def launch(lsem, s1, r1, s2, r2, s3, r3):
        x, y, c, _ = _place()
        me = 2 * x + y
        sib = (x, y, 1 - c)
        x_nbr, y_nbr = (1 - x, y, c), (x, 1 - y, c)
        x_chip, y_chip, d_chip = 2 * (1 - x) + y, 2 * x + (1 - y), 2 * (1 - x) + (1 - y)
        _handshake([sib, x_nbr, y_nbr])
        mine = pltpu.make_async_copy(w_ref, o_ref.at[me], lsem)
        mine.start()

        def rc(src, dst, ss, rs, dev):
            return pltpu.make_async_remote_copy(src_ref=src, dst_ref=dst, send_sem=ss, recv_sem=rs,
                                                device_id=dev, device_id_type=MESH)

        first = [rc(w_ref.at[c], o_ref.at[me, c], s1.at[0], r1.at[0], x_nbr),
                 rc(w_ref.at[c], o_ref.at[me, c], s1.at[1], r1.at[1], y_nbr)]
        for cp in first:
            cp.start()
        on_chip = c * x_chip + (1 - c) * y_chip
        other_chip = c * y_chip + (1 - c) * x_chip
        on_to = (c * x + (1 - c) * (1 - x), c * (1 - y) + (1 - c) * y, c)
        slot = o_ref.at[on_chip, c]
        rc(slot, slot, s1.at[1 - c], r1.at[1 - c], sib).wait_recv()
        onward = rc(slot, slot, s3, r3, on_to)
        onward.start()
        passed = [rc(slot, slot, s2.at[0], r2.at[0], sib)]
        passed[0].start()
        slot = o_ref.at[other_chip, c]
        rc(slot, slot, s1.at[c], r1.at[c], sib).wait_recv()
        passed.append(rc(slot, slot, s2.at[1], r2.at[1], sib))
        passed[1].start()
        slot = o_ref.at[d_chip, c]
        rc(slot, slot, s3, r3, sib).wait_recv()
        passed.append(rc(slot, slot, s2.at[2], r2.at[2], sib))
        passed[2].start()
        for j, chip in enumerate([other_chip, on_chip, d_chip]):
            slot = o_ref.at[chip, 1 - c]
            rc(slot, slot, s2.at[j], r2.at[j], sib).wait_recv()
        for cp in first + [onward] + passed:
            cp.wait_send()
        mine.wait()

    launch()
    return o_ref[...]


PAIR_ID = 2
CHIP_ID = 3
HBM = pltpu.MemorySpace.HBM


def _sequencer(name, collective_id, n_sems):
    dma = pltpu.SemaphoreType.DMA
    return pl.kernel(mesh=plsc.ScalarSubcoreMesh(axis_name="sc", num_cores=1), name=name,
                     scratch_types=(dma((n_sems,)), dma((n_sems,))),
                     compiler_params=pltpu.CompilerParams(collective_id=collective_id))


def _handshake(peers):
    barrier = pltpu.get_barrier_semaphore()
    for peer in peers:
        pl.semaphore_signal(barrier, inc=1, device_id=peer, device_id_type=MESH)
    pl.semaphore_wait(barrier, len(peers))


def _pair_exchange(g5, name):
    _, _, hr, cols = g5.shape
    g_ref = jax.new_ref(g5, memory_space=HBM)
    o_ref = jax.empty_ref(jax.ShapeDtypeStruct((NCHIP, hr, cols), g5.dtype), memory_space=HBM)

    @_sequencer(name, PAIR_ID, NCHIP)
    def launch(ss, rs):
        x, y, c, _ = _place()
        sib = (x, y, 1 - c)
        _handshake([sib])
        cps = [pltpu.make_async_remote_copy(src_ref=g_ref.at[j, 1 - c], dst_ref=o_ref.at[j], send_sem=ss.at[j],
                                            recv_sem=rs.at[j], device_id=sib, device_id_type=MESH)
               for j in range(NCHIP)]
        for cp in cps:
            cp.start()
        for cp in cps:
            cp.wait()

    launch()
    return o_ref[...]


RS_TR = 256


def _pair_sum(pos, g5, got, name):
    _, _, hr, cols = g5.shape

    def body(pos_ref, g_ref, r_ref, o_ref):
        o_ref[...] = (g_ref[...].astype(F32) + r_ref[...].astype(F32)).astype(BF16)

    return _pcall(
        body, name=name, scalar_prefetch=1, grid=(NCHIP, hr // RS_TR),
        in_specs=[pl.BlockSpec((None, None, RS_TR, cols), lambda j, i, p: (j, p[0], i, 0)),
                  pl.BlockSpec((None, RS_TR, cols), lambda j, i, p: (j, i, 0))],
        out_specs=pl.BlockSpec((None, RS_TR, cols), lambda j, i, p: (j, i, 0)),
        out_shape=jax.ShapeDtypeStruct((NCHIP, hr, cols), BF16),
        compiler_params=_cp("parallel", "parallel"))(pos, g5, got)


def _chip_exchange(pb, name):
    _, hr, cols = pb.shape
    p_ref = jax.new_ref(pb, memory_space=HBM)
    o_ref = jax.empty_ref(jax.ShapeDtypeStruct((3, hr, cols), BF16), memory_space=HBM)

    @_sequencer(name, CHIP_ID, 3)
    def launch(ss, rs):
        x, y, c, chips = _place()
        _handshake([(px, py, c) for px, py in chips])
        cps = [pltpu.make_async_remote_copy(src_ref=p_ref.at[2 * px + py], dst_ref=o_ref.at[j], send_sem=ss.at[j],
                                            recv_sem=rs.at[j], device_id=(px, py, c), device_id_type=MESH)
               for j, (px, py) in enumerate(chips)]
        for cp in cps:
            cp.start()
        for cp in cps:
            cp.wait()

    launch()
    return o_ref[...]


def _chip_sum(pos, g5, got, peers, name):
    _, _, hr, cols = g5.shape

    def body(pos_ref, g_ref, r_ref, p_ref, o_ref, t_ref):
        acc = g_ref[...].astype(F32) + r_ref[...].astype(F32)
        for j in range(3):
            acc += p_ref[j].astype(F32)
        o_ref[...] = acc
        t_ref[...] = jnp.zeros_like(t_ref)

    return _pcall(
        body, name=name, scalar_prefetch=1, grid=(hr // RS_TR,),
        in_specs=[pl.BlockSpec((None, None, RS_TR, cols), lambda i, p: (p[0], p[1], i, 0)),
                  pl.BlockSpec((None, RS_TR, cols), lambda i, p: (p[0], i, 0)),
                  pl.BlockSpec((3, RS_TR, cols), lambda i, p: (0, i, 0))],
        out_specs=[pl.BlockSpec((None, RS_TR, cols), lambda i, p: (p[1], i, 0)),
                   pl.BlockSpec((8, 128), lambda i, p: (0, 0))],
        out_shape=[jax.ShapeDtypeStruct((2, hr, cols), F32), jax.ShapeDtypeStruct((8, 128), F32)],
        compiler_params=_cp("arbitrary"))(pos, g5, got, peers)


def _pair_share(gsum, name):
    g_ref = jax.new_ref(gsum, memory_space=HBM)

    @_sequencer(name, PAIR_ID, 1)
    def launch(ss, rs):
        x, y, c, _ = _place()
        sib = (x, y, 1 - c)
        _handshake([sib])
        cp = pltpu.make_async_remote_copy(src_ref=g_ref.at[c], dst_ref=g_ref.at[c], send_sem=ss.at[0],
                                          recv_sem=rs.at[0], device_id=sib, device_id_type=MESH)
        cp.start()
        cp.wait_send()
        pltpu.make_async_remote_copy(src_ref=g_ref.at[1 - c], dst_ref=g_ref.at[1 - c], send_sem=ss.at[0],
                                     recv_sem=rs.at[0], device_id=sib, device_id_type=MESH).wait_recv()

    launch()
    return g_ref[...]


class _ReduceScatter:
    def __init__(self, g_arr, gi, pos_c, pos_sc):
        _, rows, cols = g_arr.shape
        self.g5 = g_arr.reshape(NCHIP, 2, rows // 2, cols)
        self.gi, self.pos_c, self.pos_sc = gi, pos_c, pos_sc
        self.got = _pair_exchange(self.g5, f"pair_exchange_{gi}")

    def phase2(self):
        pb = _pair_sum(self.pos_c, self.g5, self.got, f"pair_sum_{self.gi}")
        self.peers = _chip_exchange(pb, f"chip_exchange_{self.gi}")

    def phase3(self):
        gsum, _ = _chip_sum(self.pos_sc, self.g5, self.got, self.peers, f"chip_sum_{self.gi}")
        self.full = _pair_share(gsum, f"pair_share_{self.gi}")

    def result(self):
        _, hr, cols = self.full.shape
        return self.full.reshape(2 * hr, cols)


SMALL_ROWS = 56


ALL_ID = 4


def _all_gather_small(vec, name):
    v_ref = jax.new_ref(vec, memory_space=HBM)
    o_ref = jax.empty_ref(jax.ShapeDtypeStruct((8, SMALL_ROWS, D), F32), memory_space=HBM)

    @_sequencer(name, ALL_ID, 8)
    def launch(ss, rs):
        x, y, c, _ = _place()
        me = 4 * x + 2 * y + c
        flip = lambda v, bit: 1 - v if bit else v
        peers = [(flip(x, m >> 2), flip(y, (m >> 1) & 1), flip(c, m & 1)) for m in range(1, 8)]
        _handshake(peers)
        mine = pltpu.make_async_copy(v_ref, o_ref.at[me], ss.at[7])
        mine.start()
        cps = [pltpu.make_async_remote_copy(src_ref=v_ref, dst_ref=o_ref.at[me], send_sem=ss.at[k],
                                            recv_sem=rs.at[k], device_id=peer, device_id_type=MESH)
               for k, peer in enumerate(peers)]
        for cp in cps:
            cp.start()
        for cp in cps:
            cp.wait()
        mine.wait()

    launch()
    return o_ref[...]


def _sum_devices(parts, name):
    def body(p_ref, o_ref):
        acc = p_ref[0]
        for d in range(1, 8):
            acc += p_ref[d]
        o_ref[...] = acc

    return _pcall(
        body, name=name, grid=(SMALL_ROWS // 8,),
        in_specs=[pl.BlockSpec((8, 8, D), lambda i: (0, i, 0))],
        out_specs=pl.BlockSpec((8, D), lambda i: (i, 0)),
        out_shape=jax.ShapeDtypeStruct((SMALL_ROWS, D), F32),
        compiler_params=_cp("parallel"))(parts)


def _adamw_math(w, g, m, v):
    m2 = ADAM_B1 * m + (1.0 - ADAM_B1) * g
    v2 = ADAM_B2 * v + (1.0 - ADAM_B2) * (g * g)
    m_hat = m2 / (1.0 - ADAM_B1 ** ADAM_STEP)
    v_hat = v2 / (1.0 - ADAM_B2 ** ADAM_STEP)
    delta = -ADAM_LR * (m_hat / (jnp.sqrt(v_hat) + ADAM_EPS) + ADAM_WD * w)
    return delta, m2, v2


def _adamw(w, g, m, v, name, g_block=None):
    R, C = w.shape
    tr = R
    gw_hint = C if g_block is None else g_block[0]
    for cand in (512, 352, 256, 176, 128, 64, 32, 16, 8):
        if R % cand == 0 and cand * max(C, gw_hint) * 4 <= (2 << 20):
            tr = cand
            break
    gw, gi = (C, 0) if g_block is None else g_block

    def body(w_ref, g_ref, m_ref, v_ref, go_ref, d_ref, mo_ref, vo_ref):
        g = g_ref[:, 0:C]
        d, m2, v2 = _adamw_math(w_ref[...], g, m_ref[...], v_ref[...])
        go_ref[...] = g
        d_ref[...] = d
        mo_ref[...] = m2
        vo_ref[...] = v2

    spec = pl.BlockSpec((tr, C), lambda i: (i, 0))
    return _pcall(
        body, name=name, grid=(R // tr,),
        in_specs=[spec, pl.BlockSpec((tr, gw), lambda i: (i, gi)), spec, spec],
        out_specs=[spec] * 4, out_shape=[jax.ShapeDtypeStruct((R, C), F32)] * 4,
        compiler_params=_cp("parallel"))(w, g, m, v)


BIG = ["ffn1_w_gate", "ffn1_w_up", "ffn1_w_down", "w_in", "w_ret_o", "w_conv_o", "w_out",
       "ffn2_w_gate", "ffn2_w_up", "ffn2_w_down"]
SLAB = {"ffn1_w_gate": "g1", "ffn1_w_up": "u1", "ffn1_w_down": "d1", "w_in": "w_in", "w_ret_o": "w_ret_o",
        "w_conv_o": "w_conv_o", "w_out": "w_out", "ffn2_w_gate": "g2", "ffn2_w_up": "u2", "ffn2_w_down": "d2"}
TRANSPOSED = {"ffn1_w_down", "ffn2_w_down", "w_ret_o", "w_conv_o", "w_out"}
MINOR_ROWS = {"ffn1_w_gate", "ffn1_w_up", "ffn2_w_gate", "ffn2_w_up"}
SMALL = ["ln1_g", "ln1_b", "ln2_g", "ln2_b", "ln3_g", "ln3_b", "conv_ln_g", "conv_ln_b", "conv_b",
         "ret_gn_g", "b_in"]
ORDER = ["ffn1_w_gate", "ffn1_w_up", "ffn1_w_down", "ln1_g", "ln1_b", "w_in", "b_in", "ret_gn_g", "conv_k",
         "conv_b", "conv_ln_g", "conv_ln_b", "w_ret_o", "w_conv_o", "w_out", "ln2_g", "ln2_b",
         "ffn2_w_gate", "ffn2_w_up", "ffn2_w_down", "ln3_g", "ln3_b"]


def _slab_width(name):
    return WIDTH[SLAB[name]]


def _pack_group(weights, keys):
    by_key = {SLAB[n]: n for n in BIG}
    parts = []
    for key in keys:
        w = weights[by_key[key]]
        w = w.T if by_key[key] in TRANSPOSED else w
        parts.append(jnp.pad(w, ((0, 0), (0, WIDTH[key] - w.shape[1]))))
    return jnp.concatenate(parts, axis=1).astype(BF16)


def _pack_small(vals, loss, rows):
    flat = jnp.concatenate([vals[n].reshape(-1) for n in SMALL] + [vals["conv_k"].reshape(-1), loss.reshape(-1)])
    return jnp.pad(flat, (0, rows * D - flat.shape[0])).reshape(rows, D)


def _unpack_small(arr, shapes):
    flat = arr.reshape(-1)
    out, pos = {}, 0
    for n in SMALL + ["conv_k"]:
        size = int(np.prod(shapes[n]))
        out[n] = flat[pos:pos + size].reshape(shapes[n])
        pos += size
    return out, flat[pos]


def kernel(x, ffn1_w_gate, ffn1_w_up, ffn1_w_down, ln1_g, ln1_b, w_in, b_in, ret_gn_g, conv_k, conv_b, conv_ln_g, conv_ln_b, w_ret_o, w_conv_o, w_out, ln2_g, ln2_b, ffn2_w_gate, ffn2_w_up, ffn2_w_down, ln3_g, ln3_b, loss_target, m_ffn1_w_gate, m_ffn1_w_up, m_ffn1_w_down, m_ln1_g, m_ln1_b, m_w_in, m_b_in, m_ret_gn_g, m_conv_k, m_conv_b, m_conv_ln_g, m_conv_ln_b, m_w_ret_o, m_w_conv_o, m_w_out, m_ln2_g, m_ln2_b, m_ffn2_w_gate, m_ffn2_w_up, m_ffn2_w_down, m_ln3_g, m_ln3_b, v_ffn1_w_gate, v_ffn1_w_up, v_ffn1_w_down, v_ln1_g, v_ln1_b, v_w_in, v_b_in, v_ret_gn_g, v_conv_k, v_conv_b, v_conv_ln_g, v_conv_ln_b, v_w_ret_o, v_w_conv_o, v_w_out, v_ln2_g, v_ln2_b, v_ffn2_w_gate, v_ffn2_w_up, v_ffn2_w_down, v_ln3_g, v_ln3_b):
    args = dict(locals())
    w = {n: args[n] for n in ORDER}
    m = {n: args["m_" + n] for n in ORDER}
    v = {n: args["v_" + n] for n in ORDER}
    xi, yi, ci = lax.axis_index("x"), lax.axis_index("y"), lax.axis_index("c")
    chip = 2 * xi + yi

    shards = {n: w[n][0] for n in BIG}
    wts = []
    for gi, keys in enumerate(GATHER_GROUPS):
        slab = _pack_group(shards, keys)
        cols = slab.shape[1]
        wts.append(_gather_weights(slab.reshape(2, HALF, cols), f"gather_{gi}").reshape(NCHIP, D, cols))

    sp = {n: w[n] for n in SMALL}
    kpad = jnp.zeros((CONV_W, D), F32)
    kpad = lax.dynamic_update_slice(kpad, w["conv_k"][0, :, 0, :] * jnp.where(ci == 0, 1.0, 0.0), (0, chip * (D // NCHIP)))
    kvec = jnp.pad(kpad.reshape(-1), (0, SMALL_ROWS * D - CONV_W * D)).reshape(SMALL_ROWS, D)
    pos_c = jnp.reshape(ci, (1,)).astype(jnp.int32)
    pos_sc = jnp.stack([chip, ci]).astype(jnp.int32)
    out = {}

    def adam(gi, slab):
        for n in BIG:
            (g_of, off), width = LOC[SLAB[n]], _slab_width(n)
            if g_of != gi:
                continue
            w2 = w[n][0]
            if n in TRANSPOSED:
                res = _adamw(w2, slab[:, off:off + w2.shape[0]].T, m[n][0], v[n][0], "adamw_" + n)
                out[n] = [r[None] for r in res]
            elif n in MINOR_ROWS:
                res = _adamw(w2.T, slab[:, off:off + w2.shape[1]].T, m[n][0].T, v[n][0].T, "adamw_" + n)
                out[n] = [r.T[None] for r in res]
            else:
                res = _adamw(w2, slab, m[n][0], v[n][0], "adamw_" + n, g_block=(width, off // width))
                out[n] = [r[None] for r in res]

    grad_x, small_sum, shapes = _local_step(x[0], loss_target[0], wts, sp, kvec, pos_c, pos_sc, adam)
    small, total = _unpack_small(small_sum, shapes)

    for n in SMALL:
        res = _adamw(w[n], small[n], m[n], v[n], "adamw_" + n)
        out[n] = list(res)
    gk = lax.dynamic_slice(small["conv_k"], (0, chip * (D // NCHIP)), (CONV_W, D // NCHIP))
    res = _adamw(w["conv_k"][0, :, 0, :], gk, m["conv_k"][0, :, 0, :], v["conv_k"][0, :, 0, :], "adamw_conv_k")
    out["conv_k"] = [r[None, :, None, :] for r in res]

    grads = [out[n][0] for n in ORDER]
    deltas = [out[n][1] for n in ORDER]
    new_m = [out[n][2] for n in ORDER]
    new_v = [out[n][3] for n in ORDER]
    return (total, grad_x[None], *grads, *deltas, *new_m, *new_v)
```

```python
import dataclasses
import functools

import numpy as np
import jax
import jax.numpy as jnp
from jax import lax
from jax.experimental import pallas as pl
from jax.experimental.pallas import tpu as pltpu
from jax.experimental.pallas import tpu_sc as plsc

F32 = jnp.float32
BF16 = jnp.bfloat16

D = 1024
FS = 704
FSP = 768
FP = 4 * FSP
H = 8
DK = 128
DV = 256
CH = 128
VW = H * DV
INW = 10240
INS = INW // 4
CONV_W = 31
HALO = 32
EPS = 1e-5
ALPHA = 2.0 ** 0.25
ROPE_BASE = 10000.0
NCHIP = 4

ADAM_LR, ADAM_B1, ADAM_B2, ADAM_EPS, ADAM_WD, ADAM_STEP = 0.001, 0.9, 0.999, 1e-08, 0.01, 10

OFF = {"w_in": 0, "w_ret_o": 2560, "g1": 3072, "u1": 3840, "d1": 4608,
       "g2": 5376, "u2": 6144, "d2": 6912, "w_conv_o": 7680, "w_out": 7936}
WCOLS = 8192
WIDTH = {"w_in": INS, "w_ret_o": VW // NCHIP, "w_conv_o": D // NCHIP, "w_out": D // NCHIP,
         "g1": FSP, "u1": FSP, "d1": FSP, "g2": FSP, "u2": FSP, "d2": FSP}
GROUPS = (("g1", "u1"), ("d1",), ("w_in",), ("w_ret_o", "w_conv_o", "w_out"), ("g2", "u2", "d2"))
GATHER_GROUPS = (("g1", "u1", "d1"), ("w_in",), ("w_ret_o", "w_conv_o", "w_out"), ("g2", "u2", "d2"))


def _locate(groups):
    loc = {}
    for gi, keys in enumerate(groups):
        off = 0
        for k in keys:
            loc[k] = (gi, off)
            off += WIDTH[k]
    return loc


LOC = _locate(GROUPS)
LOC_W = _locate(GATHER_GROUPS)
GCOLS = [sum(WIDTH[k] for k in keys) for keys in GROUPS]
VMEM_LIMIT = 56 << 20


def _cp(*sem, **kw):
    return pltpu.CompilerParams(dimension_semantics=sem, vmem_limit_bytes=VMEM_LIMIT, **kw)


class _ProgramOrder:
    def __init__(self):
        self.active = False
        self.token = None


_ORDER = _ProgramOrder()


def _pcall(body, *, in_specs, scalar_prefetch=0, **kw):
    def call(*args):
        dep = _ORDER.token if _ORDER.active else None
        specs, fn = list(in_specs), body
        if dep is not None:
            n = len(args)

            def fn(*refs):
                return body(*refs[:n], *refs[n + 1:])

            specs.append(pl.BlockSpec(memory_space=pl.ANY))
            args = (*args, dep)
        params = dict(kw)
        if scalar_prefetch:
            params["grid_spec"] = pltpu.PrefetchScalarGridSpec(
                num_scalar_prefetch=scalar_prefetch, grid=params.pop("grid"), in_specs=specs,
                out_specs=params.pop("out_specs"))
        else:
            params["in_specs"] = specs
        out = pl.pallas_call(fn, **params)(*args)
        if _ORDER.active:
            _ORDER.token = jax.tree.leaves(out)[-1]
        return out

    return call


def _resident(shape, col_block):
    lead = (0,) * (len(shape) - 1)
    return pl.BlockSpec(shape, lambda *_: (*lead, col_block), pipeline_mode=pl.Buffered(1))


def _sig(x):
    return 1.0 / (1.0 + jnp.exp(-x))


def _dot(a, b):
    return jnp.dot(a, b, preferred_element_type=F32)


def _dot_nt(a, b):
    return lax.dot_general(a, b, (((1,), (1,)), ((), ())), preferred_element_type=F32)


def _ln_fwd(z, g, b):
    mu = jnp.mean(z, axis=-1, keepdims=True)
    xc = z - mu
    var = jnp.mean(xc * xc, axis=-1, keepdims=True)
    rstd = lax.rsqrt(var + EPS)
    xh = xc * rstd
    return xh * g + b, xh, rstd


def _ln_bwd(dy, xh, rstd, g):
    dxh = dy * g
    m1 = jnp.mean(dxh, axis=-1, keepdims=True)
    m2 = jnp.mean(dxh * xh, axis=-1, keepdims=True)
    return rstd * (dxh - m1 - xh * m2)


def _colsum(x):
    return jnp.sum(x, axis=0, keepdims=True)


def _zero_first(first, *refs):
    @pl.when(first)
    def _():
        for ref in refs:
            ref[...] = jnp.zeros_like(ref)


def _rope_tables(T):
    half = DK // 2
    freqs = ROPE_BASE ** (-np.arange(half, dtype=np.float32) / half)
    ang = (np.arange(T, dtype=np.float32)[:, None] * freqs[None, :]).astype(np.float32)
    cos, sin = np.cos(ang), np.sin(ang)
    return (jnp.asarray(np.concatenate([cos, cos], 1), F32),
            jnp.asarray(np.concatenate([-sin, sin], 1), F32))


def _decay_tables():
    h = np.arange(H, dtype=np.float64)
    log_g = np.log(1.0 - np.exp2(-5.0 - h))
    idx = np.arange(CH, dtype=np.float64)
    diff = idx[:, None] - idx[None, :]
    dm = np.where(diff[None] >= 0, np.exp(np.maximum(diff, 0.0)[None] * log_g[:, None, None]), 0.0)
    xi = np.exp((idx[None, :] + 1.0) * log_g[:, None])
    zeta = np.exp((CH - 1.0 - idx)[None, :] * log_g[:, None])
    cd = np.exp(CH * log_g)
    xi_t = np.broadcast_to(xi[:, :, None], (H, CH, DV))
    zeta_t = np.broadcast_to(zeta[:, :, None], (H, CH, DK))
    return (jnp.asarray(dm, F32), jnp.asarray(xi_t, F32), jnp.asarray(zeta_t, F32),
            [float(v) for v in cd])


def _cast_t(x):
    T = x.shape[0]
    tm = min(T, 512)

    def body(x_ref, xb_ref, xt_ref):
        v = x_ref[...]
        xb_ref[...] = v.astype(BF16)
        xt_ref[...] = v.T.astype(BF16)

    return _pcall(
        body, name="cast_t", grid=(T // tm,),
        in_specs=[pl.BlockSpec((tm, D), lambda i: (i, 0))],
        out_specs=[pl.BlockSpec((tm, D), lambda i: (i, 0)), pl.BlockSpec((D, tm), lambda i: (0, i))],
        out_shape=[jax.ShapeDtypeStruct((T, D), BF16), jax.ShapeDtypeStruct((D, T), BF16)],
        compiler_params=_cp("parallel"))(x)


def _ffn_up(xb, wall, og, ou, name):
    T = xb.shape[0]
    tm = min(T, 512)
    assert ou == og + FSP

    def body(x_ref, w_ref, a_ref, b_ref, h_ref):
        x = x_ref[...]
        for s in range(NCHIP):
            cols = slice(s * FSP, (s + 1) * FSP)
            a = _dot(x, w_ref[s, :, 0:FSP])
            b = _dot(x, w_ref[s, :, FSP:2 * FSP])
            a_ref[:, cols] = a.astype(BF16)
            b_ref[:, cols] = b.astype(BF16)
            h_ref[:, cols] = (a * _sig(a) * b).astype(BF16)

    ospec = pl.BlockSpec((tm, FP), lambda i: (i, 0))
    return _pcall(
        body, name=name, grid=(T // tm,),
        in_specs=[pl.BlockSpec((tm, D), lambda i: (i, 0)), _resident((NCHIP, D, 2 * FSP), og // (2 * FSP))],
        out_specs=[ospec] * 3, out_shape=[jax.ShapeDtypeStruct((T, FP), BF16)] * 3,
        compiler_params=_cp("parallel"))(xb, wall)


def _proj_ln(hb, wall, off, res, g, b, coef, name, want_b=True):
    T, K = hb.shape
    ks = K // NCHIP
    tm = min(T, 512)
    sub = min(tm, 256)

    def body(h_ref, w_ref, r_ref, g_ref, b_ref, z_ref, *rest):
        for r0 in range(0, tm, sub):
            r = slice(r0, r0 + sub)
            acc = _dot_nt(h_ref[r, 0:ks], w_ref[0])
            for s in range(1, NCHIP):
                acc += _dot_nt(h_ref[r, s * ks:(s + 1) * ks], w_ref[s])
            z = ALPHA * r_ref[r, :] + coef * acc
            z_ref[r, :] = z
            if want_b:
                y, _, _ = _ln_fwd(z, g_ref[...], b_ref[...])
                y_ref, yb_ref, yt_ref = rest
                y_ref[r, :] = y
                yb_ref[r, :] = y.astype(BF16)
                yt_ref[:, r] = y.T.astype(BF16)

    row = pl.BlockSpec((tm, D), lambda i: (i, 0))
    vec = pl.BlockSpec((1, D), lambda i: (0, 0))
    out_specs = [row]
    out_shape = [jax.ShapeDtypeStruct((T, D), F32)]
    if want_b:
        out_specs += [row, row, pl.BlockSpec((D, tm), lambda i: (0, i))]
        out_shape += [jax.ShapeDtypeStruct((T, D), F32), jax.ShapeDtypeStruct((T, D), BF16),
                      jax.ShapeDtypeStruct((D, T), BF16)]
    return _pcall(
        body, name=name, grid=(T // tm,),
        in_specs=[pl.BlockSpec((tm, K), lambda i: (i, 0)),
                  _resident((NCHIP, D, ks), off // ks), row, vec, vec],
        out_specs=out_specs, out_shape=out_shape,
        compiler_params=_cp("parallel"))(hb, wall, res, g, b)


def _inproj(xb, wall, off, b_in, cos_t, sin_t):
    T = xb.shape[0]
    tm, tn = min(T, 512), 512
    assert off == 0

    def body(x_ref, w_ref, bias_ref, cos_ref, sin_ref, o_ref):
        x = x_ref[...]
        c = cos_ref[...]
        s = sin_ref[...]
        for n0 in range(0, INW, tn):
            chip, c0 = divmod(n0, INS)
            acc = _dot(x, w_ref[chip, :, c0:c0 + tn]) + bias_ref[:, n0:n0 + tn]
            if n0 >= 2 * D:
                o_ref[:, n0:n0 + tn] = acc.astype(BF16)
                continue
            scale = DK ** -0.5 if n0 < D else 1.0
            for hh in range(tn // DK):
                xh = acc[:, hh * DK:(hh + 1) * DK]
                o = (xh * c + pltpu.roll(xh, DK // 2, 1) * s) * scale
                o_ref[:, n0 + hh * DK:n0 + (hh + 1) * DK] = o.astype(BF16)

    return _pcall(
        body, name="inproj", grid=(T // tm,),
        in_specs=[pl.BlockSpec((tm, D), lambda i: (i, 0)),
                  _resident((NCHIP, D, INS), 0),
                  pl.BlockSpec((1, INW), lambda i: (0, 0)),
                  pl.BlockSpec((tm, DK), lambda i: (i, 0)),
                  pl.BlockSpec((tm, DK), lambda i: (i, 0))],
        out_specs=pl.BlockSpec((tm, INW), lambda i: (i, 0)),
        out_shape=jax.ShapeDtypeStruct((T, INW), BF16),
        compiler_params=_cp("parallel"))(xb, wall, b_in, cos_t, sin_t)


RET_CPS = 2


def _retention_fwd(proj, gn_g, dm_t, xi_t, zeta_t, cds):
    T = proj.shape[0]
    n = T // CH
    tr = RET_CPS * CH

    def body(q_ref, k_ref, v_ref, g_ref, gn_ref, dm_ref, xi_ref, zt_ref, r_ref, ri_ref, st_ref, state):
        @pl.when(pl.program_id(0) == 0)
        def _():
            state[...] = jnp.zeros_like(state)

        for h in range(H):
            rows = slice(h * DK, (h + 1) * DK)
            cols = slice(h * DV, (h + 1) * DV)
            s_prev = state[rows, :]
            for j in range(RET_CPS):
                t = slice(j * CH, (j + 1) * CH)
                q = q_ref[t, h * DK:(h + 1) * DK]
                k = k_ref[t, h * DK:(h + 1) * DK]
                v = v_ref[t, cols]
                s_b = s_prev.astype(BF16)
                st_ref[j, rows, :] = s_b
                sc = _dot_nt(q, k) * dm_ref[h]
                r = _dot(sc.astype(BF16), v) + _dot(q, s_b) * xi_ref[h]
                kz = k.astype(F32) * zt_ref[h]
                s_prev = cds[h] * s_prev + _dot(kz.T.astype(BF16), v)
                r_ref[t, cols] = r
                mu = jnp.mean(r, axis=-1, keepdims=True)
                xc = r - mu
                var = jnp.mean(xc * xc, axis=-1, keepdims=True)
                y = xc * lax.rsqrt(var + EPS) * gn_ref[:, cols]
                g = g_ref[t, cols].astype(F32)
                ri_ref[t, cols] = (g * _sig(g) * y).astype(BF16)
            state[rows, :] = s_prev

    full3 = lambda shp: pl.BlockSpec(shp, lambda c: (0, 0, 0))
    return _pcall(
        body, name="retention_fwd", grid=(n // RET_CPS,),
        in_specs=[pl.BlockSpec((tr, D), lambda c: (c, 0)),
                  pl.BlockSpec((tr, D), lambda c: (c, 1)),
                  pl.BlockSpec((tr, VW), lambda c: (c, 1)),
                  pl.BlockSpec((tr, VW), lambda c: (c, 2)),
                  pl.BlockSpec((1, VW), lambda c: (0, 0)),
                  full3((H, CH, CH)), full3((H, CH, DV)), full3((H, CH, DK))],
        out_specs=[pl.BlockSpec((tr, VW), lambda c: (c, 0)), pl.BlockSpec((tr, VW), lambda c: (c, 0)),
                   pl.BlockSpec((RET_CPS, H * DK, DV), lambda c: (c, 0, 0))],
        out_shape=[jax.ShapeDtypeStruct((T, VW), F32), jax.ShapeDtypeStruct((T, VW), BF16),
                   jax.ShapeDtypeStruct((n, H * DK, DV), BF16)],
        scratch_shapes=[pltpu.VMEM((H * DK, DV), F32)],
        compiler_params=_cp("arbitrary"))(proj, proj, proj, proj, gn_g, dm_t, xi_t, zeta_t)


CONV_TT = 256
CONV_SB = 64
CONV_CB = 256


SUB = 8
CONV_PAD = 8


def _glu(a_ref, b_ref, rows=slice(None)):
    a = a_ref[rows, :].astype(F32)
    sb = _sig(b_ref[rows, :].astype(F32))
    return a, sb


def _shift_copies(win, sh, rows):
    win[rows:rows + CONV_PAD, :] = jnp.zeros((CONV_PAD, D), F32)
    for b in range(1, SUB):
        sh[b - 1, :, :] = win[b:b + rows, :]


def _tap(win, sh, start, size, cs):
    a, b = divmod(start, SUB)
    src = win if b == 0 else sh.at[b - 1]
    return src[SUB * a:SUB * a + size, cs]


def _conv_fwd(proj, conv_k, conv_b, ln_g, ln_b):
    T = proj.shape[0]
    tt = min(T, CONV_TT)
    ca, cb = 6 * D // D, 7 * D // D

    def body(a_ref, b_ref, pa_ref, pb_ref, k_ref, cb_ref, g_ref, bb_ref, u1_ref, u3_ref, win, sh):
        i = pl.program_id(0)
        a, sb = _glu(a_ref, b_ref)
        win[HALO:tt + HALO, :] = a * sb
        pa, psb = _glu(pa_ref, pb_ref, slice(tt - HALO, tt))
        win[0:HALO, :] = jnp.where(i > 0, pa * psb, 0.0)
        _shift_copies(win, sh, tt + HALO)
        for c0 in range(0, D, CONV_CB):
            cs = slice(c0, c0 + CONV_CB)
            for r0 in range(0, tt, CONV_SB):
                acc = jnp.zeros((CONV_SB, CONV_CB), F32)
                for w in range(CONV_W):
                    st = r0 + HALO - (CONV_W - 1) + w
                    acc += _tap(win, sh, st, CONV_SB, cs) * k_ref[w:w + 1, cs]
                u1_ref[r0:r0 + CONV_SB, cs] = acc + cb_ref[:, cs]
        u2, _, _ = _ln_fwd(u1_ref[...], g_ref[...], bb_ref[...])
        u3_ref[...] = (u2 * _sig(u2)).astype(BF16)

    vec = pl.BlockSpec((1, D), lambda i: (0, 0))
    row = pl.BlockSpec((tt, D), lambda i: (i, 0))
    return _pcall(
        body, name="conv_fwd", grid=(T // tt,),
        in_specs=[pl.BlockSpec((tt, D), lambda i: (i, ca)), pl.BlockSpec((tt, D), lambda i: (i, cb)),
                  pl.BlockSpec((tt, D), lambda i: (jnp.maximum(i - 1, 0), ca)),
                  pl.BlockSpec((tt, D), lambda i: (jnp.maximum(i - 1, 0), cb)),
                  pl.BlockSpec((CONV_W, D), lambda i: (0, 0)), vec, vec, vec],
        out_specs=[row, row],
        out_shape=[jax.ShapeDtypeStruct((T, D), F32), jax.ShapeDtypeStruct((T, D), BF16)],
        scratch_shapes=[pltpu.VMEM((tt + HALO + CONV_PAD, D), F32), pltpu.VMEM((SUB - 1, tt + HALO, D), F32)],
        compiler_params=_cp("parallel"))(proj, proj, proj, proj, conv_k, conv_b, ln_g, ln_b)


def _merge(ret_in, u3, proj, wall, off_r, off_c):
    T = ret_in.shape[0]
    tm = min(T, 512)
    kr, kc = VW // NCHIP, D // NCHIP

    def body(ri_ref, u3_ref, gr_ref, gc_ref, wr_ref, wc_ref, ro_ref, co_ref, m_ref):
        ro = _dot_nt(ri_ref[:, 0:kr], wr_ref[0])
        co = _dot_nt(u3_ref[:, 0:kc], wc_ref[0])
        for s in range(1, NCHIP):
            ro += _dot_nt(ri_ref[:, s * kr:(s + 1) * kr], wr_ref[s])
            co += _dot_nt(u3_ref[:, s * kc:(s + 1) * kc], wc_ref[s])
        ro_ref[...] = ro.astype(BF16)
        co_ref[...] = co.astype(BF16)
        m = _sig(gr_ref[...].astype(F32)) * ro + _sig(gc_ref[...].astype(F32)) * co
        m_ref[...] = m.astype(BF16)

    row = pl.BlockSpec((tm, D), lambda i: (i, 0))
    return _pcall(
        body, name="merge", grid=(T // tm,),
        in_specs=[pl.BlockSpec((tm, VW), lambda i: (i, 0)), row,
                  pl.BlockSpec((tm, D), lambda i: (i, 8)), pl.BlockSpec((tm, D), lambda i: (i, 9)),
                  pl.BlockSpec((NCHIP, D, kr), lambda i: (0, 0, off_r // kr)),
                  pl.BlockSpec((NCHIP, D, kc), lambda i: (0, 0, off_c // kc))],
        out_specs=[row] * 3, out_shape=[jax.ShapeDtypeStruct((T, D), BF16)] * 3,
        compiler_params=_cp("parallel"))(ret_in, u3, proj, proj, wall, wall)


def _loss_ln_bwd(z, g, b, target, coef):
    T = z.shape[0]
    tm = min(T, 256)
    nt = T // tm

    def body(z_ref, g_ref, b_ref, t_ref, loss_ref, dzb_ref, dzt_ref, dz_ref, dg_ref, db_ref, lacc):
        i = pl.program_id(0)
        _zero_first(i == 0, lacc, dg_ref, db_ref)
        gam = g_ref[...]
        y, xh, rstd = _ln_fwd(z_ref[...], gam, b_ref[...])
        e = y - t_ref[...]
        lacc[...] += _colsum(e * e)
        dy = e * (1.0 / D)
        dz = _ln_bwd(dy, xh, rstd, gam)
        dz_ref[...] = dz
        dzc = coef * dz
        dzb_ref[...] = dzc.astype(BF16)
        dzt_ref[...] = dzc.T.astype(BF16)
        dg_ref[...] += _colsum(dy * xh)
        db_ref[...] += _colsum(dy)

        @pl.when(i == nt - 1)
        def _():
            loss_ref[...] = (0.5 / D) * jnp.sum(lacc[...], axis=1, keepdims=True)

    row = pl.BlockSpec((tm, D), lambda i: (i, 0))
    vec = pl.BlockSpec((1, D), lambda i: (0, 0))
    return _pcall(
        body, name="loss_ln_bwd", grid=(nt,),
        in_specs=[row, vec, vec, row],
        out_specs=[pl.BlockSpec((1, 1), lambda i: (0, 0)), row, pl.BlockSpec((D, tm), lambda i: (0, i)),
                   row, vec, vec],
        out_shape=[jax.ShapeDtypeStruct((1, 1), F32), jax.ShapeDtypeStruct((T, D), BF16),
                   jax.ShapeDtypeStruct((D, T), BF16), jax.ShapeDtypeStruct((T, D), F32),
                   jax.ShapeDtypeStruct((1, D), F32), jax.ShapeDtypeStruct((1, D), F32)],
        scratch_shapes=[pltpu.VMEM((1, D), F32)],
        compiler_params=_cp("arbitrary"))(z, g, b, target)


def _ffn_bwd_h(dfb, wall, od, a, b, name):
    T = dfb.shape[0]
    tm = min(T, 512)

    def body(d_ref, w_ref, a_ref, b_ref, da_ref, db_ref):
        d = d_ref[...]
        for s in range(NCHIP):
            cols = slice(s * FSP, (s + 1) * FSP)
            dh = _dot(d, w_ref[s])
            a = a_ref[:, cols].astype(F32)
            sg = _sig(a)
            da_ref[:, cols] = (dh * b_ref[:, cols].astype(F32) * (sg * (1.0 + a * (1.0 - sg)))).astype(BF16)
            db_ref[:, cols] = (dh * a * sg).astype(BF16)

    ospec = pl.BlockSpec((tm, FP), lambda i: (i, 0))
    return _pcall(
        body, name=name, grid=(T // tm,),
        in_specs=[pl.BlockSpec((tm, D), lambda i: (i, 0)), _resident((NCHIP, D, FSP), od // FSP), ospec, ospec],
        out_specs=[ospec] * 2, out_shape=[jax.ShapeDtypeStruct((T, FP), BF16)] * 2,
        compiler_params=_cp("parallel"))(dfb, wall, a, b)


DX_SUB = 256


def _dx_partial(lhs, wall, chips, name):
    T, K = lhs.shape
    ks = K // NCHIP
    nc = len(chips)
    assert list(chips) == list(range(chips[0], chips[0] + nc)) and chips[0] % nc == 0
    tm = min(T, 512)

    def body(l_ref, w_ref, o_ref):
        for r0 in range(0, tm, DX_SUB):
            r = slice(r0, r0 + DX_SUB)
            acc = _dot_nt(l_ref[r, 0:ks], w_ref[0])
            for s in range(1, nc):
                acc += _dot_nt(l_ref[r, s * ks:(s + 1) * ks], w_ref[s])
            o_ref[r, :] = acc

    blk = chips[0] // nc
    return _pcall(
        body, name=name, grid=(T // tm,),
        in_specs=[pl.BlockSpec((tm, nc * ks), lambda i: (i, blk)),
                  pl.BlockSpec((nc, D, ks), lambda i: (blk, 0, 0), pipeline_mode=pl.Buffered(1))],
        out_specs=pl.BlockSpec((tm, D), lambda i: (i, 0)),
        out_shape=jax.ShapeDtypeStruct((T, D), F32),
        compiler_params=_cp("parallel"))(lhs, wall)


def _dx_bwd(lhs, offs, wall, dz_next, name, ln=None, chips=tuple(range(NCHIP)), partial=None):
    T, K = lhs[0].shape
    ks = K // NCHIP
    nl = len(lhs)
    nc = len(chips)
    assert list(offs) == [l * ks for l in range(nl)]
    assert list(chips) == list(range(chips[0], chips[0] + nc)) and chips[0] % nc == 0
    blk = chips[0] // nc
    tm = min(T, 512)

    def body(*refs):
        l_refs = refs[:nl]
        w_ref = refs[nl]
        dzn_ref = refs[nl + 1]
        pos = nl + 2
        if partial is not None:
            p_ref = refs[pos]
            pos += 1
        if ln is not None:
            z_ref, g_ref = refs[pos:pos + 2]
            pos += 2
        outs = refs[pos:]
        sums = list(outs[3:]) if ln is not None else list(outs[1:])

        _zero_first(pl.program_id(0) == 0, *sums)

        for r0 in range(0, tm, DX_SUB):
            r = slice(r0, r0 + DX_SUB)
            acc = None if partial is None else p_ref[r, :]
            for s in range(nc):
                rows = slice(s * ks, (s + 1) * ks)
                for l in range(nl):
                    part = _dot_nt(l_refs[l][r, rows], w_ref[s, :, l * ks:(l + 1) * ks])
                    acc = part if acc is None else acc + part
            dy = acc + ALPHA * dzn_ref[r, :]
            if ln is None:
                outs[0][r, :] = dy
            else:
                gam = g_ref[...]
                _, xh, rstd = _ln_fwd(z_ref[r, :], gam, 0.0)
                dz = _ln_bwd(dy, xh, rstd, gam)
                dzc = ln[2] * dz
                outs[0][r, :] = dzc.astype(BF16)
                outs[1][:, r] = dzc.T.astype(BF16)
                outs[2][r, :] = dz
                outs[3][...] += _colsum(dy * xh)
                outs[4][...] += _colsum(dy)

    row = pl.BlockSpec((tm, D), lambda i: (i, 0))
    vec = pl.BlockSpec((1, D), lambda i: (0, 0))
    in_specs = [pl.BlockSpec((tm, nc * ks), lambda i: (i, blk))] * nl
    in_specs += [pl.BlockSpec((nc, D, nl * ks), lambda i: (blk, 0, 0), pipeline_mode=pl.Buffered(1)), row]
    args = list(lhs) + [wall, dz_next]
    if partial is not None:
        in_specs.append(row)
        args.append(partial)
    if ln is None:
        out_specs = [row]
        out_shape = [jax.ShapeDtypeStruct((T, D), F32)]
    else:
        in_specs += [row, vec]
        args += [ln[0], ln[1]]
        out_specs = [row, pl.BlockSpec((D, tm), lambda i: (0, i)), row, vec, vec]
        out_shape = [jax.ShapeDtypeStruct((T, D), BF16), jax.ShapeDtypeStruct((D, T), BF16),
                     jax.ShapeDtypeStruct((T, D), F32), jax.ShapeDtypeStruct((1, D), F32),
                     jax.ShapeDtypeStruct((1, D), F32)]
    return _pcall(
        body, name=name, grid=(T // tm,), in_specs=in_specs, out_specs=out_specs, out_shape=out_shape,
        compiler_params=_cp("arbitrary"))(*args)


def _wgrad(lhs_t, rhs, key, name, g_all=None, colsum=False):
    T, N = rhs.shape
    tn = next(c for c in (768, 512, 256) if (N // NCHIP) % c == 0 and LOC[key][1] % c == 0)
    nps = N // NCHIP // tn
    off = LOC[key][1]
    cols = GCOLS[LOC[key][0]]

    def body(*refs):
        l_ref, r_ref = refs[0], refs[1]
        o_ref, t_ref = refs[-2 - colsum], refs[-1]
        o_ref[...] = _dot(l_ref[...], r_ref[...]).astype(BF16)
        if colsum:
            refs[-2][...] = _colsum(r_ref[...].astype(F32))
        t_ref[...] = jnp.zeros_like(t_ref)

    in_specs = [_resident((D, T), 0), pl.BlockSpec((T, tn), lambda j: (0, j))]
    args = [lhs_t, rhs]
    aliases = {}
    if g_all is not None:
        in_specs.append(pl.BlockSpec(memory_space=pl.ANY))
        args.append(g_all)
        aliases = {2: 0}
    out_specs = [pl.BlockSpec((None, D, tn), lambda j: (j // nps, 0, off // tn + j % nps))]
    out_shape = [jax.ShapeDtypeStruct((NCHIP, D, cols), BF16)]
    if colsum:
        out_specs.append(pl.BlockSpec((1, tn), lambda j: (0, j)))
        out_shape.append(jax.ShapeDtypeStruct((1, N), F32))
    out_specs.append(pl.BlockSpec((8, 128), lambda j: (0, 0)))
    out_shape.append(jax.ShapeDtypeStruct((8, 128), F32))
    res = _pcall(
        body, name=name, grid=(N // tn,), in_specs=in_specs, out_specs=out_specs, out_shape=out_shape,
        input_output_aliases=aliases,
        compiler_params=_cp("arbitrary"))(*args)
    return (res[0], res[1]) if colsum else res[0]


def _merge_bwd(dmb, wall, off, proj, ro, co):
    T = dmb.shape[0]
    tm = min(T, 512)
    ks = D // NCHIP

    def body(d_ref, w_ref, gr_ref, gc_ref, ro_ref, co_ref, dro_ref, drot_ref, dco_ref, dcot_ref, dp_ref):
        d = d_ref[...]
        dmg = jnp.concatenate([_dot(d, w_ref[s]) for s in range(NCHIP)], axis=1)
        sr = _sig(gr_ref[...].astype(F32))
        sc = _sig(gc_ref[...].astype(F32))
        dro = dmg * sr
        dco = dmg * sc
        dro_ref[...] = dro.astype(BF16)
        drot_ref[...] = dro.T.astype(BF16)
        dco_ref[...] = dco.astype(BF16)
        dcot_ref[...] = dco.T.astype(BF16)
        dp_ref[:, 0:D] = (dmg * ro_ref[...].astype(F32) * sr * (1.0 - sr)).astype(BF16)
        dp_ref[:, D:2 * D] = (dmg * co_ref[...].astype(F32) * sc * (1.0 - sc)).astype(BF16)

    row = pl.BlockSpec((tm, D), lambda i: (i, 0))
    col = pl.BlockSpec((D, tm), lambda i: (0, i))
    return _pcall(
        body, name="merge_bwd", grid=(T // tm,),
        in_specs=[row, pl.BlockSpec((NCHIP, D, ks), lambda i: (0, 0, off // ks)),
                  pl.BlockSpec((tm, D), lambda i: (i, 8)), pl.BlockSpec((tm, D), lambda i: (i, 9)), row, row],
        out_specs=[row, col, row, col, pl.BlockSpec((tm, 2 * D), lambda i: (i, 4))],
        out_shape=[jax.ShapeDtypeStruct((T, D), BF16), jax.ShapeDtypeStruct((D, T), BF16),
                   jax.ShapeDtypeStruct((T, D), BF16), jax.ShapeDtypeStruct((D, T), BF16),
                   jax.ShapeDtypeStruct((T, INW), BF16)],
        compiler_params=_cp("parallel"))(dmb, wall, proj, proj, ro, co)


def _reto_bwd(dro, wall, off, r, proj, gn_g, dproj):
    T = dro.shape[0]
    tm = min(T, 512)
    hps = H // NCHIP

    def body(d_ref, w_ref, r_ref, g_ref, gn_ref, _, dr_ref, dgn_ref, dp_ref):
        _zero_first(pl.program_id(1) == 0, dgn_ref)
        dri = _dot(d_ref[...], w_ref[...])
        rr = r_ref[...]
        mu = jnp.mean(rr, axis=-1, keepdims=True)
        xc = rr - mu
        var = jnp.mean(xc * xc, axis=-1, keepdims=True)
        rstd = lax.rsqrt(var + EPS)
        rn = xc * rstd
        gn = gn_ref[...]
        g = g_ref[...].astype(F32)
        sg = _sig(g)
        dy = dri * (g * sg)
        dp_ref[...] = (dri * (rn * gn) * (sg * (1.0 + g * (1.0 - sg)))).astype(BF16)
        dgn_ref[...] += _colsum(dy * rn)
        dr_ref[...] = _ln_bwd(dy, rn, rstd, gn).astype(BF16)

    return _pcall(
        body, name="reto_bwd", grid=(H, T // tm),
        in_specs=[pl.BlockSpec((tm, D), lambda j, i: (i, 0)),
                  pl.BlockSpec((None, D, DV), lambda j, i: (j // hps, 0, off // DV + j % hps)),
                  pl.BlockSpec((tm, DV), lambda j, i: (i, j)),
                  pl.BlockSpec((tm, DV), lambda j, i: (i, 2 * VW // DV + j)),
                  pl.BlockSpec((1, DV), lambda j, i: (0, j)),
                  pl.BlockSpec(memory_space=pl.ANY)],
        out_specs=[pl.BlockSpec((tm, DV), lambda j, i: (i, j)), pl.BlockSpec((1, DV), lambda j, i: (0, j)),
                   pl.BlockSpec((tm, DV), lambda j, i: (i, 2 * VW // DV + j))],
        out_shape=[jax.ShapeDtypeStruct((T, VW), BF16), jax.ShapeDtypeStruct((1, VW), F32),
                   jax.ShapeDtypeStruct((T, INW), BF16)],
        input_output_aliases={5: 2},
        compiler_params=_cp("arbitrary", "arbitrary"))(dro, wall, r, proj, gn_g, dproj)


def _retention_bwd(proj, dr, states, cos_t, sin_t, dm_t, xi_t, zeta_t, cds, dproj):
    T = proj.shape[0]
    n = T // CH // RET_CPS
    tr = RET_CPS * CH
    scale = DK ** -0.5

    def body(q_ref, k_ref, v_ref, dr_ref, st_ref, cos_ref, sin_ref, dm_ref, xi_ref, zt_ref, _, dp_ref, ds):
        @pl.when(pl.program_id(0) == 0)
        def _():
            ds[...] = jnp.zeros_like(ds)

        def unrope(d, t):
            return d * cos_ref[t, :] + pltpu.roll(d * sin_ref[t, :], DK // 2, 1)

        for h in range(H):
            rows = slice(h * DK, (h + 1) * DK)
            dm = dm_ref[h]
            zt = zt_ref[h]
            ds_prev = ds[rows, :]
            for j in reversed(range(RET_CPS)):
                t = slice(j * CH, (j + 1) * CH)
                q = q_ref[t, h * DK:(h + 1) * DK]
                k = k_ref[t, h * DK:(h + 1) * DK]
                v = v_ref[t, h * DV:(h + 1) * DV]
                d_r = dr_ref[t, h * DV:(h + 1) * DV]
                s_b = st_ref[j, rows, :]
                sc = _dot_nt(q, k) * dm
                dsc = _dot_nt(d_r, v) * dm
                drx = (d_r.astype(F32) * xi_ref[h]).astype(BF16)
                ds_b = ds_prev.astype(BF16)
                kz = (k.astype(F32) * zt).astype(BF16)
                dq = _dot(dsc.astype(BF16), k) + _dot_nt(drx, s_b)
                dk = _dot(dsc.T.astype(BF16), q) + _dot_nt(v, ds_b) * zt
                dv = _dot(sc.T.astype(BF16), d_r) + _dot(kz, ds_b)
                ds_prev = cds[h] * ds_prev + _dot(q.astype(F32).T.astype(BF16), drx)
                dp_ref[t, h * DK:(h + 1) * DK] = unrope(dq * scale, t).astype(BF16)
                dp_ref[t, D + h * DK:D + (h + 1) * DK] = unrope(dk, t).astype(BF16)
                dp_ref[t, 2 * D + h * DV:2 * D + (h + 1) * DV] = dv.astype(BF16)
            ds[rows, :] = ds_prev

    rv = lambda c: n - 1 - c
    full3 = lambda shp: pl.BlockSpec(shp, lambda c: (0, 0, 0))
    return _pcall(
        body, name="retention_bwd", grid=(n,),
        in_specs=[pl.BlockSpec((tr, D), lambda c: (rv(c), 0)),
                  pl.BlockSpec((tr, D), lambda c: (rv(c), 1)),
                  pl.BlockSpec((tr, VW), lambda c: (rv(c), 1)),
                  pl.BlockSpec((tr, VW), lambda c: (rv(c), 0)),
                  pl.BlockSpec((RET_CPS, H * DK, DV), lambda c: (rv(c), 0, 0)),
                  pl.BlockSpec((tr, DK), lambda c: (rv(c), 0)),
                  pl.BlockSpec((tr, DK), lambda c: (rv(c), 0)),
                  full3((H, CH, CH)), full3((H, CH, DV)), full3((H, CH, DK)),
                  pl.BlockSpec(memory_space=pl.ANY)],
        out_specs=pl.BlockSpec((tr, 2 * D + VW), lambda c: (rv(c), 0)),
        out_shape=jax.ShapeDtypeStruct((T, INW), BF16),
        input_output_aliases={10: 0},
        scratch_shapes=[pltpu.VMEM((H * DK, DV), F32)],
        compiler_params=_cp("arbitrary"))(proj, proj, proj, dr, states, cos_t, sin_t, dm_t, xi_t, zeta_t, dproj)


def _convo_bwd(dco, wall, off, u1, ln_g, ln_b):
    T = dco.shape[0]
    tm = min(T, 512)
    ks = D // NCHIP

    def body(d_ref, w_ref, u1_ref, g_ref, b_ref, du1_ref, dg_ref, db_ref, dcb_ref):
        _zero_first(pl.program_id(0) == 0, dg_ref, db_ref, dcb_ref)
        d = d_ref[...]
        du3 = jnp.concatenate([_dot(d, w_ref[s]) for s in range(NCHIP)], axis=1)
        gam = g_ref[...]
        u2, xh, rstd = _ln_fwd(u1_ref[...], gam, b_ref[...])
        sg = _sig(u2)
        du2 = du3 * (sg * (1.0 + u2 * (1.0 - sg)))
        du1 = _ln_bwd(du2, xh, rstd, gam)
        du1_ref[...] = du1
        dg_ref[...] += _colsum(du2 * xh)
        db_ref[...] += _colsum(du2)
        dcb_ref[...] += _colsum(du1)

    row = pl.BlockSpec((tm, D), lambda i: (i, 0))
    vec = pl.BlockSpec((1, D), lambda i: (0, 0))
    return _pcall(
        body, name="convo_bwd", grid=(T // tm,),
        in_specs=[row, pl.BlockSpec((NCHIP, D, ks), lambda i: (0, 0, off // ks)), row, vec, vec],
        out_specs=[row, vec, vec, vec],
        out_shape=[jax.ShapeDtypeStruct((T, D), F32)] + [jax.ShapeDtypeStruct((1, D), F32)] * 3,
        compiler_params=_cp("arbitrary"))(dco, wall, u1, ln_g, ln_b)


def _conv_bwd(du1, proj, conv_k, dproj):
    T = du1.shape[0]
    tt = min(T, CONV_TT)
    nt = T // tt
    ca, cb = 6, 7

    def body(d_ref, dn_ref, a_ref, b_ref, pa_ref, pb_ref, k_ref, _, dp_ref, dk_ref, win_u, win_d, sh_u, sh_d):
        i = pl.program_id(0)
        a, sb = _glu(a_ref, b_ref)
        win_u[HALO:tt + HALO, :] = a * sb
        pa, psb = _glu(pa_ref, pb_ref, slice(tt - HALO, tt))
        win_u[0:HALO, :] = jnp.where(i > 0, pa * psb, 0.0)
        win_d[0:tt, :] = d_ref[...]
        win_d[tt:tt + HALO, :] = jnp.where(i < nt - 1, dn_ref[0:HALO, :], 0.0)
        _shift_copies(win_u, sh_u, tt + HALO)
        _shift_copies(win_d, sh_d, tt + HALO)

        @pl.when(i == 0)
        def _():
            dk_ref[...] = jnp.zeros_like(dk_ref)

        for c0 in range(0, D, CONV_CB):
            cs = slice(c0, c0 + CONV_CB)
            for r0 in range(0, tt, CONV_SB):
                acc = jnp.zeros((CONV_SB, CONV_CB), F32)
                for w in range(CONV_W):
                    st = r0 + (CONV_W - 1) - w
                    acc += _tap(win_d, sh_d, st, CONV_SB, cs) * k_ref[w:w + 1, cs]
                aa = a_ref[r0:r0 + CONV_SB, cs].astype(F32)
                ss = _sig(b_ref[r0:r0 + CONV_SB, cs].astype(F32))
                dp_ref[r0:r0 + CONV_SB, cs] = (acc * ss).astype(BF16)
                dp_ref[r0:r0 + CONV_SB, c0 + D:c0 + D + CONV_CB] = (acc * aa * ss * (1.0 - ss)).astype(BF16)
        for c0 in range(0, D, CONV_CB):
            cs = slice(c0, c0 + CONV_CB)
            for w in range(CONV_W):
                acc = jnp.zeros((CONV_SB, CONV_CB), F32)
                for r0 in range(0, tt, CONV_SB):
                    st = r0 + HALO - (CONV_W - 1) + w
                    acc += win_d[r0:r0 + CONV_SB, cs] * _tap(win_u, sh_u, st, CONV_SB, cs)
                dk_ref[w:w + 1, cs] += _colsum(acc)

    blk = lambda f, c: pl.BlockSpec((tt, D), lambda i: (f(i), c))
    cur = lambda i: i
    prv = lambda i: jnp.maximum(i - 1, 0)
    nxt = lambda i: jnp.minimum(i + 1, nt - 1)
    return _pcall(
        body, name="conv_bwd", grid=(nt,),
        in_specs=[blk(cur, 0), blk(nxt, 0), blk(cur, ca), blk(cur, cb), blk(prv, ca), blk(prv, cb),
                  pl.BlockSpec((CONV_W, D), lambda i: (0, 0)), pl.BlockSpec(memory_space=pl.ANY)],
        out_specs=[pl.BlockSpec((tt, 2 * D), lambda i: (i, 3)), pl.BlockSpec((HALO, D), lambda i: (0, 0))],
        out_shape=[jax.ShapeDtypeStruct((T, INW), BF16), jax.ShapeDtypeStruct((HALO, D), F32)],
        input_output_aliases={7: 0},
        scratch_shapes=[pltpu.VMEM((tt + HALO + CONV_PAD, D), F32), pltpu.VMEM((tt + HALO + CONV_PAD, D), F32),
                        pltpu.VMEM((SUB - 1, tt + HALO, D), F32), pltpu.VMEM((SUB - 1, tt + HALO, D), F32)],
        compiler_params=_cp("arbitrary"))(du1, du1, proj, proj, proj, proj, conv_k, dproj)


def _local_step(x, target, wts, sp, kvec, pos_c, pos_sc, adam):
    T = x.shape[0]
    cos_t, sin_t = _rope_tables(T)
    dm_t, xi_t, zeta_t, cds = _decay_tables()
    wa = lambda key: wts[LOC_W[key][0]]
    wo = lambda key: LOC_W[key][1]
    _ORDER.active, _ORDER.token = True, None

    xb, xt = _cast_t(x)
    a1, b1, h1 = _ffn_up(xb, wa("g1"), wo("g1"), wo("u1"), "ffn1_up")
    z1, x1, x1b, x1t = _proj_ln(h1, wa("d1"), wo("d1"), x, sp["ln1_g"], sp["ln1_b"], 0.5, "ffn1_down_ln")
    proj = _inproj(x1b, wa("w_in"), wo("w_in"), sp["b_in"], cos_t, sin_t)
    r, ret_in, states = _retention_fwd(proj, sp["ret_gn_g"], dm_t, xi_t, zeta_t, cds)
    kall = _sum_devices(_all_gather_small(kvec, "gather_conv_k"), "sum_conv_k")
    sp = dict(sp, conv_k=kall.reshape(-1)[:CONV_W * D].reshape(CONV_W, D))
    u1, u3 = _conv_fwd(proj, sp["conv_k"], sp["conv_b"], sp["conv_ln_g"], sp["conv_ln_b"])
    ro, co, merged = _merge(ret_in, u3, proj, wa("w_ret_o"), wo("w_ret_o"), wo("w_conv_o"))
    z2, x2, x2b, x2t = _proj_ln(merged, wa("w_out"), wo("w_out"), x1, sp["ln2_g"], sp["ln2_b"], 1.0, "out_proj_ln")
    a2, b2, h2 = _ffn_up(x2b, wa("g2"), wo("g2"), wo("u2"), "ffn2_up")
    (z3,) = _proj_ln(h2, wa("d2"), wo("d2"), x2, sp["ln3_g"], sp["ln3_b"], 0.5, "ffn2_down", want_b=False)

    sg = {}
    rs = {}
    loss, df2b, df2t, dz3, sg["ln3_g"], sg["ln3_b"] = _loss_ln_bwd(z3, sp["ln3_g"], sp["ln3_b"], target, 0.5)
    da2, db2 = _ffn_bwd_h(df2b, wa("d2"), wo("d2"), a2, b2, "ffn2_bwd_h")
    g4 = _wgrad(df2t, h2, "d2", "wgrad_d2")
    g4 = _wgrad(x2t, da2, "g2", "wgrad_g2", g4)
    g4 = _wgrad(x2t, db2, "u2", "wgrad_u2", g4)
    rs[4] = _ReduceScatter(g4, 4, pos_c, pos_sc)
    dmb, dmt, dz2, sg["ln2_g"], sg["ln2_b"] = _dx_bwd(
        [da2, db2], [wo("g2"), wo("u2")], wa("g2"), dz3, "ffn2_dx_ln", ln=(z2, sp["ln2_g"], 1.0))
    rs[4].phase2()
    g3 = _wgrad(dmt, merged, "w_out", "wgrad_out")
    dro, drot, dco, dcot, dproj = _merge_bwd(dmb, wa("w_out"), wo("w_out"), proj, ro, co)
    g3 = _wgrad(drot, ret_in, "w_ret_o", "wgrad_ret_o", g3)
    g3 = _wgrad(dcot, u3, "w_conv_o", "wgrad_conv_o", g3)
    rs[3] = _ReduceScatter(g3, 3, pos_c, pos_sc)
    dr, sg["ret_gn_g"], dproj = _reto_bwd(dro, wa("w_ret_o"), wo("w_ret_o"), r, proj, sp["ret_gn_g"], dproj)
    rs[4].phase3()
    rs[3].phase2()
    dproj = _retention_bwd(proj, dr, states, cos_t, sin_t, dm_t, xi_t, zeta_t, cds, dproj)
    du1, sg["conv_ln_g"], sg["conv_ln_b"], sg["conv_b"] = _convo_bwd(
        dco, wa("w_conv_o"), wo("w_conv_o"), u1, sp["conv_ln_g"], sp["conv_ln_b"])
    dproj, dck = _conv_bwd(du1, proj, sp["conv_k"], dproj)
    sg["conv_k"] = dck[:CONV_W]
    adam(4, rs[4].result())
    rs[3].phase3()
    g2, sg["b_in"] = _wgrad(x1t, dproj, "w_in", "wgrad_in", colsum=True)
    rs[2] = _ReduceScatter(g2, 2, pos_c, pos_sc)
    dx_part = _dx_partial(dproj, wa("w_in"), (0, 1), "mixer_dx_part")
    df1b, df1t, dz1, sg["ln1_g"], sg["ln1_b"] = _dx_bwd(
        [dproj], [wo("w_in")], wa("w_in"), dz2, "mixer_dx_ln", ln=(z1, sp["ln1_g"], 0.5), chips=(2, 3),
        partial=dx_part)
    adam(3, rs[3].result())
    rs[2].phase2()
    shapes = {n: sg[n].shape for n in SMALL + ["conv_k"]}
    small_parts = _all_gather_small(_pack_small(sg, loss, SMALL_ROWS), "gather_small")
    da1, db1 = _ffn_bwd_h(df1b, wa("d1"), wo("d1"), a1, b1, "ffn1_bwd_h")
    small_sum = _sum_devices(small_parts, "sum_small")
    g1 = _wgrad(df1t, h1, "d1", "wgrad_d1")
    rs[1] = _ReduceScatter(g1, 1, pos_c, pos_sc)
    g0 = _wgrad(xt, da1, "g1", "wgrad_g1")
    g0 = _wgrad(xt, db1, "u1", "wgrad_u1", g0)
    rs[0] = _ReduceScatter(g0, 0, pos_c, pos_sc)
    rs[2].phase3()
    rs[1].phase2()
    rs[0].phase2()
    (grad_x,) = _dx_bwd([da1, db1], [wo("g1"), wo("u1")], wa("g1"), dz1, "ffn1_dx")
    adam(2, rs[2].result())
    rs[1].phase3()
    rs[0].phase3()
    adam(1, rs[1].result())
    adam(0, rs[0].result())
    _ORDER.active = False
    return grad_x, small_sum, shapes


MESH = pl.DeviceIdType.MESH
ANY = pl.BlockSpec(memory_space=pl.ANY)
HALF = D // 2


def _place():
    x, y, c = lax.axis_index("x"), lax.axis_index("y"), lax.axis_index("c")
    chips = [(1 - x, y), (x, 1 - y), (1 - x, 1 - y)]
    return x, y, c, chips


GATHER_PIECES = 4
GATHER_ID = 1


def _gather_weights(wloc, name):
    w_ref = jax.new_ref(wloc, memory_space=pltpu.MemorySpace.HBM)
    o_ref = jax.empty_ref(jax.ShapeDtypeStruct((NCHIP, 2, HALF, wloc.shape[-1]), BF16),
                          memory_space=pltpu.MemorySpace.HBM)
    dma = pltpu.SemaphoreType.DMA

    nq = GATHER_PIECES
    rows = HALF // nq

    @pl.kernel(mesh=plsc.ScalarSubcoreMesh(axis_name="sc", num_cores=1), name=name,
               scratch_types=(dma(()), dma((2 * nq,)), dma((2 * nq,)), dma((3 * nq,)), dma((3 * nq,)),
                              dma((nq,)), dma((nq,))),
               compiler_params=pltpu.CompilerParams(collective_id=GATHER_ID))
    def launch(lsem, s1, r1, s2, r2, s3, r3):
        x, y, c, _ = _place()
        me = 2 * x + y
        sib = (x, y, 1 - c)
        x_nbr, y_nbr = (1 - x, y, c), (x, 1 - y, c)
        x_chip, y_chip, d_chip = 2 * (1 - x) + y, 2 * x + (1 - y), 2 * (1 - x) + (1 - y)
        _handshake([sib, x_nbr, y_nbr])
        mine = pltpu.make_async_copy(w_ref, o_ref.at[me], lsem)
        mine.start()

        def rc(src, dst, ss, rs, dev):
            return pltpu.make_async_remote_copy(src_ref=src, dst_ref=dst, send_sem=ss, recv_sem=rs,
                                                device_id=dev, device_id_type=MESH)

        def piece(ref, q):
            return ref.at[pl.ds(q * rows, rows)]

        sends = []
        for q in range(nq):
            for j, nbr in enumerate((x_nbr, y_nbr)):
                sends.append(rc(piece(w_ref.at[c], q), piece(o_ref.at[me, c], q),
                                s1.at[j * nq + q], r1.at[j * nq + q], nbr))
                sends[-1].start()
        on_chip = c * x_chip + (1 - c) * y_chip
        other_chip = c * y_chip + (1 - c) * x_chip
        on_to = (c * x + (1 - c) * (1 - x), c * (1 - y) + (1 - c) * y, c)
        for q in range(nq):
            slot = piece(o_ref.at[on_chip, c], q)
            rc(slot, slot, s1.at[(1 - c) * nq + q], r1.at[(1 - c) * nq + q], sib).wait_recv()
            sends.append(rc(slot, slot, s3.at[q], r3.at[q], on_to))
            sends[-1].start()
            sends.append(rc(slot, slot, s2.at[q], r2.at[q], sib))
            sends[-1].start()
        for q in range(nq):
            slot = piece(o_ref.at[other_chip, c], q)
            rc(slot, slot, s1.at[c * nq + q], r1.at[c * nq + q], sib).wait_recv()
            sends.append(rc(slot, slot, s2.at[nq + q], r2.at[nq + q], sib))
            sends[-1].start()
        for q in range(nq):
            slot = piece(o_ref.at[d_chip, c], q)
            rc(slot, slot, s3.at[q], r3.at[q], sib).wait_recv()
            sends.append(rc(slot, slot, s2.at[2 * nq + q], r2.at[2 * nq + q], sib))
            sends[-1].start()
        for j, chip in enumerate([other_chip, on_chip, d_chip]):
            for q in range(nq):
                slot = piece(o_ref.at[chip, 1 - c], q)
                rc(slot, slot, s2.at[j * nq + q], r2.at[j * nq + q], sib).wait_recv()
        for cp in sends:
            cp.wait_send()
        mine.wait()

    launch()
    return o_ref[...]


PAIR_ID = 2
CHIP_ID = 3
HBM = pltpu.MemorySpace.HBM


def _sequencer(name, collective_id, n_sems):
    dma = pltpu.SemaphoreType.DMA
    return pl.kernel(mesh=plsc.ScalarSubcoreMesh(axis_name="sc", num_cores=1), name=name,
                     scratch_types=(dma((n_sems,)), dma((n_sems,))),
                     compiler_params=pltpu.CompilerParams(collective_id=collective_id))


def _handshake(peers):
    barrier = pltpu.get_barrier_semaphore()
    for peer in peers:
        pl.semaphore_signal(barrier, inc=1, device_id=peer, device_id_type=MESH)
    pl.semaphore_wait(barrier, len(peers))


def _pair_exchange(g5, name):
    _, _, hr, cols = g5.shape
    g_ref = jax.new_ref(g5, memory_space=HBM)
    o_ref = jax.empty_ref(jax.ShapeDtypeStruct((NCHIP, hr, cols), g5.dtype), memory_space=HBM)

    @_sequencer(name, PAIR_ID, NCHIP)
    def launch(ss, rs):
        x, y, c, _ = _place()
        sib = (x, y, 1 - c)
        _handshake([sib])
        cps = [pltpu.make_async_remote_copy(src_ref=g_ref.at[j, 1 - c], dst_ref=o_ref.at[j], send_sem=ss.at[j],
                                            recv_sem=rs.at[j], device_id=sib, device_id_type=MESH)
               for j in range(NCHIP)]
        for cp in cps:
            cp.start()
        for cp in cps:
            cp.wait()

    launch()
    return o_ref[...]


RS_TR = 256


def _pair_sum(pos, g5, got, name):
    _, _, hr, cols = g5.shape

    def body(pos_ref, g_ref, r_ref, o_ref):
        o_ref[...] = (g_ref[...].astype(F32) + r_ref[...].astype(F32)).astype(BF16)

    return _pcall(
        body, name=name, scalar_prefetch=1, grid=(NCHIP, hr // RS_TR),
        in_specs=[pl.BlockSpec((None, None, RS_TR, cols), lambda j, i, p: (j, p[0], i, 0)),
                  pl.BlockSpec((None, RS_TR, cols), lambda j, i, p: (j, i, 0))],
        out_specs=pl.BlockSpec((None, RS_TR, cols), lambda j, i, p: (j, i, 0)),
        out_shape=jax.ShapeDtypeStruct((NCHIP, hr, cols), BF16),
        compiler_params=_cp("parallel", "parallel"))(pos, g5, got)


def _chip_exchange(pb, name):
    _, hr, cols = pb.shape
    p_ref = jax.new_ref(pb, memory_space=HBM)
    o_ref = jax.empty_ref(jax.ShapeDtypeStruct((3, hr, cols), BF16), memory_space=HBM)

    @_sequencer(name, CHIP_ID, 3)
    def launch(ss, rs):
        x, y, c, chips = _place()
        _handshake([(px, py, c) for px, py in chips])
        cps = [pltpu.make_async_remote_copy(src_ref=p_ref.at[2 * px + py], dst_ref=o_ref.at[j], send_sem=ss.at[j],
                                            recv_sem=rs.at[j], device_id=(px, py, c), device_id_type=MESH)
               for j, (px, py) in enumerate(chips)]
        for cp in cps:
            cp.start()
        for cp in cps:
            cp.wait()

    launch()
    return o_ref[...]


def _chip_sum(pos, g5, got, peers, name):
    _, _, hr, cols = g5.shape

    def body(pos_ref, g_ref, r_ref, p_ref, o_ref, t_ref):
        acc = g_ref[...].astype(F32) + r_ref[...].astype(F32)
        for j in range(3):
            acc += p_ref[j].astype(F32)
        o_ref[...] = acc
        t_ref[...] = jnp.zeros_like(t_ref)

    return _pcall(
        body, name=name, scalar_prefetch=1, grid=(hr // RS_TR,),
        in_specs=[pl.BlockSpec((None, None, RS_TR, cols), lambda i, p: (p[0], p[1], i, 0)),
                  pl.BlockSpec((None, RS_TR, cols), lambda i, p: (p[0], i, 0)),
                  pl.BlockSpec((3, RS_TR, cols), lambda i, p: (0, i, 0))],
        out_specs=[pl.BlockSpec((None, RS_TR, cols), lambda i, p: (p[1], i, 0)),
                   pl.BlockSpec((8, 128), lambda i, p: (0, 0))],
        out_shape=[jax.ShapeDtypeStruct((2, hr, cols), F32), jax.ShapeDtypeStruct((8, 128), F32)],
        compiler_params=_cp("arbitrary"))(pos, g5, got, peers)


def _pair_share(gsum, name):
    g_ref = jax.new_ref(gsum, memory_space=HBM)

    @_sequencer(name, PAIR_ID, 1)
    def launch(ss, rs):
        x, y, c, _ = _place()
        sib = (x, y, 1 - c)
        _handshake([sib])
        cp = pltpu.make_async_remote_copy(src_ref=g_ref.at[c], dst_ref=g_ref.at[c], send_sem=ss.at[0],
                                          recv_sem=rs.at[0], device_id=sib, device_id_type=MESH)
        cp.start()
        cp.wait_send()
        pltpu.make_async_remote_copy(src_ref=g_ref.at[1 - c], dst_ref=g_ref.at[1 - c], send_sem=ss.at[0],
                                     recv_sem=rs.at[0], device_id=sib, device_id_type=MESH).wait_recv()

    launch()
    return g_ref[...]


class _ReduceScatter:
    def __init__(self, g_arr, gi, pos_c, pos_sc):
        _, rows, cols = g_arr.shape
        self.g5 = g_arr.reshape(NCHIP, 2, rows // 2, cols)
        self.gi, self.pos_c, self.pos_sc = gi, pos_c, pos_sc
        self.got = _pair_exchange(self.g5, f"pair_exchange_{gi}")

    def phase2(self):
        pb = _pair_sum(self.pos_c, self.g5, self.got, f"pair_sum_{self.gi}")
        self.peers = _chip_exchange(pb, f"chip_exchange_{self.gi}")

    def phase3(self):
        gsum, _ = _chip_sum(self.pos_sc, self.g5, self.got, self.peers, f"chip_sum_{self.gi}")
        self.full = _pair_share(gsum, f"pair_share_{self.gi}")

    def result(self):
        _, hr, cols = self.full.shape
        return self.full.reshape(2 * hr, cols)


SMALL_ROWS = 56


ALL_ID = 4


def _all_gather_small(vec, name):
    v_ref = jax.new_ref(vec, memory_space=HBM)
    o_ref = jax.empty_ref(jax.ShapeDtypeStruct((8, SMALL_ROWS, D), F32), memory_space=HBM)

    @_sequencer(name, ALL_ID, 8)
    def launch(ss, rs):
        x, y, c, _ = _place()
        me = 4 * x + 2 * y + c
        flip = lambda v, bit: 1 - v if bit else v
        peers = [(flip(x, m >> 2), flip(y, (m >> 1) & 1), flip(c, m & 1)) for m in range(1, 8)]
        _handshake(peers)
        mine = pltpu.make_async_copy(v_ref, o_ref.at[me], ss.at[7])
        mine.start()
        cps = [pltpu.make_async_remote_copy(src_ref=v_ref, dst_ref=o_ref.at[me], send_sem=ss.at[k],
                                            recv_sem=rs.at[k], device_id=peer, device_id_type=MESH)
               for k, peer in enumerate(peers)]
        for cp in cps:
            cp.start()
        for cp in cps:
            cp.wait()
        mine.wait()

    launch()
    return o_ref[...]


def _sum_devices(parts, name):
    def body(p_ref, o_ref):
        acc = p_ref[0]
        for d in range(1, 8):
            acc += p_ref[d]
        o_ref[...] = acc

    return _pcall(
        body, name=name, grid=(SMALL_ROWS // 8,),
        in_specs=[pl.BlockSpec((8, 8, D), lambda i: (0, i, 0))],
        out_specs=pl.BlockSpec((8, D), lambda i: (i, 0)),
        out_shape=jax.ShapeDtypeStruct((SMALL_ROWS, D), F32),
        compiler_params=_cp("parallel"))(parts)


def _adamw_math(w, g, m, v):
    m2 = ADAM_B1 * m + (1.0 - ADAM_B1) * g
    v2 = ADAM_B2 * v + (1.0 - ADAM_B2) * (g * g)
    m_hat = m2 / (1.0 - ADAM_B1 ** ADAM_STEP)
    v_hat = v2 / (1.0 - ADAM_B2 ** ADAM_STEP)
    delta = -ADAM_LR * (m_hat / (jnp.sqrt(v_hat) + ADAM_EPS) + ADAM_WD * w)
    return delta, m2, v2


def _adamw(w, g, m, v, name, g_block=None):
    R, C = w.shape
    tr = R
    gw_hint = C if g_block is None else g_block[0]
    for cand in (512, 352, 256, 176, 128, 64, 32, 16, 8):
        if R % cand == 0 and cand * max(C, gw_hint) * 4 <= (2 << 20):
            tr = cand
            break
    gw, gi = (C, 0) if g_block is None else g_block

    def body(w_ref, g_ref, m_ref, v_ref, go_ref, d_ref, mo_ref, vo_ref):
        g = g_ref[:, 0:C]
        d, m2, v2 = _adamw_math(w_ref[...], g, m_ref[...], v_ref[...])
        go_ref[...] = g
        d_ref[...] = d
        mo_ref[...] = m2
        vo_ref[...] = v2

    spec = pl.BlockSpec((tr, C), lambda i: (i, 0))
    return _pcall(
        body, name=name, grid=(R // tr,),
        in_specs=[spec, pl.BlockSpec((tr, gw), lambda i: (i, gi)), spec, spec],
        out_specs=[spec] * 4, out_shape=[jax.ShapeDtypeStruct((R, C), F32)] * 4,
        compiler_params=_cp("parallel"))(w, g, m, v)


BIG = ["ffn1_w_gate", "ffn1_w_up", "ffn1_w_down", "w_in", "w_ret_o", "w_conv_o", "w_out",
       "ffn2_w_gate", "ffn2_w_up", "ffn2_w_down"]
SLAB = {"ffn1_w_gate": "g1", "ffn1_w_up": "u1", "ffn1_w_down": "d1", "w_in": "w_in", "w_ret_o": "w_ret_o",
        "w_conv_o": "w_conv_o", "w_out": "w_out", "ffn2_w_gate": "g2", "ffn2_w_up": "u2", "ffn2_w_down": "d2"}
TRANSPOSED = {"ffn1_w_down", "ffn2_w_down", "w_ret_o", "w_conv_o", "w_out"}
MINOR_ROWS = {"ffn1_w_gate", "ffn1_w_up", "ffn2_w_gate", "ffn2_w_up"}
SMALL = ["ln1_g", "ln1_b", "ln2_g", "ln2_b", "ln3_g", "ln3_b", "conv_ln_g", "conv_ln_b", "conv_b",
         "ret_gn_g", "b_in"]
ORDER = ["ffn1_w_gate", "ffn1_w_up", "ffn1_w_down", "ln1_g", "ln1_b", "w_in", "b_in", "ret_gn_g", "conv_k",
         "conv_b", "conv_ln_g", "conv_ln_b", "w_ret_o", "w_conv_o", "w_out", "ln2_g", "ln2_b",
         "ffn2_w_gate", "ffn2_w_up", "ffn2_w_down", "ln3_g", "ln3_b"]


def _slab_width(name):
    return WIDTH[SLAB[name]]


def _pack_group(weights, keys):
    by_key = {SLAB[n]: n for n in BIG}
    parts = []
    for key in keys:
        w = weights[by_key[key]]
        w = w.T if by_key[key] in TRANSPOSED else w
        parts.append(jnp.pad(w, ((0, 0), (0, WIDTH[key] - w.shape[1]))))
    return jnp.concatenate(parts, axis=1).astype(BF16)


def _pack_small(vals, loss, rows):
    flat = jnp.concatenate([vals[n].reshape(-1) for n in SMALL] + [vals["conv_k"].reshape(-1), loss.reshape(-1)])
    return jnp.pad(flat, (0, rows * D - flat.shape[0])).reshape(rows, D)


def _unpack_small(arr, shapes):
    flat = arr.reshape(-1)
    out, pos = {}, 0
    for n in SMALL + ["conv_k"]:
        size = int(np.prod(shapes[n]))
        out[n] = flat[pos:pos + size].reshape(shapes[n])
        pos += size
    return out, flat[pos]


def kernel(x, ffn1_w_gate, ffn1_w_up, ffn1_w_down, ln1_g, ln1_b, w_in, b_in, ret_gn_g, conv_k, conv_b, conv_ln_g, conv_ln_b, w_ret_o, w_conv_o, w_out, ln2_g, ln2_b, ffn2_w_gate, ffn2_w_up, ffn2_w_down, ln3_g, ln3_b, loss_target, m_ffn1_w_gate, m_ffn1_w_up, m_ffn1_w_down, m_ln1_g, m_ln1_b, m_w_in, m_b_in, m_ret_gn_g, m_conv_k, m_conv_b, m_conv_ln_g, m_conv_ln_b, m_w_ret_o, m_w_conv_o, m_w_out, m_ln2_g, m_ln2_b, m_ffn2_w_gate, m_ffn2_w_up, m_ffn2_w_down, m_ln3_g, m_ln3_b, v_ffn1_w_gate, v_ffn1_w_up, v_ffn1_w_down, v_ln1_g, v_ln1_b, v_w_in, v_b_in, v_ret_gn_g, v_conv_k, v_conv_b, v_conv_ln_g, v_conv_ln_b, v_w_ret_o, v_w_conv_o, v_w_out, v_ln2_g, v_ln2_b, v_ffn2_w_gate, v_ffn2_w_up, v_ffn2_w_down, v_ln3_g, v_ln3_b):
    args = dict(locals())
    w = {n: args[n] for n in ORDER}
    m = {n: args["m_" + n] for n in ORDER}
    v = {n: args["v_" + n] for n in ORDER}
    xi, yi, ci = lax.axis_index("x"), lax.axis_index("y"), lax.axis_index("c")
    chip = 2 * xi + yi

    shards = {n: w[n][0] for n in BIG}
    wts = []
    for gi, keys in enumerate(GATHER_GROUPS):
        slab = _pack_group(shards, keys)
        cols = slab.shape[1]
        wts.append(_gather_weights(slab.reshape(2, HALF, cols), f"gather_{gi}").reshape(NCHIP, D, cols))

    sp = {n: w[n] for n in SMALL}
    kpad = jnp.zeros((CONV_W, D), F32)
    kpad = lax.dynamic_update_slice(kpad, w["conv_k"][0, :, 0, :] * jnp.where(ci == 0, 1.0, 0.0), (0, chip * (D // NCHIP)))
    kvec = jnp.pad(kpad.reshape(-1), (0, SMALL_ROWS * D - CONV_W * D)).reshape(SMALL_ROWS, D)
    pos_c = jnp.reshape(ci, (1,)).astype(jnp.int32)
    pos_sc = jnp.stack([chip, ci]).astype(jnp.int32)
    out = {}

    def adam(gi, slab):
        for n in BIG:
            (g_of, off), width = LOC[SLAB[n]], _slab_width(n)
            if g_of != gi:
                continue
            w2 = w[n][0]
            if n in TRANSPOSED:
                res = _adamw(w2, slab[:, off:off + w2.shape[0]].T, m[n][0], v[n][0], "adamw_" + n)
                out[n] = [r[None] for r in res]
            elif n in MINOR_ROWS:
                res = _adamw(w2.T, slab[:, off:off + w2.shape[1]].T, m[n][0].T, v[n][0].T, "adamw_" + n)
                out[n] = [r.T[None] for r in res]
            else:
                res = _adamw(w2, slab, m[n][0], v[n][0], "adamw_" + n, g_block=(width, off // width))
                out[n] = [r[None] for r in res]

    grad_x, small_sum, shapes = _local_step(x[0], loss_target[0], wts, sp, kvec, pos_c, pos_sc, adam)
    small, total = _unpack_small(small_sum, shapes)

    for n in SMALL:
        res = _adamw(w[n], small[n], m[n], v[n], "adamw_" + n)
        out[n] = list(res)
    gk = lax.dynamic_slice(small["conv_k"], (0, chip * (D // NCHIP)), (CONV_W, D // NCHIP))
    res = _adamw(w["conv_k"][0, :, 0, :], gk, m["conv_k"][0, :, 0, :], v["conv_k"][0, :, 0, :], "adamw_conv_k")
    out["conv_k"] = [r[None, :, None, :] for r in res]

    grads = [out[n][0] for n in ORDER]
    deltas = [out[n][1] for n in ORDER]
    new_m = [out[n][2] for n in ORDER]
    new_v = [out[n][3] for n in ORDER]
    return (total, grad_x[None], *grads, *deltas, *new_m, *new_v)
```

```python
import dataclasses
import functools

import numpy as np
import jax
import jax.numpy as jnp
from jax import lax
from jax.experimental import pallas as pl
from jax.experimental.pallas import tpu as pltpu
from jax.experimental.pallas import tpu_sc as plsc

F32 = jnp.float32
BF16 = jnp.bfloat16

D = 1024
FS = 704
FSP = 768
FP = 4 * FSP
H = 8
DK = 128
DV = 256
CH = 128
VW = H * DV
INW = 10240
INS = INW // 4
CONV_W = 31
HALO = 32
EPS = 1e-5
ALPHA = 2.0 ** 0.25
ROPE_BASE = 10000.0
NCHIP = 4

ADAM_LR, ADAM_B1, ADAM_B2, ADAM_EPS, ADAM_WD, ADAM_STEP = 0.001, 0.9, 0.999, 1e-08, 0.01, 10

OFF = {"w_in": 0, "w_ret_o": 2560, "g1": 3072, "u1": 3840, "d1": 4608,
       "g2": 5376, "u2": 6144, "d2": 6912, "w_conv_o": 7680, "w_out": 7936}
WCOLS = 8192
WIDTH = {"w_in": INS, "w_ret_o": VW // NCHIP, "w_conv_o": D // NCHIP, "w_out": D // NCHIP,
         "g1": FSP, "u1": FSP, "d1": FSP, "g2": FSP, "u2": FSP, "d2": FSP}
GROUPS = (("g1", "u1"), ("d1",), ("w_in",), ("w_ret_o", "w_conv_o", "w_out"), ("g2", "u2", "d2"))
GATHER_GROUPS = (("g1", "u1", "d1"), ("w_in",), ("w_ret_o", "w_conv_o", "w_out"), ("g2", "u2", "d2"))


def _locate(groups):
    loc = {}
    for gi, keys in enumerate(groups):
        off = 0
        for k in keys:
            loc[k] = (gi, off)
            off += WIDTH[k]
    return loc


LOC = _locate(GROUPS)
LOC_W = _locate(GATHER_GROUPS)
GCOLS = [sum(WIDTH[k] for k in keys) for keys in GROUPS]
VMEM_LIMIT = 56 << 20


def _cp(*sem, **kw):
    return pltpu.CompilerParams(dimension_semantics=sem, vmem_limit_bytes=VMEM_LIMIT, **kw)


class _ProgramOrder:
    def __init__(self):
        self.active = False
        self.token = None


_ORDER = _ProgramOrder()


def _pcall(body, *, in_specs, scalar_prefetch=0, **kw):
    def call(*args):
        dep = _ORDER.token if _ORDER.active else None
        specs, fn = list(in_specs), body
        if dep is not None:
            n = len(args)

            def fn(*refs):
                return body(*refs[:n], *refs[n + 1:])

            specs.append(pl.BlockSpec(memory_space=pl.ANY))
            args = (*args, dep)
        params = dict(kw)
        if scalar_prefetch:
            params["grid_spec"] = pltpu.PrefetchScalarGridSpec(
                num_scalar_prefetch=scalar_prefetch, grid=params.pop("grid"), in_specs=specs,
                out_specs=params.pop("out_specs"))
        else:
            params["in_specs"] = specs
        out = pl.pallas_call(fn, **params)(*args)
        if _ORDER.active:
            _ORDER.token = jax.tree.leaves(out)[-1]
        return out

    return call


def _resident(shape, col_block):
    lead = (0,) * (len(shape) - 1)
    return pl.BlockSpec(shape, lambda *_: (*lead, col_block), pipeline_mode=pl.Buffered(1))


def _sig(x):
    return 1.0 / (1.0 + jnp.exp(-x))


def _dot(a, b):
    return jnp.dot(a, b, preferred_element_type=F32)


def _dot_nt(a, b):
    return lax.dot_general(a, b, (((1,), (1,)), ((), ())), preferred_element_type=F32)


def _ln_fwd(z, g, b):
    mu = jnp.mean(z, axis=-1, keepdims=True)
    xc = z - mu
    var = jnp.mean(xc * xc, axis=-1, keepdims=True)
    rstd = lax.rsqrt(var + EPS)
    xh = xc * rstd
    return xh * g + b, xh, rstd


def _ln_bwd(dy, xh, rstd, g):
    dxh = dy * g
    m1 = jnp.mean(dxh, axis=-1, keepdims=True)
    m2 = jnp.mean(dxh * xh, axis=-1, keepdims=True)
    return rstd * (dxh - m1 - xh * m2)


def _colsum(x):
    return jnp.sum(x, axis=0, keepdims=True)


def _zero_first(first, *refs):
    @pl.when(first)
    def _():
        for ref in refs:
            ref[...] = jnp.zeros_like(ref)


def _rope_tables(T):
    half = DK // 2
    freqs = ROPE_BASE ** (-np.arange(half, dtype=np.float32) / half)
    ang = (np.arange(T, dtype=np.float32)[:, None] * freqs[None, :]).astype(np.float32)
    cos, sin = np.cos(ang), np.sin(ang)
    return (jnp.asarray(np.concatenate([cos, cos], 1), F32),
            jnp.asarray(np.concatenate([-sin, sin], 1), F32))


def _decay_tables():
    h = np.arange(H, dtype=np.float64)
    log_g = np.log(1.0 - np.exp2(-5.0 - h))
    idx = np.arange(CH, dtype=np.float64)
    diff = idx[:, None] - idx[None, :]
    dm = np.where(diff[None] >= 0, np.exp(np.maximum(diff, 0.0)[None] * log_g[:, None, None]), 0.0)
    xi = np.exp((idx[None, :] + 1.0) * log_g[:, None])
    zeta = np.exp((CH - 1.0 - idx)[None, :] * log_g[:, None])
    cd = np.exp(CH * log_g)
    xi_t = np.broadcast_to(xi[:, :, None], (H, CH, DV))
    zeta_t = np.broadcast_to(zeta[:, :, None], (H, CH, DK))
    return (jnp.asarray(dm, F32), jnp.asarray(xi_t, F32), jnp.asarray(zeta_t, F32),
            [float(v) for v in cd])


def _cast_t(x):
    T = x.shape[0]
    tm = min(T, 512)

    def body(x_ref, xb_ref, xt_ref):
        v = x_ref[...]
        xb_ref[...] = v.astype(BF16)
        xt_ref[...] = v.T.astype(BF16)

    return _pcall(
        body, name="cast_t", grid=(T // tm,),
        in_specs=[pl.BlockSpec((tm, D), lambda i: (i, 0))],
        out_specs=[pl.BlockSpec((tm, D), lambda i: (i, 0)), pl.BlockSpec((D, tm), lambda i: (0, i))],
        out_shape=[jax.ShapeDtypeStruct((T, D), BF16), jax.ShapeDtypeStruct((D, T), BF16)],
        compiler_params=_cp("parallel"))(x)


def _ffn_up(xb, wall, og, ou, name):
    T = xb.shape[0]
    tm = min(T, 512)
    assert ou == og + FSP

    def body(x_ref, w_ref, a_ref, b_ref, h_ref):
        x = x_ref[...]
        for s in range(NCHIP):
            cols = slice(s * FSP, (s + 1) * FSP)
            a = _dot(x, w_ref[s, :, 0:FSP])
            b = _dot(x, w_ref[s, :, FSP:2 * FSP])
            a_ref[:, cols] = a.astype(BF16)
            b_ref[:, cols] = b.astype(BF16)
            h_ref[:, cols] = (a * _sig(a) * b).astype(BF16)

    ospec = pl.BlockSpec((tm, FP), lambda i: (i, 0))
    return _pcall(
        body, name=name, grid=(T // tm,),
        in_specs=[pl.BlockSpec((tm, D), lambda i: (i, 0)), _resident((NCHIP, D, 2 * FSP), og // (2 * FSP))],
        out_specs=[ospec] * 3, out_shape=[jax.ShapeDtypeStruct((T, FP), BF16)] * 3,
        compiler_params=_cp("parallel"))(xb, wall)


def _proj_ln(hb, wall, off, res, g, b, coef, name, want_b=True):
    T, K = hb.shape
    ks = K // NCHIP
    tm = min(T, 512)
    sub = min(tm, 256)

    def body(h_ref, w_ref, r_ref, g_ref, b_ref, z_ref, *rest):
        for r0 in range(0, tm, sub):
            r = slice(r0, r0 + sub)
            acc = _dot_nt(h_ref[r, 0:ks], w_ref[0])
            for s in range(1, NCHIP):
                acc += _dot_nt(h_ref[r, s * ks:(s + 1) * ks], w_ref[s])
            z = ALPHA * r_ref[r, :] + coef * acc
            z_ref[r, :] = z
            if want_b:
                y, _, _ = _ln_fwd(z, g_ref[...], b_ref[...])
                y_ref, yb_ref, yt_ref = rest
                y_ref[r, :] = y
                yb_ref[r, :] = y.astype(BF16)
                yt_ref[:, r] = y.T.astype(BF16)

    row = pl.BlockSpec((tm, D), lambda i: (i, 0))
    vec = pl.BlockSpec((1, D), lambda i: (0, 0))
    out_specs = [row]
    out_shape = [jax.ShapeDtypeStruct((T, D), F32)]
    if want_b:
        out_specs += [row, row, pl.BlockSpec((D, tm), lambda i: (0, i))]
        out_shape += [jax.ShapeDtypeStruct((T, D), F32), jax.ShapeDtypeStruct((T, D), BF16),
                      jax.ShapeDtypeStruct((D, T), BF16)]
    return _pcall(
        body, name=name, grid=(T // tm,),
        in_specs=[pl.BlockSpec((tm, K), lambda i: (i, 0)),
                  _resident((NCHIP, D, ks), off // ks), row, vec, vec],
        out_specs=out_specs, out_shape=out_shape,
        compiler_params=_cp("parallel"))(hb, wall, res, g, b)


def _inproj(xb, wall, off, b_in, cos_t, sin_t):
    T = xb.shape[0]
    tm, tn = min(T, 512), 512
    assert off == 0

    def body(x_ref, w_ref, bias_ref, cos_ref, sin_ref, o_ref):
        x = x_ref[...]
        c = cos_ref[...]
        s = sin_ref[...]
        for n0 in range(0, INW, tn):
            chip, c0 = divmod(n0, INS)
            acc = _dot(x, w_ref[chip, :, c0:c0 + tn]) + bias_ref[:, n0:n0 + tn]
            if n0 >= 2 * D:
                o_ref[:, n0:n0 + tn] = acc.astype(BF16)
                continue
            scale = DK ** -0.5 if n0 < D else 1.0
            for hh in range(tn // DK):
                xh = acc[:, hh * DK:(hh + 1) * DK]
                o = (xh * c + pltpu.roll(xh, DK // 2, 1) * s) * scale
                o_ref[:, n0 + hh * DK:n0 + (hh + 1) * DK] = o.astype(BF16)

    return _pcall(
        body, name="inproj", grid=(T // tm,),
        in_specs=[pl.BlockSpec((tm, D), lambda i: (i, 0)),
                  _resident((NCHIP, D, INS), 0),
                  pl.BlockSpec((1, INW), lambda i: (0, 0)),
                  pl.BlockSpec((tm, DK), lambda i: (i, 0)),
                  pl.BlockSpec((tm, DK), lambda i: (i, 0))],
        out_specs=pl.BlockSpec((tm, INW), lambda i: (i, 0)),
        out_shape=jax.ShapeDtypeStruct((T, INW), BF16),
        compiler_params=_cp("parallel"))(xb, wall, b_in, cos_t, sin_t)


RET_CPS = 2


def _retention_fwd(proj, gn_g, dm_t, xi_t, zeta_t, cds):
    T = proj.shape[0]
    n = T // CH
    tr = RET_CPS * CH

    def body(q_ref, k_ref, v_ref, g_ref, gn_ref, dm_ref, xi_ref, zt_ref, r_ref, ri_ref, st_ref, state):
        @pl.when(pl.program_id(0) == 0)
        def _():
            state[...] = jnp.zeros_like(state)

        for h in range(H):
            rows = slice(h * DK, (h + 1) * DK)
            cols = slice(h * DV, (h + 1) * DV)
            s_prev = state[rows, :]
            for j in range(RET_CPS):
                t = slice(j * CH, (j + 1) * CH)
                q = q_ref[t, h * DK:(h + 1) * DK]
                k = k_ref[t, h * DK:(h + 1) * DK]
                v = v_ref[t, cols]
                s_b = s_prev.astype(BF16)
                st_ref[j, rows, :] = s_b
                sc = _dot_nt(q, k) * dm_ref[h]
                r = _dot(sc.astype(BF16), v) + _dot(q, s_b) * xi_ref[h]
                kz = k.astype(F32) * zt_ref[h]
                s_prev = cds[h] * s_prev + _dot(kz.T.astype(BF16), v)
                r_ref[t, cols] = r
                mu = jnp.mean(r, axis=-1, keepdims=True)
                xc = r - mu
                var = jnp.mean(xc * xc, axis=-1, keepdims=True)
                y = xc * lax.rsqrt(var + EPS) * gn_ref[:, cols]
                g = g_ref[t, cols].astype(F32)
                ri_ref[t, cols] = (g * _sig(g) * y).astype(BF16)
            state[rows, :] = s_prev

    full3 = lambda shp: pl.BlockSpec(shp, lambda c: (0, 0, 0))
    return _pcall(
        body, name="retention_fwd", grid=(n // RET_CPS,),
        in_specs=[pl.BlockSpec((tr, D), lambda c: (c, 0)),
                  pl.BlockSpec((tr, D), lambda c: (c, 1)),
                  pl.BlockSpec((tr, VW), lambda c: (c, 1)),
                  pl.BlockSpec((tr, VW), lambda c: (c, 2)),
                  pl.BlockSpec((1, VW), lambda c: (0, 0)),
                  full3((H, CH, CH)), full3((H, CH, DV)), full3((H, CH, DK))],
        out_specs=[pl.BlockSpec((tr, VW), lambda c: (c, 0)), pl.BlockSpec((tr, VW), lambda c: (c, 0)),
                   pl.BlockSpec((RET_CPS, H * DK, DV), lambda c: (c, 0, 0))],
        out_shape=[jax.ShapeDtypeStruct((T, VW), F32), jax.ShapeDtypeStruct((T, VW), BF16),
                   jax.ShapeDtypeStruct((n, H * DK, DV), BF16)],
        scratch_shapes=[pltpu.VMEM((H * DK, DV), F32)],
        compiler_params=_cp("arbitrary"))(proj, proj, proj, proj, gn_g, dm_t, xi_t, zeta_t)


CONV_TT = 256
CONV_SB = 64
CONV_CB = 256


SUB = 8
CONV_PAD = 8


def _glu(a_ref, b_ref, rows=slice(None)):
    a = a_ref[rows, :].astype(F32)
    sb = _sig(b_ref[rows, :].astype(F32))
    return a, sb


def _shift_copies(win, sh, rows):
    win[rows:rows + CONV_PAD, :] = jnp.zeros((CONV_PAD, D), F32)
    for b in range(1, SUB):
        sh[b - 1, :, :] = win[b:b + rows, :]


def _tap(win, sh, start, size, cs):
    a, b = divmod(start, SUB)
    src = win if b == 0 else sh.at[b - 1]
    return src[SUB * a:SUB * a + size, cs]


def _conv_fwd(proj, conv_k, conv_b, ln_g, ln_b):
    T = proj.shape[0]
    tt = min(T, CONV_TT)
    ca, cb = 6 * D // D, 7 * D // D

    def body(a_ref, b_ref, pa_ref, pb_ref, k_ref, cb_ref, g_ref, bb_ref, u1_ref, u3_ref, win, sh):
        i = pl.program_id(0)
        a, sb = _glu(a_ref, b_ref)
        win[HALO:tt + HALO, :] = a * sb
        pa, psb = _glu(pa_ref, pb_ref, slice(tt - HALO, tt))
        win[0:HALO, :] = jnp.where(i > 0, pa * psb, 0.0)
        _shift_copies(win, sh, tt + HALO)
        for c0 in range(0, D, CONV_CB):
            cs = slice(c0, c0 + CONV_CB)
            for r0 in range(0, tt, CONV_SB):
                acc = jnp.zeros((CONV_SB, CONV_CB), F32)
                for w in range(CONV_W):
                    st = r0 + HALO - (CONV_W - 1) + w
                    acc += _tap(win, sh, st, CONV_SB, cs) * k_ref[w:w + 1, cs]
                u1_ref[r0:r0 + CONV_SB, cs] = acc + cb_ref[:, cs]
        u2, _, _ = _ln_fwd(u1_ref[...], g_ref[...], bb_ref[...])
        u3_ref[...] = (u2 * _sig(u2)).astype(BF16)

    vec = pl.BlockSpec((1, D), lambda i: (0, 0))
    row = pl.BlockSpec((tt, D), lambda i: (i, 0))
    return _pcall(
        body, name="conv_fwd", grid=(T // tt,),
        in_specs=[pl.BlockSpec((tt, D), lambda i: (i, ca)), pl.BlockSpec((tt, D), lambda i: (i, cb)),
                  pl.BlockSpec((tt, D), lambda i: (jnp.maximum(i - 1, 0), ca)),
                  pl.BlockSpec((tt, D), lambda i: (jnp.maximum(i - 1, 0), cb)),
                  pl.BlockSpec((CONV_W, D), lambda i: (0, 0)), vec, vec, vec],
        out_specs=[row, row],
        out_shape=[jax.ShapeDtypeStruct((T, D), F32), jax.ShapeDtypeStruct((T, D), BF16)],
        scratch_shapes=[pltpu.VMEM((tt + HALO + CONV_PAD, D), F32), pltpu.VMEM((SUB - 1, tt + HALO, D), F32)],
        compiler_params=_cp("parallel"))(proj, proj, proj, proj, conv_k, conv_b, ln_g, ln_b)


def _merge(ret_in, u3, proj, wall, off_r, off_c):
    T = ret_in.shape[0]
    tm = min(T, 512)
    kr, kc = VW // NCHIP, D // NCHIP

    def body(ri_ref, u3_ref, gr_ref, gc_ref, wr_ref, wc_ref, ro_ref, co_ref, m_ref):
        ro = _dot_nt(ri_ref[:, 0:kr], wr_ref[0])
        co = _dot_nt(u3_ref[:, 0:kc], wc_ref[0])
        for s in range(1, NCHIP):
            ro += _dot_nt(ri_ref[:, s * kr:(s + 1) * kr], wr_ref[s])
            co += _dot_nt(u3_ref[:, s * kc:(s + 1) * kc], wc_ref[s])
        ro_ref[...] = ro.astype(BF16)
        co_ref[...] = co.astype(BF16)
        m = _sig(gr_ref[...].astype(F32)) * ro + _sig(gc_ref[...].astype(F32)) * co
        m_ref[...] = m.astype(BF16)

    row = pl.BlockSpec((tm, D), lambda i: (i, 0))
    return _pcall(
        body, name="merge", grid=(T // tm,),
        in_specs=[pl.BlockSpec((tm, VW), lambda i: (i, 0)), row,
                  pl.BlockSpec((tm, D), lambda i: (i, 8)), pl.BlockSpec((tm, D), lambda i: (i, 9)),
                  pl.BlockSpec((NCHIP, D, kr), lambda i: (0, 0, off_r // kr)),
                  pl.BlockSpec((NCHIP, D, kc), lambda i: (0, 0, off_c // kc))],
        out_specs=[row] * 3, out_shape=[jax.ShapeDtypeStruct((T, D), BF16)] * 3,
        compiler_params=_cp("parallel"))(ret_in, u3, proj, proj, wall, wall)


def _loss_ln_bwd(z, g, b, target, coef):
    T = z.shape[0]
    tm = min(T, 256)
    nt = T // tm

    def body(z_ref, g_ref, b_ref, t_ref, loss_ref, dzb_ref, dzt_ref, dz_ref, dg_ref, db_ref, lacc):
        i = pl.program_id(0)
        _zero_first(i == 0, lacc, dg_ref, db_ref)
        gam = g_ref[...]
        y, xh, rstd = _ln_fwd(z_ref[...], gam, b_ref[...])
        e = y - t_ref[...]
        lacc[...] += _colsum(e * e)
        dy = e * (1.0 / D)
        dz = _ln_bwd(dy, xh, rstd, gam)
        dz_ref[...] = dz
        dzc = coef * dz
        dzb_ref[...] = dzc.astype(BF16)
        dzt_ref[...] = dzc.T.astype(BF16)
        dg_ref[...] += _colsum(dy * xh)
        db_ref[...] += _colsum(dy)

        @pl.when(i == nt - 1)
        def _():
            loss_ref[...] = (0.5 / D) * jnp.sum(lacc[...], axis=1, keepdims=True)

    row = pl.BlockSpec((tm, D), lambda i: (i, 0))
    vec = pl.BlockSpec((1, D), lambda i: (0, 0))
    return _pcall(
        body, name="loss_ln_bwd", grid=(nt,),
        in_specs=[row, vec, vec, row],
        out_specs=[pl.BlockSpec((1, 1), lambda i: (0, 0)), row, pl.BlockSpec((D, tm), lambda i: (0, i)),
                   row, vec, vec],
        out_shape=[jax.ShapeDtypeStruct((1, 1), F32), jax.ShapeDtypeStruct((T, D), BF16),
                   jax.ShapeDtypeStruct((D, T), BF16), jax.ShapeDtypeStruct((T, D), F32),
                   jax.ShapeDtypeStruct((1, D), F32), jax.ShapeDtypeStruct((1, D), F32)],
        scratch_shapes=[pltpu.VMEM((1, D), F32)],
        compiler_params=_cp("arbitrary"))(z, g, b, target)


def _ffn_bwd_h(dfb, wall, od, a, b, name):
    T = dfb.shape[0]
    tm = min(T, 512)

    def body(d_ref, w_ref, a_ref, b_ref, da_ref, db_ref):
        d = d_ref[...]
        for s in range(NCHIP):
            cols = slice(s * FSP, (s + 1) * FSP)
            dh = _dot(d, w_ref[s])
            a = a_ref[:, cols].astype(F32)
            sg = _sig(a)
            da_ref[:, cols] = (dh * b_ref[:, cols].astype(F32) * (sg * (1.0 + a * (1.0 - sg)))).astype(BF16)
            db_ref[:, cols] = (dh * a * sg).astype(BF16)

    ospec = pl.BlockSpec((tm, FP), lambda i: (i, 0))
    return _pcall(
        body, name=name, grid=(T // tm,),
        in_specs=[pl.BlockSpec((tm, D), lambda i: (i, 0)), _resident((NCHIP, D, FSP), od // FSP), ospec, ospec],
        out_specs=[ospec] * 2, out_shape=[jax.ShapeDtypeStruct((T, FP), BF16)] * 2,
        compiler_params=_cp("parallel"))(dfb, wall, a, b)


DX_SUB = 256


def _dx_partial(lhs, wall, chips, name):
    T, K = lhs.shape
    ks = K // NCHIP
    nc = len(chips)
    assert list(chips) == list(range(chips[0], chips[0] + nc)) and chips[0] % nc == 0
    tm = min(T, 512)

    def body(l_ref, w_ref, o_ref):
        for r0 in range(0, tm, DX_SUB):
            r = slice(r0, r0 + DX_SUB)
            acc = _dot_nt(l_ref[r, 0:ks], w_ref[0])
            for s in range(1, nc):
                acc += _dot_nt(l_ref[r, s * ks:(s + 1) * ks], w_ref[s])
            o_ref[r, :] = acc

    blk = chips[0] // nc
    return _pcall(
        body, name=name, grid=(T // tm,),
        in_specs=[pl.BlockSpec((tm, nc * ks), lambda i: (i, blk)),
                  pl.BlockSpec((nc, D, ks), lambda i: (blk, 0, 0), pipeline_mode=pl.Buffered(1))],
        out_specs=pl.BlockSpec((tm, D), lambda i: (i, 0)),
        out_shape=jax.ShapeDtypeStruct((T, D), F32),
        compiler_params=_cp("parallel"))(lhs, wall)


def _dx_bwd(lhs, offs, wall, dz_next, name, ln=None, chips=tuple(range(NCHIP)), partial=None):
    T, K = lhs[0].shape
    ks = K // NCHIP
    nl = len(lhs)
    nc = len(chips)
    assert list(offs) == [l * ks for l in range(nl)]
    assert list(chips) == list(range(chips[0], chips[0] + nc)) and chips[0] % nc == 0
    blk = chips[0] // nc
    tm = min(T, 512)

    def body(*refs):
        l_refs = refs[:nl]
        w_ref = refs[nl]
        dzn_ref = refs[nl + 1]
        pos = nl + 2
        if partial is not None:
            p_ref = refs[pos]
            pos += 1
        if ln is not None:
            z_ref, g_ref = refs[pos:pos + 2]
            pos += 2
        outs = refs[pos:]
        sums = list(outs[3:]) if ln is not None else list(outs[1:])

        _zero_first(pl.program_id(0) == 0, *sums)

        for r0 in range(0, tm, DX_SUB):
            r = slice(r0, r0 + DX_SUB)
            acc = None if partial is None else p_ref[r, :]
            for s in range(nc):
                rows = slice(s * ks, (s + 1) * ks)
                for l in range(nl):
                    part = _dot_nt(l_refs[l][r, rows], w_ref[s, :, l * ks:(l + 1) * ks])
                    acc = part if acc is None else acc + part
            dy = acc + ALPHA * dzn_ref[r, :]
            if ln is None:
                outs[0][r, :] = dy
            else:
                gam = g_ref[...]
                _, xh, rstd = _ln_fwd(z_ref[r, :], gam, 0.0)
                dz = _ln_bwd(dy, xh, rstd, gam)
                dzc = ln[2] * dz
                outs[0][r, :] = dzc.astype(BF16)
                outs[1][:, r] = dzc.T.astype(BF16)
                outs[2][r, :] = dz
                outs[3][...] += _colsum(dy * xh)
                outs[4][...] += _colsum(dy)

    row = pl.BlockSpec((tm, D), lambda i: (i, 0))
    vec = pl.BlockSpec((1, D), lambda i: (0, 0))
    in_specs = [pl.BlockSpec((tm, nc * ks), lambda i: (i, blk))] * nl
    in_specs += [pl.BlockSpec((nc, D, nl * ks), lambda i: (blk, 0, 0), pipeline_mode=pl.Buffered(1)), row]
    args = list(lhs) + [wall, dz_next]
    if partial is not None:
        in_specs.append(row)
        args.append(partial)
    if ln is None:
        out_specs = [row]
        out_shape = [jax.ShapeDtypeStruct((T, D), F32)]
    else:
        in_specs += [row, vec]
        args += [ln[0], ln[1]]
        out_specs = [row, pl.BlockSpec((D, tm), lambda i: (0, i)), row, vec, vec]
        out_shape = [jax.ShapeDtypeStruct((T, D), BF16), jax.ShapeDtypeStruct((D, T), BF16),
                     jax.ShapeDtypeStruct((T, D), F32), jax.ShapeDtypeStruct((1, D), F32),
                     jax.ShapeDtypeStruct((1, D), F32)]
    return _pcall(
        body, name=name, grid=(T // tm,), in_specs=in_specs, out_specs=out_specs, out_shape=out_shape,
        compiler_params=_cp("arbitrary"))(*args)


def _wgrad(lhs_t, rhs, key, name, g_all=None, colsum=False):
    T, N = rhs.shape
    tn = next(c for c in (768, 512, 256) if (N // NCHIP) % c == 0 and LOC[key][1] % c == 0)
    nps = N // NCHIP // tn
    off = LOC[key][1]
    cols = GCOLS[LOC[key][0]]

    def body(*refs):
        l_ref, r_ref = refs[0], refs[1]
        o_ref, t_ref = refs[-2 - colsum], refs[-1]
        o_ref[...] = _dot(l_ref[...], r_ref[...]).astype(BF16)
        if colsum:
            refs[-2][...] = _colsum(r_ref[...].astype(F32))
        t_ref[...] = jnp.zeros_like(t_ref)

    in_specs = [_resident((D, T), 0), pl.BlockSpec((T, tn), lambda j: (0, j))]
    args = [lhs_t, rhs]
    aliases = {}
    if g_all is not None:
        in_specs.append(pl.BlockSpec(memory_space=pl.ANY))
        args.append(g_all)
        aliases = {2: 0}
    out_specs = [pl.BlockSpec((None, D, tn), lambda j: (j // nps, 0, off // tn + j % nps))]
    out_shape = [jax.ShapeDtypeStruct((NCHIP, D, cols), BF16)]
    if colsum:
        out_specs.append(pl.BlockSpec((1, tn), lambda j: (0, j)))
        out_shape.append(jax.ShapeDtypeStruct((1, N), F32))
    out_specs.append(pl.BlockSpec((8, 128), lambda j: (0, 0)))
    out_shape.append(jax.ShapeDtypeStruct((8, 128), F32))
    res = _pcall(
        body, name=name, grid=(N // tn,), in_specs=in_specs, out_specs=out_specs, out_shape=out_shape,
        input_output_aliases=aliases,
        compiler_params=_cp("arbitrary"))(*args)
    return (res[0], res[1]) if colsum else res[0]


def _merge_bwd(dmb, wall, off, proj, ro, co):
    T = dmb.shape[0]
    tm = min(T, 512)
    ks = D // NCHIP

    def body(d_ref, w_ref, gr_ref, gc_ref, ro_ref, co_ref, dro_ref, drot_ref, dco_ref, dcot_ref, dp_ref):
        d = d_ref[...]
        dmg = jnp.concatenate([_dot(d, w_ref[s]) for s in range(NCHIP)], axis=1)
        sr = _sig(gr_ref[...].astype(F32))
        sc = _sig(gc_ref[...].astype(F32))
        dro = dmg * sr
        dco = dmg * sc
        dro_ref[...] = dro.astype(BF16)
        drot_ref[...] = dro.T.astype(BF16)
        dco_ref[...] = dco.astype(BF16)
        dcot_ref[...] = dco.T.astype(BF16)
        dp_ref[:, 0:D] = (dmg * ro_ref[...].astype(F32) * sr * (1.0 - sr)).astype(BF16)
        dp_ref[:, D:2 * D] = (dmg * co_ref[...].astype(F32) * sc * (1.0 - sc)).astype(BF16)

    row = pl.BlockSpec((tm, D), lambda i: (i, 0))
    col = pl.BlockSpec((D, tm), lambda i: (0, i))
    return _pcall(
        body, name="merge_bwd", grid=(T // tm,),
        in_specs=[row, pl.BlockSpec((NCHIP, D, ks), lambda i: (0, 0, off // ks)),
                  pl.BlockSpec((tm, D), lambda i: (i, 8)), pl.BlockSpec((tm, D), lambda i: (i, 9)), row, row],
        out_specs=[row, col, row, col, pl.BlockSpec((tm, 2 * D), lambda i: (i, 4))],
        out_shape=[jax.ShapeDtypeStruct((T, D), BF16), jax.ShapeDtypeStruct((D, T), BF16),
                   jax.ShapeDtypeStruct((T, D), BF16), jax.ShapeDtypeStruct((D, T), BF16),
                   jax.ShapeDtypeStruct((T, INW), BF16)],
        compiler_params=_cp("parallel"))(dmb, wall, proj, proj, ro, co)


def _reto_bwd(dro, wall, off, r, proj, gn_g, dproj):
    T = dro.shape[0]
    tm = min(T, 512)
    hps = H // NCHIP

    def body(d_ref, w_ref, r_ref, g_ref, gn_ref, _, dr_ref, dgn_ref, dp_ref):
        _zero_first(pl.program_id(1) == 0, dgn_ref)
        dri = _dot(d_ref[...], w_ref[...])
        rr = r_ref[...]
        mu = jnp.mean(rr, axis=-1, keepdims=True)
        xc = rr - mu
        var = jnp.mean(xc * xc, axis=-1, keepdims=True)
        rstd = lax.rsqrt(var + EPS)
        rn = xc * rstd
        gn = gn_ref[...]
        g = g_ref[...].astype(F32)
        sg = _sig(g)
        dy = dri * (g * sg)
        dp_ref[...] = (dri * (rn * gn) * (sg * (1.0 + g * (1.0 - sg)))).astype(BF16)
        dgn_ref[...] += _colsum(dy * rn)
        dr_ref[...] = _ln_bwd(dy, rn, rstd, gn).astype(BF16)

    return _pcall(
        body, name="reto_bwd", grid=(H, T // tm),
        in_specs=[pl.BlockSpec((tm, D), lambda j, i: (i, 0)),
                  pl.BlockSpec((None, D, DV), lambda j, i: (j // hps, 0, off // DV + j % hps)),
                  pl.BlockSpec((tm, DV), lambda j, i: (i, j)),
                  pl.BlockSpec((tm, DV), lambda j, i: (i, 2 * VW // DV + j)),
                  pl.BlockSpec((1, DV), lambda j, i: (0, j)),
                  pl.BlockSpec(memory_space=pl.ANY)],
        out_specs=[pl.BlockSpec((tm, DV), lambda j, i: (i, j)), pl.BlockSpec((1, DV), lambda j, i: (0, j)),
                   pl.BlockSpec((tm, DV), lambda j, i: (i, 2 * VW // DV + j))],
        out_shape=[jax.ShapeDtypeStruct((T, VW), BF16), jax.ShapeDtypeStruct((1, VW), F32),
                   jax.ShapeDtypeStruct((T, INW), BF16)],
        input_output_aliases={5: 2},
        compiler_params=_cp("arbitrary", "arbitrary"))(dro, wall, r, proj, gn_g, dproj)


def _retention_bwd(proj, dr, states, cos_t, sin_t, dm_t, xi_t, zeta_t, cds, dproj):
    T = proj.shape[0]
    n = T // CH // RET_CPS
    tr = RET_CPS * CH
    scale = DK ** -0.5

    def body(q_ref, k_ref, v_ref, dr_ref, st_ref, cos_ref, sin_ref, dm_ref, xi_ref, zt_ref, _, dp_ref, ds):
        @pl.when(pl.program_id(0) == 0)
        def _():
            ds[...] = jnp.zeros_like(ds)

        def unrope(d, t):
            return d * cos_ref[t, :] + pltpu.roll(d * sin_ref[t, :], DK // 2, 1)

        for h in range(H):
            rows = slice(h * DK, (h + 1) * DK)
            dm = dm_ref[h]
            zt = zt_ref[h]
            ds_prev = ds[rows, :]
            for j in reversed(range(RET_CPS)):
                t = slice(j * CH, (j + 1) * CH)
                q = q_ref[t, h * DK:(h + 1) * DK]
                k = k_ref[t, h * DK:(h + 1) * DK]
                v = v_ref[t, h * DV:(h + 1) * DV]
                d_r = dr_ref[t, h * DV:(h + 1) * DV]
                s_b = st_ref[j, rows, :]
                sc = _dot_nt(q, k) * dm
                dsc = _dot_nt(d_r, v) * dm
                drx = (d_r.astype(F32) * xi_ref[h]).astype(BF16)
                ds_b = ds_prev.astype(BF16)
                kz = (k.astype(F32) * zt).astype(BF16)
                dq = _dot(dsc.astype(BF16), k) + _dot_nt(drx, s_b)
                dk = _dot(dsc.T.astype(BF16), q) + _dot_nt(v, ds_b) * zt
                dv = _dot(sc.T.astype(BF16), d_r) + _dot(kz, ds_b)
                ds_prev = cds[h] * ds_prev + _dot(q.astype(F32).T.astype(BF16), drx)
                dp_ref[t, h * DK:(h + 1) * DK] = unrope(dq * scale, t).astype(BF16)
                dp_ref[t, D + h * DK:D + (h + 1) * DK] = unrope(dk, t).astype(BF16)
                dp_ref[t, 2 * D + h * DV:2 * D + (h + 1) * DV] = dv.astype(BF16)
            ds[rows, :] = ds_prev

    rv = lambda c: n - 1 - c
    full3 = lambda shp: pl.BlockSpec(shp, lambda c: (0, 0, 0))
    return _pcall(
        body, name="retention_bwd", grid=(n,),
        in_specs=[pl.BlockSpec((tr, D), lambda c: (rv(c), 0)),
                  pl.BlockSpec((tr, D), lambda c: (rv(c), 1)),
                  pl.BlockSpec((tr, VW), lambda c: (rv(c), 1)),
                  pl.BlockSpec((tr, VW), lambda c: (rv(c), 0)),
                  pl.BlockSpec((RET_CPS, H * DK, DV), lambda c: (rv(c), 0, 0)),
                  pl.BlockSpec((tr, DK), lambda c: (rv(c), 0)),
                  pl.BlockSpec((tr, DK), lambda c: (rv(c), 0)),
                  full3((H, CH, CH)), full3((H, CH, DV)), full3((H, CH, DK)),
                  pl.BlockSpec(memory_space=pl.ANY)],
        out_specs=pl.BlockSpec((tr, 2 * D + VW), lambda c: (rv(c), 0)),
        out_shape=jax.ShapeDtypeStruct((T, INW), BF16),
        input_output_aliases={10: 0},
        scratch_shapes=[pltpu.VMEM((H * DK, DV), F32)],
        compiler_params=_cp("arbitrary"))(proj, proj, proj, dr, states, cos_t, sin_t, dm_t, xi_t, zeta_t, dproj)


def _convo_bwd(dco, wall, off, u1, ln_g, ln_b):
    T = dco.shape[0]
    tm = min(T, 512)
    ks = D // NCHIP

    def body(d_ref, w_ref, u1_ref, g_ref, b_ref, du1_ref, dg_ref, db_ref, dcb_ref):
        _zero_first(pl.program_id(0) == 0, dg_ref, db_ref, dcb_ref)
        d = d_ref[...]
        du3 = jnp.concatenate([_dot(d, w_ref[s]) for s in range(NCHIP)], axis=1)
        gam = g_ref[...]
        u2, xh, rstd = _ln_fwd(u1_ref[...], gam, b_ref[...])
        sg = _sig(u2)
        du2 = du3 * (sg * (1.0 + u2 * (1.0 - sg)))
        du1 = _ln_bwd(du2, xh, rstd, gam)
        du1_ref[...] = du1
        dg_ref[...] += _colsum(du2 * xh)
        db_ref[...] += _colsum(du2)
        dcb_ref[...] += _colsum(du1)

    row = pl.BlockSpec((tm, D), lambda i: (i, 0))
    vec = pl.BlockSpec((1, D), lambda i: (0, 0))
    return _pcall(
        body, name="convo_bwd", grid=(T // tm,),
        in_specs=[row, pl.BlockSpec((NCHIP, D, ks), lambda i: (0, 0, off // ks)), row, vec, vec],
        out_specs=[row, vec, vec, vec],
        out_shape=[jax.ShapeDtypeStruct((T, D), F32)] + [jax.ShapeDtypeStruct((1, D), F32)] * 3,
        compiler_params=_cp("arbitrary"))(dco, wall, u1, ln_g, ln_b)


def _conv_bwd(du1, proj, conv_k, dproj):
    T = du1.shape[0]
    tt = min(T, CONV_TT)
    nt = T // tt
    ca, cb = 6, 7

    def body(d_ref, dn_ref, a_ref, b_ref, pa_ref, pb_ref, k_ref, _, dp_ref, dk_ref, win_u, win_d, sh_u, sh_d):
        i = pl.program_id(0)
        a, sb = _glu(a_ref, b_ref)
        win_u[HALO:tt + HALO, :] = a * sb
        pa, psb = _glu(pa_ref, pb_ref, slice(tt - HALO, tt))
        win_u[0:HALO, :] = jnp.where(i > 0, pa * psb, 0.0)
        win_d[0:tt, :] = d_ref[...]
        win_d[tt:tt + HALO, :] = jnp.where(i < nt - 1, dn_ref[0:HALO, :], 0.0)
        _shift_copies(win_u, sh_u, tt + HALO)
        _shift_copies(win_d, sh_d, tt + HALO)

        @pl.when(i == 0)
        def _():
            dk_ref[...] = jnp.zeros_like(dk_ref)

        for c0 in range(0, D, CONV_CB):
            cs = slice(c0, c0 + CONV_CB)
            for r0 in range(0, tt, CONV_SB):
                acc = jnp.zeros((CONV_SB, CONV_CB), F32)
                for w in range(CONV_W):
                    st = r0 + (CONV_W - 1) - w
                    acc += _tap(win_d, sh_d, st, CONV_SB, cs) * k_ref[w:w + 1, cs]
                aa = a_ref[r0:r0 + CONV_SB, cs].astype(F32)
                ss = _sig(b_ref[r0:r0 + CONV_SB, cs].astype(F32))
                dp_ref[r0:r0 + CONV_SB, cs] = (acc * ss).astype(BF16)
                dp_ref[r0:r0 + CONV_SB, c0 + D:c0 + D + CONV_CB] = (acc * aa * ss * (1.0 - ss)).astype(BF16)
        for c0 in range(0, D, CONV_CB):
            cs = slice(c0, c0 + CONV_CB)
            for w in range(CONV_W):
                acc = jnp.zeros((CONV_SB, CONV_CB), F32)
                for r0 in range(0, tt, CONV_SB):
                    st = r0 + HALO - (CONV_W - 1) + w
                    acc += win_d[r0:r0 + CONV_SB, cs] * _tap(win_u, sh_u, st, CONV_SB, cs)
                dk_ref[w:w + 1, cs] += _colsum(acc)

    blk = lambda f, c: pl.BlockSpec((tt, D), lambda i: (f(i), c))
    cur = lambda i: i
    prv = lambda i: jnp.maximum(i - 1, 0)
    nxt = lambda i: jnp.minimum(i + 1, nt - 1)
    return _pcall(
        body, name="conv_bwd", grid=(nt,),
        in_specs=[blk(cur, 0), blk(nxt, 0), blk(cur, ca), blk(cur, cb), blk(prv, ca), blk(prv, cb),
                  pl.BlockSpec((CONV_W, D), lambda i: (0, 0)), pl.BlockSpec(memory_space=pl.ANY)],
        out_specs=[pl.BlockSpec((tt, 2 * D), lambda i: (i, 3)), pl.BlockSpec((HALO, D), lambda i: (0, 0))],
        out_shape=[jax.ShapeDtypeStruct((T, INW), BF16), jax.ShapeDtypeStruct((HALO, D), F32)],
        input_output_aliases={7: 0},
        scratch_shapes=[pltpu.VMEM((tt + HALO + CONV_PAD, D), F32), pltpu.VMEM((tt + HALO + CONV_PAD, D), F32),
                        pltpu.VMEM((SUB - 1, tt + HALO, D), F32), pltpu.VMEM((SUB - 1, tt + HALO, D), F32)],
        compiler_params=_cp("arbitrary"))(du1, du1, proj, proj, proj, proj, conv_k, dproj)


def _local_step(x, target, wts, sp, kvec, pos_c, pos_sc, adam):
    T = x.shape[0]
    cos_t, sin_t = _rope_tables(T)
    dm_t, xi_t, zeta_t, cds = _decay_tables()
    wa = lambda key: wts[LOC_W[key][0]]
    wo = lambda key: LOC_W[key][1]
    _ORDER.active, _ORDER.token = True, None

    xb, xt = _cast_t(x)
    a1, b1, h1 = _ffn_up(xb, wa("g1"), wo("g1"), wo("u1"), "ffn1_up")
    z1, x1, x1b, x1t = _proj_ln(h1, wa("d1"), wo("d1"), x, sp["ln1_g"], sp["ln1_b"], 0.5, "ffn1_down_ln")
    proj = _inproj(x1b, wa("w_in"), wo("w_in"), sp["b_in"], cos_t, sin_t)
    r, ret_in, states = _retention_fwd(proj, sp["ret_gn_g"], dm_t, xi_t, zeta_t, cds)
    kall = _sum_devices(_all_gather_small(kvec, "gather_conv_k"), "sum_conv_k")
    sp = dict(sp, conv_k=kall.reshape(-1)[:CONV_W * D].reshape(CONV_W, D))
    u1, u3 = _conv_fwd(proj, sp["conv_k"], sp["conv_b"], sp["conv_ln_g"], sp["conv_ln_b"])
    ro, co, merged = _merge(ret_in, u3, proj, wa("w_ret_o"), wo("w_ret_o"), wo("w_conv_o"))
    z2, x2, x2b, x2t = _proj_ln(merged, wa("w_out"), wo("w_out"), x1, sp["ln2_g"], sp["ln2_b"], 1.0, "out_proj_ln")
    a2, b2, h2 = _ffn_up(x2b, wa("g2"), wo("g2"), wo("u2"), "ffn2_up")
    (z3,) = _proj_ln(h2, wa("d2"), wo("d2"), x2, sp["ln3_g"], sp["ln3_b"], 0.5, "ffn2_down", want_b=False)

    sg = {}
    rs = {}
    loss, df2b, df2t, dz3, sg["ln3_g"], sg["ln3_b"] = _loss_ln_bwd(z3, sp["ln3_g"], sp["ln3_b"], target, 0.5)
    da2, db2 = _ffn_bwd_h(df2b, wa("d2"), wo("d2"), a2, b2, "ffn2_bwd_h")
    g4 = _wgrad(df2t, h2, "d2", "wgrad_d2")
    g4 = _wgrad(x2t, da2, "g2", "wgrad_g2", g4)
    g4 = _wgrad(x2t, db2, "u2", "wgrad_u2", g4)
    rs[4] = _ReduceScatter(g4, 4, pos_c, pos_sc)
    dmb, dmt, dz2, sg["ln2_g"], sg["ln2_b"] = _dx_bwd(
        [da2, db2], [wo("g2"), wo("u2")], wa("g2"), dz3, "ffn2_dx_ln", ln=(z2, sp["ln2_g"], 1.0))
    rs[4].phase2()
    g3 = _wgrad(dmt, merged, "w_out", "wgrad_out")
    dro, drot, dco, dcot, dproj = _merge_bwd(dmb, wa("w_out"), wo("w_out"), proj, ro, co)
    g3 = _wgrad(drot, ret_in, "w_ret_o", "wgrad_ret_o", g3)
    g3 = _wgrad(dcot, u3, "w_conv_o", "wgrad_conv_o", g3)
    rs[3] = _ReduceScatter(g3, 3, pos_c, pos_sc)
    dr, sg["ret_gn_g"], dproj = _reto_bwd(dro, wa("w_ret_o"), wo("w_ret_o"), r, proj, sp["ret_gn_g"], dproj)
    rs[4].phase3()
    rs[3].phase2()
    dproj = _retention_bwd(proj, dr, states, cos_t, sin_t, dm_t, xi_t, zeta_t, cds, dproj)
    du1, sg["conv_ln_g"], sg["conv_ln_b"], sg["conv_b"] = _convo_bwd(
        dco, wa("w_conv_o"), wo("w_conv_o"), u1, sp["conv_ln_g"], sp["conv_ln_b"])
    dproj, dck = _conv_bwd(du1, proj, sp["conv_k"], dproj)
    sg["conv_k"] = dck[:CONV_W]
    adam(4, rs[4].result())
    rs[3].phase3()
    g2, sg["b_in"] = _wgrad(x1t, dproj, "w_in", "wgrad_in", colsum=True)
    rs[2] = _ReduceScatter(g2, 2, pos_c, pos_sc)
    dx_part = _dx_partial(dproj, wa("w_in"), (0, 1), "mixer_dx_part")
    df1b, df1t, dz1, sg["ln1_g"], sg["ln1_b"] = _dx_bwd(
        [dproj], [wo("w_in")], wa("w_in"), dz2, "mixer_dx_ln", ln=(z1, sp["ln1_g"], 0.5), chips=(2, 3),
        partial=dx_part)
    adam(3, rs[3].result())
    rs[2].phase2()
    shapes = {n: sg[n].shape for n in SMALL + ["conv_k"]}
    small_parts = _all_gather_small(_pack_small(sg, loss, SMALL_ROWS), "gather_small")
    da1, db1 = _ffn_bwd_h(df1b, wa("d1"), wo("d1"), a1, b1, "ffn1_bwd_h")
    small_sum = _sum_devices(small_parts, "sum_small")
    g1 = _wgrad(df1t, h1, "d1", "wgrad_d1")
    rs[1] = _ReduceScatter(g1, 1, pos_c, pos_sc)
    g0 = _wgrad(xt, da1, "g1", "wgrad_g1")
    g0 = _wgrad(xt, db1, "u1", "wgrad_u1", g0)
    rs[0] = _ReduceScatter(g0, 0, pos_c, pos_sc)
    rs[2].phase3()
    rs[1].phase2()
    rs[0].phase2()
    (grad_x,) = _dx_bwd([da1, db1], [wo("g1"), wo("u1")], wa("g1"), dz1, "ffn1_dx")
    adam(2, rs[2].result())
    rs[1].phase3()
    rs[0].phase3()
    adam(1, rs[1].result())
    adam(0, rs[0].result())
    _ORDER.active = False
    return grad_x, small_sum, shapes


MESH = pl.DeviceIdType.MESH
ANY = pl.BlockSpec(memory_space=pl.ANY)
HALF = D // 2


def _place():
    x, y, c = lax.axis_index("x"), lax.axis_index("y"), lax.axis_index("c")
    chips = [(1 - x, y), (x, 1 - y), (1 - x, 1 - y)]
    return x, y, c, chips


GATHER_PIECES = 4
GATHER_ID = 1


def _gather_weights(wloc, name):
    w_ref = jax.new_ref(wloc, memory_space=pltpu.MemorySpace.HBM)
    o_ref = jax.empty_ref(jax.ShapeDtypeStruct((NCHIP, 2, HALF, wloc.shape[-1]), BF16),
                          memory_space=pltpu.MemorySpace.HBM)
    dma = pltpu.SemaphoreType.DMA

    nq = GATHER_PIECES
    rows = HALF // nq

    @pl.kernel(mesh=plsc.ScalarSubcoreMesh(axis_name="sc", num_cores=1), name=name,
               scratch_types=(dma((2 * nq,)), dma((2 * nq,)), dma((3 * nq,)), dma((3 * nq,)),
                              dma((nq,)), dma((nq,))),
               compiler_params=pltpu.CompilerParams(collective_id=GATHER_ID))
    def launch(s1, r1, s2, r2, s3, r3):
        x, y, c, _ = _place()
        me = 2 * x + y
        sib = (x, y, 1 - c)
        x_nbr, y_nbr = (1 - x, y, c), (x, 1 - y, c)
        x_chip, y_chip, d_chip = 2 * (1 - x) + y, 2 * x + (1 - y), 2 * (1 - x) + (1 - y)
        _handshake([sib, x_nbr, y_nbr])

        def rc(src, dst, ss, rs, dev):
            return pltpu.make_async_remote_copy(src_ref=src, dst_ref=dst, send_sem=ss, recv_sem=rs,
                                                device_id=dev, device_id_type=MESH)

        def piece(ref, q):
            return ref.at[pl.ds(q * rows, rows)]

        sends = []
        for q in range(nq):
            for j, nbr in enumerate((x_nbr, y_nbr)):
                sends.append(rc(piece(w_ref.at[c], q), piece(o_ref.at[me, c], q),
                                s1.at[j * nq + q], r1.at[j * nq + q], nbr))
                sends[-1].start()
        on_chip = c * x_chip + (1 - c) * y_chip
        other_chip = c * y_chip + (1 - c) * x_chip
        on_to = (c * x + (1 - c) * (1 - x), c * (1 - y) + (1 - c) * y, c)
        for q in range(nq):
            slot = piece(o_ref.at[on_chip, c], q)
            rc(slot, slot, s1.at[(1 - c) * nq + q], r1.at[(1 - c) * nq + q], sib).wait_recv()
            sends.append(rc(slot, slot, s3.at[q], r3.at[q], on_to))
            sends[-1].start()
            sends.append(rc(slot, slot, s2.at[q], r2.at[q], sib))
            sends[-1].start()
        for q in range(nq):
            slot = piece(o_ref.at[other_chip, c], q)
            rc(slot, slot, s1.at[c * nq + q], r1.at[c * nq + q], sib).wait_recv()
            sends.append(rc(slot, slot, s2.at[nq + q], r2.at[nq + q], sib))
            sends[-1].start()
        for q in range(nq):
            slot = piece(o_ref.at[d_chip, c], q)
            rc(slot, slot, s3.at[q], r3.at[q], sib).wait_recv()
            sends.append(rc(slot, slot, s2.at[2 * nq + q], r2.at[2 * nq + q], sib))
            sends[-1].start()
        for j, chip in enumerate([other_chip, on_chip, d_chip]):
            for q in range(nq):
                slot = piece(o_ref.at[chip, 1 - c], q)
                rc(slot, slot, s2.at[j * nq + q], r2.at[j * nq + q], sib).wait_recv()
        for cp in sends:
            cp.wait_send()

    launch()
    own = 2 * lax.axis_index("x") + lax.axis_index("y")
    return lax.dynamic_update_slice(o_ref[...], wloc[None], (own, 0, 0, 0))


PAIR_ID = 2
CHIP_ID = 3
HBM = pltpu.MemorySpace.HBM


def _sequencer(name, collective_id, n_sems):
    dma = pltpu.SemaphoreType.DMA
    return pl.kernel(mesh=plsc.ScalarSubcoreMesh(axis_name="sc", num_cores=1), name=name,
                     scratch_types=(dma((n_sems,)), dma((n_sems,))),
                     compiler_params=pltpu.CompilerParams(collective_id=collective_id))


def _handshake(peers):
    barrier = pltpu.get_barrier_semaphore()
    for peer in peers:
        pl.semaphore_signal(barrier, inc=1, device_id=peer, device_id_type=MESH)
    pl.semaphore_wait(barrier, len(peers))


def _pair_exchange(g5, name):
    _, _, hr, cols = g5.shape
    g_ref = jax.new_ref(g5, memory_space=HBM)
    o_ref = jax.empty_ref(jax.ShapeDtypeStruct((NCHIP, hr, cols), g5.dtype), memory_space=HBM)

    @_sequencer(name, PAIR_ID, NCHIP)
    def launch(ss, rs):
        x, y, c, _ = _place()
        sib = (x, y, 1 - c)
        _handshake([sib])
        cps = [pltpu.make_async_remote_copy(src_ref=g_ref.at[j, 1 - c], dst_ref=o_ref.at[j], send_sem=ss.at[j],
                                            recv_sem=rs.at[j], device_id=sib, device_id_type=MESH)
               for j in range(NCHIP)]
        for cp in cps:
            cp.start()
        for cp in cps:
            cp.wait()

    launch()
    return o_ref[...]


RS_TR = 256


def _pair_sum(pos, g5, got, name):
    _, _, hr, cols = g5.shape

    def body(pos_ref, g_ref, r_ref, o_ref):
        o_ref[...] = (g_ref[...].astype(F32) + r_ref[...].astype(F32)).astype(BF16)

    return _pcall(
        body, name=name, scalar_prefetch=1, grid=(NCHIP, hr // RS_TR),
        in_specs=[pl.BlockSpec((None, None, RS_TR, cols), lambda j, i, p: (j, p[0], i, 0)),
                  pl.BlockSpec((None, RS_TR, cols), lambda j, i, p: (j, i, 0))],
        out_specs=pl.BlockSpec((None, RS_TR, cols), lambda j, i, p: (j, i, 0)),
        out_shape=jax.ShapeDtypeStruct((NCHIP, hr, cols), BF16),
        compiler_params=_cp("parallel", "parallel"))(pos, g5, got)


def _chip_exchange(pb, name):
    _, hr, cols = pb.shape
    p_ref = jax.new_ref(pb, memory_space=HBM)
    o_ref = jax.empty_ref(jax.ShapeDtypeStruct((3, hr, cols), BF16), memory_space=HBM)

    @_sequencer(name, CHIP_ID, 3)
    def launch(ss, rs):
        x, y, c, chips = _place()
        _handshake([(px, py, c) for px, py in chips])
        cps = [pltpu.make_async_remote_copy(src_ref=p_ref.at[2 * px + py], dst_ref=o_ref.at[j], send_sem=ss.at[j],
                                            recv_sem=rs.at[j], device_id=(px, py, c), device_id_type=MESH)
               for j, (px, py) in enumerate(chips)]
        for cp in cps:
            cp.start()
        for cp in cps:
            cp.wait()

    launch()
    return o_ref[...]


def _chip_sum(pos, g5, got, peers, name):
    _, _, hr, cols = g5.shape

    def body(pos_ref, g_ref, r_ref, p_ref, o_ref, t_ref):
        acc = g_ref[...].astype(F32) + r_ref[...].astype(F32)
        for j in range(3):
            acc += p_ref[j].astype(F32)
        o_ref[...] = acc
        t_ref[...] = jnp.zeros_like(t_ref)

    return _pcall(
        body, name=name, scalar_prefetch=1, grid=(hr // RS_TR,),
        in_specs=[pl.BlockSpec((None, None, RS_TR, cols), lambda i, p: (p[0], p[1], i, 0)),
                  pl.BlockSpec((None, RS_TR, cols), lambda i, p: (p[0], i, 0)),
                  pl.BlockSpec((3, RS_TR, cols), lambda i, p: (0, i, 0))],
        out_specs=[pl.BlockSpec((None, RS_TR, cols), lambda i, p: (p[1], i, 0)),
                   pl.BlockSpec((8, 128), lambda i, p: (0, 0))],
        out_shape=[jax.ShapeDtypeStruct((2, hr, cols), F32), jax.ShapeDtypeStruct((8, 128), F32)],
        compiler_params=_cp("arbitrary"))(pos, g5, got, peers)


def _pair_share(gsum, name):
    g_ref = jax.new_ref(gsum, memory_space=HBM)

    @_sequencer(name, PAIR_ID, 1)
    def launch(ss, rs):
        x, y, c, _ = _place()
        sib = (x, y, 1 - c)
        _handshake([sib])
        cp = pltpu.make_async_remote_copy(src_ref=g_ref.at[c], dst_ref=g_ref.at[c], send_sem=ss.at[0],
                                          recv_sem=rs.at[0], device_id=sib, device_id_type=MESH)
        cp.start()
        cp.wait_send()
        pltpu.make_async_remote_copy(src_ref=g_ref.at[1 - c], dst_ref=g_ref.at[1 - c], send_sem=ss.at[0],
                                     recv_sem=rs.at[0], device_id=sib, device_id_type=MESH).wait_recv()

    launch()
    return g_ref[...]


class _ReduceScatter:
    def __init__(self, g_arr, gi, pos_c, pos_sc):
        _, rows, cols = g_arr.shape
        self.g5 = g_arr.reshape(NCHIP, 2, rows // 2, cols)
        self.gi, self.pos_c, self.pos_sc = gi, pos_c, pos_sc
        self.got = _pair_exchange(self.g5, f"pair_exchange_{gi}")

    def phase2(self):
        pb = _pair_sum(self.pos_c, self.g5, self.got, f"pair_sum_{self.gi}")
        self.peers = _chip_exchange(pb, f"chip_exchange_{self.gi}")

    def phase3(self):
        gsum, _ = _chip_sum(self.pos_sc, self.g5, self.got, self.peers, f"chip_sum_{self.gi}")
        self.full = _pair_share(gsum, f"pair_share_{self.gi}")

    def result(self):
        _, hr, cols = self.full.shape
        return self.full.reshape(2 * hr, cols)


SMALL_ROWS = 56


ALL_ID = 4


def _all_gather_small(vec, name):
    v_ref = jax.new_ref(vec, memory_space=HBM)
    o_ref = jax.empty_ref(jax.ShapeDtypeStruct((8, SMALL_ROWS, D), F32), memory_space=HBM)

    @_sequencer(name, ALL_ID, 8)
    def launch(ss, rs):
        x, y, c, _ = _place()
        me = 4 * x + 2 * y + c
        flip = lambda v, bit: 1 - v if bit else v
        peers = [(flip(x, m >> 2), flip(y, (m >> 1) & 1), flip(c, m & 1)) for m in range(1, 8)]
        _handshake(peers)
        mine = pltpu.make_async_copy(v_ref, o_ref.at[me], ss.at[7])
        mine.start()
        cps = [pltpu.make_async_remote_copy(src_ref=v_ref, dst_ref=o_ref.at[me], send_sem=ss.at[k],
                                            recv_sem=rs.at[k], device_id=peer, device_id_type=MESH)
               for k, peer in enumerate(peers)]
        for cp in cps:
            cp.start()
        for cp in cps:
            cp.wait()
        mine.wait()

    launch()
    return o_ref[...]


def _sum_devices(parts, name):
    def body(p_ref, o_ref):
        acc = p_ref[0]
        for d in range(1, 8):
            acc += p_ref[d]
        o_ref[...] = acc

    return _pcall(
        body, name=name, grid=(SMALL_ROWS // 8,),
        in_specs=[pl.BlockSpec((8, 8, D), lambda i: (0, i, 0))],
        out_specs=pl.BlockSpec((8, D), lambda i: (i, 0)),
        out_shape=jax.ShapeDtypeStruct((SMALL_ROWS, D), F32),
        compiler_params=_cp("parallel"))(parts)


def _adamw_math(w, g, m, v):
    m2 = ADAM_B1 * m + (1.0 - ADAM_B1) * g
    v2 = ADAM_B2 * v + (1.0 - ADAM_B2) * (g * g)
    m_hat = m2 / (1.0 - ADAM_B1 ** ADAM_STEP)
    v_hat = v2 / (1.0 - ADAM_B2 ** ADAM_STEP)
    delta = -ADAM_LR * (m_hat / (jnp.sqrt(v_hat) + ADAM_EPS) + ADAM_WD * w)
    return delta, m2, v2


def _adamw(w, g, m, v, name, g_block=None):
    R, C = w.shape
    tr = R
    gw_hint = C if g_block is None else g_block[0]
    for cand in (512, 352, 256, 176, 128, 64, 32, 16, 8):
        if R % cand == 0 and cand * max(C, gw_hint) * 4 <= (2 << 20):
            tr = cand
            break
    gw, gi = (C, 0) if g_block is None else g_block

    def body(w_ref, g_ref, m_ref, v_ref, go_ref, d_ref, mo_ref, vo_ref):
        g = g_ref[:, 0:C]
        d, m2, v2 = _adamw_math(w_ref[...], g, m_ref[...], v_ref[...])
        go_ref[...] = g
        d_ref[...] = d
        mo_ref[...] = m2
        vo_ref[...] = v2

    spec = pl.BlockSpec((tr, C), lambda i: (i, 0))
    return _pcall(
        body, name=name, grid=(R // tr,),
        in_specs=[spec, pl.BlockSpec((tr, gw), lambda i: (i, gi)), spec, spec],
        out_specs=[spec] * 4, out_shape=[jax.ShapeDtypeStruct((R, C), F32)] * 4,
        compiler_params=_cp("parallel"))(w, g, m, v)


BIG = ["ffn1_w_gate", "ffn1_w_up", "ffn1_w_down", "w_in", "w_ret_o", "w_conv_o", "w_out",
       "ffn2_w_gate", "ffn2_w_up", "ffn2_w_down"]
SLAB = {"ffn1_w_gate": "g1", "ffn1_w_up": "u1", "ffn1_w_down": "d1", "w_in": "w_in", "w_ret_o": "w_ret_o",
        "w_conv_o": "w_conv_o", "w_out": "w_out", "ffn2_w_gate": "g2", "ffn2_w_up": "u2", "ffn2_w_down": "d2"}
TRANSPOSED = {"ffn1_w_down", "ffn2_w_down", "w_ret_o", "w_conv_o", "w_out"}
MINOR_ROWS = {"ffn1_w_gate", "ffn1_w_up", "ffn2_w_gate", "ffn2_w_up"}
SMALL = ["ln1_g", "ln1_b", "ln2_g", "ln2_b", "ln3_g", "ln3_b", "conv_ln_g", "conv_ln_b", "conv_b",
         "ret_gn_g", "b_in"]
ORDER = ["ffn1_w_gate", "ffn1_w_up", "ffn1_w_down", "ln1_g", "ln1_b", "w_in", "b_in", "ret_gn_g", "conv_k",
         "conv_b", "conv_ln_g", "conv_ln_b", "w_ret_o", "w_conv_o", "w_out", "ln2_g", "ln2_b",
         "ffn2_w_gate", "ffn2_w_up", "ffn2_w_down", "ln3_g", "ln3_b"]


def _slab_width(name):
    return WIDTH[SLAB[name]]


def _pack_group(weights, keys):
    by_key = {SLAB[n]: n for n in BIG}
    parts = []
    for key in keys:
        w = weights[by_key[key]]
        w = w.T if by_key[key] in TRANSPOSED else w
        parts.append(jnp.pad(w, ((0, 0), (0, WIDTH[key] - w.shape[1]))))
    return jnp.concatenate(parts, axis=1).astype(BF16)


def _pack_small(vals, loss, rows):
    flat = jnp.concatenate([vals[n].reshape(-1) for n in SMALL] + [vals["conv_k"].reshape(-1), loss.reshape(-1)])
    return jnp.pad(flat, (0, rows * D - flat.shape[0])).reshape(rows, D)


def _unpack_small(arr, shapes):
    flat = arr.reshape(-1)
    out, pos = {}, 0
    for n in SMALL + ["conv_k"]:
        size = int(np.prod(shapes[n]))
        out[n] = flat[pos:pos + size].reshape(shapes[n])
        pos += size
    return out, flat[pos]


def kernel(x, ffn1_w_gate, ffn1_w_up, ffn1_w_down, ln1_g, ln1_b, w_in, b_in, ret_gn_g, conv_k, conv_b, conv_ln_g, conv_ln_b, w_ret_o, w_conv_o, w_out, ln2_g, ln2_b, ffn2_w_gate, ffn2_w_up, ffn2_w_down, ln3_g, ln3_b, loss_target, m_ffn1_w_gate, m_ffn1_w_up, m_ffn1_w_down, m_ln1_g, m_ln1_b, m_w_in, m_b_in, m_ret_gn_g, m_conv_k, m_conv_b, m_conv_ln_g, m_conv_ln_b, m_w_ret_o, m_w_conv_o, m_w_out, m_ln2_g, m_ln2_b, m_ffn2_w_gate, m_ffn2_w_up, m_ffn2_w_down, m_ln3_g, m_ln3_b, v_ffn1_w_gate, v_ffn1_w_up, v_ffn1_w_down, v_ln1_g, v_ln1_b, v_w_in, v_b_in, v_ret_gn_g, v_conv_k, v_conv_b, v_conv_ln_g, v_conv_ln_b, v_w_ret_o, v_w_conv_o, v_w_out, v_ln2_g, v_ln2_b, v_ffn2_w_gate, v_ffn2_w_up, v_ffn2_w_down, v_ln3_g, v_ln3_b):
    args = dict(locals())
    w = {n: args[n] for n in ORDER}
    m = {n: args["m_" + n] for n in ORDER}
    v = {n: args["v_" + n] for n in ORDER}
    xi, yi, ci = lax.axis_index("x"), lax.axis_index("y"), lax.axis_index("c")
    chip = 2 * xi + yi

    shards = {n: w[n][0] for n in BIG}
    wts = []
    for gi, keys in enumerate(GATHER_GROUPS):
        slab = _pack_group(shards, keys)
        cols = slab.shape[1]
        wts.append(_gather_weights(slab.reshape(2, HALF, cols), f"gather_{gi}").reshape(NCHIP, D, cols))

    sp = {n: w[n] for n in SMALL}
    kpad = jnp.zeros((CONV_W, D), F32)
    kpad = lax.dynamic_update_slice(kpad, w["conv_k"][0, :, 0, :] * jnp.where(ci == 0, 1.0, 0.0), (0, chip * (D // NCHIP)))
    kvec = jnp.pad(kpad.reshape(-1), (0, SMALL_ROWS * D - CONV_W * D)).reshape(SMALL_ROWS, D)
    pos_c = jnp.reshape(ci, (1,)).astype(jnp.int32)
    pos_sc = jnp.stack([chip, ci]).astype(jnp.int32)
    out = {}

    def adam(gi, slab):
        for n in BIG:
            (g_of, off), width = LOC[SLAB[n]], _slab_width(n)
            if g_of != gi:
                continue
            w2 = w[n][0]
            if n in TRANSPOSED:
                res = _adamw(w2, slab[:, off:off + w2.shape[0]].T, m[n][0], v[n][0], "adamw_" + n)
                out[n] = [r[None] for r in res]
            elif n in MINOR_ROWS:
                res = _adamw(w2.T, slab[:, off:off + w2.shape[1]].T, m[n][0].T, v[n][0].T, "adamw_" + n)
                out[n] = [r.T[None] for r in res]
            else:
                res = _adamw(w2, slab, m[n][0], v[n][0], "adamw_" + n, g_block=(width, off // width))
                out[n] = [r[None] for r in res]

    grad_x, small_sum, shapes = _local_step(x[0], loss_target[0], wts, sp, kvec, pos_c, pos_sc, adam)
    small, total = _unpack_small(small_sum, shapes)

    for n in SMALL:
        res = _adamw(w[n], small[n], m[n], v[n], "adamw_" + n)
        out[n] = list(res)
    gk = lax.dynamic_slice(small["conv_k"], (0, chip * (D // NCHIP)), (CONV_W, D // NCHIP))
    res = _adamw(w["conv_k"][0, :, 0, :], gk, m["conv_k"][0, :, 0, :], v["conv_k"][0, :, 0, :], "adamw_conv_k")
    out["conv_k"] = [r[None, :, None, :] for r in res]

    grads = [out[n][0] for n in ORDER]
    deltas = [out[n][1] for n in ORDER]
    new_m = [out[n][2] for n in ORDER]
    new_v = [out[n][3] for n in ORDER]
    return (total, grad_x[None], *grads, *deltas, *new_m, *new_v)
```

```python
import dataclasses
import functools

import numpy as np
import jax
import jax.numpy as jnp
from jax import lax
from jax.experimental import pallas as pl
from jax.experimental.pallas import tpu as pltpu
from jax.experimental.pallas import tpu_sc as plsc

F32 = jnp.float32
BF16 = jnp.bfloat16

D = 1024
FS = 704
FSP = 768
FP = 4 * FSP
H = 8
DK = 128
DV = 256
CH = 128
VW = H * DV
INW = 10240
INS = INW // 4
CONV_W = 31
HALO = 32
EPS = 1e-5
ALPHA = 2.0 ** 0.25
ROPE_BASE = 10000.0
NCHIP = 4

ADAM_LR, ADAM_B1, ADAM_B2, ADAM_EPS, ADAM_WD, ADAM_STEP = 0.001, 0.9, 0.999, 1e-08, 0.01, 10

OFF = {"w_in": 0, "w_ret_o": 2560, "g1": 3072, "u1": 3840, "d1": 4608,
       "g2": 5376, "u2": 6144, "d2": 6912, "w_conv_o": 7680, "w_out": 7936}
WCOLS = 8192
WIDTH = {"w_in": INS, "w_ret_o": VW // NCHIP, "w_conv_o": D // NCHIP, "w_out": D // NCHIP,
         "g1": FSP, "u1": FSP, "d1": FSP, "g2": FSP, "u2": FSP, "d2": FSP}
GROUPS = (("g1", "u1"), ("d1",), ("w_in",), ("w_ret_o", "w_conv_o", "w_out"), ("g2", "u2", "d2"))
GATHER_GROUPS = (("g1", "u1", "d1"), ("w_in",), ("w_ret_o", "w_conv_o", "w_out"), ("g2", "u2", "d2"))


def _locate(groups):
    loc = {}
    for gi, keys in enumerate(groups):
        off = 0
        for k in keys:
            loc[k] = (gi, off)
            off += WIDTH[k]
    return loc


LOC = _locate(GROUPS)
LOC_W = _locate(GATHER_GROUPS)
GCOLS = [sum(WIDTH[k] for k in keys) for keys in GROUPS]
VMEM_LIMIT = 56 << 20


def _cp(*sem, **kw):
    return pltpu.CompilerParams(dimension_semantics=sem, vmem_limit_bytes=VMEM_LIMIT, **kw)


class _ProgramOrder:
    def __init__(self):
        self.active = False
        self.token = None


_ORDER = _ProgramOrder()


def _pcall(body, *, in_specs, scalar_prefetch=0, **kw):
    def call(*args):
        dep = _ORDER.token if _ORDER.active else None
        specs, fn = list(in_specs), body
        if dep is not None:
            n = len(args)

            def fn(*refs):
                return body(*refs[:n], *refs[n + 1:])

            specs.append(pl.BlockSpec(memory_space=pl.ANY))
            args = (*args, dep)
        params = dict(kw)
        if scalar_prefetch:
            params["grid_spec"] = pltpu.PrefetchScalarGridSpec(
                num_scalar_prefetch=scalar_prefetch, grid=params.pop("grid"), in_specs=specs,
                out_specs=params.pop("out_specs"))
        else:
            params["in_specs"] = specs
        out = pl.pallas_call(fn, **params)(*args)
        if _ORDER.active:
            _ORDER.token = jax.tree.leaves(out)[-1]
        return out

    return call


def _resident(shape, col_block):
    lead = (0,) * (len(shape) - 1)
    return pl.BlockSpec(shape, lambda *_: (*lead, col_block), pipeline_mode=pl.Buffered(1))


def _sig(x):
    return 1.0 / (1.0 + jnp.exp(-x))


def _dot(a, b):
    return jnp.dot(a, b, preferred_element_type=F32)


def _dot_nt(a, b):
    return lax.dot_general(a, b, (((1,), (1,)), ((), ())), preferred_element_type=F32)


def _ln_fwd(z, g, b):
    mu = jnp.mean(z, axis=-1, keepdims=True)
    xc = z - mu
    var = jnp.mean(xc * xc, axis=-1, keepdims=True)
    rstd = lax.rsqrt(var + EPS)
    xh = xc * rstd
    return xh * g + b, xh, rstd


def _ln_bwd(dy, xh, rstd, g):
    dxh = dy * g
    m1 = jnp.mean(dxh, axis=-1, keepdims=True)
    m2 = jnp.mean(dxh * xh, axis=-1, keepdims=True)
    return rstd * (dxh - m1 - xh * m2)


def _colsum(x):
    return jnp.sum(x, axis=0, keepdims=True)


def _zero_first(first, *refs):
    @pl.when(first)
    def _():
        for ref in refs:
            ref[...] = jnp.zeros_like(ref)


def _rope_tables(T):
    half = DK // 2
    freqs = ROPE_BASE ** (-np.arange(half, dtype=np.float32) / half)
    ang = (np.arange(T, dtype=np.float32)[:, None] * freqs[None, :]).astype(np.float32)
    cos, sin = np.cos(ang), np.sin(ang)
    return (jnp.asarray(np.concatenate([cos, cos], 1), F32),
            jnp.asarray(np.concatenate([-sin, sin], 1), F32))


def _decay_tables():
    h = np.arange(H, dtype=np.float64)
    log_g = np.log(1.0 - np.exp2(-5.0 - h))
    idx = np.arange(CH, dtype=np.float64)
    diff = idx[:, None] - idx[None, :]
    dm = np.where(diff[None] >= 0, np.exp(np.maximum(diff, 0.0)[None] * log_g[:, None, None]), 0.0)
    xi = np.exp((idx[None, :] + 1.0) * log_g[:, None])
    zeta = np.exp((CH - 1.0 - idx)[None, :] * log_g[:, None])
    cd = np.exp(CH * log_g)
    xi_t = np.broadcast_to(xi[:, :, None], (H, CH, DV))
    zeta_t = np.broadcast_to(zeta[:, :, None], (H, CH, DK))
    return (jnp.asarray(dm, F32), jnp.asarray(xi_t, F32), jnp.asarray(zeta_t, F32),
            [float(v) for v in cd])


def _cast_t(x):
    T = x.shape[0]
    tm = min(T, 512)

    def body(x_ref, xb_ref, xt_ref):
        v = x_ref[...]
        xb_ref[...] = v.astype(BF16)
        xt_ref[...] = v.T.astype(BF16)

    return _pcall(
        body, name="cast_t", grid=(T // tm,),
        in_specs=[pl.BlockSpec((tm, D), lambda i: (i, 0))],
        out_specs=[pl.BlockSpec((tm, D), lambda i: (i, 0)), pl.BlockSpec((D, tm), lambda i: (0, i))],
        out_shape=[jax.ShapeDtypeStruct((T, D), BF16), jax.ShapeDtypeStruct((D, T), BF16)],
        compiler_params=_cp("parallel"))(x)


def _ffn_up(xb, wall, og, ou, name):
    T = xb.shape[0]
    tm = min(T, 512)
    assert ou == og + FSP

    def body(x_ref, w_ref, a_ref, b_ref, h_ref):
        x = x_ref[...]
        for s in range(NCHIP):
            cols = slice(s * FSP, (s + 1) * FSP)
            a = _dot(x, w_ref[s, :, 0:FSP])
            b = _dot(x, w_ref[s, :, FSP:2 * FSP])
            a_ref[:, cols] = a.astype(BF16)
            b_ref[:, cols] = b.astype(BF16)
            h_ref[:, cols] = (a * _sig(a) * b).astype(BF16)

    ospec = pl.BlockSpec((tm, FP), lambda i: (i, 0))
    return _pcall(
        body, name=name, grid=(T // tm,),
        in_specs=[pl.BlockSpec((tm, D), lambda i: (i, 0)), _resident((NCHIP, D, 2 * FSP), og // (2 * FSP))],
        out_specs=[ospec] * 3, out_shape=[jax.ShapeDtypeStruct((T, FP), BF16)] * 3,
        compiler_params=_cp("parallel"))(xb, wall)


def _proj_ln(hb, wall, off, res, g, b, coef, name, want_b=True):
    T, K = hb.shape
    ks = K // NCHIP
    tm = min(T, 512)
    sub = min(tm, 256)

    def body(h_ref, w_ref, r_ref, g_ref, b_ref, z_ref, *rest):
        for r0 in range(0, tm, sub):
            r = slice(r0, r0 + sub)
            acc = _dot_nt(h_ref[r, 0:ks], w_ref[0])
            for s in range(1, NCHIP):
                acc += _dot_nt(h_ref[r, s * ks:(s + 1) * ks], w_ref[s])
            z = ALPHA * r_ref[r, :] + coef * acc
            z_ref[r, :] = z
            if want_b:
                y, _, _ = _ln_fwd(z, g_ref[...], b_ref[...])
                y_ref, yb_ref, yt_ref = rest
                y_ref[r, :] = y
                yb_ref[r, :] = y.astype(BF16)
                yt_ref[:, r] = y.T.astype(BF16)

    row = pl.BlockSpec((tm, D), lambda i: (i, 0))
    vec = pl.BlockSpec((1, D), lambda i: (0, 0))
    out_specs = [row]
    out_shape = [jax.ShapeDtypeStruct((T, D), F32)]
    if want_b:
        out_specs += [row, row, pl.BlockSpec((D, tm), lambda i: (0, i))]
        out_shape += [jax.ShapeDtypeStruct((T, D), F32), jax.ShapeDtypeStruct((T, D), BF16),
                      jax.ShapeDtypeStruct((D, T), BF16)]
    return _pcall(
        body, name=name, grid=(T // tm,),
        in_specs=[pl.BlockSpec((tm, K), lambda i: (i, 0)),
                  _resident((NCHIP, D, ks), off // ks), row, vec, vec],
        out_specs=out_specs, out_shape=out_shape,
        compiler_params=_cp("parallel"))(hb, wall, res, g, b)


def _inproj(xb, wall, off, b_in, cos_t, sin_t):
    T = xb.shape[0]
    tm, tn = min(T, 512), 512
    assert off == 0

    def body(x_ref, w_ref, bias_ref, cos_ref, sin_ref, o_ref):
        x = x_ref[...]
        c = cos_ref[...]
        s = sin_ref[...]
        for n0 in range(0, INW, tn):
            chip, c0 = divmod(n0, INS)
            acc = _dot(x, w_ref[chip, :, c0:c0 + tn]) + bias_ref[:, n0:n0 + tn]
            if n0 >= 2 * D:
                o_ref[:, n0:n0 + tn] = acc.astype(BF16)
                continue
            scale = DK ** -0.5 if n0 < D else 1.0
            for hh in range(tn // DK):
                xh = acc[:, hh * DK:(hh + 1) * DK]
                o = (xh * c + pltpu.roll(xh, DK // 2, 1) * s) * scale
                o_ref[:, n0 + hh * DK:n0 + (hh + 1) * DK] = o.astype(BF16)

    return _pcall(
        body, name="inproj", grid=(T // tm,),
        in_specs=[pl.BlockSpec((tm, D), lambda i: (i, 0)),
                  _resident((NCHIP, D, INS), 0),
                  pl.BlockSpec((1, INW), lambda i: (0, 0)),
                  pl.BlockSpec((tm, DK), lambda i: (i, 0)),
                  pl.BlockSpec((tm, DK), lambda i: (i, 0))],
        out_specs=pl.BlockSpec((tm, INW), lambda i: (i, 0)),
        out_shape=jax.ShapeDtypeStruct((T, INW), BF16),
        compiler_params=_cp("parallel"))(xb, wall, b_in, cos_t, sin_t)


RET_CPS = 2


def _retention_fwd(proj, gn_g, dm_t, xi_t, zeta_t, cds):
    T = proj.shape[0]
    n = T // CH
    tr = RET_CPS * CH

    def body(q_ref, k_ref, v_ref, g_ref, gn_ref, dm_ref, xi_ref, zt_ref, r_ref, ri_ref, st_ref, state):
        @pl.when(pl.program_id(0) == 0)
        def _():
            state[...] = jnp.zeros_like(state)

        for h in range(H):
            rows = slice(h * DK, (h + 1) * DK)
            cols = slice(h * DV, (h + 1) * DV)
            s_prev = state[rows, :]
            for j in range(RET_CPS):
                t = slice(j * CH, (j + 1) * CH)
                q = q_ref[t, h * DK:(h + 1) * DK]
                k = k_ref[t, h * DK:(h + 1) * DK]
                v = v_ref[t, cols]
                s_b = s_prev.astype(BF16)
                st_ref[j, rows, :] = s_b
                sc = _dot_nt(q, k) * dm_ref[h]
                r = _dot(sc.astype(BF16), v) + _dot(q, s_b) * xi_ref[h]
                kz = k.astype(F32) * zt_ref[h]
                s_prev = cds[h] * s_prev + _dot(kz.T.astype(BF16), v)
                r_ref[t, cols] = r
                mu = jnp.mean(r, axis=-1, keepdims=True)
                xc = r - mu
                var = jnp.mean(xc * xc, axis=-1, keepdims=True)
                y = xc * lax.rsqrt(var + EPS) * gn_ref[:, cols]
                g = g_ref[t, cols].astype(F32)
                ri_ref[t, cols] = (g * _sig(g) * y).astype(BF16)
            state[rows, :] = s_prev

    full3 = lambda shp: pl.BlockSpec(shp, lambda c: (0, 0, 0))
    return _pcall(
        body, name="retention_fwd", grid=(n // RET_CPS,),
        in_specs=[pl.BlockSpec((tr, D), lambda c: (c, 0)),
                  pl.BlockSpec((tr, D), lambda c: (c, 1)),
                  pl.BlockSpec((tr, VW), lambda c: (c, 1)),
                  pl.BlockSpec((tr, VW), lambda c: (c, 2)),
                  pl.BlockSpec((1, VW), lambda c: (0, 0)),
                  full3((H, CH, CH)), full3((H, CH, DV)), full3((H, CH, DK))],
        out_specs=[pl.BlockSpec((tr, VW), lambda c: (c, 0)), pl.BlockSpec((tr, VW), lambda c: (c, 0)),
                   pl.BlockSpec((RET_CPS, H * DK, DV), lambda c: (c, 0, 0))],
        out_shape=[jax.ShapeDtypeStruct((T, VW), F32), jax.ShapeDtypeStruct((T, VW), BF16),
                   jax.ShapeDtypeStruct((n, H * DK, DV), BF16)],
        scratch_shapes=[pltpu.VMEM((H * DK, DV), F32)],
        compiler_params=_cp("arbitrary"))(proj, proj, proj, proj, gn_g, dm_t, xi_t, zeta_t)


CONV_TT = 256
CONV_SB = 64
CONV_CB = 256


SUB = 8
CONV_PAD = 8


def _glu(a_ref, b_ref, rows=slice(None)):
    a = a_ref[rows, :].astype(F32)
    sb = _sig(b_ref[rows, :].astype(F32))
    return a, sb


def _shift_copies(win, sh, rows):
    win[rows:rows + CONV_PAD, :] = jnp.zeros((CONV_PAD, D), F32)
    for b in range(1, SUB):
        sh[b - 1, :, :] = win[b:b + rows, :]


def _tap(win, sh, start, size, cs):
    a, b = divmod(start, SUB)
    src = win if b == 0 else sh.at[b - 1]
    return src[SUB * a:SUB * a + size, cs]


def _conv_fwd(proj, conv_k, conv_b, ln_g, ln_b):
    T = proj.shape[0]
    tt = min(T, CONV_TT)
    ca, cb = 6 * D // D, 7 * D // D

    def body(a_ref, b_ref, pa_ref, pb_ref, k_ref, cb_ref, g_ref, bb_ref, u1_ref, u3_ref, win, sh):
        i = pl.program_id(0)
        a, sb = _glu(a_ref, b_ref)
        win[HALO:tt + HALO, :] = a * sb
        pa, psb = _glu(pa_ref, pb_ref, slice(tt - HALO, tt))
        win[0:HALO, :] = jnp.where(i > 0, pa * psb, 0.0)
        _shift_copies(win, sh, tt + HALO)
        for c0 in range(0, D, CONV_CB):
            cs = slice(c0, c0 + CONV_CB)
            for r0 in range(0, tt, CONV_SB):
                acc = jnp.zeros((CONV_SB, CONV_CB), F32)
                for w in range(CONV_W):
                    st = r0 + HALO - (CONV_W - 1) + w
                    acc += _tap(win, sh, st, CONV_SB, cs) * k_ref[w:w + 1, cs]
                u1_ref[r0:r0 + CONV_SB, cs] = acc + cb_ref[:, cs]
        u2, _, _ = _ln_fwd(u1_ref[...], g_ref[...], bb_ref[...])
        u3_ref[...] = (u2 * _sig(u2)).astype(BF16)

    vec = pl.BlockSpec((1, D), lambda i: (0, 0))
    row = pl.BlockSpec((tt, D), lambda i: (i, 0))
    return _pcall(
        body, name="conv_fwd", grid=(T // tt,),
        in_specs=[pl.BlockSpec((tt, D), lambda i: (i, ca)), pl.BlockSpec((tt, D), lambda i: (i, cb)),
                  pl.BlockSpec((tt, D), lambda i: (jnp.maximum(i - 1, 0), ca)),
                  pl.BlockSpec((tt, D), lambda i: (jnp.maximum(i - 1, 0), cb)),
                  pl.BlockSpec((CONV_W, D), lambda i: (0, 0)), vec, vec, vec],
        out_specs=[row, row],
        out_shape=[jax.ShapeDtypeStruct((T, D), F32), jax.ShapeDtypeStruct((T, D), BF16)],
        scratch_shapes=[pltpu.VMEM((tt + HALO + CONV_PAD, D), F32), pltpu.VMEM((SUB - 1, tt + HALO, D), F32)],
        compiler_params=_cp("parallel"))(proj, proj, proj, proj, conv_k, conv_b, ln_g, ln_b)


def _merge(ret_in, u3, proj, wall, off_r, off_c):
    T = ret_in.shape[0]
    tm = min(T, 512)
    kr, kc = VW // NCHIP, D // NCHIP

    def body(ri_ref, u3_ref, gr_ref, gc_ref, wr_ref, wc_ref, ro_ref, co_ref, m_ref):
        ro = _dot_nt(ri_ref[:, 0:kr], wr_ref[0])
        co = _dot_nt(u3_ref[:, 0:kc], wc_ref[0])
        for s in range(1, NCHIP):
            ro += _dot_nt(ri_ref[:, s * kr:(s + 1) * kr], wr_ref[s])
            co += _dot_nt(u3_ref[:, s * kc:(s + 1) * kc], wc_ref[s])
        ro_ref[...] = ro.astype(BF16)
        co_ref[...] = co.astype(BF16)
        m = _sig(gr_ref[...].astype(F32)) * ro + _sig(gc_ref[...].astype(F32)) * co
        m_ref[...] = m.astype(BF16)

    row = pl.BlockSpec((tm, D), lambda i: (i, 0))
    return _pcall(
        body, name="merge", grid=(T // tm,),
        in_specs=[pl.BlockSpec((tm, VW), lambda i: (i, 0)), row,
                  pl.BlockSpec((tm, D), lambda i: (i, 8)), pl.BlockSpec((tm, D), lambda i: (i, 9)),
                  pl.BlockSpec((NCHIP, D, kr), lambda i: (0, 0, off_r // kr)),
                  pl.BlockSpec((NCHIP, D, kc), lambda i: (0, 0, off_c // kc))],
        out_specs=[row] * 3, out_shape=[jax.ShapeDtypeStruct((T, D), BF16)] * 3,
        compiler_params=_cp("parallel"))(ret_in, u3, proj, proj, wall, wall)


def _loss_ln_bwd(z, g, b, target, coef):
    T = z.shape[0]
    tm = min(T, 256)
    nt = T // tm

    def body(z_ref, g_ref, b_ref, t_ref, loss_ref, dzb_ref, dzt_ref, dz_ref, dg_ref, db_ref, lacc):
        i = pl.program_id(0)
        _zero_first(i == 0, lacc, dg_ref, db_ref)
        gam = g_ref[...]
        y, xh, rstd = _ln_fwd(z_ref[...], gam, b_ref[...])
        e = y - t_ref[...]
        lacc[...] += _colsum(e * e)
        dy = e * (1.0 / D)
        dz = _ln_bwd(dy, xh, rstd, gam)
        dz_ref[...] = dz
        dzc = coef * dz
        dzb_ref[...] = dzc.astype(BF16)
        dzt_ref[...] = dzc.T.astype(BF16)
        dg_ref[...] += _colsum(dy * xh)
        db_ref[...] += _colsum(dy)

        @pl.when(i == nt - 1)
        def _():
            loss_ref[...] = (0.5 / D) * jnp.sum(lacc[...], axis=1, keepdims=True)

    row = pl.BlockSpec((tm, D), lambda i: (i, 0))
    vec = pl.BlockSpec((1, D), lambda i: (0, 0))
    return _pcall(
        body, name="loss_ln_bwd", grid=(nt,),
        in_specs=[row, vec, vec, row],
        out_specs=[pl.BlockSpec((1, 1), lambda i: (0, 0)), row, pl.BlockSpec((D, tm), lambda i: (0, i)),
                   row, vec, vec],
        out_shape=[jax.ShapeDtypeStruct((1, 1), F32), jax.ShapeDtypeStruct((T, D), BF16),
                   jax.ShapeDtypeStruct((D, T), BF16), jax.ShapeDtypeStruct((T, D), F32),
                   jax.ShapeDtypeStruct((1, D), F32), jax.ShapeDtypeStruct((1, D), F32)],
        scratch_shapes=[pltpu.VMEM((1, D), F32)],
        compiler_params=_cp("arbitrary"))(z, g, b, target)


def _ffn_bwd_h(dfb, wall, od, a, b, name):
    T = dfb.shape[0]
    tm = min(T, 512)

    def body(d_ref, w_ref, a_ref, b_ref, da_ref, db_ref):
        d = d_ref[...]
        for s in range(NCHIP):
            cols = slice(s * FSP, (s + 1) * FSP)
            dh = _dot(d, w_ref[s])
            a = a_ref[:, cols].astype(F32)
            sg = _sig(a)
            da_ref[:, cols] = (dh * b_ref[:, cols].astype(F32) * (sg * (1.0 + a * (1.0 - sg)))).astype(BF16)
            db_ref[:, cols] = (dh * a * sg).astype(BF16)

    ospec = pl.BlockSpec((tm, FP), lambda i: (i, 0))
    return _pcall(
        body, name=name, grid=(T // tm,),
        in_specs=[pl.BlockSpec((tm, D), lambda i: (i, 0)), _resident((NCHIP, D, FSP), od // FSP), ospec, ospec],
        out_specs=[ospec] * 2, out_shape=[jax.ShapeDtypeStruct((T, FP), BF16)] * 2,
        compiler_params=_cp("parallel"))(dfb, wall, a, b)


DX_SUB = 256


def _dx_partial(lhs, wall, chips, name):
    T, K = lhs.shape
    ks = K // NCHIP
    nc = len(chips)
    assert list(chips) == list(range(chips[0], chips[0] + nc)) and chips[0] % nc == 0
    tm = min(T, 512)

    def body(l_ref, w_ref, o_ref):
        for r0 in range(0, tm, DX_SUB):
            r = slice(r0, r0 + DX_SUB)
            acc = _dot_nt(l_ref[r, 0:ks], w_ref[0])
            for s in range(1, nc):
                acc += _dot_nt(l_ref[r, s * ks:(s + 1) * ks], w_ref[s])
            o_ref[r, :] = acc

    blk = chips[0] // nc
    return _pcall(
        body, name=name, grid=(T // tm,),
        in_specs=[pl.BlockSpec((tm, nc * ks), lambda i: (i, blk)),
                  pl.BlockSpec((nc, D, ks), lambda i: (blk, 0, 0), pipeline_mode=pl.Buffered(1))],
        out_specs=pl.BlockSpec((tm, D), lambda i: (i, 0)),
        out_shape=jax.ShapeDtypeStruct((T, D), F32),
        compiler_params=_cp("parallel"))(lhs, wall)


def _dx_bwd(lhs, offs, wall, dz_next, name, ln=None, chips=tuple(range(NCHIP)), partial=None):
    T, K = lhs[0].shape
    ks = K // NCHIP
    nl = len(lhs)
    nc = len(chips)
    assert list(offs) == [l * ks for l in range(nl)]
    assert list(chips) == list(range(chips[0], chips[0] + nc)) and chips[0] % nc == 0
    blk = chips[0] // nc
    tm = min(T, 512)

    def body(*refs):
        l_refs = refs[:nl]
        w_ref = refs[nl]
        dzn_ref = refs[nl + 1]
        pos = nl + 2
        if partial is not None:
            p_ref = refs[pos]
            pos += 1
        if ln is not None:
            z_ref, g_ref = refs[pos:pos + 2]
            pos += 2
        outs = refs[pos:]
        sums = list(outs[3:]) if ln is not None else list(outs[1:])

        _zero_first(pl.program_id(0) == 0, *sums)

        for r0 in range(0, tm, DX_SUB):
            r = slice(r0, r0 + DX_SUB)
            acc = None if partial is None else p_ref[r, :]
            for s in range(nc):
                rows = slice(s * ks, (s + 1) * ks)
                for l in range(nl):
                    part = _dot_nt(l_refs[l][r, rows], w_ref[s, :, l * ks:(l + 1) * ks])
                    acc = part if acc is None else acc + part
            dy = acc + ALPHA * dzn_ref[r, :]
            if ln is None:
                outs[0][r, :] = dy
            else:
                gam = g_ref[...]
                _, xh, rstd = _ln_fwd(z_ref[r, :], gam, 0.0)
                dz = _ln_bwd(dy, xh, rstd, gam)
                dzc = ln[2] * dz
                outs[0][r, :] = dzc.astype(BF16)
                outs[1][:, r] = dzc.T.astype(BF16)
                outs[2][r, :] = dz
                outs[3][...] += _colsum(dy * xh)
                outs[4][...] += _colsum(dy)

    row = pl.BlockSpec((tm, D), lambda i: (i, 0))
    vec = pl.BlockSpec((1, D), lambda i: (0, 0))
    in_specs = [pl.BlockSpec((tm, nc * ks), lambda i: (i, blk))] * nl
    in_specs += [pl.BlockSpec((nc, D, nl * ks), lambda i: (blk, 0, 0), pipeline_mode=pl.Buffered(1)), row]
    args = list(lhs) + [wall, dz_next]
    if partial is not None:
        in_specs.append(row)
        args.append(partial)
    if ln is None:
        out_specs = [row]
        out_shape = [jax.ShapeDtypeStruct((T, D), F32)]
    else:
        in_specs += [row, vec]
        args += [ln[0], ln[1]]
        out_specs = [row, pl.BlockSpec((D, tm), lambda i: (0, i)), row, vec, vec]
        out_shape = [jax.ShapeDtypeStruct((T, D), BF16), jax.ShapeDtypeStruct((D, T), BF16),
                     jax.ShapeDtypeStruct((T, D), F32), jax.ShapeDtypeStruct((1, D), F32),
                     jax.ShapeDtypeStruct((1, D), F32)]
    return _pcall(
        body, name=name, grid=(T // tm,), in_specs=in_specs, out_specs=out_specs, out_shape=out_shape,
        compiler_params=_cp("arbitrary"))(*args)


def _wgrad(lhs_t, rhs, key, name, g_all=None, colsum=False):
    T, N = rhs.shape
    tn = next(c for c in (768, 512, 256) if (N // NCHIP) % c == 0 and LOC[key][1] % c == 0)
    nps = N // NCHIP // tn
    off = LOC[key][1]
    cols = GCOLS[LOC[key][0]]

    def body(*refs):
        l_ref, r_ref = refs[0], refs[1]
        o_ref, t_ref = refs[-2 - colsum], refs[-1]
        o_ref[...] = _dot(l_ref[...], r_ref[...]).astype(BF16)
        if colsum:
            refs[-2][...] = _colsum(r_ref[...].astype(F32))
        t_ref[...] = jnp.zeros_like(t_ref)

    in_specs = [_resident((D, T), 0), pl.BlockSpec((T, tn), lambda j: (0, j))]
    args = [lhs_t, rhs]
    aliases = {}
    if g_all is not None:
        in_specs.append(pl.BlockSpec(memory_space=pl.ANY))
        args.append(g_all)
        aliases = {2: 0}
    out_specs = [pl.BlockSpec((None, D, tn), lambda j: (j // nps, 0, off // tn + j % nps))]
    out_shape = [jax.ShapeDtypeStruct((NCHIP, D, cols), BF16)]
    if colsum:
        out_specs.append(pl.BlockSpec((1, tn), lambda j: (0, j)))
        out_shape.append(jax.ShapeDtypeStruct((1, N), F32))
    out_specs.append(pl.BlockSpec((8, 128), lambda j: (0, 0)))
    out_shape.append(jax.ShapeDtypeStruct((8, 128), F32))
    res = _pcall(
        body, name=name, grid=(N // tn,), in_specs=in_specs, out_specs=out_specs, out_shape=out_shape,
        input_output_aliases=aliases,
        compiler_params=_cp("arbitrary"))(*args)
    return (res[0], res[1]) if colsum else res[0]


def _merge_bwd(dmb, wall, off, proj, ro, co):
    T = dmb.shape[0]
    tm = min(T, 512)
    ks = D // NCHIP

    def body(d_ref, w_ref, gr_ref, gc_ref, ro_ref, co_ref, dro_ref, drot_ref, dco_ref, dcot_ref, dp_ref):
        d = d_ref[...]
        dmg = jnp.concatenate([_dot(d, w_ref[s]) for s in range(NCHIP)], axis=1)
        sr = _sig(gr_ref[...].astype(F32))
        sc = _sig(gc_ref[...].astype(F32))
        dro = dmg * sr
        dco = dmg * sc
        dro_ref[...] = dro.astype(BF16)
        drot_ref[...] = dro.T.astype(BF16)
        dco_ref[...] = dco.astype(BF16)
        dcot_ref[...] = dco.T.astype(BF16)
        dp_ref[:, 0:D] = (dmg * ro_ref[...].astype(F32) * sr * (1.0 - sr)).astype(BF16)
        dp_ref[:, D:2 * D] = (dmg * co_ref[...].astype(F32) * sc * (1.0 - sc)).astype(BF16)

    row = pl.BlockSpec((tm, D), lambda i: (i, 0))
    col = pl.BlockSpec((D, tm), lambda i: (0, i))
    return _pcall(
        body, name="merge_bwd", grid=(T // tm,),
        in_specs=[row, pl.BlockSpec((NCHIP, D, ks), lambda i: (0, 0, off // ks)),
                  pl.BlockSpec((tm, D), lambda i: (i, 8)), pl.BlockSpec((tm, D), lambda i: (i, 9)), row, row],
        out_specs=[row, col, row, col, pl.BlockSpec((tm, 2 * D), lambda i: (i, 4))],
        out_shape=[jax.ShapeDtypeStruct((T, D), BF16), jax.ShapeDtypeStruct((D, T), BF16),
                   jax.ShapeDtypeStruct((T, D), BF16), jax.ShapeDtypeStruct((D, T), BF16),
                   jax.ShapeDtypeStruct((T, INW), BF16)],
        compiler_params=_cp("parallel"))(dmb, wall, proj, proj, ro, co)


def _reto_bwd(dro, wall, off, r, proj, gn_g, dproj):
    T = dro.shape[0]
    tm = min(T, 512)
    hps = H // NCHIP

    def body(d_ref, w_ref, r_ref, g_ref, gn_ref, _, dr_ref, dgn_ref, dp_ref):
        _zero_first(pl.program_id(1) == 0, dgn_ref)
        dri = _dot(d_ref[...], w_ref[...])
        rr = r_ref[...]
        mu = jnp.mean(rr, axis=-1, keepdims=True)
        xc = rr - mu
        var = jnp.mean(xc * xc, axis=-1, keepdims=True)
        rstd = lax.rsqrt(var + EPS)
        rn = xc * rstd
        gn = gn_ref[...]
        g = g_ref[...].astype(F32)
        sg = _sig(g)
        dy = dri * (g * sg)
        dp_ref[...] = (dri * (rn * gn) * (sg * (1.0 + g * (1.0 - sg)))).astype(BF16)
        dgn_ref[...] += _colsum(dy * rn)
        dr_ref[...] = _ln_bwd(dy, rn, rstd, gn).astype(BF16)

    return _pcall(
        body, name="reto_bwd", grid=(H, T // tm),
        in_specs=[pl.BlockSpec((tm, D), lambda j, i: (i, 0)),
                  pl.BlockSpec((None, D, DV), lambda j, i: (j // hps, 0, off // DV + j % hps)),
                  pl.BlockSpec((tm, DV), lambda j, i: (i, j)),
                  pl.BlockSpec((tm, DV), lambda j, i: (i, 2 * VW // DV + j)),
                  pl.BlockSpec((1, DV), lambda j, i: (0, j)),
                  pl.BlockSpec(memory_space=pl.ANY)],
        out_specs=[pl.BlockSpec((tm, DV), lambda j, i: (i, j)), pl.BlockSpec((1, DV), lambda j, i: (0, j)),
                   pl.BlockSpec((tm, DV), lambda j, i: (i, 2 * VW // DV + j))],
        out_shape=[jax.ShapeDtypeStruct((T, VW), BF16), jax.ShapeDtypeStruct((1, VW), F32),
                   jax.ShapeDtypeStruct((T, INW), BF16)],
        input_output_aliases={5: 2},
        compiler_params=_cp("arbitrary", "arbitrary"))(dro, wall, r, proj, gn_g, dproj)


def _retention_bwd(proj, dr, states, cos_t, sin_t, dm_t, xi_t, zeta_t, cds, dproj):
    T = proj.shape[0]
    n = T // CH // RET_CPS
    tr = RET_CPS * CH
    scale = DK ** -0.5

    def body(q_ref, k_ref, v_ref, dr_ref, st_ref, cos_ref, sin_ref, dm_ref, xi_ref, zt_ref, _, dp_ref, ds):
        @pl.when(pl.program_id(0) == 0)
        def _():
            ds[...] = jnp.zeros_like(ds)

        def unrope(d, t):
            return d * cos_ref[t, :] + pltpu.roll(d * sin_ref[t, :], DK // 2, 1)

        for h in range(H):
            rows = slice(h * DK, (h + 1) * DK)
            dm = dm_ref[h]
            zt = zt_ref[h]
            ds_prev = ds[rows, :]
            for j in reversed(range(RET_CPS)):
                t = slice(j * CH, (j + 1) * CH)
                q = q_ref[t, h * DK:(h + 1) * DK]
                k = k_ref[t, h * DK:(h + 1) * DK]
                v = v_ref[t, h * DV:(h + 1) * DV]
                d_r = dr_ref[t, h * DV:(h + 1) * DV]
                s_b = st_ref[j, rows, :]
                sc = _dot_nt(q, k) * dm
                dsc = _dot_nt(d_r, v) * dm
                drx = (d_r.astype(F32) * xi_ref[h]).astype(BF16)
                ds_b = ds_prev.astype(BF16)
                kz = (k.astype(F32) * zt).astype(BF16)
                dq = _dot(dsc.astype(BF16), k) + _dot_nt(drx, s_b)
                dk = _dot(dsc.T.astype(BF16), q) + _dot_nt(v, ds_b) * zt
                dv = _dot(sc.T.astype(BF16), d_r) + _dot(kz, ds_b)
                ds_prev = cds[h] * ds_prev + _dot(q.astype(F32).T.astype(BF16), drx)
                dp_ref[t, h * DK:(h + 1) * DK] = unrope(dq * scale, t).astype(BF16)
                dp_ref[t, D + h * DK:D + (h + 1) * DK] = unrope(dk, t).astype(BF16)
                dp_ref[t, 2 * D + h * DV:2 * D + (h + 1) * DV] = dv.astype(BF16)
            ds[rows, :] = ds_prev

    rv = lambda c: n - 1 - c
    full3 = lambda shp: pl.BlockSpec(shp, lambda c: (0, 0, 0))
    return _pcall(
        body, name="retention_bwd", grid=(n,),
        in_specs=[pl.BlockSpec((tr, D), lambda c: (rv(c), 0)),
                  pl.BlockSpec((tr, D), lambda c: (rv(c), 1)),
                  pl.BlockSpec((tr, VW), lambda c: (rv(c), 1)),
                  pl.BlockSpec((tr, VW), lambda c: (rv(c), 0)),
                  pl.BlockSpec((RET_CPS, H * DK, DV), lambda c: (rv(c), 0, 0)),
                  pl.BlockSpec((tr, DK), lambda c: (rv(c), 0)),
                  pl.BlockSpec((tr, DK), lambda c: (rv(c), 0)),
                  full3((H, CH, CH)), full3((H, CH, DV)), full3((H, CH, DK)),
                  pl.BlockSpec(memory_space=pl.ANY)],
        out_specs=pl.BlockSpec((tr, 2 * D + VW), lambda c: (rv(c), 0)),
        out_shape=jax.ShapeDtypeStruct((T, INW), BF16),
        input_output_aliases={10: 0},
        scratch_shapes=[pltpu.VMEM((H * DK, DV), F32)],
        compiler_params=_cp("arbitrary"))(proj, proj, proj, dr, states, cos_t, sin_t, dm_t, xi_t, zeta_t, dproj)


def _convo_bwd(dco, wall, off, u1, ln_g, ln_b):
    T = dco.shape[0]
    tm = min(T, 512)
    ks = D // NCHIP

    def body(d_ref, w_ref, u1_ref, g_ref, b_ref, du1_ref, dg_ref, db_ref, dcb_ref):
        _zero_first(pl.program_id(0) == 0, dg_ref, db_ref, dcb_ref)
        d = d_ref[...]
        du3 = jnp.concatenate([_dot(d, w_ref[s]) for s in range(NCHIP)], axis=1)
        gam = g_ref[...]
        u2, xh, rstd = _ln_fwd(u1_ref[...], gam, b_ref[...])
        sg = _sig(u2)
        du2 = du3 * (sg * (1.0 + u2 * (1.0 - sg)))
        du1 = _ln_bwd(du2, xh, rstd, gam)
        du1_ref[...] = du1
        dg_ref[...] += _colsum(du2 * xh)
        db_ref[...] += _colsum(du2)
        dcb_ref[...] += _colsum(du1)

    row = pl.BlockSpec((tm, D), lambda i: (i, 0))
    vec = pl.BlockSpec((1, D), lambda i: (0, 0))
    return _pcall(
        body, name="convo_bwd", grid=(T // tm,),
        in_specs=[row, pl.BlockSpec((NCHIP, D, ks), lambda i: (0, 0, off // ks)), row, vec, vec],
        out_specs=[row, vec, vec, vec],
        out_shape=[jax.ShapeDtypeStruct((T, D), F32)] + [jax.ShapeDtypeStruct((1, D), F32)] * 3,
        compiler_params=_cp("arbitrary"))(dco, wall, u1, ln_g, ln_b)


def _conv_bwd(du1, proj, conv_k, dproj):
    T = du1.shape[0]
    tt = min(T, CONV_TT)
    nt = T // tt
    ca, cb = 6, 7

    def body(d_ref, dn_ref, a_ref, b_ref, pa_ref, pb_ref, k_ref, _, dp_ref, dk_ref, win_u, win_d, sh_u, sh_d):
        i = pl.program_id(0)
        a, sb = _glu(a_ref, b_ref)
        win_u[HALO:tt + HALO, :] = a * sb
        pa, psb = _glu(pa_ref, pb_ref, slice(tt - HALO, tt))
        win_u[0:HALO, :] = jnp.where(i > 0, pa * psb, 0.0)
        win_d[0:tt, :] = d_ref[...]
        win_d[tt:tt + HALO, :] = jnp.where(i < nt - 1, dn_ref[0:HALO, :], 0.0)
        _shift_copies(win_u, sh_u, tt + HALO)
        _shift_copies(win_d, sh_d, tt + HALO)

        @pl.when(i == 0)
        def _():
            dk_ref[...] = jnp.zeros_like(dk_ref)

        for c0 in range(0, D, CONV_CB):
            cs = slice(c0, c0 + CONV_CB)
            for r0 in range(0, tt, CONV_SB):
                acc = jnp.zeros((CONV_SB, CONV_CB), F32)
                for w in range(CONV_W):
                    st = r0 + (CONV_W - 1) - w
                    acc += _tap(win_d, sh_d, st, CONV_SB, cs) * k_ref[w:w + 1, cs]
                aa = a_ref[r0:r0 + CONV_SB, cs].astype(F32)
                ss = _sig(b_ref[r0:r0 + CONV_SB, cs].astype(F32))
                dp_ref[r0:r0 + CONV_SB, cs] = (acc * ss).astype(BF16)
                dp_ref[r0:r0 + CONV_SB, c0 + D:c0 + D + CONV_CB] = (acc * aa * ss * (1.0 - ss)).astype(BF16)
        for c0 in range(0, D, CONV_CB):
            cs = slice(c0, c0 + CONV_CB)
            for w in range(CONV_W):
                acc = jnp.zeros((CONV_SB, CONV_CB), F32)
                for r0 in range(0, tt, CONV_SB):
                    st = r0 + HALO - (CONV_W - 1) + w
                    acc += win_d[r0:r0 + CONV_SB, cs] * _tap(win_u, sh_u, st, CONV_SB, cs)
                dk_ref[w:w + 1, cs] += _colsum(acc)

    blk = lambda f, c: pl.BlockSpec((tt, D), lambda i: (f(i), c))
    cur = lambda i: i
    prv = lambda i: jnp.maximum(i - 1, 0)
    nxt = lambda i: jnp.minimum(i + 1, nt - 1)
    return _pcall(
        body, name="conv_bwd", grid=(nt,),
        in_specs=[blk(cur, 0), blk(nxt, 0), blk(cur, ca), blk(cur, cb), blk(prv, ca), blk(prv, cb),
                  pl.BlockSpec((CONV_W, D), lambda i: (0, 0)), pl.BlockSpec(memory_space=pl.ANY)],
        out_specs=[pl.BlockSpec((tt, 2 * D), lambda i: (i, 3)), pl.BlockSpec((HALO, D), lambda i: (0, 0))],
        out_shape=[jax.ShapeDtypeStruct((T, INW), BF16), jax.ShapeDtypeStruct((HALO, D), F32)],
        input_output_aliases={7: 0},
        scratch_shapes=[pltpu.VMEM((tt + HALO + CONV_PAD, D), F32), pltpu.VMEM((tt + HALO + CONV_PAD, D), F32),
                        pltpu.VMEM((SUB - 1, tt + HALO, D), F32), pltpu.VMEM((SUB - 1, tt + HALO, D), F32)],
        compiler_params=_cp("arbitrary"))(du1, du1, proj, proj, proj, proj, conv_k, dproj)


def _local_step(x, target, wts, sp, kvec, pos_c, pos_sc, adam):
    T = x.shape[0]
    cos_t, sin_t = _rope_tables(T)
    dm_t, xi_t, zeta_t, cds = _decay_tables()
    wa = lambda key: wts[LOC_W[key][0]]
    wo = lambda key: LOC_W[key][1]
    _ORDER.active, _ORDER.token = True, None

    xb, xt = _cast_t(x)
    a1, b1, h1 = _ffn_up(xb, wa("g1"), wo("g1"), wo("u1"), "ffn1_up")
    z1, x1, x1b, x1t = _proj_ln(h1, wa("d1"), wo("d1"), x, sp["ln1_g"], sp["ln1_b"], 0.5, "ffn1_down_ln")
    proj = _inproj(x1b, wa("w_in"), wo("w_in"), sp["b_in"], cos_t, sin_t)
    r, ret_in, states = _retention_fwd(proj, sp["ret_gn_g"], dm_t, xi_t, zeta_t, cds)
    kall = _sum_devices(_all_gather_small(kvec, "gather_conv_k"), "sum_conv_k")
    sp = dict(sp, conv_k=kall.reshape(-1)[:CONV_W * D].reshape(CONV_W, D))
    u1, u3 = _conv_fwd(proj, sp["conv_k"], sp["conv_b"], sp["conv_ln_g"], sp["conv_ln_b"])
    ro, co, merged = _merge(ret_in, u3, proj, wa("w_ret_o"), wo("w_ret_o"), wo("w_conv_o"))
    z2, x2, x2b, x2t = _proj_ln(merged, wa("w_out"), wo("w_out"), x1, sp["ln2_g"], sp["ln2_b"], 1.0, "out_proj_ln")
    a2, b2, h2 = _ffn_up(x2b, wa("g2"), wo("g2"), wo("u2"), "ffn2_up")
    (z3,) = _proj_ln(h2, wa("d2"), wo("d2"), x2, sp["ln3_g"], sp["ln3_b"], 0.5, "ffn2_down", want_b=False)

    sg = {}
    rs = {}
    loss, df2b, df2t, dz3, sg["ln3_g"], sg["ln3_b"] = _loss_ln_bwd(z3, sp["ln3_g"], sp["ln3_b"], target, 0.5)
    da2, db2 = _ffn_bwd_h(df2b, wa("d2"), wo("d2"), a2, b2, "ffn2_bwd_h")
    g4 = _wgrad(df2t, h2, "d2", "wgrad_d2")
    g4 = _wgrad(x2t, da2, "g2", "wgrad_g2", g4)
    g4 = _wgrad(x2t, db2, "u2", "wgrad_u2", g4)
    rs[4] = _ReduceScatter(g4, 4, pos_c, pos_sc)
    dmb, dmt, dz2, sg["ln2_g"], sg["ln2_b"] = _dx_bwd(
        [da2, db2], [wo("g2"), wo("u2")], wa("g2"), dz3, "ffn2_dx_ln", ln=(z2, sp["ln2_g"], 1.0))
    rs[4].phase2()
    g3 = _wgrad(dmt, merged, "w_out", "wgrad_out")
    dro, drot, dco, dcot, dproj = _merge_bwd(dmb, wa("w_out"), wo("w_out"), proj, ro, co)
    g3 = _wgrad(drot, ret_in, "w_ret_o", "wgrad_ret_o", g3)
    g3 = _wgrad(dcot, u3, "w_conv_o", "wgrad_conv_o", g3)
    rs[3] = _ReduceScatter(g3, 3, pos_c, pos_sc)
    dr, sg["ret_gn_g"], dproj = _reto_bwd(dro, wa("w_ret_o"), wo("w_ret_o"), r, proj, sp["ret_gn_g"], dproj)
    rs[4].phase3()
    rs[3].phase2()
    dproj = _retention_bwd(proj, dr, states, cos_t, sin_t, dm_t, xi_t, zeta_t, cds, dproj)
    du1, sg["conv_ln_g"], sg["conv_ln_b"], sg["conv_b"] = _convo_bwd(
        dco, wa("w_conv_o"), wo("w_conv_o"), u1, sp["conv_ln_g"], sp["conv_ln_b"])
    dproj, dck = _conv_bwd(du1, proj, sp["conv_k"], dproj)
    sg["conv_k"] = dck[:CONV_W]
    adam(4, rs[4].result())
    rs[3].phase3()
    g2, sg["b_in"] = _wgrad(x1t, dproj, "w_in", "wgrad_in", colsum=True)
    rs[2] = _ReduceScatter(g2, 2, pos_c, pos_sc)
    dx_part = _dx_partial(dproj, wa("w_in"), (0, 1), "mixer_dx_part")
    df1b, df1t, dz1, sg["ln1_g"], sg["ln1_b"] = _dx_bwd(
        [dproj], [wo("w_in")], wa("w_in"), dz2, "mixer_dx_ln", ln=(z1, sp["ln1_g"], 0.5), chips=(2, 3),
        partial=dx_part)
    adam(3, rs[3].result())
    rs[2].phase2()
    shapes = {n: sg[n].shape for n in SMALL + ["conv_k"]}
    small_parts = _all_gather_small(_pack_small(sg, loss, SMALL_ROWS), "gather_small")
    da1, db1 = _ffn_bwd_h(df1b, wa("d1"), wo("d1"), a1, b1, "ffn1_bwd_h")
    small_sum = _sum_devices(small_parts, "sum_small")
    g1 = _wgrad(df1t, h1, "d1", "wgrad_d1")
    rs[1] = _ReduceScatter(g1, 1, pos_c, pos_sc)
    g0 = _wgrad(xt, da1, "g1", "wgrad_g1")
    g0 = _wgrad(xt, db1, "u1", "wgrad_u1", g0)
    rs[0] = _ReduceScatter(g0, 0, pos_c, pos_sc)
    rs[2].phase3()
    rs[1].phase2()
    rs[0].phase2()
    (grad_x,) = _dx_bwd([da1, db1], [wo("g1"), wo("u1")], wa("g1"), dz1, "ffn1_dx")
    adam(2, rs[2].result())
    rs[1].phase3()
    rs[0].phase3()
    adam(1, rs[1].result())
    adam(0, rs[0].result())
    _ORDER.active = False
    return grad_x, small_sum, shapes


MESH = pl.DeviceIdType.MESH
ANY = pl.BlockSpec(memory_space=pl.ANY)
HALF = D // 2


def _place():
    x, y, c = lax.axis_index("x"), lax.axis_index("y"), lax.axis_index("c")
    chips = [(1 - x, y), (x, 1 - y), (1 - x, 1 - y)]
    return x, y, c, chips


GATHER_PIECES = 4
GATHER_ID = 1


def _gather_weights(wloc, name):
    w_ref = jax.new_ref(wloc, memory_space=pltpu.MemorySpace.HBM)
    o_ref = jax.empty_ref(jax.ShapeDtypeStruct((NCHIP, 2, HALF, wloc.shape[-1]), BF16),
                          memory_space=pltpu.MemorySpace.HBM)
    dma = pltpu.SemaphoreType.DMA

    nq = GATHER_PIECES
    rows = HALF // nq

    @pl.kernel(mesh=plsc.ScalarSubcoreMesh(axis_name="sc", num_cores=1), name=name,
               scratch_types=(dma((2 * nq,)), dma((2 * nq,)), dma((3 * nq,)), dma((3 * nq,)),
                              dma((nq,)), dma((nq,))),
               compiler_params=pltpu.CompilerParams(collective_id=GATHER_ID))
    def launch(s1, r1, s2, r2, s3, r3):
        x, y, c, _ = _place()
        me = 2 * x + y
        sib = (x, y, 1 - c)
        x_nbr, y_nbr = (1 - x, y, c), (x, 1 - y, c)
        x_chip, y_chip, d_chip = 2 * (1 - x) + y, 2 * x + (1 - y), 2 * (1 - x) + (1 - y)
        _handshake([sib, x_nbr, y_nbr])

        def rc(src, dst, ss, rs, dev):
            return pltpu.make_async_remote_copy(src_ref=src, dst_ref=dst, send_sem=ss, recv_sem=rs,
                                                device_id=dev, device_id_type=MESH)

        def piece(ref, q):
            return ref.at[pl.ds(q * rows, rows)]

        sends = []
        for q in range(nq):
            for j, nbr in enumerate((x_nbr, y_nbr)):
                sends.append(rc(piece(w_ref.at[c], q), piece(o_ref.at[me, c], q),
                                s1.at[j * nq + q], r1.at[j * nq + q], nbr))
                sends[-1].start()
        on_chip = c * x_chip + (1 - c) * y_chip
        other_chip = c * y_chip + (1 - c) * x_chip
        on_to = (c * x + (1 - c) * (1 - x), c * (1 - y) + (1 - c) * y, c)
        for q in range(nq):
            slot = piece(o_ref.at[on_chip, c], q)
            rc(slot, slot, s1.at[(1 - c) * nq + q], r1.at[(1 - c) * nq + q], sib).wait_recv()
            sends.append(rc(slot, slot, s3.at[q], r3.at[q], on_to))
            sends[-1].start()
            sends.append(rc(slot, slot, s2.at[q], r2.at[q], sib))
            sends[-1].start()
        for q in range(nq):
            slot = piece(o_ref.at[other_chip, c], q)
            rc(slot, slot, s1.at[c * nq + q], r1.at[c * nq + q], sib).wait_recv()
            sends.append(rc(slot, slot, s2.at[nq + q], r2.at[nq + q], sib))
            sends[-1].start()
        for q in range(nq):
            slot = piece(o_ref.at[d_chip, c], q)
            rc(slot, slot, s3.at[q], r3.at[q], sib).wait_recv()
            sends.append(rc(slot, slot, s2.at[2 * nq + q], r2.at[2 * nq + q], sib))
            sends[-1].start()
        for j, chip in enumerate([other_chip, on_chip, d_chip]):
            for q in range(nq):
                slot = piece(o_ref.at[chip, 1 - c], q)
                rc(slot, slot, s2.at[j * nq + q], r2.at[j * nq + q], sib).wait_recv()
        for cp in sends:
            cp.wait_send()

    launch()
    return _place_own(o_ref[...], wloc, name + "_own")


def _place_own(gathered, wloc, name):
    def body(w_ref, g_ref, o_ref, sem):
        own = 2 * lax.axis_index("x") + lax.axis_index("y")
        cp = pltpu.make_async_copy(w_ref, o_ref.at[own], sem)
        cp.start()
        cp.wait()

    return pl.pallas_call(
        body, name=name, in_specs=[ANY, ANY], out_specs=ANY,
        out_shape=jax.ShapeDtypeStruct(gathered.shape, gathered.dtype),
        input_output_aliases={1: 0},
        scratch_shapes=[pltpu.SemaphoreType.DMA(())])(wloc, gathered)


PAIR_ID = 2
CHIP_ID = 3
HBM = pltpu.MemorySpace.HBM


def _sequencer(name, collective_id, n_sems):
    dma = pltpu.SemaphoreType.DMA
    return pl.kernel(mesh=plsc.ScalarSubcoreMesh(axis_name="sc", num_cores=1), name=name,
                     scratch_types=(dma((n_sems,)), dma((n_sems,))),
                     compiler_params=pltpu.CompilerParams(collective_id=collective_id))


def _handshake(peers):
    barrier = pltpu.get_barrier_semaphore()
    for peer in peers:
        pl.semaphore_signal(barrier, inc=1, device_id=peer, device_id_type=MESH)
    pl.semaphore_wait(barrier, len(peers))


def _pair_exchange(g5, name):
    _, _, hr, cols = g5.shape
    g_ref = jax.new_ref(g5, memory_space=HBM)
    o_ref = jax.empty_ref(jax.ShapeDtypeStruct((NCHIP, hr, cols), g5.dtype), memory_space=HBM)

    @_sequencer(name, PAIR_ID, NCHIP)
    def launch(ss, rs):
        x, y, c, _ = _place()
        sib = (x, y, 1 - c)
        _handshake([sib])
        cps = [pltpu.make_async_remote_copy(src_ref=g_ref.at[j, 1 - c], dst_ref=o_ref.at[j], send_sem=ss.at[j],
                                            recv_sem=rs.at[j], device_id=sib, device_id_type=MESH)
               for j in range(NCHIP)]
        for cp in cps:
            cp.start()
        for cp in cps:
            cp.wait()

    launch()
    return o_ref[...]


RS_TR = 256


def _pair_sum(pos, g5, got, name):
    _, _, hr, cols = g5.shape

    def body(pos_ref, g_ref, r_ref, o_ref):
        o_ref[...] = (g_ref[...].astype(F32) + r_ref[...].astype(F32)).astype(BF16)

    return _pcall(
        body, name=name, scalar_prefetch=1, grid=(NCHIP, hr // RS_TR),
        in_specs=[pl.BlockSpec((None, None, RS_TR, cols), lambda j, i, p: (j, p[0], i, 0)),
                  pl.BlockSpec((None, RS_TR, cols), lambda j, i, p: (j, i, 0))],
        out_specs=pl.BlockSpec((None, RS_TR, cols), lambda j, i, p: (j, i, 0)),
        out_shape=jax.ShapeDtypeStruct((NCHIP, hr, cols), BF16),
        compiler_params=_cp("parallel", "parallel"))(pos, g5, got)


def _chip_exchange(pb, name):
    _, hr, cols = pb.shape
    p_ref = jax.new_ref(pb, memory_space=HBM)
    o_ref = jax.empty_ref(jax.ShapeDtypeStruct((3, hr, cols), BF16), memory_space=HBM)

    @_sequencer(name, CHIP_ID, 3)
    def launch(ss, rs):
        x, y, c, chips = _place()
        _handshake([(px, py, c) for px, py in chips])
        cps = [pltpu.make_async_remote_copy(src_ref=p_ref.at[2 * px + py], dst_ref=o_ref.at[j], send_sem=ss.at[j],
                                            recv_sem=rs.at[j], device_id=(px, py, c), device_id_type=MESH)
               for j, (px, py) in enumerate(chips)]
        for cp in cps:
            cp.start()
        for cp in cps:
            cp.wait()

    launch()
    return o_ref[...]


def _chip_sum(pos, g5, got, peers, name):
    _, _, hr, cols = g5.shape

    def body(pos_ref, g_ref, r_ref, p_ref, o_ref, t_ref):
        acc = g_ref[...].astype(F32) + r_ref[...].astype(F32)
        for j in range(3):
            acc += p_ref[j].astype(F32)
        o_ref[...] = acc
        t_ref[...] = jnp.zeros_like(t_ref)

    return _pcall(
        body, name=name, scalar_prefetch=1, grid=(hr // RS_TR,),
        in_specs=[pl.BlockSpec((None, None, RS_TR, cols), lambda i, p: (p[0], p[1], i, 0)),
                  pl.BlockSpec((None, RS_TR, cols), lambda i, p: (p[0], i, 0)),
                  pl.BlockSpec((3, RS_TR, cols), lambda i, p: (0, i, 0))],
        out_specs=[pl.BlockSpec((None, RS_TR, cols), lambda i, p: (p[1], i, 0)),
                   pl.BlockSpec((8, 128), lambda i, p: (0, 0))],
        out_shape=[jax.ShapeDtypeStruct((2, hr, cols), F32), jax.ShapeDtypeStruct((8, 128), F32)],
        compiler_params=_cp("arbitrary"))(pos, g5, got, peers)


def _pair_share(gsum, name):
    g_ref = jax.new_ref(gsum, memory_space=HBM)

    @_sequencer(name, PAIR_ID, 1)
    def launch(ss, rs):
        x, y, c, _ = _place()
        sib = (x, y, 1 - c)
        _handshake([sib])
        cp = pltpu.make_async_remote_copy(src_ref=g_ref.at[c], dst_ref=g_ref.at[c], send_sem=ss.at[0],
                                          recv_sem=rs.at[0], device_id=sib, device_id_type=MESH)
        cp.start()
        cp.wait_send()
        pltpu.make_async_remote_copy(src_ref=g_ref.at[1 - c], dst_ref=g_ref.at[1 - c], send_sem=ss.at[0],
                                     recv_sem=rs.at[0], device_id=sib, device_id_type=MESH).wait_recv()

    launch()
    return g_ref[...]


class _ReduceScatter:
    def __init__(self, g_arr, gi, pos_c, pos_sc):
        _, rows, cols = g_arr.shape
        self.g5 = g_arr.reshape(NCHIP, 2, rows // 2, cols)
        self.gi, self.pos_c, self.pos_sc = gi, pos_c, pos_sc
        self.got = _pair_exchange(self.g5, f"pair_exchange_{gi}")

    def phase2(self):
        pb = _pair_sum(self.pos_c, self.g5, self.got, f"pair_sum_{self.gi}")
        self.peers = _chip_exchange(pb, f"chip_exchange_{self.gi}")

    def phase3(self):
        gsum, _ = _chip_sum(self.pos_sc, self.g5, self.got, self.peers, f"chip_sum_{self.gi}")
        self.full = _pair_share(gsum, f"pair_share_{self.gi}")

    def result(self):
        _, hr, cols = self.full.shape
        return self.full.reshape(2 * hr, cols)


SMALL_ROWS = 56


ALL_ID = 4


def _all_gather_small(vec, name):
    v_ref = jax.new_ref(vec, memory_space=HBM)
    o_ref = jax.empty_ref(jax.ShapeDtypeStruct((8, SMALL_ROWS, D), F32), memory_space=HBM)

    @_sequencer(name, ALL_ID, 8)
    def launch(ss, rs):
        x, y, c, _ = _place()
        me = 4 * x + 2 * y + c
        flip = lambda v, bit: 1 - v if bit else v
        peers = [(flip(x, m >> 2), flip(y, (m >> 1) & 1), flip(c, m & 1)) for m in range(1, 8)]
        _handshake(peers)
        mine = pltpu.make_async_copy(v_ref, o_ref.at[me], ss.at[7])
        mine.start()
        cps = [pltpu.make_async_remote_copy(src_ref=v_ref, dst_ref=o_ref.at[me], send_sem=ss.at[k],
                                            recv_sem=rs.at[k], device_id=peer, device_id_type=MESH)
               for k, peer in enumerate(peers)]
        for cp in cps:
            cp.start()
        for cp in cps:
            cp.wait()
        mine.wait()

    launch()
    return o_ref[...]


def _sum_devices(parts, name):
    def body(p_ref, o_ref):
        acc = p_ref[0]
        for d in range(1, 8):
            acc += p_ref[d]
        o_ref[...] = acc

    return _pcall(
        body, name=name, grid=(SMALL_ROWS // 8,),
        in_specs=[pl.BlockSpec((8, 8, D), lambda i: (0, i, 0))],
        out_specs=pl.BlockSpec((8, D), lambda i: (i, 0)),
        out_shape=jax.ShapeDtypeStruct((SMALL_ROWS, D), F32),
        compiler_params=_cp("parallel"))(parts)


def _adamw_math(w, g, m, v):
    m2 = ADAM_B1 * m + (1.0 - ADAM_B1) * g
    v2 = ADAM_B2 * v + (1.0 - ADAM_B2) * (g * g)
    m_hat = m2 / (1.0 - ADAM_B1 ** ADAM_STEP)
    v_hat = v2 / (1.0 - ADAM_B2 ** ADAM_STEP)
    delta = -ADAM_LR * (m_hat / (jnp.sqrt(v_hat) + ADAM_EPS) + ADAM_WD * w)
    return delta, m2, v2


def _adamw(w, g, m, v, name, g_block=None):
    R, C = w.shape
    tr = R
    gw_hint = C if g_block is None else g_block[0]
    for cand in (512, 352, 256, 176, 128, 64, 32, 16, 8):
        if R % cand == 0 and cand * max(C, gw_hint) * 4 <= (2 << 20):
            tr = cand
            break
    gw, gi = (C, 0) if g_block is None else g_block

    def body(w_ref, g_ref, m_ref, v_ref, go_ref, d_ref, mo_ref, vo_ref):
        g = g_ref[:, 0:C]
        d, m2, v2 = _adamw_math(w_ref[...], g, m_ref[...], v_ref[...])
        go_ref[...] = g
        d_ref[...] = d
        mo_ref[...] = m2
        vo_ref[...] = v2

    spec = pl.BlockSpec((tr, C), lambda i: (i, 0))
    return _pcall(
        body, name=name, grid=(R // tr,),
        in_specs=[spec, pl.BlockSpec((tr, gw), lambda i: (i, gi)), spec, spec],
        out_specs=[spec] * 4, out_shape=[jax.ShapeDtypeStruct((R, C), F32)] * 4,
        compiler_params=_cp("parallel"))(w, g, m, v)


BIG = ["ffn1_w_gate", "ffn1_w_up", "ffn1_w_down", "w_in", "w_ret_o", "w_conv_o", "w_out",
       "ffn2_w_gate", "ffn2_w_up", "ffn2_w_down"]
SLAB = {"ffn1_w_gate": "g1", "ffn1_w_up": "u1", "ffn1_w_down": "d1", "w_in": "w_in", "w_ret_o": "w_ret_o",
        "w_conv_o": "w_conv_o", "w_out": "w_out", "ffn2_w_gate": "g2", "ffn2_w_up": "u2", "ffn2_w_down": "d2"}
TRANSPOSED = {"ffn1_w_down", "ffn2_w_down", "w_ret_o", "w_conv_o", "w_out"}
MINOR_ROWS = {"ffn1_w_gate", "ffn1_w_up", "ffn2_w_gate", "ffn2_w_up"}
SMALL = ["ln1_g", "ln1_b", "ln2_g", "ln2_b", "ln3_g", "ln3_b", "conv_ln_g", "conv_ln_b", "conv_b",
         "ret_gn_g", "b_in"]
ORDER = ["ffn1_w_gate", "ffn1_w_up", "ffn1_w_down", "ln1_g", "ln1_b", "w_in", "b_in", "ret_gn_g", "conv_k",
         "conv_b", "conv_ln_g", "conv_ln_b", "w_ret_o", "w_conv_o", "w_out", "ln2_g", "ln2_b",
         "ffn2_w_gate", "ffn2_w_up", "ffn2_w_down", "ln3_g", "ln3_b"]


def _slab_width(name):
    return WIDTH[SLAB[name]]


def _pack_group(weights, keys):
    by_key = {SLAB[n]: n for n in BIG}
    parts = []
    for key in keys:
        w = weights[by_key[key]]
        w = w.T if by_key[key] in TRANSPOSED else w
        parts.append(jnp.pad(w, ((0, 0), (0, WIDTH[key] - w.shape[1]))))
    return jnp.concatenate(parts, axis=1).astype(BF16)


def _pack_small(vals, loss, rows):
    flat = jnp.concatenate([vals[n].reshape(-1) for n in SMALL] + [vals["conv_k"].reshape(-1), loss.reshape(-1)])
    return jnp.pad(flat, (0, rows * D - flat.shape[0])).reshape(rows, D)


def _unpack_small(arr, shapes):
    flat = arr.reshape(-1)
    out, pos = {}, 0
    for n in SMALL + ["conv_k"]:
        size = int(np.prod(shapes[n]))
        out[n] = flat[pos:pos + size].reshape(shapes[n])
        pos += size
    return out, flat[pos]


def kernel(x, ffn1_w_gate, ffn1_w_up, ffn1_w_down, ln1_g, ln1_b, w_in, b_in, ret_gn_g, conv_k, conv_b, conv_ln_g, conv_ln_b, w_ret_o, w_conv_o, w_out, ln2_g, ln2_b, ffn2_w_gate, ffn2_w_up, ffn2_w_down, ln3_g, ln3_b, loss_target, m_ffn1_w_gate, m_ffn1_w_up, m_ffn1_w_down, m_ln1_g, m_ln1_b, m_w_in, m_b_in, m_ret_gn_g, m_conv_k, m_conv_b, m_conv_ln_g, m_conv_ln_b, m_w_ret_o, m_w_conv_o, m_w_out, m_ln2_g, m_ln2_b, m_ffn2_w_gate, m_ffn2_w_up, m_ffn2_w_down, m_ln3_g, m_ln3_b, v_ffn1_w_gate, v_ffn1_w_up, v_ffn1_w_down, v_ln1_g, v_ln1_b, v_w_in, v_b_in, v_ret_gn_g, v_conv_k, v_conv_b, v_conv_ln_g, v_conv_ln_b, v_w_ret_o, v_w_conv_o, v_w_out, v_ln2_g, v_ln2_b, v_ffn2_w_gate, v_ffn2_w_up, v_ffn2_w_down, v_ln3_g, v_ln3_b):
    args = dict(locals())
    w = {n: args[n] for n in ORDER}
    m = {n: args["m_" + n] for n in ORDER}
    v = {n: args["v_" + n] for n in ORDER}
    xi, yi, ci = lax.axis_index("x"), lax.axis_index("y"), lax.axis_index("c")
    chip = 2 * xi + yi

    shards = {n: w[n][0] for n in BIG}
    wts = []
    for gi, keys in enumerate(GATHER_GROUPS):
        slab = _pack_group(shards, keys)
        cols = slab.shape[1]
        wts.append(_gather_weights(slab.reshape(2, HALF, cols), f"gather_{gi}").reshape(NCHIP, D, cols))

    sp = {n: w[n] for n in SMALL}
    kpad = jnp.zeros((CONV_W, D), F32)
    kpad = lax.dynamic_update_slice(kpad, w["conv_k"][0, :, 0, :] * jnp.where(ci == 0, 1.0, 0.0), (0, chip * (D // NCHIP)))
    kvec = jnp.pad(kpad.reshape(-1), (0, SMALL_ROWS * D - CONV_W * D)).reshape(SMALL_ROWS, D)
    pos_c = jnp.reshape(ci, (1,)).astype(jnp.int32)
    pos_sc = jnp.stack([chip, ci]).astype(jnp.int32)
    out = {}

    def adam(gi, slab):
        for n in BIG:
            (g_of, off), width = LOC[SLAB[n]], _slab_width(n)
            if g_of != gi:
                continue
            w2 = w[n][0]
            if n in TRANSPOSED:
                res = _adamw(w2, slab[:, off:off + w2.shape[0]].T, m[n][0], v[n][0], "adamw_" + n)
                out[n] = [r[None] for r in res]
            elif n in MINOR_ROWS:
                res = _adamw(w2.T, slab[:, off:off + w2.shape[1]].T, m[n][0].T, v[n][0].T, "adamw_" + n)
                out[n] = [r.T[None] for r in res]
            else:
                res = _adamw(w2, slab, m[n][0], v[n][0], "adamw_" + n, g_block=(width, off // width))
                out[n] = [r[None] for r in res]

    grad_x, small_sum, shapes = _local_step(x[0], loss_target[0], wts, sp, kvec, pos_c, pos_sc, adam)
    small, total = _unpack_small(small_sum, shapes)

    for n in SMALL:
        res = _adamw(w[n], small[n], m[n], v[n], "adamw_" + n)
        out[n] = list(res)
    gk = lax.dynamic_slice(small["conv_k"], (0, chip * (D // NCHIP)), (CONV_W, D // NCHIP))
    res = _adamw(w["conv_k"][0, :, 0, :], gk, m["conv_k"][0, :, 0, :], v["conv_k"][0, :, 0, :], "adamw_conv_k")
    out["conv_k"] = [r[None, :, None, :] for r in res]

    grads = [out[n][0] for n in ORDER]
    deltas = [out[n][1] for n in ORDER]
    new_m = [out[n][2] for n in ORDER]
    new_v = [out[n][3] for n in ORDER]
    return (total, grad_x[None], *grads, *deltas, *new_m, *new_v)
```

```python
import dataclasses
import functools

import numpy as np
import jax
import jax.numpy as jnp
from jax import lax
from jax.experimental import pallas as pl
from jax.experimental.pallas import tpu as pltpu
from jax.experimental.pallas import tpu_sc as plsc

F32 = jnp.float32
BF16 = jnp.bfloat16

D = 1024
FS = 704
FSP = 768
FP = 4 * FSP
H = 8
DK = 128
DV = 256
CH = 128
VW = H * DV
INW = 10240
INS = INW // 4
CONV_W = 31
HALO = 32
EPS = 1e-5
ALPHA = 2.0 ** 0.25
ROPE_BASE = 10000.0
NCHIP = 4

ADAM_LR, ADAM_B1, ADAM_B2, ADAM_EPS, ADAM_WD, ADAM_STEP = 0.001, 0.9, 0.999, 1e-08, 0.01, 10

OFF = {"w_in": 0, "w_ret_o": 2560, "g1": 3072, "u1": 3840, "d1": 4608,
       "g2": 5376, "u2": 6144, "d2": 6912, "w_conv_o": 7680, "w_out": 7936}
WCOLS = 8192
WIDTH = {"w_in": INS, "w_ret_o": VW // NCHIP, "w_conv_o": D // NCHIP, "w_out": D // NCHIP,
         "g1": FSP, "u1": FSP, "d1": FSP, "g2": FSP, "u2": FSP, "d2": FSP}
GROUPS = (("g1", "u1"), ("d1",), ("w_in",), ("w_ret_o", "w_conv_o", "w_out"), ("g2", "u2", "d2"))
GATHER_GROUPS = (("g1", "u1", "d1"), ("w_in",), ("w_ret_o", "w_conv_o", "w_out"), ("g2", "u2", "d2"))


def _locate(groups):
    loc = {}
    for gi, keys in enumerate(groups):
        off = 0
        for k in keys:
            loc[k] = (gi, off)
            off += WIDTH[k]
    return loc


LOC = _locate(GROUPS)
LOC_W = _locate(GATHER_GROUPS)
GCOLS = [sum(WIDTH[k] for k in keys) for keys in GROUPS]
VMEM_LIMIT = 56 << 20


def _cp(*sem, **kw):
    return pltpu.CompilerParams(dimension_semantics=sem, vmem_limit_bytes=VMEM_LIMIT, **kw)


class _ProgramOrder:
    def __init__(self):
        self.active = False
        self.token = None


_ORDER = _ProgramOrder()


def _pcall(body, *, in_specs, scalar_prefetch=0, **kw):
    def call(*args):
        dep = _ORDER.token if _ORDER.active else None
        specs, fn = list(in_specs), body
        if dep is not None:
            n = len(args)

            def fn(*refs):
                return body(*refs[:n], *refs[n + 1:])

            specs.append(pl.BlockSpec(memory_space=pl.ANY))
            args = (*args, dep)
        params = dict(kw)
        if scalar_prefetch:
            params["grid_spec"] = pltpu.PrefetchScalarGridSpec(
                num_scalar_prefetch=scalar_prefetch, grid=params.pop("grid"), in_specs=specs,
                out_specs=params.pop("out_specs"))
        else:
            params["in_specs"] = specs
        out = pl.pallas_call(fn, **params)(*args)
        if _ORDER.active:
            _ORDER.token = jax.tree.leaves(out)[-1]
        return out

    return call


def _resident(shape, col_block):
    lead = (0,) * (len(shape) - 1)
    return pl.BlockSpec(shape, lambda *_: (*lead, col_block), pipeline_mode=pl.Buffered(1))


def _sig(x):
    return 1.0 / (1.0 + jnp.exp(-x))


def _dot(a, b):
    return jnp.dot(a, b, preferred_element_type=F32)


def _dot_nt(a, b):
    return lax.dot_general(a, b, (((1,), (1,)), ((), ())), preferred_element_type=F32)


def _ln_fwd(z, g, b):
    mu = jnp.mean(z, axis=-1, keepdims=True)
    xc = z - mu
    var = jnp.mean(xc * xc, axis=-1, keepdims=True)
    rstd = lax.rsqrt(var + EPS)
    xh = xc * rstd
    return xh * g + b, xh, rstd


def _ln_bwd(dy, xh, rstd, g):
    dxh = dy * g
    m1 = jnp.mean(dxh, axis=-1, keepdims=True)
    m2 = jnp.mean(dxh * xh, axis=-1, keepdims=True)
    return rstd * (dxh - m1 - xh * m2)


def _colsum(x):
    return jnp.sum(x, axis=0, keepdims=True)


def _zero_first(first, *refs):
    @pl.when(first)
    def _():
        for ref in refs:
            ref[...] = jnp.zeros_like(ref)


def _rope_tables(T):
    half = DK // 2
    freqs = ROPE_BASE ** (-np.arange(half, dtype=np.float32) / half)
    ang = (np.arange(T, dtype=np.float32)[:, None] * freqs[None, :]).astype(np.float32)
    cos, sin = np.cos(ang), np.sin(ang)
    return (jnp.asarray(np.concatenate([cos, cos], 1), F32),
            jnp.asarray(np.concatenate([-sin, sin], 1), F32))


def _decay_tables():
    h = np.arange(H, dtype=np.float64)
    log_g = np.log(1.0 - np.exp2(-5.0 - h))
    idx = np.arange(CH, dtype=np.float64)
    diff = idx[:, None] - idx[None, :]
    dm = np.where(diff[None] >= 0, np.exp(np.maximum(diff, 0.0)[None] * log_g[:, None, None]), 0.0)
    xi = np.exp((idx[None, :] + 1.0) * log_g[:, None])
    zeta = np.exp((CH - 1.0 - idx)[None, :] * log_g[:, None])
    cd = np.exp(CH * log_g)
    xi_t = np.broadcast_to(xi[:, :, None], (H, CH, DV))
    zeta_t = np.broadcast_to(zeta[:, :, None], (H, CH, DK))
    return (jnp.asarray(dm, F32), jnp.asarray(xi_t, F32), jnp.asarray(zeta_t, F32),
            [float(v) for v in cd])


def _cast_t(x):
    T = x.shape[0]
    tm = min(T, 512)

    def body(x_ref, xb_ref, xt_ref):
        v = x_ref[...]
        xb_ref[...] = v.astype(BF16)
        xt_ref[...] = v.T.astype(BF16)

    return _pcall(
        body, name="cast_t", grid=(T // tm,),
        in_specs=[pl.BlockSpec((tm, D), lambda i: (i, 0))],
        out_specs=[pl.BlockSpec((tm, D), lambda i: (i, 0)), pl.BlockSpec((D, tm), lambda i: (0, i))],
        out_shape=[jax.ShapeDtypeStruct((T, D), BF16), jax.ShapeDtypeStruct((D, T), BF16)],
        compiler_params=_cp("parallel"))(x)


def _ffn_up(xb, wall, og, ou, name):
    T = xb.shape[0]
    tm = min(T, 512)
    assert ou == og + FSP

    def body(x_ref, w_ref, a_ref, b_ref, h_ref):
        x = x_ref[...]
        for s in range(NCHIP):
            cols = slice(s * FSP, (s + 1) * FSP)
            a = _dot(x, w_ref[s, :, 0:FSP])
            b = _dot(x, w_ref[s, :, FSP:2 * FSP])
            a_ref[:, cols] = a.astype(BF16)
            b_ref[:, cols] = b.astype(BF16)
            h_ref[:, cols] = (a * _sig(a) * b).astype(BF16)

    ospec = pl.BlockSpec((tm, FP), lambda i: (i, 0))
    return _pcall(
        body, name=name, grid=(T // tm,),
        in_specs=[pl.BlockSpec((tm, D), lambda i: (i, 0)), _resident((NCHIP, D, 2 * FSP), og // (2 * FSP))],
        out_specs=[ospec] * 3, out_shape=[jax.ShapeDtypeStruct((T, FP), BF16)] * 3,
        compiler_params=_cp("parallel"))(xb, wall)


def _proj_ln(hb, wall, off, res, g, b, coef, name, want_b=True):
    T, K = hb.shape
    ks = K // NCHIP
    tm = min(T, 512)
    sub = min(tm, 256)

    def body(h_ref, w_ref, r_ref, g_ref, b_ref, z_ref, *rest):
        for r0 in range(0, tm, sub):
            r = slice(r0, r0 + sub)
            acc = _dot_nt(h_ref[r, 0:ks], w_ref[0])
            for s in range(1, NCHIP):
                acc += _dot_nt(h_ref[r, s * ks:(s + 1) * ks], w_ref[s])
            z = ALPHA * r_ref[r, :] + coef * acc
            z_ref[r, :] = z
            if want_b:
                y, _, _ = _ln_fwd(z, g_ref[...], b_ref[...])
                y_ref, yb_ref, yt_ref = rest
                y_ref[r, :] = y
                yb_ref[r, :] = y.astype(BF16)
                yt_ref[:, r] = y.T.astype(BF16)

    row = pl.BlockSpec((tm, D), lambda i: (i, 0))
    vec = pl.BlockSpec((1, D), lambda i: (0, 0))
    out_specs = [row]
    out_shape = [jax.ShapeDtypeStruct((T, D), F32)]
    if want_b:
        out_specs += [row, row, pl.BlockSpec((D, tm), lambda i: (0, i))]
        out_shape += [jax.ShapeDtypeStruct((T, D), F32), jax.ShapeDtypeStruct((T, D), BF16),
                      jax.ShapeDtypeStruct((D, T), BF16)]
    return _pcall(
        body, name=name, grid=(T // tm,),
        in_specs=[pl.BlockSpec((tm, K), lambda i: (i, 0)),
                  _resident((NCHIP, D, ks), off // ks), row, vec, vec],
        out_specs=out_specs, out_shape=out_shape,
        compiler_params=_cp("parallel"))(hb, wall, res, g, b)


def _inproj(xb, wall, off, b_in, cos_t, sin_t):
    T = xb.shape[0]
    tm, tn = min(T, 512), 512
    assert off == 0

    def body(x_ref, w_ref, bias_ref, cos_ref, sin_ref, o_ref):
        x = x_ref[...]
        c = cos_ref[...]
        s = sin_ref[...]
        for n0 in range(0, INW, tn):
            chip, c0 = divmod(n0, INS)
            acc = _dot(x, w_ref[chip, :, c0:c0 + tn]) + bias_ref[:, n0:n0 + tn]
            if n0 >= 2 * D:
                o_ref[:, n0:n0 + tn] = acc.astype(BF16)
                continue
            scale = DK ** -0.5 if n0 < D else 1.0
            for hh in range(tn // DK):
                xh = acc[:, hh * DK:(hh + 1) * DK]
                o = (xh * c + pltpu.roll(xh, DK // 2, 1) * s) * scale
                o_ref[:, n0 + hh * DK:n0 + (hh + 1) * DK] = o.astype(BF16)

    return _pcall(
        body, name="inproj", grid=(T // tm,),
        in_specs=[pl.BlockSpec((tm, D), lambda i: (i, 0)),
                  _resident((NCHIP, D, INS), 0),
                  pl.BlockSpec((1, INW), lambda i: (0, 0)),
                  pl.BlockSpec((tm, DK), lambda i: (i, 0)),
                  pl.BlockSpec((tm, DK), lambda i: (i, 0))],
        out_specs=pl.BlockSpec((tm, INW), lambda i: (i, 0)),
        out_shape=jax.ShapeDtypeStruct((T, INW), BF16),
        compiler_params=_cp("parallel"))(xb, wall, b_in, cos_t, sin_t)


RET_CPS = 2


def _retention_fwd(proj, gn_g, dm_t, xi_t, zeta_t, cds):
    T = proj.shape[0]
    n = T // CH
    tr = RET_CPS * CH

    def body(q_ref, k_ref, v_ref, g_ref, gn_ref, dm_ref, xi_ref, zt_ref, r_ref, ri_ref, st_ref, state):
        @pl.when(pl.program_id(0) == 0)
        def _():
            state[...] = jnp.zeros_like(state)

        for h in range(H):
            rows = slice(h * DK, (h + 1) * DK)
            cols = slice(h * DV, (h + 1) * DV)
            s_prev = state[rows, :]
            for j in range(RET_CPS):
                t = slice(j * CH, (j + 1) * CH)
                q = q_ref[t, h * DK:(h + 1) * DK]
                k = k_ref[t, h * DK:(h + 1) * DK]
                v = v_ref[t, cols]
                s_b = s_prev.astype(BF16)
                st_ref[j, rows, :] = s_b
                sc = _dot_nt(q, k) * dm_ref[h]
                r = _dot(sc.astype(BF16), v) + _dot(q, s_b) * xi_ref[h]
                kz = k.astype(F32) * zt_ref[h]
                s_prev = cds[h] * s_prev + _dot(kz.T.astype(BF16), v)
                r_ref[t, cols] = r
                mu = jnp.mean(r, axis=-1, keepdims=True)
                xc = r - mu
                var = jnp.mean(xc * xc, axis=-1, keepdims=True)
                y = xc * lax.rsqrt(var + EPS) * gn_ref[:, cols]
                g = g_ref[t, cols].astype(F32)
                ri_ref[t, cols] = (g * _sig(g) * y).astype(BF16)
            state[rows, :] = s_prev

    full3 = lambda shp: pl.BlockSpec(shp, lambda c: (0, 0, 0))
    return _pcall(
        body, name="retention_fwd", grid=(n // RET_CPS,),
        in_specs=[pl.BlockSpec((tr, D), lambda c: (c, 0)),
                  pl.BlockSpec((tr, D), lambda c: (c, 1)),
                  pl.BlockSpec((tr, VW), lambda c: (c, 1)),
                  pl.BlockSpec((tr, VW), lambda c: (c, 2)),
                  pl.BlockSpec((1, VW), lambda c: (0, 0)),
                  full3((H, CH, CH)), full3((H, CH, DV)), full3((H, CH, DK))],
        out_specs=[pl.BlockSpec((tr, VW), lambda c: (c, 0)), pl.BlockSpec((tr, VW), lambda c: (c, 0)),
                   pl.BlockSpec((RET_CPS, H * DK, DV), lambda c: (c, 0, 0))],
        out_shape=[jax.ShapeDtypeStruct((T, VW), F32), jax.ShapeDtypeStruct((T, VW), BF16),
                   jax.ShapeDtypeStruct((n, H * DK, DV), BF16)],
        scratch_shapes=[pltpu.VMEM((H * DK, DV), F32)],
        compiler_params=_cp("arbitrary"))(proj, proj, proj, proj, gn_g, dm_t, xi_t, zeta_t)


CONV_TT = 256
CONV_SB = 64
CONV_CB = 256


SUB = 8
CONV_PAD = 8


def _glu(a_ref, b_ref, rows=slice(None)):
    a = a_ref[rows, :].astype(F32)
    sb = _sig(b_ref[rows, :].astype(F32))
    return a, sb


def _shift_copies(win, sh, rows):
    win[rows:rows + CONV_PAD, :] = jnp.zeros((CONV_PAD, D), F32)
    for b in range(1, SUB):
        sh[b - 1, :, :] = win[b:b + rows, :]


def _tap(win, sh, start, size, cs):
    a, b = divmod(start, SUB)
    src = win if b == 0 else sh.at[b - 1]
    return src[SUB * a:SUB * a + size, cs]


def _conv_fwd(proj, conv_k, conv_b, ln_g, ln_b):
    T = proj.shape[0]
    tt = min(T, CONV_TT)
    ca, cb = 6 * D // D, 7 * D // D

    def body(a_ref, b_ref, pa_ref, pb_ref, k_ref, cb_ref, g_ref, bb_ref, u1_ref, u3_ref, win, sh):
        i = pl.program_id(0)
        a, sb = _glu(a_ref, b_ref)
        win[HALO:tt + HALO, :] = a * sb
        pa, psb = _glu(pa_ref, pb_ref, slice(tt - HALO, tt))
        win[0:HALO, :] = jnp.where(i > 0, pa * psb, 0.0)
        _shift_copies(win, sh, tt + HALO)
        for c0 in range(0, D, CONV_CB):
            cs = slice(c0, c0 + CONV_CB)
            for r0 in range(0, tt, CONV_SB):
                acc = jnp.zeros((CONV_SB, CONV_CB), F32)
                for w in range(CONV_W):
                    st = r0 + HALO - (CONV_W - 1) + w
                    acc += _tap(win, sh, st, CONV_SB, cs) * k_ref[w:w + 1, cs]
                u1_ref[r0:r0 + CONV_SB, cs] = acc + cb_ref[:, cs]
        u2, _, _ = _ln_fwd(u1_ref[...], g_ref[...], bb_ref[...])
        u3_ref[...] = (u2 * _sig(u2)).astype(BF16)

    vec = pl.BlockSpec((1, D), lambda i: (0, 0))
    row = pl.BlockSpec((tt, D), lambda i: (i, 0))
    return _pcall(
        body, name="conv_fwd", grid=(T // tt,),
        in_specs=[pl.BlockSpec((tt, D), lambda i: (i, ca)), pl.BlockSpec((tt, D), lambda i: (i, cb)),
                  pl.BlockSpec((tt, D), lambda i: (jnp.maximum(i - 1, 0), ca)),
                  pl.BlockSpec((tt, D), lambda i: (jnp.maximum(i - 1, 0), cb)),
                  pl.BlockSpec((CONV_W, D), lambda i: (0, 0)), vec, vec, vec],
        out_specs=[row, row],
        out_shape=[jax.ShapeDtypeStruct((T, D), F32), jax.ShapeDtypeStruct((T, D), BF16)],
        scratch_shapes=[pltpu.VMEM((tt + HALO + CONV_PAD, D), F32), pltpu.VMEM((SUB - 1, tt + HALO, D), F32)],
        compiler_params=_cp("parallel"))(proj, proj, proj, proj, conv_k, conv_b, ln_g, ln_b)


def _merge(ret_in, u3, proj, wall, off_r, off_c):
    T = ret_in.shape[0]
    tm = min(T, 512)
    kr, kc = VW // NCHIP, D // NCHIP

    def body(ri_ref, u3_ref, gr_ref, gc_ref, wr_ref, wc_ref, ro_ref, co_ref, m_ref):
        ro = _dot_nt(ri_ref[:, 0:kr], wr_ref[0])
        co = _dot_nt(u3_ref[:, 0:kc], wc_ref[0])
        for s in range(1, NCHIP):
            ro += _dot_nt(ri_ref[:, s * kr:(s + 1) * kr], wr_ref[s])
            co += _dot_nt(u3_ref[:, s * kc:(s + 1) * kc], wc_ref[s])
        ro_ref[...] = ro.astype(BF16)
        co_ref[...] = co.astype(BF16)
        m = _sig(gr_ref[...].astype(F32)) * ro + _sig(gc_ref[...].astype(F32)) * co
        m_ref[...] = m.astype(BF16)

    row = pl.BlockSpec((tm, D), lambda i: (i, 0))
    return _pcall(
        body, name="merge", grid=(T // tm,),
        in_specs=[pl.BlockSpec((tm, VW), lambda i: (i, 0)), row,
                  pl.BlockSpec((tm, D), lambda i: (i, 8)), pl.BlockSpec((tm, D), lambda i: (i, 9)),
                  pl.BlockSpec((NCHIP, D, kr), lambda i: (0, 0, off_r // kr)),
                  pl.BlockSpec((NCHIP, D, kc), lambda i: (0, 0, off_c // kc))],
        out_specs=[row] * 3, out_shape=[jax.ShapeDtypeStruct((T, D), BF16)] * 3,
        compiler_params=_cp("parallel"))(ret_in, u3, proj, proj, wall, wall)


def _loss_ln_bwd(z, g, b, target, coef):
    T = z.shape[0]
    tm = min(T, 256)
    nt = T // tm

    def body(z_ref, g_ref, b_ref, t_ref, loss_ref, dzb_ref, dzt_ref, dz_ref, dg_ref, db_ref, lacc):
        i = pl.program_id(0)
        _zero_first(i == 0, lacc, dg_ref, db_ref)
        gam = g_ref[...]
        y, xh, rstd = _ln_fwd(z_ref[...], gam, b_ref[...])
        e = y - t_ref[...]
        lacc[...] += _colsum(e * e)
        dy = e * (1.0 / D)
        dz = _ln_bwd(dy, xh, rstd, gam)
        dz_ref[...] = dz
        dzc = coef * dz
        dzb_ref[...] = dzc.astype(BF16)
        dzt_ref[...] = dzc.T.astype(BF16)
        dg_ref[...] += _colsum(dy * xh)
        db_ref[...] += _colsum(dy)

        @pl.when(i == nt - 1)
        def _():
            loss_ref[...] = (0.5 / D) * jnp.sum(lacc[...], axis=1, keepdims=True)

    row = pl.BlockSpec((tm, D), lambda i: (i, 0))
    vec = pl.BlockSpec((1, D), lambda i: (0, 0))
    return _pcall(
        body, name="loss_ln_bwd", grid=(nt,),
        in_specs=[row, vec, vec, row],
        out_specs=[pl.BlockSpec((1, 1), lambda i: (0, 0)), row, pl.BlockSpec((D, tm), lambda i: (0, i)),
                   row, vec, vec],
        out_shape=[jax.ShapeDtypeStruct((1, 1), F32), jax.ShapeDtypeStruct((T, D), BF16),
                   jax.ShapeDtypeStruct((D, T), BF16), jax.ShapeDtypeStruct((T, D), F32),
                   jax.ShapeDtypeStruct((1, D), F32), jax.ShapeDtypeStruct((1, D), F32)],
        scratch_shapes=[pltpu.VMEM((1, D), F32)],
        compiler_params=_cp("arbitrary"))(z, g, b, target)


def _ffn_bwd_h(dfb, wall, od, a, b, name):
    T = dfb.shape[0]
    tm = min(T, 512)

    def body(d_ref, w_ref, a_ref, b_ref, da_ref, db_ref):
        d = d_ref[...]
        for s in range(NCHIP):
            cols = slice(s * FSP, (s + 1) * FSP)
            dh = _dot(d, w_ref[s])
            a = a_ref[:, cols].astype(F32)
            sg = _sig(a)
            da_ref[:, cols] = (dh * b_ref[:, cols].astype(F32) * (sg * (1.0 + a * (1.0 - sg)))).astype(BF16)
            db_ref[:, cols] = (dh * a * sg).astype(BF16)

    ospec = pl.BlockSpec((tm, FP), lambda i: (i, 0))
    return _pcall(
        body, name=name, grid=(T // tm,),
        in_specs=[pl.BlockSpec((tm, D), lambda i: (i, 0)), _resident((NCHIP, D, FSP), od // FSP), ospec, ospec],
        out_specs=[ospec] * 2, out_shape=[jax.ShapeDtypeStruct((T, FP), BF16)] * 2,
        compiler_params=_cp("parallel"))(dfb, wall, a, b)


DX_SUB = 256


def _dx_partial(lhs, wall, chips, name):
    T, K = lhs.shape
    ks = K // NCHIP
    nc = len(chips)
    assert list(chips) == list(range(chips[0], chips[0] + nc)) and chips[0] % nc == 0
    tm = min(T, 512)

    def body(l_ref, w_ref, o_ref):
        for r0 in range(0, tm, DX_SUB):
            r = slice(r0, r0 + DX_SUB)
            acc = _dot_nt(l_ref[r, 0:ks], w_ref[0])
            for s in range(1, nc):
                acc += _dot_nt(l_ref[r, s * ks:(s + 1) * ks], w_ref[s])
            o_ref[r, :] = acc

    blk = chips[0] // nc
    return _pcall(
        body, name=name, grid=(T // tm,),
        in_specs=[pl.BlockSpec((tm, nc * ks), lambda i: (i, blk)),
                  pl.BlockSpec((nc, D, ks), lambda i: (blk, 0, 0), pipeline_mode=pl.Buffered(1))],
        out_specs=pl.BlockSpec((tm, D), lambda i: (i, 0)),
        out_shape=jax.ShapeDtypeStruct((T, D), F32),
        compiler_params=_cp("parallel"))(lhs, wall)


def _dx_bwd(lhs, offs, wall, dz_next, name, ln=None, chips=tuple(range(NCHIP)), partial=None):
    T, K = lhs[0].shape
    ks = K // NCHIP
    nl = len(lhs)
    nc = len(chips)
    assert list(offs) == [l * ks for l in range(nl)]
    assert list(chips) == list(range(chips[0], chips[0] + nc)) and chips[0] % nc == 0
    blk = chips[0] // nc
    tm = min(T, 512)

    def body(*refs):
        l_refs = refs[:nl]
        w_ref = refs[nl]
        dzn_ref = refs[nl + 1]
        pos = nl + 2
        if partial is not None:
            p_ref = refs[pos]
            pos += 1
        if ln is not None:
            z_ref, g_ref = refs[pos:pos + 2]
            pos += 2
        outs = refs[pos:]
        sums = list(outs[3:]) if ln is not None else list(outs[1:])

        _zero_first(pl.program_id(0) == 0, *sums)

        for r0 in range(0, tm, DX_SUB):
            r = slice(r0, r0 + DX_SUB)
            acc = None if partial is None else p_ref[r, :]
            for s in range(nc):
                rows = slice(s * ks, (s + 1) * ks)
                for l in range(nl):
                    part = _dot_nt(l_refs[l][r, rows], w_ref[s, :, l * ks:(l + 1) * ks])
                    acc = part if acc is None else acc + part
            dy = acc + ALPHA * dzn_ref[r, :]
            if ln is None:
                outs[0][r, :] = dy
            else:
                gam = g_ref[...]
                _, xh, rstd = _ln_fwd(z_ref[r, :], gam, 0.0)
                dz = _ln_bwd(dy, xh, rstd, gam)
                dzc = ln[2] * dz
                outs[0][r, :] = dzc.astype(BF16)
                outs[1][:, r] = dzc.T.astype(BF16)
                outs[2][r, :] = dz
                outs[3][...] += _colsum(dy * xh)
                outs[4][...] += _colsum(dy)

    row = pl.BlockSpec((tm, D), lambda i: (i, 0))
    vec = pl.BlockSpec((1, D), lambda i: (0, 0))
    in_specs = [pl.BlockSpec((tm, nc * ks), lambda i: (i, blk))] * nl
    in_specs += [pl.BlockSpec((nc, D, nl * ks), lambda i: (blk, 0, 0), pipeline_mode=pl.Buffered(1)), row]
    args = list(lhs) + [wall, dz_next]
    if partial is not None:
        in_specs.append(row)
        args.append(partial)
    if ln is None:
        out_specs = [row]
        out_shape = [jax.ShapeDtypeStruct((T, D), F32)]
    else:
        in_specs += [row, vec]
        args += [ln[0], ln[1]]
        out_specs = [row, pl.BlockSpec((D, tm), lambda i: (0, i)), row, vec, vec]
        out_shape = [jax.ShapeDtypeStruct((T, D), BF16), jax.ShapeDtypeStruct((D, T), BF16),
                     jax.ShapeDtypeStruct((T, D), F32), jax.ShapeDtypeStruct((1, D), F32),
                     jax.ShapeDtypeStruct((1, D), F32)]
    return _pcall(
        body, name=name, grid=(T // tm,), in_specs=in_specs, out_specs=out_specs, out_shape=out_shape,
        compiler_params=_cp("arbitrary"))(*args)


def _wgrad(lhs_t, rhs, key, name, g_all=None, colsum=False):
    T, N = rhs.shape
    tn = next(c for c in (768, 512, 256) if (N // NCHIP) % c == 0 and LOC[key][1] % c == 0)
    nps = N // NCHIP // tn
    off = LOC[key][1]
    cols = GCOLS[LOC[key][0]]

    def body(*refs):
        l_ref, r_ref = refs[0], refs[1]
        o_ref, t_ref = refs[-2 - colsum], refs[-1]
        o_ref[...] = _dot(l_ref[...], r_ref[...]).astype(BF16)
        if colsum:
            refs[-2][...] = _colsum(r_ref[...].astype(F32))
        t_ref[...] = jnp.zeros_like(t_ref)

    in_specs = [_resident((D, T), 0), pl.BlockSpec((T, tn), lambda j: (0, j))]
    args = [lhs_t, rhs]
    aliases = {}
    if g_all is not None:
        in_specs.append(pl.BlockSpec(memory_space=pl.ANY))
        args.append(g_all)
        aliases = {2: 0}
    out_specs = [pl.BlockSpec((None, D, tn), lambda j: (j // nps, 0, off // tn + j % nps))]
    out_shape = [jax.ShapeDtypeStruct((NCHIP, D, cols), BF16)]
    if colsum:
        out_specs.append(pl.BlockSpec((1, tn), lambda j: (0, j)))
        out_shape.append(jax.ShapeDtypeStruct((1, N), F32))
    out_specs.append(pl.BlockSpec((8, 128), lambda j: (0, 0)))
    out_shape.append(jax.ShapeDtypeStruct((8, 128), F32))
    res = _pcall(
        body, name=name, grid=(N // tn,), in_specs=in_specs, out_specs=out_specs, out_shape=out_shape,
        input_output_aliases=aliases,
        compiler_params=_cp("arbitrary"))(*args)
    return (res[0], res[1]) if colsum else res[0]


def _merge_bwd(dmb, wall, off, proj, ro, co):
    T = dmb.shape[0]
    tm = min(T, 512)
    ks = D // NCHIP

    def body(d_ref, w_ref, gr_ref, gc_ref, ro_ref, co_ref, dro_ref, drot_ref, dco_ref, dcot_ref, dp_ref):
        d = d_ref[...]
        dmg = jnp.concatenate([_dot(d, w_ref[s]) for s in range(NCHIP)], axis=1)
        sr = _sig(gr_ref[...].astype(F32))
        sc = _sig(gc_ref[...].astype(F32))
        dro = dmg * sr
        dco = dmg * sc
        dro_ref[...] = dro.astype(BF16)
        drot_ref[...] = dro.T.astype(BF16)
        dco_ref[...] = dco.astype(BF16)
        dcot_ref[...] = dco.T.astype(BF16)
        dp_ref[:, 0:D] = (dmg * ro_ref[...].astype(F32) * sr * (1.0 - sr)).astype(BF16)
        dp_ref[:, D:2 * D] = (dmg * co_ref[...].astype(F32) * sc * (1.0 - sc)).astype(BF16)

    row = pl.BlockSpec((tm, D), lambda i: (i, 0))
    col = pl.BlockSpec((D, tm), lambda i: (0, i))
    return _pcall(
        body, name="merge_bwd", grid=(T // tm,),
        in_specs=[row, pl.BlockSpec((NCHIP, D, ks), lambda i: (0, 0, off // ks)),
                  pl.BlockSpec((tm, D), lambda i: (i, 8)), pl.BlockSpec((tm, D), lambda i: (i, 9)), row, row],
        out_specs=[row, col, row, col, pl.BlockSpec((tm, 2 * D), lambda i: (i, 4))],
        out_shape=[jax.ShapeDtypeStruct((T, D), BF16), jax.ShapeDtypeStruct((D, T), BF16),
                   jax.ShapeDtypeStruct((T, D), BF16), jax.ShapeDtypeStruct((D, T), BF16),
                   jax.ShapeDtypeStruct((T, INW), BF16)],
        compiler_params=_cp("parallel"))(dmb, wall, proj, proj, ro, co)


def _reto_bwd(dro, wall, off, r, proj, gn_g, dproj):
    T = dro.shape[0]
    tm = min(T, 512)
    hps = H // NCHIP

    def body(d_ref, w_ref, r_ref, g_ref, gn_ref, _, dr_ref, dgn_ref, dp_ref):
        _zero_first(pl.program_id(1) == 0, dgn_ref)
        dri = _dot(d_ref[...], w_ref[...])
        rr = r_ref[...]
        mu = jnp.mean(rr, axis=-1, keepdims=True)
        xc = rr - mu
        var = jnp.mean(xc * xc, axis=-1, keepdims=True)
        rstd = lax.rsqrt(var + EPS)
        rn = xc * rstd
        gn = gn_ref[...]
        g = g_ref[...].astype(F32)
        sg = _sig(g)
        dy = dri * (g * sg)
        dp_ref[...] = (dri * (rn * gn) * (sg * (1.0 + g * (1.0 - sg)))).astype(BF16)
        dgn_ref[...] += _colsum(dy * rn)
        dr_ref[...] = _ln_bwd(dy, rn, rstd, gn).astype(BF16)

    return _pcall(
        body, name="reto_bwd", grid=(H, T // tm),
        in_specs=[pl.BlockSpec((tm, D), lambda j, i: (i, 0)),
                  pl.BlockSpec((None, D, DV), lambda j, i: (j // hps, 0, off // DV + j % hps)),
                  pl.BlockSpec((tm, DV), lambda j, i: (i, j)),
                  pl.BlockSpec((tm, DV), lambda j, i: (i, 2 * VW // DV + j)),
                  pl.BlockSpec((1, DV), lambda j, i: (0, j)),
                  pl.BlockSpec(memory_space=pl.ANY)],
        out_specs=[pl.BlockSpec((tm, DV), lambda j, i: (i, j)), pl.BlockSpec((1, DV), lambda j, i: (0, j)),
                   pl.BlockSpec((tm, DV), lambda j, i: (i, 2 * VW // DV + j))],
        out_shape=[jax.ShapeDtypeStruct((T, VW), BF16), jax.ShapeDtypeStruct((1, VW), F32),
                   jax.ShapeDtypeStruct((T, INW), BF16)],
        input_output_aliases={5: 2},
        compiler_params=_cp("arbitrary", "arbitrary"))(dro, wall, r, proj, gn_g, dproj)


def _retention_bwd(proj, dr, states, cos_t, sin_t, dm_t, xi_t, zeta_t, cds, dproj):
    T = proj.shape[0]
    n = T // CH // RET_CPS
    tr = RET_CPS * CH
    scale = DK ** -0.5

    def body(q_ref, k_ref, v_ref, dr_ref, st_ref, cos_ref, sin_ref, dm_ref, xi_ref, zt_ref, _, dp_ref, ds):
        @pl.when(pl.program_id(0) == 0)
        def _():
            ds[...] = jnp.zeros_like(ds)

        def unrope(d, t):
            return d * cos_ref[t, :] + pltpu.roll(d * sin_ref[t, :], DK // 2, 1)

        for h in range(H):
            rows = slice(h * DK, (h + 1) * DK)
            dm = dm_ref[h]
            zt = zt_ref[h]
            ds_prev = ds[rows, :]
            for j in reversed(range(RET_CPS)):
                t = slice(j * CH, (j + 1) * CH)
                q = q_ref[t, h * DK:(h + 1) * DK]
                k = k_ref[t, h * DK:(h + 1) * DK]
                v = v_ref[t, h * DV:(h + 1) * DV]
                d_r = dr_ref[t, h * DV:(h + 1) * DV]
                s_b = st_ref[j, rows, :]
                sc = _dot_nt(q, k) * dm
                dsc = _dot_nt(d_r, v) * dm
                drx = (d_r.astype(F32) * xi_ref[h]).astype(BF16)
                ds_b = ds_prev.astype(BF16)
                kz = (k.astype(F32) * zt).astype(BF16)
                dq = _dot(dsc.astype(BF16), k) + _dot_nt(drx, s_b)
                dk = _dot(dsc.T.astype(BF16), q) + _dot_nt(v, ds_b) * zt
                dv = _dot(sc.T.astype(BF16), d_r) + _dot(kz, ds_b)
                ds_prev = cds[h] * ds_prev + _dot(q.astype(F32).T.astype(BF16), drx)
                dp_ref[t, h * DK:(h + 1) * DK] = unrope(dq * scale, t).astype(BF16)
                dp_ref[t, D + h * DK:D + (h + 1) * DK] = unrope(dk, t).astype(BF16)
                dp_ref[t, 2 * D + h * DV:2 * D + (h + 1) * DV] = dv.astype(BF16)
            ds[rows, :] = ds_prev

    rv = lambda c: n - 1 - c
    full3 = lambda shp: pl.BlockSpec(shp, lambda c: (0, 0, 0))
    return _pcall(
        body, name="retention_bwd", grid=(n,),
        in_specs=[pl.BlockSpec((tr, D), lambda c: (rv(c), 0)),
                  pl.BlockSpec((tr, D), lambda c: (rv(c), 1)),
                  pl.BlockSpec((tr, VW), lambda c: (rv(c), 1)),
                  pl.BlockSpec((tr, VW), lambda c: (rv(c), 0)),
                  pl.BlockSpec((RET_CPS, H * DK, DV), lambda c: (rv(c), 0, 0)),
                  pl.BlockSpec((tr, DK), lambda c: (rv(c), 0)),
                  pl.BlockSpec((tr, DK), lambda c: (rv(c), 0)),
                  full3((H, CH, CH)), full3((H, CH, DV)), full3((H, CH, DK)),
                  pl.BlockSpec(memory_space=pl.ANY)],
        out_specs=pl.BlockSpec((tr, 2 * D + VW), lambda c: (rv(c), 0)),
        out_shape=jax.ShapeDtypeStruct((T, INW), BF16),
        input_output_aliases={10: 0},
        scratch_shapes=[pltpu.VMEM((H * DK, DV), F32)],
        compiler_params=_cp("arbitrary"))(proj, proj, proj, dr, states, cos_t, sin_t, dm_t, xi_t, zeta_t, dproj)


def _convo_bwd(dco, wall, off, u1, ln_g, ln_b):
    T = dco.shape[0]
    tm = min(T, 512)
    ks = D // NCHIP

    def body(d_ref, w_ref, u1_ref, g_ref, b_ref, du1_ref, dg_ref, db_ref, dcb_ref):
        _zero_first(pl.program_id(0) == 0, dg_ref, db_ref, dcb_ref)
        d = d_ref[...]
        du3 = jnp.concatenate([_dot(d, w_ref[s]) for s in range(NCHIP)], axis=1)
        gam = g_ref[...]
        u2, xh, rstd = _ln_fwd(u1_ref[...], gam, b_ref[...])
        sg = _sig(u2)
        du2 = du3 * (sg * (1.0 + u2 * (1.0 - sg)))
        du1 = _ln_bwd(du2, xh, rstd, gam)
        du1_ref[...] = du1
        dg_ref[...] += _colsum(du2 * xh)
        db_ref[...] += _colsum(du2)
        dcb_ref[...] += _colsum(du1)

    row = pl.BlockSpec((tm, D), lambda i: (i, 0))
    vec = pl.BlockSpec((1, D), lambda i: (0, 0))
    return _pcall(
        body, name="convo_bwd", grid=(T // tm,),
        in_specs=[row, pl.BlockSpec((NCHIP, D, ks), lambda i: (0, 0, off // ks)), row, vec, vec],
        out_specs=[row, vec, vec, vec],
        out_shape=[jax.ShapeDtypeStruct((T, D), F32)] + [jax.ShapeDtypeStruct((1, D), F32)] * 3,
        compiler_params=_cp("arbitrary"))(dco, wall, u1, ln_g, ln_b)


def _conv_bwd(du1, proj, conv_k, dproj):
    T = du1.shape[0]
    tt = min(T, CONV_TT)
    nt = T // tt
    ca, cb = 6, 7

    def body(d_ref, dn_ref, a_ref, b_ref, pa_ref, pb_ref, k_ref, _, dp_ref, dk_ref, win_u, win_d, sh_u, sh_d):
        i = pl.program_id(0)
        a, sb = _glu(a_ref, b_ref)
        win_u[HALO:tt + HALO, :] = a * sb
        pa, psb = _glu(pa_ref, pb_ref, slice(tt - HALO, tt))
        win_u[0:HALO, :] = jnp.where(i > 0, pa * psb, 0.0)
        win_d[0:tt, :] = d_ref[...]
        win_d[tt:tt + HALO, :] = jnp.where(i < nt - 1, dn_ref[0:HALO, :], 0.0)
        _shift_copies(win_u, sh_u, tt + HALO)
        _shift_copies(win_d, sh_d, tt + HALO)

        @pl.when(i == 0)
        def _():
            dk_ref[...] = jnp.zeros_like(dk_ref)

        for c0 in range(0, D, CONV_CB):
            cs = slice(c0, c0 + CONV_CB)
            for r0 in range(0, tt, CONV_SB):
                acc = jnp.zeros((CONV_SB, CONV_CB), F32)
                for w in range(CONV_W):
                    st = r0 + (CONV_W - 1) - w
                    acc += _tap(win_d, sh_d, st, CONV_SB, cs) * k_ref[w:w + 1, cs]
                aa = a_ref[r0:r0 + CONV_SB, cs].astype(F32)
                ss = _sig(b_ref[r0:r0 + CONV_SB, cs].astype(F32))
                dp_ref[r0:r0 + CONV_SB, cs] = (acc * ss).astype(BF16)
                dp_ref[r0:r0 + CONV_SB, c0 + D:c0 + D + CONV_CB] = (acc * aa * ss * (1.0 - ss)).astype(BF16)
        for c0 in range(0, D, CONV_CB):
            cs = slice(c0, c0 + CONV_CB)
            for w in range(CONV_W):
                acc = jnp.zeros((CONV_SB, CONV_CB), F32)
                for r0 in range(0, tt, CONV_SB):
                    st = r0 + HALO - (CONV_W - 1) + w
                    acc += win_d[r0:r0 + CONV_SB, cs] * _tap(win_u, sh_u, st, CONV_SB, cs)
                dk_ref[w:w + 1, cs] += _colsum(acc)

    blk = lambda f, c: pl.BlockSpec((tt, D), lambda i: (f(i), c))
    cur = lambda i: i
    prv = lambda i: jnp.maximum(i - 1, 0)
    nxt = lambda i: jnp.minimum(i + 1, nt - 1)
    return _pcall(
        body, name="conv_bwd", grid=(nt,),
        in_specs=[blk(cur, 0), blk(nxt, 0), blk(cur, ca), blk(cur, cb), blk(prv, ca), blk(prv, cb),
                  pl.BlockSpec((CONV_W, D), lambda i: (0, 0)), pl.BlockSpec(memory_space=pl.ANY)],
        out_specs=[pl.BlockSpec((tt, 2 * D), lambda i: (i, 3)), pl.BlockSpec((HALO, D), lambda i: (0, 0))],
        out_shape=[jax.ShapeDtypeStruct((T, INW), BF16), jax.ShapeDtypeStruct((HALO, D), F32)],
        input_output_aliases={7: 0},
        scratch_shapes=[pltpu.VMEM((tt + HALO + CONV_PAD, D), F32), pltpu.VMEM((tt + HALO + CONV_PAD, D), F32),
                        pltpu.VMEM((SUB - 1, tt + HALO, D), F32), pltpu.VMEM((SUB - 1, tt + HALO, D), F32)],
        compiler_params=_cp("arbitrary"))(du1, du1, proj, proj, proj, proj, conv_k, dproj)


def _local_step(x, target, gathered, slabs, sp, kvec, pos_c, pos_sc, adam):
    T = x.shape[0]
    cos_t, sin_t = _rope_tables(T)
    dm_t, xi_t, zeta_t, cds = _decay_tables()
    wts = [_with_own_slab(g, s) for g, s in zip(gathered[:-1], slabs[:-1])] + [None]
    wa = lambda key: wts[LOC_W[key][0]]
    wo = lambda key: LOC_W[key][1]
    _ORDER.active, _ORDER.token = True, None

    xb, xt = _cast_t(x)
    a1, b1, h1 = _ffn_up(xb, wa("g1"), wo("g1"), wo("u1"), "ffn1_up")
    z1, x1, x1b, x1t = _proj_ln(h1, wa("d1"), wo("d1"), x, sp["ln1_g"], sp["ln1_b"], 0.5, "ffn1_down_ln")
    proj = _inproj(x1b, wa("w_in"), wo("w_in"), sp["b_in"], cos_t, sin_t)
    r, ret_in, states = _retention_fwd(proj, sp["ret_gn_g"], dm_t, xi_t, zeta_t, cds)
    kall = _sum_devices(_all_gather_small(kvec, "gather_conv_k"), "sum_conv_k")
    sp = dict(sp, conv_k=kall.reshape(-1)[:CONV_W * D].reshape(CONV_W, D))
    u1, u3 = _conv_fwd(proj, sp["conv_k"], sp["conv_b"], sp["conv_ln_g"], sp["conv_ln_b"])
    ro, co, merged = _merge(ret_in, u3, proj, wa("w_ret_o"), wo("w_ret_o"), wo("w_conv_o"))
    z2, x2, x2b, x2t = _proj_ln(merged, wa("w_out"), wo("w_out"), x1, sp["ln2_g"], sp["ln2_b"], 1.0, "out_proj_ln")
    wts[-1] = _with_own_slab(gathered[-1], slabs[-1], after=x2[0, 0])
    a2, b2, h2 = _ffn_up(x2b, wa("g2"), wo("g2"), wo("u2"), "ffn2_up")
    (z3,) = _proj_ln(h2, wa("d2"), wo("d2"), x2, sp["ln3_g"], sp["ln3_b"], 0.5, "ffn2_down", want_b=False)

    sg = {}
    rs = {}
    loss, df2b, df2t, dz3, sg["ln3_g"], sg["ln3_b"] = _loss_ln_bwd(z3, sp["ln3_g"], sp["ln3_b"], target, 0.5)
    da2, db2 = _ffn_bwd_h(df2b, wa("d2"), wo("d2"), a2, b2, "ffn2_bwd_h")
    g4 = _wgrad(df2t, h2, "d2", "wgrad_d2")
    g4 = _wgrad(x2t, da2, "g2", "wgrad_g2", g4)
    g4 = _wgrad(x2t, db2, "u2", "wgrad_u2", g4)
    rs[4] = _ReduceScatter(g4, 4, pos_c, pos_sc)
    dmb, dmt, dz2, sg["ln2_g"], sg["ln2_b"] = _dx_bwd(
        [da2, db2], [wo("g2"), wo("u2")], wa("g2"), dz3, "ffn2_dx_ln", ln=(z2, sp["ln2_g"], 1.0))
    rs[4].phase2()
    g3 = _wgrad(dmt, merged, "w_out", "wgrad_out")
    dro, drot, dco, dcot, dproj = _merge_bwd(dmb, wa("w_out"), wo("w_out"), proj, ro, co)
    g3 = _wgrad(drot, ret_in, "w_ret_o", "wgrad_ret_o", g3)
    g3 = _wgrad(dcot, u3, "w_conv_o", "wgrad_conv_o", g3)
    rs[3] = _ReduceScatter(g3, 3, pos_c, pos_sc)
    dr, sg["ret_gn_g"], dproj = _reto_bwd(dro, wa("w_ret_o"), wo("w_ret_o"), r, proj, sp["ret_gn_g"], dproj)
    rs[4].phase3()
    rs[3].phase2()
    dproj = _retention_bwd(proj, dr, states, cos_t, sin_t, dm_t, xi_t, zeta_t, cds, dproj)
    du1, sg["conv_ln_g"], sg["conv_ln_b"], sg["conv_b"] = _convo_bwd(
        dco, wa("w_conv_o"), wo("w_conv_o"), u1, sp["conv_ln_g"], sp["conv_ln_b"])
    dproj, dck = _conv_bwd(du1, proj, sp["conv_k"], dproj)
    sg["conv_k"] = dck[:CONV_W]
    adam(4, rs[4].result())
    rs[3].phase3()
    g2, sg["b_in"] = _wgrad(x1t, dproj, "w_in", "wgrad_in", colsum=True)
    rs[2] = _ReduceScatter(g2, 2, pos_c, pos_sc)
    dx_part = _dx_partial(dproj, wa("w_in"), (0, 1), "mixer_dx_part")
    df1b, df1t, dz1, sg["ln1_g"], sg["ln1_b"] = _dx_bwd(
        [dproj], [wo("w_in")], wa("w_in"), dz2, "mixer_dx_ln", ln=(z1, sp["ln1_g"], 0.5), chips=(2, 3),
        partial=dx_part)
    adam(3, rs[3].result())
    rs[2].phase2()
    shapes = {n: sg[n].shape for n in SMALL + ["conv_k"]}
    small_parts = _all_gather_small(_pack_small(sg, loss, SMALL_ROWS), "gather_small")
    da1, db1 = _ffn_bwd_h(df1b, wa("d1"), wo("d1"), a1, b1, "ffn1_bwd_h")
    small_sum = _sum_devices(small_parts, "sum_small")
    g1 = _wgrad(df1t, h1, "d1", "wgrad_d1")
    rs[1] = _ReduceScatter(g1, 1, pos_c, pos_sc)
    g0 = _wgrad(xt, da1, "g1", "wgrad_g1")
    g0 = _wgrad(xt, db1, "u1", "wgrad_u1", g0)
    rs[0] = _ReduceScatter(g0, 0, pos_c, pos_sc)
    rs[2].phase3()
    rs[1].phase2()
    rs[0].phase2()
    (grad_x,) = _dx_bwd([da1, db1], [wo("g1"), wo("u1")], wa("g1"), dz1, "ffn1_dx")
    adam(2, rs[2].result())
    rs[1].phase3()
    rs[0].phase3()
    adam(1, rs[1].result())
    adam(0, rs[0].result())
    _ORDER.active = False
    return grad_x, small_sum, shapes


MESH = pl.DeviceIdType.MESH
ANY = pl.BlockSpec(memory_space=pl.ANY)
HALF = D // 2


def _place():
    x, y, c = lax.axis_index("x"), lax.axis_index("y"), lax.axis_index("c")
    chips = [(1 - x, y), (x, 1 - y), (1 - x, 1 - y)]
    return x, y, c, chips


GATHER_PIECES = 4
GATHER_ID = 1


def _gather_weights(wloc, name):
    w_ref = jax.new_ref(wloc, memory_space=pltpu.MemorySpace.HBM)
    o_ref = jax.empty_ref(jax.ShapeDtypeStruct((NCHIP, 2, HALF, wloc.shape[-1]), BF16),
                          memory_space=pltpu.MemorySpace.HBM)
    dma = pltpu.SemaphoreType.DMA

    nq = GATHER_PIECES
    rows = HALF // nq

    @pl.kernel(mesh=plsc.ScalarSubcoreMesh(axis_name="sc", num_cores=1), name=name,
               scratch_types=(dma((2 * nq,)), dma((2 * nq,)), dma((3 * nq,)), dma((3 * nq,)),
                              dma((nq,)), dma((nq,))),
               compiler_params=pltpu.CompilerParams(collective_id=GATHER_ID))
    def launch(s1, r1, s2, r2, s3, r3):
        x, y, c, _ = _place()
        me = 2 * x + y
        sib = (x, y, 1 - c)
        x_nbr, y_nbr = (1 - x, y, c), (x, 1 - y, c)
        x_chip, y_chip, d_chip = 2 * (1 - x) + y, 2 * x + (1 - y), 2 * (1 - x) + (1 - y)
        _handshake([sib, x_nbr, y_nbr])

        def rc(src, dst, ss, rs, dev):
            return pltpu.make_async_remote_copy(src_ref=src, dst_ref=dst, send_sem=ss, recv_sem=rs,
                                                device_id=dev, device_id_type=MESH)

        def piece(ref, q):
            return ref.at[pl.ds(q * rows, rows)]

        sends = []
        for q in range(nq):
            for j, nbr in enumerate((x_nbr, y_nbr)):
                sends.append(rc(piece(w_ref.at[c], q), piece(o_ref.at[me, c], q),
                                s1.at[j * nq + q], r1.at[j * nq + q], nbr))
                sends[-1].start()
        on_chip = c * x_chip + (1 - c) * y_chip
        other_chip = c * y_chip + (1 - c) * x_chip
        on_to = (c * x + (1 - c) * (1 - x), c * (1 - y) + (1 - c) * y, c)
        for q in range(nq):
            slot = piece(o_ref.at[on_chip, c], q)
            rc(slot, slot, s1.at[(1 - c) * nq + q], r1.at[(1 - c) * nq + q], sib).wait_recv()
            sends.append(rc(slot, slot, s3.at[q], r3.at[q], on_to))
            sends[-1].start()
            sends.append(rc(slot, slot, s2.at[q], r2.at[q], sib))
            sends[-1].start()
        for q in range(nq):
            slot = piece(o_ref.at[other_chip, c], q)
            rc(slot, slot, s1.at[c * nq + q], r1.at[c * nq + q], sib).wait_recv()
            sends.append(rc(slot, slot, s2.at[nq + q], r2.at[nq + q], sib))
            sends[-1].start()
        for q in range(nq):
            slot = piece(o_ref.at[d_chip, c], q)
            rc(slot, slot, s3.at[q], r3.at[q], sib).wait_recv()
            sends.append(rc(slot, slot, s2.at[2 * nq + q], r2.at[2 * nq + q], sib))
            sends[-1].start()
        for j, chip in enumerate([other_chip, on_chip, d_chip]):
            for q in range(nq):
                slot = piece(o_ref.at[chip, 1 - c], q)
                rc(slot, slot, s2.at[j * nq + q], r2.at[j * nq + q], sib).wait_recv()
        for cp in sends:
            cp.wait_send()

    launch()
    return o_ref[...]


def _with_own_slab(gathered, wloc, after=None):
    own = 2 * lax.axis_index("x") + lax.axis_index("y")
    if after is not None:
        wloc = wloc + (after * 0.0).astype(wloc.dtype)
    cols = wloc.shape[-1]
    return lax.dynamic_update_slice(gathered, wloc[None], (own, 0, 0, 0)).reshape(NCHIP, D, cols)


PAIR_ID = 2
CHIP_ID = 3
HBM = pltpu.MemorySpace.HBM


def _sequencer(name, collective_id, n_sems):
    dma = pltpu.SemaphoreType.DMA
    return pl.kernel(mesh=plsc.ScalarSubcoreMesh(axis_name="sc", num_cores=1), name=name,
                     scratch_types=(dma((n_sems,)), dma((n_sems,))),
                     compiler_params=pltpu.CompilerParams(collective_id=collective_id))


def _handshake(peers):
    barrier = pltpu.get_barrier_semaphore()
    for peer in peers:
        pl.semaphore_signal(barrier, inc=1, device_id=peer, device_id_type=MESH)
    pl.semaphore_wait(barrier, len(peers))


def _pair_exchange(g5, name):
    _, _, hr, cols = g5.shape
    g_ref = jax.new_ref(g5, memory_space=HBM)
    o_ref = jax.empty_ref(jax.ShapeDtypeStruct((NCHIP, hr, cols), g5.dtype), memory_space=HBM)

    @_sequencer(name, PAIR_ID, NCHIP)
    def launch(ss, rs):
        x, y, c, _ = _place()
        sib = (x, y, 1 - c)
        _handshake([sib])
        cps = [pltpu.make_async_remote_copy(src_ref=g_ref.at[j, 1 - c], dst_ref=o_ref.at[j], send_sem=ss.at[j],
                                            recv_sem=rs.at[j], device_id=sib, device_id_type=MESH)
               for j in range(NCHIP)]
        for cp in cps:
            cp.start()
        for cp in cps:
            cp.wait()

    launch()
    return o_ref[...]


RS_TR = 256


def _pair_sum(pos, g5, got, name):
    _, _, hr, cols = g5.shape

    def body(pos_ref, g_ref, r_ref, o_ref):
        o_ref[...] = (g_ref[...].astype(F32) + r_ref[...].astype(F32)).astype(BF16)

    return _pcall(
        body, name=name, scalar_prefetch=1, grid=(NCHIP, hr // RS_TR),
        in_specs=[pl.BlockSpec((None, None, RS_TR, cols), lambda j, i, p: (j, p[0], i, 0)),
                  pl.BlockSpec((None, RS_TR, cols), lambda j, i, p: (j, i, 0))],
        out_specs=pl.BlockSpec((None, RS_TR, cols), lambda j, i, p: (j, i, 0)),
        out_shape=jax.ShapeDtypeStruct((NCHIP, hr, cols), BF16),
        compiler_params=_cp("parallel", "parallel"))(pos, g5, got)


def _chip_exchange(pb, name):
    _, hr, cols = pb.shape
    p_ref = jax.new_ref(pb, memory_space=HBM)
    o_ref = jax.empty_ref(jax.ShapeDtypeStruct((3, hr, cols), BF16), memory_space=HBM)

    @_sequencer(name, CHIP_ID, 3)
    def launch(ss, rs):
        x, y, c, chips = _place()
        _handshake([(px, py, c) for px, py in chips])
        cps = [pltpu.make_async_remote_copy(src_ref=p_ref.at[2 * px + py], dst_ref=o_ref.at[j], send_sem=ss.at[j],
                                            recv_sem=rs.at[j], device_id=(px, py, c), device_id_type=MESH)
               for j, (px, py) in enumerate(chips)]
        for cp in cps:
            cp.start()
        for cp in cps:
            cp.wait()

    launch()
    return o_ref[...]


def _chip_sum(pos, g5, got, peers, name):
    _, _, hr, cols = g5.shape

    def body(pos_ref, g_ref, r_ref, p_ref, o_ref, t_ref):
        acc = g_ref[...].astype(F32) + r_ref[...].astype(F32)
        for j in range(3):
            acc += p_ref[j].astype(F32)
        o_ref[...] = acc
        t_ref[...] = jnp.zeros_like(t_ref)

    return _pcall(
        body, name=name, scalar_prefetch=1, grid=(hr // RS_TR,),
        in_specs=[pl.BlockSpec((None, None, RS_TR, cols), lambda i, p: (p[0], p[1], i, 0)),
                  pl.BlockSpec((None, RS_TR, cols), lambda i, p: (p[0], i, 0)),
                  pl.BlockSpec((3, RS_TR, cols), lambda i, p: (0, i, 0))],
        out_specs=[pl.BlockSpec((None, RS_TR, cols), lambda i, p: (p[1], i, 0)),
                   pl.BlockSpec((8, 128), lambda i, p: (0, 0))],
        out_shape=[jax.ShapeDtypeStruct((2, hr, cols), F32), jax.ShapeDtypeStruct((8, 128), F32)],
        compiler_params=_cp("arbitrary"))(pos, g5, got, peers)


def _pair_share(gsum, name):
    g_ref = jax.new_ref(gsum, memory_space=HBM)

    @_sequencer(name, PAIR_ID, 1)
    def launch(ss, rs):
        x, y, c, _ = _place()
        sib = (x, y, 1 - c)
        _handshake([sib])
        cp = pltpu.make_async_remote_copy(src_ref=g_ref.at[c], dst_ref=g_ref.at[c], send_sem=ss.at[0],
                                          recv_sem=rs.at[0], device_id=sib, device_id_type=MESH)
        cp.start()
        cp.wait_send()
        pltpu.make_async_remote_copy(src_ref=g_ref.at[1 - c], dst_ref=g_ref.at[1 - c], send_sem=ss.at[0],
                                     recv_sem=rs.at[0], device_id=sib, device_id_type=MESH).wait_recv()

    launch()
    return g_ref[...]


class _ReduceScatter:
    def __init__(self, g_arr, gi, pos_c, pos_sc):
        _, rows, cols = g_arr.shape
        self.g5 = g_arr.reshape(NCHIP, 2, rows // 2, cols)
        self.gi, self.pos_c, self.pos_sc = gi, pos_c, pos_sc
        self.got = _pair_exchange(self.g5, f"pair_exchange_{gi}")

    def phase2(self):
        pb = _pair_sum(self.pos_c, self.g5, self.got, f"pair_sum_{self.gi}")
        self.peers = _chip_exchange(pb, f"chip_exchange_{self.gi}")

    def phase3(self):
        gsum, _ = _chip_sum(self.pos_sc, self.g5, self.got, self.peers, f"chip_sum_{self.gi}")
        self.full = _pair_share(gsum, f"pair_share_{self.gi}")

    def result(self):
        _, hr, cols = self.full.shape
        return self.full.reshape(2 * hr, cols)


SMALL_ROWS = 56


ALL_ID = 4


def _all_gather_small(vec, name):
    v_ref = jax.new_ref(vec, memory_space=HBM)
    o_ref = jax.empty_ref(jax.ShapeDtypeStruct((8, SMALL_ROWS, D), F32), memory_space=HBM)

    @_sequencer(name, ALL_ID, 8)
    def launch(ss, rs):
        x, y, c, _ = _place()
        me = 4 * x + 2 * y + c
        flip = lambda v, bit: 1 - v if bit else v
        peers = [(flip(x, m >> 2), flip(y, (m >> 1) & 1), flip(c, m & 1)) for m in range(1, 8)]
        _handshake(peers)
        mine = pltpu.make_async_copy(v_ref, o_ref.at[me], ss.at[7])
        mine.start()
        cps = [pltpu.make_async_remote_copy(src_ref=v_ref, dst_ref=o_ref.at[me], send_sem=ss.at[k],
                                            recv_sem=rs.at[k], device_id=peer, device_id_type=MESH)
               for k, peer in enumerate(peers)]
        for cp in cps:
            cp.start()
        for cp in cps:
            cp.wait()
        mine.wait()

    launch()
    return o_ref[...]


def _sum_devices(parts, name):
    def body(p_ref, o_ref):
        acc = p_ref[0]
        for d in range(1, 8):
            acc += p_ref[d]
        o_ref[...] = acc

    return _pcall(
        body, name=name, grid=(SMALL_ROWS // 8,),
        in_specs=[pl.BlockSpec((8, 8, D), lambda i: (0, i, 0))],
        out_specs=pl.BlockSpec((8, D), lambda i: (i, 0)),
        out_shape=jax.ShapeDtypeStruct((SMALL_ROWS, D), F32),
        compiler_params=_cp("parallel"))(parts)


def _adamw_math(w, g, m, v):
    m2 = ADAM_B1 * m + (1.0 - ADAM_B1) * g
    v2 = ADAM_B2 * v + (1.0 - ADAM_B2) * (g * g)
    m_hat = m2 / (1.0 - ADAM_B1 ** ADAM_STEP)
    v_hat = v2 / (1.0 - ADAM_B2 ** ADAM_STEP)
    delta = -ADAM_LR * (m_hat / (jnp.sqrt(v_hat) + ADAM_EPS) + ADAM_WD * w)
    return delta, m2, v2


def _adamw(w, g, m, v, name, g_block=None):
    R, C = w.shape
    tr = R
    gw_hint = C if g_block is None else g_block[0]
    for cand in (512, 352, 256, 176, 128, 64, 32, 16, 8):
        if R % cand == 0 and cand * max(C, gw_hint) * 4 <= (2 << 20):
            tr = cand
            break
    gw, gi = (C, 0) if g_block is None else g_block

    def body(w_ref, g_ref, m_ref, v_ref, go_ref, d_ref, mo_ref, vo_ref):
        g = g_ref[:, 0:C]
        d, m2, v2 = _adamw_math(w_ref[...], g, m_ref[...], v_ref[...])
        go_ref[...] = g
        d_ref[...] = d
        mo_ref[...] = m2
        vo_ref[...] = v2

    spec = pl.BlockSpec((tr, C), lambda i: (i, 0))
    return _pcall(
        body, name=name, grid=(R // tr,),
        in_specs=[spec, pl.BlockSpec((tr, gw), lambda i: (i, gi)), spec, spec],
        out_specs=[spec] * 4, out_shape=[jax.ShapeDtypeStruct((R, C), F32)] * 4,
        compiler_params=_cp("parallel"))(w, g, m, v)


BIG = ["ffn1_w_gate", "ffn1_w_up", "ffn1_w_down", "w_in", "w_ret_o", "w_conv_o", "w_out",
       "ffn2_w_gate", "ffn2_w_up", "ffn2_w_down"]
SLAB = {"ffn1_w_gate": "g1", "ffn1_w_up": "u1", "ffn1_w_down": "d1", "w_in": "w_in", "w_ret_o": "w_ret_o",
        "w_conv_o": "w_conv_o", "w_out": "w_out", "ffn2_w_gate": "g2", "ffn2_w_up": "u2", "ffn2_w_down": "d2"}
TRANSPOSED = {"ffn1_w_down", "ffn2_w_down", "w_ret_o", "w_conv_o", "w_out"}
MINOR_ROWS = {"ffn1_w_gate", "ffn1_w_up", "ffn2_w_gate", "ffn2_w_up"}
SMALL = ["ln1_g", "ln1_b", "ln2_g", "ln2_b", "ln3_g", "ln3_b", "conv_ln_g", "conv_ln_b", "conv_b",
         "ret_gn_g", "b_in"]
ORDER = ["ffn1_w_gate", "ffn1_w_up", "ffn1_w_down", "ln1_g", "ln1_b", "w_in", "b_in", "ret_gn_g", "conv_k",
         "conv_b", "conv_ln_g", "conv_ln_b", "w_ret_o", "w_conv_o", "w_out", "ln2_g", "ln2_b",
         "ffn2_w_gate", "ffn2_w_up", "ffn2_w_down", "ln3_g", "ln3_b"]


def _slab_width(name):
    return WIDTH[SLAB[name]]


def _pack_group(weights, keys):
    by_key = {SLAB[n]: n for n in BIG}
    parts = []
    for key in keys:
        w = weights[by_key[key]]
        w = w.T if by_key[key] in TRANSPOSED else w
        parts.append(jnp.pad(w, ((0, 0), (0, WIDTH[key] - w.shape[1]))))
    return jnp.concatenate(parts, axis=1).astype(BF16)


def _pack_small(vals, loss, rows):
    flat = jnp.concatenate([vals[n].reshape(-1) for n in SMALL] + [vals["conv_k"].reshape(-1), loss.reshape(-1)])
    return jnp.pad(flat, (0, rows * D - flat.shape[0])).reshape(rows, D)


def _unpack_small(arr, shapes):
    flat = arr.reshape(-1)
    out, pos = {}, 0
    for n in SMALL + ["conv_k"]:
        size = int(np.prod(shapes[n]))
        out[n] = flat[pos:pos + size].reshape(shapes[n])
        pos += size
    return out, flat[pos]


def kernel(x, ffn1_w_gate, ffn1_w_up, ffn1_w_down, ln1_g, ln1_b, w_in, b_in, ret_gn_g, conv_k, conv_b, conv_ln_g, conv_ln_b, w_ret_o, w_conv_o, w_out, ln2_g, ln2_b, ffn2_w_gate, ffn2_w_up, ffn2_w_down, ln3_g, ln3_b, loss_target, m_ffn1_w_gate, m_ffn1_w_up, m_ffn1_w_down, m_ln1_g, m_ln1_b, m_w_in, m_b_in, m_ret_gn_g, m_conv_k, m_conv_b, m_conv_ln_g, m_conv_ln_b, m_w_ret_o, m_w_conv_o, m_w_out, m_ln2_g, m_ln2_b, m_ffn2_w_gate, m_ffn2_w_up, m_ffn2_w_down, m_ln3_g, m_ln3_b, v_ffn1_w_gate, v_ffn1_w_up, v_ffn1_w_down, v_ln1_g, v_ln1_b, v_w_in, v_b_in, v_ret_gn_g, v_conv_k, v_conv_b, v_conv_ln_g, v_conv_ln_b, v_w_ret_o, v_w_conv_o, v_w_out, v_ln2_g, v_ln2_b, v_ffn2_w_gate, v_ffn2_w_up, v_ffn2_w_down, v_ln3_g, v_ln3_b):
    args = dict(locals())
    w = {n: args[n] for n in ORDER}
    m = {n: args["m_" + n] for n in ORDER}
    v = {n: args["v_" + n] for n in ORDER}
    xi, yi, ci = lax.axis_index("x"), lax.axis_index("y"), lax.axis_index("c")
    chip = 2 * xi + yi

    shards = {n: w[n][0] for n in BIG}
    slabs = [_pack_group(shards, keys) for keys in GATHER_GROUPS]
    slabs = [slab.reshape(2, HALF, slab.shape[1]) for slab in slabs]
    gathered = [_gather_weights(slab, f"gather_{gi}") for gi, slab in enumerate(slabs)]

    sp = {n: w[n] for n in SMALL}
    kpad = jnp.zeros((CONV_W, D), F32)
    kpad = lax.dynamic_update_slice(kpad, w["conv_k"][0, :, 0, :] * jnp.where(ci == 0, 1.0, 0.0), (0, chip * (D // NCHIP)))
    kvec = jnp.pad(kpad.reshape(-1), (0, SMALL_ROWS * D - CONV_W * D)).reshape(SMALL_ROWS, D)
    pos_c = jnp.reshape(ci, (1,)).astype(jnp.int32)
    pos_sc = jnp.stack([chip, ci]).astype(jnp.int32)
    out = {}

    def adam(gi, slab):
        for n in BIG:
            (g_of, off), width = LOC[SLAB[n]], _slab_width(n)
            if g_of != gi:
                continue
            w2 = w[n][0]
            if n in TRANSPOSED:
                res = _adamw(w2, slab[:, off:off + w2.shape[0]].T, m[n][0], v[n][0], "adamw_" + n)
                out[n] = [r[None] for r in res]
            elif n in MINOR_ROWS:
                res = _adamw(w2.T, slab[:, off:off + w2.shape[1]].T, m[n][0].T, v[n][0].T, "adamw_" + n)
                out[n] = [r.T[None] for r in res]
            else:
                res = _adamw(w2, slab, m[n][0], v[n][0], "adamw_" + n, g_block=(width, off // width))
                out[n] = [r[None] for r in res]

    grad_x, small_sum, shapes = _local_step(x[0], loss_target[0], gathered, slabs, sp, kvec, pos_c, pos_sc, adam)
    small, total = _unpack_small(small_sum, shapes)

    for n in SMALL:
        res = _adamw(w[n], small[n], m[n], v[n], "adamw_" + n)
        out[n] = list(res)
    gk = lax.dynamic_slice(small["conv_k"], (0, chip * (D // NCHIP)), (CONV_W, D // NCHIP))
    res = _adamw(w["conv_k"][0, :, 0, :], gk, m["conv_k"][0, :, 0, :], v["conv_k"][0, :, 0, :], "adamw_conv_k")
    out["conv_k"] = [r[None, :, None, :] for r in res]

    grads = [out[n][0] for n in ORDER]
    deltas = [out[n][1] for n in ORDER]
    new_m = [out[n][2] for n in ORDER]
    new_v = [out[n][3] for n in ORDER]
    return (total, grad_x[None], *grads, *deltas, *new_m, *new_v)
```

```python
import dataclasses
import functools

import numpy as np
import jax
import jax.numpy as jnp
from jax import lax
from jax.experimental import pallas as pl
from jax.experimental.pallas import tpu as pltpu
from jax.experimental.pallas import tpu_sc as plsc

F32 = jnp.float32
BF16 = jnp.bfloat16

D = 1024
FS = 704
FSP = 768
FP = 4 * FSP
H = 8
DK = 128
DV = 256
CH = 128
VW = H * DV
INW = 10240
INS = INW // 4
CONV_W = 31
HALO = 32
EPS = 1e-5
ALPHA = 2.0 ** 0.25
ROPE_BASE = 10000.0
NCHIP = 4

ADAM_LR, ADAM_B1, ADAM_B2, ADAM_EPS, ADAM_WD, ADAM_STEP = 0.001, 0.9, 0.999, 1e-08, 0.01, 10

OFF = {"w_in": 0, "w_ret_o": 2560, "g1": 3072, "u1": 3840, "d1": 4608,
       "g2": 5376, "u2": 6144, "d2": 6912, "w_conv_o": 7680, "w_out": 7936}
WCOLS = 8192
WIDTH = {"w_in": INS, "w_ret_o": VW // NCHIP, "w_conv_o": D // NCHIP, "w_out": D // NCHIP,
         "g1": FSP, "u1": FSP, "d1": FSP, "g2": FSP, "u2": FSP, "d2": FSP}
GROUPS = (("g1", "u1"), ("d1",), ("w_in",), ("w_ret_o", "w_conv_o", "w_out"), ("g2", "u2", "d2"))
GATHER_GROUPS = (("g1", "u1"), ("d1",), ("w_in",), ("w_ret_o", "w_conv_o", "w_out"), ("g2", "u2", "d2"))


def _locate(groups):
    loc = {}
    for gi, keys in enumerate(groups):
        off = 0
        for k in keys:
            loc[k] = (gi, off)
            off += WIDTH[k]
    return loc


LOC = _locate(GROUPS)
LOC_W = _locate(GATHER_GROUPS)
GCOLS = [sum(WIDTH[k] for k in keys) for keys in GROUPS]
VMEM_LIMIT = 56 << 20


def _cp(*sem, **kw):
    return pltpu.CompilerParams(dimension_semantics=sem, vmem_limit_bytes=VMEM_LIMIT, **kw)


class _ProgramOrder:
    def __init__(self):
        self.active = False
        self.token = None


_ORDER = _ProgramOrder()


def _pcall(body, *, in_specs, scalar_prefetch=0, **kw):
    def call(*args):
        dep = _ORDER.token if _ORDER.active else None
        specs, fn = list(in_specs), body
        if dep is not None:
            n = len(args)

            def fn(*refs):
                return body(*refs[:n], *refs[n + 1:])

            specs.append(pl.BlockSpec(memory_space=pl.ANY))
            args = (*args, dep)
        params = dict(kw)
        if scalar_prefetch:
            params["grid_spec"] = pltpu.PrefetchScalarGridSpec(
                num_scalar_prefetch=scalar_prefetch, grid=params.pop("grid"), in_specs=specs,
                out_specs=params.pop("out_specs"))
        else:
            params["in_specs"] = specs
        out = pl.pallas_call(fn, **params)(*args)
        if _ORDER.active:
            _ORDER.token = jax.tree.leaves(out)[-1]
        return out

    return call


def _resident(shape, col_block):
    lead = (0,) * (len(shape) - 1)
    return pl.BlockSpec(shape, lambda *_: (*lead, col_block), pipeline_mode=pl.Buffered(1))


def _sig(x):
    return 1.0 / (1.0 + jnp.exp(-x))


def _dot(a, b):
    return jnp.dot(a, b, preferred_element_type=F32)


def _dot_nt(a, b):
    return lax.dot_general(a, b, (((1,), (1,)), ((), ())), preferred_element_type=F32)


def _ln_fwd(z, g, b):
    mu = jnp.mean(z, axis=-1, keepdims=True)
    xc = z - mu
    var = jnp.mean(xc * xc, axis=-1, keepdims=True)
    rstd = lax.rsqrt(var + EPS)
    xh = xc * rstd
    return xh * g + b, xh, rstd


def _ln_bwd(dy, xh, rstd, g):
    dxh = dy * g
    m1 = jnp.mean(dxh, axis=-1, keepdims=True)
    m2 = jnp.mean(dxh * xh, axis=-1, keepdims=True)
    return rstd * (dxh - m1 - xh * m2)


def _colsum(x):
    return jnp.sum(x, axis=0, keepdims=True)


def _zero_first(first, *refs):
    @pl.when(first)
    def _():
        for ref in refs:
            ref[...] = jnp.zeros_like(ref)


def _rope_tables(T):
    half = DK // 2
    freqs = ROPE_BASE ** (-np.arange(half, dtype=np.float32) / half)
    ang = (np.arange(T, dtype=np.float32)[:, None] * freqs[None, :]).astype(np.float32)
    cos, sin = np.cos(ang), np.sin(ang)
    return (jnp.asarray(np.concatenate([cos, cos], 1), F32),
            jnp.asarray(np.concatenate([-sin, sin], 1), F32))


def _decay_tables():
    h = np.arange(H, dtype=np.float64)
    log_g = np.log(1.0 - np.exp2(-5.0 - h))
    idx = np.arange(CH, dtype=np.float64)
    diff = idx[:, None] - idx[None, :]
    dm = np.where(diff[None] >= 0, np.exp(np.maximum(diff, 0.0)[None] * log_g[:, None, None]), 0.0)
    xi = np.exp((idx[None, :] + 1.0) * log_g[:, None])
    zeta = np.exp((CH - 1.0 - idx)[None, :] * log_g[:, None])
    cd = np.exp(CH * log_g)
    xi_t = np.broadcast_to(xi[:, :, None], (H, CH, DV))
    zeta_t = np.broadcast_to(zeta[:, :, None], (H, CH, DK))
    return (jnp.asarray(dm, F32), jnp.asarray(xi_t, F32), jnp.asarray(zeta_t, F32),
            [float(v) for v in cd])


def _cast_t(x):
    T = x.shape[0]
    tm = min(T, 512)

    def body(x_ref, xb_ref, xt_ref):
        v = x_ref[...]
        xb_ref[...] = v.astype(BF16)
        xt_ref[...] = v.T.astype(BF16)

    return _pcall(
        body, name="cast_t", grid=(T // tm,),
        in_specs=[pl.BlockSpec((tm, D), lambda i: (i, 0))],
        out_specs=[pl.BlockSpec((tm, D), lambda i: (i, 0)), pl.BlockSpec((D, tm), lambda i: (0, i))],
        out_shape=[jax.ShapeDtypeStruct((T, D), BF16), jax.ShapeDtypeStruct((D, T), BF16)],
        compiler_params=_cp("parallel"))(x)


def _ffn_up(xb, wall, og, ou, name):
    T = xb.shape[0]
    tm = min(T, 512)
    assert ou == og + FSP

    def body(x_ref, w_ref, a_ref, b_ref, h_ref):
        x = x_ref[...]
        for s in range(NCHIP):
            cols = slice(s * FSP, (s + 1) * FSP)
            a = _dot(x, w_ref[s, :, 0:FSP])
            b = _dot(x, w_ref[s, :, FSP:2 * FSP])
            a_ref[:, cols] = a.astype(BF16)
            b_ref[:, cols] = b.astype(BF16)
            h_ref[:, cols] = (a * _sig(a) * b).astype(BF16)

    ospec = pl.BlockSpec((tm, FP), lambda i: (i, 0))
    return _pcall(
        body, name=name, grid=(T // tm,),
        in_specs=[pl.BlockSpec((tm, D), lambda i: (i, 0)), _resident((NCHIP, D, 2 * FSP), og // (2 * FSP))],
        out_specs=[ospec] * 3, out_shape=[jax.ShapeDtypeStruct((T, FP), BF16)] * 3,
        compiler_params=_cp("parallel"))(xb, wall)


def _proj_ln(hb, wall, off, res, g, b, coef, name, want_b=True):
    T, K = hb.shape
    ks = K // NCHIP
    tm = min(T, 512)
    sub = min(tm, 256)

    def body(h_ref, w_ref, r_ref, g_ref, b_ref, z_ref, *rest):
        for r0 in range(0, tm, sub):
            r = slice(r0, r0 + sub)
            acc = _dot_nt(h_ref[r, 0:ks], w_ref[0])
            for s in range(1, NCHIP):
                acc += _dot_nt(h_ref[r, s * ks:(s + 1) * ks], w_ref[s])
            z = ALPHA * r_ref[r, :] + coef * acc
            z_ref[r, :] = z
            if want_b:
                y, _, _ = _ln_fwd(z, g_ref[...], b_ref[...])
                y_ref, yb_ref, yt_ref = rest
                y_ref[r, :] = y
                yb_ref[r, :] = y.astype(BF16)
                yt_ref[:, r] = y.T.astype(BF16)

    row = pl.BlockSpec((tm, D), lambda i: (i, 0))
    vec = pl.BlockSpec((1, D), lambda i: (0, 0))
    out_specs = [row]
    out_shape = [jax.ShapeDtypeStruct((T, D), F32)]
    if want_b:
        out_specs += [row, row, pl.BlockSpec((D, tm), lambda i: (0, i))]
        out_shape += [jax.ShapeDtypeStruct((T, D), F32), jax.ShapeDtypeStruct((T, D), BF16),
                      jax.ShapeDtypeStruct((D, T), BF16)]
    return _pcall(
        body, name=name, grid=(T // tm,),
        in_specs=[pl.BlockSpec((tm, K), lambda i: (i, 0)),
                  _resident((NCHIP, D, ks), off // ks), row, vec, vec],
        out_specs=out_specs, out_shape=out_shape,
        compiler_params=_cp("parallel"))(hb, wall, res, g, b)


def _inproj(xb, wall, off, b_in, cos_t, sin_t):
    T = xb.shape[0]
    tm, tn = min(T, 512), 512
    assert off == 0

    def body(x_ref, w_ref, bias_ref, cos_ref, sin_ref, o_ref):
        x = x_ref[...]
        c = cos_ref[...]
        s = sin_ref[...]
        for n0 in range(0, INW, tn):
            chip, c0 = divmod(n0, INS)
            acc = _dot(x, w_ref[chip, :, c0:c0 + tn]) + bias_ref[:, n0:n0 + tn]
            if n0 >= 2 * D:
                o_ref[:, n0:n0 + tn] = acc.astype(BF16)
                continue
            scale = DK ** -0.5 if n0 < D else 1.0
            for hh in range(tn // DK):
                xh = acc[:, hh * DK:(hh + 1) * DK]
                o = (xh * c + pltpu.roll(xh, DK // 2, 1) * s) * scale
                o_ref[:, n0 + hh * DK:n0 + (hh + 1) * DK] = o.astype(BF16)

    return _pcall(
        body, name="inproj", grid=(T // tm,),
        in_specs=[pl.BlockSpec((tm, D), lambda i: (i, 0)),
                  _resident((NCHIP, D, INS), 0),
                  pl.BlockSpec((1, INW), lambda i: (0, 0)),
                  pl.BlockSpec((tm, DK), lambda i: (i, 0)),
                  pl.BlockSpec((tm, DK), lambda i: (i, 0))],
        out_specs=pl.BlockSpec((tm, INW), lambda i: (i, 0)),
        out_shape=jax.ShapeDtypeStruct((T, INW), BF16),
        compiler_params=_cp("parallel"))(xb, wall, b_in, cos_t, sin_t)


RET_CPS = 2


def _retention_fwd(proj, gn_g, dm_t, xi_t, zeta_t, cds):
    T = proj.shape[0]
    n = T // CH
    tr = RET_CPS * CH

    def body(q_ref, k_ref, v_ref, g_ref, gn_ref, dm_ref, xi_ref, zt_ref, r_ref, ri_ref, st_ref, state):
        @pl.when(pl.program_id(0) == 0)
        def _():
            state[...] = jnp.zeros_like(state)

        for h in range(H):
            rows = slice(h * DK, (h + 1) * DK)
            cols = slice(h * DV, (h + 1) * DV)
            s_prev = state[rows, :]
            for j in range(RET_CPS):
                t = slice(j * CH, (j + 1) * CH)
                q = q_ref[t, h * DK:(h + 1) * DK]
                k = k_ref[t, h * DK:(h + 1) * DK]
                v = v_ref[t, cols]
                s_b = s_prev.astype(BF16)
                st_ref[j, rows, :] = s_b
                sc = _dot_nt(q, k) * dm_ref[h]
                r = _dot(sc.astype(BF16), v) + _dot(q, s_b) * xi_ref[h]
                kz = k.astype(F32) * zt_ref[h]
                s_prev = cds[h] * s_prev + _dot(kz.T.astype(BF16), v)
                r_ref[t, cols] = r
                mu = jnp.mean(r, axis=-1, keepdims=True)
                xc = r - mu
                var = jnp.mean(xc * xc, axis=-1, keepdims=True)
                y = xc * lax.rsqrt(var + EPS) * gn_ref[:, cols]
                g = g_ref[t, cols].astype(F32)
                ri_ref[t, cols] = (g * _sig(g) * y).astype(BF16)
            state[rows, :] = s_prev

    full3 = lambda shp: pl.BlockSpec(shp, lambda c: (0, 0, 0))
    return _pcall(
        body, name="retention_fwd", grid=(n // RET_CPS,),
        in_specs=[pl.BlockSpec((tr, D), lambda c: (c, 0)),
                  pl.BlockSpec((tr, D), lambda c: (c, 1)),
                  pl.BlockSpec((tr, VW), lambda c: (c, 1)),
                  pl.BlockSpec((tr, VW), lambda c: (c, 2)),
                  pl.BlockSpec((1, VW), lambda c: (0, 0)),
                  full3((H, CH, CH)), full3((H, CH, DV)), full3((H, CH, DK))],
        out_specs=[pl.BlockSpec((tr, VW), lambda c: (c, 0)), pl.BlockSpec((tr, VW), lambda c: (c, 0)),
                   pl.BlockSpec((RET_CPS, H * DK, DV), lambda c: (c, 0, 0))],
        out_shape=[jax.ShapeDtypeStruct((T, VW), F32), jax.ShapeDtypeStruct((T, VW), BF16),
                   jax.ShapeDtypeStruct((n, H * DK, DV), BF16)],
        scratch_shapes=[pltpu.VMEM((H * DK, DV), F32)],
        compiler_params=_cp("arbitrary"))(proj, proj, proj, proj, gn_g, dm_t, xi_t, zeta_t)


CONV_TT = 256
CONV_SB = 64
CONV_CB = 256


SUB = 8
CONV_PAD = 8


def _glu(a_ref, b_ref, rows=slice(None)):
    a = a_ref[rows, :].astype(F32)
    sb = _sig(b_ref[rows, :].astype(F32))
    return a, sb


def _shift_copies(win, sh, rows):
    win[rows:rows + CONV_PAD, :] = jnp.zeros((CONV_PAD, D), F32)
    for b in range(1, SUB):
        sh[b - 1, :, :] = win[b:b + rows, :]


def _tap(win, sh, start, size, cs):
    a, b = divmod(start, SUB)
    src = win if b == 0 else sh.at[b - 1]
    return src[SUB * a:SUB * a + size, cs]


def _conv_fwd(proj, conv_k, conv_b, ln_g, ln_b):
    T = proj.shape[0]
    tt = min(T, CONV_TT)
    ca, cb = 6 * D // D, 7 * D // D

    def body(a_ref, b_ref, pa_ref, pb_ref, k_ref, cb_ref, g_ref, bb_ref, u1_ref, u3_ref, win, sh):
        i = pl.program_id(0)
        a, sb = _glu(a_ref, b_ref)
        win[HALO:tt + HALO, :] = a * sb
        pa, psb = _glu(pa_ref, pb_ref, slice(tt - HALO, tt))
        win[0:HALO, :] = jnp.where(i > 0, pa * psb, 0.0)
        _shift_copies(win, sh, tt + HALO)
        for c0 in range(0, D, CONV_CB):
            cs = slice(c0, c0 + CONV_CB)
            for r0 in range(0, tt, CONV_SB):
                acc = jnp.zeros((CONV_SB, CONV_CB), F32)
                for w in range(CONV_W):
                    st = r0 + HALO - (CONV_W - 1) + w
                    acc += _tap(win, sh, st, CONV_SB, cs) * k_ref[w:w + 1, cs]
                u1_ref[r0:r0 + CONV_SB, cs] = acc + cb_ref[:, cs]
        u2, _, _ = _ln_fwd(u1_ref[...], g_ref[...], bb_ref[...])
        u3_ref[...] = (u2 * _sig(u2)).astype(BF16)

    vec = pl.BlockSpec((1, D), lambda i: (0, 0))
    row = pl.BlockSpec((tt, D), lambda i: (i, 0))
    return _pcall(
        body, name="conv_fwd", grid=(T // tt,),
        in_specs=[pl.BlockSpec((tt, D), lambda i: (i, ca)), pl.BlockSpec((tt, D), lambda i: (i, cb)),
                  pl.BlockSpec((tt, D), lambda i: (jnp.maximum(i - 1, 0), ca)),
                  pl.BlockSpec((tt, D), lambda i: (jnp.maximum(i - 1, 0), cb)),
                  pl.BlockSpec((CONV_W, D), lambda i: (0, 0)), vec, vec, vec],
        out_specs=[row, row],
        out_shape=[jax.ShapeDtypeStruct((T, D), F32), jax.ShapeDtypeStruct((T, D), BF16)],
        scratch_shapes=[pltpu.VMEM((tt + HALO + CONV_PAD, D), F32), pltpu.VMEM((SUB - 1, tt + HALO, D), F32)],
        compiler_params=_cp("parallel"))(proj, proj, proj, proj, conv_k, conv_b, ln_g, ln_b)


def _merge(ret_in, u3, proj, wall, off_r, off_c):
    T = ret_in.shape[0]
    tm = min(T, 512)
    kr, kc = VW // NCHIP, D // NCHIP

    def body(ri_ref, u3_ref, gr_ref, gc_ref, wr_ref, wc_ref, ro_ref, co_ref, m_ref):
        ro = _dot_nt(ri_ref[:, 0:kr], wr_ref[0])
        co = _dot_nt(u3_ref[:, 0:kc], wc_ref[0])
        for s in range(1, NCHIP):
            ro += _dot_nt(ri_ref[:, s * kr:(s + 1) * kr], wr_ref[s])
            co += _dot_nt(u3_ref[:, s * kc:(s + 1) * kc], wc_ref[s])
        ro_ref[...] = ro.astype(BF16)
        co_ref[...] = co.astype(BF16)
        m = _sig(gr_ref[...].astype(F32)) * ro + _sig(gc_ref[...].astype(F32)) * co
        m_ref[...] = m.astype(BF16)

    row = pl.BlockSpec((tm, D), lambda i: (i, 0))
    return _pcall(
        body, name="merge", grid=(T // tm,),
        in_specs=[pl.BlockSpec((tm, VW), lambda i: (i, 0)), row,
                  pl.BlockSpec((tm, D), lambda i: (i, 8)), pl.BlockSpec((tm, D), lambda i: (i, 9)),
                  pl.BlockSpec((NCHIP, D, kr), lambda i: (0, 0, off_r // kr)),
                  pl.BlockSpec((NCHIP, D, kc), lambda i: (0, 0, off_c // kc))],
        out_specs=[row] * 3, out_shape=[jax.ShapeDtypeStruct((T, D), BF16)] * 3,
        compiler_params=_cp("parallel"))(ret_in, u3, proj, proj, wall, wall)


def _loss_ln_bwd(z, g, b, target, coef):
    T = z.shape[0]
    tm = min(T, 256)
    nt = T // tm

    def body(z_ref, g_ref, b_ref, t_ref, loss_ref, dzb_ref, dzt_ref, dz_ref, dg_ref, db_ref, lacc):
        i = pl.program_id(0)
        _zero_first(i == 0, lacc, dg_ref, db_ref)
        gam = g_ref[...]
        y, xh, rstd = _ln_fwd(z_ref[...], gam, b_ref[...])
        e = y - t_ref[...]
        lacc[...] += _colsum(e * e)
        dy = e * (1.0 / D)
        dz = _ln_bwd(dy, xh, rstd, gam)
        dz_ref[...] = dz
        dzc = coef * dz
        dzb_ref[...] = dzc.astype(BF16)
        dzt_ref[...] = dzc.T.astype(BF16)
        dg_ref[...] += _colsum(dy * xh)
        db_ref[...] += _colsum(dy)

        @pl.when(i == nt - 1)
        def _():
            loss_ref[...] = (0.5 / D) * jnp.sum(lacc[...], axis=1, keepdims=True)

    row = pl.BlockSpec((tm, D), lambda i: (i, 0))
    vec = pl.BlockSpec((1, D), lambda i: (0, 0))
    return _pcall(
        body, name="loss_ln_bwd", grid=(nt,),
        in_specs=[row, vec, vec, row],
        out_specs=[pl.BlockSpec((1, 1), lambda i: (0, 0)), row, pl.BlockSpec((D, tm), lambda i: (0, i)),
                   row, vec, vec],
        out_shape=[jax.ShapeDtypeStruct((1, 1), F32), jax.ShapeDtypeStruct((T, D), BF16),
                   jax.ShapeDtypeStruct((D, T), BF16), jax.ShapeDtypeStruct((T, D), F32),
                   jax.ShapeDtypeStruct((1, D), F32), jax.ShapeDtypeStruct((1, D), F32)],
        scratch_shapes=[pltpu.VMEM((1, D), F32)],
        compiler_params=_cp("arbitrary"))(z, g, b, target)


def _ffn_bwd_h(dfb, wall, od, a, b, name):
    T = dfb.shape[0]
    tm = min(T, 512)

    def body(d_ref, w_ref, a_ref, b_ref, da_ref, db_ref):
        d = d_ref[...]
        for s in range(NCHIP):
            cols = slice(s * FSP, (s + 1) * FSP)
            dh = _dot(d, w_ref[s])
            a = a_ref[:, cols].astype(F32)
            sg = _sig(a)
            da_ref[:, cols] = (dh * b_ref[:, cols].astype(F32) * (sg * (1.0 + a * (1.0 - sg)))).astype(BF16)
            db_ref[:, cols] = (dh * a * sg).astype(BF16)

    ospec = pl.BlockSpec((tm, FP), lambda i: (i, 0))
    return _pcall(
        body, name=name, grid=(T // tm,),
        in_specs=[pl.BlockSpec((tm, D), lambda i: (i, 0)), _resident((NCHIP, D, FSP), od // FSP), ospec, ospec],
        out_specs=[ospec] * 2, out_shape=[jax.ShapeDtypeStruct((T, FP), BF16)] * 2,
        compiler_params=_cp("parallel"))(dfb, wall, a, b)


DX_SUB = 256


def _dx_partial(lhs, wall, chips, name):
    T, K = lhs.shape
    ks = K // NCHIP
    nc = len(chips)
    assert list(chips) == list(range(chips[0], chips[0] + nc)) and chips[0] % nc == 0
    tm = min(T, 512)

    def body(l_ref, w_ref, o_ref):
        for r0 in range(0, tm, DX_SUB):
            r = slice(r0, r0 + DX_SUB)
            acc = _dot_nt(l_ref[r, 0:ks], w_ref[0])
            for s in range(1, nc):
                acc += _dot_nt(l_ref[r, s * ks:(s + 1) * ks], w_ref[s])
            o_ref[r, :] = acc

    blk = chips[0] // nc
    return _pcall(
        body, name=name, grid=(T // tm,),
        in_specs=[pl.BlockSpec((tm, nc * ks), lambda i: (i, blk)),
                  pl.BlockSpec((nc, D, ks), lambda i: (blk, 0, 0), pipeline_mode=pl.Buffered(1))],
        out_specs=pl.BlockSpec((tm, D), lambda i: (i, 0)),
        out_shape=jax.ShapeDtypeStruct((T, D), F32),
        compiler_params=_cp("parallel"))(lhs, wall)


def _dx_bwd(lhs, offs, wall, dz_next, name, ln=None, chips=tuple(range(NCHIP)), partial=None):
    T, K = lhs[0].shape
    ks = K // NCHIP
    nl = len(lhs)
    nc = len(chips)
    assert list(offs) == [l * ks for l in range(nl)]
    assert list(chips) == list(range(chips[0], chips[0] + nc)) and chips[0] % nc == 0
    blk = chips[0] // nc
    tm = min(T, 512)

    def body(*refs):
        l_refs = refs[:nl]
        w_ref = refs[nl]
        dzn_ref = refs[nl + 1]
        pos = nl + 2
        if partial is not None:
            p_ref = refs[pos]
            pos += 1
        if ln is not None:
            z_ref, g_ref = refs[pos:pos + 2]
            pos += 2
        outs = refs[pos:]
        sums = list(outs[3:]) if ln is not None else list(outs[1:])

        _zero_first(pl.program_id(0) == 0, *sums)

        for r0 in range(0, tm, DX_SUB):
            r = slice(r0, r0 + DX_SUB)
            acc = None if partial is None else p_ref[r, :]
            for s in range(nc):
                rows = slice(s * ks, (s + 1) * ks)
                for l in range(nl):
                    part = _dot_nt(l_refs[l][r, rows], w_ref[s, :, l * ks:(l + 1) * ks])
                    acc = part if acc is None else acc + part
            dy = acc + ALPHA * dzn_ref[r, :]
            if ln is None:
                outs[0][r, :] = dy
            else:
                gam = g_ref[...]
                _, xh, rstd = _ln_fwd(z_ref[r, :], gam, 0.0)
                dz = _ln_bwd(dy, xh, rstd, gam)
                dzc = ln[2] * dz
                outs[0][r, :] = dzc.astype(BF16)
                outs[1][:, r] = dzc.T.astype(BF16)
                outs[2][r, :] = dz
                outs[3][...] += _colsum(dy * xh)
                outs[4][...] += _colsum(dy)

    row = pl.BlockSpec((tm, D), lambda i: (i, 0))
    vec = pl.BlockSpec((1, D), lambda i: (0, 0))
    in_specs = [pl.BlockSpec((tm, nc * ks), lambda i: (i, blk))] * nl
    in_specs += [pl.BlockSpec((nc, D, nl * ks), lambda i: (blk, 0, 0), pipeline_mode=pl.Buffered(1)), row]
    args = list(lhs) + [wall, dz_next]
    if partial is not None:
        in_specs.append(row)
        args.append(partial)
    if ln is None:
        out_specs = [row]
        out_shape = [jax.ShapeDtypeStruct((T, D), F32)]
    else:
        in_specs += [row, vec]
        args += [ln[0], ln[1]]
        out_specs = [row, pl.BlockSpec((D, tm), lambda i: (0, i)), row, vec, vec]
        out_shape = [jax.ShapeDtypeStruct((T, D), BF16), jax.ShapeDtypeStruct((D, T), BF16),
                     jax.ShapeDtypeStruct((T, D), F32), jax.ShapeDtypeStruct((1, D), F32),
                     jax.ShapeDtypeStruct((1, D), F32)]
    return _pcall(
        body, name=name, grid=(T // tm,), in_specs=in_specs, out_specs=out_specs, out_shape=out_shape,
        compiler_params=_cp("arbitrary"))(*args)


def _wgrad(lhs_t, rhs, key, name, g_all=None, colsum=False):
    T, N = rhs.shape
    tn = next(c for c in (768, 512, 256) if (N // NCHIP) % c == 0 and LOC[key][1] % c == 0)
    nps = N // NCHIP // tn
    off = LOC[key][1]
    cols = GCOLS[LOC[key][0]]

    def body(*refs):
        l_ref, r_ref = refs[0], refs[1]
        o_ref, t_ref = refs[-2 - colsum], refs[-1]
        o_ref[...] = _dot(l_ref[...], r_ref[...]).astype(BF16)
        if colsum:
            refs[-2][...] = _colsum(r_ref[...].astype(F32))
        t_ref[...] = jnp.zeros_like(t_ref)

    in_specs = [_resident((D, T), 0), pl.BlockSpec((T, tn), lambda j: (0, j))]
    args = [lhs_t, rhs]
    aliases = {}
    if g_all is not None:
        in_specs.append(pl.BlockSpec(memory_space=pl.ANY))
        args.append(g_all)
        aliases = {2: 0}
    out_specs = [pl.BlockSpec((None, D, tn), lambda j: (j // nps, 0, off // tn + j % nps))]
    out_shape = [jax.ShapeDtypeStruct((NCHIP, D, cols), BF16)]
    if colsum:
        out_specs.append(pl.BlockSpec((1, tn), lambda j: (0, j)))
        out_shape.append(jax.ShapeDtypeStruct((1, N), F32))
    out_specs.append(pl.BlockSpec((8, 128), lambda j: (0, 0)))
    out_shape.append(jax.ShapeDtypeStruct((8, 128), F32))
    res = _pcall(
        body, name=name, grid=(N // tn,), in_specs=in_specs, out_specs=out_specs, out_shape=out_shape,
        input_output_aliases=aliases,
        compiler_params=_cp("arbitrary"))(*args)
    return (res[0], res[1]) if colsum else res[0]


def _merge_bwd(dmb, wall, off, proj, ro, co):
    T = dmb.shape[0]
    tm = min(T, 512)
    ks = D // NCHIP

    def body(d_ref, w_ref, gr_ref, gc_ref, ro_ref, co_ref, dro_ref, drot_ref, dco_ref, dcot_ref, dp_ref):
        d = d_ref[...]
        dmg = jnp.concatenate([_dot(d, w_ref[s]) for s in range(NCHIP)], axis=1)
        sr = _sig(gr_ref[...].astype(F32))
        sc = _sig(gc_ref[...].astype(F32))
        dro = dmg * sr
        dco = dmg * sc
        dro_ref[...] = dro.astype(BF16)
        drot_ref[...] = dro.T.astype(BF16)
        dco_ref[...] = dco.astype(BF16)
        dcot_ref[...] = dco.T.astype(BF16)
        dp_ref[:, 0:D] = (dmg * ro_ref[...].astype(F32) * sr * (1.0 - sr)).astype(BF16)
        dp_ref[:, D:2 * D] = (dmg * co_ref[...].astype(F32) * sc * (1.0 - sc)).astype(BF16)

    row = pl.BlockSpec((tm, D), lambda i: (i, 0))
    col = pl.BlockSpec((D, tm), lambda i: (0, i))
    return _pcall(
        body, name="merge_bwd", grid=(T // tm,),
        in_specs=[row, pl.BlockSpec((NCHIP, D, ks), lambda i: (0, 0, off // ks)),
                  pl.BlockSpec((tm, D), lambda i: (i, 8)), pl.BlockSpec((tm, D), lambda i: (i, 9)), row, row],
        out_specs=[row, col, row, col, pl.BlockSpec((tm, 2 * D), lambda i: (i, 4))],
        out_shape=[jax.ShapeDtypeStruct((T, D), BF16), jax.ShapeDtypeStruct((D, T), BF16),
                   jax.ShapeDtypeStruct((T, D), BF16), jax.ShapeDtypeStruct((D, T), BF16),
                   jax.ShapeDtypeStruct((T, INW), BF16)],
        compiler_params=_cp("parallel"))(dmb, wall, proj, proj, ro, co)


def _reto_bwd(dro, wall, off, r, proj, gn_g, dproj):
    T = dro.shape[0]
    tm = min(T, 512)
    hps = H // NCHIP

    def body(d_ref, w_ref, r_ref, g_ref, gn_ref, _, dr_ref, dgn_ref, dp_ref):
        _zero_first(pl.program_id(1) == 0, dgn_ref)
        dri = _dot(d_ref[...], w_ref[...])
        rr = r_ref[...]
        mu = jnp.mean(rr, axis=-1, keepdims=True)
        xc = rr - mu
        var = jnp.mean(xc * xc, axis=-1, keepdims=True)
        rstd = lax.rsqrt(var + EPS)
        rn = xc * rstd
        gn = gn_ref[...]
        g = g_ref[...].astype(F32)
        sg = _sig(g)
        dy = dri * (g * sg)
        dp_ref[...] = (dri * (rn * gn) * (sg * (1.0 + g * (1.0 - sg)))).astype(BF16)
        dgn_ref[...] += _colsum(dy * rn)
        dr_ref[...] = _ln_bwd(dy, rn, rstd, gn).astype(BF16)

    return _pcall(
        body, name="reto_bwd", grid=(H, T // tm),
        in_specs=[pl.BlockSpec((tm, D), lambda j, i: (i, 0)),
                  pl.BlockSpec((None, D, DV), lambda j, i: (j // hps, 0, off // DV + j % hps)),
                  pl.BlockSpec((tm, DV), lambda j, i: (i, j)),
                  pl.BlockSpec((tm, DV), lambda j, i: (i, 2 * VW // DV + j)),
                  pl.BlockSpec((1, DV), lambda j, i: (0, j)),
                  pl.BlockSpec(memory_space=pl.ANY)],
        out_specs=[pl.BlockSpec((tm, DV), lambda j, i: (i, j)), pl.BlockSpec((1, DV), lambda j, i: (0, j)),
                   pl.BlockSpec((tm, DV), lambda j, i: (i, 2 * VW // DV + j))],
        out_shape=[jax.ShapeDtypeStruct((T, VW), BF16), jax.ShapeDtypeStruct((1, VW), F32),
                   jax.ShapeDtypeStruct((T, INW), BF16)],
        input_output_aliases={5: 2},
        compiler_params=_cp("arbitrary", "arbitrary"))(dro, wall, r, proj, gn_g, dproj)


def _retention_bwd(proj, dr, states, cos_t, sin_t, dm_t, xi_t, zeta_t, cds, dproj):
    T = proj.shape[0]
    n = T // CH // RET_CPS
    tr = RET_CPS * CH
    scale = DK ** -0.5

    def body(q_ref, k_ref, v_ref, dr_ref, st_ref, cos_ref, sin_ref, dm_ref, xi_ref, zt_ref, _, dp_ref, ds):
        @pl.when(pl.program_id(0) == 0)
        def _():
            ds[...] = jnp.zeros_like(ds)

        def unrope(d, t):
            return d * cos_ref[t, :] + pltpu.roll(d * sin_ref[t, :], DK // 2, 1)

        for h in range(H):
            rows = slice(h * DK, (h + 1) * DK)
            dm = dm_ref[h]
            zt = zt_ref[h]
            ds_prev = ds[rows, :]
            for j in reversed(range(RET_CPS)):
                t = slice(j * CH, (j + 1) * CH)
                q = q_ref[t, h * DK:(h + 1) * DK]
                k = k_ref[t, h * DK:(h + 1) * DK]
                v = v_ref[t, h * DV:(h + 1) * DV]
                d_r = dr_ref[t, h * DV:(h + 1) * DV]
                s_b = st_ref[j, rows, :]
                sc = _dot_nt(q, k) * dm
                dsc = _dot_nt(d_r, v) * dm
                drx = (d_r.astype(F32) * xi_ref[h]).astype(BF16)
                ds_b = ds_prev.astype(BF16)
                kz = (k.astype(F32) * zt).astype(BF16)
                dq = _dot(dsc.astype(BF16), k) + _dot_nt(drx, s_b)
                dk = _dot(dsc.T.astype(BF16), q) + _dot_nt(v, ds_b) * zt
                dv = _dot(sc.T.astype(BF16), d_r) + _dot(kz, ds_b)
                ds_prev = cds[h] * ds_prev + _dot(q.astype(F32).T.astype(BF16), drx)
                dp_ref[t, h * DK:(h + 1) * DK] = unrope(dq * scale, t).astype(BF16)
                dp_ref[t, D + h * DK:D + (h + 1) * DK] = unrope(dk, t).astype(BF16)
                dp_ref[t, 2 * D + h * DV:2 * D + (h + 1) * DV] = dv.astype(BF16)
            ds[rows, :] = ds_prev

    rv = lambda c: n - 1 - c
    full3 = lambda shp: pl.BlockSpec(shp, lambda c: (0, 0, 0))
    return _pcall(
        body, name="retention_bwd", grid=(n,),
        in_specs=[pl.BlockSpec((tr, D), lambda c: (rv(c), 0)),
                  pl.BlockSpec((tr, D), lambda c: (rv(c), 1)),
                  pl.BlockSpec((tr, VW), lambda c: (rv(c), 1)),
                  pl.BlockSpec((tr, VW), lambda c: (rv(c), 0)),
                  pl.BlockSpec((RET_CPS, H * DK, DV), lambda c: (rv(c), 0, 0)),
                  pl.BlockSpec((tr, DK), lambda c: (rv(c), 0)),
                  pl.BlockSpec((tr, DK), lambda c: (rv(c), 0)),
                  full3((H, CH, CH)), full3((H, CH, DV)), full3((H, CH, DK)),
                  pl.BlockSpec(memory_space=pl.ANY)],
        out_specs=pl.BlockSpec((tr, 2 * D + VW), lambda c: (rv(c), 0)),
        out_shape=jax.ShapeDtypeStruct((T, INW), BF16),
        input_output_aliases={10: 0},
        scratch_shapes=[pltpu.VMEM((H * DK, DV), F32)],
        compiler_params=_cp("arbitrary"))(proj, proj, proj, dr, states, cos_t, sin_t, dm_t, xi_t, zeta_t, dproj)


def _convo_bwd(dco, wall, off, u1, ln_g, ln_b):
    T = dco.shape[0]
    tm = min(T, 512)
    ks = D // NCHIP

    def body(d_ref, w_ref, u1_ref, g_ref, b_ref, du1_ref, dg_ref, db_ref, dcb_ref):
        _zero_first(pl.program_id(0) == 0, dg_ref, db_ref, dcb_ref)
        d = d_ref[...]
        du3 = jnp.concatenate([_dot(d, w_ref[s]) for s in range(NCHIP)], axis=1)
        gam = g_ref[...]
        u2, xh, rstd = _ln_fwd(u1_ref[...], gam, b_ref[...])
        sg = _sig(u2)
        du2 = du3 * (sg * (1.0 + u2 * (1.0 - sg)))
        du1 = _ln_bwd(du2, xh, rstd, gam)
        du1_ref[...] = du1
        dg_ref[...] += _colsum(du2 * xh)
        db_ref[...] += _colsum(du2)
        dcb_ref[...] += _colsum(du1)

    row = pl.BlockSpec((tm, D), lambda i: (i, 0))
    vec = pl.BlockSpec((1, D), lambda i: (0, 0))
    return _pcall(
        body, name="convo_bwd", grid=(T // tm,),
        in_specs=[row, pl.BlockSpec((NCHIP, D, ks), lambda i: (0, 0, off // ks)), row, vec, vec],
        out_specs=[row, vec, vec, vec],
        out_shape=[jax.ShapeDtypeStruct((T, D), F32)] + [jax.ShapeDtypeStruct((1, D), F32)] * 3,
        compiler_params=_cp("arbitrary"))(dco, wall, u1, ln_g, ln_b)


def _conv_bwd(du1, proj, conv_k, dproj):
    T = du1.shape[0]
    tt = min(T, CONV_TT)
    nt = T // tt
    ca, cb = 6, 7

    def body(d_ref, dn_ref, a_ref, b_ref, pa_ref, pb_ref, k_ref, _, dp_ref, dk_ref, win_u, win_d, sh_u, sh_d):
        i = pl.program_id(0)
        a, sb = _glu(a_ref, b_ref)
        win_u[HALO:tt + HALO, :] = a * sb
        pa, psb = _glu(pa_ref, pb_ref, slice(tt - HALO, tt))
        win_u[0:HALO, :] = jnp.where(i > 0, pa * psb, 0.0)
        win_d[0:tt, :] = d_ref[...]
        win_d[tt:tt + HALO, :] = jnp.where(i < nt - 1, dn_ref[0:HALO, :], 0.0)
        _shift_copies(win_u, sh_u, tt + HALO)
        _shift_copies(win_d, sh_d, tt + HALO)

        @pl.when(i == 0)
        def _():
            dk_ref[...] = jnp.zeros_like(dk_ref)

        for c0 in range(0, D, CONV_CB):
            cs = slice(c0, c0 + CONV_CB)
            for r0 in range(0, tt, CONV_SB):
                acc = jnp.zeros((CONV_SB, CONV_CB), F32)
                for w in range(CONV_W):
                    st = r0 + (CONV_W - 1) - w
                    acc += _tap(win_d, sh_d, st, CONV_SB, cs) * k_ref[w:w + 1, cs]
                aa = a_ref[r0:r0 + CONV_SB, cs].astype(F32)
                ss = _sig(b_ref[r0:r0 + CONV_SB, cs].astype(F32))
                dp_ref[r0:r0 + CONV_SB, cs] = (acc * ss).astype(BF16)
                dp_ref[r0:r0 + CONV_SB, c0 + D:c0 + D + CONV_CB] = (acc * aa * ss * (1.0 - ss)).astype(BF16)
        for c0 in range(0, D, CONV_CB):
            cs = slice(c0, c0 + CONV_CB)
            for w in range(CONV_W):
                acc = jnp.zeros((CONV_SB, CONV_CB), F32)
                for r0 in range(0, tt, CONV_SB):
                    st = r0 + HALO - (CONV_W - 1) + w
                    acc += win_d[r0:r0 + CONV_SB, cs] * _tap(win_u, sh_u, st, CONV_SB, cs)
                dk_ref[w:w + 1, cs] += _colsum(acc)

    blk = lambda f, c: pl.BlockSpec((tt, D), lambda i: (f(i), c))
    cur = lambda i: i
    prv = lambda i: jnp.maximum(i - 1, 0)
    nxt = lambda i: jnp.minimum(i + 1, nt - 1)
    return _pcall(
        body, name="conv_bwd", grid=(nt,),
        in_specs=[blk(cur, 0), blk(nxt, 0), blk(cur, ca), blk(cur, cb), blk(prv, ca), blk(prv, cb),
                  pl.BlockSpec((CONV_W, D), lambda i: (0, 0)), pl.BlockSpec(memory_space=pl.ANY)],
        out_specs=[pl.BlockSpec((tt, 2 * D), lambda i: (i, 3)), pl.BlockSpec((HALO, D), lambda i: (0, 0))],
        out_shape=[jax.ShapeDtypeStruct((T, INW), BF16), jax.ShapeDtypeStruct((HALO, D), F32)],
        input_output_aliases={7: 0},
        scratch_shapes=[pltpu.VMEM((tt + HALO + CONV_PAD, D), F32), pltpu.VMEM((tt + HALO + CONV_PAD, D), F32),
                        pltpu.VMEM((SUB - 1, tt + HALO, D), F32), pltpu.VMEM((SUB - 1, tt + HALO, D), F32)],
        compiler_params=_cp("arbitrary"))(du1, du1, proj, proj, proj, proj, conv_k, dproj)


def _local_step(x, target, gathered, slabs, sp, kvec, pos_c, pos_sc, adam):
    T = x.shape[0]
    cos_t, sin_t = _rope_tables(T)
    dm_t, xi_t, zeta_t, cds = _decay_tables()
    wts = [_with_own_slab(g, s) for g, s in zip(gathered[:-1], slabs[:-1])] + [None]
    wa = lambda key: wts[LOC_W[key][0]]
    wo = lambda key: LOC_W[key][1]
    _ORDER.active, _ORDER.token = True, None

    xb, xt = _cast_t(x)
    a1, b1, h1 = _ffn_up(xb, wa("g1"), wo("g1"), wo("u1"), "ffn1_up")
    z1, x1, x1b, x1t = _proj_ln(h1, wa("d1"), wo("d1"), x, sp["ln1_g"], sp["ln1_b"], 0.5, "ffn1_down_ln")
    proj = _inproj(x1b, wa("w_in"), wo("w_in"), sp["b_in"], cos_t, sin_t)
    r, ret_in, states = _retention_fwd(proj, sp["ret_gn_g"], dm_t, xi_t, zeta_t, cds)
    kall = _sum_devices(_all_gather_small(kvec, "gather_conv_k"), "sum_conv_k")
    sp = dict(sp, conv_k=kall.reshape(-1)[:CONV_W * D].reshape(CONV_W, D))
    u1, u3 = _conv_fwd(proj, sp["conv_k"], sp["conv_b"], sp["conv_ln_g"], sp["conv_ln_b"])
    ro, co, merged = _merge(ret_in, u3, proj, wa("w_ret_o"), wo("w_ret_o"), wo("w_conv_o"))
    z2, x2, x2b, x2t = _proj_ln(merged, wa("w_out"), wo("w_out"), x1, sp["ln2_g"], sp["ln2_b"], 1.0, "out_proj_ln")
    wts[-1] = _with_own_slab(gathered[-1], slabs[-1], after=x2[0, 0])
    a2, b2, h2 = _ffn_up(x2b, wa("g2"), wo("g2"), wo("u2"), "ffn2_up")
    (z3,) = _proj_ln(h2, wa("d2"), wo("d2"), x2, sp["ln3_g"], sp["ln3_b"], 0.5, "ffn2_down", want_b=False)

    sg = {}
    rs = {}
    loss, df2b, df2t, dz3, sg["ln3_g"], sg["ln3_b"] = _loss_ln_bwd(z3, sp["ln3_g"], sp["ln3_b"], target, 0.5)
    da2, db2 = _ffn_bwd_h(df2b, wa("d2"), wo("d2"), a2, b2, "ffn2_bwd_h")
    g4 = _wgrad(df2t, h2, "d2", "wgrad_d2")
    g4 = _wgrad(x2t, da2, "g2", "wgrad_g2", g4)
    g4 = _wgrad(x2t, db2, "u2", "wgrad_u2", g4)
    rs[4] = _ReduceScatter(g4, 4, pos_c, pos_sc)
    dmb, dmt, dz2, sg["ln2_g"], sg["ln2_b"] = _dx_bwd(
        [da2, db2], [wo("g2"), wo("u2")], wa("g2"), dz3, "ffn2_dx_ln", ln=(z2, sp["ln2_g"], 1.0))
    rs[4].phase2()
    g3 = _wgrad(dmt, merged, "w_out", "wgrad_out")
    dro, drot, dco, dcot, dproj = _merge_bwd(dmb, wa("w_out"), wo("w_out"), proj, ro, co)
    g3 = _wgrad(drot, ret_in, "w_ret_o", "wgrad_ret_o", g3)
    g3 = _wgrad(dcot, u3, "w_conv_o", "wgrad_conv_o", g3)
    rs[3] = _ReduceScatter(g3, 3, pos_c, pos_sc)
    dr, sg["ret_gn_g"], dproj = _reto_bwd(dro, wa("w_ret_o"), wo("w_ret_o"), r, proj, sp["ret_gn_g"], dproj)
    rs[4].phase3()
    rs[3].phase2()
    dproj = _retention_bwd(proj, dr, states, cos_t, sin_t, dm_t, xi_t, zeta_t, cds, dproj)
    du1, sg["conv_ln_g"], sg["conv_ln_b"], sg["conv_b"] = _convo_bwd(
        dco, wa("w_conv_o"), wo("w_conv_o"), u1, sp["conv_ln_g"], sp["conv_ln_b"])
    dproj, dck = _conv_bwd(du1, proj, sp["conv_k"], dproj)
    sg["conv_k"] = dck[:CONV_W]
    adam(4, rs[4].result())
    rs[3].phase3()
    g2, sg["b_in"] = _wgrad(x1t, dproj, "w_in", "wgrad_in", colsum=True)
    rs[2] = _ReduceScatter(g2, 2, pos_c, pos_sc)
    dx_part = _dx_partial(dproj, wa("w_in"), (0, 1), "mixer_dx_part")
    df1b, df1t, dz1, sg["ln1_g"], sg["ln1_b"] = _dx_bwd(
        [dproj], [wo("w_in")], wa("w_in"), dz2, "mixer_dx_ln", ln=(z1, sp["ln1_g"], 0.5), chips=(2, 3),
        partial=dx_part)
    adam(3, rs[3].result())
    rs[2].phase2()
    shapes = {n: sg[n].shape for n in SMALL + ["conv_k"]}
    small_parts = _all_gather_small(_pack_small(sg, loss, SMALL_ROWS), "gather_small")
    da1, db1 = _ffn_bwd_h(df1b, wa("d1"), wo("d1"), a1, b1, "ffn1_bwd_h")
    small_sum = _sum_devices(small_parts, "sum_small")
    g1 = _wgrad(df1t, h1, "d1", "wgrad_d1")
    rs[1] = _ReduceScatter(g1, 1, pos_c, pos_sc)
    g0 = _wgrad(xt, da1, "g1", "wgrad_g1")
    g0 = _wgrad(xt, db1, "u1", "wgrad_u1", g0)
    rs[0] = _ReduceScatter(g0, 0, pos_c, pos_sc)
    rs[2].phase3()
    rs[1].phase2()
    rs[0].phase2()
    (grad_x,) = _dx_bwd([da1, db1], [wo("g1"), wo("u1")], wa("g1"), dz1, "ffn1_dx")
    adam(2, rs[2].result())
    rs[1].phase3()
    rs[0].phase3()
    adam(1, rs[1].result())
    adam(0, rs[0].result())
    _ORDER.active = False
    return grad_x, small_sum, shapes


MESH = pl.DeviceIdType.MESH
ANY = pl.BlockSpec(memory_space=pl.ANY)
HALF = D // 2


def _place():
    x, y, c = lax.axis_index("x"), lax.axis_index("y"), lax.axis_index("c")
    chips = [(1 - x, y), (x, 1 - y), (1 - x, 1 - y)]
    return x, y, c, chips


GATHER_PIECES = 4
GATHER_ID = 1


def _gather_weights(wloc, name):
    w_ref = jax.new_ref(wloc, memory_space=pltpu.MemorySpace.HBM)
    o_ref = jax.empty_ref(jax.ShapeDtypeStruct((NCHIP, 2, HALF, wloc.shape[-1]), BF16),
                          memory_space=pltpu.MemorySpace.HBM)
    dma = pltpu.SemaphoreType.DMA

    nq = GATHER_PIECES
    rows = HALF // nq

    @pl.kernel(mesh=plsc.ScalarSubcoreMesh(axis_name="sc", num_cores=1), name=name,
               scratch_types=(dma((2 * nq,)), dma((2 * nq,)), dma((3 * nq,)), dma((3 * nq,)),
                              dma((nq,)), dma((nq,))),
               compiler_params=pltpu.CompilerParams(collective_id=GATHER_ID))
    def launch(s1, r1, s2, r2, s3, r3):
        x, y, c, _ = _place()
        me = 2 * x + y
        sib = (x, y, 1 - c)
        x_nbr, y_nbr = (1 - x, y, c), (x, 1 - y, c)
        x_chip, y_chip, d_chip = 2 * (1 - x) + y, 2 * x + (1 - y), 2 * (1 - x) + (1 - y)
        _handshake([sib, x_nbr, y_nbr])

        def rc(src, dst, ss, rs, dev):
            return pltpu.make_async_remote_copy(src_ref=src, dst_ref=dst, send_sem=ss, recv_sem=rs,
                                                device_id=dev, device_id_type=MESH)

        def piece(ref, q):
            return ref.at[pl.ds(q * rows, rows)]

        sends = []
        for q in range(nq):
            for j, nbr in enumerate((x_nbr, y_nbr)):
                sends.append(rc(piece(w_ref.at[c], q), piece(o_ref.at[me, c], q),
                                s1.at[j * nq + q], r1.at[j * nq + q], nbr))
                sends[-1].start()
        on_chip = c * x_chip + (1 - c) * y_chip
        other_chip = c * y_chip + (1 - c) * x_chip
        on_to = (c * x + (1 - c) * (1 - x), c * (1 - y) + (1 - c) * y, c)
        for q in range(nq):
            slot = piece(o_ref.at[on_chip, c], q)
            rc(slot, slot, s1.at[(1 - c) * nq + q], r1.at[(1 - c) * nq + q], sib).wait_recv()
            sends.append(rc(slot, slot, s3.at[q], r3.at[q], on_to))
            sends[-1].start()
            sends.append(rc(slot, slot, s2.at[q], r2.at[q], sib))
            sends[-1].start()
        for q in range(nq):
            slot = piece(o_ref.at[other_chip, c], q)
            rc(slot, slot, s1.at[c * nq + q], r1.at[c * nq + q], sib).wait_recv()
            sends.append(rc(slot, slot, s2.at[nq + q], r2.at[nq + q], sib))
            sends[-1].start()
        for q in range(nq):
            slot = piece(o_ref.at[d_chip, c], q)
            rc(slot, slot, s3.at[q], r3.at[q], sib).wait_recv()
            sends.append(rc(slot, slot, s2.at[2 * nq + q], r2.at[2 * nq + q], sib))
            sends[-1].start()
        for j, chip in enumerate([other_chip, on_chip, d_chip]):
            for q in range(nq):
                slot = piece(o_ref.at[chip, 1 - c], q)
                rc(slot, slot, s2.at[j * nq + q], r2.at[j * nq + q], sib).wait_recv()
        for cp in sends:
            cp.wait_send()

    launch()
    return o_ref[...]


def _with_own_slab(gathered, wloc, after=None):
    own = 2 * lax.axis_index("x") + lax.axis_index("y")
    if after is not None:
        wloc = wloc + (after * 0.0).astype(wloc.dtype)
    cols = wloc.shape[-1]
    return lax.dynamic_update_slice(gathered, wloc[None], (own, 0, 0, 0)).reshape(NCHIP, D, cols)


PAIR_ID = 2
CHIP_ID = 3
HBM = pltpu.MemorySpace.HBM


def _sequencer(name, collective_id, n_sems):
    dma = pltpu.SemaphoreType.DMA
    return pl.kernel(mesh=plsc.ScalarSubcoreMesh(axis_name="sc", num_cores=1), name=name,
                     scratch_types=(dma((n_sems,)), dma((n_sems,))),
                     compiler_params=pltpu.CompilerParams(collective_id=collective_id))


def _handshake(peers):
    barrier = pltpu.get_barrier_semaphore()
    for peer in peers:
        pl.semaphore_signal(barrier, inc=1, device_id=peer, device_id_type=MESH)
    pl.semaphore_wait(barrier, len(peers))


def _pair_exchange(g5, name):
    _, _, hr, cols = g5.shape
    g_ref = jax.new_ref(g5, memory_space=HBM)
    o_ref = jax.empty_ref(jax.ShapeDtypeStruct((NCHIP, hr, cols), g5.dtype), memory_space=HBM)

    @_sequencer(name, PAIR_ID, NCHIP)
    def launch(ss, rs):
        x, y, c, _ = _place()
        sib = (x, y, 1 - c)
        _handshake([sib])
        cps = [pltpu.make_async_remote_copy(src_ref=g_ref.at[j, 1 - c], dst_ref=o_ref.at[j], send_sem=ss.at[j],
                                            recv_sem=rs.at[j], device_id=sib, device_id_type=MESH)
               for j in range(NCHIP)]
        for cp in cps:
            cp.start()
        for cp in cps:
            cp.wait()

    launch()
    return o_ref[...]


RS_TR = 256


def _pair_sum(pos, g5, got, name):
    _, _, hr, cols = g5.shape

    def body(pos_ref, g_ref, r_ref, o_ref):
        o_ref[...] = (g_ref[...].astype(F32) + r_ref[...].astype(F32)).astype(BF16)

    return _pcall(
        body, name=name, scalar_prefetch=1, grid=(NCHIP, hr // RS_TR),
        in_specs=[pl.BlockSpec((None, None, RS_TR, cols), lambda j, i, p: (j, p[0], i, 0)),
                  pl.BlockSpec((None, RS_TR, cols), lambda j, i, p: (j, i, 0))],
        out_specs=pl.BlockSpec((None, RS_TR, cols), lambda j, i, p: (j, i, 0)),
        out_shape=jax.ShapeDtypeStruct((NCHIP, hr, cols), BF16),
        compiler_params=_cp("parallel", "parallel"))(pos, g5, got)


def _chip_exchange(pb, name):
    _, hr, cols = pb.shape
    p_ref = jax.new_ref(pb, memory_space=HBM)
    o_ref = jax.empty_ref(jax.ShapeDtypeStruct((3, hr, cols), BF16), memory_space=HBM)

    @_sequencer(name, CHIP_ID, 3)
    def launch(ss, rs):
        x, y, c, chips = _place()
        _handshake([(px, py, c) for px, py in chips])
        cps = [pltpu.make_async_remote_copy(src_ref=p_ref.at[2 * px + py], dst_ref=o_ref.at[j], send_sem=ss.at[j],
                                            recv_sem=rs.at[j], device_id=(px, py, c), device_id_type=MESH)
               for j, (px, py) in enumerate(chips)]
        for cp in cps:
            cp.start()
        for cp in cps:
            cp.wait()

    launch()
    return o_ref[...]


def _chip_sum(pos, g5, got, peers, name):
    _, _, hr, cols = g5.shape

    def body(pos_ref, g_ref, r_ref, p_ref, o_ref, t_ref):
        acc = g_ref[...].astype(F32) + r_ref[...].astype(F32)
        for j in range(3):
            acc += p_ref[j].astype(F32)
        o_ref[...] = acc
        t_ref[...] = jnp.zeros_like(t_ref)

    return _pcall(
        body, name=name, scalar_prefetch=1, grid=(hr // RS_TR,),
        in_specs=[pl.BlockSpec((None, None, RS_TR, cols), lambda i, p: (p[0], p[1], i, 0)),
                  pl.BlockSpec((None, RS_TR, cols), lambda i, p: (p[0], i, 0)),
                  pl.BlockSpec((3, RS_TR, cols), lambda i, p: (0, i, 0))],
        out_specs=[pl.BlockSpec((None, RS_TR, cols), lambda i, p: (p[1], i, 0)),
                   pl.BlockSpec((8, 128), lambda i, p: (0, 0))],
        out_shape=[jax.ShapeDtypeStruct((2, hr, cols), F32), jax.ShapeDtypeStruct((8, 128), F32)],
        compiler_params=_cp("arbitrary"))(pos, g5, got, peers)


def _pair_share(gsum, name):
    g_ref = jax.new_ref(gsum, memory_space=HBM)

    @_sequencer(name, PAIR_ID, 1)
    def launch(ss, rs):
        x, y, c, _ = _place()
        sib = (x, y, 1 - c)
        _handshake([sib])
        cp = pltpu.make_async_remote_copy(src_ref=g_ref.at[c], dst_ref=g_ref.at[c], send_sem=ss.at[0],
                                          recv_sem=rs.at[0], device_id=sib, device_id_type=MESH)
        cp.start()
        cp.wait_send()
        pltpu.make_async_remote_copy(src_ref=g_ref.at[1 - c], dst_ref=g_ref.at[1 - c], send_sem=ss.at[0],
                                     recv_sem=rs.at[0], device_id=sib, device_id_type=MESH).wait_recv()

    launch()
    return g_ref[...]


class _ReduceScatter:
    def __init__(self, g_arr, gi, pos_c, pos_sc):
        _, rows, cols = g_arr.shape
        self.g5 = g_arr.reshape(NCHIP, 2, rows // 2, cols)
        self.gi, self.pos_c, self.pos_sc = gi, pos_c, pos_sc
        self.got = _pair_exchange(self.g5, f"pair_exchange_{gi}")

    def phase2(self):
        pb = _pair_sum(self.pos_c, self.g5, self.got, f"pair_sum_{self.gi}")
        self.peers = _chip_exchange(pb, f"chip_exchange_{self.gi}")

    def phase3(self):
        gsum, _ = _chip_sum(self.pos_sc, self.g5, self.got, self.peers, f"chip_sum_{self.gi}")
        self.full = _pair_share(gsum, f"pair_share_{self.gi}")

    def result(self):
        _, hr, cols = self.full.shape
        return self.full.reshape(2 * hr, cols)


SMALL_ROWS = 56


ALL_ID = 4


def _all_gather_small(vec, name):
    v_ref = jax.new_ref(vec, memory_space=HBM)
    o_ref = jax.empty_ref(jax.ShapeDtypeStruct((8, SMALL_ROWS, D), F32), memory_space=HBM)

    @_sequencer(name, ALL_ID, 8)
    def launch(ss, rs):
        x, y, c, _ = _place()
        me = 4 * x + 2 * y + c
        flip = lambda v, bit: 1 - v if bit else v
        peers = [(flip(x, m >> 2), flip(y, (m >> 1) & 1), flip(c, m & 1)) for m in range(1, 8)]
        _handshake(peers)
        mine = pltpu.make_async_copy(v_ref, o_ref.at[me], ss.at[7])
        mine.start()
        cps = [pltpu.make_async_remote_copy(src_ref=v_ref, dst_ref=o_ref.at[me], send_sem=ss.at[k],
                                            recv_sem=rs.at[k], device_id=peer, device_id_type=MESH)
               for k, peer in enumerate(peers)]
        for cp in cps:
            cp.start()
        for cp in cps:
            cp.wait()
        mine.wait()

    launch()
    return o_ref[...]


def _sum_devices(parts, name):
    def body(p_ref, o_ref):
        acc = p_ref[0]
        for d in range(1, 8):
            acc += p_ref[d]
        o_ref[...] = acc

    return _pcall(
        body, name=name, grid=(SMALL_ROWS // 8,),
        in_specs=[pl.BlockSpec((8, 8, D), lambda i: (0, i, 0))],
        out_specs=pl.BlockSpec((8, D), lambda i: (i, 0)),
        out_shape=jax.ShapeDtypeStruct((SMALL_ROWS, D), F32),
        compiler_params=_cp("parallel"))(parts)


def _adamw_math(w, g, m, v):
    m2 = ADAM_B1 * m + (1.0 - ADAM_B1) * g
    v2 = ADAM_B2 * v + (1.0 - ADAM_B2) * (g * g)
    m_hat = m2 / (1.0 - ADAM_B1 ** ADAM_STEP)
    v_hat = v2 / (1.0 - ADAM_B2 ** ADAM_STEP)
    delta = -ADAM_LR * (m_hat / (jnp.sqrt(v_hat) + ADAM_EPS) + ADAM_WD * w)
    return delta, m2, v2


def _adamw(w, g, m, v, name, g_block=None):
    R, C = w.shape
    tr = R
    gw_hint = C if g_block is None else g_block[0]
    for cand in (512, 352, 256, 176, 128, 64, 32, 16, 8):
        if R % cand == 0 and cand * max(C, gw_hint) * 4 <= (2 << 20):
            tr = cand
            break
    gw, gi = (C, 0) if g_block is None else g_block

    def body(w_ref, g_ref, m_ref, v_ref, go_ref, d_ref, mo_ref, vo_ref):
        g = g_ref[:, 0:C]
        d, m2, v2 = _adamw_math(w_ref[...], g, m_ref[...], v_ref[...])
        go_ref[...] = g
        d_ref[...] = d
        mo_ref[...] = m2
        vo_ref[...] = v2

    spec = pl.BlockSpec((tr, C), lambda i: (i, 0))
    return _pcall(
        body, name=name, grid=(R // tr,),
        in_specs=[spec, pl.BlockSpec((tr, gw), lambda i: (i, gi)), spec, spec],
        out_specs=[spec] * 4, out_shape=[jax.ShapeDtypeStruct((R, C), F32)] * 4,
        compiler_params=_cp("parallel"))(w, g, m, v)


BIG = ["ffn1_w_gate", "ffn1_w_up", "ffn1_w_down", "w_in", "w_ret_o", "w_conv_o", "w_out",
       "ffn2_w_gate", "ffn2_w_up", "ffn2_w_down"]
SLAB = {"ffn1_w_gate": "g1", "ffn1_w_up": "u1", "ffn1_w_down": "d1", "w_in": "w_in", "w_ret_o": "w_ret_o",
        "w_conv_o": "w_conv_o", "w_out": "w_out", "ffn2_w_gate": "g2", "ffn2_w_up": "u2", "ffn2_w_down": "d2"}
TRANSPOSED = {"ffn1_w_down", "ffn2_w_down", "w_ret_o", "w_conv_o", "w_out"}
MINOR_ROWS = {"ffn1_w_gate", "ffn1_w_up", "ffn2_w_gate", "ffn2_w_up"}
SMALL = ["ln1_g", "ln1_b", "ln2_g", "ln2_b", "ln3_g", "ln3_b", "conv_ln_g", "conv_ln_b", "conv_b",
         "ret_gn_g", "b_in"]
ORDER = ["ffn1_w_gate", "ffn1_w_up", "ffn1_w_down", "ln1_g", "ln1_b", "w_in", "b_in", "ret_gn_g", "conv_k",
         "conv_b", "conv_ln_g", "conv_ln_b", "w_ret_o", "w_conv_o", "w_out", "ln2_g", "ln2_b",
         "ffn2_w_gate", "ffn2_w_up", "ffn2_w_down", "ln3_g", "ln3_b"]


def _slab_width(name):
    return WIDTH[SLAB[name]]


def _pack_group(weights, keys):
    by_key = {SLAB[n]: n for n in BIG}
    parts = []
    for key in keys:
        w = weights[by_key[key]]
        w = w.T if by_key[key] in TRANSPOSED else w
        parts.append(jnp.pad(w, ((0, 0), (0, WIDTH[key] - w.shape[1]))))
    return jnp.concatenate(parts, axis=1).astype(BF16)


def _pack_small(vals, loss, rows):
    flat = jnp.concatenate([vals[n].reshape(-1) for n in SMALL] + [vals["conv_k"].reshape(-1), loss.reshape(-1)])
    return jnp.pad(flat, (0, rows * D - flat.shape[0])).reshape(rows, D)


def _unpack_small(arr, shapes):
    flat = arr.reshape(-1)
    out, pos = {}, 0
    for n in SMALL + ["conv_k"]:
        size = int(np.prod(shapes[n]))
        out[n] = flat[pos:pos + size].reshape(shapes[n])
        pos += size
    return out, flat[pos]


def kernel(x, ffn1_w_gate, ffn1_w_up, ffn1_w_down, ln1_g, ln1_b, w_in, b_in, ret_gn_g, conv_k, conv_b, conv_ln_g, conv_ln_b, w_ret_o, w_conv_o, w_out, ln2_g, ln2_b, ffn2_w_gate, ffn2_w_up, ffn2_w_down, ln3_g, ln3_b, loss_target, m_ffn1_w_gate, m_ffn1_w_up, m_ffn1_w_down, m_ln1_g, m_ln1_b, m_w_in, m_b_in, m_ret_gn_g, m_conv_k, m_conv_b, m_conv_ln_g, m_conv_ln_b, m_w_ret_o, m_w_conv_o, m_w_out, m_ln2_g, m_ln2_b, m_ffn2_w_gate, m_ffn2_w_up, m_ffn2_w_down, m_ln3_g, m_ln3_b, v_ffn1_w_gate, v_ffn1_w_up, v_ffn1_w_down, v_ln1_g, v_ln1_b, v_w_in, v_b_in, v_ret_gn_g, v_conv_k, v_conv_b, v_conv_ln_g, v_conv_ln_b, v_w_ret_o, v_w_conv_o, v_w_out, v_ln2_g, v_ln2_b, v_ffn2_w_gate, v_ffn2_w_up, v_ffn2_w_down, v_ln3_g, v_ln3_b):
    args = dict(locals())
    w = {n: args[n] for n in ORDER}
    m = {n: args["m_" + n] for n in ORDER}
    v = {n: args["v_" + n] for n in ORDER}
    xi, yi, ci = lax.axis_index("x"), lax.axis_index("y"), lax.axis_index("c")
    chip = 2 * xi + yi

    shards = {n: w[n][0] for n in BIG}
    slabs = [_pack_group(shards, keys) for keys in GATHER_GROUPS]
    slabs = [slab.reshape(2, HALF, slab.shape[1]) for slab in slabs]
    gathered = [_gather_weights(slab, f"gather_{gi}") for gi, slab in enumerate(slabs)]

    sp = {n: w[n] for n in SMALL}
    kpad = jnp.zeros((CONV_W, D), F32)
    kpad = lax.dynamic_update_slice(kpad, w["conv_k"][0, :, 0, :] * jnp.where(ci == 0, 1.0, 0.0), (0, chip * (D // NCHIP)))
    kvec = jnp.pad(kpad.reshape(-1), (0, SMALL_ROWS * D - CONV_W * D)).reshape(SMALL_ROWS, D)
    pos_c = jnp.reshape(ci, (1,)).astype(jnp.int32)
    pos_sc = jnp.stack([chip, ci]).astype(jnp.int32)
    out = {}

    def adam(gi, slab):
        for n in BIG:
            (g_of, off), width = LOC[SLAB[n]], _slab_width(n)
            if g_of != gi:
                continue
            w2 = w[n][0]
            if n in TRANSPOSED:
                res = _adamw(w2, slab[:, off:off + w2.shape[0]].T, m[n][0], v[n][0], "adamw_" + n)
                out[n] = [r[None] for r in res]
            elif n in MINOR_ROWS:
                res = _adamw(w2.T, slab[:, off:off + w2.shape[1]].T, m[n][0].T, v[n][0].T, "adamw_" + n)
                out[n] = [r.T[None] for r in res]
            else:
                res = _adamw(w2, slab, m[n][0], v[n][0], "adamw_" + n, g_block=(width, off // width))
                out[n] = [r[None] for r in res]

    grad_x, small_sum, shapes = _local_step(x[0], loss_target[0], gathered, slabs, sp, kvec, pos_c, pos_sc, adam)
    small, total = _unpack_small(small_sum, shapes)

    for n in SMALL:
        res = _adamw(w[n], small[n], m[n], v[n], "adamw_" + n)
        out[n] = list(res)
    gk = lax.dynamic_slice(small["conv_k"], (0, chip * (D // NCHIP)), (CONV_W, D // NCHIP))
    res = _adamw(w["conv_k"][0, :, 0, :], gk, m["conv_k"][0, :, 0, :], v["conv_k"][0, :, 0, :], "adamw_conv_k")
    out["conv_k"] = [r[None, :, None, :] for r in res]

    grads = [out[n][0] for n in ORDER]
    deltas = [out[n][1] for n in ORDER]
    new_m = [out[n][2] for n in ORDER]
    new_v = [out[n][3] for n in ORDER]
    return (total, grad_x[None], *grads, *deltas, *new_m, *new_v)
```

```python
import dataclasses
import functools

import numpy as np
import jax
import jax.numpy as jnp
from jax import lax
from jax.experimental import pallas as pl
from jax.experimental.pallas import tpu as pltpu
from jax.experimental.pallas import tpu_sc as plsc

F32 = jnp.float32
BF16 = jnp.bfloat16

D = 1024
FS = 704
FSP = 768
FP = 4 * FSP
H = 8
DK = 128
DV = 256
CH = 128
VW = H * DV
INW = 10240
INS = INW // 4
CONV_W = 31
HALO = 32
EPS = 1e-5
ALPHA = 2.0 ** 0.25
ROPE_BASE = 10000.0
NCHIP = 4

ADAM_LR, ADAM_B1, ADAM_B2, ADAM_EPS, ADAM_WD, ADAM_STEP = 0.001, 0.9, 0.999, 1e-08, 0.01, 10

OFF = {"w_in": 0, "w_ret_o": 2560, "g1": 3072, "u1": 3840, "d1": 4608,
       "g2": 5376, "u2": 6144, "d2": 6912, "w_conv_o": 7680, "w_out": 7936}
WCOLS = 8192
WIDTH = {"w_in": INS, "w_ret_o": VW // NCHIP, "w_conv_o": D // NCHIP, "w_out": D // NCHIP,
         "g1": FSP, "u1": FSP, "d1": FSP, "g2": FSP, "u2": FSP, "d2": FSP}
GROUPS = (("g1", "u1"), ("d1",), ("w_in",), ("w_ret_o", "w_conv_o", "w_out"), ("g2", "u2", "d2"))
GATHER_GROUPS = (("g1", "u1"), ("d1",), ("w_in",), ("w_ret_o", "w_conv_o", "w_out"), ("g2", "u2", "d2"))


def _locate(groups):
    loc = {}
    for gi, keys in enumerate(groups):
        off = 0
        for k in keys:
            loc[k] = (gi, off)
            off += WIDTH[k]
    return loc


LOC = _locate(GROUPS)
LOC_W = _locate(GATHER_GROUPS)
GCOLS = [sum(WIDTH[k] for k in keys) for keys in GROUPS]
VMEM_LIMIT = 56 << 20


def _cp(*sem, **kw):
    return pltpu.CompilerParams(dimension_semantics=sem, vmem_limit_bytes=VMEM_LIMIT, **kw)


class _ProgramOrder:
    def __init__(self):
        self.active = False
        self.token = None


_ORDER = _ProgramOrder()


def _pcall(body, *, in_specs, scalar_prefetch=0, **kw):
    def call(*args):
        dep = _ORDER.token if _ORDER.active else None
        specs, fn = list(in_specs), body
        if dep is not None:
            n = len(args)

            def fn(*refs):
                return body(*refs[:n], *refs[n + 1:])

            specs.append(pl.BlockSpec(memory_space=pl.ANY))
            args = (*args, dep)
        params = dict(kw)
        if scalar_prefetch:
            params["grid_spec"] = pltpu.PrefetchScalarGridSpec(
                num_scalar_prefetch=scalar_prefetch, grid=params.pop("grid"), in_specs=specs,
                out_specs=params.pop("out_specs"))
        else:
            params["in_specs"] = specs
        out = pl.pallas_call(fn, **params)(*args)
        if _ORDER.active:
            _ORDER.token = jax.tree.leaves(out)[-1]
        return out

    return call


def _resident(shape, col_block):
    lead = (0,) * (len(shape) - 1)
    return pl.BlockSpec(shape, lambda *_: (*lead, col_block), pipeline_mode=pl.Buffered(1))


def _sig(x):
    return 1.0 / (1.0 + jnp.exp(-x))


def _dot(a, b):
    return jnp.dot(a, b, preferred_element_type=F32)


def _dot_nt(a, b):
    return lax.dot_general(a, b, (((1,), (1,)), ((), ())), preferred_element_type=F32)


def _ln_fwd(z, g, b):
    mu = jnp.mean(z, axis=-1, keepdims=True)
    xc = z - mu
    var = jnp.mean(xc * xc, axis=-1, keepdims=True)
    rstd = lax.rsqrt(var + EPS)
    xh = xc * rstd
    return xh * g + b, xh, rstd


def _ln_bwd(dy, xh, rstd, g):
    dxh = dy * g
    m1 = jnp.mean(dxh, axis=-1, keepdims=True)
    m2 = jnp.mean(dxh * xh, axis=-1, keepdims=True)
    return rstd * (dxh - m1 - xh * m2)


def _colsum(x):
    return jnp.sum(x, axis=0, keepdims=True)


def _zero_first(first, *refs):
    @pl.when(first)
    def _():
        for ref in refs:
            ref[...] = jnp.zeros_like(ref)


def _rope_tables(T):
    half = DK // 2
    freqs = ROPE_BASE ** (-np.arange(half, dtype=np.float32) / half)
    ang = (np.arange(T, dtype=np.float32)[:, None] * freqs[None, :]).astype(np.float32)
    cos, sin = np.cos(ang), np.sin(ang)
    return (jnp.asarray(np.concatenate([cos, cos], 1), F32),
            jnp.asarray(np.concatenate([-sin, sin], 1), F32))


def _decay_tables():
    h = np.arange(H, dtype=np.float64)
    log_g = np.log(1.0 - np.exp2(-5.0 - h))
    idx = np.arange(CH, dtype=np.float64)
    diff = idx[:, None] - idx[None, :]
    dm = np.where(diff[None] >= 0, np.exp(np.maximum(diff, 0.0)[None] * log_g[:, None, None]), 0.0)
    xi = np.exp((idx[None, :] + 1.0) * log_g[:, None])
    zeta = np.exp((CH - 1.0 - idx)[None, :] * log_g[:, None])
    cd = np.exp(CH * log_g)
    xi_t = np.broadcast_to(xi[:, :, None], (H, CH, DV))
    zeta_t = np.broadcast_to(zeta[:, :, None], (H, CH, DK))
    return (jnp.asarray(dm, F32), jnp.asarray(xi_t, F32), jnp.asarray(zeta_t, F32),
            [float(v) for v in cd])


def _cast_t(x):
    T = x.shape[0]
    tm = min(T, 512)

    def body(x_ref, xb_ref, xt_ref):
        v = x_ref[...]
        xb_ref[...] = v.astype(BF16)
        xt_ref[...] = v.T.astype(BF16)

    return _pcall(
        body, name="cast_t", grid=(T // tm,),
        in_specs=[pl.BlockSpec((tm, D), lambda i: (i, 0))],
        out_specs=[pl.BlockSpec((tm, D), lambda i: (i, 0)), pl.BlockSpec((D, tm), lambda i: (0, i))],
        out_shape=[jax.ShapeDtypeStruct((T, D), BF16), jax.ShapeDtypeStruct((D, T), BF16)],
        compiler_params=_cp("parallel"))(x)


def _ffn_up(xb, wall, og, ou, name):
    T = xb.shape[0]
    tm = min(T, 512)
    assert ou == og + FSP

    def body(x_ref, w_ref, a_ref, b_ref, h_ref):
        x = x_ref[...]
        for s in range(NCHIP):
            cols = slice(s * FSP, (s + 1) * FSP)
            a = _dot(x, w_ref[s, :, 0:FSP])
            b = _dot(x, w_ref[s, :, FSP:2 * FSP])
            a_ref[:, cols] = a.astype(BF16)
            b_ref[:, cols] = b.astype(BF16)
            h_ref[:, cols] = (a * _sig(a) * b).astype(BF16)

    ospec = pl.BlockSpec((tm, FP), lambda i: (i, 0))
    return _pcall(
        body, name=name, grid=(T // tm,),
        in_specs=[pl.BlockSpec((tm, D), lambda i: (i, 0)), _resident((NCHIP, D, 2 * FSP), og // (2 * FSP))],
        out_specs=[ospec] * 3, out_shape=[jax.ShapeDtypeStruct((T, FP), BF16)] * 3,
        compiler_params=_cp("parallel"))(xb, wall)


def _proj_ln(hb, wall, off, res, g, b, coef, name, want_b=True):
    T, K = hb.shape
    ks = K // NCHIP
    tm = min(T, 512)
    sub = min(tm, 256)

    def body(h_ref, w_ref, r_ref, g_ref, b_ref, z_ref, *rest):
        for r0 in range(0, tm, sub):
            r = slice(r0, r0 + sub)
            acc = _dot_nt(h_ref[r, 0:ks], w_ref[0])
            for s in range(1, NCHIP):
                acc += _dot_nt(h_ref[r, s * ks:(s + 1) * ks], w_ref[s])
            z = ALPHA * r_ref[r, :] + coef * acc
            z_ref[r, :] = z
            if want_b:
                y, _, _ = _ln_fwd(z, g_ref[...], b_ref[...])
                y_ref, yb_ref, yt_ref = rest
                y_ref[r, :] = y
                yb_ref[r, :] = y.astype(BF16)
                yt_ref[:, r] = y.T.astype(BF16)

    row = pl.BlockSpec((tm, D), lambda i: (i, 0))
    vec = pl.BlockSpec((1, D), lambda i: (0, 0))
    out_specs = [row]
    out_shape = [jax.ShapeDtypeStruct((T, D), F32)]
    if want_b:
        out_specs += [row, row, pl.BlockSpec((D, tm), lambda i: (0, i))]
        out_shape += [jax.ShapeDtypeStruct((T, D), F32), jax.ShapeDtypeStruct((T, D), BF16),
                      jax.ShapeDtypeStruct((D, T), BF16)]
    return _pcall(
        body, name=name, grid=(T // tm,),
        in_specs=[pl.BlockSpec((tm, K), lambda i: (i, 0)),
                  _resident((NCHIP, D, ks), off // ks), row, vec, vec],
        out_specs=out_specs, out_shape=out_shape,
        compiler_params=_cp("parallel"))(hb, wall, res, g, b)


def _inproj(xb, wall, off, b_in, cos_t, sin_t):
    T = xb.shape[0]
    tm, tn = min(T, 512), 512
    assert off == 0

    def body(x_ref, w_ref, bias_ref, cos_ref, sin_ref, o_ref):
        x = x_ref[...]
        c = cos_ref[...]
        s = sin_ref[...]
        for n0 in range(0, INW, tn):
            chip, c0 = divmod(n0, INS)
            acc = _dot(x, w_ref[chip, :, c0:c0 + tn]) + bias_ref[:, n0:n0 + tn]
            if n0 >= 2 * D:
                o_ref[:, n0:n0 + tn] = acc.astype(BF16)
                continue
            scale = DK ** -0.5 if n0 < D else 1.0
            for hh in range(tn // DK):
                xh = acc[:, hh * DK:(hh + 1) * DK]
                o = (xh * c + pltpu.roll(xh, DK // 2, 1) * s) * scale
                o_ref[:, n0 + hh * DK:n0 + (hh + 1) * DK] = o.astype(BF16)

    return _pcall(
        body, name="inproj", grid=(T // tm,),
        in_specs=[pl.BlockSpec((tm, D), lambda i: (i, 0)),
                  _resident((NCHIP, D, INS), 0),
                  pl.BlockSpec((1, INW), lambda i: (0, 0)),
                  pl.BlockSpec((tm, DK), lambda i: (i, 0)),
                  pl.BlockSpec((tm, DK), lambda i: (i, 0))],
        out_specs=pl.BlockSpec((tm, INW), lambda i: (i, 0)),
        out_shape=jax.ShapeDtypeStruct((T, INW), BF16),
        compiler_params=_cp("parallel"))(xb, wall, b_in, cos_t, sin_t)


RET_CPS = 2


def _retention_fwd(proj, gn_g, dm_t, xi_t, zeta_t, cds):
    T = proj.shape[0]
    n = T // CH
    tr = RET_CPS * CH

    def body(q_ref, k_ref, v_ref, g_ref, gn_ref, dm_ref, xi_ref, zt_ref, r_ref, ri_ref, st_ref, state):
        @pl.when(pl.program_id(0) == 0)
        def _():
            state[...] = jnp.zeros_like(state)

        for h in range(H):
            rows = slice(h * DK, (h + 1) * DK)
            cols = slice(h * DV, (h + 1) * DV)
            s_prev = state[rows, :]
            for j in range(RET_CPS):
                t = slice(j * CH, (j + 1) * CH)
                q = q_ref[t, h * DK:(h + 1) * DK]
                k = k_ref[t, h * DK:(h + 1) * DK]
                v = v_ref[t, cols]
                s_b = s_prev.astype(BF16)
                st_ref[j, rows, :] = s_b
                sc = _dot_nt(q, k) * dm_ref[h]
                r = _dot(sc.astype(BF16), v) + _dot(q, s_b) * xi_ref[h]
                kz = k.astype(F32) * zt_ref[h]
                s_prev = cds[h] * s_prev + _dot(kz.T.astype(BF16), v)
                r_ref[t, cols] = r
                mu = jnp.mean(r, axis=-1, keepdims=True)
                xc = r - mu
                var = jnp.mean(xc * xc, axis=-1, keepdims=True)
                y = xc * lax.rsqrt(var + EPS) * gn_ref[:, cols]
                g = g_ref[t, cols].astype(F32)
                ri_ref[t, cols] = (g * _sig(g) * y).astype(BF16)
            state[rows, :] = s_prev

    full3 = lambda shp: pl.BlockSpec(shp, lambda c: (0, 0, 0))
    return _pcall(
        body, name="retention_fwd", grid=(n // RET_CPS,),
        in_specs=[pl.BlockSpec((tr, D), lambda c: (c, 0)),
                  pl.BlockSpec((tr, D), lambda c: (c, 1)),
                  pl.BlockSpec((tr, VW), lambda c: (c, 1)),
                  pl.BlockSpec((tr, VW), lambda c: (c, 2)),
                  pl.BlockSpec((1, VW), lambda c: (0, 0)),
                  full3((H, CH, CH)), full3((H, CH, DV)), full3((H, CH, DK))],
        out_specs=[pl.BlockSpec((tr, VW), lambda c: (c, 0)), pl.BlockSpec((tr, VW), lambda c: (c, 0)),
                   pl.BlockSpec((RET_CPS, H * DK, DV), lambda c: (c, 0, 0))],
        out_shape=[jax.ShapeDtypeStruct((T, VW), F32), jax.ShapeDtypeStruct((T, VW), BF16),
                   jax.ShapeDtypeStruct((n, H * DK, DV), BF16)],
        scratch_shapes=[pltpu.VMEM((H * DK, DV), F32)],
        compiler_params=_cp("arbitrary"))(proj, proj, proj, proj, gn_g, dm_t, xi_t, zeta_t)


CONV_TT = 256
CONV_SB = 64
CONV_CB = 256


SUB = 8
CONV_PAD = 8


def _glu(a_ref, b_ref, rows=slice(None)):
    a = a_ref[rows, :].astype(F32)
    sb = _sig(b_ref[rows, :].astype(F32))
    return a, sb


def _shift_copies(win, sh, rows):
    win[rows:rows + CONV_PAD, :] = jnp.zeros((CONV_PAD, D), F32)
    for b in range(1, SUB):
        sh[b - 1, :, :] = win[b:b + rows, :]


def _tap(win, sh, start, size, cs):
    a, b = divmod(start, SUB)
    src = win if b == 0 else sh.at[b - 1]
    return src[SUB * a:SUB * a + size, cs]


def _conv_fwd(proj, conv_k, conv_b, ln_g, ln_b):
    T = proj.shape[0]
    tt = min(T, CONV_TT)
    ca, cb = 6 * D // D, 7 * D // D

    def body(a_ref, b_ref, pa_ref, pb_ref, k_ref, cb_ref, g_ref, bb_ref, u1_ref, u3_ref, win, sh):
        i = pl.program_id(0)
        a, sb = _glu(a_ref, b_ref)
        win[HALO:tt + HALO, :] = a * sb
        pa, psb = _glu(pa_ref, pb_ref, slice(tt - HALO, tt))
        win[0:HALO, :] = jnp.where(i > 0, pa * psb, 0.0)
        _shift_copies(win, sh, tt + HALO)
        for c0 in range(0, D, CONV_CB):
            cs = slice(c0, c0 + CONV_CB)
            for r0 in range(0, tt, CONV_SB):
                acc = jnp.zeros((CONV_SB, CONV_CB), F32)
                for w in range(CONV_W):
                    st = r0 + HALO - (CONV_W - 1) + w
                    acc += _tap(win, sh, st, CONV_SB, cs) * k_ref[w:w + 1, cs]
                u1_ref[r0:r0 + CONV_SB, cs] = acc + cb_ref[:, cs]
        u2, _, _ = _ln_fwd(u1_ref[...], g_ref[...], bb_ref[...])
        u3_ref[...] = (u2 * _sig(u2)).astype(BF16)

    vec = pl.BlockSpec((1, D), lambda i: (0, 0))
    row = pl.BlockSpec((tt, D), lambda i: (i, 0))
    return _pcall(
        body, name="conv_fwd", grid=(T // tt,),
        in_specs=[pl.BlockSpec((tt, D), lambda i: (i, ca)), pl.BlockSpec((tt, D), lambda i: (i, cb)),
                  pl.BlockSpec((tt, D), lambda i: (jnp.maximum(i - 1, 0), ca)),
                  pl.BlockSpec((tt, D), lambda i: (jnp.maximum(i - 1, 0), cb)),
                  pl.BlockSpec((CONV_W, D), lambda i: (0, 0)), vec, vec, vec],
        out_specs=[row, row],
        out_shape=[jax.ShapeDtypeStruct((T, D), F32), jax.ShapeDtypeStruct((T, D), BF16)],
        scratch_shapes=[pltpu.VMEM((tt + HALO + CONV_PAD, D), F32), pltpu.VMEM((SUB - 1, tt + HALO, D), F32)],
        compiler_params=_cp("parallel"))(proj, proj, proj, proj, conv_k, conv_b, ln_g, ln_b)


def _merge(ret_in, u3, proj, wall, off_r, off_c):
    T = ret_in.shape[0]
    tm = min(T, 512)
    kr, kc = VW // NCHIP, D // NCHIP

    def body(ri_ref, u3_ref, gr_ref, gc_ref, wr_ref, wc_ref, ro_ref, co_ref, m_ref):
        ro = _dot_nt(ri_ref[:, 0:kr], wr_ref[0])
        co = _dot_nt(u3_ref[:, 0:kc], wc_ref[0])
        for s in range(1, NCHIP):
            ro += _dot_nt(ri_ref[:, s * kr:(s + 1) * kr], wr_ref[s])
            co += _dot_nt(u3_ref[:, s * kc:(s + 1) * kc], wc_ref[s])
        ro_ref[...] = ro.astype(BF16)
        co_ref[...] = co.astype(BF16)
        m = _sig(gr_ref[...].astype(F32)) * ro + _sig(gc_ref[...].astype(F32)) * co
        m_ref[...] = m.astype(BF16)

    row = pl.BlockSpec((tm, D), lambda i: (i, 0))
    return _pcall(
        body, name="merge", grid=(T // tm,),
        in_specs=[pl.BlockSpec((tm, VW), lambda i: (i, 0)), row,
                  pl.BlockSpec((tm, D), lambda i: (i, 8)), pl.BlockSpec((tm, D), lambda i: (i, 9)),
                  pl.BlockSpec((NCHIP, D, kr), lambda i: (0, 0, off_r // kr)),
                  pl.BlockSpec((NCHIP, D, kc), lambda i: (0, 0, off_c // kc))],
        out_specs=[row] * 3, out_shape=[jax.ShapeDtypeStruct((T, D), BF16)] * 3,
        compiler_params=_cp("parallel"))(ret_in, u3, proj, proj, wall, wall)


def _loss_ln_bwd(z, g, b, target, coef):
    T = z.shape[0]
    tm = min(T, 256)
    nt = T // tm

    def body(z_ref, g_ref, b_ref, t_ref, loss_ref, dzb_ref, dzt_ref, dz_ref, dg_ref, db_ref, lacc):
        i = pl.program_id(0)
        _zero_first(i == 0, lacc, dg_ref, db_ref)
        gam = g_ref[...]
        y, xh, rstd = _ln_fwd(z_ref[...], gam, b_ref[...])
        e = y - t_ref[...]
        lacc[...] += _colsum(e * e)
        dy = e * (1.0 / D)
        dz = _ln_bwd(dy, xh, rstd, gam)
        dz_ref[...] = dz
        dzc = coef * dz
        dzb_ref[...] = dzc.astype(BF16)
        dzt_ref[...] = dzc.T.astype(BF16)
        dg_ref[...] += _colsum(dy * xh)
        db_ref[...] += _colsum(dy)

        @pl.when(i == nt - 1)
        def _():
            loss_ref[...] = (0.5 / D) * jnp.sum(lacc[...], axis=1, keepdims=True)

    row = pl.BlockSpec((tm, D), lambda i: (i, 0))
    vec = pl.BlockSpec((1, D), lambda i: (0, 0))
    return _pcall(
        body, name="loss_ln_bwd", grid=(nt,),
        in_specs=[row, vec, vec, row],
        out_specs=[pl.BlockSpec((1, 1), lambda i: (0, 0)), row, pl.BlockSpec((D, tm), lambda i: (0, i)),
                   row, vec, vec],
        out_shape=[jax.ShapeDtypeStruct((1, 1), F32), jax.ShapeDtypeStruct((T, D), BF16),
                   jax.ShapeDtypeStruct((D, T), BF16), jax.ShapeDtypeStruct((T, D), F32),
                   jax.ShapeDtypeStruct((1, D), F32), jax.ShapeDtypeStruct((1, D), F32)],
        scratch_shapes=[pltpu.VMEM((1, D), F32)],
        compiler_params=_cp("arbitrary"))(z, g, b, target)


def _ffn_bwd_h(dfb, wall, od, a, b, name):
    T = dfb.shape[0]
    tm = min(T, 512)

    def body(d_ref, w_ref, a_ref, b_ref, da_ref, db_ref):
        d = d_ref[...]
        for s in range(NCHIP):
            cols = slice(s * FSP, (s + 1) * FSP)
            dh = _dot(d, w_ref[s])
            a = a_ref[:, cols].astype(F32)
            sg = _sig(a)
            da_ref[:, cols] = (dh * b_ref[:, cols].astype(F32) * (sg * (1.0 + a * (1.0 - sg)))).astype(BF16)
            db_ref[:, cols] = (dh * a * sg).astype(BF16)

    ospec = pl.BlockSpec((tm, FP), lambda i: (i, 0))
    return _pcall(
        body, name=name, grid=(T // tm,),
        in_specs=[pl.BlockSpec((tm, D), lambda i: (i, 0)), _resident((NCHIP, D, FSP), od // FSP), ospec, ospec],
        out_specs=[ospec] * 2, out_shape=[jax.ShapeDtypeStruct((T, FP), BF16)] * 2,
        compiler_params=_cp("parallel"))(dfb, wall, a, b)


DX_SUB = 256


def _dx_partial(lhs, wall, chips, name):
    T, K = lhs.shape
    ks = K // NCHIP
    nc = len(chips)
    assert list(chips) == list(range(chips[0], chips[0] + nc)) and chips[0] % nc == 0
    tm = min(T, 512)

    def body(l_ref, w_ref, o_ref):
        for r0 in range(0, tm, DX_SUB):
            r = slice(r0, r0 + DX_SUB)
            acc = _dot_nt(l_ref[r, 0:ks], w_ref[0])
            for s in range(1, nc):
                acc += _dot_nt(l_ref[r, s * ks:(s + 1) * ks], w_ref[s])
            o_ref[r, :] = acc

    blk = chips[0] // nc
    return _pcall(
        body, name=name, grid=(T // tm,),
        in_specs=[pl.BlockSpec((tm, nc * ks), lambda i: (i, blk)),
                  pl.BlockSpec((nc, D, ks), lambda i: (blk, 0, 0), pipeline_mode=pl.Buffered(1))],
        out_specs=pl.BlockSpec((tm, D), lambda i: (i, 0)),
        out_shape=jax.ShapeDtypeStruct((T, D), F32),
        compiler_params=_cp("parallel"))(lhs, wall)


def _dx_bwd(lhs, offs, wall, dz_next, name, ln=None, chips=tuple(range(NCHIP)), partial=None):
    T, K = lhs[0].shape
    ks = K // NCHIP
    nl = len(lhs)
    nc = len(chips)
    assert list(offs) == [l * ks for l in range(nl)]
    assert list(chips) == list(range(chips[0], chips[0] + nc)) and chips[0] % nc == 0
    blk = chips[0] // nc
    tm = min(T, 512)

    def body(*refs):
        l_refs = refs[:nl]
        w_ref = refs[nl]
        dzn_ref = refs[nl + 1]
        pos = nl + 2
        if partial is not None:
            p_ref = refs[pos]
            pos += 1
        if ln is not None:
            z_ref, g_ref = refs[pos:pos + 2]
            pos += 2
        outs = refs[pos:]
        sums = list(outs[3:]) if ln is not None else list(outs[1:])

        _zero_first(pl.program_id(0) == 0, *sums)

        for r0 in range(0, tm, DX_SUB):
            r = slice(r0, r0 + DX_SUB)
            acc = None if partial is None else p_ref[r, :]
            for s in range(nc):
                rows = slice(s * ks, (s + 1) * ks)
                for l in range(nl):
                    part = _dot_nt(l_refs[l][r, rows], w_ref[s, :, l * ks:(l + 1) * ks])
                    acc = part if acc is None else acc + part
            dy = acc + ALPHA * dzn_ref[r, :]
            if ln is None:
                outs[0][r, :] = dy
            else:
                gam = g_ref[...]
                _, xh, rstd = _ln_fwd(z_ref[r, :], gam, 0.0)
                dz = _ln_bwd(dy, xh, rstd, gam)
                dzc = ln[2] * dz
                outs[0][r, :] = dzc.astype(BF16)
                outs[1][:, r] = dzc.T.astype(BF16)
                outs[2][r, :] = dz
                outs[3][...] += _colsum(dy * xh)
                outs[4][...] += _colsum(dy)

    row = pl.BlockSpec((tm, D), lambda i: (i, 0))
    vec = pl.BlockSpec((1, D), lambda i: (0, 0))
    in_specs = [pl.BlockSpec((tm, nc * ks), lambda i: (i, blk))] * nl
    in_specs += [pl.BlockSpec((nc, D, nl * ks), lambda i: (blk, 0, 0), pipeline_mode=pl.Buffered(1)), row]
    args = list(lhs) + [wall, dz_next]
    if partial is not None:
        in_specs.append(row)
        args.append(partial)
    if ln is None:
        out_specs = [row]
        out_shape = [jax.ShapeDtypeStruct((T, D), F32)]
    else:
        in_specs += [row, vec]
        args += [ln[0], ln[1]]
        out_specs = [row, pl.BlockSpec((D, tm), lambda i: (0, i)), row, vec, vec]
        out_shape = [jax.ShapeDtypeStruct((T, D), BF16), jax.ShapeDtypeStruct((D, T), BF16),
                     jax.ShapeDtypeStruct((T, D), F32), jax.ShapeDtypeStruct((1, D), F32),
                     jax.ShapeDtypeStruct((1, D), F32)]
    return _pcall(
        body, name=name, grid=(T // tm,), in_specs=in_specs, out_specs=out_specs, out_shape=out_shape,
        compiler_params=_cp("arbitrary"))(*args)


def _wgrad(lhs_t, rhs, key, name, g_all=None, colsum=False):
    T, N = rhs.shape
    tn = next(c for c in (768, 512, 256) if (N // NCHIP) % c == 0 and LOC[key][1] % c == 0)
    nps = N // NCHIP // tn
    off = LOC[key][1]
    cols = GCOLS[LOC[key][0]]

    def body(*refs):
        l_ref, r_ref = refs[0], refs[1]
        o_ref, t_ref = refs[-2 - colsum], refs[-1]
        o_ref[...] = _dot(l_ref[...], r_ref[...]).astype(BF16)
        if colsum:
            refs[-2][...] = _colsum(r_ref[...].astype(F32))
        t_ref[...] = jnp.zeros_like(t_ref)

    in_specs = [_resident((D, T), 0), pl.BlockSpec((T, tn), lambda j: (0, j))]
    args = [lhs_t, rhs]
    aliases = {}
    if g_all is not None:
        in_specs.append(pl.BlockSpec(memory_space=pl.ANY))
        args.append(g_all)
        aliases = {2: 0}
    out_specs = [pl.BlockSpec((None, D, tn), lambda j: (j // nps, 0, off // tn + j % nps))]
    out_shape = [jax.ShapeDtypeStruct((NCHIP, D, cols), BF16)]
    if colsum:
        out_specs.append(pl.BlockSpec((1, tn), lambda j: (0, j)))
        out_shape.append(jax.ShapeDtypeStruct((1, N), F32))
    out_specs.append(pl.BlockSpec((8, 128), lambda j: (0, 0)))
    out_shape.append(jax.ShapeDtypeStruct((8, 128), F32))
    res = _pcall(
        body, name=name, grid=(N // tn,), in_specs=in_specs, out_specs=out_specs, out_shape=out_shape,
        input_output_aliases=aliases,
        compiler_params=_cp("arbitrary"))(*args)
    return (res[0], res[1]) if colsum else res[0]


def _merge_bwd(dmb, wall, off, proj, ro, co):
    T = dmb.shape[0]
    tm = min(T, 512)
    ks = D // NCHIP

    def body(d_ref, w_ref, gr_ref, gc_ref, ro_ref, co_ref, dro_ref, drot_ref, dco_ref, dcot_ref, dp_ref):
        d = d_ref[...]
        dmg = jnp.concatenate([_dot(d, w_ref[s]) for s in range(NCHIP)], axis=1)
        sr = _sig(gr_ref[...].astype(F32))
        sc = _sig(gc_ref[...].astype(F32))
        dro = dmg * sr
        dco = dmg * sc
        dro_ref[...] = dro.astype(BF16)
        drot_ref[...] = dro.T.astype(BF16)
        dco_ref[...] = dco.astype(BF16)
        dcot_ref[...] = dco.T.astype(BF16)
        dp_ref[:, 0:D] = (dmg * ro_ref[...].astype(F32) * sr * (1.0 - sr)).astype(BF16)
        dp_ref[:, D:2 * D] = (dmg * co_ref[...].astype(F32) * sc * (1.0 - sc)).astype(BF16)

    row = pl.BlockSpec((tm, D), lambda i: (i, 0))
    col = pl.BlockSpec((D, tm), lambda i: (0, i))
    return _pcall(
        body, name="merge_bwd", grid=(T // tm,),
        in_specs=[row, pl.BlockSpec((NCHIP, D, ks), lambda i: (0, 0, off // ks)),
                  pl.BlockSpec((tm, D), lambda i: (i, 8)), pl.BlockSpec((tm, D), lambda i: (i, 9)), row, row],
        out_specs=[row, col, row, col, pl.BlockSpec((tm, 2 * D), lambda i: (i, 4))],
        out_shape=[jax.ShapeDtypeStruct((T, D), BF16), jax.ShapeDtypeStruct((D, T), BF16),
                   jax.ShapeDtypeStruct((T, D), BF16), jax.ShapeDtypeStruct((D, T), BF16),
                   jax.ShapeDtypeStruct((T, INW), BF16)],
        compiler_params=_cp("parallel"))(dmb, wall, proj, proj, ro, co)


def _reto_bwd(dro, wall, off, r, proj, gn_g, dproj):
    T = dro.shape[0]
    tm = min(T, 512)
    hps = H // NCHIP

    def body(d_ref, w_ref, r_ref, g_ref, gn_ref, _, dr_ref, dgn_ref, dp_ref):
        _zero_first(pl.program_id(1) == 0, dgn_ref)
        dri = _dot(d_ref[...], w_ref[...])
        rr = r_ref[...]
        mu = jnp.mean(rr, axis=-1, keepdims=True)
        xc = rr - mu
        var = jnp.mean(xc * xc, axis=-1, keepdims=True)
        rstd = lax.rsqrt(var + EPS)
        rn = xc * rstd
        gn = gn_ref[...]
        g = g_ref[...].astype(F32)
        sg = _sig(g)
        dy = dri * (g * sg)
        dp_ref[...] = (dri * (rn * gn) * (sg * (1.0 + g * (1.0 - sg)))).astype(BF16)
        dgn_ref[...] += _colsum(dy * rn)
        dr_ref[...] = _ln_bwd(dy, rn, rstd, gn).astype(BF16)

    return _pcall(
        body, name="reto_bwd", grid=(H, T // tm),
        in_specs=[pl.BlockSpec((tm, D), lambda j, i: (i, 0)),
                  pl.BlockSpec((None, D, DV), lambda j, i: (j // hps, 0, off // DV + j % hps)),
                  pl.BlockSpec((tm, DV), lambda j, i: (i, j)),
                  pl.BlockSpec((tm, DV), lambda j, i: (i, 2 * VW // DV + j)),
                  pl.BlockSpec((1, DV), lambda j, i: (0, j)),
                  pl.BlockSpec(memory_space=pl.ANY)],
        out_specs=[pl.BlockSpec((tm, DV), lambda j, i: (i, j)), pl.BlockSpec((1, DV), lambda j, i: (0, j)),
                   pl.BlockSpec((tm, DV), lambda j, i: (i, 2 * VW // DV + j))],
        out_shape=[jax.ShapeDtypeStruct((T, VW), BF16), jax.ShapeDtypeStruct((1, VW), F32),
                   jax.ShapeDtypeStruct((T, INW), BF16)],
        input_output_aliases={5: 2},
        compiler_params=_cp("arbitrary", "arbitrary"))(dro, wall, r, proj, gn_g, dproj)


def _retention_bwd(proj, dr, states, cos_t, sin_t, dm_t, xi_t, zeta_t, cds, dproj):
    T = proj.shape[0]
    n = T // CH // RET_CPS
    tr = RET_CPS * CH
    scale = DK ** -0.5

    def body(q_ref, k_ref, v_ref, dr_ref, st_ref, cos_ref, sin_ref, dm_ref, xi_ref, zt_ref, _, dp_ref, ds):
        @pl.when(pl.program_id(0) == 0)
        def _():
            ds[...] = jnp.zeros_like(ds)

        def unrope(d, t):
            return d * cos_ref[t, :] + pltpu.roll(d * sin_ref[t, :], DK // 2, 1)

        for h in range(H):
            rows = slice(h * DK, (h + 1) * DK)
            dm = dm_ref[h]
            zt = zt_ref[h]
            ds_prev = ds[rows, :]
            for j in reversed(range(RET_CPS)):
                t = slice(j * CH, (j + 1) * CH)
                q = q_ref[t, h * DK:(h + 1) * DK]
                k = k_ref[t, h * DK:(h + 1) * DK]
                v = v_ref[t, h * DV:(h + 1) * DV]
                d_r = dr_ref[t, h * DV:(h + 1) * DV]
                s_b = st_ref[j, rows, :]
                sc = _dot_nt(q, k) * dm
                dsc = _dot_nt(d_r, v) * dm
                drx = (d_r.astype(F32) * xi_ref[h]).astype(BF16)
                ds_b = ds_prev.astype(BF16)
                kz = (k.astype(F32) * zt).astype(BF16)
                dq = _dot(dsc.astype(BF16), k) + _dot_nt(drx, s_b)
                dk = _dot(dsc.T.astype(BF16), q) + _dot_nt(v, ds_b) * zt
                dv = _dot(sc.T.astype(BF16), d_r) + _dot(kz, ds_b)
                ds_prev = cds[h] * ds_prev + _dot(q.astype(F32).T.astype(BF16), drx)
                dp_ref[t, h * DK:(h + 1) * DK] = unrope(dq * scale, t).astype(BF16)
                dp_ref[t, D + h * DK:D + (h + 1) * DK] = unrope(dk, t).astype(BF16)
                dp_ref[t, 2 * D + h * DV:2 * D + (h + 1) * DV] = dv.astype(BF16)
            ds[rows, :] = ds_prev

    rv = lambda c: n - 1 - c
    full3 = lambda shp: pl.BlockSpec(shp, lambda c: (0, 0, 0))
    return _pcall(
        body, name="retention_bwd", grid=(n,),
        in_specs=[pl.BlockSpec((tr, D), lambda c: (rv(c), 0)),
                  pl.BlockSpec((tr, D), lambda c: (rv(c), 1)),
                  pl.BlockSpec((tr, VW), lambda c: (rv(c), 1)),
                  pl.BlockSpec((tr, VW), lambda c: (rv(c), 0)),
                  pl.BlockSpec((RET_CPS, H * DK, DV), lambda c: (rv(c), 0, 0)),
                  pl.BlockSpec((tr, DK), lambda c: (rv(c), 0)),
                  pl.BlockSpec((tr, DK), lambda c: (rv(c), 0)),
                  full3((H, CH, CH)), full3((H, CH, DV)), full3((H, CH, DK)),
                  pl.BlockSpec(memory_space=pl.ANY)],
        out_specs=pl.BlockSpec((tr, 2 * D + VW), lambda c: (rv(c), 0)),
        out_shape=jax.ShapeDtypeStruct((T, INW), BF16),
        input_output_aliases={10: 0},
        scratch_shapes=[pltpu.VMEM((H * DK, DV), F32)],
        compiler_params=_cp("arbitrary"))(proj, proj, proj, dr, states, cos_t, sin_t, dm_t, xi_t, zeta_t, dproj)


def _convo_bwd(dco, wall, off, u1, ln_g, ln_b):
    T = dco.shape[0]
    tm = min(T, 512)
    ks = D // NCHIP

    def body(d_ref, w_ref, u1_ref, g_ref, b_ref, du1_ref, dg_ref, db_ref, dcb_ref):
        _zero_first(pl.program_id(0) == 0, dg_ref, db_ref, dcb_ref)
        d = d_ref[...]
        du3 = jnp.concatenate([_dot(d, w_ref[s]) for s in range(NCHIP)], axis=1)
        gam = g_ref[...]
        u2, xh, rstd = _ln_fwd(u1_ref[...], gam, b_ref[...])
        sg = _sig(u2)
        du2 = du3 * (sg * (1.0 + u2 * (1.0 - sg)))
        du1 = _ln_bwd(du2, xh, rstd, gam)
        du1_ref[...] = du1
        dg_ref[...] += _colsum(du2 * xh)
        db_ref[...] += _colsum(du2)
        dcb_ref[...] += _colsum(du1)

    row = pl.BlockSpec((tm, D), lambda i: (i, 0))
    vec = pl.BlockSpec((1, D), lambda i: (0, 0))
    return _pcall(
        body, name="convo_bwd", grid=(T // tm,),
        in_specs=[row, pl.BlockSpec((NCHIP, D, ks), lambda i: (0, 0, off // ks)), row, vec, vec],
        out_specs=[row, vec, vec, vec],
        out_shape=[jax.ShapeDtypeStruct((T, D), F32)] + [jax.ShapeDtypeStruct((1, D), F32)] * 3,
        compiler_params=_cp("arbitrary"))(dco, wall, u1, ln_g, ln_b)


def _conv_bwd(du1, proj, conv_k, dproj):
    T = du1.shape[0]
    tt = min(T, CONV_TT)
    nt = T // tt
    ca, cb = 6, 7

    def body(d_ref, dn_ref, a_ref, b_ref, pa_ref, pb_ref, k_ref, _, dp_ref, dk_ref, win_u, win_d, sh_u, sh_d):
        i = pl.program_id(0)
        a, sb = _glu(a_ref, b_ref)
        win_u[HALO:tt + HALO, :] = a * sb
        pa, psb = _glu(pa_ref, pb_ref, slice(tt - HALO, tt))
        win_u[0:HALO, :] = jnp.where(i > 0, pa * psb, 0.0)
        win_d[0:tt, :] = d_ref[...]
        win_d[tt:tt + HALO, :] = jnp.where(i < nt - 1, dn_ref[0:HALO, :], 0.0)
        _shift_copies(win_u, sh_u, tt + HALO)
        _shift_copies(win_d, sh_d, tt + HALO)

        @pl.when(i == 0)
        def _():
            dk_ref[...] = jnp.zeros_like(dk_ref)

        for c0 in range(0, D, CONV_CB):
            cs = slice(c0, c0 + CONV_CB)
            for r0 in range(0, tt, CONV_SB):
                acc = jnp.zeros((CONV_SB, CONV_CB), F32)
                for w in range(CONV_W):
                    st = r0 + (CONV_W - 1) - w
                    acc += _tap(win_d, sh_d, st, CONV_SB, cs) * k_ref[w:w + 1, cs]
                aa = a_ref[r0:r0 + CONV_SB, cs].astype(F32)
                ss = _sig(b_ref[r0:r0 + CONV_SB, cs].astype(F32))
                dp_ref[r0:r0 + CONV_SB, cs] = (acc * ss).astype(BF16)
                dp_ref[r0:r0 + CONV_SB, c0 + D:c0 + D + CONV_CB] = (acc * aa * ss * (1.0 - ss)).astype(BF16)
        for c0 in range(0, D, CONV_CB):
            cs = slice(c0, c0 + CONV_CB)
            for w in range(CONV_W):
                acc = jnp.zeros((CONV_SB, CONV_CB), F32)
                for r0 in range(0, tt, CONV_SB):
                    st = r0 + HALO - (CONV_W - 1) + w
                    acc += win_d[r0:r0 + CONV_SB, cs] * _tap(win_u, sh_u, st, CONV_SB, cs)
                dk_ref[w:w + 1, cs] += _colsum(acc)

    blk = lambda f, c: pl.BlockSpec((tt, D), lambda i: (f(i), c))
    cur = lambda i: i
    prv = lambda i: jnp.maximum(i - 1, 0)
    nxt = lambda i: jnp.minimum(i + 1, nt - 1)
    return _pcall(
        body, name="conv_bwd", grid=(nt,),
        in_specs=[blk(cur, 0), blk(nxt, 0), blk(cur, ca), blk(cur, cb), blk(prv, ca), blk(prv, cb),
                  pl.BlockSpec((CONV_W, D), lambda i: (0, 0)), pl.BlockSpec(memory_space=pl.ANY)],
        out_specs=[pl.BlockSpec((tt, 2 * D), lambda i: (i, 3)), pl.BlockSpec((HALO, D), lambda i: (0, 0))],
        out_shape=[jax.ShapeDtypeStruct((T, INW), BF16), jax.ShapeDtypeStruct((HALO, D), F32)],
        input_output_aliases={7: 0},
        scratch_shapes=[pltpu.VMEM((tt + HALO + CONV_PAD, D), F32), pltpu.VMEM((tt + HALO + CONV_PAD, D), F32),
                        pltpu.VMEM((SUB - 1, tt + HALO, D), F32), pltpu.VMEM((SUB - 1, tt + HALO, D), F32)],
        compiler_params=_cp("arbitrary"))(du1, du1, proj, proj, proj, proj, conv_k, dproj)


def _local_step(x, target, gathered, slabs, sp, kvec, pos_c, pos_sc, adam):
    T = x.shape[0]
    cos_t, sin_t = _rope_tables(T)
    dm_t, xi_t, zeta_t, cds = _decay_tables()
    wts = [_with_own_slab(g, s) for g, s in zip(gathered[:-1], slabs[:-1])] + [None]
    wa = lambda key: wts[LOC_W[key][0]]
    wo = lambda key: LOC_W[key][1]
    _ORDER.active, _ORDER.token = True, None

    xb, xt = _cast_t(x)
    a1, b1, h1 = _ffn_up(xb, wa("g1"), wo("g1"), wo("u1"), "ffn1_up")
    z1, x1, x1b, x1t = _proj_ln(h1, wa("d1"), wo("d1"), x, sp["ln1_g"], sp["ln1_b"], 0.5, "ffn1_down_ln")
    proj = _inproj(x1b, wa("w_in"), wo("w_in"), sp["b_in"], cos_t, sin_t)
    r, ret_in, states = _retention_fwd(proj, sp["ret_gn_g"], dm_t, xi_t, zeta_t, cds)
    kall = _sum_devices(_all_gather_small(kvec, "gather_conv_k"), "sum_conv_k")
    sp = dict(sp, conv_k=kall.reshape(-1)[:CONV_W * D].reshape(CONV_W, D))
    u1, u3 = _conv_fwd(proj, sp["conv_k"], sp["conv_b"], sp["conv_ln_g"], sp["conv_ln_b"])
    ro, co, merged = _merge(ret_in, u3, proj, wa("w_ret_o"), wo("w_ret_o"), wo("w_conv_o"))
    z2, x2, x2b, x2t = _proj_ln(merged, wa("w_out"), wo("w_out"), x1, sp["ln2_g"], sp["ln2_b"], 1.0, "out_proj_ln")
    wts[-1] = _with_own_slab(gathered[-1], slabs[-1], after=x2[0, 0])
    a2, b2, h2 = _ffn_up(x2b, wa("g2"), wo("g2"), wo("u2"), "ffn2_up")
    (z3,) = _proj_ln(h2, wa("d2"), wo("d2"), x2, sp["ln3_g"], sp["ln3_b"], 0.5, "ffn2_down", want_b=False)

    sg = {}
    rs = {}
    loss, df2b, df2t, dz3, sg["ln3_g"], sg["ln3_b"] = _loss_ln_bwd(z3, sp["ln3_g"], sp["ln3_b"], target, 0.5)
    da2, db2 = _ffn_bwd_h(df2b, wa("d2"), wo("d2"), a2, b2, "ffn2_bwd_h")
    g4 = _wgrad(df2t, h2, "d2", "wgrad_d2")
    g4 = _wgrad(x2t, da2, "g2", "wgrad_g2", g4)
    g4 = _wgrad(x2t, db2, "u2", "wgrad_u2", g4)
    rs[4] = _ReduceScatter(g4, 4, pos_c, pos_sc)
    dmb, dmt, dz2, sg["ln2_g"], sg["ln2_b"] = _dx_bwd(
        [da2, db2], [wo("g2"), wo("u2")], wa("g2"), dz3, "ffn2_dx_ln", ln=(z2, sp["ln2_g"], 1.0))
    rs[4].phase2()
    g3 = _wgrad(dmt, merged, "w_out", "wgrad_out")
    dro, drot, dco, dcot, dproj = _merge_bwd(dmb, wa("w_out"), wo("w_out"), proj, ro, co)
    g3 = _wgrad(drot, ret_in, "w_ret_o", "wgrad_ret_o", g3)
    g3 = _wgrad(dcot, u3, "w_conv_o", "wgrad_conv_o", g3)
    rs[3] = _ReduceScatter(g3, 3, pos_c, pos_sc)
    dr, sg["ret_gn_g"], dproj = _reto_bwd(dro, wa("w_ret_o"), wo("w_ret_o"), r, proj, sp["ret_gn_g"], dproj)
    rs[4].phase3()
    rs[3].phase2()
    dproj = _retention_bwd(proj, dr, states, cos_t, sin_t, dm_t, xi_t, zeta_t, cds, dproj)
    du1, sg["conv_ln_g"], sg["conv_ln_b"], sg["conv_b"] = _convo_bwd(
        dco, wa("w_conv_o"), wo("w_conv_o"), u1, sp["conv_ln_g"], sp["conv_ln_b"])
    dproj, dck = _conv_bwd(du1, proj, sp["conv_k"], dproj)
    sg["conv_k"] = dck[:CONV_W]
    adam(4, rs[4].result())
    rs[3].phase3()
    g2, sg["b_in"] = _wgrad(x1t, dproj, "w_in", "wgrad_in", colsum=True)
    rs[2] = _ReduceScatter(g2, 2, pos_c, pos_sc)
    dx_part = _dx_partial(dproj, wa("w_in"), (0, 1), "mixer_dx_part")
    df1b, df1t, dz1, sg["ln1_g"], sg["ln1_b"] = _dx_bwd(
        [dproj], [wo("w_in")], wa("w_in"), dz2, "mixer_dx_ln", ln=(z1, sp["ln1_g"], 0.5), chips=(2, 3),
        partial=dx_part)
    adam(3, rs[3].result())
    rs[2].phase2()
    shapes = {n: sg[n].shape for n in SMALL + ["conv_k"]}
    small_parts = _all_gather_small(_pack_small(sg, loss, SMALL_ROWS), "gather_small")
    da1, db1 = _ffn_bwd_h(df1b, wa("d1"), wo("d1"), a1, b1, "ffn1_bwd_h")
    small_sum = _sum_devices(small_parts, "sum_small")
    g1 = _wgrad(df1t, h1, "d1", "wgrad_d1")
    rs[1] = _ReduceScatter(g1, 1, pos_c, pos_sc)
    g0 = _wgrad(xt, da1, "g1", "wgrad_g1")
    g0 = _wgrad(xt, db1, "u1", "wgrad_u1", g0)
    rs[0] = _ReduceScatter(g0, 0, pos_c, pos_sc)
    rs[2].phase3()
    rs[1].phase2()
    rs[0].phase2()
    (grad_x,) = _dx_bwd([da1, db1], [wo("g1"), wo("u1")], wa("g1"), dz1, "ffn1_dx")
    adam(2, rs[2].result())
    rs[1].phase3()
    rs[0].phase3()
    adam(1, rs[1].result())
    adam(0, rs[0].result())
    _ORDER.active = False
    return grad_x, small_sum, shapes


MESH = pl.DeviceIdType.MESH
ANY = pl.BlockSpec(memory_space=pl.ANY)
HALF = D // 2


def _place():
    x, y, c = lax.axis_index("x"), lax.axis_index("y"), lax.axis_index("c")
    chips = [(1 - x, y), (x, 1 - y), (1 - x, 1 - y)]
    return x, y, c, chips


GATHER_PIECES = 4
GATHER_ID = 1


def _gather_weights(wloc, name):
    w_ref = jax.new_ref(wloc, memory_space=pltpu.MemorySpace.HBM)
    o_ref = jax.empty_ref(jax.ShapeDtypeStruct((NCHIP, 2, HALF, wloc.shape[-1]), BF16),
                          memory_space=pltpu.MemorySpace.HBM)
    dma = pltpu.SemaphoreType.DMA

    nq = GATHER_PIECES
    rows = HALF // nq

    @pl.kernel(mesh=plsc.ScalarSubcoreMesh(axis_name="sc", num_cores=1), name=name,
               scratch_types=(dma((2 * nq,)), dma((2 * nq,)), dma((3 * nq,)), dma((3 * nq,)),
                              dma((nq,)), dma((nq,))),
               compiler_params=pltpu.CompilerParams(collective_id=GATHER_ID))
    def launch(s1, r1, s2, r2, s3, r3):
        x, y, c, _ = _place()
        me = 2 * x + y
        sib = (x, y, 1 - c)
        x_nbr, y_nbr = (1 - x, y, c), (x, 1 - y, c)
        x_chip, y_chip, d_chip = 2 * (1 - x) + y, 2 * x + (1 - y), 2 * (1 - x) + (1 - y)
        _handshake([sib, x_nbr, y_nbr])

        def rc(src, dst, ss, rs, dev):
            return pltpu.make_async_remote_copy(src_ref=src, dst_ref=dst, send_sem=ss, recv_sem=rs,
                                                device_id=dev, device_id_type=MESH)

        def piece(ref, q):
            return ref.at[pl.ds(q * rows, rows)]

        sends = []
        for q in range(nq):
            for j, nbr in enumerate((x_nbr, y_nbr)):
                sends.append(rc(piece(w_ref.at[c], q), piece(o_ref.at[me, c], q),
                                s1.at[j * nq + q], r1.at[j * nq + q], nbr))
                sends[-1].start()
        on_chip = c * x_chip + (1 - c) * y_chip
        other_chip = c * y_chip + (1 - c) * x_chip
        on_to = (c * x + (1 - c) * (1 - x), c * (1 - y) + (1 - c) * y, c)
        for q in range(nq):
            slot = piece(o_ref.at[on_chip, c], q)
            rc(slot, slot, s1.at[(1 - c) * nq + q], r1.at[(1 - c) * nq + q], sib).wait_recv()
            sends.append(rc(slot, slot, s3.at[q], r3.at[q], on_to))
            sends[-1].start()
            sends.append(rc(slot, slot, s2.at[q], r2.at[q], sib))
            sends[-1].start()
        for q in range(nq):
            slot = piece(o_ref.at[other_chip, c], q)
            rc(slot, slot, s1.at[c * nq + q], r1.at[c * nq + q], sib).wait_recv()
            sends.append(rc(slot, slot, s2.at[nq + q], r2.at[nq + q], sib))
            sends[-1].start()
        for q in range(nq):
            slot = piece(o_ref.at[d_chip, c], q)
            rc(slot, slot, s3.at[q], r3.at[q], sib).wait_recv()
            sends.append(rc(slot, slot, s2.at[2 * nq + q], r2.at[2 * nq + q], sib))
            sends[-1].start()
        for j, chip in enumerate([other_chip, on_chip, d_chip]):
            for q in range(nq):
                slot = piece(o_ref.at[chip, 1 - c], q)
                rc(slot, slot, s2.at[j * nq + q], r2.at[j * nq + q], sib).wait_recv()
        for cp in sends:
            cp.wait_send()

    launch()
    return o_ref[...]


def _with_own_slab(gathered, wloc, after=None):
    own = 2 * lax.axis_index("x") + lax.axis_index("y")
    if after is not None:
        wloc = wloc + (after * 0.0).astype(wloc.dtype)
    cols = wloc.shape[-1]
    return lax.dynamic_update_slice(gathered, wloc[None], (own, 0, 0, 0)).reshape(NCHIP, D, cols)


PAIR_ID = 2
CHIP_ID = 3
HBM = pltpu.MemorySpace.HBM


def _sequencer(name, collective_id, n_sems):
    dma = pltpu.SemaphoreType.DMA
    return pl.kernel(mesh=plsc.ScalarSubcoreMesh(axis_name="sc", num_cores=1), name=name,
                     scratch_types=(dma((n_sems,)), dma((n_sems,))),
                     compiler_params=pltpu.CompilerParams(collective_id=collective_id))


def _handshake(peers):
    barrier = pltpu.get_barrier_semaphore()
    for peer in peers:
        pl.semaphore_signal(barrier, inc=1, device_id=peer, device_id_type=MESH)
    pl.semaphore_wait(barrier, len(peers))


def _pair_exchange(g5, name):
    _, _, hr, cols = g5.shape
    g_ref = jax.new_ref(g5, memory_space=HBM)
    o_ref = jax.empty_ref(jax.ShapeDtypeStruct((NCHIP, hr, cols), g5.dtype), memory_space=HBM)

    @_sequencer(name, PAIR_ID, NCHIP)
    def launch(ss, rs):
        x, y, c, _ = _place()
        sib = (x, y, 1 - c)
        _handshake([sib])
        cps = [pltpu.make_async_remote_copy(src_ref=g_ref.at[j, 1 - c], dst_ref=o_ref.at[j], send_sem=ss.at[j],
                                            recv_sem=rs.at[j], device_id=sib, device_id_type=MESH)
               for j in range(NCHIP)]
        for cp in cps:
            cp.start()
        for cp in cps:
            cp.wait()

    launch()
    return o_ref[...]


RS_TR = 256


def _pair_sum(pos, g5, got, name):
    _, _, hr, cols = g5.shape

    def body(pos_ref, g_ref, r_ref, o_ref):
        o_ref[...] = (g_ref[...].astype(F32) + r_ref[...].astype(F32)).astype(BF16)

    return _pcall(
        body, name=name, scalar_prefetch=1, grid=(NCHIP, hr // RS_TR),
        in_specs=[pl.BlockSpec((None, None, RS_TR, cols), lambda j, i, p: (j, p[0], i, 0)),
                  pl.BlockSpec((None, RS_TR, cols), lambda j, i, p: (j, i, 0))],
        out_specs=pl.BlockSpec((None, RS_TR, cols), lambda j, i, p: (j, i, 0)),
        out_shape=jax.ShapeDtypeStruct((NCHIP, hr, cols), BF16),
        compiler_params=_cp("parallel", "parallel"))(pos, g5, got)


def _chip_exchange(pb, name):
    _, hr, cols = pb.shape
    p_ref = jax.new_ref(pb, memory_space=HBM)
    o_ref = jax.empty_ref(jax.ShapeDtypeStruct((3, hr, cols), BF16), memory_space=HBM)

    @_sequencer(name, CHIP_ID, 3)
    def launch(ss, rs):
        x, y, c, chips = _place()
        _handshake([(px, py, c) for px, py in chips])
        cps = [pltpu.make_async_remote_copy(src_ref=p_ref.at[2 * px + py], dst_ref=o_ref.at[j], send_sem=ss.at[j],
                                            recv_sem=rs.at[j], device_id=(px, py, c), device_id_type=MESH)
               for j, (px, py) in enumerate(chips)]
        for cp in cps:
            cp.start()
        for cp in cps:
            cp.wait()

    launch()
    return o_ref[...]


def _chip_sum(pos, g5, got, peers, name):
    _, _, hr, cols = g5.shape

    def body(pos_ref, g_ref, r_ref, p_ref, o_ref, t_ref):
        acc = g_ref[...].astype(F32) + r_ref[...].astype(F32)
        for j in range(3):
            acc += p_ref[j].astype(F32)
        o_ref[...] = acc
        t_ref[...] = jnp.zeros_like(t_ref)

    return _pcall(
        body, name=name, scalar_prefetch=1, grid=(hr // RS_TR,),
        in_specs=[pl.BlockSpec((None, None, RS_TR, cols), lambda i, p: (p[0], p[1], i, 0)),
                  pl.BlockSpec((None, RS_TR, cols), lambda i, p: (p[0], i, 0)),
                  pl.BlockSpec((3, RS_TR, cols), lambda i, p: (0, i, 0))],
        out_specs=[pl.BlockSpec((None, RS_TR, cols), lambda i, p: (p[1], i, 0)),
                   pl.BlockSpec((8, 128), lambda i, p: (0, 0))],
        out_shape=[jax.ShapeDtypeStruct((2, hr, cols), F32), jax.ShapeDtypeStruct((8, 128), F32)],
        compiler_params=_cp("arbitrary"))(pos, g5, got, peers)


def _pair_share(gsum, name):
    g_ref = jax.new_ref(gsum, memory_space=HBM)

    @_sequencer(name, PAIR_ID, 1)
    def launch(ss, rs):
        x, y, c, _ = _place()
        sib = (x, y, 1 - c)
        _handshake([sib])
        cp = pltpu.make_async_remote_copy(src_ref=g_ref.at[c], dst_ref=g_ref.at[c], send_sem=ss.at[0],
                                          recv_sem=rs.at[0], device_id=sib, device_id_type=MESH)
        cp.start()
        cp.wait_send()
        pltpu.make_async_remote_copy(src_ref=g_ref.at[1 - c], dst_ref=g_ref.at[1 - c], send_sem=ss.at[0],
                                     recv_sem=rs.at[0], device_id=sib, device_id_type=MESH).wait_recv()

    launch()
    return g_ref[...]


class _ReduceScatter:
    def __init__(self, g_arr, gi, pos_c, pos_sc):
        _, rows, cols = g_arr.shape
        self.g5 = g_arr.reshape(NCHIP, 2, rows // 2, cols)
        self.gi, self.pos_c, self.pos_sc = gi, pos_c, pos_sc
        self.got = _pair_exchange(self.g5, f"pair_exchange_{gi}")

    def phase2(self):
        pb = _pair_sum(self.pos_c, self.g5, self.got, f"pair_sum_{self.gi}")
        self.peers = _chip_exchange(pb, f"chip_exchange_{self.gi}")

    def phase3(self):
        gsum, _ = _chip_sum(self.pos_sc, self.g5, self.got, self.peers, f"chip_sum_{self.gi}")
        self.full = _pair_share(gsum, f"pair_share_{self.gi}")

    def result(self):
        _, hr, cols = self.full.shape
        return self.full.reshape(2 * hr, cols)


SMALL_ROWS = 56


ALL_ID = 4


def _all_gather_small(vec, name):
    v_ref = jax.new_ref(vec, memory_space=HBM)
    o_ref = jax.empty_ref(jax.ShapeDtypeStruct((8, SMALL_ROWS, D), F32), memory_space=HBM)

    @_sequencer(name, ALL_ID, 8)
    def launch(ss, rs):
        x, y, c, _ = _place()
        me = 4 * x + 2 * y + c
        flip = lambda v, bit: 1 - v if bit else v
        peers = [(flip(x, m >> 2), flip(y, (m >> 1) & 1), flip(c, m & 1)) for m in range(1, 8)]
        _handshake(peers)
        mine = pltpu.make_async_copy(v_ref, o_ref.at[me], ss.at[7])
        mine.start()
        cps = [pltpu.make_async_remote_copy(src_ref=v_ref, dst_ref=o_ref.at[me], send_sem=ss.at[k],
                                            recv_sem=rs.at[k], device_id=peer, device_id_type=MESH)
               for k, peer in enumerate(peers)]
        for cp in cps:
            cp.start()
        for cp in cps:
            cp.wait()
        mine.wait()

    launch()
    return o_ref[...]


def _sum_devices(parts, name):
    def body(p_ref, o_ref):
        acc = p_ref[0]
        for d in range(1, 8):
            acc += p_ref[d]
        o_ref[...] = acc

    return _pcall(
        body, name=name, grid=(SMALL_ROWS // 8,),
        in_specs=[pl.BlockSpec((8, 8, D), lambda i: (0, i, 0))],
        out_specs=pl.BlockSpec((8, D), lambda i: (i, 0)),
        out_shape=jax.ShapeDtypeStruct((SMALL_ROWS, D), F32),
        compiler_params=_cp("parallel"))(parts)


def _adamw_math(w, g, m, v):
    m2 = ADAM_B1 * m + (1.0 - ADAM_B1) * g
    v2 = ADAM_B2 * v + (1.0 - ADAM_B2) * (g * g)
    m_hat = m2 / (1.0 - ADAM_B1 ** ADAM_STEP)
    v_hat = v2 / (1.0 - ADAM_B2 ** ADAM_STEP)
    delta = -ADAM_LR * (m_hat / (jnp.sqrt(v_hat) + ADAM_EPS) + ADAM_WD * w)
    return delta, m2, v2


def _adamw(w, g, m, v, name, g_block=None):
    R, C = w.shape
    tr = R
    gw_hint = C if g_block is None else g_block[0]
    for cand in (512, 352, 256, 176, 128, 64, 32, 16, 8):
        if R % cand == 0 and cand * max(C, gw_hint) * 4 <= (2 << 20):
            tr = cand
            break
    gw, gi = (C, 0) if g_block is None else g_block

    def body(w_ref, g_ref, m_ref, v_ref, go_ref, d_ref, mo_ref, vo_ref):
        g = g_ref[:, 0:C]
        d, m2, v2 = _adamw_math(w_ref[...], g, m_ref[...], v_ref[...])
        go_ref[...] = g
        d_ref[...] = d
        mo_ref[...] = m2
        vo_ref[...] = v2

    spec = pl.BlockSpec((tr, C), lambda i: (i, 0))
    return _pcall(
        body, name=name, grid=(R // tr,),
        in_specs=[spec, pl.BlockSpec((tr, gw), lambda i: (i, gi)), spec, spec],
        out_specs=[spec] * 4, out_shape=[jax.ShapeDtypeStruct((R, C), F32)] * 4,
        compiler_params=_cp("parallel"))(w, g, m, v)


SC_TILES = 32
SC_LANES = 16


def _adamw_tiles(w, g, m, v):
    per_tile = w.shape[0] // SC_TILES
    assert per_tile * SC_TILES == w.shape[0] and per_tile % SC_LANES == 0

    def body(w_hbm, g_hbm, m_hbm, v_hbm, d_out, m_out, v_out, wb, gb, mb, vb):
        tile = lax.axis_index("subcore") * 2 + lax.axis_index("core")
        mine = pl.ds(tile * per_tile, per_tile)
        pltpu.sync_copy(w_hbm.at[mine], wb)
        pltpu.sync_copy(g_hbm.at[mine], gb)
        pltpu.sync_copy(m_hbm.at[mine], mb)
        pltpu.sync_copy(v_hbm.at[mine], vb)

        @pl.loop(0, per_tile, step=SC_LANES)
        def _(i):
            s = pl.ds(i, SC_LANES)
            d, m2, v2 = _adamw_math(wb[s], gb[s], mb[s], vb[s])
            wb[s] = d
            mb[s] = m2
            vb[s] = v2

        pltpu.sync_copy(wb, d_out.at[mine])
        pltpu.sync_copy(mb, m_out.at[mine])
        pltpu.sync_copy(vb, v_out.at[mine])

    sds = jax.ShapeDtypeStruct(w.shape, F32)
    buf = pltpu.VMEM((per_tile,), F32)
    return pl.kernel(
        body, name="adamw_small_tiles", out_type=(sds, sds, sds),
        mesh=plsc.VectorSubcoreMesh(core_axis_name="core", subcore_axis_name="subcore"),
        scratch_types=[buf, buf, buf, buf])(w, g, m, v)


BIG = ["ffn1_w_gate", "ffn1_w_up", "ffn1_w_down", "w_in", "w_ret_o", "w_conv_o", "w_out",
       "ffn2_w_gate", "ffn2_w_up", "ffn2_w_down"]
SLAB = {"ffn1_w_gate": "g1", "ffn1_w_up": "u1", "ffn1_w_down": "d1", "w_in": "w_in", "w_ret_o": "w_ret_o",
        "w_conv_o": "w_conv_o", "w_out": "w_out", "ffn2_w_gate": "g2", "ffn2_w_up": "u2", "ffn2_w_down": "d2"}
TRANSPOSED = {"ffn1_w_down", "ffn2_w_down", "w_ret_o", "w_conv_o", "w_out"}
MINOR_ROWS = {"ffn1_w_gate", "ffn1_w_up", "ffn2_w_gate", "ffn2_w_up"}
SMALL = ["ln1_g", "ln1_b", "ln2_g", "ln2_b", "ln3_g", "ln3_b", "conv_ln_g", "conv_ln_b", "conv_b",
         "ret_gn_g", "b_in"]
ORDER = ["ffn1_w_gate", "ffn1_w_up", "ffn1_w_down", "ln1_g", "ln1_b", "w_in", "b_in", "ret_gn_g", "conv_k",
         "conv_b", "conv_ln_g", "conv_ln_b", "w_ret_o", "w_conv_o", "w_out", "ln2_g", "ln2_b",
         "ffn2_w_gate", "ffn2_w_up", "ffn2_w_down", "ln3_g", "ln3_b"]


def _slab_width(name):
    return WIDTH[SLAB[name]]


def _pack_group(weights, keys):
    by_key = {SLAB[n]: n for n in BIG}
    parts = []
    for key in keys:
        w = weights[by_key[key]]
        w = w.T if by_key[key] in TRANSPOSED else w
        parts.append(jnp.pad(w, ((0, 0), (0, WIDTH[key] - w.shape[1]))))
    return jnp.concatenate(parts, axis=1).astype(BF16)


def _pack_small(vals, loss, rows):
    flat = jnp.concatenate([vals[n].reshape(-1) for n in SMALL] + [vals["conv_k"].reshape(-1), loss.reshape(-1)])
    return jnp.pad(flat, (0, rows * D - flat.shape[0])).reshape(rows, D)


def _unpack_small(arr, shapes):
    flat = arr.reshape(-1)
    out, pos = {}, 0
    for n in SMALL + ["conv_k"]:
        size = int(np.prod(shapes[n]))
        out[n] = flat[pos:pos + size].reshape(shapes[n])
        pos += size
    return out, flat[pos]


def kernel(x, ffn1_w_gate, ffn1_w_up, ffn1_w_down, ln1_g, ln1_b, w_in, b_in, ret_gn_g, conv_k, conv_b, conv_ln_g, conv_ln_b, w_ret_o, w_conv_o, w_out, ln2_g, ln2_b, ffn2_w_gate, ffn2_w_up, ffn2_w_down, ln3_g, ln3_b, loss_target, m_ffn1_w_gate, m_ffn1_w_up, m_ffn1_w_down, m_ln1_g, m_ln1_b, m_w_in, m_b_in, m_ret_gn_g, m_conv_k, m_conv_b, m_conv_ln_g, m_conv_ln_b, m_w_ret_o, m_w_conv_o, m_w_out, m_ln2_g, m_ln2_b, m_ffn2_w_gate, m_ffn2_w_up, m_ffn2_w_down, m_ln3_g, m_ln3_b, v_ffn1_w_gate, v_ffn1_w_up, v_ffn1_w_down, v_ln1_g, v_ln1_b, v_w_in, v_b_in, v_ret_gn_g, v_conv_k, v_conv_b, v_conv_ln_g, v_conv_ln_b, v_w_ret_o, v_w_conv_o, v_w_out, v_ln2_g, v_ln2_b, v_ffn2_w_gate, v_ffn2_w_up, v_ffn2_w_down, v_ln3_g, v_ln3_b):
    args = dict(locals())
    w = {n: args[n] for n in ORDER}
    m = {n: args["m_" + n] for n in ORDER}
    v = {n: args["v_" + n] for n in ORDER}
    xi, yi, ci = lax.axis_index("x"), lax.axis_index("y"), lax.axis_index("c")
    chip = 2 * xi + yi

    shards = {n: w[n][0] for n in BIG}
    slabs = [_pack_group(shards, keys) for keys in GATHER_GROUPS]
    slabs = [slab.reshape(2, HALF, slab.shape[1]) for slab in slabs]
    gathered = [_gather_weights(slab, f"gather_{gi}") for gi, slab in enumerate(slabs)]

    sp = {n: w[n] for n in SMALL}
    kpad = jnp.zeros((CONV_W, D), F32)
    kpad = lax.dynamic_update_slice(kpad, w["conv_k"][0, :, 0, :] * jnp.where(ci == 0, 1.0, 0.0), (0, chip * (D // NCHIP)))
    kvec = jnp.pad(kpad.reshape(-1), (0, SMALL_ROWS * D - CONV_W * D)).reshape(SMALL_ROWS, D)
    pos_c = jnp.reshape(ci, (1,)).astype(jnp.int32)
    pos_sc = jnp.stack([chip, ci]).astype(jnp.int32)
    out = {}

    def adam(gi, slab):
        for n in BIG:
            (g_of, off), width = LOC[SLAB[n]], _slab_width(n)
            if g_of != gi:
                continue
            w2 = w[n][0]
            if n in TRANSPOSED:
                res = _adamw(w2, slab[:, off:off + w2.shape[0]].T, m[n][0], v[n][0], "adamw_" + n)
                out[n] = [r[None] for r in res]
            elif n in MINOR_ROWS:
                res = _adamw(w2.T, slab[:, off:off + w2.shape[1]].T, m[n][0].T, v[n][0].T, "adamw_" + n)
                out[n] = [r.T[None] for r in res]
            else:
                res = _adamw(w2, slab, m[n][0], v[n][0], "adamw_" + n, g_block=(width, off // width))
                out[n] = [r[None] for r in res]

    grad_x, small_sum, shapes = _local_step(x[0], loss_target[0], gathered, slabs, sp, kvec, pos_c, pos_sc, adam)
    small, total = _unpack_small(small_sum, shapes)

    flat = lambda d: jnp.concatenate([d[n].reshape(-1) for n in SMALL])
    d_s, m_s, v_s = _adamw_tiles(flat(w), flat(small), flat(m), flat(v))
    pos = 0
    for n in SMALL:
        size = w[n].size
        cut = lambda a: a[pos:pos + size].reshape(w[n].shape)
        out[n] = [small[n], cut(d_s), cut(m_s), cut(v_s)]
        pos += size
    gk = lax.dynamic_slice(small["conv_k"], (0, chip * (D // NCHIP)), (CONV_W, D // NCHIP))
    res = _adamw(w["conv_k"][0, :, 0, :], gk, m["conv_k"][0, :, 0, :], v["conv_k"][0, :, 0, :], "adamw_conv_k")
    out["conv_k"] = [r[None, :, None, :] for r in res]

    grads = [out[n][0] for n in ORDER]
    deltas = [out[n][1] for n in ORDER]
    new_m = [out[n][2] for n in ORDER]
    new_v = [out[n][3] for n in ORDER]
    return (total, grad_x[None], *grads, *deltas, *new_m, *new_v)
```
